```python
import jax
import jax.numpy as jnp
from jax import lax
import numpy as np

D_MODEL = 2048
BATCH = 4
SEQ = 4096
DEPTH = 2

GRID_W = 64
CTX_LEN = 256
HEAD_DIM = 128
NA_HEADS = D_MODEL // (2 * HEAD_DIM)
NA_WIN_ROWS = 8
NA_WIN_COLS = 16
GQA_Q_HEADS = D_MODEL // (2 * HEAD_DIM)
GQA_KV_HEADS = 2
GQA_GROUP = GQA_Q_HEADS // GQA_KV_HEADS
Q_BLOCK = 128
ROPE_THETA = 10000.0
NA_WIDTH = NA_HEADS * HEAD_DIM
GQA_Q_WIDTH = GQA_Q_HEADS * HEAD_DIM
GQA_KV_WIDTH = GQA_KV_HEADS * HEAD_DIM
ATTN_SPLITS = (NA_WIDTH, 2 * NA_WIDTH, 3 * NA_WIDTH, 3 * NA_WIDTH + GQA_Q_WIDTH, 3 * NA_WIDTH + GQA_Q_WIDTH + GQA_KV_WIDTH)
ATTN_IN_WIDTH = 3 * NA_WIDTH + GQA_Q_WIDTH + 2 * GQA_KV_WIDTH
ATTN_OUT_WIDTH = NA_WIDTH + GQA_Q_WIDTH
ML_HEADS = 8
ML_V_DIM = D_MODEL // ML_HEADS
ML_QK_DIM = ML_V_DIM // 2
ML_CHUNK = 64
ML_QK_WIDTH = ML_HEADS * ML_QK_DIM
ML_V_WIDTH = ML_HEADS * ML_V_DIM
ML_SPLITS = (ML_QK_WIDTH, 2 * ML_QK_WIDTH, 2 * ML_QK_WIDTH + ML_V_WIDTH, 2 * ML_QK_WIDTH + 2 * ML_V_WIDTH)
ML_IN_WIDTH = 2 * ML_QK_WIDTH + 2 * ML_V_WIDTH + 4 * ML_HEADS
N_EXPERTS = 64
TOP_K = 8
N_GROUPS = 8
TOPK_GROUPS = 4
EXPERTS_PER_GROUP = N_EXPERTS // N_GROUPS
EXPERT_DIM = D_MODEL // 4
SHARED_DIM = D_MODEL // 4
ROUTED_SCALE = 2.5
MOE_BLOCK = 256
NORM_EPS = 1e-6
NEG_INF = -1e30
N_EVEN = (DEPTH + 1) // 2
N_ODD = DEPTH // 2

kernel_name = 'hybrid_natten_gqa_mlstm_moe_dit'


def rms_norm(x, gain):
    xf = x.astype(jnp.float32)
    y = xf * lax.rsqrt(jnp.mean(xf * xf, axis=-1, keepdims=True) + NORM_EPS)
    return (y * gain.astype(jnp.float32)).astype(x.dtype)


def to_heads(t, n_heads):
    b, n, w = t.shape
    return t.reshape(b, n, n_heads, w // n_heads).transpose(0, 2, 1, 3)


def from_heads(t):
    b, h, n, d = t.shape
    return t.transpose(0, 2, 1, 3).reshape(b, n, h * d)


def rope_2d_tables(n_tokens):
    t = jnp.arange(n_tokens, dtype=jnp.int32)
    row = (t // GRID_W).astype(jnp.float32)
    col = (t % GRID_W).astype(jnp.float32)
    n_freq = HEAD_DIM // 4
    inv_freq = ROPE_THETA ** (-jnp.arange(n_freq, dtype=jnp.float32) / n_freq)
    ang = jnp.concatenate([row[:, None] * inv_freq, col[:, None] * inv_freq], axis=-1)
    return jnp.cos(ang), jnp.sin(ang)


def apply_rope(x, cos, sin):
    xf = x.astype(jnp.float32)
    x1, x2 = xf[..., 0::2], xf[..., 1::2]
    out = jnp.stack([x1 * cos - x2 * sin, x1 * sin + x2 * cos], axis=-1)
    return out.reshape(x.shape).astype(x.dtype)


def grouped_attention(q, k, v):
    s = jnp.einsum('bngqd,bnkd->bngqk', q, k, preferred_element_type=jnp.float32) * (q.shape[-1] ** -0.5)
    p = jax.nn.softmax(s, axis=-1).astype(v.dtype)
    return jnp.einsum('bngqk,bnkd->bngqd', p, v)


def blocked_grouped_attention(q, k, v):
    b, nkv, g, s, dh = q.shape
    nb = s // Q_BLOCK
    qb = jnp.moveaxis(q.reshape(b, nkv, g, nb, Q_BLOCK, dh), 3, 0)
    ob = lax.map(lambda qi: grouped_attention(qi, k, v), qb)
    return jnp.moveaxis(ob, 0, 3).reshape(b, nkv, g, s, dh)


def neighborhood_attention(q, k, v, k_ctx, v_ctx, rpb):
    b, nh, s, dh = q.shape
    rows = s // GRID_W
    wr = min(NA_WIN_ROWS, rows)
    r = jnp.arange(rows)
    key_rows = jnp.clip(r - wr // 2, 0, rows - wr)[:, None] + jnp.arange(wr)[None, :]
    cq = jnp.arange(GRID_W)
    col_start = jnp.clip(cq - NA_WIN_COLS // 2, 0, GRID_W - NA_WIN_COLS)
    col_ok = (cq[None, :] >= col_start[:, None]) & (cq[None, :] < col_start[:, None] + NA_WIN_COLS)
    row_idx = key_rows - r[:, None] + NA_WIN_ROWS - 1
    col_idx = jnp.clip(cq[None, :] - cq[:, None] + NA_WIN_COLS - 1, 0, 2 * NA_WIN_COLS - 2)
    bias = rpb.astype(jnp.float32)[:, row_idx[:, None, :, None], col_idx[None, :, None, :]]
    qg = q.reshape(b, nh, rows, GRID_W, dh)
    kg = k.reshape(b, nh, rows, GRID_W, dh)[:, :, key_rows]
    vg = v.reshape(b, nh, rows, GRID_W, dh)[:, :, key_rows]
    scale = dh ** -0.5
    s_win = jnp.einsum('bhrqd,bhrwkd->bhrqwk', qg, kg, preferred_element_type=jnp.float32) * scale + bias[None]
    s_win = jnp.where(col_ok[:, None, :], s_win, NEG_INF).reshape(b, nh, rows, GRID_W, wr * GRID_W)
    s_ctx = jnp.einsum('bhrqd,bhld->bhrql', qg, k_ctx, preferred_element_type=jnp.float32) * scale
    p = jax.nn.softmax(jnp.concatenate([s_win, s_ctx], axis=-1), axis=-1).astype(v.dtype)
    p_win = p[..., :wr * GRID_W].reshape(b, nh, rows, GRID_W, wr, GRID_W)
    p_ctx = p[..., wr * GRID_W:]
    out = jnp.einsum('bhrqwk,bhrwkd->bhrqd', p_win, vg) + jnp.einsum('bhrql,bhld->bhrqd', p_ctx, v_ctx)
    return out.reshape(b, nh, s, dh)


def attn_mixer(h, hc, w_in, w_out, rpb, q_gain, k_gain, cos, sin, with_ctx_out):
    def project(t):
        qa, ka, va, qb, kb, vb = jnp.split(t @ w_in, ATTN_SPLITS, axis=-1)
        qa, ka, va = to_heads(qa, NA_HEADS), to_heads(ka, NA_HEADS), to_heads(va, NA_HEADS)
        qb = rms_norm(to_heads(qb, GQA_Q_HEADS), q_gain)
        kb = rms_norm(to_heads(kb, GQA_KV_HEADS), k_gain)
        vb = to_heads(vb, GQA_KV_HEADS)
        return qa, ka, va, qb, kb, vb

    qa, ka, va, qb, kb, vb = project(h)
    qa_c, ka_c, va_c, qb_c, kb_c, vb_c = project(hc)
    bsz, _, s, _ = qb.shape
    oa = neighborhood_attention(qa, ka, va, ka_c, va_c, rpb)
    qb = apply_rope(qb, cos, sin)
    kb = apply_rope(kb, cos, sin)
    k_all = jnp.concatenate([kb_c, kb], axis=2)
    v_all = jnp.concatenate([vb_c, vb], axis=2)
    ob = blocked_grouped_attention(qb.reshape(bsz, GQA_KV_HEADS, GQA_GROUP, s, HEAD_DIM), k_all, v_all)
    ob = ob.reshape(bsz, GQA_Q_HEADS, s, HEAD_DIM)
    y = jnp.concatenate([from_heads(oa), from_heads(ob)], axis=-1) @ w_out
    if not with_ctx_out:
        return y, None
    n_ctx = hc.shape[1]
    oa_c = grouped_attention(qa_c[:, :, None], ka_c, va_c)[:, :, 0]
    ob_c = grouped_attention(qb_c.reshape(bsz, GQA_KV_HEADS, GQA_GROUP, n_ctx, HEAD_DIM), kb_c, vb_c)
    ob_c = ob_c.reshape(bsz, GQA_Q_HEADS, n_ctx, HEAD_DIM)
    yc = jnp.concatenate([from_heads(oa_c), from_heads(ob_c)], axis=-1) @ w_out
    return y, yc


def mlstm_scan(q, k, v, log_i, log_f, state, with_outputs):
    b, nh, t, _ = q.shape
    nc = t // ML_CHUNK

    def chunks(a):
        return jnp.moveaxis(a.reshape(b, nh, nc, ML_CHUNK, *a.shape[3:]), 2, 0)

    causal = jnp.tril(jnp.ones((ML_CHUNK, ML_CHUNK), dtype=bool))

    def step(carry, xs):
        c_st, n_st, m_st = carry
        qc, kc, vc, li, lf = xs
        bc = jnp.cumsum(lf, axis=-1)
        b_tot = bc[..., -1]
        a_end = b_tot[..., None] - bc + li
        m_new = jnp.maximum(b_tot + m_st, jnp.max(a_end, axis=-1))
        w_end = jnp.exp(a_end - m_new[..., None])
        decay = jnp.exp(b_tot + m_st - m_new)
        c_new = decay[..., None, None] * c_st + jnp.einsum('bhsk,bhsv->bhkv', kc * w_end[..., None], vc)
        n_new = decay[..., None] * n_st + jnp.einsum('bhs,bhsk->bhk', w_end, kc)
        if not with_outputs:
            return (c_new, n_new, m_new), None
        d_mat = jnp.where(causal, bc[..., :, None] - bc[..., None, :] + li[..., None, :], -jnp.inf)
        m_row = jnp.maximum(bc + m_st[..., None], jnp.max(d_mat, axis=-1))
        w_inter = jnp.exp(bc + m_st[..., None] - m_row)
        s_mat = jnp.einsum('bhtk,bhsk->bhts', qc, kc) * jnp.exp(d_mat - m_row[..., None])
        num = w_inter[..., None] * jnp.einsum('bhtk,bhkv->bhtv', qc, c_st) + jnp.einsum('bhts,bhsv->bhtv', s_mat, vc)
        den = w_inter * jnp.einsum('bhtk,bhk->bht', qc, n_st) + jnp.sum(s_mat, axis=-1)
        h = num / jnp.maximum(jnp.abs(den), jnp.exp(-m_row))[..., None]
        return (c_new, n_new, m_new), h

    state, hs = lax.scan(step, state, (chunks(q), chunks(k), chunks(v), chunks(log_i), chunks(log_f)))
    if not with_outputs:
        return None, state
    return jnp.moveaxis(hs, 0, 2).reshape(b, nh, t, v.shape[-1]), state


def mlstm_mixer(h, hc, w_in, w_out, gate_b, head_gain, with_ctx_out):
    def project(t):
        bsz, n, _ = t.shape
        q, k, v, o, g = jnp.split(t @ w_in, ML_SPLITS, axis=-1)
        q = to_heads(q, ML_HEADS).astype(jnp.float32)
        k = to_heads(k, ML_HEADS).astype(jnp.float32) * (ML_QK_DIM ** -0.5)
        v = to_heads(v, ML_HEADS).astype(jnp.float32)
        g = (g.astype(jnp.float32).reshape(bsz, n, 4, ML_HEADS) + gate_b.astype(jnp.float32)).transpose(2, 0, 3, 1)
        return q, k, v, o, g[:2], jax.nn.log_sigmoid(g[2:])

    def readout(h_sum, o):
        bsz, _, n, _ = h_sum.shape
        hn = rms_norm(h_sum.transpose(0, 2, 1, 3), head_gain.reshape(ML_HEADS, ML_V_DIM))
        return (hn.reshape(bsz, n, ML_V_WIDTH).astype(o.dtype) * jax.nn.sigmoid(o)) @ w_out

    q, k, v, o, log_i, log_f = project(h)
    qc, kc, vc, oc, log_ic, log_fc = project(hc)
    bsz = h.shape[0]
    zero_state = (jnp.zeros((bsz, ML_HEADS, ML_QK_DIM, ML_V_DIM), jnp.float32),
                  jnp.zeros((bsz, ML_HEADS, ML_QK_DIM), jnp.float32),
                  jnp.zeros((bsz, ML_HEADS), jnp.float32))
    h_lat = 0.0
    h_ctx = 0.0
    for direction in range(2):
        rev = (lambda a: jnp.flip(a, axis=2)) if direction == 1 else (lambda a: a)
        hc_d, ctx_state = mlstm_scan(rev(qc), rev(kc), rev(vc), rev(log_ic[direction]), rev(log_fc[direction]), zero_state, with_ctx_out)
        hl_d, _ = mlstm_scan(rev(q), rev(k), rev(v), rev(log_i[direction]), rev(log_f[direction]), ctx_state, True)
        h_lat = h_lat + rev(hl_d)
        if with_ctx_out:
            h_ctx = h_ctx + rev(hc_d)
    y = readout(h_lat, o)
    yc = readout(h_ctx, oc) if with_ctx_out else None
    return y, yc


def moe_ffn(h, router_w, router_b, w1, w3, w2, sw1, sw3, sw2):
    t, d = h.shape
    s = jax.nn.sigmoid(jnp.dot(h, router_w, preferred_element_type=jnp.float32))
    s_sel = s + router_b.astype(jnp.float32)
    group_score = jnp.sum(lax.top_k(s_sel.reshape(t, N_GROUPS, EXPERTS_PER_GROUP), 2)[0], axis=-1)
    _, g_idx = lax.top_k(group_score, TOPK_GROUPS)
    g_mask = jnp.any(g_idx[:, :, None] == jnp.arange(N_GROUPS)[None, None, :], axis=1)
    s_sel = jnp.where(jnp.repeat(g_mask, EXPERTS_PER_GROUP, axis=1), s_sel, -jnp.inf)
    _, e_idx = lax.top_k(s_sel, TOP_K)
    w = jnp.take_along_axis(s, e_idx, axis=1)
    w = w / jnp.sum(w, axis=-1, keepdims=True) * ROUTED_SCALE

    tk = t * TOP_K
    e_flat = e_idx.reshape(-1)
    tok_flat = jnp.repeat(jnp.arange(t, dtype=jnp.int32), TOP_K)
    order = jnp.argsort(e_flat)
    e_s, tok_s, w_s = e_flat[order], tok_flat[order], w.reshape(-1)[order]
    counts = jnp.bincount(e_flat, length=N_EXPERTS)
    padded = (counts + MOE_BLOCK - 1) // MOE_BLOCK * MOE_BLOCK
    pad_end = jnp.cumsum(padded)
    pad_start = pad_end - padded
    start = jnp.cumsum(counts) - counts
    dest = pad_start[e_s] + jnp.arange(tk, dtype=jnp.int32) - start[e_s]
    n_blocks = -(-tk // MOE_BLOCK) + N_EXPERTS
    n_rows = n_blocks * MOE_BLOCK
    src_tok = jnp.zeros((n_rows,), jnp.int32).at[dest].set(tok_s)
    src_w = jnp.zeros((n_rows,), jnp.float32).at[dest].set(w_s)
    block_e = jnp.minimum(jnp.searchsorted(pad_end, jnp.arange(n_blocks) * MOE_BLOCK, side='right'), N_EXPERTS - 1)

    def expert_block(args):
        tok, wt, e = args
        xb = h[tok]
        y = (jax.nn.silu(xb @ w1[e]) * (xb @ w3[e])) @ w2[e]
        return y * wt[:, None].astype(y.dtype)

    y_buf = lax.map(expert_block, (src_tok.reshape(n_blocks, MOE_BLOCK), src_w.reshape(n_blocks, MOE_BLOCK), block_e))
    routed = jax.ops.segment_sum(y_buf.reshape(n_rows, d), src_tok, num_segments=t)
    shared = (jax.nn.silu(h @ sw1) * (h @ sw3)) @ sw2
    return shared + routed


def setup_inputs(seed: int = 0) -> dict:
    key = jax.random.key(seed)
    ks = iter(jax.random.split(key, 40))

    def nrm(shape, scale):
        return jax.random.normal(next(ks), shape, jnp.float32) * scale

    d = D_MODEL
    x = nrm((BATCH, SEQ, d), 1.0)
    c = nrm((BATCH, d), 1.0)
    ctx = nrm((BATCH, CTX_LEN, d), 1.0)
    c_ctx = nrm((d,), 1.0)
    ada_w = nrm((DEPTH, d, 6 * d), 0.5 * d ** -0.5)
    ada_b = nrm((DEPTH, 6 * d), 0.02)
    norm_mix = 1.0 + nrm((DEPTH, d), 0.02)
    norm_ffn = 1.0 + nrm((DEPTH, d), 0.02)
    attn_w_in = nrm((N_EVEN, d, ATTN_IN_WIDTH), d ** -0.5)
    attn_w_out = nrm((N_EVEN, ATTN_OUT_WIDTH, d), ATTN_OUT_WIDTH ** -0.5)
    attn_rpb = nrm((N_EVEN, NA_HEADS, 2 * NA_WIN_ROWS - 1, 2 * NA_WIN_COLS - 1), 0.05)
    attn_q_gain = 1.0 + nrm((N_EVEN, HEAD_DIM), 0.02)
    attn_k_gain = 1.0 + nrm((N_EVEN, HEAD_DIM), 0.02)
    ml_w_in = nrm((N_ODD, d, ML_IN_WIDTH), d ** -0.5)
    ml_w_out = nrm((N_ODD, ML_V_WIDTH, d), ML_V_WIDTH ** -0.5)
    in_gate_b = nrm((N_ODD, 2, ML_HEADS), 0.1)
    forget_gate_b = 3.0 + 3.0 * jax.random.uniform(next(ks), (N_ODD, 2, ML_HEADS), jnp.float32)
    ml_gate_b = jnp.concatenate([in_gate_b, forget_gate_b], axis=1)
    ml_head_gain = 1.0 + nrm((N_ODD, ML_V_WIDTH), 0.02)
    router_w = nrm((DEPTH, d, N_EXPERTS), d ** -0.5)
    router_b = nrm((DEPTH, N_EXPERTS), 0.01)
    exp_w1 = nrm((DEPTH, N_EXPERTS, d, EXPERT_DIM), d ** -0.5)
    exp_w3 = nrm((DEPTH, N_EXPERTS, d, EXPERT_DIM), d ** -0.5)
    exp_w2 = nrm((DEPTH, N_EXPERTS, EXPERT_DIM, d), EXPERT_DIM ** -0.5)
    sh_w1 = nrm((DEPTH, d, SHARED_DIM), d ** -0.5)
    sh_w3 = nrm((DEPTH, d, SHARED_DIM), d ** -0.5)
    sh_w2 = nrm((DEPTH, SHARED_DIM, d), SHARED_DIM ** -0.5)
    final_norm = 1.0 + nrm((d,), 0.02)
    return {'x': x, 'c': c, 'ctx': ctx, 'c_ctx': c_ctx, 'ada_w': ada_w, 'ada_b': ada_b,
            'norm_mix': norm_mix, 'norm_ffn': norm_ffn, 'attn_w_in': attn_w_in, 'attn_w_out': attn_w_out,
            'attn_rpb': attn_rpb, 'attn_q_gain': attn_q_gain, 'attn_k_gain': attn_k_gain,
            'ml_w_in': ml_w_in, 'ml_w_out': ml_w_out, 'ml_gate_b': ml_gate_b, 'ml_head_gain': ml_head_gain,
            'router_w': router_w, 'router_b': router_b, 'exp_w1': exp_w1, 'exp_w3': exp_w3, 'exp_w2': exp_w2,
            'sh_w1': sh_w1, 'sh_w3': sh_w3, 'sh_w2': sh_w2, 'final_norm': final_norm}


def reference(x, c, ctx, c_ctx, ada_w, ada_b, norm_mix, norm_ffn, attn_w_in, attn_w_out, attn_rpb,
              attn_q_gain, attn_k_gain, ml_w_in, ml_w_out, ml_gate_b, ml_head_gain, router_w, router_b,
              exp_w1, exp_w3, exp_w2, sh_w1, sh_w3, sh_w2, final_norm):
    b, s, d = x.shape
    cos, sin = rope_2d_tables(s)
    xc = ctx
    for layer in range(DEPTH):
        last = layer == DEPTH - 1
        mod = (jax.nn.silu(c) @ ada_w[layer] + ada_b[layer]).reshape(b, 6, 1, d)
        mod_c = (jax.nn.silu(c_ctx) @ ada_w[layer] + ada_b[layer]).reshape(6, d)
        hx = rms_norm(x, norm_mix[layer]) * (1.0 + mod[:, 1]) + mod[:, 0]
        hc = rms_norm(xc, norm_mix[layer]) * (1.0 + mod_c[1]) + mod_c[0]
        j = layer // 2
        if layer % 2 == 0:
            y, yc = attn_mixer(hx, hc, attn_w_in[j], attn_w_out[j], attn_rpb[j], attn_q_gain[j], attn_k_gain[j], cos, sin, not last)
        else:
            y, yc = mlstm_mixer(hx, hc, ml_w_in[j], ml_w_out[j], ml_gate_b[j], ml_head_gain[j], not last)
        x = x + mod[:, 2] * y
        hx = rms_norm(x, norm_ffn[layer]) * (1.0 + mod[:, 4]) + mod[:, 3]
        moe_args = (router_w[layer], router_b[layer], exp_w1[layer], exp_w3[layer], exp_w2[layer],
                    sh_w1[layer], sh_w3[layer], sh_w2[layer])
        if last:
            f = moe_ffn(hx.reshape(b * s, d), *moe_args).reshape(b, s, d)
        else:
            xc = xc + mod_c[2] * yc
            hc = rms_norm(xc, norm_ffn[layer]) * (1.0 + mod_c[4]) + mod_c[3]
            f_all = moe_ffn(jnp.concatenate([hx.reshape(b * s, d), hc.reshape(-1, d)], axis=0), *moe_args)
            f = f_all[:b * s].reshape(b, s, d)
            xc = xc + mod_c[5] * f_all[b * s:].reshape(xc.shape)
        x = x + mod[:, 5] * f
    return rms_norm(x, final_norm)
```

```python
import functools

import jax
import jax.numpy as jnp
from jax import lax
from jax.experimental import pallas as pl
from jax.experimental.pallas import tpu as pltpu

F32 = jnp.float32
BF16 = jnp.bfloat16
I32 = jnp.int32
U32 = jnp.uint32

D = 2048
B = 4
S = 4096
L = 256
T_LAT = B * S
T_CTX = B * L
T_ALL = T_LAT + T_CTX
GRID_W = 64
ROWS = S // GRID_W
HD = 128
NA_HEADS = 8
NA_WIN_ROWS = 8
NA_WIN_COLS = 16
GQA_Q_HEADS = 8
GQA_KV_HEADS = 2
GQA_GROUP = 4
ROPE_THETA = 10000.0
ATTN_IN = 4608
ML_HEADS = 8
ML_V = 256
ML_QK = 128
ML_MAIN = 6144
N_EXPERTS = 64
TOP_K = 8
N_GROUPS = 8
TOPK_GROUPS = 4
EXPERT_DIM = 512
ROUTED_SCALE = 2.5
EPS = 1e-6
NEG_INF = -1e30
ATT_SCALE = HD ** -0.5
ML_KSCALE = ML_QK ** -0.5

LANE = 128
NA_QROWS = 4
NA_SLAB = NA_QROWS + NA_WIN_ROWS - 1
NA_QB = NA_QROWS * GRID_W
NA_KB = NA_SLAB * GRID_W
ML_CH = 256
EXP_BM = 512
PK_W = D // 2
PK_S = PK_W // LANE
Y_S = D // LANE
VMEM_LIMIT = 48 * 1024 * 1024


def _cp(sem, vmem=VMEM_LIMIT):
    return pltpu.CompilerParams(dimension_semantics=sem, vmem_limit_bytes=vmem)


def _mod_row(start_row):
    return jnp.where(start_row < T_LAT, start_row // S, B)


def _ada_kernel(c_ref, w_ref, b_ref, o_ref):
    c = c_ref[...]
    a = (c * jax.nn.sigmoid(c)).astype(BF16)
    w = w_ref[0].astype(BF16)
    o_ref[0] = jnp.dot(a, w, preferred_element_type=F32) + b_ref[0]


def ada_ln(cvec, ada_w, ada_b):
    depth = ada_w.shape[0]
    n = ada_w.shape[2]
    tn = 1024
    return pl.pallas_call(
        _ada_kernel,
        grid=(depth, n // tn),
        in_specs=[pl.BlockSpec((8, D), lambda l, j: (0, 0)),
                  pl.BlockSpec((1, D, tn), lambda l, j: (l, 0, j)),
                  pl.BlockSpec((1, 1, tn), lambda l, j: (l, 0, j))],
        out_specs=pl.BlockSpec((1, 8, tn), lambda l, j: (l, 0, j)),
        out_shape=jax.ShapeDtypeStruct((depth, 8, n), F32),
        compiler_params=_cp(("arbitrary", "arbitrary")),
        name="ada_ln",
    )(cvec, ada_w, ada_b.reshape(depth, 1, n))


def _norm_mod_kernel(x_ref, g_ref, mod_ref, *out_refs, base, pack, tm):
    x = x_ref[...]
    y = x * lax.rsqrt(jnp.mean(x * x, axis=-1, keepdims=True) + EPS) * g_ref[...]
    m = mod_ref[0]
    h = y * (1.0 + m[base + 1:base + 2]) + m[base:base + 1]
    hb = h.astype(BF16)
    out_refs[0][...] = hb
    if pack:
        u = pltpu.bitcast(hb.astype(F32), U32)
        for s in range(PK_S):
            lo = u[:, s * LANE:(s + 1) * LANE]
            hi = u[:, PK_W + s * LANE:PK_W + (s + 1) * LANE]
            out_refs[1][pl.ds(s, tm, stride=PK_S), :] = (hi & jnp.uint32(0xFFFF0000)) | (lo >> 16)


def norm_mod(x, gain, mod, base, n_rows, pack):
    tm = 256
    out_shape = [jax.ShapeDtypeStruct((n_rows, D), BF16)]
    out_specs = [pl.BlockSpec((tm, D), lambda i: (i, 0))]
    if pack:
        out_shape.append(jax.ShapeDtypeStruct((n_rows * PK_S, LANE), U32))
        out_specs.append(pl.BlockSpec((tm * PK_S, LANE), lambda i: (i, 0)))
    res = pl.pallas_call(
        functools.partial(_norm_mod_kernel, base=base, pack=pack, tm=tm),
        grid=(n_rows // tm,),
        in_specs=[pl.BlockSpec((tm, D), lambda i: (i, 0)),
                  pl.BlockSpec((1, D), lambda i: (0, 0)),
                  pl.BlockSpec((1, 6, D), lambda i: (_mod_row(i * tm), 0, 0))],
        out_specs=out_specs,
        out_shape=out_shape,
        compiler_params=_cp(("arbitrary",)),
        name="norm_mod",
    )(x, gain.reshape(1, D), mod)
    return res if pack else res[0]


def _final_norm_kernel(x_ref, g_ref, o_ref):
    x = x_ref[...]
    o_ref[...] = x * lax.rsqrt(jnp.mean(x * x, axis=-1, keepdims=True) + EPS) * g_ref[...]


def final_norm(x, gain):
    tm = 256
    return pl.pallas_call(
        _final_norm_kernel,
        grid=(T_LAT // tm,),
        in_specs=[pl.BlockSpec((tm, D), lambda i: (i, 0)),
                  pl.BlockSpec((1, D), lambda i: (0, 0))],
        out_specs=pl.BlockSpec((tm, D), lambda i: (i, 0)),
        out_shape=jax.ShapeDtypeStruct((T_LAT, D), F32),
        compiler_params=_cp(("arbitrary",)),
        name="final_norm",
    )(x, gain.reshape(1, D))


def _mm_kernel(a_ref, w_ref, o_ref):
    o_ref[...] = jnp.dot(a_ref[...], w_ref[...], preferred_element_type=F32).astype(o_ref.dtype)


def matmul(a, w, out_dtype, tm=1024, tn=512):
    m, k = a.shape
    n = w.shape[1]
    return pl.pallas_call(
        _mm_kernel,
        grid=(m // tm, n // tn),
        in_specs=[pl.BlockSpec((tm, k), lambda i, j: (i, 0)),
                  pl.BlockSpec((k, tn), lambda i, j: (0, j))],
        out_specs=pl.BlockSpec((tm, tn), lambda i, j: (i, j)),
        out_shape=jax.ShapeDtypeStruct((m, n), out_dtype),
        compiler_params=_cp(("arbitrary", "arbitrary")),
        name="matmul",
    )(a, w)


def _mm_res_kernel(a_ref, w_ref, x_ref, mod_ref, o_ref, *, slot):
    acc = jnp.dot(a_ref[...], w_ref[...], preferred_element_type=F32)
    o_ref[...] = x_ref[...] + mod_ref[0][slot:slot + 1] * acc


def matmul_gated_residual(a, w, x, mod, slot, tm=1024, tn=512):
    m, k = a.shape
    n = w.shape[1]
    return pl.pallas_call(
        functools.partial(_mm_res_kernel, slot=slot),
        grid=(m // tm, n // tn),
        in_specs=[pl.BlockSpec((tm, k), lambda i, j: (i, 0)),
                  pl.BlockSpec((k, tn), lambda i, j: (0, j)),
                  pl.BlockSpec((tm, tn), lambda i, j: (i, j)),
                  pl.BlockSpec((1, 6, tn), lambda i, j: (_mod_row(i * tm), 0, j))],
        out_specs=pl.BlockSpec((tm, tn), lambda i, j: (i, j)),
        out_shape=jax.ShapeDtypeStruct((m, n), F32),
        compiler_params=_cp(("arbitrary", "arbitrary")),
        name="matmul_gated_residual",
    )(a, w, x, mod)


def _dot_nt(a, b):
    return lax.dot_general(a, b, (((1,), (1,)), ((), ())), preferred_element_type=F32)


def _rms_head(x, gain):
    return x * lax.rsqrt(jnp.mean(x * x, axis=-1, keepdims=True) + EPS) * gain


def _rope(x, cosf, sinf):
    lane = lax.broadcasted_iota(I32, x.shape, 1)
    nxt = pltpu.roll(x, LANE - 1, 1)
    prv = pltpu.roll(x, 1, 1)
    return x * cosf + jnp.where((lane & 1) == 0, nxt, prv) * sinf


def _softmax_av(parts):
    m = functools.reduce(jnp.maximum, [jnp.max(s, axis=-1, keepdims=True) for s, _ in parts])
    l = None
    o = None
    for s, v in parts:
        p = jnp.exp(s - m)
        li = jnp.sum(p, axis=-1, keepdims=True)
        oi = jnp.dot(p.astype(BF16), v, preferred_element_type=F32)
        l = li if l is None else l + li
        o = oi if o is None else o + oi
    return o / l


def _na_kernel(q_ref, k_ref, v_ref, kc_ref, vc_ref, tab_ref, o_ref):
    kc = kc_ref[...]
    vc = vc_ref[...]
    n_blocks = ROWS // NA_QROWS

    def body(j, carry):
        ks = jnp.clip(j * NA_QROWS - NA_WIN_ROWS // 2, 0, ROWS - NA_SLAB)
        typ = jnp.where(j == 0, 0, jnp.where(j == n_blocks - 1, 2, 1))
        qs = pl.multiple_of(j * NA_QB, NA_QB)
        kst = pl.multiple_of(ks * GRID_W, GRID_W)
        q = q_ref[pl.ds(qs, NA_QB), :]
        k = k_ref[pl.ds(kst, NA_KB), :]
        v = v_ref[pl.ds(kst, NA_KB), :]
        s_win = _dot_nt(q, k) * ATT_SCALE + tab_ref[typ, 0]
        s_ctx = _dot_nt(q, kc) * ATT_SCALE
        o_ref[pl.ds(qs, NA_QB), :] = _softmax_av([(s_win, v), (s_ctx, vc)]).astype(BF16)
        return carry

    lax.fori_loop(0, n_blocks, body, 0)


def na_bias_table(rpb):
    def one(r0, ks):
        r = r0 + jnp.arange(NA_QROWS)
        kr = ks + jnp.arange(NA_SLAB)
        start = jnp.clip(r - NA_WIN_ROWS // 2, 0, ROWS - NA_WIN_ROWS)
        row_ok = (kr[None, :] >= start[:, None]) & (kr[None, :] < start[:, None] + NA_WIN_ROWS)
        row_idx = jnp.clip(kr[None, :] - r[:, None] + NA_WIN_ROWS - 1, 0, 2 * NA_WIN_ROWS - 2)
        cq = jnp.arange(GRID_W)
        col_start = jnp.clip(cq - NA_WIN_COLS // 2, 0, GRID_W - NA_WIN_COLS)
        col_ok = (cq[None, :] >= col_start[:, None]) & (cq[None, :] < col_start[:, None] + NA_WIN_COLS)
        col_idx = jnp.clip(cq[None, :] - cq[:, None] + NA_WIN_COLS - 1, 0, 2 * NA_WIN_COLS - 2)
        bias = rpb.astype(F32)[:, row_idx[:, None, :, None], col_idx[None, :, None, :]]
        ok = row_ok[:, None, :, None] & col_ok[None, :, None, :]
        return jnp.where(ok[None], bias, NEG_INF).reshape(NA_HEADS, NA_QB, NA_KB)

    mid = 2 * NA_QROWS
    last = ROWS - NA_QROWS
    return jnp.stack([one(0, 0), one(mid, mid - NA_WIN_ROWS // 2), one(last, ROWS - NA_SLAB)])


def neighborhood_attention(p, table):
    cb = S // L
    return pl.pallas_call(
        _na_kernel,
        grid=(NA_HEADS, B),
        in_specs=[pl.BlockSpec((S, HD), lambda h, b: (b, h)),
                  pl.BlockSpec((S, HD), lambda h, b: (b, NA_HEADS + h)),
                  pl.BlockSpec((S, HD), lambda h, b: (b, 2 * NA_HEADS + h)),
                  pl.BlockSpec((L, HD), lambda h, b: (B * cb + b, NA_HEADS + h)),
                  pl.BlockSpec((L, HD), lambda h, b: (B * cb + b, 2 * NA_HEADS + h)),
                  pl.BlockSpec((3, 1, NA_QB, NA_KB), lambda h, b: (0, h, 0, 0))],
        out_specs=pl.BlockSpec((S, HD), lambda h, b: (b, h)),
        out_shape=jax.ShapeDtypeStruct((T_LAT, NA_HEADS * HD), BF16),
        compiler_params=_cp(("arbitrary", "arbitrary")),
        name="neighborhood_attention",
    )(p, p, p, p, p, table)


GQA_TQ = 512
GQA_QCOL = 3 * NA_HEADS
GQA_KCOL = GQA_QCOL + GQA_Q_HEADS
GQA_VCOL = GQA_KCOL + GQA_KV_HEADS


def _gqa_kernel(q_ref, k_ref, v_ref, kc_ref, vc_ref, cq_ref, sq_ref, ck_ref, sk_ref, qg_ref, kg_ref,
                o_ref, kn_ref, kcn_ref):
    @pl.when((pl.program_id(2) == 0) & (pl.program_id(3) == 0))
    def _():
        kn = _rope(_rms_head(k_ref[...].astype(F32), kg_ref[...]), ck_ref[...], sk_ref[...])
        kn_ref[...] = kn.astype(BF16)
        kcn_ref[...] = _rms_head(kc_ref[...].astype(F32), kg_ref[...]).astype(BF16)

    q = _rope(_rms_head(q_ref[...].astype(F32), qg_ref[...]), cq_ref[...], sq_ref[...]).astype(BF16)
    s_lat = _dot_nt(q, kn_ref[...]) * ATT_SCALE
    s_ctx = _dot_nt(q, kcn_ref[...]) * ATT_SCALE
    o_ref[...] = _softmax_av([(s_lat, v_ref[...]), (s_ctx, vc_ref[...])]).astype(BF16)


def gqa_attention(p, cosf, sinf, q_gain, k_gain):
    nq = S // GQA_TQ
    cb = S // L
    grid = (B, GQA_KV_HEADS, nq, GQA_GROUP)
    return pl.pallas_call(
        _gqa_kernel,
        grid=grid,
        in_specs=[pl.BlockSpec((GQA_TQ, HD), lambda b, n, i, g: (b * nq + i, GQA_QCOL + n * GQA_GROUP + g)),
                  pl.BlockSpec((S, HD), lambda b, n, i, g: (b, GQA_KCOL + n)),
                  pl.BlockSpec((S, HD), lambda b, n, i, g: (b, GQA_VCOL + n)),
                  pl.BlockSpec((L, HD), lambda b, n, i, g: (B * cb + b, GQA_KCOL + n)),
                  pl.BlockSpec((L, HD), lambda b, n, i, g: (B * cb + b, GQA_VCOL + n)),
                  pl.BlockSpec((GQA_TQ, HD), lambda b, n, i, g: (i, 0)),
                  pl.BlockSpec((GQA_TQ, HD), lambda b, n, i, g: (i, 0)),
                  pl.BlockSpec((S, HD), lambda b, n, i, g: (0, 0)),
                  pl.BlockSpec((S, HD), lambda b, n, i, g: (0, 0)),
                  pl.BlockSpec((1, HD), lambda b, n, i, g: (0, 0)),
                  pl.BlockSpec((1, HD), lambda b, n, i, g: (0, 0))],
        out_specs=pl.BlockSpec((GQA_TQ, HD), lambda b, n, i, g: (b * nq + i, n * GQA_GROUP + g)),
        out_shape=jax.ShapeDtypeStruct((T_LAT, GQA_Q_HEADS * HD), BF16),
        scratch_shapes=[pltpu.VMEM((S, HD), BF16), pltpu.VMEM((L, HD), BF16)],
        compiler_params=_cp(("arbitrary",) * 4, vmem=56 * 1024 * 1024),
        name="gqa_attention",
    )(p, p, p, p, p, cosf, sinf, cosf, sinf, q_gain.reshape(1, HD), k_gain.reshape(1, HD))


def _ctx_attn_kernel(p_ref, qg_ref, kg_ref, o_ref):
    def col(c):
        return p_ref[:, c * HD:(c + 1) * HD]

    for h in range(NA_HEADS):
        s = _dot_nt(col(h), col(NA_HEADS + h)) * ATT_SCALE
        o_ref[:, h * HD:(h + 1) * HD] = _softmax_av([(s, col(2 * NA_HEADS + h))]).astype(BF16)
    for n in range(GQA_KV_HEADS):
        kn = _rms_head(col(GQA_KCOL + n).astype(F32), kg_ref[...]).astype(BF16)
        v = col(GQA_VCOL + n)
        for g in range(GQA_GROUP):
            h = n * GQA_GROUP + g
            qn = _rms_head(col(GQA_QCOL + h).astype(F32), qg_ref[...]).astype(BF16)
            s = _dot_nt(qn, kn) * ATT_SCALE
            o_ref[:, (NA_HEADS + h) * HD:(NA_HEADS + h + 1) * HD] = _softmax_av([(s, v)]).astype(BF16)


def ctx_attention(p, q_gain, k_gain):
    cb = S // L
    return pl.pallas_call(
        _ctx_attn_kernel,
        grid=(B,),
        in_specs=[pl.BlockSpec((L, ATTN_IN), lambda b: (B * cb + b, 0)),
                  pl.BlockSpec((1, HD), lambda b: (0, 0)),
                  pl.BlockSpec((1, HD), lambda b: (0, 0))],
        out_specs=pl.BlockSpec((L, D), lambda b: (b, 0)),
        out_shape=jax.ShapeDtypeStruct((T_CTX, D), BF16),
        compiler_params=_cp(("arbitrary",)),
        name="ctx_attention",
    )(p, q_gain.reshape(1, HD), k_gain.reshape(1, HD))


def _log_sigmoid(x):
    return -(jnp.maximum(-x, 0.0) + jnp.log1p(jnp.exp(-jnp.abs(x))))


def _dot_hi(a, b):
    return jnp.dot(a, b, precision=lax.Precision.HIGHEST, preferred_element_type=F32)


def _gate_kernel(h_ref, wg_ref, wgt_ref, b_ref, bt_ref, lt_ref, ut_ref, col_ref, row_ref):
    nh = ML_HEADS
    hx = h_ref[...]
    g = jnp.dot(hx, wg_ref[...], preferred_element_type=F32) + b_ref[...]
    gt = _dot_nt(wgt_ref[...], hx) + bt_ref[...]
    li = g[:, 0:2 * nh]
    lf = _log_sigmoid(g[:, 2 * nh:4 * nh])
    lit = gt[0:2 * nh]
    lft = _log_sigmoid(gt[2 * nh:4 * nh])
    lt = lt_ref[...]
    ut = ut_ref[...]
    lane = lax.broadcasted_iota(I32, lf.shape, 1)
    bc = jnp.where(lane < nh, _dot_hi(lt, lf), _dot_hi(ut, lf))
    tot = jnp.sum(lf, axis=0, keepdims=True)
    aend = tot - bc + li
    col_ref[...] = jnp.concatenate([bc, aend, jnp.zeros((ML_CH, LANE - 4 * nh), F32)], axis=1)
    sub = lax.broadcasted_iota(I32, lft.shape, 0)
    bct = jnp.where(sub < nh, _dot_hi(lft, ut), _dot_hi(lft, lt))
    tott = jnp.sum(lft, axis=1, keepdims=True)
    gtr = lit - bct
    row_ref[0] = jnp.concatenate([bct, gtr, tott + gtr, jnp.broadcast_to(tott, bct.shape)], axis=0)


def mlstm_gates(hx, wg, gate_b):
    nh = ML_HEADS
    n_ch = T_ALL // ML_CH
    wg_pad = jnp.zeros((D, LANE), BF16).at[:, :4 * nh].set(wg.astype(BF16))
    b_pad = jnp.zeros((1, LANE), F32).at[0, :4 * nh].set(gate_b.reshape(-1))
    wgt = wg.astype(BF16).T
    bt = gate_b.reshape(4 * nh, 1).astype(F32)
    lt = jnp.tril(jnp.ones((ML_CH, ML_CH), F32))
    ut = jnp.triu(jnp.ones((ML_CH, ML_CH), F32))
    col, row = pl.pallas_call(
        _gate_kernel,
        grid=(n_ch,),
        in_specs=[pl.BlockSpec((ML_CH, D), lambda i: (i, 0)),
                  pl.BlockSpec((D, LANE), lambda i: (0, 0)),
                  pl.BlockSpec((4 * nh, D), lambda i: (0, 0)),
                  pl.BlockSpec((1, LANE), lambda i: (0, 0)),
                  pl.BlockSpec((4 * nh, 1), lambda i: (0, 0)),
                  pl.BlockSpec((ML_CH, ML_CH), lambda i: (0, 0)),
                  pl.BlockSpec((ML_CH, ML_CH), lambda i: (0, 0))],
        out_specs=[pl.BlockSpec((ML_CH, LANE), lambda i: (i, 0)),
                   pl.BlockSpec((1, 8 * nh, ML_CH), lambda i: (i, 0, 0))],
        out_shape=[jax.ShapeDtypeStruct((T_ALL, LANE), F32),
                   jax.ShapeDtypeStruct((n_ch, 8 * nh, ML_CH), F32)],
        compiler_params=_cp(("arbitrary",)),
        name="mlstm_gates",
    )(hx, wg_pad, wgt, b_pad, bt, lt, ut)
    col3 = col[:, :4 * nh].reshape(T_ALL, 2, 2 * nh).transpose(2, 0, 1)
    col3 = jnp.pad(col3, ((0, 0), (0, 0), (0, 6)))
    row4 = row.reshape(n_ch, 4, 2 * nh, ML_CH).transpose(2, 0, 1, 3)
    row4 = jnp.pad(row4, ((0, 0), (0, 0), (0, 4), (0, 0)))
    return col3, row4


def _mlstm_kernel(q_ref, k_ref, v_ref, col_ref, row_ref, o_ref, c_ref, n_ref, m_ref):
    d = pl.program_id(1)
    st = pl.program_id(3)

    @pl.when(st == 0)
    def _():
        c_ref[...] = jnp.zeros_like(c_ref)
        n_ref[...] = jnp.zeros_like(n_ref)
        m_ref[...] = jnp.zeros_like(m_ref)

    q = q_ref[...]
    kf = k_ref[...].astype(F32) * ML_KSCALE
    kb = kf.astype(BF16)
    v = v_ref[...]
    col = col_ref[0]
    row = row_ref[0, 0]
    bc_col = col[:, 0:1]
    aend_col = col[:, 1:2]
    bc_row = row[0:1]
    g_row = row[1:2]
    aend_row = row[2:3]
    btot = row[3:4, 0:1]
    m_st = m_ref[...]
    c_st = c_ref[...]
    n_st = n_ref[...]
    m_new = jnp.maximum(btot + m_st, jnp.max(aend_row, axis=1, keepdims=True))

    @pl.when(st > 0)
    def _():
        r = lax.broadcasted_iota(I32, (ML_CH, ML_CH), 0)
        c = lax.broadcasted_iota(I32, (ML_CH, ML_CH), 1)
        causal = (r - c) * (1 - 2 * d) >= 0
        d_mat = jnp.where(causal, bc_col + g_row, -jnp.inf)
        m_row = jnp.maximum(bc_col + m_st, jnp.max(d_mat, axis=1, keepdims=True))
        w_inter = jnp.exp(bc_col + m_st - m_row)
        s_mat = _dot_nt(q, kb) * jnp.exp(d_mat - m_row)
        num = (w_inter * jnp.dot(q, c_st.astype(BF16), preferred_element_type=F32)
               + jnp.dot(s_mat.astype(BF16), v, preferred_element_type=F32))
        den = (w_inter * jnp.sum(q.astype(F32) * n_st, axis=1, keepdims=True)
               + jnp.sum(s_mat, axis=1, keepdims=True))
        o_ref[0] = num / jnp.maximum(jnp.abs(den), jnp.exp(-m_row))

    w_end_col = jnp.exp(aend_col - m_new)
    w_end_row = jnp.exp(aend_row - m_new)
    decay = jnp.exp(btot + m_st - m_new)
    kw = (kf * w_end_col).astype(BF16)
    c_ref[...] = decay * c_st + lax.dot_general(kw, v, (((0,), (0,)), ((), ())), preferred_element_type=F32)
    w8 = jnp.broadcast_to(w_end_row, (8, ML_CH)).astype(BF16)
    n_ref[...] = decay * n_st + jnp.dot(w8, kb, preferred_element_type=F32)[0:1]
    m_ref[...] = m_new
    del bc_row


def mlstm_scan(p, col3, row4):
    n_lat = S // ML_CH
    steps = n_lat + 1
    lat_blocks = T_LAT // ML_CH

    def chunk(b, d, st):
        c = jnp.where(d == 0, st - 1, n_lat - st)
        return jnp.where(st == 0, lat_blocks + b, b * n_lat + c)

    def out_chunk(b, d, st):
        s1 = jnp.maximum(st, 1)
        return b * n_lat + jnp.where(d == 0, s1 - 1, n_lat - s1)

    vcol = (2 * ML_HEADS * ML_QK) // ML_V
    return pl.pallas_call(
        _mlstm_kernel,
        grid=(B, 2, ML_HEADS, steps),
        in_specs=[pl.BlockSpec((ML_CH, ML_QK), lambda b, d, h, s: (chunk(b, d, s), h)),
                  pl.BlockSpec((ML_CH, ML_QK), lambda b, d, h, s: (chunk(b, d, s), ML_HEADS + h)),
                  pl.BlockSpec((ML_CH, ML_V), lambda b, d, h, s: (chunk(b, d, s), vcol + h)),
                  pl.BlockSpec((1, ML_CH, 8), lambda b, d, h, s: (d * ML_HEADS + h, chunk(b, d, s), 0)),
                  pl.BlockSpec((1, 1, 8, ML_CH), lambda b, d, h, s: (d * ML_HEADS + h, chunk(b, d, s), 0, 0))],
        out_specs=pl.BlockSpec((1, ML_CH, ML_V), lambda b, d, h, s: (d, out_chunk(b, d, s), h)),
        out_shape=jax.ShapeDtypeStruct((2, T_LAT, ML_HEADS * ML_V), F32),
        scratch_shapes=[pltpu.VMEM((ML_QK, ML_V), F32), pltpu.VMEM((1, ML_QK), F32), pltpu.VMEM((1, 1), F32)],
        compiler_params=_cp(("arbitrary",) * 4),
        name="mlstm_scan",
    )(p, p, p, col3, row4)


def _readout_kernel(h_ref, o_ref, g_ref, a_ref):
    hs = h_ref[0] + h_ref[1]
    for h in range(ML_HEADS):
        sl = slice(h * ML_V, (h + 1) * ML_V)
        x = hs[:, sl]
        hn = x * lax.rsqrt(jnp.mean(x * x, axis=-1, keepdims=True) + EPS) * g_ref[:, sl]
        a_ref[:, sl] = (hn * jax.nn.sigmoid(o_ref[:, sl].astype(F32))).astype(BF16)


def mlstm_readout(hdir, p, head_gain):
    tm = 256
    ocol = (2 * ML_HEADS * ML_QK + ML_HEADS * ML_V) // D
    return pl.pallas_call(
        _readout_kernel,
        grid=(T_LAT // tm,),
        in_specs=[pl.BlockSpec((2, tm, D), lambda i: (0, i, 0)),
                  pl.BlockSpec((tm, D), lambda i: (i, ocol)),
                  pl.BlockSpec((1, D), lambda i: (0, 0))],
        out_specs=pl.BlockSpec((tm, D), lambda i: (i, 0)),
        out_shape=jax.ShapeDtypeStruct((T_LAT, D), BF16),
        compiler_params=_cp(("arbitrary",)),
        name="mlstm_readout",
    )(hdir, p, head_gain.reshape(1, D))


ROUTER_TM = 512


def _router_kernel(h_ref, w_ref, rb_ref, erow_ref, tri_ref, eidx_ref, wts_ref, pos_ref, cnt_ref, carry_ref):
    ng = N_GROUPS
    epg = N_EXPERTS // N_GROUPS
    tm = ROUTER_TM
    ninf = -jnp.inf

    @pl.when(pl.program_id(0) == 0)
    def _():
        carry_ref[...] = jnp.zeros_like(carry_ref)

    s = jax.nn.sigmoid(_dot_nt(w_ref[...], h_ref[...]))
    ssel = s + rb_ref[...]
    sraw = [s[ng * j:ng * (j + 1)] for j in range(epg)]
    slab = [ssel[ng * j:ng * (j + 1)] for j in range(epg)]
    m1 = functools.reduce(jnp.maximum, slab)
    jfirst = functools.reduce(jnp.minimum, [jnp.where(slab[j] == m1, j, epg) for j in range(epg)])
    m2 = functools.reduce(jnp.maximum, [jnp.where(jfirst == j, ninf, slab[j]) for j in range(epg)])
    gs = m1 + m2
    giota = lax.broadcasted_iota(I32, (ng, tm), 0)
    gsel = jnp.zeros((ng, tm), F32)
    for _ in range(TOPK_GROUPS):
        mx = jnp.max(gs, axis=0, keepdims=True)
        gi = jnp.min(jnp.where(gs == mx, giota, ng), axis=0, keepdims=True)
        hit = giota == gi
        gsel = jnp.where(hit, 1.0, gsel)
        gs = jnp.where(hit, ninf, gs)
    msl = [jnp.where(gsel > 0.0, slab[j], ninf) for j in range(epg)]
    eid = [giota * epg + j for j in range(epg)]
    selm = [jnp.zeros((ng, tm), F32) for _ in range(epg)]
    e_list, w_list = [], []
    for _ in range(TOP_K):
        mx = jnp.max(functools.reduce(jnp.maximum, msl), axis=0, keepdims=True)
        cand = functools.reduce(jnp.minimum, [jnp.where(msl[j] == mx, eid[j], N_EXPERTS) for j in range(epg)])
        esel = jnp.min(cand, axis=0, keepdims=True)
        hits = [eid[j] == esel for j in range(epg)]
        wk = functools.reduce(lambda a, b: a + b, [jnp.where(hits[j], sraw[j], 0.0) for j in range(epg)])
        w_list.append(jnp.sum(wk, axis=0, keepdims=True))
        e_list.append(esel)
        msl = [jnp.where(hits[j], ninf, msl[j]) for j in range(epg)]
        selm = [jnp.where(hits[j], 1.0, selm[j]) for j in range(epg)]
    wsum = functools.reduce(lambda a, b: a + b, w_list)
    wts_ref[...] = jnp.concatenate([w / wsum * ROUTED_SCALE for w in w_list], axis=0)
    eidx_ref[...] = jnp.concatenate(e_list, axis=0)
    sel = jnp.concatenate(selm, axis=0)
    carry = carry_ref[...]
    posfull = jnp.dot(sel.astype(BF16), tri_ref[...], preferred_element_type=F32) + carry
    erow = erow_ref[...]
    pos = [jnp.sum(jnp.where(erow == e, posfull, 0.0), axis=0, keepdims=True) for e in e_list]
    pos_ref[...] = jnp.concatenate(pos, axis=0).astype(I32)
    carry = carry + jnp.sum(sel, axis=1, keepdims=True)
    carry_ref[...] = carry
    cnt_ref[...] = carry


def moe_router(hx, router_w, router_b, n_tok):
    tm = ROUTER_TM
    epg = N_EXPERTS // N_GROUPS
    perm = (jnp.arange(N_EXPERTS) % N_GROUPS) * epg + jnp.arange(N_EXPERTS) // N_GROUPS
    w_t = router_w.astype(BF16).T[perm]
    rb = router_b.astype(F32)[perm].reshape(N_EXPERTS, 1)
    erow = perm.astype(I32).reshape(N_EXPERTS, 1)
    tri = jnp.triu(jnp.ones((tm, tm), BF16), 1)
    eidx, wts, pos, counts = pl.pallas_call(
        _router_kernel,
        grid=(n_tok // tm,),
        in_specs=[pl.BlockSpec((tm, D), lambda i: (i, 0)),
                  pl.BlockSpec((N_EXPERTS, D), lambda i: (0, 0)),
                  pl.BlockSpec((N_EXPERTS, 1), lambda i: (0, 0)),
                  pl.BlockSpec((N_EXPERTS, 1), lambda i: (0, 0)),
                  pl.BlockSpec((tm, tm), lambda i: (0, 0))],
        out_specs=[pl.BlockSpec((TOP_K, tm), lambda i: (0, i)),
                   pl.BlockSpec((TOP_K, tm), lambda i: (0, i)),
                   pl.BlockSpec((TOP_K, tm), lambda i: (0, i)),
                   pl.BlockSpec((N_EXPERTS, 1), lambda i: (0, 0))],
        out_shape=[jax.ShapeDtypeStruct((TOP_K, n_tok), I32),
                   jax.ShapeDtypeStruct((TOP_K, n_tok), F32),
                   jax.ShapeDtypeStruct((TOP_K, n_tok), I32),
                   jax.ShapeDtypeStruct((N_EXPERTS, 1), F32)],
        scratch_shapes=[pltpu.VMEM((N_EXPERTS, 1), F32)],
        compiler_params=_cp(("arbitrary",)),
        name="moe_router",
    )(hx, w_t, rb, erow, tri)
    return eidx, wts, pos, counts.reshape(N_EXPERTS)[perm]


DISPATCH_TM = 512


def _dispatch_kernel(slot_ref, hx_hbm, xs_hbm, sem):
    base = pl.program_id(0) * DISPATCH_TM

    def issue(t, carry):
        src = hx_hbm.at[pl.ds(pl.multiple_of((base + t) * PK_S, PK_S), PK_S), :]
        for k in range(TOP_K):
            dst = xs_hbm.at[pl.ds(pl.multiple_of(slot_ref[k, t] * PK_S, PK_S), PK_S), :]
            pltpu.make_async_copy(src, dst, sem).start()
        return carry

    lax.fori_loop(0, DISPATCH_TM, issue, 0)

    def drain(t, carry):
        for k in range(TOP_K):
            pltpu.make_async_copy(hx_hbm.at[pl.ds(0, PK_S), :], xs_hbm.at[pl.ds(0, PK_S), :], sem).wait()
        return carry

    lax.fori_loop(0, DISPATCH_TM, drain, 0)


def moe_dispatch(slots, hx_packed, n_tok, n_rows):
    tm = DISPATCH_TM
    return pl.pallas_call(
        _dispatch_kernel,
        grid=(n_tok // tm,),
        in_specs=[pl.BlockSpec((TOP_K, tm), lambda i: (0, i), memory_space=pltpu.SMEM),
                  pl.BlockSpec(memory_space=pl.ANY)],
        out_specs=pl.BlockSpec(memory_space=pl.ANY),
        out_shape=jax.ShapeDtypeStruct((n_rows * PK_S, LANE), U32),
        scratch_shapes=[pltpu.SemaphoreType.DMA(())],
        compiler_params=_cp(("arbitrary",)),
        name="moe_dispatch",
    )(slots, hx_packed)


def _expert_kernel(be_ref, valid_ref, nused_ref, xs_ref, w1_ref, w3_ref, w2_ref, y_ref, w1b, w3b, w2b, xb):
    i = pl.program_id(0)
    bm = EXP_BM

    @pl.when(i < nused_ref[0])
    def _():
        prev = be_ref[jnp.maximum(i - 1, 0)]

        @pl.when((i == 0) | (be_ref[i] != prev))
        def _():
            w1b[...] = w1_ref[0, 0].astype(BF16)
            w3b[...] = w3_ref[0, 0].astype(BF16)
            w2b[...] = w2_ref[0, 0].astype(BF16)

        rows = lax.broadcasted_iota(I32, (bm, LANE), 0)
        live = rows < valid_ref[i]
        for s in range(PK_S):
            u = jnp.where(live, xs_ref[pl.ds(s, bm, stride=PK_S), :], jnp.uint32(0))
            xb[:, s * LANE:(s + 1) * LANE] = pltpu.bitcast(u << 16, F32).astype(BF16)
            xb[:, PK_W + s * LANE:PK_W + (s + 1) * LANE] = (
                pltpu.bitcast(u & jnp.uint32(0xFFFF0000), F32).astype(BF16))
        x = xb[...]
        h1 = jnp.dot(x, w1b[...], preferred_element_type=F32)
        h3 = jnp.dot(x, w3b[...], preferred_element_type=F32)
        a = (h1 * jax.nn.sigmoid(h1) * h3).astype(BF16)
        y = jnp.dot(a, w2b[...], preferred_element_type=F32)
        for s in range(Y_S):
            y_ref[pl.ds(s, bm, stride=Y_S), :] = y[:, s * LANE:(s + 1) * LANE]


def moe_experts(block_e, valid, n_used, xs, w1, w3, w2, layer, n_blocks):
    bm = EXP_BM

    def blk(i, be, va, nu):
        return jnp.minimum(i, nu[0] - 1)

    grid_spec = pltpu.PrefetchScalarGridSpec(
        num_scalar_prefetch=3,
        grid=(n_blocks,),
        in_specs=[pl.BlockSpec((bm * PK_S, LANE), lambda i, be, va, nu: (blk(i, be, va, nu), 0)),
                  pl.BlockSpec((1, 1, D, EXPERT_DIM), lambda i, be, va, nu: (layer, be[blk(i, be, va, nu)], 0, 0)),
                  pl.BlockSpec((1, 1, D, EXPERT_DIM), lambda i, be, va, nu: (layer, be[blk(i, be, va, nu)], 0, 0)),
                  pl.BlockSpec((1, 1, EXPERT_DIM, D), lambda i, be, va, nu: (layer, be[blk(i, be, va, nu)], 0, 0))],
        out_specs=pl.BlockSpec((bm * Y_S, LANE), lambda i, be, va, nu: (blk(i, be, va, nu), 0)),
        scratch_shapes=[pltpu.VMEM((D, EXPERT_DIM), BF16), pltpu.VMEM((D, EXPERT_DIM), BF16),
                        pltpu.VMEM((EXPERT_DIM, D), BF16), pltpu.VMEM((bm, D), BF16)],
    )
    return pl.pallas_call(
        _expert_kernel,
        grid_spec=grid_spec,
        out_shape=jax.ShapeDtypeStruct((n_blocks * bm * Y_S, LANE), F32),
        compiler_params=_cp(("arbitrary",), vmem=56 * 1024 * 1024),
        name="moe_experts",
    )(block_e, valid, n_used, xs, w1, w3, w2)


def _shared_kernel(x_ref, w1_ref, w3_ref, w2_ref, o_ref):
    x = x_ref[...]
    h1 = jnp.dot(x, w1_ref[...], preferred_element_type=F32)
    h3 = jnp.dot(x, w3_ref[...], preferred_element_type=F32)
    a = (h1 * jax.nn.sigmoid(h1) * h3).astype(BF16)
    o_ref[...] = jnp.dot(a, w2_ref[...], preferred_element_type=F32)


def shared_expert(hx, w1, w3, w2, n_tok):
    tm = 512
    return pl.pallas_call(
        _shared_kernel,
        grid=(n_tok // tm,),
        in_specs=[pl.BlockSpec((tm, D), lambda i: (i, 0)),
                  pl.BlockSpec((D, EXPERT_DIM), lambda i: (0, 0)),
                  pl.BlockSpec((D, EXPERT_DIM), lambda i: (0, 0)),
                  pl.BlockSpec((EXPERT_DIM, D), lambda i: (0, 0))],
        out_specs=pl.BlockSpec((tm, D), lambda i: (i, 0)),
        out_shape=jax.ShapeDtypeStruct((n_tok, D), F32),
        compiler_params=_cp(("arbitrary",)),
        name="shared_expert",
    )(hx, w1, w3, w2)


COMBINE_TM = 128


def _combine_kernel(slot_ref, w_ref, sh_ref, x_ref, mod_ref, y_hbm, o_ref, buf, sem):
    tm = COMBINE_TM

    def issue(t, carry):
        for k in range(TOP_K):
            src = y_hbm.at[pl.ds(pl.multiple_of(slot_ref[k, t] * Y_S, Y_S), Y_S), :]
            dst = buf.at[pl.ds(pl.multiple_of((k * tm + t) * Y_S, Y_S), Y_S), :]
            pltpu.make_async_copy(src, dst, sem).start()
        return carry

    lax.fori_loop(0, tm, issue, 0)

    def drain(t, carry):
        for k in range(TOP_K):
            pltpu.make_async_copy(y_hbm.at[pl.ds(0, Y_S), :], buf.at[pl.ds(0, Y_S), :], sem).wait()
        return carry

    lax.fori_loop(0, tm, drain, 0)

    w = w_ref[...]
    gate = mod_ref[0][5:6]
    for s in range(Y_S):
        sl = slice(s * LANE, (s + 1) * LANE)
        acc = None
        for k in range(TOP_K):
            term = w[:, k:k + 1] * buf[pl.ds(k * tm * Y_S + s, tm, stride=Y_S), :]
            acc = term if acc is None else acc + term
        o_ref[:, sl] = x_ref[:, sl] + gate[:, sl] * (sh_ref[:, sl] + acc)


def moe_combine(slots, wts_tok, shared, x, mod, y, n_tok):
    tm = COMBINE_TM
    return pl.pallas_call(
        _combine_kernel,
        grid=(n_tok // tm,),
        in_specs=[pl.BlockSpec((TOP_K, tm), lambda i: (0, i), memory_space=pltpu.SMEM),
                  pl.BlockSpec((tm, TOP_K), lambda i: (i, 0)),
                  pl.BlockSpec((tm, D), lambda i: (i, 0)),
                  pl.BlockSpec((tm, D), lambda i: (i, 0)),
                  pl.BlockSpec((1, 6, D), lambda i: (_mod_row(i * tm), 0, 0)),
                  pl.BlockSpec(memory_space=pl.ANY)],
        out_specs=pl.BlockSpec((tm, D), lambda i: (i, 0)),
        out_shape=jax.ShapeDtypeStruct((n_tok, D), F32),
        scratch_shapes=[pltpu.VMEM((TOP_K * tm * Y_S, LANE), F32), pltpu.SemaphoreType.DMA(())],
        compiler_params=_cp(("arbitrary",)),
        name="moe_combine",
    )(slots, wts_tok, shared, x, mod, y)


def moe_layer(x, mod, norm_gain, router_w, router_b, exp_w1, exp_w3, exp_w2, sw1, sw3, sw2, layer, n_tok):
    bm = EXP_BM
    n_blocks = -(-n_tok * TOP_K // bm) + N_EXPERTS
    hx, hx_packed = norm_mod(x, norm_gain, mod, 3, n_tok, pack=True)
    eidx, wts, pos, counts = moe_router(hx, router_w, router_b, n_tok)
    shared = shared_expert(hx, sw1.astype(BF16), sw3.astype(BF16), sw2.astype(BF16), n_tok)
    cnt = counts.reshape(N_EXPERTS).astype(I32)
    padded = (cnt + bm - 1) // bm * bm
    pad_end = jnp.cumsum(padded)
    pad_start = pad_end - padded
    slots = pad_start[eidx] + pos
    blk_row = jnp.arange(n_blocks, dtype=I32) * bm
    block_e = jnp.minimum(jnp.searchsorted(pad_end, blk_row, side='right'), N_EXPERTS - 1).astype(I32)
    valid = jnp.clip(cnt[block_e] - (blk_row - pad_start[block_e]), 0, bm).astype(I32)
    n_used = (pad_end[-1:] // bm).astype(I32)
    xs = moe_dispatch(slots, hx_packed, n_tok, n_blocks * bm)
    y = moe_experts(block_e, valid, n_used, xs, exp_w1, exp_w3, exp_w2, layer, n_blocks)
    return moe_combine(slots, wts.T, shared, x, mod, y, n_tok)


def _rope_tables():
    t = jnp.arange(S, dtype=I32)
    row = (t // GRID_W).astype(F32)
    col = (t % GRID_W).astype(F32)
    n_freq = HD // 4
    inv_freq = ROPE_THETA ** (-jnp.arange(n_freq, dtype=F32) / n_freq)
    ang = jnp.concatenate([row[:, None] * inv_freq, col[:, None] * inv_freq], axis=-1)
    cosf = jnp.repeat(jnp.cos(ang), 2, axis=-1)
    sinf = jnp.stack([-jnp.sin(ang), jnp.sin(ang)], axis=-1).reshape(S, HD)
    return cosf, sinf


def kernel(x, c, ctx, c_ctx, ada_w, ada_b, norm_mix, norm_ffn, attn_w_in, attn_w_out, attn_rpb, attn_q_gain,
           attn_k_gain, ml_w_in, ml_w_out, ml_gate_b, ml_head_gain, router_w, router_b, exp_w1, exp_w3, exp_w2,
           sh_w1, sh_w3, sh_w2, final_norm_gain):
    depth = ada_w.shape[0]
    xa = jnp.concatenate([x.reshape(T_LAT, D), ctx.reshape(T_CTX, D)], axis=0)
    cvec = jnp.concatenate([c, c_ctx[None], jnp.zeros((8 - B - 1, D), F32)], axis=0)
    mod_all = ada_ln(cvec, ada_w, ada_b).reshape(depth, 8, 6, D)
    cosf, sinf = _rope_tables()

    mod = mod_all[0]
    hx = norm_mod(xa, norm_mix[0], mod, 0, T_ALL, pack=False)
    p = matmul(hx, attn_w_in[0].astype(BF16), BF16)
    o_na = neighborhood_attention(p, na_bias_table(attn_rpb[0]))
    o_gqa = gqa_attention(p, cosf, sinf, attn_q_gain[0], attn_k_gain[0])
    o_ctx = ctx_attention(p, attn_q_gain[0], attn_k_gain[0])
    o_all = jnp.concatenate([jnp.concatenate([o_na, o_gqa], axis=1), o_ctx], axis=0)
    xa = matmul_gated_residual(o_all, attn_w_out[0].astype(BF16), xa, mod, 2)
    xa = moe_layer(xa, mod, norm_ffn[0], router_w[0], router_b[0], exp_w1, exp_w3, exp_w2,
                   sh_w1[0], sh_w3[0], sh_w2[0], 0, T_ALL)

    mod = mod_all[1]
    hx = norm_mod(xa, norm_mix[1], mod, 0, T_ALL, pack=False)
    w_in = ml_w_in[0]
    p = matmul(hx, w_in[:, :ML_MAIN].astype(BF16), BF16)
    col3, row4 = mlstm_gates(hx, w_in[:, ML_MAIN:], ml_gate_b[0])
    hdir = mlstm_scan(p, col3, row4)
    a = mlstm_readout(hdir, p, ml_head_gain[0])
    xl = matmul_gated_residual(a, ml_w_out[0].astype(BF16), xa, mod, 2)
    xl = moe_layer(xl, mod, norm_ffn[1], router_w[1], router_b[1], exp_w1, exp_w3, exp_w2,
                   sh_w1[1], sh_w3[1], sh_w2[1], 1, T_LAT)
    return final_norm(xl, final_norm_gain).reshape(B, S, D)
```

```python
import functools

import jax
import jax.numpy as jnp
from jax import lax
from jax.experimental import pallas as pl
from jax.experimental.pallas import tpu as pltpu

F32 = jnp.float32
BF16 = jnp.bfloat16
I32 = jnp.int32
U32 = jnp.uint32

D = 2048
B = 4
S = 4096
L = 256
T_LAT = B * S
T_CTX = B * L
T_ALL = T_LAT + T_CTX
GRID_W = 64
ROWS = S // GRID_W
HD = 128
NA_HEADS = 8
NA_WIN_ROWS = 8
NA_WIN_COLS = 16
GQA_Q_HEADS = 8
GQA_KV_HEADS = 2
GQA_GROUP = 4
ROPE_THETA = 10000.0
ATTN_IN = 4608
ML_HEADS = 8
ML_V = 256
ML_QK = 128
ML_MAIN = 6144
N_EXPERTS = 64
TOP_K = 8
N_GROUPS = 8
TOPK_GROUPS = 4
EXPERT_DIM = 512
ROUTED_SCALE = 2.5
EPS = 1e-6
NEG_INF = -1e30
ATT_SCALE = HD ** -0.5
ML_KSCALE = ML_QK ** -0.5

LANE = 128
NA_QROWS = 4
NA_SLAB = NA_QROWS + NA_WIN_ROWS - 1
NA_QB = NA_QROWS * GRID_W
NA_KB = NA_SLAB * GRID_W
ML_CH = 256
EXP_BM = 512
PK_W = D // 2
PK_S = PK_W // LANE
Y_S = D // LANE
VMEM_LIMIT = 48 * 1024 * 1024


def _cp(sem, vmem=VMEM_LIMIT):
    return pltpu.CompilerParams(dimension_semantics=sem, vmem_limit_bytes=vmem)


def _mod_row(start_row):
    return jnp.where(start_row < T_LAT, start_row // S, B)


def _ada_kernel(c_ref, w_ref, b_ref, o_ref):
    c = c_ref[...]
    a = (c * jax.nn.sigmoid(c)).astype(BF16)
    w = w_ref[0].astype(BF16)
    o_ref[0] = jnp.dot(a, w, preferred_element_type=F32) + b_ref[0]


def ada_ln(cvec, ada_w, ada_b):
    depth = ada_w.shape[0]
    n = ada_w.shape[2]
    tn = 1024
    return pl.pallas_call(
        _ada_kernel,
        grid=(depth, n // tn),
        in_specs=[pl.BlockSpec((8, D), lambda l, j: (0, 0)),
                  pl.BlockSpec((1, D, tn), lambda l, j: (l, 0, j)),
                  pl.BlockSpec((1, 1, tn), lambda l, j: (l, 0, j))],
        out_specs=pl.BlockSpec((1, 8, tn), lambda l, j: (l, 0, j)),
        out_shape=jax.ShapeDtypeStruct((depth, 8, n), F32),
        compiler_params=_cp(("arbitrary", "arbitrary")),
        name="ada_ln",
    )(cvec, ada_w, ada_b.reshape(depth, 1, n))


def _norm_mod_kernel(x_ref, g_ref, mod_ref, *out_refs, base, pack, tm):
    x = x_ref[...]
    y = x * lax.rsqrt(jnp.mean(x * x, axis=-1, keepdims=True) + EPS) * g_ref[...]
    m = mod_ref[0]
    h = y * (1.0 + m[base + 1:base + 2]) + m[base:base + 1]
    hb = h.astype(BF16)
    out_refs[0][...] = hb
    if pack:
        u = pltpu.bitcast(hb.astype(F32), U32)
        for s in range(PK_S):
            lo = u[:, s * LANE:(s + 1) * LANE]
            hi = u[:, PK_W + s * LANE:PK_W + (s + 1) * LANE]
            out_refs[1][pl.ds(s, tm, stride=PK_S), :] = (hi & jnp.uint32(0xFFFF0000)) | (lo >> 16)


def norm_mod(x, gain, mod, base, n_rows, pack):
    tm = 256
    out_shape = [jax.ShapeDtypeStruct((n_rows, D), BF16)]
    out_specs = [pl.BlockSpec((tm, D), lambda i: (i, 0))]
    if pack:
        out_shape.append(jax.ShapeDtypeStruct((n_rows * PK_S, LANE), U32))
        out_specs.append(pl.BlockSpec((tm * PK_S, LANE), lambda i: (i, 0)))
    res = pl.pallas_call(
        functools.partial(_norm_mod_kernel, base=base, pack=pack, tm=tm),
        grid=(n_rows // tm,),
        in_specs=[pl.BlockSpec((tm, D), lambda i: (i, 0)),
                  pl.BlockSpec((1, D), lambda i: (0, 0)),
                  pl.BlockSpec((1, 6, D), lambda i: (_mod_row(i * tm), 0, 0))],
        out_specs=out_specs,
        out_shape=out_shape,
        compiler_params=_cp(("arbitrary",)),
        name="norm_mod",
    )(x, gain.reshape(1, D), mod)
    return res if pack else res[0]


def _final_norm_kernel(x_ref, g_ref, o_ref):
    x = x_ref[...]
    o_ref[...] = x * lax.rsqrt(jnp.mean(x * x, axis=-1, keepdims=True) + EPS) * g_ref[...]


def final_norm(x, gain):
    tm = 256
    return pl.pallas_call(
        _final_norm_kernel,
        grid=(T_LAT // tm,),
        in_specs=[pl.BlockSpec((tm, D), lambda i: (i, 0)),
                  pl.BlockSpec((1, D), lambda i: (0, 0))],
        out_specs=pl.BlockSpec((tm, D), lambda i: (i, 0)),
        out_shape=jax.ShapeDtypeStruct((T_LAT, D), F32),
        compiler_params=_cp(("arbitrary",)),
        name="final_norm",
    )(x, gain.reshape(1, D))


def _mm_kernel(a_ref, w_ref, o_ref):
    o_ref[...] = jnp.dot(a_ref[...], w_ref[...], preferred_element_type=F32).astype(o_ref.dtype)


def matmul(a, w, out_dtype, tm=1024, tn=512):
    m, k = a.shape
    n = w.shape[1]
    return pl.pallas_call(
        _mm_kernel,
        grid=(m // tm, n // tn),
        in_specs=[pl.BlockSpec((tm, k), lambda i, j: (i, 0)),
                  pl.BlockSpec((k, tn), lambda i, j: (0, j))],
        out_specs=pl.BlockSpec((tm, tn), lambda i, j: (i, j)),
        out_shape=jax.ShapeDtypeStruct((m, n), out_dtype),
        compiler_params=_cp(("arbitrary", "arbitrary")),
        name="matmul",
    )(a, w)


def _mm_res_kernel(a_ref, w_ref, x_ref, mod_ref, o_ref, *, slot):
    acc = jnp.dot(a_ref[...], w_ref[...], preferred_element_type=F32)
    o_ref[...] = x_ref[...] + mod_ref[0][slot:slot + 1] * acc


def matmul_gated_residual(a, w, x, mod, slot, tm=1024, tn=512):
    m, k = a.shape
    n = w.shape[1]
    return pl.pallas_call(
        functools.partial(_mm_res_kernel, slot=slot),
        grid=(m // tm, n // tn),
        in_specs=[pl.BlockSpec((tm, k), lambda i, j: (i, 0)),
                  pl.BlockSpec((k, tn), lambda i, j: (0, j)),
                  pl.BlockSpec((tm, tn), lambda i, j: (i, j)),
                  pl.BlockSpec((1, 6, tn), lambda i, j: (_mod_row(i * tm), 0, j))],
        out_specs=pl.BlockSpec((tm, tn), lambda i, j: (i, j)),
        out_shape=jax.ShapeDtypeStruct((m, n), F32),
        compiler_params=_cp(("arbitrary", "arbitrary")),
        name="matmul_gated_residual",
    )(a, w, x, mod)


def _dot_nt(a, b):
    return lax.dot_general(a, b, (((1,), (1,)), ((), ())), preferred_element_type=F32)


def _rms_head(x, gain):
    return x * lax.rsqrt(jnp.mean(x * x, axis=-1, keepdims=True) + EPS) * gain


def _rope(x, cosf, sinf):
    lane = lax.broadcasted_iota(I32, x.shape, 1)
    nxt = pltpu.roll(x, LANE - 1, 1)
    prv = pltpu.roll(x, 1, 1)
    return x * cosf + jnp.where((lane & 1) == 0, nxt, prv) * sinf


def _softmax_av(parts):
    m = functools.reduce(jnp.maximum, [jnp.max(s, axis=-1, keepdims=True) for s, _ in parts])
    l = None
    o = None
    for s, v in parts:
        p = jnp.exp(s - m)
        li = jnp.sum(p, axis=-1, keepdims=True)
        oi = jnp.dot(p.astype(BF16), v, preferred_element_type=F32)
        l = li if l is None else l + li
        o = oi if o is None else o + oi
    return o / l


def _na_kernel(q_ref, k_ref, v_ref, kc_ref, vc_ref, tab_ref, o_ref):
    kc = kc_ref[...]
    vc = vc_ref[...]
    n_blocks = ROWS // NA_QROWS

    def body(j, carry):
        ks = jnp.clip(j * NA_QROWS - NA_WIN_ROWS // 2, 0, ROWS - NA_SLAB)
        typ = jnp.where(j == 0, 0, jnp.where(j == n_blocks - 1, 2, 1))
        qs = pl.multiple_of(j * NA_QB, NA_QB)
        kst = pl.multiple_of(ks * GRID_W, GRID_W)
        q = q_ref[pl.ds(qs, NA_QB), :]
        k = k_ref[pl.ds(kst, NA_KB), :]
        v = v_ref[pl.ds(kst, NA_KB), :]
        s_win = _dot_nt(q, k) * ATT_SCALE + tab_ref[typ, 0]
        s_ctx = _dot_nt(q, kc) * ATT_SCALE
        o_ref[pl.ds(qs, NA_QB), :] = _softmax_av([(s_win, v), (s_ctx, vc)]).astype(BF16)
        return carry

    lax.fori_loop(0, n_blocks, body, 0)


def na_bias_table(rpb):
    def one(r0, ks):
        r = r0 + jnp.arange(NA_QROWS)
        kr = ks + jnp.arange(NA_SLAB)
        start = jnp.clip(r - NA_WIN_ROWS // 2, 0, ROWS - NA_WIN_ROWS)
        row_ok = (kr[None, :] >= start[:, None]) & (kr[None, :] < start[:, None] + NA_WIN_ROWS)
        row_idx = jnp.clip(kr[None, :] - r[:, None] + NA_WIN_ROWS - 1, 0, 2 * NA_WIN_ROWS - 2)
        cq = jnp.arange(GRID_W)
        col_start = jnp.clip(cq - NA_WIN_COLS // 2, 0, GRID_W - NA_WIN_COLS)
        col_ok = (cq[None, :] >= col_start[:, None]) & (cq[None, :] < col_start[:, None] + NA_WIN_COLS)
        col_idx = jnp.clip(cq[None, :] - cq[:, None] + NA_WIN_COLS - 1, 0, 2 * NA_WIN_COLS - 2)
        r_hot = jax.nn.one_hot(row_idx, 2 * NA_WIN_ROWS - 1, dtype=F32)
        c_hot = jax.nn.one_hot(col_idx, 2 * NA_WIN_COLS - 1, dtype=F32)
        bias = jnp.einsum('qka,hab,xyb->hqxky', r_hot, rpb.astype(F32), c_hot, precision=lax.Precision.HIGHEST)
        ok = row_ok[:, None, :, None] & col_ok[None, :, None, :]
        return jnp.where(ok[None], bias, NEG_INF).reshape(NA_HEADS, NA_QB, NA_KB)

    mid = 2 * NA_QROWS
    last = ROWS - NA_QROWS
    return jnp.stack([one(0, 0), one(mid, mid - NA_WIN_ROWS // 2), one(last, ROWS - NA_SLAB)])


def neighborhood_attention(p, table):
    cb = S // L
    return pl.pallas_call(
        _na_kernel,
        grid=(NA_HEADS, B),
        in_specs=[pl.BlockSpec((S, HD), lambda h, b: (b, h)),
                  pl.BlockSpec((S, HD), lambda h, b: (b, NA_HEADS + h)),
                  pl.BlockSpec((S, HD), lambda h, b: (b, 2 * NA_HEADS + h)),
                  pl.BlockSpec((L, HD), lambda h, b: (B * cb + b, NA_HEADS + h)),
                  pl.BlockSpec((L, HD), lambda h, b: (B * cb + b, 2 * NA_HEADS + h)),
                  pl.BlockSpec((3, 1, NA_QB, NA_KB), lambda h, b: (0, h, 0, 0))],
        out_specs=pl.BlockSpec((S, HD), lambda h, b: (b, h)),
        out_shape=jax.ShapeDtypeStruct((T_LAT, NA_HEADS * HD), BF16),
        compiler_params=_cp(("arbitrary", "arbitrary")),
        name="neighborhood_attention",
    )(p, p, p, p, p, table)


GQA_TQ = 512
GQA_QCOL = 3 * NA_HEADS
GQA_KCOL = GQA_QCOL + GQA_Q_HEADS
GQA_VCOL = GQA_KCOL + GQA_KV_HEADS


def _gqa_kernel(q_ref, k_ref, v_ref, kc_ref, vc_ref, cq_ref, sq_ref, ck_ref, sk_ref, qg_ref, kg_ref,
                o_ref, kn_ref, kcn_ref):
    @pl.when((pl.program_id(2) == 0) & (pl.program_id(3) == 0))
    def _():
        kn = _rope(_rms_head(k_ref[...].astype(F32), kg_ref[...]), ck_ref[...], sk_ref[...])
        kn_ref[...] = kn.astype(BF16)
        kcn_ref[...] = _rms_head(kc_ref[...].astype(F32), kg_ref[...]).astype(BF16)

    q = _rope(_rms_head(q_ref[...].astype(F32), qg_ref[...]), cq_ref[...], sq_ref[...]).astype(BF16)
    s_lat = _dot_nt(q, kn_ref[...]) * ATT_SCALE
    s_ctx = _dot_nt(q, kcn_ref[...]) * ATT_SCALE
    o_ref[...] = _softmax_av([(s_lat, v_ref[...]), (s_ctx, vc_ref[...])]).astype(BF16)


def gqa_attention(p, cosf, sinf, q_gain, k_gain):
    nq = S // GQA_TQ
    cb = S // L
    grid = (B, GQA_KV_HEADS, nq, GQA_GROUP)
    return pl.pallas_call(
        _gqa_kernel,
        grid=grid,
        in_specs=[pl.BlockSpec((GQA_TQ, HD), lambda b, n, i, g: (b * nq + i, GQA_QCOL + n * GQA_GROUP + g)),
                  pl.BlockSpec((S, HD), lambda b, n, i, g: (b, GQA_KCOL + n)),
                  pl.BlockSpec((S, HD), lambda b, n, i, g: (b, GQA_VCOL + n)),
                  pl.BlockSpec((L, HD), lambda b, n, i, g: (B * cb + b, GQA_KCOL + n)),
                  pl.BlockSpec((L, HD), lambda b, n, i, g: (B * cb + b, GQA_VCOL + n)),
                  pl.BlockSpec((GQA_TQ, HD), lambda b, n, i, g: (i, 0)),
                  pl.BlockSpec((GQA_TQ, HD), lambda b, n, i, g: (i, 0)),
                  pl.BlockSpec((S, HD), lambda b, n, i, g: (0, 0)),
                  pl.BlockSpec((S, HD), lambda b, n, i, g: (0, 0)),
                  pl.BlockSpec((1, HD), lambda b, n, i, g: (0, 0)),
                  pl.BlockSpec((1, HD), lambda b, n, i, g: (0, 0))],
        out_specs=pl.BlockSpec((GQA_TQ, HD), lambda b, n, i, g: (b * nq + i, n * GQA_GROUP + g)),
        out_shape=jax.ShapeDtypeStruct((T_LAT, GQA_Q_HEADS * HD), BF16),
        scratch_shapes=[pltpu.VMEM((S, HD), BF16), pltpu.VMEM((L, HD), BF16)],
        compiler_params=_cp(("arbitrary",) * 4, vmem=56 * 1024 * 1024),
        name="gqa_attention",
    )(p, p, p, p, p, cosf, sinf, cosf, sinf, q_gain.reshape(1, HD), k_gain.reshape(1, HD))


def _ctx_attn_kernel(p_ref, qg_ref, kg_ref, o_ref):
    def col(c):
        return p_ref[:, c * HD:(c + 1) * HD]

    for h in range(NA_HEADS):
        s = _dot_nt(col(h), col(NA_HEADS + h)) * ATT_SCALE
        o_ref[:, h * HD:(h + 1) * HD] = _softmax_av([(s, col(2 * NA_HEADS + h))]).astype(BF16)
    for n in range(GQA_KV_HEADS):
        kn = _rms_head(col(GQA_KCOL + n).astype(F32), kg_ref[...]).astype(BF16)
        v = col(GQA_VCOL + n)
        for g in range(GQA_GROUP):
            h = n * GQA_GROUP + g
            qn = _rms_head(col(GQA_QCOL + h).astype(F32), qg_ref[...]).astype(BF16)
            s = _dot_nt(qn, kn) * ATT_SCALE
            o_ref[:, (NA_HEADS + h) * HD:(NA_HEADS + h + 1) * HD] = _softmax_av([(s, v)]).astype(BF16)


def ctx_attention(p, q_gain, k_gain):
    cb = S // L
    return pl.pallas_call(
        _ctx_attn_kernel,
        grid=(B,),
        in_specs=[pl.BlockSpec((L, ATTN_IN), lambda b: (B * cb + b, 0)),
                  pl.BlockSpec((1, HD), lambda b: (0, 0)),
                  pl.BlockSpec((1, HD), lambda b: (0, 0))],
        out_specs=pl.BlockSpec((L, D), lambda b: (b, 0)),
        out_shape=jax.ShapeDtypeStruct((T_CTX, D), BF16),
        compiler_params=_cp(("arbitrary",)),
        name="ctx_attention",
    )(p, q_gain.reshape(1, HD), k_gain.reshape(1, HD))


def _log_sigmoid(x):
    return -(jnp.maximum(-x, 0.0) + jnp.log1p(jnp.exp(-jnp.abs(x))))


def _dot_hi(a, b):
    return jnp.dot(a, b, precision=lax.Precision.HIGHEST, preferred_element_type=F32)


def _gate_kernel(h_ref, wg_ref, wgt_ref, b_ref, bt_ref, lt_ref, ut_ref, col_ref, row_ref):
    nh = ML_HEADS
    hx = h_ref[...]
    g = jnp.dot(hx, wg_ref[...], preferred_element_type=F32) + b_ref[...]
    gt = _dot_nt(wgt_ref[...], hx) + bt_ref[...]
    li = g[:, 0:2 * nh]
    lf = _log_sigmoid(g[:, 2 * nh:4 * nh])
    lit = gt[0:2 * nh]
    lft = _log_sigmoid(gt[2 * nh:4 * nh])
    lt = lt_ref[...]
    ut = ut_ref[...]
    lane = lax.broadcasted_iota(I32, lf.shape, 1)
    bc = jnp.where(lane < nh, _dot_hi(lt, lf), _dot_hi(ut, lf))
    tot = jnp.sum(lf, axis=0, keepdims=True)
    aend = tot - bc + li
    col_ref[...] = jnp.concatenate([bc, aend, jnp.zeros((ML_CH, LANE - 4 * nh), F32)], axis=1)
    sub = lax.broadcasted_iota(I32, lft.shape, 0)
    bct = jnp.where(sub < nh, _dot_hi(lft, ut), _dot_hi(lft, lt))
    tott = jnp.sum(lft, axis=1, keepdims=True)
    gtr = lit - bct
    row_ref[0] = jnp.concatenate([bct, gtr, tott + gtr, jnp.broadcast_to(tott, bct.shape)], axis=0)


def mlstm_gates(hx, wg, gate_b):
    nh = ML_HEADS
    n_ch = T_ALL // ML_CH
    wg_pad = jnp.zeros((D, LANE), BF16).at[:, :4 * nh].set(wg.astype(BF16))
    b_pad = jnp.zeros((1, LANE), F32).at[0, :4 * nh].set(gate_b.reshape(-1))
    wgt = wg.astype(BF16).T
    bt = gate_b.reshape(4 * nh, 1).astype(F32)
    lt = jnp.tril(jnp.ones((ML_CH, ML_CH), F32))
    ut = jnp.triu(jnp.ones((ML_CH, ML_CH), F32))
    col, row = pl.pallas_call(
        _gate_kernel,
        grid=(n_ch,),
        in_specs=[pl.BlockSpec((ML_CH, D), lambda i: (i, 0)),
                  pl.BlockSpec((D, LANE), lambda i: (0, 0)),
                  pl.BlockSpec((4 * nh, D), lambda i: (0, 0)),
                  pl.BlockSpec((1, LANE), lambda i: (0, 0)),
                  pl.BlockSpec((4 * nh, 1), lambda i: (0, 0)),
                  pl.BlockSpec((ML_CH, ML_CH), lambda i: (0, 0)),
                  pl.BlockSpec((ML_CH, ML_CH), lambda i: (0, 0))],
        out_specs=[pl.BlockSpec((ML_CH, LANE), lambda i: (i, 0)),
                   pl.BlockSpec((1, 8 * nh, ML_CH), lambda i: (i, 0, 0))],
        out_shape=[jax.ShapeDtypeStruct((T_ALL, LANE), F32),
                   jax.ShapeDtypeStruct((n_ch, 8 * nh, ML_CH), F32)],
        compiler_params=_cp(("arbitrary",)),
        name="mlstm_gates",
    )(hx, wg_pad, wgt, b_pad, bt, lt, ut)
    col3 = col[:, :4 * nh].reshape(T_ALL, 2, 2 * nh).transpose(2, 0, 1)
    col3 = jnp.pad(col3, ((0, 0), (0, 0), (0, 6)))
    row4 = row.reshape(n_ch, 4, 2 * nh, ML_CH).transpose(2, 0, 1, 3)
    row4 = jnp.pad(row4, ((0, 0), (0, 0), (0, 4), (0, 0)))
    return col3, row4


def _mlstm_kernel(q_ref, k_ref, v_ref, col_ref, row_ref, o_ref, c_ref, n_ref, m_ref):
    d = pl.program_id(1)
    st = pl.program_id(3)

    @pl.when(st == 0)
    def _():
        c_ref[...] = jnp.zeros_like(c_ref)
        n_ref[...] = jnp.zeros_like(n_ref)
        m_ref[...] = jnp.zeros_like(m_ref)

    q = q_ref[...]
    kf = k_ref[...].astype(F32) * ML_KSCALE
    kb = kf.astype(BF16)
    v = v_ref[...]
    col = col_ref[0]
    row = row_ref[0, 0]
    bc_col = col[:, 0:1]
    aend_col = col[:, 1:2]
    bc_row = row[0:1]
    g_row = row[1:2]
    aend_row = row[2:3]
    btot = row[3:4, 0:1]
    m_st = m_ref[...]
    c_st = c_ref[...]
    n_st = n_ref[...]
    m_new = jnp.maximum(btot + m_st, jnp.max(aend_row, axis=1, keepdims=True))

    @pl.when(st > 0)
    def _():
        r = lax.broadcasted_iota(I32, (ML_CH, ML_CH), 0)
        c = lax.broadcasted_iota(I32, (ML_CH, ML_CH), 1)
        causal = (r - c) * (1 - 2 * d) >= 0
        d_mat = jnp.where(causal, bc_col + g_row, -jnp.inf)
        m_row = jnp.maximum(bc_col + m_st, jnp.max(d_mat, axis=1, keepdims=True))
        w_inter = jnp.exp(bc_col + m_st - m_row)
        s_mat = _dot_nt(q, kb) * jnp.exp(d_mat - m_row)
        num = (w_inter * jnp.dot(q, c_st.astype(BF16), preferred_element_type=F32)
               + jnp.dot(s_mat.astype(BF16), v, preferred_element_type=F32))
        den = (w_inter * jnp.sum(q.astype(F32) * n_st, axis=1, keepdims=True)
               + jnp.sum(s_mat, axis=1, keepdims=True))
        o_ref[0] = num / jnp.maximum(jnp.abs(den), jnp.exp(-m_row))

    w_end_col = jnp.exp(aend_col - m_new)
    w_end_row = jnp.exp(aend_row - m_new)
    decay = jnp.exp(btot + m_st - m_new)
    kw = (kf * w_end_col).astype(BF16)
    c_ref[...] = decay * c_st + lax.dot_general(kw, v, (((0,), (0,)), ((), ())), preferred_element_type=F32)
    w8 = jnp.broadcast_to(w_end_row, (8, ML_CH)).astype(BF16)
    n_ref[...] = decay * n_st + jnp.dot(w8, kb, preferred_element_type=F32)[0:1]
    m_ref[...] = m_new
    del bc_row


def mlstm_scan(p, col3, row4):
    n_lat = S // ML_CH
    steps = n_lat + 1
    lat_blocks = T_LAT // ML_CH

    def chunk(b, d, st):
        c = jnp.where(d == 0, st - 1, n_lat - st)
        return jnp.where(st == 0, lat_blocks + b, b * n_lat + c)

    def out_chunk(b, d, st):
        s1 = jnp.maximum(st, 1)
        return b * n_lat + jnp.where(d == 0, s1 - 1, n_lat - s1)

    vcol = (2 * ML_HEADS * ML_QK) // ML_V
    return pl.pallas_call(
        _mlstm_kernel,
        grid=(B, 2, ML_HEADS, steps),
        in_specs=[pl.BlockSpec((ML_CH, ML_QK), lambda b, d, h, s: (chunk(b, d, s), h)),
                  pl.BlockSpec((ML_CH, ML_QK), lambda b, d, h, s: (chunk(b, d, s), ML_HEADS + h)),
                  pl.BlockSpec((ML_CH, ML_V), lambda b, d, h, s: (chunk(b, d, s), vcol + h)),
                  pl.BlockSpec((1, ML_CH, 8), lambda b, d, h, s: (d * ML_HEADS + h, chunk(b, d, s), 0)),
                  pl.BlockSpec((1, 1, 8, ML_CH), lambda b, d, h, s: (d * ML_HEADS + h, chunk(b, d, s), 0, 0))],
        out_specs=pl.BlockSpec((1, ML_CH, ML_V), lambda b, d, h, s: (d, out_chunk(b, d, s), h)),
        out_shape=jax.ShapeDtypeStruct((2, T_LAT, ML_HEADS * ML_V), F32),
        scratch_shapes=[pltpu.VMEM((ML_QK, ML_V), F32), pltpu.VMEM((1, ML_QK), F32), pltpu.VMEM((1, 1), F32)],
        compiler_params=_cp(("arbitrary",) * 4),
        name="mlstm_scan",
    )(p, p, p, col3, row4)


def _readout_kernel(h_ref, o_ref, g_ref, a_ref):
    hs = h_ref[0] + h_ref[1]
    for h in range(ML_HEADS):
        sl = slice(h * ML_V, (h + 1) * ML_V)
        x = hs[:, sl]
        hn = x * lax.rsqrt(jnp.mean(x * x, axis=-1, keepdims=True) + EPS) * g_ref[:, sl]
        a_ref[:, sl] = (hn * jax.nn.sigmoid(o_ref[:, sl].astype(F32))).astype(BF16)


def mlstm_readout(hdir, p, head_gain):
    tm = 256
    ocol = (2 * ML_HEADS * ML_QK + ML_HEADS * ML_V) // D
    return pl.pallas_call(
        _readout_kernel,
        grid=(T_LAT // tm,),
        in_specs=[pl.BlockSpec((2, tm, D), lambda i: (0, i, 0)),
                  pl.BlockSpec((tm, D), lambda i: (i, ocol)),
                  pl.BlockSpec((1, D), lambda i: (0, 0))],
        out_specs=pl.BlockSpec((tm, D), lambda i: (i, 0)),
        out_shape=jax.ShapeDtypeStruct((T_LAT, D), BF16),
        compiler_params=_cp(("arbitrary",)),
        name="mlstm_readout",
    )(hdir, p, head_gain.reshape(1, D))


ROUTER_TM = 512


def _router_kernel(h_ref, w_ref, rb_ref, erow_ref, tri_ref, eidx_ref, wts_ref, pos_ref, cnt_ref, carry_ref):
    ng = N_GROUPS
    epg = N_EXPERTS // N_GROUPS
    tm = ROUTER_TM
    ninf = -jnp.inf

    @pl.when(pl.program_id(0) == 0)
    def _():
        carry_ref[...] = jnp.zeros_like(carry_ref)

    s = jax.nn.sigmoid(_dot_nt(w_ref[...], h_ref[...]))
    ssel = s + rb_ref[...]
    sraw = [s[ng * j:ng * (j + 1)] for j in range(epg)]
    slab = [ssel[ng * j:ng * (j + 1)] for j in range(epg)]
    m1 = functools.reduce(jnp.maximum, slab)
    jfirst = functools.reduce(jnp.minimum, [jnp.where(slab[j] == m1, j, epg) for j in range(epg)])
    m2 = functools.reduce(jnp.maximum, [jnp.where(jfirst == j, ninf, slab[j]) for j in range(epg)])
    gs = m1 + m2
    giota = lax.broadcasted_iota(I32, (ng, tm), 0)
    gsel = jnp.zeros((ng, tm), F32)
    for _ in range(TOPK_GROUPS):
        mx = jnp.max(gs, axis=0, keepdims=True)
        gi = jnp.min(jnp.where(gs == mx, giota, ng), axis=0, keepdims=True)
        hit = giota == gi
        gsel = jnp.where(hit, 1.0, gsel)
        gs = jnp.where(hit, ninf, gs)
    msl = [jnp.where(gsel > 0.0, slab[j], ninf) for j in range(epg)]
    eid = [giota * epg + j for j in range(epg)]
    selm = [jnp.zeros((ng, tm), F32) for _ in range(epg)]
    e_list, w_list = [], []
    for _ in range(TOP_K):
        mx = jnp.max(functools.reduce(jnp.maximum, msl), axis=0, keepdims=True)
        cand = functools.reduce(jnp.minimum, [jnp.where(msl[j] == mx, eid[j], N_EXPERTS) for j in range(epg)])
        esel = jnp.min(cand, axis=0, keepdims=True)
        hits = [eid[j] == esel for j in range(epg)]
        wk = functools.reduce(lambda a, b: a + b, [jnp.where(hits[j], sraw[j], 0.0) for j in range(epg)])
        w_list.append(jnp.sum(wk, axis=0, keepdims=True))
        e_list.append(esel)
        msl = [jnp.where(hits[j], ninf, msl[j]) for j in range(epg)]
        selm = [jnp.where(hits[j], 1.0, selm[j]) for j in range(epg)]
    wsum = functools.reduce(lambda a, b: a + b, w_list)
    wts_ref[...] = jnp.concatenate([w / wsum * ROUTED_SCALE for w in w_list], axis=0)
    eidx_ref[...] = jnp.concatenate(e_list, axis=0)
    sel = jnp.concatenate(selm, axis=0)
    carry = carry_ref[...]
    posfull = jnp.dot(sel.astype(BF16), tri_ref[...], preferred_element_type=F32) + carry
    erow = erow_ref[...]
    pos = [jnp.sum(jnp.where(erow == e, posfull, 0.0), axis=0, keepdims=True) for e in e_list]
    pos_ref[...] = jnp.concatenate(pos, axis=0).astype(I32)
    carry = carry + jnp.sum(sel, axis=1, keepdims=True)
    carry_ref[...] = carry
    cnt_ref[...] = carry


def moe_router(hx, router_w, router_b, n_tok):
    tm = ROUTER_TM
    epg = N_EXPERTS // N_GROUPS
    perm = (jnp.arange(N_EXPERTS) % N_GROUPS) * epg + jnp.arange(N_EXPERTS) // N_GROUPS
    w_t = router_w.astype(BF16).T[perm]
    rb = router_b.astype(F32)[perm].reshape(N_EXPERTS, 1)
    erow = perm.astype(I32).reshape(N_EXPERTS, 1)
    tri = jnp.triu(jnp.ones((tm, tm), BF16), 1)
    eidx, wts, pos, counts = pl.pallas_call(
        _router_kernel,
        grid=(n_tok // tm,),
        in_specs=[pl.BlockSpec((tm, D), lambda i: (i, 0)),
                  pl.BlockSpec((N_EXPERTS, D), lambda i: (0, 0)),
                  pl.BlockSpec((N_EXPERTS, 1), lambda i: (0, 0)),
                  pl.BlockSpec((N_EXPERTS, 1), lambda i: (0, 0)),
                  pl.BlockSpec((tm, tm), lambda i: (0, 0))],
        out_specs=[pl.BlockSpec((TOP_K, tm), lambda i: (0, i)),
                   pl.BlockSpec((TOP_K, tm), lambda i: (0, i)),
                   pl.BlockSpec((TOP_K, tm), lambda i: (0, i)),
                   pl.BlockSpec((N_EXPERTS, 1), lambda i: (0, 0))],
        out_shape=[jax.ShapeDtypeStruct((TOP_K, n_tok), I32),
                   jax.ShapeDtypeStruct((TOP_K, n_tok), F32),
                   jax.ShapeDtypeStruct((TOP_K, n_tok), I32),
                   jax.ShapeDtypeStruct((N_EXPERTS, 1), F32)],
        scratch_shapes=[pltpu.VMEM((N_EXPERTS, 1), F32)],
        compiler_params=_cp(("arbitrary",)),
        name="moe_router",
    )(hx, w_t, rb, erow, tri)
    return eidx, wts, pos, counts.reshape(N_EXPERTS)[perm]


DISPATCH_TM = 512


def _dispatch_kernel(slot_ref, hx_ref, xs_hbm, sem):
    def issue(t, carry):
        src = hx_ref.at[pl.ds(pl.multiple_of(t * PK_S, PK_S), PK_S), :]
        for k in range(TOP_K):
            dst = xs_hbm.at[pl.ds(pl.multiple_of(slot_ref[k, t] * PK_S, PK_S), PK_S), :]
            pltpu.make_async_copy(src, dst, sem).start()
        return carry

    lax.fori_loop(0, DISPATCH_TM, issue, 0)

    def drain(t, carry):
        for k in range(TOP_K):
            pltpu.make_async_copy(hx_ref.at[pl.ds(0, PK_S), :], xs_hbm.at[pl.ds(0, PK_S), :], sem).wait()
        return carry

    lax.fori_loop(0, DISPATCH_TM, drain, 0)


def moe_dispatch(slots, hx_packed, n_tok, n_rows):
    tm = DISPATCH_TM
    return pl.pallas_call(
        _dispatch_kernel,
        grid=(n_tok // tm,),
        in_specs=[pl.BlockSpec((TOP_K, tm), lambda i: (0, i), memory_space=pltpu.SMEM),
                  pl.BlockSpec((tm * PK_S, LANE), lambda i: (i, 0))],
        out_specs=pl.BlockSpec(memory_space=pl.ANY),
        out_shape=jax.ShapeDtypeStruct((n_rows * PK_S, LANE), U32),
        scratch_shapes=[pltpu.SemaphoreType.DMA(())],
        compiler_params=_cp(("arbitrary",)),
        name="moe_dispatch",
    )(slots, hx_packed)


def _expert_kernel(be_ref, valid_ref, nused_ref, xs_ref, w1_ref, w3_ref, w2_ref, y_ref, w1b, w3b, w2b, xb):
    i = pl.program_id(0)
    bm = EXP_BM

    @pl.when(i < nused_ref[0])
    def _():
        prev = be_ref[jnp.maximum(i - 1, 0)]

        @pl.when((i == 0) | (be_ref[i] != prev))
        def _():
            w1b[...] = w1_ref[0, 0].astype(BF16)
            w3b[...] = w3_ref[0, 0].astype(BF16)
            w2b[...] = w2_ref[0, 0].astype(BF16)

        rows = lax.broadcasted_iota(I32, (bm, LANE), 0)
        live = rows < valid_ref[i]
        for s in range(PK_S):
            u = jnp.where(live, xs_ref[pl.ds(s, bm, stride=PK_S), :], jnp.uint32(0))
            xb[:, s * LANE:(s + 1) * LANE] = pltpu.bitcast(u << 16, F32).astype(BF16)
            xb[:, PK_W + s * LANE:PK_W + (s + 1) * LANE] = (
                pltpu.bitcast(u & jnp.uint32(0xFFFF0000), F32).astype(BF16))
        x = xb[...]
        h1 = jnp.dot(x, w1b[...], preferred_element_type=F32)
        h3 = jnp.dot(x, w3b[...], preferred_element_type=F32)
        a = (h1 * jax.nn.sigmoid(h1) * h3).astype(BF16)
        y = jnp.dot(a, w2b[...], preferred_element_type=F32)
        for s in range(Y_S):
            y_ref[pl.ds(s, bm, stride=Y_S), :] = y[:, s * LANE:(s + 1) * LANE]


def moe_experts(block_e, valid, n_used, xs, w1, w3, w2, layer, n_blocks):
    bm = EXP_BM

    def blk(i, be, va, nu):
        return jnp.minimum(i, nu[0] - 1)

    grid_spec = pltpu.PrefetchScalarGridSpec(
        num_scalar_prefetch=3,
        grid=(n_blocks,),
        in_specs=[pl.BlockSpec((bm * PK_S, LANE), lambda i, be, va, nu: (blk(i, be, va, nu), 0)),
                  pl.BlockSpec((1, 1, D, EXPERT_DIM), lambda i, be, va, nu: (layer, be[blk(i, be, va, nu)], 0, 0)),
                  pl.BlockSpec((1, 1, D, EXPERT_DIM), lambda i, be, va, nu: (layer, be[blk(i, be, va, nu)], 0, 0)),
                  pl.BlockSpec((1, 1, EXPERT_DIM, D), lambda i, be, va, nu: (layer, be[blk(i, be, va, nu)], 0, 0))],
        out_specs=pl.BlockSpec((bm * Y_S, LANE), lambda i, be, va, nu: (blk(i, be, va, nu), 0)),
        scratch_shapes=[pltpu.VMEM((D, EXPERT_DIM), BF16), pltpu.VMEM((D, EXPERT_DIM), BF16),
                        pltpu.VMEM((EXPERT_DIM, D), BF16), pltpu.VMEM((bm, D), BF16)],
    )
    return pl.pallas_call(
        _expert_kernel,
        grid_spec=grid_spec,
        out_shape=jax.ShapeDtypeStruct((n_blocks * bm * Y_S, LANE), F32),
        compiler_params=_cp(("arbitrary",), vmem=56 * 1024 * 1024),
        name="moe_experts",
    )(block_e, valid, n_used, xs, w1, w3, w2)


def _shared_kernel(x_ref, w1_ref, w3_ref, w2_ref, o_ref):
    x = x_ref[...]
    h1 = jnp.dot(x, w1_ref[...], preferred_element_type=F32)
    h3 = jnp.dot(x, w3_ref[...], preferred_element_type=F32)
    a = (h1 * jax.nn.sigmoid(h1) * h3).astype(BF16)
    o_ref[...] = jnp.dot(a, w2_ref[...], preferred_element_type=F32)


def shared_expert(hx, w1, w3, w2, n_tok):
    tm = 512
    return pl.pallas_call(
        _shared_kernel,
        grid=(n_tok // tm,),
        in_specs=[pl.BlockSpec((tm, D), lambda i: (i, 0)),
                  pl.BlockSpec((D, EXPERT_DIM), lambda i: (0, 0)),
                  pl.BlockSpec((D, EXPERT_DIM), lambda i: (0, 0)),
                  pl.BlockSpec((EXPERT_DIM, D), lambda i: (0, 0))],
        out_specs=pl.BlockSpec((tm, D), lambda i: (i, 0)),
        out_shape=jax.ShapeDtypeStruct((n_tok, D), F32),
        compiler_params=_cp(("arbitrary",)),
        name="shared_expert",
    )(hx, w1, w3, w2)


COMBINE_TM = 128


def _combine_kernel(slot_ref, w_ref, sh_ref, x_ref, mod_ref, y_hbm, o_ref, buf, sem):
    tm = COMBINE_TM

    def issue(t, carry):
        for k in range(TOP_K):
            src = y_hbm.at[pl.ds(pl.multiple_of(slot_ref[k, t] * Y_S, Y_S), Y_S), :]
            dst = buf.at[pl.ds(pl.multiple_of((k * tm + t) * Y_S, Y_S), Y_S), :]
            pltpu.make_async_copy(src, dst, sem).start()
        return carry

    lax.fori_loop(0, tm, issue, 0)

    def drain(t, carry):
        for k in range(TOP_K):
            pltpu.make_async_copy(y_hbm.at[pl.ds(0, Y_S), :], buf.at[pl.ds(0, Y_S), :], sem).wait()
        return carry

    lax.fori_loop(0, tm, drain, 0)

    w = w_ref[...]
    gate = mod_ref[0][5:6]
    for s in range(Y_S):
        sl = slice(s * LANE, (s + 1) * LANE)
        acc = None
        for k in range(TOP_K):
            term = w[:, k:k + 1] * buf[pl.ds(k * tm * Y_S + s, tm, stride=Y_S), :]
            acc = term if acc is None else acc + term
        o_ref[:, sl] = x_ref[:, sl] + gate[:, sl] * (sh_ref[:, sl] + acc)


def moe_combine(slots, wts_tok, shared, x, mod, y, n_tok):
    tm = COMBINE_TM
    return pl.pallas_call(
        _combine_kernel,
        grid=(n_tok // tm,),
        in_specs=[pl.BlockSpec((TOP_K, tm), lambda i: (0, i), memory_space=pltpu.SMEM),
                  pl.BlockSpec((tm, TOP_K), lambda i: (i, 0)),
                  pl.BlockSpec((tm, D), lambda i: (i, 0)),
                  pl.BlockSpec((tm, D), lambda i: (i, 0)),
                  pl.BlockSpec((1, 6, D), lambda i: (_mod_row(i * tm), 0, 0)),
                  pl.BlockSpec(memory_space=pl.ANY)],
        out_specs=pl.BlockSpec((tm, D), lambda i: (i, 0)),
        out_shape=jax.ShapeDtypeStruct((n_tok, D), F32),
        scratch_shapes=[pltpu.VMEM((TOP_K * tm * Y_S, LANE), F32), pltpu.SemaphoreType.DMA(())],
        compiler_params=_cp(("arbitrary",)),
        name="moe_combine",
    )(slots, wts_tok, shared, x, mod, y)


def _lookup(table, idx):
    e = jnp.arange(table.shape[0], dtype=I32).reshape((-1,) + (1,) * idx.ndim)
    return jnp.sum(jnp.where(idx[None] == e, table.reshape(e.shape), 0), axis=0)


def moe_layer(x, mod, norm_gain, router_w, router_b, exp_w1, exp_w3, exp_w2, sw1, sw3, sw2, layer, n_tok):
    bm = EXP_BM
    n_blocks = -(-n_tok * TOP_K // bm) + N_EXPERTS
    hx, hx_packed = norm_mod(x, norm_gain, mod, 3, n_tok, pack=True)
    eidx, wts, pos, counts = moe_router(hx, router_w, router_b, n_tok)
    shared = shared_expert(hx, sw1.astype(BF16), sw3.astype(BF16), sw2.astype(BF16), n_tok)
    cnt = counts.astype(I32)
    padded = (cnt + bm - 1) // bm * bm
    pad_end = jnp.cumsum(padded)
    pad_start = pad_end - padded
    slots = _lookup(pad_start, eidx) + pos
    blk_row = jnp.arange(n_blocks, dtype=I32) * bm
    block_e = jnp.minimum(jnp.sum((pad_end[:, None] <= blk_row[None, :]).astype(I32), axis=0), N_EXPERTS - 1)
    valid = jnp.clip(_lookup(cnt, block_e) - (blk_row - _lookup(pad_start, block_e)), 0, bm).astype(I32)
    n_used = (pad_end[-1:] // bm).astype(I32)
    xs = moe_dispatch(slots, hx_packed, n_tok, n_blocks * bm)
    y = moe_experts(block_e, valid, n_used, xs, exp_w1, exp_w3, exp_w2, layer, n_blocks)
    return moe_combine(slots, wts.T, shared, x, mod, y, n_tok)


def _rope_tables():
    t = jnp.arange(S, dtype=I32)
    row = (t // GRID_W).astype(F32)
    col = (t % GRID_W).astype(F32)
    n_freq = HD // 4
    inv_freq = ROPE_THETA ** (-jnp.arange(n_freq, dtype=F32) / n_freq)
    ang = jnp.concatenate([row[:, None] * inv_freq, col[:, None] * inv_freq], axis=-1)
    cosf = jnp.repeat(jnp.cos(ang), 2, axis=-1)
    sinf = jnp.stack([-jnp.sin(ang), jnp.sin(ang)], axis=-1).reshape(S, HD)
    return cosf, sinf


def kernel(x, c, ctx, c_ctx, ada_w, ada_b, norm_mix, norm_ffn, attn_w_in, attn_w_out, attn_rpb, attn_q_gain,
           attn_k_gain, ml_w_in, ml_w_out, ml_gate_b, ml_head_gain, router_w, router_b, exp_w1, exp_w3, exp_w2,
           sh_w1, sh_w3, sh_w2, final_norm_gain):
    depth = ada_w.shape[0]
    xa = jnp.concatenate([x.reshape(T_LAT, D), ctx.reshape(T_CTX, D)], axis=0)
    cvec = jnp.concatenate([c, c_ctx[None], jnp.zeros((8 - B - 1, D), F32)], axis=0)
    mod_all = ada_ln(cvec, ada_w, ada_b).reshape(depth, 8, 6, D)
    cosf, sinf = _rope_tables()

    mod = mod_all[0]
    hx = norm_mod(xa, norm_mix[0], mod, 0, T_ALL, pack=False)
    p = matmul(hx, attn_w_in[0].astype(BF16), BF16)
    o_na = neighborhood_attention(p, na_bias_table(attn_rpb[0]))
    o_gqa = gqa_attention(p, cosf, sinf, attn_q_gain[0], attn_k_gain[0])
    o_ctx = ctx_attention(p, attn_q_gain[0], attn_k_gain[0])
    o_all = jnp.concatenate([jnp.concatenate([o_na, o_gqa], axis=1), o_ctx], axis=0)
    xa = matmul_gated_residual(o_all, attn_w_out[0].astype(BF16), xa, mod, 2)
    xa = moe_layer(xa, mod, norm_ffn[0], router_w[0], router_b[0], exp_w1, exp_w3, exp_w2,
                   sh_w1[0], sh_w3[0], sh_w2[0], 0, T_ALL)

    mod = mod_all[1]
    hx = norm_mod(xa, norm_mix[1], mod, 0, T_ALL, pack=False)
    w_in = ml_w_in[0]
    p = matmul(hx, w_in[:, :ML_MAIN].astype(BF16), BF16)
    col3, row4 = mlstm_gates(hx, w_in[:, ML_MAIN:], ml_gate_b[0])
    hdir = mlstm_scan(p, col3, row4)
    a = mlstm_readout(hdir, p, ml_head_gain[0])
    xl = matmul_gated_residual(a, ml_w_out[0].astype(BF16), xa, mod, 2)
    xl = moe_layer(xl, mod, norm_ffn[1], router_w[1], router_b[1], exp_w1, exp_w3, exp_w2,
                   sh_w1[1], sh_w3[1], sh_w2[1], 1, T_LAT)
    return final_norm(xl, final_norm_gain).reshape(B, S, D)
```

```python
import functools

import jax
import jax.numpy as jnp
from jax import lax
from jax.experimental import pallas as pl
from jax.experimental.pallas import tpu as pltpu

F32 = jnp.float32
BF16 = jnp.bfloat16
I32 = jnp.int32
U32 = jnp.uint32

D = 2048
B = 4
S = 4096
L = 256
T_LAT = B * S
T_CTX = B * L
T_ALL = T_LAT + T_CTX
GRID_W = 64
ROWS = S // GRID_W
HD = 128
NA_HEADS = 8
NA_WIN_ROWS = 8
NA_WIN_COLS = 16
GQA_Q_HEADS = 8
GQA_KV_HEADS = 2
GQA_GROUP = 4
ROPE_THETA = 10000.0
ATTN_IN = 4608
ML_HEADS = 8
ML_V = 256
ML_QK = 128
ML_MAIN = 6144
N_EXPERTS = 64
TOP_K = 8
N_GROUPS = 8
TOPK_GROUPS = 4
EXPERT_DIM = 512
ROUTED_SCALE = 2.5
EPS = 1e-6
NEG_INF = -1e30
ATT_SCALE = HD ** -0.5
ML_KSCALE = ML_QK ** -0.5

LANE = 128
NA_QROWS = 4
NA_SLAB = NA_QROWS + NA_WIN_ROWS - 1
NA_QB = NA_QROWS * GRID_W
NA_KB = NA_SLAB * GRID_W
ML_CH = 256
EXP_BM = 512
PK_W = D // 2
PK_S = PK_W // LANE
Y_S = D // LANE
VMEM_LIMIT = 48 * 1024 * 1024


def _cp(sem, vmem=VMEM_LIMIT):
    return pltpu.CompilerParams(dimension_semantics=sem, vmem_limit_bytes=vmem)


def _mod_row(start_row):
    return jnp.where(start_row < T_LAT, start_row // S, B)


def _ada_kernel(c_ref, w_ref, b_ref, o_ref):
    c = c_ref[...]
    a = (c * jax.nn.sigmoid(c)).astype(BF16)
    w = w_ref[0].astype(BF16)
    o_ref[0] = jnp.dot(a, w, preferred_element_type=F32) + b_ref[0]


def ada_ln(cvec, ada_w, ada_b):
    depth = ada_w.shape[0]
    n = ada_w.shape[2]
    tn = 1024
    return pl.pallas_call(
        _ada_kernel,
        grid=(depth, n // tn),
        in_specs=[pl.BlockSpec((8, D), lambda l, j: (0, 0)),
                  pl.BlockSpec((1, D, tn), lambda l, j: (l, 0, j)),
                  pl.BlockSpec((1, 1, tn), lambda l, j: (l, 0, j))],
        out_specs=pl.BlockSpec((1, 8, tn), lambda l, j: (l, 0, j)),
        out_shape=jax.ShapeDtypeStruct((depth, 8, n), F32),
        compiler_params=_cp(("arbitrary", "arbitrary")),
        name="ada_ln",
    )(cvec, ada_w, ada_b.reshape(depth, 1, n))


def _norm_mod_kernel(x_ref, g_ref, mod_ref, *out_refs, base, pack, tm):
    x = x_ref[...]
    y = x * lax.rsqrt(jnp.mean(x * x, axis=-1, keepdims=True) + EPS) * g_ref[...]
    m = mod_ref[0]
    h = y * (1.0 + m[base + 1:base + 2]) + m[base:base + 1]
    hb = h.astype(BF16)
    out_refs[0][...] = hb
    if pack:
        u = pltpu.bitcast(hb.astype(F32), U32)
        out_refs[1][...] = (u[:, PK_W:] & jnp.uint32(0xFFFF0000)) | (u[:, :PK_W] >> 16)


def norm_mod(x, gain, mod, base, n_rows, pack):
    tm = 256
    out_shape = [jax.ShapeDtypeStruct((n_rows, D), BF16)]
    out_specs = [pl.BlockSpec((tm, D), lambda i: (i, 0))]
    if pack:
        out_shape.append(jax.ShapeDtypeStruct((n_rows, PK_W), U32))
        out_specs.append(pl.BlockSpec((tm, PK_W), lambda i: (i, 0)))
    res = pl.pallas_call(
        functools.partial(_norm_mod_kernel, base=base, pack=pack, tm=tm),
        grid=(n_rows // tm,),
        in_specs=[pl.BlockSpec((tm, D), lambda i: (i, 0)),
                  pl.BlockSpec((1, D), lambda i: (0, 0)),
                  pl.BlockSpec((1, 6, D), lambda i: (_mod_row(i * tm), 0, 0))],
        out_specs=out_specs,
        out_shape=out_shape,
        compiler_params=_cp(("arbitrary",)),
        name="norm_mod",
    )(x, gain.reshape(1, D), mod)
    return res if pack else res[0]


def _final_norm_kernel(x_ref, g_ref, o_ref):
    x = x_ref[...]
    o_ref[...] = x * lax.rsqrt(jnp.mean(x * x, axis=-1, keepdims=True) + EPS) * g_ref[...]


def final_norm(x, gain):
    tm = 256
    return pl.pallas_call(
        _final_norm_kernel,
        grid=(T_LAT // tm,),
        in_specs=[pl.BlockSpec((tm, D), lambda i: (i, 0)),
                  pl.BlockSpec((1, D), lambda i: (0, 0))],
        out_specs=pl.BlockSpec((tm, D), lambda i: (i, 0)),
        out_shape=jax.ShapeDtypeStruct((T_LAT, D), F32),
        compiler_params=_cp(("arbitrary",)),
        name="final_norm",
    )(x, gain.reshape(1, D))


def _mm_kernel(a_ref, w_ref, o_ref):
    o_ref[...] = jnp.dot(a_ref[...], w_ref[...], preferred_element_type=F32).astype(o_ref.dtype)


def matmul(a, w, out_dtype, tm=1024, tn=512):
    m, k = a.shape
    n = w.shape[1]
    return pl.pallas_call(
        _mm_kernel,
        grid=(m // tm, n // tn),
        in_specs=[pl.BlockSpec((tm, k), lambda i, j: (i, 0)),
                  pl.BlockSpec((k, tn), lambda i, j: (0, j))],
        out_specs=pl.BlockSpec((tm, tn), lambda i, j: (i, j)),
        out_shape=jax.ShapeDtypeStruct((m, n), out_dtype),
        compiler_params=_cp(("arbitrary", "arbitrary")),
        name="matmul",
    )(a, w)


def _mm_res_kernel(a_ref, w_ref, x_ref, mod_ref, o_ref, *, slot):
    acc = jnp.dot(a_ref[...], w_ref[...], preferred_element_type=F32)
    o_ref[...] = x_ref[...] + mod_ref[0][slot:slot + 1] * acc


def matmul_gated_residual(a, w, x, mod, slot, tm=1024, tn=512):
    m, k = a.shape
    n = w.shape[1]
    return pl.pallas_call(
        functools.partial(_mm_res_kernel, slot=slot),
        grid=(m // tm, n // tn),
        in_specs=[pl.BlockSpec((tm, k), lambda i, j: (i, 0)),
                  pl.BlockSpec((k, tn), lambda i, j: (0, j)),
                  pl.BlockSpec((tm, tn), lambda i, j: (i, j)),
                  pl.BlockSpec((1, 6, tn), lambda i, j: (_mod_row(i * tm), 0, j))],
        out_specs=pl.BlockSpec((tm, tn), lambda i, j: (i, j)),
        out_shape=jax.ShapeDtypeStruct((m, n), F32),
        compiler_params=_cp(("arbitrary", "arbitrary")),
        name="matmul_gated_residual",
    )(a, w, x, mod)


def _dot_nt(a, b):
    return lax.dot_general(a, b, (((1,), (1,)), ((), ())), preferred_element_type=F32)


def _rms_head(x, gain):
    return x * lax.rsqrt(jnp.mean(x * x, axis=-1, keepdims=True) + EPS) * gain


def _rope(x, cosf, sinf):
    lane = lax.broadcasted_iota(I32, x.shape, 1)
    nxt = pltpu.roll(x, LANE - 1, 1)
    prv = pltpu.roll(x, 1, 1)
    return x * cosf + jnp.where((lane & 1) == 0, nxt, prv) * sinf


def _softmax_av(parts):
    m = functools.reduce(jnp.maximum, [jnp.max(s, axis=-1, keepdims=True) for s, _ in parts])
    l = None
    o = None
    for s, v in parts:
        p = jnp.exp(s - m)
        li = jnp.sum(p, axis=-1, keepdims=True)
        oi = jnp.dot(p.astype(BF16), v, preferred_element_type=F32)
        l = li if l is None else l + li
        o = oi if o is None else o + oi
    return o / l


def _na_kernel(q_ref, k_ref, v_ref, kc_ref, vc_ref, tab_ref, o_ref):
    kc = kc_ref[...]
    vc = vc_ref[...]
    n_blocks = ROWS // NA_QROWS

    def body(j, carry):
        ks = jnp.clip(j * NA_QROWS - NA_WIN_ROWS // 2, 0, ROWS - NA_SLAB)
        typ = jnp.where(j == 0, 0, jnp.where(j == n_blocks - 1, 2, 1))
        qs = pl.multiple_of(j * NA_QB, NA_QB)
        kst = pl.multiple_of(ks * GRID_W, GRID_W)
        q = q_ref[pl.ds(qs, NA_QB), :]
        k = k_ref[pl.ds(kst, NA_KB), :]
        v = v_ref[pl.ds(kst, NA_KB), :]
        s_win = _dot_nt(q, k) * ATT_SCALE + tab_ref[typ, 0]
        s_ctx = _dot_nt(q, kc) * ATT_SCALE
        o_ref[pl.ds(qs, NA_QB), :] = _softmax_av([(s_win, v), (s_ctx, vc)]).astype(BF16)
        return carry

    lax.fori_loop(0, n_blocks, body, 0)


def na_bias_table(rpb):
    def one(r0, ks):
        r = r0 + jnp.arange(NA_QROWS)
        kr = ks + jnp.arange(NA_SLAB)
        start = jnp.clip(r - NA_WIN_ROWS // 2, 0, ROWS - NA_WIN_ROWS)
        row_ok = (kr[None, :] >= start[:, None]) & (kr[None, :] < start[:, None] + NA_WIN_ROWS)
        row_idx = jnp.clip(kr[None, :] - r[:, None] + NA_WIN_ROWS - 1, 0, 2 * NA_WIN_ROWS - 2)
        cq = jnp.arange(GRID_W)
        col_start = jnp.clip(cq - NA_WIN_COLS // 2, 0, GRID_W - NA_WIN_COLS)
        col_ok = (cq[None, :] >= col_start[:, None]) & (cq[None, :] < col_start[:, None] + NA_WIN_COLS)
        col_idx = jnp.clip(cq[None, :] - cq[:, None] + NA_WIN_COLS - 1, 0, 2 * NA_WIN_COLS - 2)
        r_hot = jax.nn.one_hot(row_idx, 2 * NA_WIN_ROWS - 1, dtype=F32)
        c_hot = jax.nn.one_hot(col_idx, 2 * NA_WIN_COLS - 1, dtype=F32)
        bias = jnp.einsum('qka,hab,xyb->hqxky', r_hot, rpb.astype(F32), c_hot, precision=lax.Precision.HIGHEST)
        ok = row_ok[:, None, :, None] & col_ok[None, :, None, :]
        return jnp.where(ok[None], bias, NEG_INF).reshape(NA_HEADS, NA_QB, NA_KB)

    mid = 2 * NA_QROWS
    last = ROWS - NA_QROWS
    return jnp.stack([one(0, 0), one(mid, mid - NA_WIN_ROWS // 2), one(last, ROWS - NA_SLAB)])


def neighborhood_attention(p, table):
    cb = S // L
    return pl.pallas_call(
        _na_kernel,
        grid=(NA_HEADS, B),
        in_specs=[pl.BlockSpec((S, HD), lambda h, b: (b, h)),
                  pl.BlockSpec((S, HD), lambda h, b: (b, NA_HEADS + h)),
                  pl.BlockSpec((S, HD), lambda h, b: (b, 2 * NA_HEADS + h)),
                  pl.BlockSpec((L, HD), lambda h, b: (B * cb + b, NA_HEADS + h)),
                  pl.BlockSpec((L, HD), lambda h, b: (B * cb + b, 2 * NA_HEADS + h)),
                  pl.BlockSpec((3, 1, NA_QB, NA_KB), lambda h, b: (0, h, 0, 0))],
        out_specs=pl.BlockSpec((S, HD), lambda h, b: (b, h)),
        out_shape=jax.ShapeDtypeStruct((T_LAT, NA_HEADS * HD), BF16),
        compiler_params=_cp(("arbitrary", "arbitrary")),
        name="neighborhood_attention",
    )(p, p, p, p, p, table)


GQA_TQ = 256
GQA_CK = 512
GQA_QCOL = 3 * NA_HEADS
GQA_KCOL = GQA_QCOL + GQA_Q_HEADS
GQA_VCOL = GQA_KCOL + GQA_KV_HEADS


def _gqa_kernel(q_ref, k_ref, v_ref, kc_ref, vc_ref, cq_ref, sq_ref, ck_ref, sk_ref, qg_ref, kg_ref,
                o_ref, kn_ref, kcn_ref):
    @pl.when(pl.program_id(2) == 0)
    def _():
        kn = _rope(_rms_head(k_ref[...].astype(F32), kg_ref[...]), ck_ref[...], sk_ref[...])
        kn_ref[...] = kn.astype(BF16)
        kcn_ref[...] = _rms_head(kc_ref[...].astype(F32), kg_ref[...]).astype(BF16)

    cos = cq_ref[...]
    sin = sq_ref[...]
    heads = []
    for g in range(GQA_GROUP):
        qh = _rope(_rms_head(q_ref[:, g * HD:(g + 1) * HD].astype(F32), qg_ref[...]), cos, sin)
        heads.append((qh * ATT_SCALE).astype(BF16))
    q = jnp.concatenate(heads, axis=0)
    chunks = [(kn_ref[c * GQA_CK:(c + 1) * GQA_CK, :], v_ref[c * GQA_CK:(c + 1) * GQA_CK, :])
              for c in range(S // GQA_CK)]
    chunks.append((kcn_ref[...], vc_ref[...]))
    m = l = acc = None
    for kk, vv in chunks:
        s = _dot_nt(q, kk)
        mc = jnp.max(s, axis=-1, keepdims=True)
        if m is None:
            m_new = mc
            p = jnp.exp(s - m_new)
            l = jnp.sum(p, axis=-1, keepdims=True)
            acc = jnp.dot(p.astype(BF16), vv, preferred_element_type=F32)
        else:
            m_new = jnp.maximum(m, mc)
            alpha = jnp.exp(m - m_new)
            p = jnp.exp(s - m_new)
            l = alpha * l + jnp.sum(p, axis=-1, keepdims=True)
            acc = alpha * acc + jnp.dot(p.astype(BF16), vv, preferred_element_type=F32)
        m = m_new
    o = acc / l
    for g in range(GQA_GROUP):
        o_ref[:, g * HD:(g + 1) * HD] = o[g * GQA_TQ:(g + 1) * GQA_TQ].astype(BF16)


def gqa_attention(p, cosf, sinf, q_gain, k_gain):
    nq = S // GQA_TQ
    cb = S // L
    gw = GQA_GROUP * HD
    return pl.pallas_call(
        _gqa_kernel,
        grid=(B, GQA_KV_HEADS, nq),
        in_specs=[pl.BlockSpec((GQA_TQ, gw), lambda b, n, i: (b * nq + i, GQA_QCOL // GQA_GROUP + n)),
                  pl.BlockSpec((S, HD), lambda b, n, i: (b, GQA_KCOL + n)),
                  pl.BlockSpec((S, HD), lambda b, n, i: (b, GQA_VCOL + n)),
                  pl.BlockSpec((L, HD), lambda b, n, i: (B * cb + b, GQA_KCOL + n)),
                  pl.BlockSpec((L, HD), lambda b, n, i: (B * cb + b, GQA_VCOL + n)),
                  pl.BlockSpec((GQA_TQ, HD), lambda b, n, i: (i, 0)),
                  pl.BlockSpec((GQA_TQ, HD), lambda b, n, i: (i, 0)),
                  pl.BlockSpec((S, HD), lambda b, n, i: (0, 0)),
                  pl.BlockSpec((S, HD), lambda b, n, i: (0, 0)),
                  pl.BlockSpec((1, HD), lambda b, n, i: (0, 0)),
                  pl.BlockSpec((1, HD), lambda b, n, i: (0, 0))],
        out_specs=pl.BlockSpec((GQA_TQ, gw), lambda b, n, i: (b * nq + i, n)),
        out_shape=jax.ShapeDtypeStruct((T_LAT, GQA_Q_HEADS * HD), BF16),
        scratch_shapes=[pltpu.VMEM((S, HD), BF16), pltpu.VMEM((L, HD), BF16)],
        compiler_params=_cp(("arbitrary",) * 3),
        name="gqa_attention",
    )(p, p, p, p, p, cosf, sinf, cosf, sinf, q_gain.reshape(1, HD), k_gain.reshape(1, HD))


def _ctx_attn_kernel(p_ref, qg_ref, kg_ref, o_ref):
    def col(c):
        return p_ref[:, c * HD:(c + 1) * HD]

    for h in range(NA_HEADS):
        s = _dot_nt(col(h), col(NA_HEADS + h)) * ATT_SCALE
        o_ref[:, h * HD:(h + 1) * HD] = _softmax_av([(s, col(2 * NA_HEADS + h))]).astype(BF16)
    for n in range(GQA_KV_HEADS):
        kn = _rms_head(col(GQA_KCOL + n).astype(F32), kg_ref[...]).astype(BF16)
        v = col(GQA_VCOL + n)
        for g in range(GQA_GROUP):
            h = n * GQA_GROUP + g
            qn = _rms_head(col(GQA_QCOL + h).astype(F32), qg_ref[...]).astype(BF16)
            s = _dot_nt(qn, kn) * ATT_SCALE
            o_ref[:, (NA_HEADS + h) * HD:(NA_HEADS + h + 1) * HD] = _softmax_av([(s, v)]).astype(BF16)


def ctx_attention(p, q_gain, k_gain):
    cb = S // L
    return pl.pallas_call(
        _ctx_attn_kernel,
        grid=(B,),
        in_specs=[pl.BlockSpec((L, ATTN_IN), lambda b: (B * cb + b, 0)),
                  pl.BlockSpec((1, HD), lambda b: (0, 0)),
                  pl.BlockSpec((1, HD), lambda b: (0, 0))],
        out_specs=pl.BlockSpec((L, D), lambda b: (b, 0)),
        out_shape=jax.ShapeDtypeStruct((T_CTX, D), BF16),
        compiler_params=_cp(("arbitrary",)),
        name="ctx_attention",
    )(p, q_gain.reshape(1, HD), k_gain.reshape(1, HD))


def _log_sigmoid(x):
    return -(jnp.maximum(-x, 0.0) + jnp.log1p(jnp.exp(-jnp.abs(x))))


def _dot_hi(a, b):
    return jnp.dot(a, b, precision=lax.Precision.HIGHEST, preferred_element_type=F32)


def _gate_kernel(h_ref, wg_ref, wgt_ref, b_ref, bt_ref, lt_ref, ut_ref, col_ref, row_ref):
    nh = ML_HEADS
    hx = h_ref[...]
    g = jnp.dot(hx, wg_ref[...], preferred_element_type=F32) + b_ref[...]
    gt = _dot_nt(wgt_ref[...], hx) + bt_ref[...]
    li = g[:, 0:2 * nh]
    lf = _log_sigmoid(g[:, 2 * nh:4 * nh])
    lit = gt[0:2 * nh]
    lft = _log_sigmoid(gt[2 * nh:4 * nh])
    lt = lt_ref[...]
    ut = ut_ref[...]
    lane = lax.broadcasted_iota(I32, lf.shape, 1)
    bc = jnp.where(lane < nh, _dot_hi(lt, lf), _dot_hi(ut, lf))
    tot = jnp.sum(lf, axis=0, keepdims=True)
    aend = tot - bc + li
    col_ref[...] = jnp.concatenate([bc, aend, jnp.zeros((ML_CH, LANE - 4 * nh), F32)], axis=1)
    sub = lax.broadcasted_iota(I32, lft.shape, 0)
    bct = jnp.where(sub < nh, _dot_hi(lft, ut), _dot_hi(lft, lt))
    tott = jnp.sum(lft, axis=1, keepdims=True)
    gtr = lit - bct
    row_ref[0] = jnp.concatenate([bct, gtr, tott + gtr, jnp.broadcast_to(tott, bct.shape)], axis=0)


def mlstm_gates(hx, wg, gate_b):
    nh = ML_HEADS
    n_ch = T_ALL // ML_CH
    wg_pad = jnp.zeros((D, LANE), BF16).at[:, :4 * nh].set(wg.astype(BF16))
    b_pad = jnp.zeros((1, LANE), F32).at[0, :4 * nh].set(gate_b.reshape(-1))
    wgt = wg.astype(BF16).T
    bt = gate_b.reshape(4 * nh, 1).astype(F32)
    lt = jnp.tril(jnp.ones((ML_CH, ML_CH), F32))
    ut = jnp.triu(jnp.ones((ML_CH, ML_CH), F32))
    col, row = pl.pallas_call(
        _gate_kernel,
        grid=(n_ch,),
        in_specs=[pl.BlockSpec((ML_CH, D), lambda i: (i, 0)),
                  pl.BlockSpec((D, LANE), lambda i: (0, 0)),
                  pl.BlockSpec((4 * nh, D), lambda i: (0, 0)),
                  pl.BlockSpec((1, LANE), lambda i: (0, 0)),
                  pl.BlockSpec((4 * nh, 1), lambda i: (0, 0)),
                  pl.BlockSpec((ML_CH, ML_CH), lambda i: (0, 0)),
                  pl.BlockSpec((ML_CH, ML_CH), lambda i: (0, 0))],
        out_specs=[pl.BlockSpec((ML_CH, LANE), lambda i: (i, 0)),
                   pl.BlockSpec((1, 8 * nh, ML_CH), lambda i: (i, 0, 0))],
        out_shape=[jax.ShapeDtypeStruct((T_ALL, LANE), F32),
                   jax.ShapeDtypeStruct((n_ch, 8 * nh, ML_CH), F32)],
        compiler_params=_cp(("arbitrary",)),
        name="mlstm_gates",
    )(hx, wg_pad, wgt, b_pad, bt, lt, ut)
    col3 = col[:, :4 * nh].reshape(T_ALL, 2, 2 * nh).transpose(2, 0, 1)
    col3 = jnp.pad(col3, ((0, 0), (0, 0), (0, 6)))
    row4 = row.reshape(n_ch, 4, 2 * nh, ML_CH).transpose(2, 0, 1, 3)
    row4 = jnp.pad(row4, ((0, 0), (0, 0), (0, 4), (0, 0)))
    return col3, row4


def _mlstm_step(d, q_ref, k_ref, v_ref, col_ref, row_ref, o_ref, c_ref, n_ref, m_ref):
    q = q_ref[...]
    kf = k_ref[...].astype(F32) * ML_KSCALE
    kb = kf.astype(BF16)
    v = v_ref[...]
    col = col_ref[0]
    row = row_ref[0, 0]
    bc_col = col[:, 0:1]
    aend_col = col[:, 1:2]
    g_row = row[1:2]
    aend_row = row[2:3]
    btot = row[3:4, 0:1]
    m_st = m_ref[d]
    c_st = c_ref[d]
    n_st = n_ref[d]
    m_new = jnp.maximum(btot + m_st, jnp.max(aend_row, axis=1, keepdims=True))

    r = lax.broadcasted_iota(I32, (ML_CH, ML_CH), 0)
    c = lax.broadcasted_iota(I32, (ML_CH, ML_CH), 1)
    causal = (r >= c) if d == 0 else (r <= c)
    d_mat = jnp.where(causal, bc_col + g_row, -jnp.inf)
    m_row = jnp.maximum(bc_col + m_st, jnp.max(d_mat, axis=1, keepdims=True))
    w_inter = jnp.exp(bc_col + m_st - m_row)
    s_mat = _dot_nt(q, kb) * jnp.exp(d_mat - m_row)
    num = (w_inter * jnp.dot(q, c_st.astype(BF16), preferred_element_type=F32)
           + jnp.dot(s_mat.astype(BF16), v, preferred_element_type=F32))
    den = (w_inter * jnp.sum(q.astype(F32) * n_st, axis=1, keepdims=True)
           + jnp.sum(s_mat, axis=1, keepdims=True))
    o_ref[...] = num / jnp.maximum(jnp.abs(den), jnp.exp(-m_row))

    w_end_col = jnp.exp(aend_col - m_new)
    w_end_row = jnp.exp(aend_row - m_new)
    decay = jnp.exp(btot + m_st - m_new)
    kw = (kf * w_end_col).astype(BF16)
    c_ref[d] = decay * c_st + lax.dot_general(kw, v, (((0,), (0,)), ((), ())), preferred_element_type=F32)
    w8 = jnp.broadcast_to(w_end_row, (8, ML_CH)).astype(BF16)
    n_ref[d] = decay * n_st + jnp.dot(w8, kb, preferred_element_type=F32)[0:1]
    m_ref[d] = m_new


def _mlstm_kernel(qf, kf, vf, colf, rowf, qb, kb, vb, colb, rowb, of, ob, c_ref, n_ref, m_ref):
    @pl.when(pl.program_id(2) == 0)
    def _():
        c_ref[...] = jnp.zeros_like(c_ref)
        n_ref[...] = jnp.zeros_like(n_ref)
        m_ref[...] = jnp.zeros_like(m_ref)

    _mlstm_step(0, qf, kf, vf, colf, rowf, of, c_ref, n_ref, m_ref)
    _mlstm_step(1, qb, kb, vb, colb, rowb, ob, c_ref, n_ref, m_ref)


def mlstm_scan(p, col3, row4):
    n_lat = S // ML_CH
    steps = n_lat + 1
    lat_blocks = T_LAT // ML_CH
    vcol = (2 * ML_HEADS * ML_QK) // ML_V

    def chunk(b, d, st):
        c = (st - 1) if d == 0 else (n_lat - st)
        return jnp.where(st == 0, lat_blocks + b, b * n_lat + c)

    def out_chunk(b, d, st):
        s1 = jnp.maximum(st, 1)
        return b * n_lat + ((s1 - 1) if d == 0 else (n_lat - s1))

    def dir_specs(d):
        return [pl.BlockSpec((ML_CH, ML_QK), lambda b, h, s: (chunk(b, d, s), h)),
                pl.BlockSpec((ML_CH, ML_QK), lambda b, h, s: (chunk(b, d, s), ML_HEADS + h)),
                pl.BlockSpec((ML_CH, ML_V), lambda b, h, s: (chunk(b, d, s), vcol + h)),
                pl.BlockSpec((1, ML_CH, 8), lambda b, h, s: (d * ML_HEADS + h, chunk(b, d, s), 0)),
                pl.BlockSpec((1, 1, 8, ML_CH), lambda b, h, s: (d * ML_HEADS + h, chunk(b, d, s), 0, 0))]

    return pl.pallas_call(
        _mlstm_kernel,
        grid=(B, ML_HEADS, steps),
        in_specs=dir_specs(0) + dir_specs(1),
        out_specs=[pl.BlockSpec((ML_CH, ML_V), lambda b, h, s: (out_chunk(b, 0, s), h)),
                   pl.BlockSpec((ML_CH, ML_V), lambda b, h, s: (out_chunk(b, 1, s), h))],
        out_shape=[jax.ShapeDtypeStruct((T_LAT, ML_HEADS * ML_V), F32),
                   jax.ShapeDtypeStruct((T_LAT, ML_HEADS * ML_V), F32)],
        scratch_shapes=[pltpu.VMEM((2, ML_QK, ML_V), F32), pltpu.VMEM((2, 1, ML_QK), F32),
                        pltpu.VMEM((2, 1, 1), F32)],
        compiler_params=_cp(("arbitrary",) * 3),
        name="mlstm_scan",
    )(p, p, p, col3, row4, p, p, p, col3, row4)


def _readout_kernel(hf_ref, hb_ref, o_ref, g_ref, a_ref):
    hs = hf_ref[...] + hb_ref[...]
    for h in range(ML_HEADS):
        sl = slice(h * ML_V, (h + 1) * ML_V)
        x = hs[:, sl]
        hn = x * lax.rsqrt(jnp.mean(x * x, axis=-1, keepdims=True) + EPS) * g_ref[:, sl]
        a_ref[:, sl] = (hn * jax.nn.sigmoid(o_ref[:, sl].astype(F32))).astype(BF16)


def mlstm_readout(hdir, p, head_gain):
    tm = 256
    ocol = (2 * ML_HEADS * ML_QK + ML_HEADS * ML_V) // D
    return pl.pallas_call(
        _readout_kernel,
        grid=(T_LAT // tm,),
        in_specs=[pl.BlockSpec((tm, D), lambda i: (i, 0)),
                  pl.BlockSpec((tm, D), lambda i: (i, 0)),
                  pl.BlockSpec((tm, D), lambda i: (i, ocol)),
                  pl.BlockSpec((1, D), lambda i: (0, 0))],
        out_specs=pl.BlockSpec((tm, D), lambda i: (i, 0)),
        out_shape=jax.ShapeDtypeStruct((T_LAT, D), BF16),
        compiler_params=_cp(("arbitrary",)),
        name="mlstm_readout",
    )(hdir[0], hdir[1], p, head_gain.reshape(1, D))


ROUTER_TM = 512


def _router_kernel(h_ref, w_ref, rb_ref, erow_ref, tri_ref, eidx_ref, wts_ref, pos_ref, cnt_ref, carry_ref):
    ng = N_GROUPS
    epg = N_EXPERTS // N_GROUPS
    tm = ROUTER_TM
    ninf = -jnp.inf

    @pl.when(pl.program_id(0) == 0)
    def _():
        carry_ref[...] = jnp.zeros_like(carry_ref)

    s = jax.nn.sigmoid(_dot_nt(w_ref[...], h_ref[...]))
    ssel = s + rb_ref[...]
    sraw = [s[ng * j:ng * (j + 1)] for j in range(epg)]
    slab = [ssel[ng * j:ng * (j + 1)] for j in range(epg)]
    m1 = functools.reduce(jnp.maximum, slab)
    jfirst = functools.reduce(jnp.minimum, [jnp.where(slab[j] == m1, j, epg) for j in range(epg)])
    m2 = functools.reduce(jnp.maximum, [jnp.where(jfirst == j, ninf, slab[j]) for j in range(epg)])
    gs = m1 + m2
    giota = lax.broadcasted_iota(I32, (ng, tm), 0)
    gsel = jnp.zeros((ng, tm), F32)
    for _ in range(TOPK_GROUPS):
        mx = jnp.max(gs, axis=0, keepdims=True)
        gi = jnp.min(jnp.where(gs == mx, giota, ng), axis=0, keepdims=True)
        hit = giota == gi
        gsel = jnp.where(hit, 1.0, gsel)
        gs = jnp.where(hit, ninf, gs)
    msl = [jnp.where(gsel > 0.0, slab[j], ninf) for j in range(epg)]
    eid = [giota * epg + j for j in range(epg)]
    selm = [jnp.zeros((ng, tm), F32) for _ in range(epg)]
    e_list, w_list = [], []
    for _ in range(TOP_K):
        mx = jnp.max(functools.reduce(jnp.maximum, msl), axis=0, keepdims=True)
        cand = functools.reduce(jnp.minimum, [jnp.where(msl[j] == mx, eid[j], N_EXPERTS) for j in range(epg)])
        esel = jnp.min(cand, axis=0, keepdims=True)
        hits = [eid[j] == esel for j in range(epg)]
        wk = functools.reduce(lambda a, b: a + b, [jnp.where(hits[j], sraw[j], 0.0) for j in range(epg)])
        w_list.append(jnp.sum(wk, axis=0, keepdims=True))
        e_list.append(esel)
        msl = [jnp.where(hits[j], ninf, msl[j]) for j in range(epg)]
        selm = [jnp.where(hits[j], 1.0, selm[j]) for j in range(epg)]
    wsum = functools.reduce(lambda a, b: a + b, w_list)
    wts_ref[...] = jnp.concatenate([w / wsum * ROUTED_SCALE for w in w_list], axis=0)
    eidx_ref[...] = jnp.concatenate(e_list, axis=0)
    sel = jnp.concatenate(selm, axis=0)
    carry = carry_ref[...]
    posfull = jnp.dot(sel.astype(BF16), tri_ref[...], preferred_element_type=F32) + carry
    erow = erow_ref[...]
    pos = [jnp.sum(jnp.where(erow == e, posfull, 0.0), axis=0, keepdims=True) for e in e_list]
    pos_ref[...] = jnp.concatenate(pos, axis=0).astype(I32)
    carry = carry + jnp.sum(sel, axis=1, keepdims=True)
    carry_ref[...] = carry
    cnt_ref[...] = carry


def moe_router(hx, router_w, router_b, n_tok):
    tm = ROUTER_TM
    epg = N_EXPERTS // N_GROUPS
    perm = (jnp.arange(N_EXPERTS) % N_GROUPS) * epg + jnp.arange(N_EXPERTS) // N_GROUPS
    w_t = router_w.astype(BF16).T[perm]
    rb = router_b.astype(F32)[perm].reshape(N_EXPERTS, 1)
    erow = perm.astype(I32).reshape(N_EXPERTS, 1)
    tri = jnp.triu(jnp.ones((tm, tm), BF16), 1)
    eidx, wts, pos, counts = pl.pallas_call(
        _router_kernel,
        grid=(n_tok // tm,),
        in_specs=[pl.BlockSpec((tm, D), lambda i: (i, 0)),
                  pl.BlockSpec((N_EXPERTS, D), lambda i: (0, 0)),
                  pl.BlockSpec((N_EXPERTS, 1), lambda i: (0, 0)),
                  pl.BlockSpec((N_EXPERTS, 1), lambda i: (0, 0)),
                  pl.BlockSpec((tm, tm), lambda i: (0, 0))],
        out_specs=[pl.BlockSpec((TOP_K, tm), lambda i: (0, i)),
                   pl.BlockSpec((TOP_K, tm), lambda i: (0, i)),
                   pl.BlockSpec((TOP_K, tm), lambda i: (0, i)),
                   pl.BlockSpec((N_EXPERTS, 1), lambda i: (0, 0))],
        out_shape=[jax.ShapeDtypeStruct((TOP_K, n_tok), I32),
                   jax.ShapeDtypeStruct((TOP_K, n_tok), F32),
                   jax.ShapeDtypeStruct((TOP_K, n_tok), I32),
                   jax.ShapeDtypeStruct((N_EXPERTS, 1), F32)],
        scratch_shapes=[pltpu.VMEM((N_EXPERTS, 1), F32)],
        compiler_params=_cp(("arbitrary",)),
        name="moe_router",
    )(hx, w_t, rb, erow, tri)
    return eidx, wts, pos, counts.reshape(N_EXPERTS)[perm]


DISPATCH_TM = 512


def _dispatch_kernel(slot_ref, hx_ref, xs_hbm, sem):
    def issue(t, carry):
        src = hx_ref.at[pl.ds(t, 1), :]
        for k in range(TOP_K):
            pltpu.make_async_copy(src, xs_hbm.at[pl.ds(slot_ref[k, t], 1), :], sem).start()
        return carry

    lax.fori_loop(0, DISPATCH_TM, issue, 0)
    for _ in range(TOP_K):
        pltpu.make_async_copy(hx_ref, xs_hbm.at[pl.ds(0, DISPATCH_TM), :], sem).wait()


def moe_dispatch(slots, hx_packed, n_tok, n_rows):
    tm = DISPATCH_TM
    return pl.pallas_call(
        _dispatch_kernel,
        grid=(n_tok // tm,),
        in_specs=[pl.BlockSpec((TOP_K, tm), lambda i: (0, i), memory_space=pltpu.SMEM),
                  pl.BlockSpec((tm, PK_W), lambda i: (i, 0))],
        out_specs=pl.BlockSpec(memory_space=pl.ANY),
        out_shape=jax.ShapeDtypeStruct((n_rows, PK_W), U32),
        scratch_shapes=[pltpu.SemaphoreType.DMA(())],
        compiler_params=_cp(("arbitrary",)),
        name="moe_dispatch",
    )(slots, hx_packed)


def _expert_kernel(be_ref, valid_ref, nused_ref, xs_ref, w1_ref, w3_ref, w2_ref, y_ref, w1b, w3b, w2b, xb):
    i = pl.program_id(0)
    bm = EXP_BM

    @pl.when(i < nused_ref[0])
    def _():
        prev = be_ref[jnp.maximum(i - 1, 0)]

        @pl.when((i == 0) | (be_ref[i] != prev))
        def _():
            w1b[...] = w1_ref[0, 0].astype(BF16)
            w3b[...] = w3_ref[0, 0].astype(BF16)
            w2b[...] = w2_ref[0, 0].astype(BF16)

        rows = lax.broadcasted_iota(I32, (bm, PK_W), 0)
        u = jnp.where(rows < valid_ref[i], xs_ref[...], jnp.uint32(0))
        xb[:, :PK_W] = pltpu.bitcast(u << 16, F32).astype(BF16)
        xb[:, PK_W:] = pltpu.bitcast(u & jnp.uint32(0xFFFF0000), F32).astype(BF16)
        x = xb[...]
        h1 = jnp.dot(x, w1b[...], preferred_element_type=F32)
        h3 = jnp.dot(x, w3b[...], preferred_element_type=F32)
        a = (h1 * jax.nn.sigmoid(h1) * h3).astype(BF16)
        y_ref[...] = jnp.dot(a, w2b[...], preferred_element_type=F32)


def moe_experts(block_e, valid, n_used, xs, w1, w3, w2, layer, n_blocks):
    bm = EXP_BM

    def blk(i, be, va, nu):
        return jnp.minimum(i, nu[0] - 1)

    grid_spec = pltpu.PrefetchScalarGridSpec(
        num_scalar_prefetch=3,
        grid=(n_blocks,),
        in_specs=[pl.BlockSpec((bm, PK_W), lambda i, be, va, nu: (blk(i, be, va, nu), 0)),
                  pl.BlockSpec((1, 1, D, EXPERT_DIM), lambda i, be, va, nu: (layer, be[blk(i, be, va, nu)], 0, 0)),
                  pl.BlockSpec((1, 1, D, EXPERT_DIM), lambda i, be, va, nu: (layer, be[blk(i, be, va, nu)], 0, 0)),
                  pl.BlockSpec((1, 1, EXPERT_DIM, D), lambda i, be, va, nu: (layer, be[blk(i, be, va, nu)], 0, 0))],
        out_specs=pl.BlockSpec((bm, D), lambda i, be, va, nu: (blk(i, be, va, nu), 0)),
        scratch_shapes=[pltpu.VMEM((D, EXPERT_DIM), BF16), pltpu.VMEM((D, EXPERT_DIM), BF16),
                        pltpu.VMEM((EXPERT_DIM, D), BF16), pltpu.VMEM((bm, D), BF16)],
    )
    return pl.pallas_call(
        _expert_kernel,
        grid_spec=grid_spec,
        out_shape=jax.ShapeDtypeStruct((n_blocks * bm, D), F32),
        compiler_params=_cp(("arbitrary",), vmem=56 * 1024 * 1024),
        name="moe_experts",
    )(block_e, valid, n_used, xs, w1, w3, w2)


def _shared_kernel(x_ref, w1_ref, w3_ref, w2_ref, o_ref):
    x = x_ref[...]
    h1 = jnp.dot(x, w1_ref[...], preferred_element_type=F32)
    h3 = jnp.dot(x, w3_ref[...], preferred_element_type=F32)
    a = (h1 * jax.nn.sigmoid(h1) * h3).astype(BF16)
    o_ref[...] = jnp.dot(a, w2_ref[...], preferred_element_type=F32)


def shared_expert(hx, w1, w3, w2, n_tok):
    tm = 512
    return pl.pallas_call(
        _shared_kernel,
        grid=(n_tok // tm,),
        in_specs=[pl.BlockSpec((tm, D), lambda i: (i, 0)),
                  pl.BlockSpec((D, EXPERT_DIM), lambda i: (0, 0)),
                  pl.BlockSpec((D, EXPERT_DIM), lambda i: (0, 0)),
                  pl.BlockSpec((EXPERT_DIM, D), lambda i: (0, 0))],
        out_specs=pl.BlockSpec((tm, D), lambda i: (i, 0)),
        out_shape=jax.ShapeDtypeStruct((n_tok, D), F32),
        compiler_params=_cp(("arbitrary",)),
        name="shared_expert",
    )(hx, w1, w3, w2)


COMBINE_TM = 128


def _combine_kernel(slot_ref, w_ref, sh_ref, x_ref, mod_ref, y_hbm, o_ref, buf, sem):
    tm = COMBINE_TM

    def issue(t, carry):
        for k in range(TOP_K):
            src = y_hbm.at[pl.ds(slot_ref[k, t], 1), :]
            pltpu.make_async_copy(src, buf.at[pl.ds(k * tm + t, 1), :], sem).start()
        return carry

    lax.fori_loop(0, tm, issue, 0)
    pltpu.make_async_copy(y_hbm.at[pl.ds(0, TOP_K * tm), :], buf, sem).wait()

    w = w_ref[...]
    acc = None
    for k in range(TOP_K):
        term = w[:, k:k + 1] * buf[k * tm:(k + 1) * tm, :]
        acc = term if acc is None else acc + term
    o_ref[...] = x_ref[...] + mod_ref[0][5:6] * (sh_ref[...] + acc)


def moe_combine(slots, wts_tok, shared, x, mod, y, n_tok):
    tm = COMBINE_TM
    return pl.pallas_call(
        _combine_kernel,
        grid=(n_tok // tm,),
        in_specs=[pl.BlockSpec((TOP_K, tm), lambda i: (0, i), memory_space=pltpu.SMEM),
                  pl.BlockSpec((tm, TOP_K), lambda i: (i, 0)),
                  pl.BlockSpec((tm, D), lambda i: (i, 0)),
                  pl.BlockSpec((tm, D), lambda i: (i, 0)),
                  pl.BlockSpec((1, 6, D), lambda i: (_mod_row(i * tm), 0, 0)),
                  pl.BlockSpec(memory_space=pl.ANY)],
        out_specs=pl.BlockSpec((tm, D), lambda i: (i, 0)),
        out_shape=jax.ShapeDtypeStruct((n_tok, D), F32),
        scratch_shapes=[pltpu.VMEM((TOP_K * tm, D), F32), pltpu.SemaphoreType.DMA(())],
        compiler_params=_cp(("arbitrary",)),
        name="moe_combine",
    )(slots, wts_tok, shared, x, mod, y)


def _lookup(table, idx):
    e = jnp.arange(table.shape[0], dtype=I32).reshape((-1,) + (1,) * idx.ndim)
    return jnp.sum(jnp.where(idx[None] == e, table.reshape(e.shape), 0), axis=0)


def moe_layer(x, mod, norm_gain, router_w, router_b, exp_w1, exp_w3, exp_w2, sw1, sw3, sw2, layer, n_tok):
    bm = EXP_BM
    n_blocks = -(-n_tok * TOP_K // bm) + N_EXPERTS
    hx, hx_packed = norm_mod(x, norm_gain, mod, 3, n_tok, pack=True)
    eidx, wts, pos, counts = moe_router(hx, router_w, router_b, n_tok)
    shared = shared_expert(hx, sw1.astype(BF16), sw3.astype(BF16), sw2.astype(BF16), n_tok)
    cnt = counts.astype(I32)
    padded = (cnt + bm - 1) // bm * bm
    pad_end = jnp.cumsum(padded)
    pad_start = pad_end - padded
    slots = _lookup(pad_start, eidx) + pos
    blk_row = jnp.arange(n_blocks, dtype=I32) * bm
    block_e = jnp.minimum(jnp.sum((pad_end[:, None] <= blk_row[None, :]).astype(I32), axis=0), N_EXPERTS - 1)
    valid = jnp.clip(_lookup(cnt, block_e) - (blk_row - _lookup(pad_start, block_e)), 0, bm).astype(I32)
    n_used = (pad_end[-1:] // bm).astype(I32)
    xs = moe_dispatch(slots, hx_packed, n_tok, n_blocks * bm)
    y = moe_experts(block_e, valid, n_used, xs, exp_w1, exp_w3, exp_w2, layer, n_blocks)
    return moe_combine(slots, wts.T, shared, x, mod, y, n_tok)


def _rope_tables():
    t = jnp.arange(S, dtype=I32)
    row = (t // GRID_W).astype(F32)
    col = (t % GRID_W).astype(F32)
    n_freq = HD // 4
    inv_freq = ROPE_THETA ** (-jnp.arange(n_freq, dtype=F32) / n_freq)
    ang = jnp.concatenate([row[:, None] * inv_freq, col[:, None] * inv_freq], axis=-1)
    cosf = jnp.repeat(jnp.cos(ang), 2, axis=-1)
    sinf = jnp.stack([-jnp.sin(ang), jnp.sin(ang)], axis=-1).reshape(S, HD)
    return cosf, sinf


def kernel(x, c, ctx, c_ctx, ada_w, ada_b, norm_mix, norm_ffn, attn_w_in, attn_w_out, attn_rpb, attn_q_gain,
           attn_k_gain, ml_w_in, ml_w_out, ml_gate_b, ml_head_gain, router_w, router_b, exp_w1, exp_w3, exp_w2,
           sh_w1, sh_w3, sh_w2, final_norm_gain):
    depth = ada_w.shape[0]
    xa = jnp.concatenate([x.reshape(T_LAT, D), ctx.reshape(T_CTX, D)], axis=0)
    cvec = jnp.concatenate([c, c_ctx[None], jnp.zeros((8 - B - 1, D), F32)], axis=0)
    mod_all = ada_ln(cvec, ada_w, ada_b).reshape(depth, 8, 6, D)
    cosf, sinf = _rope_tables()

    mod = mod_all[0]
    hx = norm_mod(xa, norm_mix[0], mod, 0, T_ALL, pack=False)
    p = matmul(hx, attn_w_in[0].astype(BF16), BF16)
    o_na = neighborhood_attention(p, na_bias_table(attn_rpb[0]))
    o_gqa = gqa_attention(p, cosf, sinf, attn_q_gain[0], attn_k_gain[0])
    o_ctx = ctx_attention(p, attn_q_gain[0], attn_k_gain[0])
    o_all = jnp.concatenate([jnp.concatenate([o_na, o_gqa], axis=1), o_ctx], axis=0)
    xa = matmul_gated_residual(o_all, attn_w_out[0].astype(BF16), xa, mod, 2)
    xa = moe_layer(xa, mod, norm_ffn[0], router_w[0], router_b[0], exp_w1, exp_w3, exp_w2,
                   sh_w1[0], sh_w3[0], sh_w2[0], 0, T_ALL)

    mod = mod_all[1]
    hx = norm_mod(xa, norm_mix[1], mod, 0, T_ALL, pack=False)
    w_in = ml_w_in[0]
    p = matmul(hx, w_in[:, :ML_MAIN].astype(BF16), BF16)
    col3, row4 = mlstm_gates(hx, w_in[:, ML_MAIN:], ml_gate_b[0])
    hdir = mlstm_scan(p, col3, row4)
    a = mlstm_readout(hdir, p, ml_head_gain[0])
    xl = matmul_gated_residual(a, ml_w_out[0].astype(BF16), xa, mod, 2)
    xl = moe_layer(xl, mod, norm_ffn[1], router_w[1], router_b[1], exp_w1, exp_w3, exp_w2,
                   sh_w1[1], sh_w3[1], sh_w2[1], 1, T_LAT)
    return final_norm(xl, final_norm_gain).reshape(B, S, D)
```

```python
import functools

import jax
import jax.numpy as jnp
from jax import lax
from jax.experimental import pallas as pl
from jax.experimental.pallas import tpu as pltpu

F32 = jnp.float32
BF16 = jnp.bfloat16
I32 = jnp.int32
U32 = jnp.uint32

D = 2048
B = 4
S = 4096
L = 256
T_LAT = B * S
T_CTX = B * L
T_ALL = T_LAT + T_CTX
GRID_W = 64
ROWS = S // GRID_W
HD = 128
NA_HEADS = 8
NA_WIN_ROWS = 8
NA_WIN_COLS = 16
GQA_Q_HEADS = 8
GQA_KV_HEADS = 2
GQA_GROUP = 4
ROPE_THETA = 10000.0
ATTN_IN = 4608
ML_HEADS = 8
ML_V = 256
ML_QK = 128
ML_MAIN = 6144
N_EXPERTS = 64
TOP_K = 8
N_GROUPS = 8
TOPK_GROUPS = 4
EXPERT_DIM = 512
ROUTED_SCALE = 2.5
EPS = 1e-6
NEG_INF = -1e30
ATT_SCALE = HD ** -0.5
LOG2E = 1.4426950408889634
ML_KSCALE = ML_QK ** -0.5

LANE = 128
NA_QROWS = 4
NA_SLAB = NA_QROWS + NA_WIN_ROWS - 1
NA_QB = NA_QROWS * GRID_W
NA_KB = NA_SLAB * GRID_W
ML_CH = 256
ML_HPS = 2
EXP_BM = 512
PK_W = D // 2
VMEM_LIMIT = 48 * 1024 * 1024


def _cp(sem, vmem=VMEM_LIMIT):
    return pltpu.CompilerParams(dimension_semantics=sem, vmem_limit_bytes=vmem)


def _pack_bf16_pairs(xb):
    u = pltpu.bitcast(xb.astype(F32), U32)
    return (u[:, PK_W:] & jnp.uint32(0xFFFF0000)) | (u[:, :PK_W] >> 16)


def _unpack_bf16_pairs(u):
    return pltpu.bitcast(u << 16, F32), pltpu.bitcast(u & jnp.uint32(0xFFFF0000), F32)


def _mod_row(start_row):
    return jnp.where(start_row < T_LAT, start_row // S, B)


def _ada_kernel(c_ref, w_ref, b_ref, o_ref):
    c = c_ref[...]
    a = (c * jax.nn.sigmoid(c)).astype(BF16)
    w = w_ref[0].astype(BF16)
    o_ref[0] = jnp.dot(a, w, preferred_element_type=F32) + b_ref[0]


def ada_ln(cvec, ada_w, ada_b):
    depth = ada_w.shape[0]
    n = ada_w.shape[2]
    tn = 1024
    return pl.pallas_call(
        _ada_kernel,
        grid=(depth, n // tn),
        in_specs=[pl.BlockSpec((8, D), lambda l, j: (0, 0)),
                  pl.BlockSpec((1, D, tn), lambda l, j: (l, 0, j)),
                  pl.BlockSpec((1, 1, tn), lambda l, j: (l, 0, j))],
        out_specs=pl.BlockSpec((1, 8, tn), lambda l, j: (l, 0, j)),
        out_shape=jax.ShapeDtypeStruct((depth, 8, n), F32),
        compiler_params=_cp(("arbitrary", "arbitrary")),
        name="ada_ln",
    )(cvec, ada_w, ada_b.reshape(depth, 1, n))


def _norm_mod_kernel(x_ref, g_ref, mod_ref, *out_refs, base, pack):
    x = x_ref[...]
    y = x * lax.rsqrt(jnp.mean(x * x, axis=-1, keepdims=True) + EPS) * g_ref[...]
    m = mod_ref[0]
    h = y * (1.0 + m[base + 1:base + 2]) + m[base:base + 1]
    hb = h.astype(BF16)
    out_refs[0][...] = hb
    if pack:
        out_refs[1][...] = _pack_bf16_pairs(hb)


def norm_mod(x, gain, mod, base, n_rows, pack):
    tm = 256
    out_shape = [jax.ShapeDtypeStruct((n_rows, D), BF16)]
    out_specs = [pl.BlockSpec((tm, D), lambda i: (i, 0))]
    if pack:
        out_shape.append(jax.ShapeDtypeStruct((n_rows, PK_W), U32))
        out_specs.append(pl.BlockSpec((tm, PK_W), lambda i: (i, 0)))
    res = pl.pallas_call(
        functools.partial(_norm_mod_kernel, base=base, pack=pack),
        grid=(n_rows // tm,),
        in_specs=[pl.BlockSpec((tm, D), lambda i: (i, 0)),
                  pl.BlockSpec((1, D), lambda i: (0, 0)),
                  pl.BlockSpec((1, 6, D), lambda i: (_mod_row(i * tm), 0, 0))],
        out_specs=out_specs,
        out_shape=out_shape,
        compiler_params=_cp(("arbitrary",)),
        name="norm_mod",
    )(x, gain.reshape(1, D), mod)
    return res if pack else res[0]


def _final_norm_kernel(x_ref, g_ref, o_ref):
    x = x_ref[...]
    o_ref[...] = x * lax.rsqrt(jnp.mean(x * x, axis=-1, keepdims=True) + EPS) * g_ref[...]


def final_norm(x, gain):
    tm = 256
    return pl.pallas_call(
        _final_norm_kernel,
        grid=(T_LAT // tm,),
        in_specs=[pl.BlockSpec((tm, D), lambda i: (i, 0)),
                  pl.BlockSpec((1, D), lambda i: (0, 0))],
        out_specs=pl.BlockSpec((tm, D), lambda i: (i, 0)),
        out_shape=jax.ShapeDtypeStruct((T_LAT, D), F32),
        compiler_params=_cp(("arbitrary",)),
        name="final_norm",
    )(x, gain.reshape(1, D))


def _mm_kernel(a_ref, w_ref, o_ref):
    o_ref[...] = jnp.dot(a_ref[...], w_ref[...], preferred_element_type=F32).astype(o_ref.dtype)


def matmul(a, w, out_dtype, tm=1024, tn=512):
    m, k = a.shape
    n = w.shape[1]
    return pl.pallas_call(
        _mm_kernel,
        grid=(m // tm, n // tn),
        in_specs=[pl.BlockSpec((tm, k), lambda i, j: (i, 0)),
                  pl.BlockSpec((k, tn), lambda i, j: (0, j))],
        out_specs=pl.BlockSpec((tm, tn), lambda i, j: (i, j)),
        out_shape=jax.ShapeDtypeStruct((m, n), out_dtype),
        compiler_params=_cp(("arbitrary", "arbitrary")),
        name="matmul",
    )(a, w)


def _mm_res_kernel(a_ref, w_ref, x_ref, mod_ref, o_ref, *, slot):
    acc = jnp.dot(a_ref[...], w_ref[...], preferred_element_type=F32)
    o_ref[...] = x_ref[...] + mod_ref[0][slot:slot + 1] * acc


def matmul_gated_residual(a, w, x, mod, slot, tm=1024, tn=512):
    m, k = a.shape
    n = w.shape[1]
    return pl.pallas_call(
        functools.partial(_mm_res_kernel, slot=slot),
        grid=(m // tm, n // tn),
        in_specs=[pl.BlockSpec((tm, k), lambda i, j: (i, 0)),
                  pl.BlockSpec((k, tn), lambda i, j: (0, j)),
                  pl.BlockSpec((tm, tn), lambda i, j: (i, j)),
                  pl.BlockSpec((1, 6, tn), lambda i, j: (_mod_row(i * tm), 0, j))],
        out_specs=pl.BlockSpec((tm, tn), lambda i, j: (i, j)),
        out_shape=jax.ShapeDtypeStruct((m, n), F32),
        compiler_params=_cp(("arbitrary", "arbitrary")),
        name="matmul_gated_residual",
    )(a, w, x, mod)


def _dot_nt(a, b):
    return lax.dot_general(a, b, (((1,), (1,)), ((), ())), preferred_element_type=F32)


def _rms_head(x, gain):
    return x * lax.rsqrt(jnp.mean(x * x, axis=-1, keepdims=True) + EPS) * gain


def _rope(x, cosf, sinf):
    lane = lax.broadcasted_iota(I32, x.shape, 1)
    nxt = pltpu.roll(x, LANE - 1, 1)
    prv = pltpu.roll(x, 1, 1)
    return x * cosf + jnp.where((lane & 1) == 0, nxt, prv) * sinf


def _softmax_av(parts):
    m = functools.reduce(jnp.maximum, [jnp.max(s, axis=-1, keepdims=True) for s, _ in parts])
    l = None
    o = None
    for s, v in parts:
        p = jnp.exp(s - m)
        li = jnp.sum(p, axis=-1, keepdims=True)
        oi = jnp.dot(p.astype(BF16), v, preferred_element_type=F32)
        l = li if l is None else l + li
        o = oi if o is None else o + oi
    return o / l


def _na_kernel(q_ref, k_ref, v_ref, kc_ref, vc_ref, tab_ref, o_ref):
    kc = kc_ref[...]
    vc = vc_ref[...]
    n_blocks = ROWS // NA_QROWS

    def body(j, carry):
        ks = jnp.clip(j * NA_QROWS - NA_WIN_ROWS // 2, 0, ROWS - NA_SLAB)
        typ = jnp.where(j == 0, 0, jnp.where(j == n_blocks - 1, 2, 1))
        qs = pl.multiple_of(j * NA_QB, NA_QB)
        kst = pl.multiple_of(ks * GRID_W, GRID_W)
        q = q_ref[pl.ds(qs, NA_QB), :]
        k = k_ref[pl.ds(kst, NA_KB), :]
        v = v_ref[pl.ds(kst, NA_KB), :]
        s_win = _dot_nt(q, k) * ATT_SCALE + tab_ref[typ, 0]
        s_ctx = _dot_nt(q, kc) * ATT_SCALE
        o_ref[pl.ds(qs, NA_QB), :] = _softmax_av([(s_win, v), (s_ctx, vc)]).astype(BF16)
        return carry

    lax.fori_loop(0, n_blocks, body, 0)


def na_bias_table(rpb):
    def one(r0, ks):
        r = r0 + jnp.arange(NA_QROWS)
        kr = ks + jnp.arange(NA_SLAB)
        start = jnp.clip(r - NA_WIN_ROWS // 2, 0, ROWS - NA_WIN_ROWS)
        row_ok = (kr[None, :] >= start[:, None]) & (kr[None, :] < start[:, None] + NA_WIN_ROWS)
        row_idx = jnp.clip(kr[None, :] - r[:, None] + NA_WIN_ROWS - 1, 0, 2 * NA_WIN_ROWS - 2)
        cq = jnp.arange(GRID_W)
        col_start = jnp.clip(cq - NA_WIN_COLS // 2, 0, GRID_W - NA_WIN_COLS)
        col_ok = (cq[None, :] >= col_start[:, None]) & (cq[None, :] < col_start[:, None] + NA_WIN_COLS)
        col_idx = jnp.clip(cq[None, :] - cq[:, None] + NA_WIN_COLS - 1, 0, 2 * NA_WIN_COLS - 2)
        r_hot = jax.nn.one_hot(row_idx, 2 * NA_WIN_ROWS - 1, dtype=F32)
        c_hot = jax.nn.one_hot(col_idx, 2 * NA_WIN_COLS - 1, dtype=F32)
        bias = jnp.einsum('qka,hab,xyb->hqxky', r_hot, rpb.astype(F32), c_hot, precision=lax.Precision.HIGHEST)
        ok = row_ok[:, None, :, None] & col_ok[None, :, None, :]
        return jnp.where(ok[None], bias, NEG_INF).reshape(NA_HEADS, NA_QB, NA_KB)

    mid = 2 * NA_QROWS
    last = ROWS - NA_QROWS
    return jnp.stack([one(0, 0), one(mid, mid - NA_WIN_ROWS // 2), one(last, ROWS - NA_SLAB)])


def neighborhood_attention(p, table):
    cb = S // L
    return pl.pallas_call(
        _na_kernel,
        grid=(NA_HEADS, B),
        in_specs=[pl.BlockSpec((S, HD), lambda h, b: (b, h)),
                  pl.BlockSpec((S, HD), lambda h, b: (b, NA_HEADS + h)),
                  pl.BlockSpec((S, HD), lambda h, b: (b, 2 * NA_HEADS + h)),
                  pl.BlockSpec((L, HD), lambda h, b: (B * cb + b, NA_HEADS + h)),
                  pl.BlockSpec((L, HD), lambda h, b: (B * cb + b, 2 * NA_HEADS + h)),
                  pl.BlockSpec((3, 1, NA_QB, NA_KB), lambda h, b: (0, h, 0, 0))],
        out_specs=pl.BlockSpec((S, HD), lambda h, b: (b, h)),
        out_shape=jax.ShapeDtypeStruct((T_LAT, NA_HEADS * HD), BF16),
        compiler_params=_cp(("arbitrary", "arbitrary")),
        name="neighborhood_attention",
    )(p, p, p, p, p, table)


GQA_TQ = 256
GQA_CK = 512
GQA_QCOL = 3 * NA_HEADS
GQA_KCOL = GQA_QCOL + GQA_Q_HEADS
GQA_VCOL = GQA_KCOL + GQA_KV_HEADS


def _gqa_kernel(q_ref, k_ref, v_ref, kc_ref, vc_ref, cq_ref, sq_ref, ck_ref, sk_ref, qg_ref, kg_ref,
                o_ref, kn_ref, kcn_ref):
    @pl.when(pl.program_id(2) == 0)
    def _():
        kn = _rope(_rms_head(k_ref[...].astype(F32), kg_ref[...]), ck_ref[...], sk_ref[...])
        kn_ref[...] = kn.astype(BF16)
        kcn_ref[...] = _rms_head(kc_ref[...].astype(F32), kg_ref[...]).astype(BF16)

    cos = cq_ref[...]
    sin = sq_ref[...]
    heads = []
    for g in range(GQA_GROUP):
        qh = _rope(_rms_head(q_ref[:, g * HD:(g + 1) * HD].astype(F32), qg_ref[...]), cos, sin)
        heads.append((qh * (ATT_SCALE * LOG2E)).astype(BF16))
    q = jnp.concatenate(heads, axis=0)
    chunks = [(kn_ref[c * GQA_CK:(c + 1) * GQA_CK, :], v_ref[c * GQA_CK:(c + 1) * GQA_CK, :])
              for c in range(S // GQA_CK)]
    chunks.append((kcn_ref[...], vc_ref[...]))
    m = l = acc = None
    for kk, vv in chunks:
        s = _dot_nt(q, kk)
        mc = jnp.max(s, axis=-1, keepdims=True)
        if m is None:
            m_new = mc
            p = jnp.exp2(s - m_new)
            l = jnp.sum(p, axis=-1, keepdims=True)
            acc = jnp.dot(p.astype(BF16), vv, preferred_element_type=F32)
        else:
            m_new = jnp.maximum(m, mc)
            alpha = jnp.exp2(m - m_new)
            p = jnp.exp2(s - m_new)
            l = alpha * l + jnp.sum(p, axis=-1, keepdims=True)
            acc = alpha * acc + jnp.dot(p.astype(BF16), vv, preferred_element_type=F32)
        m = m_new
    o = acc / l
    for g in range(GQA_GROUP):
        o_ref[:, g * HD:(g + 1) * HD] = o[g * GQA_TQ:(g + 1) * GQA_TQ].astype(BF16)


def gqa_attention(p, cosf, sinf, q_gain, k_gain):
    nq = S // GQA_TQ
    cb = S // L
    gw = GQA_GROUP * HD
    return pl.pallas_call(
        _gqa_kernel,
        grid=(B, GQA_KV_HEADS, nq),
        in_specs=[pl.BlockSpec((GQA_TQ, gw), lambda b, n, i: (b * nq + i, GQA_QCOL // GQA_GROUP + n)),
                  pl.BlockSpec((S, HD), lambda b, n, i: (b, GQA_KCOL + n)),
                  pl.BlockSpec((S, HD), lambda b, n, i: (b, GQA_VCOL + n)),
                  pl.BlockSpec((L, HD), lambda b, n, i: (B * cb + b, GQA_KCOL + n)),
                  pl.BlockSpec((L, HD), lambda b, n, i: (B * cb + b, GQA_VCOL + n)),
                  pl.BlockSpec((GQA_TQ, HD), lambda b, n, i: (i, 0)),
                  pl.BlockSpec((GQA_TQ, HD), lambda b, n, i: (i, 0)),
                  pl.BlockSpec((S, HD), lambda b, n, i: (0, 0)),
                  pl.BlockSpec((S, HD), lambda b, n, i: (0, 0)),
                  pl.BlockSpec((1, HD), lambda b, n, i: (0, 0)),
                  pl.BlockSpec((1, HD), lambda b, n, i: (0, 0))],
        out_specs=pl.BlockSpec((GQA_TQ, gw), lambda b, n, i: (b * nq + i, n)),
        out_shape=jax.ShapeDtypeStruct((T_LAT, GQA_Q_HEADS * HD), BF16),
        scratch_shapes=[pltpu.VMEM((S, HD), BF16), pltpu.VMEM((L, HD), BF16)],
        compiler_params=_cp(("arbitrary",) * 3),
        name="gqa_attention",
    )(p, p, p, p, p, cosf, sinf, cosf, sinf, q_gain.reshape(1, HD), k_gain.reshape(1, HD))


def _ctx_attn_kernel(p_ref, qg_ref, kg_ref, o_ref):
    def col(c):
        return p_ref[:, c * HD:(c + 1) * HD]

    for h in range(NA_HEADS):
        s = _dot_nt(col(h), col(NA_HEADS + h)) * ATT_SCALE
        o_ref[:, h * HD:(h + 1) * HD] = _softmax_av([(s, col(2 * NA_HEADS + h))]).astype(BF16)
    for n in range(GQA_KV_HEADS):
        kn = _rms_head(col(GQA_KCOL + n).astype(F32), kg_ref[...]).astype(BF16)
        v = col(GQA_VCOL + n)
        for g in range(GQA_GROUP):
            h = n * GQA_GROUP + g
            qn = _rms_head(col(GQA_QCOL + h).astype(F32), qg_ref[...]).astype(BF16)
            s = _dot_nt(qn, kn) * ATT_SCALE
            o_ref[:, (NA_HEADS + h) * HD:(NA_HEADS + h + 1) * HD] = _softmax_av([(s, v)]).astype(BF16)


def ctx_attention(p, q_gain, k_gain):
    cb = S // L
    return pl.pallas_call(
        _ctx_attn_kernel,
        grid=(B,),
        in_specs=[pl.BlockSpec((L, ATTN_IN), lambda b: (B * cb + b, 0)),
                  pl.BlockSpec((1, HD), lambda b: (0, 0)),
                  pl.BlockSpec((1, HD), lambda b: (0, 0))],
        out_specs=pl.BlockSpec((L, D), lambda b: (b, 0)),
        out_shape=jax.ShapeDtypeStruct((T_CTX, D), BF16),
        compiler_params=_cp(("arbitrary",)),
        name="ctx_attention",
    )(p, q_gain.reshape(1, HD), k_gain.reshape(1, HD))


def _log_sigmoid(x):
    return -(jnp.maximum(-x, 0.0) + jnp.log1p(jnp.exp(-jnp.abs(x))))


def _dot_hi(a, b):
    return jnp.dot(a, b, precision=lax.Precision.HIGHEST, preferred_element_type=F32)


def _gate_kernel(h_ref, wg_ref, wgt_ref, b_ref, bt_ref, lt_ref, ut_ref, col_ref, row_ref):
    nh = ML_HEADS
    hx = h_ref[...]
    g = jnp.dot(hx, wg_ref[...], preferred_element_type=F32) + b_ref[...]
    gt = _dot_nt(wgt_ref[...], hx) + bt_ref[...]
    li = g[:, 0:2 * nh]
    lf = _log_sigmoid(g[:, 2 * nh:4 * nh])
    lit = gt[0:2 * nh]
    lft = _log_sigmoid(gt[2 * nh:4 * nh])
    lt = lt_ref[...]
    ut = ut_ref[...]
    lane = lax.broadcasted_iota(I32, lf.shape, 1)
    bc = jnp.where(lane < nh, _dot_hi(lt, lf), _dot_hi(ut, lf))
    tot = jnp.sum(lf, axis=0, keepdims=True)
    aend = tot - bc + li
    col_ref[...] = jnp.concatenate([bc, aend, jnp.zeros((ML_CH, LANE - 4 * nh), F32)], axis=1)
    sub = lax.broadcasted_iota(I32, lft.shape, 0)
    bct = jnp.where(sub < nh, _dot_hi(lft, ut), _dot_hi(lft, lt))
    tott = jnp.sum(lft, axis=1, keepdims=True)
    gtr = lit - bct
    row_ref[0] = jnp.concatenate([bct, gtr, tott + gtr, jnp.broadcast_to(tott, bct.shape)], axis=0)


def mlstm_gates(hx, wg, gate_b):
    nh = ML_HEADS
    n_ch = T_ALL // ML_CH
    wg_pad = jnp.zeros((D, LANE), BF16).at[:, :4 * nh].set(wg.astype(BF16))
    b_pad = jnp.zeros((1, LANE), F32).at[0, :4 * nh].set(gate_b.reshape(-1))
    wgt = wg.astype(BF16).T
    bt = gate_b.reshape(4 * nh, 1).astype(F32)
    lt = jnp.tril(jnp.ones((ML_CH, ML_CH), F32))
    ut = jnp.triu(jnp.ones((ML_CH, ML_CH), F32))
    col, row = pl.pallas_call(
        _gate_kernel,
        grid=(n_ch,),
        in_specs=[pl.BlockSpec((ML_CH, D), lambda i: (i, 0)),
                  pl.BlockSpec((D, LANE), lambda i: (0, 0)),
                  pl.BlockSpec((4 * nh, D), lambda i: (0, 0)),
                  pl.BlockSpec((1, LANE), lambda i: (0, 0)),
                  pl.BlockSpec((4 * nh, 1), lambda i: (0, 0)),
                  pl.BlockSpec((ML_CH, ML_CH), lambda i: (0, 0)),
                  pl.BlockSpec((ML_CH, ML_CH), lambda i: (0, 0))],
        out_specs=[pl.BlockSpec((ML_CH, LANE), lambda i: (i, 0)),
                   pl.BlockSpec((1, 8 * nh, ML_CH), lambda i: (i, 0, 0))],
        out_shape=[jax.ShapeDtypeStruct((T_ALL, LANE), F32),
                   jax.ShapeDtypeStruct((n_ch, 8 * nh, ML_CH), F32)],
        compiler_params=_cp(("arbitrary",)),
        name="mlstm_gates",
    )(hx, wg_pad, wgt, b_pad, bt, lt, ut)
    col3 = col[:, :4 * nh].reshape(T_ALL, 2, 2 * nh).transpose(2, 0, 1)
    col3 = jnp.pad(col3, ((0, 0), (0, 0), (0, 6)))
    row4 = row.reshape(n_ch, 4, 2 * nh, ML_CH).transpose(2, 0, 1, 3)
    row4 = jnp.pad(row4, ((0, 0), (0, 0), (0, 4), (0, 0)))
    return col3, row4


def _mlstm_step(d, hh, q_ref, k_ref, v_ref, col_ref, row_ref, o_ref, c_ref, n_ref, m_ref):
    sl = d * ML_HPS + hh
    q = q_ref[:, hh * ML_QK:(hh + 1) * ML_QK]
    kf = k_ref[:, hh * ML_QK:(hh + 1) * ML_QK].astype(F32) * ML_KSCALE
    kb = kf.astype(BF16)
    v = v_ref[:, hh * ML_V:(hh + 1) * ML_V]
    col = col_ref[hh]
    row = row_ref[hh, 0]
    bc_col = col[:, 0:1]
    aend_col = col[:, 1:2]
    g_row = row[1:2]
    aend_row = row[2:3]
    btot = row[3:4, 0:1]
    m_st = m_ref[sl]
    c_st = c_ref[sl]
    n_st = n_ref[sl]
    m_new = jnp.maximum(btot + m_st, jnp.max(aend_row, axis=1, keepdims=True))

    r = lax.broadcasted_iota(I32, (ML_CH, ML_CH), 0)
    c = lax.broadcasted_iota(I32, (ML_CH, ML_CH), 1)
    causal = (r >= c) if d == 0 else (r <= c)
    d_mat = jnp.where(causal, bc_col + g_row, -jnp.inf)
    m_row = jnp.maximum(bc_col + m_st, jnp.max(d_mat, axis=1, keepdims=True))
    w_inter = jnp.exp(bc_col + m_st - m_row)
    s_mat = _dot_nt(q, kb) * jnp.exp(d_mat - m_row)
    num = (w_inter * jnp.dot(q, c_st.astype(BF16), preferred_element_type=F32)
           + jnp.dot(s_mat.astype(BF16), v, preferred_element_type=F32))
    den = (w_inter * jnp.sum(q.astype(F32) * n_st, axis=1, keepdims=True)
           + jnp.sum(s_mat, axis=1, keepdims=True))
    o_ref[:, hh * ML_V:(hh + 1) * ML_V] = num / jnp.maximum(jnp.abs(den), jnp.exp(-m_row))

    w_end_col = jnp.exp(aend_col - m_new)
    w_end_row = jnp.exp(aend_row - m_new)
    decay = jnp.exp(btot + m_st - m_new)
    kw = (kf * w_end_col).astype(BF16)
    c_ref[sl] = decay * c_st + lax.dot_general(kw, v, (((0,), (0,)), ((), ())), preferred_element_type=F32)
    w8 = jnp.broadcast_to(w_end_row, (8, ML_CH)).astype(BF16)
    n_ref[sl] = decay * n_st + jnp.dot(w8, kb, preferred_element_type=F32)[0:1]
    m_ref[sl] = m_new


def _mlstm_kernel(qf, kf, vf, colf, rowf, qb, kb, vb, colb, rowb, of, ob, c_ref, n_ref, m_ref):
    @pl.when(pl.program_id(2) == 0)
    def _():
        c_ref[...] = jnp.zeros_like(c_ref)
        n_ref[...] = jnp.zeros_like(n_ref)
        m_ref[...] = jnp.zeros_like(m_ref)

    for hh in range(ML_HPS):
        _mlstm_step(0, hh, qf, kf, vf, colf, rowf, of, c_ref, n_ref, m_ref)
        _mlstm_step(1, hh, qb, kb, vb, colb, rowb, ob, c_ref, n_ref, m_ref)


def mlstm_scan(p, col3, row4):
    n_lat = S // ML_CH
    steps = n_lat + 1
    lat_blocks = T_LAT // ML_CH
    hps = ML_HPS
    n_groups = ML_HEADS // hps
    kcol = (ML_HEADS * ML_QK) // (hps * ML_QK)
    vcol = (2 * ML_HEADS * ML_QK) // (hps * ML_V)

    def chunk(b, d, st):
        c = (st - 1) if d == 0 else (n_lat - st)
        return jnp.where(st == 0, lat_blocks + b, b * n_lat + c)

    def out_chunk(b, d, st):
        s1 = jnp.maximum(st, 1)
        return b * n_lat + ((s1 - 1) if d == 0 else (n_lat - s1))

    def dir_specs(d):
        return [pl.BlockSpec((ML_CH, hps * ML_QK), lambda b, h, s: (chunk(b, d, s), h)),
                pl.BlockSpec((ML_CH, hps * ML_QK), lambda b, h, s: (chunk(b, d, s), kcol + h)),
                pl.BlockSpec((ML_CH, hps * ML_V), lambda b, h, s: (chunk(b, d, s), vcol + h)),
                pl.BlockSpec((hps, ML_CH, 8), lambda b, h, s: (d * n_groups + h, chunk(b, d, s), 0)),
                pl.BlockSpec((hps, 1, 8, ML_CH), lambda b, h, s: (d * n_groups + h, chunk(b, d, s), 0, 0))]

    return pl.pallas_call(
        _mlstm_kernel,
        grid=(B, n_groups, steps),
        in_specs=dir_specs(0) + dir_specs(1),
        out_specs=[pl.BlockSpec((ML_CH, hps * ML_V), lambda b, h, s: (out_chunk(b, 0, s), h)),
                   pl.BlockSpec((ML_CH, hps * ML_V), lambda b, h, s: (out_chunk(b, 1, s), h))],
        out_shape=[jax.ShapeDtypeStruct((T_LAT, ML_HEADS * ML_V), F32),
                   jax.ShapeDtypeStruct((T_LAT, ML_HEADS * ML_V), F32)],
        scratch_shapes=[pltpu.VMEM((2 * hps, ML_QK, ML_V), F32), pltpu.VMEM((2 * hps, 1, ML_QK), F32),
                        pltpu.VMEM((2 * hps, 1, 1), F32)],
        compiler_params=_cp(("arbitrary",) * 3),
        name="mlstm_scan",
    )(p, p, p, col3, row4, p, p, p, col3, row4)


def _readout_kernel(hf_ref, hb_ref, o_ref, g_ref, a_ref):
    hs = hf_ref[...] + hb_ref[...]
    for h in range(ML_HEADS):
        sl = slice(h * ML_V, (h + 1) * ML_V)
        x = hs[:, sl]
        hn = x * lax.rsqrt(jnp.mean(x * x, axis=-1, keepdims=True) + EPS) * g_ref[:, sl]
        a_ref[:, sl] = (hn * jax.nn.sigmoid(o_ref[:, sl].astype(F32))).astype(BF16)


def mlstm_readout(hdir, p, head_gain):
    tm = 256
    ocol = (2 * ML_HEADS * ML_QK + ML_HEADS * ML_V) // D
    return pl.pallas_call(
        _readout_kernel,
        grid=(T_LAT // tm,),
        in_specs=[pl.BlockSpec((tm, D), lambda i: (i, 0)),
                  pl.BlockSpec((tm, D), lambda i: (i, 0)),
                  pl.BlockSpec((tm, D), lambda i: (i, ocol)),
                  pl.BlockSpec((1, D), lambda i: (0, 0))],
        out_specs=pl.BlockSpec((tm, D), lambda i: (i, 0)),
        out_shape=jax.ShapeDtypeStruct((T_LAT, D), BF16),
        compiler_params=_cp(("arbitrary",)),
        name="mlstm_readout",
    )(hdir[0], hdir[1], p, head_gain.reshape(1, D))


ROUTER_TM = 512


def _router_kernel(h_ref, w_ref, rb_ref, erow_ref, tri_ref, eidx_ref, wts_ref, pos_ref, cnt_ref, carry_ref):
    ng = N_GROUPS
    epg = N_EXPERTS // N_GROUPS
    tm = ROUTER_TM
    ninf = -jnp.inf

    @pl.when(pl.program_id(0) == 0)
    def _():
        carry_ref[...] = jnp.zeros_like(carry_ref)

    s = jax.nn.sigmoid(_dot_nt(w_ref[...], h_ref[...]))
    ssel = s + rb_ref[...]
    sraw = [s[ng * j:ng * (j + 1)] for j in range(epg)]
    slab = [ssel[ng * j:ng * (j + 1)] for j in range(epg)]
    m1 = functools.reduce(jnp.maximum, slab)
    jfirst = functools.reduce(jnp.minimum, [jnp.where(slab[j] == m1, j, epg) for j in range(epg)])
    m2 = functools.reduce(jnp.maximum, [jnp.where(jfirst == j, ninf, slab[j]) for j in range(epg)])
    gs = m1 + m2
    giota = lax.broadcasted_iota(I32, (ng, tm), 0)
    gsel = jnp.zeros((ng, tm), F32)
    for _ in range(TOPK_GROUPS):
        mx = jnp.max(gs, axis=0, keepdims=True)
        gi = jnp.min(jnp.where(gs == mx, giota, ng), axis=0, keepdims=True)
        hit = giota == gi
        gsel = jnp.where(hit, 1.0, gsel)
        gs = jnp.where(hit, ninf, gs)
    msl = [jnp.where(gsel > 0.0, slab[j], ninf) for j in range(epg)]
    eid = [giota * epg + j for j in range(epg)]
    selm = [jnp.zeros((ng, tm), F32) for _ in range(epg)]
    e_list, w_list = [], []
    for _ in range(TOP_K):
        mx = jnp.max(functools.reduce(jnp.maximum, msl), axis=0, keepdims=True)
        cand = functools.reduce(jnp.minimum, [jnp.where(msl[j] == mx, eid[j], N_EXPERTS) for j in range(epg)])
        esel = jnp.min(cand, axis=0, keepdims=True)
        hits = [eid[j] == esel for j in range(epg)]
        wk = functools.reduce(lambda a, b: a + b, [jnp.where(hits[j], sraw[j], 0.0) for j in range(epg)])
        w_list.append(jnp.sum(wk, axis=0, keepdims=True))
        e_list.append(esel)
        msl = [jnp.where(hits[j], ninf, msl[j]) for j in range(epg)]
        selm = [jnp.where(hits[j], 1.0, selm[j]) for j in range(epg)]
    wsum = functools.reduce(lambda a, b: a + b, w_list)
    wts_ref[...] = jnp.concatenate([w / wsum * ROUTED_SCALE for w in w_list], axis=0)
    eidx_ref[...] = jnp.concatenate(e_list, axis=0)
    sel = jnp.concatenate(selm, axis=0)
    carry = carry_ref[...]
    posfull = jnp.dot(sel.astype(BF16), tri_ref[...], preferred_element_type=F32) + carry
    erow = erow_ref[...]
    pos = [jnp.sum(jnp.where(erow == e, posfull, 0.0), axis=0, keepdims=True) for e in e_list]
    pos_ref[...] = jnp.concatenate(pos, axis=0).astype(I32)
    carry = carry + jnp.sum(sel, axis=1, keepdims=True)
    carry_ref[...] = carry
    cnt_ref[...] = carry


def moe_router(hx, router_w, router_b, n_tok):
    tm = ROUTER_TM
    epg = N_EXPERTS // N_GROUPS
    perm = (jnp.arange(N_EXPERTS) % N_GROUPS) * epg + jnp.arange(N_EXPERTS) // N_GROUPS
    w_t = router_w.astype(BF16).T[perm]
    rb = router_b.astype(F32)[perm].reshape(N_EXPERTS, 1)
    erow = perm.astype(I32).reshape(N_EXPERTS, 1)
    tri = jnp.triu(jnp.ones((tm, tm), BF16), 1)
    eidx, wts, pos, counts = pl.pallas_call(
        _router_kernel,
        grid=(n_tok // tm,),
        in_specs=[pl.BlockSpec((tm, D), lambda i: (i, 0)),
                  pl.BlockSpec((N_EXPERTS, D), lambda i: (0, 0)),
                  pl.BlockSpec((N_EXPERTS, 1), lambda i: (0, 0)),
                  pl.BlockSpec((N_EXPERTS, 1), lambda i: (0, 0)),
                  pl.BlockSpec((tm, tm), lambda i: (0, 0))],
        out_specs=[pl.BlockSpec((TOP_K, tm), lambda i: (0, i)),
                   pl.BlockSpec((TOP_K, tm), lambda i: (0, i)),
                   pl.BlockSpec((TOP_K, tm), lambda i: (0, i)),
                   pl.BlockSpec((N_EXPERTS, 1), lambda i: (0, 0))],
        out_shape=[jax.ShapeDtypeStruct((TOP_K, n_tok), I32),
                   jax.ShapeDtypeStruct((TOP_K, n_tok), F32),
                   jax.ShapeDtypeStruct((TOP_K, n_tok), I32),
                   jax.ShapeDtypeStruct((N_EXPERTS, 1), F32)],
        scratch_shapes=[pltpu.VMEM((N_EXPERTS, 1), F32)],
        compiler_params=_cp(("arbitrary",)),
        name="moe_router",
    )(hx, w_t, rb, erow, tri)
    return eidx, wts, pos, counts.reshape(N_EXPERTS)[perm]


DISPATCH_TM = 512


def _dispatch_kernel(slot_ref, hx_ref, xs_hbm, sem):
    def issue(t, carry):
        src = hx_ref.at[pl.ds(t, 1), :]
        for k in range(TOP_K):
            pltpu.make_async_copy(src, xs_hbm.at[pl.ds(slot_ref[k, t], 1), :], sem).start()
        return carry

    lax.fori_loop(0, DISPATCH_TM, issue, 0)
    for _ in range(TOP_K):
        pltpu.make_async_copy(hx_ref, xs_hbm.at[pl.ds(0, DISPATCH_TM), :], sem).wait()


def moe_dispatch(slots, hx_packed, n_tok, n_rows):
    tm = DISPATCH_TM
    return pl.pallas_call(
        _dispatch_kernel,
        grid=(n_tok // tm,),
        in_specs=[pl.BlockSpec((TOP_K, tm), lambda i: (0, i), memory_space=pltpu.SMEM),
                  pl.BlockSpec((tm, PK_W), lambda i: (i, 0))],
        out_specs=pl.BlockSpec(memory_space=pl.ANY),
        out_shape=jax.ShapeDtypeStruct((n_rows, PK_W), U32),
        scratch_shapes=[pltpu.SemaphoreType.DMA(())],
        compiler_params=_cp(("arbitrary",)),
        name="moe_dispatch",
    )(slots, hx_packed)


def _expert_kernel(be_ref, valid_ref, nused_ref, xs_ref, w1_ref, w3_ref, w2_ref, y_ref, w1b, w3b, w2b, xb):
    i = pl.program_id(0)
    bm = EXP_BM

    @pl.when(i < nused_ref[0])
    def _():
        prev = be_ref[jnp.maximum(i - 1, 0)]

        @pl.when((i == 0) | (be_ref[i] != prev))
        def _():
            w1b[...] = w1_ref[0, 0].astype(BF16)
            w3b[...] = w3_ref[0, 0].astype(BF16)
            w2b[...] = w2_ref[0, 0].astype(BF16)

        rows = lax.broadcasted_iota(I32, (bm, PK_W), 0)
        lo, hi = _unpack_bf16_pairs(jnp.where(rows < valid_ref[i], xs_ref[...], jnp.uint32(0)))
        xb[:, :PK_W] = lo.astype(BF16)
        xb[:, PK_W:] = hi.astype(BF16)
        x = xb[...]
        h1 = jnp.dot(x, w1b[...], preferred_element_type=F32)
        h3 = jnp.dot(x, w3b[...], preferred_element_type=F32)
        a = (h1 * jax.nn.sigmoid(h1) * h3).astype(BF16)
        y = jnp.dot(a, w2b[...], preferred_element_type=F32)
        y_ref[...] = _pack_bf16_pairs(y.astype(BF16))


def moe_experts(block_e, valid, n_used, xs, w1, w3, w2, layer, n_blocks):
    bm = EXP_BM

    def blk(i, be, va, nu):
        return jnp.minimum(i, nu[0] - 1)

    grid_spec = pltpu.PrefetchScalarGridSpec(
        num_scalar_prefetch=3,
        grid=(n_blocks,),
        in_specs=[pl.BlockSpec((bm, PK_W), lambda i, be, va, nu: (blk(i, be, va, nu), 0)),
                  pl.BlockSpec((1, 1, D, EXPERT_DIM), lambda i, be, va, nu: (layer, be[blk(i, be, va, nu)], 0, 0)),
                  pl.BlockSpec((1, 1, D, EXPERT_DIM), lambda i, be, va, nu: (layer, be[blk(i, be, va, nu)], 0, 0)),
                  pl.BlockSpec((1, 1, EXPERT_DIM, D), lambda i, be, va, nu: (layer, be[blk(i, be, va, nu)], 0, 0))],
        out_specs=pl.BlockSpec((bm, PK_W), lambda i, be, va, nu: (blk(i, be, va, nu), 0)),
        scratch_shapes=[pltpu.VMEM((D, EXPERT_DIM), BF16), pltpu.VMEM((D, EXPERT_DIM), BF16),
                        pltpu.VMEM((EXPERT_DIM, D), BF16), pltpu.VMEM((bm, D), BF16)],
    )
    return pl.pallas_call(
        _expert_kernel,
        grid_spec=grid_spec,
        out_shape=jax.ShapeDtypeStruct((n_blocks * bm, PK_W), U32),
        compiler_params=_cp(("arbitrary",), vmem=56 * 1024 * 1024),
        name="moe_experts",
    )(block_e, valid, n_used, xs, w1, w3, w2)


def _shared_kernel(x_ref, w1_ref, w3_ref, w2_ref, o_ref):
    x = x_ref[...]
    h1 = jnp.dot(x, w1_ref[...], preferred_element_type=F32)
    h3 = jnp.dot(x, w3_ref[...], preferred_element_type=F32)
    a = (h1 * jax.nn.sigmoid(h1) * h3).astype(BF16)
    o_ref[...] = jnp.dot(a, w2_ref[...], preferred_element_type=F32)


def shared_expert(hx, w1, w3, w2, n_tok):
    tm = 512
    return pl.pallas_call(
        _shared_kernel,
        grid=(n_tok // tm,),
        in_specs=[pl.BlockSpec((tm, D), lambda i: (i, 0)),
                  pl.BlockSpec((D, EXPERT_DIM), lambda i: (0, 0)),
                  pl.BlockSpec((D, EXPERT_DIM), lambda i: (0, 0)),
                  pl.BlockSpec((EXPERT_DIM, D), lambda i: (0, 0))],
        out_specs=pl.BlockSpec((tm, D), lambda i: (i, 0)),
        out_shape=jax.ShapeDtypeStruct((n_tok, D), F32),
        compiler_params=_cp(("arbitrary",)),
        name="shared_expert",
    )(hx, w1, w3, w2)


COMBINE_TM = 128


def _combine_kernel(slot_ref, w_ref, sh_ref, x_ref, mod_ref, y_hbm, o_ref, buf, sem):
    tm = COMBINE_TM

    def issue(t, carry):
        for k in range(TOP_K):
            src = y_hbm.at[pl.ds(slot_ref[k, t], 1), :]
            pltpu.make_async_copy(src, buf.at[pl.ds(k * tm + t, 1), :], sem).start()
        return carry

    lax.fori_loop(0, tm, issue, 0)
    pltpu.make_async_copy(y_hbm.at[pl.ds(0, TOP_K * tm), :], buf, sem).wait()

    w = w_ref[...]
    acc_lo = acc_hi = None
    for k in range(TOP_K):
        lo, hi = _unpack_bf16_pairs(buf[k * tm:(k + 1) * tm, :])
        wk = w[:, k:k + 1]
        acc_lo = wk * lo if acc_lo is None else acc_lo + wk * lo
        acc_hi = wk * hi if acc_hi is None else acc_hi + wk * hi
    gate = mod_ref[0][5:6]
    o_ref[:, :PK_W] = x_ref[:, :PK_W] + gate[:, :PK_W] * (sh_ref[:, :PK_W] + acc_lo)
    o_ref[:, PK_W:] = x_ref[:, PK_W:] + gate[:, PK_W:] * (sh_ref[:, PK_W:] + acc_hi)


def moe_combine(slots, wts_tok, shared, x, mod, y, n_tok):
    tm = COMBINE_TM
    return pl.pallas_call(
        _combine_kernel,
        grid=(n_tok // tm,),
        in_specs=[pl.BlockSpec((TOP_K, tm), lambda i: (0, i), memory_space=pltpu.SMEM),
                  pl.BlockSpec((tm, TOP_K), lambda i: (i, 0)),
                  pl.BlockSpec((tm, D), lambda i: (i, 0)),
                  pl.BlockSpec((tm, D), lambda i: (i, 0)),
                  pl.BlockSpec((1, 6, D), lambda i: (_mod_row(i * tm), 0, 0)),
                  pl.BlockSpec(memory_space=pl.ANY)],
        out_specs=pl.BlockSpec((tm, D), lambda i: (i, 0)),
        out_shape=jax.ShapeDtypeStruct((n_tok, D), F32),
        scratch_shapes=[pltpu.VMEM((TOP_K * tm, PK_W), U32), pltpu.SemaphoreType.DMA(())],
        compiler_params=_cp(("arbitrary",)),
        name="moe_combine",
    )(slots, wts_tok, shared, x, mod, y)


def _lookup(table, idx):
    e = jnp.arange(table.shape[0], dtype=I32).reshape((-1,) + (1,) * idx.ndim)
    return jnp.sum(jnp.where(idx[None] == e, table.reshape(e.shape), 0), axis=0)


def moe_layer(x, mod, norm_gain, router_w, router_b, exp_w1, exp_w3, exp_w2, sw1, sw3, sw2, layer, n_tok):
    bm = EXP_BM
    n_blocks = -(-n_tok * TOP_K // bm) + N_EXPERTS
    hx, hx_packed = norm_mod(x, norm_gain, mod, 3, n_tok, pack=True)
    eidx, wts, pos, counts = moe_router(hx, router_w, router_b, n_tok)
    shared = shared_expert(hx, sw1.astype(BF16), sw3.astype(BF16), sw2.astype(BF16), n_tok)
    cnt = counts.astype(I32)
    padded = (cnt + bm - 1) // bm * bm
    pad_end = jnp.cumsum(padded)
    pad_start = pad_end - padded
    slots = _lookup(pad_start, eidx) + pos
    blk_row = jnp.arange(n_blocks, dtype=I32) * bm
    block_e = jnp.minimum(jnp.sum((pad_end[:, None] <= blk_row[None, :]).astype(I32), axis=0), N_EXPERTS - 1)
    valid = jnp.clip(_lookup(cnt, block_e) - (blk_row - _lookup(pad_start, block_e)), 0, bm).astype(I32)
    n_used = (pad_end[-1:] // bm).astype(I32)
    xs = moe_dispatch(slots, hx_packed, n_tok, n_blocks * bm)
    y = moe_experts(block_e, valid, n_used, xs, exp_w1, exp_w3, exp_w2, layer, n_blocks)
    return moe_combine(slots, wts.T, shared, x, mod, y, n_tok)


def _rope_tables():
    t = jnp.arange(S, dtype=I32)
    row = (t // GRID_W).astype(F32)
    col = (t % GRID_W).astype(F32)
    n_freq = HD // 4
    inv_freq = ROPE_THETA ** (-jnp.arange(n_freq, dtype=F32) / n_freq)
    ang = jnp.concatenate([row[:, None] * inv_freq, col[:, None] * inv_freq], axis=-1)
    cosf = jnp.repeat(jnp.cos(ang), 2, axis=-1)
    sinf = jnp.stack([-jnp.sin(ang), jnp.sin(ang)], axis=-1).reshape(S, HD)
    return cosf, sinf


def kernel(x, c, ctx, c_ctx, ada_w, ada_b, norm_mix, norm_ffn, attn_w_in, attn_w_out, attn_rpb, attn_q_gain,
           attn_k_gain, ml_w_in, ml_w_out, ml_gate_b, ml_head_gain, router_w, router_b, exp_w1, exp_w3, exp_w2,
           sh_w1, sh_w3, sh_w2, final_norm_gain):
    depth = ada_w.shape[0]
    xa = jnp.concatenate([x.reshape(T_LAT, D), ctx.reshape(T_CTX, D)], axis=0)
    cvec = jnp.concatenate([c, c_ctx[None], jnp.zeros((8 - B - 1, D), F32)], axis=0)
    mod_all = ada_ln(cvec, ada_w, ada_b).reshape(depth, 8, 6, D)
    cosf, sinf = _rope_tables()

    mod = mod_all[0]
    hx = norm_mod(xa, norm_mix[0], mod, 0, T_ALL, pack=False)
    p = matmul(hx, attn_w_in[0].astype(BF16), BF16)
    o_na = neighborhood_attention(p, na_bias_table(attn_rpb[0]))
    o_gqa = gqa_attention(p, cosf, sinf, attn_q_gain[0], attn_k_gain[0])
    o_ctx = ctx_attention(p, attn_q_gain[0], attn_k_gain[0])
    o_all = jnp.concatenate([jnp.concatenate([o_na, o_gqa], axis=1), o_ctx], axis=0)
    xa = matmul_gated_residual(o_all, attn_w_out[0].astype(BF16), xa, mod, 2)
    xa = moe_layer(xa, mod, norm_ffn[0], router_w[0], router_b[0], exp_w1, exp_w3, exp_w2,
                   sh_w1[0], sh_w3[0], sh_w2[0], 0, T_ALL)

    mod = mod_all[1]
    hx = norm_mod(xa, norm_mix[1], mod, 0, T_ALL, pack=False)
    w_in = ml_w_in[0]
    p = matmul(hx, w_in[:, :ML_MAIN].astype(BF16), BF16)
    col3, row4 = mlstm_gates(hx, w_in[:, ML_MAIN:], ml_gate_b[0])
    hdir = mlstm_scan(p, col3, row4)
    a = mlstm_readout(hdir, p, ml_head_gain[0])
    xl = matmul_gated_residual(a, ml_w_out[0].astype(BF16), xa, mod, 2)
    xl = moe_layer(xl, mod, norm_ffn[1], router_w[1], router_b[1], exp_w1, exp_w3, exp_w2,
                   sh_w1[1], sh_w3[1], sh_w2[1], 1, T_LAT)
    return final_norm(xl, final_norm_gain).reshape(B, S, D)
```

```python
import functools

import jax
import jax.numpy as jnp
from jax import lax
from jax.experimental import pallas as pl
from jax.experimental.pallas import tpu as pltpu

F32 = jnp.float32
BF16 = jnp.bfloat16
I32 = jnp.int32
U32 = jnp.uint32

D = 2048
B = 4
S = 4096
L = 256
T_LAT = B * S
T_CTX = B * L
T_ALL = T_LAT + T_CTX
GRID_W = 64
ROWS = S // GRID_W
HD = 128
NA_HEADS = 8
NA_WIN_ROWS = 8
NA_WIN_COLS = 16
GQA_Q_HEADS = 8
GQA_KV_HEADS = 2
GQA_GROUP = 4
ROPE_THETA = 10000.0
ATTN_IN = 4608
ML_HEADS = 8
ML_V = 256
ML_QK = 128
ML_MAIN = 6144
N_EXPERTS = 64
TOP_K = 8
N_GROUPS = 8
TOPK_GROUPS = 4
EXPERT_DIM = 512
ROUTED_SCALE = 2.5
EPS = 1e-6
NEG_INF = -1e30
ATT_SCALE = HD ** -0.5
LOG2E = 1.4426950408889634
ML_KSCALE = ML_QK ** -0.5

LANE = 128
NA_QROWS = 4
NA_SLAB = NA_QROWS + NA_WIN_ROWS - 1
NA_QB = NA_QROWS * GRID_W
NA_KB = NA_SLAB * GRID_W
ML_CH = 256
ML_HPS = 2
EXP_BM = 512
PK_W = D // 2
VMEM_LIMIT = 48 * 1024 * 1024


def _cp(sem, vmem=VMEM_LIMIT):
    return pltpu.CompilerParams(dimension_semantics=sem, vmem_limit_bytes=vmem)


def _pack_bf16_pairs(xb):
    u = pltpu.bitcast(xb.astype(F32), U32)
    return (u[:, PK_W:] & jnp.uint32(0xFFFF0000)) | (u[:, :PK_W] >> 16)


def _unpack_bf16_pairs(u):
    return pltpu.bitcast(u << 16, F32), pltpu.bitcast(u & jnp.uint32(0xFFFF0000), F32)


def _mod_row(start_row):
    return jnp.where(start_row < T_LAT, start_row // S, B)


def _ada_kernel(c_ref, w_ref, b_ref, o_ref):
    c = c_ref[...]
    a = (c * jax.nn.sigmoid(c)).astype(BF16)
    w = w_ref[0].astype(BF16)
    o_ref[0] = jnp.dot(a, w, preferred_element_type=F32) + b_ref[0]


def ada_ln(cvec, ada_w, ada_b):
    depth = ada_w.shape[0]
    n = ada_w.shape[2]
    tn = 1024
    return pl.pallas_call(
        _ada_kernel,
        grid=(depth, n // tn),
        in_specs=[pl.BlockSpec((8, D), lambda l, j: (0, 0)),
                  pl.BlockSpec((1, D, tn), lambda l, j: (l, 0, j)),
                  pl.BlockSpec((1, 1, tn), lambda l, j: (l, 0, j))],
        out_specs=pl.BlockSpec((1, 8, tn), lambda l, j: (l, 0, j)),
        out_shape=jax.ShapeDtypeStruct((depth, 8, n), F32),
        compiler_params=_cp(("arbitrary", "arbitrary")),
        name="ada_ln",
    )(cvec, ada_w, ada_b.reshape(depth, 1, n))


def _norm_mod_kernel(x_ref, g_ref, mod_ref, *out_refs, base, pack):
    x = x_ref[...]
    y = x * lax.rsqrt(jnp.mean(x * x, axis=-1, keepdims=True) + EPS) * g_ref[...]
    m = mod_ref[0]
    h = y * (1.0 + m[base + 1:base + 2]) + m[base:base + 1]
    hb = h.astype(BF16)
    out_refs[0][...] = hb
    if pack:
        out_refs[1][...] = _pack_bf16_pairs(hb)


def norm_mod(x, gain, mod, base, n_rows, pack):
    tm = 256
    out_shape = [jax.ShapeDtypeStruct((n_rows, D), BF16)]
    out_specs = [pl.BlockSpec((tm, D), lambda i: (i, 0))]
    if pack:
        out_shape.append(jax.ShapeDtypeStruct((n_rows, PK_W), U32))
        out_specs.append(pl.BlockSpec((tm, PK_W), lambda i: (i, 0)))
    res = pl.pallas_call(
        functools.partial(_norm_mod_kernel, base=base, pack=pack),
        grid=(n_rows // tm,),
        in_specs=[pl.BlockSpec((tm, D), lambda i: (i, 0)),
                  pl.BlockSpec((1, D), lambda i: (0, 0)),
                  pl.BlockSpec((1, 6, D), lambda i: (_mod_row(i * tm), 0, 0))],
        out_specs=out_specs,
        out_shape=out_shape,
        compiler_params=_cp(("arbitrary",)),
        name="norm_mod",
    )(x, gain.reshape(1, D), mod)
    return res if pack else res[0]


def _final_norm_kernel(x_ref, g_ref, o_ref):
    x = x_ref[...]
    o_ref[...] = x * lax.rsqrt(jnp.mean(x * x, axis=-1, keepdims=True) + EPS) * g_ref[...]


def final_norm(x, gain):
    tm = 256
    return pl.pallas_call(
        _final_norm_kernel,
        grid=(T_LAT // tm,),
        in_specs=[pl.BlockSpec((tm, D), lambda i: (i, 0)),
                  pl.BlockSpec((1, D), lambda i: (0, 0))],
        out_specs=pl.BlockSpec((tm, D), lambda i: (i, 0)),
        out_shape=jax.ShapeDtypeStruct((T_LAT, D), F32),
        compiler_params=_cp(("arbitrary",)),
        name="final_norm",
    )(x, gain.reshape(1, D))


def _mm_kernel(a_ref, w_ref, o_ref):
    o_ref[...] = jnp.dot(a_ref[...], w_ref[...], preferred_element_type=F32).astype(o_ref.dtype)


def matmul(a, w, out_dtype, tm=1024, tn=512):
    m, k = a.shape
    n = w.shape[1]
    return pl.pallas_call(
        _mm_kernel,
        grid=(m // tm, n // tn),
        in_specs=[pl.BlockSpec((tm, k), lambda i, j: (i, 0)),
                  pl.BlockSpec((k, tn), lambda i, j: (0, j))],
        out_specs=pl.BlockSpec((tm, tn), lambda i, j: (i, j)),
        out_shape=jax.ShapeDtypeStruct((m, n), out_dtype),
        compiler_params=_cp(("arbitrary", "arbitrary")),
        name="matmul",
    )(a, w)


def _mm_res_kernel(a_ref, w_ref, x_ref, mod_ref, o_ref, *, slot):
    acc = jnp.dot(a_ref[...], w_ref[...], preferred_element_type=F32)
    o_ref[...] = x_ref[...] + mod_ref[0][slot:slot + 1] * acc


def matmul_gated_residual(a, w, x, mod, slot, tm=1024, tn=512):
    m, k = a.shape
    n = w.shape[1]
    return pl.pallas_call(
        functools.partial(_mm_res_kernel, slot=slot),
        grid=(m // tm, n // tn),
        in_specs=[pl.BlockSpec((tm, k), lambda i, j: (i, 0)),
                  pl.BlockSpec((k, tn), lambda i, j: (0, j)),
                  pl.BlockSpec((tm, tn), lambda i, j: (i, j)),
                  pl.BlockSpec((1, 6, tn), lambda i, j: (_mod_row(i * tm), 0, j))],
        out_specs=pl.BlockSpec((tm, tn), lambda i, j: (i, j)),
        out_shape=jax.ShapeDtypeStruct((m, n), F32),
        compiler_params=_cp(("arbitrary", "arbitrary")),
        name="matmul_gated_residual",
    )(a, w, x, mod)


def _dot_nt(a, b):
    return lax.dot_general(a, b, (((1,), (1,)), ((), ())), preferred_element_type=F32)


def _rms_head(x, gain):
    return x * lax.rsqrt(jnp.mean(x * x, axis=-1, keepdims=True) + EPS) * gain


def _rope(x, cosf, sinf):
    lane = lax.broadcasted_iota(I32, x.shape, 1)
    nxt = pltpu.roll(x, LANE - 1, 1)
    prv = pltpu.roll(x, 1, 1)
    return x * cosf + jnp.where((lane & 1) == 0, nxt, prv) * sinf


def _softmax_av(parts):
    m = functools.reduce(jnp.maximum, [jnp.max(s, axis=-1, keepdims=True) for s, _ in parts])
    l = None
    o = None
    for s, v in parts:
        p = jnp.exp(s - m)
        li = jnp.sum(p, axis=-1, keepdims=True)
        oi = jnp.dot(p.astype(BF16), v, preferred_element_type=F32)
        l = li if l is None else l + li
        o = oi if o is None else o + oi
    return o / l


def _na_kernel(q_ref, k_ref, v_ref, kc_ref, vc_ref, tab_ref, o_ref):
    kc = kc_ref[...]
    vc = vc_ref[...]
    n_blocks = ROWS // NA_QROWS

    def body(j, carry):
        ks = jnp.clip(j * NA_QROWS - NA_WIN_ROWS // 2, 0, ROWS - NA_SLAB)
        typ = jnp.where(j == 0, 0, jnp.where(j == n_blocks - 1, 2, 1))
        qs = pl.multiple_of(j * NA_QB, NA_QB)
        kst = pl.multiple_of(ks * GRID_W, GRID_W)
        q = q_ref[pl.ds(qs, NA_QB), :]
        k = k_ref[pl.ds(kst, NA_KB), :]
        v = v_ref[pl.ds(kst, NA_KB), :]
        s_win = _dot_nt(q, k) * ATT_SCALE + tab_ref[typ, 0]
        s_ctx = _dot_nt(q, kc) * ATT_SCALE
        o_ref[pl.ds(qs, NA_QB), :] = _softmax_av([(s_win, v), (s_ctx, vc)]).astype(BF16)
        return carry

    lax.fori_loop(0, n_blocks, body, 0)


def na_bias_table(rpb):
    def one(r0, ks):
        r = r0 + jnp.arange(NA_QROWS)
        kr = ks + jnp.arange(NA_SLAB)
        start = jnp.clip(r - NA_WIN_ROWS // 2, 0, ROWS - NA_WIN_ROWS)
        row_ok = (kr[None, :] >= start[:, None]) & (kr[None, :] < start[:, None] + NA_WIN_ROWS)
        row_idx = jnp.clip(kr[None, :] - r[:, None] + NA_WIN_ROWS - 1, 0, 2 * NA_WIN_ROWS - 2)
        cq = jnp.arange(GRID_W)
        col_start = jnp.clip(cq - NA_WIN_COLS // 2, 0, GRID_W - NA_WIN_COLS)
        col_ok = (cq[None, :] >= col_start[:, None]) & (cq[None, :] < col_start[:, None] + NA_WIN_COLS)
        col_idx = jnp.clip(cq[None, :] - cq[:, None] + NA_WIN_COLS - 1, 0, 2 * NA_WIN_COLS - 2)
        r_hot = jax.nn.one_hot(row_idx, 2 * NA_WIN_ROWS - 1, dtype=F32)
        c_hot = jax.nn.one_hot(col_idx, 2 * NA_WIN_COLS - 1, dtype=F32)
        bias = jnp.einsum('qka,hab,xyb->hqxky', r_hot, rpb.astype(F32), c_hot, precision=lax.Precision.HIGHEST)
        ok = row_ok[:, None, :, None] & col_ok[None, :, None, :]
        return jnp.where(ok[None], bias, NEG_INF).reshape(NA_HEADS, NA_QB, NA_KB)

    mid = 2 * NA_QROWS
    last = ROWS - NA_QROWS
    return jnp.stack([one(0, 0), one(mid, mid - NA_WIN_ROWS // 2), one(last, ROWS - NA_SLAB)])


def neighborhood_attention(p, table):
    cb = S // L
    return pl.pallas_call(
        _na_kernel,
        grid=(NA_HEADS, B),
        in_specs=[pl.BlockSpec((S, HD), lambda h, b: (b, h)),
                  pl.BlockSpec((S, HD), lambda h, b: (b, NA_HEADS + h)),
                  pl.BlockSpec((S, HD), lambda h, b: (b, 2 * NA_HEADS + h)),
                  pl.BlockSpec((L, HD), lambda h, b: (B * cb + b, NA_HEADS + h)),
                  pl.BlockSpec((L, HD), lambda h, b: (B * cb + b, 2 * NA_HEADS + h)),
                  pl.BlockSpec((3, 1, NA_QB, NA_KB), lambda h, b: (0, h, 0, 0))],
        out_specs=pl.BlockSpec((S, HD), lambda h, b: (b, h)),
        out_shape=jax.ShapeDtypeStruct((T_LAT, NA_HEADS * HD), BF16),
        compiler_params=_cp(("arbitrary", "arbitrary")),
        name="neighborhood_attention",
    )(p, p, p, p, p, table)


GQA_TQ = 256
GQA_CK = 512
GQA_QCOL = 3 * NA_HEADS
GQA_KCOL = GQA_QCOL + GQA_Q_HEADS
GQA_VCOL = GQA_KCOL + GQA_KV_HEADS


def _gqa_kernel(q_ref, k_ref, v_ref, kc_ref, vc_ref, cq_ref, sq_ref, ck_ref, sk_ref, qg_ref, kg_ref,
                o_ref, kn_ref, kcn_ref):
    @pl.when(pl.program_id(2) == 0)
    def _():
        kn = _rope(_rms_head(k_ref[...].astype(F32), kg_ref[...]), ck_ref[...], sk_ref[...])
        kn_ref[...] = kn.astype(BF16)
        kcn_ref[...] = _rms_head(kc_ref[...].astype(F32), kg_ref[...]).astype(BF16)

    cos = cq_ref[...]
    sin = sq_ref[...]
    heads = []
    for g in range(GQA_GROUP):
        qh = _rope(_rms_head(q_ref[:, g * HD:(g + 1) * HD].astype(F32), qg_ref[...]), cos, sin)
        heads.append((qh * (ATT_SCALE * LOG2E)).astype(BF16))
    q = jnp.concatenate(heads, axis=0)
    chunks = [(kn_ref[c * GQA_CK:(c + 1) * GQA_CK, :], v_ref[c * GQA_CK:(c + 1) * GQA_CK, :])
              for c in range(S // GQA_CK)]
    chunks.append((kcn_ref[...], vc_ref[...]))
    m = l = acc = None
    for kk, vv in chunks:
        s = _dot_nt(q, kk)
        mc = jnp.max(s, axis=-1, keepdims=True)
        if m is None:
            m_new = mc
            p = jnp.exp2(s - m_new)
            l = jnp.sum(p, axis=-1, keepdims=True)
            acc = jnp.dot(p.astype(BF16), vv, preferred_element_type=F32)
        else:
            m_new = jnp.maximum(m, mc)
            alpha = jnp.exp2(m - m_new)
            p = jnp.exp2(s - m_new)
            l = alpha * l + jnp.sum(p, axis=-1, keepdims=True)
            acc = alpha * acc + jnp.dot(p.astype(BF16), vv, preferred_element_type=F32)
        m = m_new
    o = acc / l
    for g in range(GQA_GROUP):
        o_ref[:, g * HD:(g + 1) * HD] = o[g * GQA_TQ:(g + 1) * GQA_TQ].astype(BF16)


def gqa_attention(p, cosf, sinf, q_gain, k_gain):
    nq = S // GQA_TQ
    cb = S // L
    gw = GQA_GROUP * HD
    return pl.pallas_call(
        _gqa_kernel,
        grid=(B, GQA_KV_HEADS, nq),
        in_specs=[pl.BlockSpec((GQA_TQ, gw), lambda b, n, i: (b * nq + i, GQA_QCOL // GQA_GROUP + n)),
                  pl.BlockSpec((S, HD), lambda b, n, i: (b, GQA_KCOL + n)),
                  pl.BlockSpec((S, HD), lambda b, n, i: (b, GQA_VCOL + n)),
                  pl.BlockSpec((L, HD), lambda b, n, i: (B * cb + b, GQA_KCOL + n)),
                  pl.BlockSpec((L, HD), lambda b, n, i: (B * cb + b, GQA_VCOL + n)),
                  pl.BlockSpec((GQA_TQ, HD), lambda b, n, i: (i, 0)),
                  pl.BlockSpec((GQA_TQ, HD), lambda b, n, i: (i, 0)),
                  pl.BlockSpec((S, HD), lambda b, n, i: (0, 0)),
                  pl.BlockSpec((S, HD), lambda b, n, i: (0, 0)),
                  pl.BlockSpec((1, HD), lambda b, n, i: (0, 0)),
                  pl.BlockSpec((1, HD), lambda b, n, i: (0, 0))],
        out_specs=pl.BlockSpec((GQA_TQ, gw), lambda b, n, i: (b * nq + i, n)),
        out_shape=jax.ShapeDtypeStruct((T_LAT, GQA_Q_HEADS * HD), BF16),
        scratch_shapes=[pltpu.VMEM((S, HD), BF16), pltpu.VMEM((L, HD), BF16)],
        compiler_params=_cp(("arbitrary",) * 3),
        name="gqa_attention",
    )(p, p, p, p, p, cosf, sinf, cosf, sinf, q_gain.reshape(1, HD), k_gain.reshape(1, HD))


def _ctx_attn_kernel(p_ref, qg_ref, kg_ref, o_ref):
    def col(c):
        return p_ref[:, c * HD:(c + 1) * HD]

    for h in range(NA_HEADS):
        s = _dot_nt(col(h), col(NA_HEADS + h)) * ATT_SCALE
        o_ref[:, h * HD:(h + 1) * HD] = _softmax_av([(s, col(2 * NA_HEADS + h))]).astype(BF16)
    for n in range(GQA_KV_HEADS):
        kn = _rms_head(col(GQA_KCOL + n).astype(F32), kg_ref[...]).astype(BF16)
        v = col(GQA_VCOL + n)
        for g in range(GQA_GROUP):
            h = n * GQA_GROUP + g
            qn = _rms_head(col(GQA_QCOL + h).astype(F32), qg_ref[...]).astype(BF16)
            s = _dot_nt(qn, kn) * ATT_SCALE
            o_ref[:, (NA_HEADS + h) * HD:(NA_HEADS + h + 1) * HD] = _softmax_av([(s, v)]).astype(BF16)


def ctx_attention(p, q_gain, k_gain):
    cb = S // L
    return pl.pallas_call(
        _ctx_attn_kernel,
        grid=(B,),
        in_specs=[pl.BlockSpec((L, ATTN_IN), lambda b: (B * cb + b, 0)),
                  pl.BlockSpec((1, HD), lambda b: (0, 0)),
                  pl.BlockSpec((1, HD), lambda b: (0, 0))],
        out_specs=pl.BlockSpec((L, D), lambda b: (b, 0)),
        out_shape=jax.ShapeDtypeStruct((T_CTX, D), BF16),
        compiler_params=_cp(("arbitrary",)),
        name="ctx_attention",
    )(p, q_gain.reshape(1, HD), k_gain.reshape(1, HD))


def _log_sigmoid(x):
    return -(jnp.maximum(-x, 0.0) + jnp.log1p(jnp.exp(-jnp.abs(x))))


def _dot_hi(a, b):
    return jnp.dot(a, b, precision=lax.Precision.HIGHEST, preferred_element_type=F32)


def _gate_kernel(h_ref, wg_ref, wgt_ref, b_ref, bt_ref, lt_ref, ut_ref, col_ref, row_ref):
    nh = ML_HEADS
    hx = h_ref[...]
    g = jnp.dot(hx, wg_ref[...], preferred_element_type=F32) + b_ref[...]
    gt = _dot_nt(wgt_ref[...], hx) + bt_ref[...]
    li = g[:, 0:2 * nh]
    lf = _log_sigmoid(g[:, 2 * nh:4 * nh])
    lit = gt[0:2 * nh]
    lft = _log_sigmoid(gt[2 * nh:4 * nh])
    lt = lt_ref[...]
    ut = ut_ref[...]
    lane = lax.broadcasted_iota(I32, lf.shape, 1)
    bc = jnp.where(lane < nh, _dot_hi(lt, lf), _dot_hi(ut, lf))
    tot = jnp.sum(lf, axis=0, keepdims=True)
    aend = tot - bc + li
    col_ref[...] = jnp.concatenate([bc, aend, jnp.zeros((ML_CH, LANE - 4 * nh), F32)], axis=1)
    sub = lax.broadcasted_iota(I32, lft.shape, 0)
    bct = jnp.where(sub < nh, _dot_hi(lft, ut), _dot_hi(lft, lt))
    tott = jnp.sum(lft, axis=1, keepdims=True)
    gtr = lit - bct
    row_ref[0] = jnp.concatenate([bct, gtr, tott + gtr, jnp.broadcast_to(tott, bct.shape)], axis=0)


def mlstm_gates(hx, wg, gate_b):
    nh = ML_HEADS
    n_ch = T_ALL // ML_CH
    wg_pad = jnp.zeros((D, LANE), BF16).at[:, :4 * nh].set(wg.astype(BF16))
    b_pad = jnp.zeros((1, LANE), F32).at[0, :4 * nh].set(gate_b.reshape(-1))
    wgt = wg.astype(BF16).T
    bt = gate_b.reshape(4 * nh, 1).astype(F32)
    lt = jnp.tril(jnp.ones((ML_CH, ML_CH), F32))
    ut = jnp.triu(jnp.ones((ML_CH, ML_CH), F32))
    col, row = pl.pallas_call(
        _gate_kernel,
        grid=(n_ch,),
        in_specs=[pl.BlockSpec((ML_CH, D), lambda i: (i, 0)),
                  pl.BlockSpec((D, LANE), lambda i: (0, 0)),
                  pl.BlockSpec((4 * nh, D), lambda i: (0, 0)),
                  pl.BlockSpec((1, LANE), lambda i: (0, 0)),
                  pl.BlockSpec((4 * nh, 1), lambda i: (0, 0)),
                  pl.BlockSpec((ML_CH, ML_CH), lambda i: (0, 0)),
                  pl.BlockSpec((ML_CH, ML_CH), lambda i: (0, 0))],
        out_specs=[pl.BlockSpec((ML_CH, LANE), lambda i: (i, 0)),
                   pl.BlockSpec((1, 8 * nh, ML_CH), lambda i: (i, 0, 0))],
        out_shape=[jax.ShapeDtypeStruct((T_ALL, LANE), F32),
                   jax.ShapeDtypeStruct((n_ch, 8 * nh, ML_CH), F32)],
        compiler_params=_cp(("arbitrary",)),
        name="mlstm_gates",
    )(hx, wg_pad, wgt, b_pad, bt, lt, ut)
    col3 = col[:, :4 * nh].reshape(T_ALL, 2, 2 * nh).transpose(2, 0, 1)
    col3 = jnp.pad(col3, ((0, 0), (0, 0), (0, 6)))
    row4 = row.reshape(n_ch, 4, 2 * nh, ML_CH).transpose(2, 0, 1, 3)
    row4 = jnp.pad(row4, ((0, 0), (0, 0), (0, 4), (0, 0)))
    return col3, row4


def _mlstm_step(d, hh, q_ref, k_ref, v_ref, col_ref, row_ref, o_ref, c_ref, n_ref, m_ref):
    sl = d * ML_HPS + hh
    q = q_ref[:, hh * ML_QK:(hh + 1) * ML_QK]
    kf = k_ref[:, hh * ML_QK:(hh + 1) * ML_QK].astype(F32) * ML_KSCALE
    kb = kf.astype(BF16)
    v = v_ref[:, hh * ML_V:(hh + 1) * ML_V]
    col = col_ref[hh]
    row = row_ref[hh, 0]
    bc_col = col[:, 0:1]
    aend_col = col[:, 1:2]
    g_row = row[1:2]
    aend_row = row[2:3]
    btot = row[3:4, 0:1]
    m_st = m_ref[sl]
    c_st = c_ref[sl]
    n_st = n_ref[sl]
    m_new = jnp.maximum(btot + m_st, jnp.max(aend_row, axis=1, keepdims=True))

    r = lax.broadcasted_iota(I32, (ML_CH, ML_CH), 0)
    c = lax.broadcasted_iota(I32, (ML_CH, ML_CH), 1)
    causal = (r >= c) if d == 0 else (r <= c)
    d_mat = jnp.where(causal, bc_col + g_row, -jnp.inf)
    m_row = jnp.maximum(bc_col + m_st, jnp.max(d_mat, axis=1, keepdims=True))
    w_inter = jnp.exp(bc_col + m_st - m_row)
    s_mat = _dot_nt(q, kb) * jnp.exp(d_mat - m_row)
    num = (w_inter * jnp.dot(q, c_st.astype(BF16), preferred_element_type=F32)
           + jnp.dot(s_mat.astype(BF16), v, preferred_element_type=F32))
    den = (w_inter * jnp.sum(q.astype(F32) * n_st, axis=1, keepdims=True)
           + jnp.sum(s_mat, axis=1, keepdims=True))
    o_ref[:, hh * ML_V:(hh + 1) * ML_V] = num / jnp.maximum(jnp.abs(den), jnp.exp(-m_row))

    w_end_col = jnp.exp(aend_col - m_new)
    w_end_row = jnp.exp(aend_row - m_new)
    decay = jnp.exp(btot + m_st - m_new)
    kw = (kf * w_end_col).astype(BF16)
    c_ref[sl] = decay * c_st + lax.dot_general(kw, v, (((0,), (0,)), ((), ())), preferred_element_type=F32)
    w8 = jnp.broadcast_to(w_end_row, (8, ML_CH)).astype(BF16)
    n_ref[sl] = decay * n_st + jnp.dot(w8, kb, preferred_element_type=F32)[0:1]
    m_ref[sl] = m_new


def _mlstm_kernel(qf, kf, vf, colf, rowf, qb, kb, vb, colb, rowb, of, ob, c_ref, n_ref, m_ref):
    @pl.when(pl.program_id(2) == 0)
    def _():
        c_ref[...] = jnp.zeros_like(c_ref)
        n_ref[...] = jnp.zeros_like(n_ref)
        m_ref[...] = jnp.zeros_like(m_ref)

    for hh in range(ML_HPS):
        _mlstm_step(0, hh, qf, kf, vf, colf, rowf, of, c_ref, n_ref, m_ref)
        _mlstm_step(1, hh, qb, kb, vb, colb, rowb, ob, c_ref, n_ref, m_ref)


def mlstm_scan(p, col3, row4):
    n_lat = S // ML_CH
    steps = n_lat + 1
    lat_blocks = T_LAT // ML_CH
    hps = ML_HPS
    n_groups = ML_HEADS // hps
    kcol = (ML_HEADS * ML_QK) // (hps * ML_QK)
    vcol = (2 * ML_HEADS * ML_QK) // (hps * ML_V)

    def chunk(b, d, st):
        c = (st - 1) if d == 0 else (n_lat - st)
        return jnp.where(st == 0, lat_blocks + b, b * n_lat + c)

    def out_chunk(b, d, st):
        s1 = jnp.maximum(st, 1)
        return b * n_lat + ((s1 - 1) if d == 0 else (n_lat - s1))

    def dir_specs(d):
        return [pl.BlockSpec((ML_CH, hps * ML_QK), lambda b, h, s: (chunk(b, d, s), h)),
                pl.BlockSpec((ML_CH, hps * ML_QK), lambda b, h, s: (chunk(b, d, s), kcol + h)),
                pl.BlockSpec((ML_CH, hps * ML_V), lambda b, h, s: (chunk(b, d, s), vcol + h)),
                pl.BlockSpec((hps, ML_CH, 8), lambda b, h, s: (d * n_groups + h, chunk(b, d, s), 0)),
                pl.BlockSpec((hps, 1, 8, ML_CH), lambda b, h, s: (d * n_groups + h, chunk(b, d, s), 0, 0))]

    return pl.pallas_call(
        _mlstm_kernel,
        grid=(B, n_groups, steps),
        in_specs=dir_specs(0) + dir_specs(1),
        out_specs=[pl.BlockSpec((ML_CH, hps * ML_V), lambda b, h, s: (out_chunk(b, 0, s), h)),
                   pl.BlockSpec((ML_CH, hps * ML_V), lambda b, h, s: (out_chunk(b, 1, s), h))],
        out_shape=[jax.ShapeDtypeStruct((T_LAT, ML_HEADS * ML_V), F32),
                   jax.ShapeDtypeStruct((T_LAT, ML_HEADS * ML_V), F32)],
        scratch_shapes=[pltpu.VMEM((2 * hps, ML_QK, ML_V), F32), pltpu.VMEM((2 * hps, 1, ML_QK), F32),
                        pltpu.VMEM((2 * hps, 1, 1), F32)],
        compiler_params=_cp(("arbitrary",) * 3),
        name="mlstm_scan",
    )(p, p, p, col3, row4, p, p, p, col3, row4)


def _readout_kernel(hf_ref, hb_ref, o_ref, g_ref, a_ref):
    hs = hf_ref[...] + hb_ref[...]
    for h in range(ML_HEADS):
        sl = slice(h * ML_V, (h + 1) * ML_V)
        x = hs[:, sl]
        hn = x * lax.rsqrt(jnp.mean(x * x, axis=-1, keepdims=True) + EPS) * g_ref[:, sl]
        a_ref[:, sl] = (hn * jax.nn.sigmoid(o_ref[:, sl].astype(F32))).astype(BF16)


def mlstm_readout(hdir, p, head_gain):
    tm = 256
    ocol = (2 * ML_HEADS * ML_QK + ML_HEADS * ML_V) // D
    return pl.pallas_call(
        _readout_kernel,
        grid=(T_LAT // tm,),
        in_specs=[pl.BlockSpec((tm, D), lambda i: (i, 0)),
                  pl.BlockSpec((tm, D), lambda i: (i, 0)),
                  pl.BlockSpec((tm, D), lambda i: (i, ocol)),
                  pl.BlockSpec((1, D), lambda i: (0, 0))],
        out_specs=pl.BlockSpec((tm, D), lambda i: (i, 0)),
        out_shape=jax.ShapeDtypeStruct((T_LAT, D), BF16),
        compiler_params=_cp(("arbitrary",)),
        name="mlstm_readout",
    )(hdir[0], hdir[1], p, head_gain.reshape(1, D))


ROUTER_TM = 512


def _router_kernel(h_ref, w_ref, rb_ref, erow_ref, tri_ref, eidx_ref, wts_ref, pos_ref, cnt_ref, carry_ref):
    ng = N_GROUPS
    epg = N_EXPERTS // N_GROUPS
    tm = ROUTER_TM
    ninf = -jnp.inf

    @pl.when(pl.program_id(0) == 0)
    def _():
        carry_ref[...] = jnp.zeros_like(carry_ref)

    s = jax.nn.sigmoid(_dot_nt(w_ref[...], h_ref[...]))
    ssel = s + rb_ref[...]
    sraw = [s[ng * j:ng * (j + 1)] for j in range(epg)]
    slab = [ssel[ng * j:ng * (j + 1)] for j in range(epg)]
    m1 = functools.reduce(jnp.maximum, slab)
    jfirst = functools.reduce(jnp.minimum, [jnp.where(slab[j] == m1, j, epg) for j in range(epg)])
    m2 = functools.reduce(jnp.maximum, [jnp.where(jfirst == j, ninf, slab[j]) for j in range(epg)])
    gs = m1 + m2
    giota = lax.broadcasted_iota(I32, (ng, tm), 0)
    gsel = jnp.zeros((ng, tm), F32)
    for _ in range(TOPK_GROUPS):
        mx = jnp.max(gs, axis=0, keepdims=True)
        gi = jnp.min(jnp.where(gs == mx, giota, ng), axis=0, keepdims=True)
        hit = giota == gi
        gsel = jnp.where(hit, 1.0, gsel)
        gs = jnp.where(hit, ninf, gs)
    msl = [jnp.where(gsel > 0.0, slab[j], ninf) for j in range(epg)]
    eid = [giota * epg + j for j in range(epg)]
    selm = [jnp.zeros((ng, tm), F32) for _ in range(epg)]
    e_list, w_list = [], []
    for _ in range(TOP_K):
        mx = jnp.max(functools.reduce(jnp.maximum, msl), axis=0, keepdims=True)
        cand = functools.reduce(jnp.minimum, [jnp.where(msl[j] == mx, eid[j], N_EXPERTS) for j in range(epg)])
        esel = jnp.min(cand, axis=0, keepdims=True)
        hits = [eid[j] == esel for j in range(epg)]
        wk = functools.reduce(lambda a, b: a + b, [jnp.where(hits[j], sraw[j], 0.0) for j in range(epg)])
        w_list.append(jnp.sum(wk, axis=0, keepdims=True))
        e_list.append(esel)
        msl = [jnp.where(hits[j], ninf, msl[j]) for j in range(epg)]
        selm = [jnp.where(hits[j], 1.0, selm[j]) for j in range(epg)]
    wsum = functools.reduce(lambda a, b: a + b, w_list)
    wts_ref[...] = jnp.concatenate([w / wsum * ROUTED_SCALE for w in w_list], axis=0)
    eidx_ref[...] = jnp.concatenate(e_list, axis=0)
    sel = jnp.concatenate(selm, axis=0)
    carry = carry_ref[...]
    posfull = jnp.dot(sel.astype(BF16), tri_ref[...], preferred_element_type=F32) + carry
    erow = erow_ref[...]
    pos = [jnp.sum(jnp.where(erow == e, posfull, 0.0), axis=0, keepdims=True) for e in e_list]
    pos_ref[...] = jnp.concatenate(pos, axis=0).astype(I32)
    carry = carry + jnp.sum(sel, axis=1, keepdims=True)
    carry_ref[...] = carry
    cnt_ref[...] = carry


def moe_router(hx, router_w, router_b, n_tok):
    tm = ROUTER_TM
    epg = N_EXPERTS // N_GROUPS
    perm = (jnp.arange(N_EXPERTS) % N_GROUPS) * epg + jnp.arange(N_EXPERTS) // N_GROUPS
    w_t = router_w.astype(BF16).T[perm]
    rb = router_b.astype(F32)[perm].reshape(N_EXPERTS, 1)
    erow = perm.astype(I32).reshape(N_EXPERTS, 1)
    tri = jnp.triu(jnp.ones((tm, tm), BF16), 1)
    eidx, wts, pos, counts = pl.pallas_call(
        _router_kernel,
        grid=(n_tok // tm,),
        in_specs=[pl.BlockSpec((tm, D), lambda i: (i, 0)),
                  pl.BlockSpec((N_EXPERTS, D), lambda i: (0, 0)),
                  pl.BlockSpec((N_EXPERTS, 1), lambda i: (0, 0)),
                  pl.BlockSpec((N_EXPERTS, 1), lambda i: (0, 0)),
                  pl.BlockSpec((tm, tm), lambda i: (0, 0))],
        out_specs=[pl.BlockSpec((TOP_K, tm), lambda i: (0, i)),
                   pl.BlockSpec((TOP_K, tm), lambda i: (0, i)),
                   pl.BlockSpec((TOP_K, tm), lambda i: (0, i)),
                   pl.BlockSpec((N_EXPERTS, 1), lambda i: (0, 0))],
        out_shape=[jax.ShapeDtypeStruct((TOP_K, n_tok), I32),
                   jax.ShapeDtypeStruct((TOP_K, n_tok), F32),
                   jax.ShapeDtypeStruct((TOP_K, n_tok), I32),
                   jax.ShapeDtypeStruct((N_EXPERTS, 1), F32)],
        scratch_shapes=[pltpu.VMEM((N_EXPERTS, 1), F32)],
        compiler_params=_cp(("arbitrary",)),
        name="moe_router",
    )(hx, w_t, rb, erow, tri)
    return eidx, wts, pos, counts.reshape(N_EXPERTS)[perm]


DISPATCH_TM = 512


def _dispatch_kernel(slot_ref, hx_ref, xs_hbm, sem):
    def issue(t, carry):
        src = hx_ref.at[pl.ds(t, 1), :]
        for k in range(TOP_K):
            pltpu.make_async_copy(src, xs_hbm.at[pl.ds(slot_ref[k, t], 1), :], sem).start()
        return carry

    lax.fori_loop(0, DISPATCH_TM, issue, 0)
    for _ in range(TOP_K):
        pltpu.make_async_copy(hx_ref, xs_hbm.at[pl.ds(0, DISPATCH_TM), :], sem).wait()


def moe_dispatch(slots, hx_packed, n_tok, n_rows):
    tm = DISPATCH_TM
    return pl.pallas_call(
        _dispatch_kernel,
        grid=(n_tok // tm,),
        in_specs=[pl.BlockSpec((TOP_K, tm), lambda i: (0, i), memory_space=pltpu.SMEM),
                  pl.BlockSpec((tm, PK_W), lambda i: (i, 0))],
        out_specs=pl.BlockSpec(memory_space=pl.ANY),
        out_shape=jax.ShapeDtypeStruct((n_rows, PK_W), U32),
        scratch_shapes=[pltpu.SemaphoreType.DMA(())],
        compiler_params=_cp(("arbitrary",)),
        name="moe_dispatch",
    )(slots, hx_packed)


def _expert_kernel(be_ref, valid_ref, nused_ref, first_ref, next_ref, slot_ref,
                   xs_ref, w1_hbm, w3_hbm, w2_hbm, y_ref, w1s, w3s, w2s, w1b, w3b, w2b, xb, sems, *, layer):
    i = pl.program_id(0)
    bm = EXP_BM

    def weight_copies(e, s):
        return [pltpu.make_async_copy(w_hbm.at[layer, e], stage.at[s], sems.at[s, j])
                for j, (w_hbm, stage) in enumerate(((w1_hbm, w1s), (w3_hbm, w3s), (w2_hbm, w2s)))]

    @pl.when(i < nused_ref[0])
    def _():
        @pl.when(first_ref[i] == 1)
        def _():
            s = slot_ref[i]

            @pl.when(i == 0)
            def _():
                for cp in weight_copies(be_ref[0], 0):
                    cp.start()

            for cp in weight_copies(be_ref[i], s):
                cp.wait()

            @pl.when(next_ref[i] >= 0)
            def _():
                for cp in weight_copies(next_ref[i], 1 - s):
                    cp.start()

            w1b[...] = w1s[s].astype(BF16)
            w3b[...] = w3s[s].astype(BF16)
            w2b[...] = w2s[s].astype(BF16)

        rows = lax.broadcasted_iota(I32, (bm, PK_W), 0)
        lo, hi = _unpack_bf16_pairs(jnp.where(rows < valid_ref[i], xs_ref[...], jnp.uint32(0)))
        xb[:, :PK_W] = lo.astype(BF16)
        xb[:, PK_W:] = hi.astype(BF16)
        x = xb[...]
        h1 = jnp.dot(x, w1b[...], preferred_element_type=F32)
        h3 = jnp.dot(x, w3b[...], preferred_element_type=F32)
        a = (h1 * jax.nn.sigmoid(h1) * h3).astype(BF16)
        y = jnp.dot(a, w2b[...], preferred_element_type=F32)
        y_ref[...] = _pack_bf16_pairs(y.astype(BF16))


def moe_experts(block_e, valid, n_used, first, next_e, slot, xs, w1, w3, w2, layer, n_blocks):
    bm = EXP_BM

    def blk(i, be, va, nu, fi, ne, sl):
        return (jnp.minimum(i, nu[0] - 1), 0)

    grid_spec = pltpu.PrefetchScalarGridSpec(
        num_scalar_prefetch=6,
        grid=(n_blocks,),
        in_specs=[pl.BlockSpec((bm, PK_W), blk),
                  pl.BlockSpec(memory_space=pl.ANY),
                  pl.BlockSpec(memory_space=pl.ANY),
                  pl.BlockSpec(memory_space=pl.ANY)],
        out_specs=pl.BlockSpec((bm, PK_W), blk),
        scratch_shapes=[pltpu.VMEM((2, D, EXPERT_DIM), F32), pltpu.VMEM((2, D, EXPERT_DIM), F32),
                        pltpu.VMEM((2, EXPERT_DIM, D), F32),
                        pltpu.VMEM((D, EXPERT_DIM), BF16), pltpu.VMEM((D, EXPERT_DIM), BF16),
                        pltpu.VMEM((EXPERT_DIM, D), BF16), pltpu.VMEM((bm, D), BF16),
                        pltpu.SemaphoreType.DMA((2, 3))],
    )
    return pl.pallas_call(
        functools.partial(_expert_kernel, layer=layer),
        grid_spec=grid_spec,
        out_shape=jax.ShapeDtypeStruct((n_blocks * bm, PK_W), U32),
        compiler_params=_cp(("arbitrary",), vmem=56 * 1024 * 1024),
        name="moe_experts",
    )(block_e, valid, n_used, first, next_e, slot, xs, w1, w3, w2)


def _shared_kernel(x_ref, w1_ref, w3_ref, w2_ref, o_ref):
    x = x_ref[...]
    h1 = jnp.dot(x, w1_ref[...], preferred_element_type=F32)
    h3 = jnp.dot(x, w3_ref[...], preferred_element_type=F32)
    a = (h1 * jax.nn.sigmoid(h1) * h3).astype(BF16)
    o_ref[...] = jnp.dot(a, w2_ref[...], preferred_element_type=F32)


def shared_expert(hx, w1, w3, w2, n_tok):
    tm = 512
    return pl.pallas_call(
        _shared_kernel,
        grid=(n_tok // tm,),
        in_specs=[pl.BlockSpec((tm, D), lambda i: (i, 0)),
                  pl.BlockSpec((D, EXPERT_DIM), lambda i: (0, 0)),
                  pl.BlockSpec((D, EXPERT_DIM), lambda i: (0, 0)),
                  pl.BlockSpec((EXPERT_DIM, D), lambda i: (0, 0))],
        out_specs=pl.BlockSpec((tm, D), lambda i: (i, 0)),
        out_shape=jax.ShapeDtypeStruct((n_tok, D), F32),
        compiler_params=_cp(("arbitrary",)),
        name="shared_expert",
    )(hx, w1, w3, w2)


COMBINE_TM = 128


def _combine_kernel(slot_ref, w_ref, sh_ref, x_ref, mod_ref, y_hbm, o_ref, buf, sem):
    tm = COMBINE_TM

    def issue(t, carry):
        for k in range(TOP_K):
            src = y_hbm.at[pl.ds(slot_ref[k, t], 1), :]
            pltpu.make_async_copy(src, buf.at[pl.ds(k * tm + t, 1), :], sem).start()
        return carry

    lax.fori_loop(0, tm, issue, 0)
    pltpu.make_async_copy(y_hbm.at[pl.ds(0, TOP_K * tm), :], buf, sem).wait()

    w = w_ref[...]
    acc_lo = acc_hi = None
    for k in range(TOP_K):
        lo, hi = _unpack_bf16_pairs(buf[k * tm:(k + 1) * tm, :])
        wk = w[:, k:k + 1]
        acc_lo = wk * lo if acc_lo is None else acc_lo + wk * lo
        acc_hi = wk * hi if acc_hi is None else acc_hi + wk * hi
    gate = mod_ref[0][5:6]
    o_ref[:, :PK_W] = x_ref[:, :PK_W] + gate[:, :PK_W] * (sh_ref[:, :PK_W] + acc_lo)
    o_ref[:, PK_W:] = x_ref[:, PK_W:] + gate[:, PK_W:] * (sh_ref[:, PK_W:] + acc_hi)


def moe_combine(slots, wts_tok, shared, x, mod, y, n_tok):
    tm = COMBINE_TM
    return pl.pallas_call(
        _combine_kernel,
        grid=(n_tok // tm,),
        in_specs=[pl.BlockSpec((TOP_K, tm), lambda i: (0, i), memory_space=pltpu.SMEM),
                  pl.BlockSpec((tm, TOP_K), lambda i: (i, 0)),
                  pl.BlockSpec((tm, D), lambda i: (i, 0)),
                  pl.BlockSpec((tm, D), lambda i: (i, 0)),
                  pl.BlockSpec((1, 6, D), lambda i: (_mod_row(i * tm), 0, 0)),
                  pl.BlockSpec(memory_space=pl.ANY)],
        out_specs=pl.BlockSpec((tm, D), lambda i: (i, 0)),
        out_shape=jax.ShapeDtypeStruct((n_tok, D), F32),
        scratch_shapes=[pltpu.VMEM((TOP_K * tm, PK_W), U32), pltpu.SemaphoreType.DMA(())],
        compiler_params=_cp(("arbitrary",)),
        name="moe_combine",
    )(slots, wts_tok, shared, x, mod, y)


def _lookup(table, idx):
    e = jnp.arange(table.shape[0], dtype=I32).reshape((-1,) + (1,) * idx.ndim)
    return jnp.sum(jnp.where(idx[None] == e, table.reshape(e.shape), 0), axis=0)


def moe_layer(x, mod, norm_gain, router_w, router_b, exp_w1, exp_w3, exp_w2, sw1, sw3, sw2, layer, n_tok):
    bm = EXP_BM
    n_blocks = -(-n_tok * TOP_K // bm) + N_EXPERTS
    hx, hx_packed = norm_mod(x, norm_gain, mod, 3, n_tok, pack=True)
    eidx, wts, pos, counts = moe_router(hx, router_w, router_b, n_tok)
    shared = shared_expert(hx, sw1.astype(BF16), sw3.astype(BF16), sw2.astype(BF16), n_tok)
    cnt = counts.astype(I32)
    padded = (cnt + bm - 1) // bm * bm
    pad_end = jnp.cumsum(padded)
    pad_start = pad_end - padded
    slots = _lookup(pad_start, eidx) + pos
    blk_row = jnp.arange(n_blocks, dtype=I32) * bm
    block_e = jnp.minimum(jnp.sum((pad_end[:, None] <= blk_row[None, :]).astype(I32), axis=0), N_EXPERTS - 1)
    valid = jnp.clip(_lookup(cnt, block_e) - (blk_row - _lookup(pad_start, block_e)), 0, bm).astype(I32)
    n_used = (pad_end[-1:] // bm).astype(I32)
    prev_e = jnp.concatenate([jnp.full((1,), -1, I32), block_e[:-1]])
    first = ((blk_row < pad_end[-1]) & (block_e != prev_e)).astype(I32)
    stage_slot = ((jnp.cumsum(first) - 1) % 2).astype(I32)
    eids = jnp.arange(N_EXPERTS, dtype=I32)
    later = jnp.where((eids[None, :] > eids[:, None]) & (padded[None, :] > 0), eids[None, :], N_EXPERTS)
    next_nonempty = jnp.min(later, axis=1)
    next_nonempty = jnp.where(next_nonempty == N_EXPERTS, -1, next_nonempty)
    next_e = _lookup(next_nonempty, block_e).astype(I32)
    xs = moe_dispatch(slots, hx_packed, n_tok, n_blocks * bm)
    y = moe_experts(block_e, valid, n_used, first, next_e, stage_slot, xs, exp_w1, exp_w3, exp_w2, layer, n_blocks)
    return moe_combine(slots, wts.T, shared, x, mod, y, n_tok)


def _rope_tables():
    t = jnp.arange(S, dtype=I32)
    row = (t // GRID_W).astype(F32)
    col = (t % GRID_W).astype(F32)
    n_freq = HD // 4
    inv_freq = ROPE_THETA ** (-jnp.arange(n_freq, dtype=F32) / n_freq)
    ang = jnp.concatenate([row[:, None] * inv_freq, col[:, None] * inv_freq], axis=-1)
    cosf = jnp.repeat(jnp.cos(ang), 2, axis=-1)
    sinf = jnp.stack([-jnp.sin(ang), jnp.sin(ang)], axis=-1).reshape(S, HD)
    return cosf, sinf


def kernel(x, c, ctx, c_ctx, ada_w, ada_b, norm_mix, norm_ffn, attn_w_in, attn_w_out, attn_rpb, attn_q_gain,
           attn_k_gain, ml_w_in, ml_w_out, ml_gate_b, ml_head_gain, router_w, router_b, exp_w1, exp_w3, exp_w2,
           sh_w1, sh_w3, sh_w2, final_norm_gain):
    depth = ada_w.shape[0]
    xa = jnp.concatenate([x.reshape(T_LAT, D), ctx.reshape(T_CTX, D)], axis=0)
    cvec = jnp.concatenate([c, c_ctx[None], jnp.zeros((8 - B - 1, D), F32)], axis=0)
    mod_all = ada_ln(cvec, ada_w, ada_b).reshape(depth, 8, 6, D)
    cosf, sinf = _rope_tables()

    mod = mod_all[0]
    hx = norm_mod(xa, norm_mix[0], mod, 0, T_ALL, pack=False)
    p = matmul(hx, attn_w_in[0].astype(BF16), BF16)
    o_na = neighborhood_attention(p, na_bias_table(attn_rpb[0]))
    o_gqa = gqa_attention(p, cosf, sinf, attn_q_gain[0], attn_k_gain[0])
    o_ctx = ctx_attention(p, attn_q_gain[0], attn_k_gain[0])
    o_all = jnp.concatenate([jnp.concatenate([o_na, o_gqa], axis=1), o_ctx], axis=0)
    xa = matmul_gated_residual(o_all, attn_w_out[0].astype(BF16), xa, mod, 2)
    xa = moe_layer(xa, mod, norm_ffn[0], router_w[0], router_b[0], exp_w1, exp_w3, exp_w2,
                   sh_w1[0], sh_w3[0], sh_w2[0], 0, T_ALL)

    mod = mod_all[1]
    hx = norm_mod(xa, norm_mix[1], mod, 0, T_ALL, pack=False)
    w_in = ml_w_in[0]
    p = matmul(hx, w_in[:, :ML_MAIN].astype(BF16), BF16)
    col3, row4 = mlstm_gates(hx, w_in[:, ML_MAIN:], ml_gate_b[0])
    hdir = mlstm_scan(p, col3, row4)
    a = mlstm_readout(hdir, p, ml_head_gain[0])
    xl = matmul_gated_residual(a, ml_w_out[0].astype(BF16), xa, mod, 2)
    xl = moe_layer(xl, mod, norm_ffn[1], router_w[1], router_b[1], exp_w1, exp_w3, exp_w2,
                   sh_w1[1], sh_w3[1], sh_w2[1], 1, T_LAT)
    return final_norm(xl, final_norm_gain).reshape(B, S, D)
```

```python
import functools

import jax
import jax.numpy as jnp
from jax import lax
from jax.experimental import pallas as pl
from jax.experimental.pallas import tpu as pltpu

F32 = jnp.float32
BF16 = jnp.bfloat16
I32 = jnp.int32
U32 = jnp.uint32

D = 2048
B = 4
S = 4096
L = 256
T_LAT = B * S
T_CTX = B * L
T_ALL = T_LAT + T_CTX
GRID_W = 64
ROWS = S // GRID_W
HD = 128
NA_HEADS = 8
NA_WIN_ROWS = 8
NA_WIN_COLS = 16
GQA_Q_HEADS = 8
GQA_KV_HEADS = 2
GQA_GROUP = 4
ROPE_THETA = 10000.0
ATTN_IN = 4608
ML_HEADS = 8
ML_V = 256
ML_QK = 128
ML_MAIN = 6144
N_EXPERTS = 64
TOP_K = 8
N_GROUPS = 8
TOPK_GROUPS = 4
EXPERT_DIM = 512
ROUTED_SCALE = 2.5
EPS = 1e-6
NEG_INF = -1e30
ATT_SCALE = HD ** -0.5
LOG2E = 1.4426950408889634
ML_KSCALE = ML_QK ** -0.5

LANE = 128
NA_QROWS = 4
NA_SLAB = NA_QROWS + NA_WIN_ROWS - 1
NA_QB = NA_QROWS * GRID_W
NA_KB = NA_SLAB * GRID_W
ML_CH = 256
ML_HPS = 2
EXP_BM = 512
PK_W = D // 2
VMEM_LIMIT = 48 * 1024 * 1024


def _cp(sem, vmem=VMEM_LIMIT):
    return pltpu.CompilerParams(dimension_semantics=sem, vmem_limit_bytes=vmem)


def _pack_bf16_pairs(xb):
    u = pltpu.bitcast(xb.astype(F32), U32)
    return (u[:, PK_W:] & jnp.uint32(0xFFFF0000)) | (u[:, :PK_W] >> 16)


def _unpack_bf16_pairs(u):
    return pltpu.bitcast(u << 16, F32), pltpu.bitcast(u & jnp.uint32(0xFFFF0000), F32)


def _mod_row(start_row):
    return jnp.where(start_row < T_LAT, start_row // S, B)


def _ada_kernel(c_ref, w_ref, b_ref, o_ref):
    c = c_ref[...]
    a = (c * jax.nn.sigmoid(c)).astype(BF16)
    w = w_ref[0].astype(BF16)
    o_ref[0] = jnp.dot(a, w, preferred_element_type=F32) + b_ref[0]


def ada_ln(cvec, ada_w, ada_b):
    depth = ada_w.shape[0]
    n = ada_w.shape[2]
    tn = 1024
    return pl.pallas_call(
        _ada_kernel,
        grid=(depth, n // tn),
        in_specs=[pl.BlockSpec((8, D), lambda l, j: (0, 0)),
                  pl.BlockSpec((1, D, tn), lambda l, j: (l, 0, j)),
                  pl.BlockSpec((1, 1, tn), lambda l, j: (l, 0, j))],
        out_specs=pl.BlockSpec((1, 8, tn), lambda l, j: (l, 0, j)),
        out_shape=jax.ShapeDtypeStruct((depth, 8, n), F32),
        compiler_params=_cp(("arbitrary", "arbitrary")),
        name="ada_ln",
    )(cvec, ada_w, ada_b.reshape(depth, 1, n))


def _norm_mod_kernel(x_ref, g_ref, mod_ref, *out_refs, base, pack):
    x = x_ref[...]
    y = x * lax.rsqrt(jnp.mean(x * x, axis=-1, keepdims=True) + EPS) * g_ref[...]
    m = mod_ref[0]
    h = y * (1.0 + m[base + 1:base + 2]) + m[base:base + 1]
    hb = h.astype(BF16)
    out_refs[0][...] = hb
    if pack:
        out_refs[1][...] = _pack_bf16_pairs(hb)


def norm_mod(x, gain, mod, base, n_rows, pack):
    tm = 256
    out_shape = [jax.ShapeDtypeStruct((n_rows, D), BF16)]
    out_specs = [pl.BlockSpec((tm, D), lambda i: (i, 0))]
    if pack:
        out_shape.append(jax.ShapeDtypeStruct((n_rows, PK_W), U32))
        out_specs.append(pl.BlockSpec((tm, PK_W), lambda i: (i, 0)))
    res = pl.pallas_call(
        functools.partial(_norm_mod_kernel, base=base, pack=pack),
        grid=(n_rows // tm,),
        in_specs=[pl.BlockSpec((tm, D), lambda i: (i, 0)),
                  pl.BlockSpec((1, D), lambda i: (0, 0)),
                  pl.BlockSpec((1, 6, D), lambda i: (_mod_row(i * tm), 0, 0))],
        out_specs=out_specs,
        out_shape=out_shape,
        compiler_params=_cp(("arbitrary",)),
        name="norm_mod",
    )(x, gain.reshape(1, D), mod)
    return res if pack else res[0]


def _norm_mm_kernel(x_ref, g_ref, mod_ref, w_ref, o_ref, *rest, emit_h):
    hb_ref = rest[-1]

    @pl.when(pl.program_id(1) == 0)
    def _():
        x = x_ref[...]
        y = x * lax.rsqrt(jnp.mean(x * x, axis=-1, keepdims=True) + EPS) * g_ref[...]
        m = mod_ref[0]
        hb_ref[...] = (y * (1.0 + m[1:2]) + m[0:1]).astype(BF16)
        if emit_h:
            rest[0][...] = hb_ref[...]

    o_ref[...] = jnp.dot(hb_ref[...], w_ref[...], preferred_element_type=F32).astype(o_ref.dtype)


def norm_matmul(x, gain, mod, w, emit_h, tm=1024, tn=512):
    m = x.shape[0]
    n = w.shape[1]
    out_shape = [jax.ShapeDtypeStruct((m, n), BF16)]
    out_specs = [pl.BlockSpec((tm, tn), lambda i, j: (i, j))]
    if emit_h:
        out_shape.append(jax.ShapeDtypeStruct((m, D), BF16))
        out_specs.append(pl.BlockSpec((tm, D), lambda i, j: (i, 0)))
    res = pl.pallas_call(
        functools.partial(_norm_mm_kernel, emit_h=emit_h),
        grid=(m // tm, n // tn),
        in_specs=[pl.BlockSpec((tm, D), lambda i, j: (i, 0)),
                  pl.BlockSpec((1, D), lambda i, j: (0, 0)),
                  pl.BlockSpec((1, 6, D), lambda i, j: (_mod_row(i * tm), 0, 0)),
                  pl.BlockSpec((D, tn), lambda i, j: (0, j))],
        out_specs=out_specs,
        out_shape=out_shape,
        scratch_shapes=[pltpu.VMEM((tm, D), BF16)],
        compiler_params=_cp(("arbitrary", "arbitrary")),
        name="norm_matmul",
    )(x, gain.reshape(1, D), mod, w)
    return res if emit_h else res[0]


def _mm_res_kernel(a_ref, w_ref, x_ref, mod_ref, o_ref, *, slot):
    acc = jnp.dot(a_ref[...], w_ref[...], preferred_element_type=F32)
    o_ref[...] = x_ref[...] + mod_ref[0][slot:slot + 1] * acc


def matmul_gated_residual(a, w, x, mod, slot, tm=1024, tn=512):
    m, k = a.shape
    n = w.shape[1]
    return pl.pallas_call(
        functools.partial(_mm_res_kernel, slot=slot),
        grid=(m // tm, n // tn),
        in_specs=[pl.BlockSpec((tm, k), lambda i, j: (i, 0)),
                  pl.BlockSpec((k, tn), lambda i, j: (0, j)),
                  pl.BlockSpec((tm, tn), lambda i, j: (i, j)),
                  pl.BlockSpec((1, 6, tn), lambda i, j: (_mod_row(i * tm), 0, j))],
        out_specs=pl.BlockSpec((tm, tn), lambda i, j: (i, j)),
        out_shape=jax.ShapeDtypeStruct((m, n), F32),
        compiler_params=_cp(("arbitrary", "arbitrary")),
        name="matmul_gated_residual",
    )(a, w, x, mod)


def _dot_nt(a, b):
    return lax.dot_general(a, b, (((1,), (1,)), ((), ())), preferred_element_type=F32)


def _rms_head(x, gain):
    return x * lax.rsqrt(jnp.mean(x * x, axis=-1, keepdims=True) + EPS) * gain


def _rope(x, cosf, sinf):
    lane = lax.broadcasted_iota(I32, x.shape, 1)
    nxt = pltpu.roll(x, LANE - 1, 1)
    prv = pltpu.roll(x, 1, 1)
    return x * cosf + jnp.where((lane & 1) == 0, nxt, prv) * sinf


def _softmax_av(parts):
    m = functools.reduce(jnp.maximum, [jnp.max(s, axis=-1, keepdims=True) for s, _ in parts])
    l = None
    o = None
    for s, v in parts:
        p = jnp.exp(s - m)
        li = jnp.sum(p, axis=-1, keepdims=True)
        oi = jnp.dot(p.astype(BF16), v, preferred_element_type=F32)
        l = li if l is None else l + li
        o = oi if o is None else o + oi
    return o / l


def _na_kernel(q_ref, k_ref, v_ref, kc_ref, vc_ref, tab_ref, o_ref):
    kc = kc_ref[...]
    vc = vc_ref[...]
    n_blocks = ROWS // NA_QROWS

    def body(j, carry):
        ks = jnp.clip(j * NA_QROWS - NA_WIN_ROWS // 2, 0, ROWS - NA_SLAB)
        typ = jnp.where(j == 0, 0, jnp.where(j == n_blocks - 1, 2, 1))
        qs = pl.multiple_of(j * NA_QB, NA_QB)
        kst = pl.multiple_of(ks * GRID_W, GRID_W)
        q = q_ref[pl.ds(qs, NA_QB), :]
        k = k_ref[pl.ds(kst, NA_KB), :]
        v = v_ref[pl.ds(kst, NA_KB), :]
        s_win = _dot_nt(q, k) * ATT_SCALE + tab_ref[typ, 0]
        s_ctx = _dot_nt(q, kc) * ATT_SCALE
        o_ref[pl.ds(qs, NA_QB), :] = _softmax_av([(s_win, v), (s_ctx, vc)]).astype(BF16)
        return carry

    lax.fori_loop(0, n_blocks, body, 0)


def na_bias_table(rpb):
    def one(r0, ks):
        r = r0 + jnp.arange(NA_QROWS)
        kr = ks + jnp.arange(NA_SLAB)
        start = jnp.clip(r - NA_WIN_ROWS // 2, 0, ROWS - NA_WIN_ROWS)
        row_ok = (kr[None, :] >= start[:, None]) & (kr[None, :] < start[:, None] + NA_WIN_ROWS)
        row_idx = jnp.clip(kr[None, :] - r[:, None] + NA_WIN_ROWS - 1, 0, 2 * NA_WIN_ROWS - 2)
        cq = jnp.arange(GRID_W)
        col_start = jnp.clip(cq - NA_WIN_COLS // 2, 0, GRID_W - NA_WIN_COLS)
        col_ok = (cq[None, :] >= col_start[:, None]) & (cq[None, :] < col_start[:, None] + NA_WIN_COLS)
        col_idx = jnp.clip(cq[None, :] - cq[:, None] + NA_WIN_COLS - 1, 0, 2 * NA_WIN_COLS - 2)
        r_hot = jax.nn.one_hot(row_idx, 2 * NA_WIN_ROWS - 1, dtype=F32)
        c_hot = jax.nn.one_hot(col_idx, 2 * NA_WIN_COLS - 1, dtype=F32)
        bias = jnp.einsum('qka,hab,xyb->hqxky', r_hot, rpb.astype(F32), c_hot, precision=lax.Precision.HIGHEST)
        ok = row_ok[:, None, :, None] & col_ok[None, :, None, :]
        return jnp.where(ok[None], bias, NEG_INF).reshape(NA_HEADS, NA_QB, NA_KB)

    mid = 2 * NA_QROWS
    last = ROWS - NA_QROWS
    return jnp.stack([one(0, 0), one(mid, mid - NA_WIN_ROWS // 2), one(last, ROWS - NA_SLAB)])


def neighborhood_attention(p, table):
    cb = S // L
    return pl.pallas_call(
        _na_kernel,
        grid=(NA_HEADS, B),
        in_specs=[pl.BlockSpec((S, HD), lambda h, b: (b, h)),
                  pl.BlockSpec((S, HD), lambda h, b: (b, NA_HEADS + h)),
                  pl.BlockSpec((S, HD), lambda h, b: (b, 2 * NA_HEADS + h)),
                  pl.BlockSpec((L, HD), lambda h, b: (B * cb + b, NA_HEADS + h)),
                  pl.BlockSpec((L, HD), lambda h, b: (B * cb + b, 2 * NA_HEADS + h)),
                  pl.BlockSpec((3, 1, NA_QB, NA_KB), lambda h, b: (0, h, 0, 0))],
        out_specs=pl.BlockSpec((S, HD), lambda h, b: (b, h)),
        out_shape=jax.ShapeDtypeStruct((T_LAT, NA_HEADS * HD), BF16),
        compiler_params=_cp(("arbitrary", "arbitrary")),
        name="neighborhood_attention",
    )(p, p, p, p, p, table)


GQA_TQ = 256
GQA_CK = 512
GQA_QCOL = 3 * NA_HEADS
GQA_KCOL = GQA_QCOL + GQA_Q_HEADS
GQA_VCOL = GQA_KCOL + GQA_KV_HEADS


def _gqa_kernel(q_ref, k_ref, v_ref, kc_ref, vc_ref, cq_ref, sq_ref, ck_ref, sk_ref, qg_ref, kg_ref,
                o_ref, kn_ref, kcn_ref):
    @pl.when(pl.program_id(2) == 0)
    def _():
        kn = _rope(_rms_head(k_ref[...].astype(F32), kg_ref[...]), ck_ref[...], sk_ref[...])
        kn_ref[...] = kn.astype(BF16)
        kcn_ref[...] = _rms_head(kc_ref[...].astype(F32), kg_ref[...]).astype(BF16)

    cos = cq_ref[...]
    sin = sq_ref[...]
    heads = []
    for g in range(GQA_GROUP):
        qh = _rope(_rms_head(q_ref[:, g * HD:(g + 1) * HD].astype(F32), qg_ref[...]), cos, sin)
        heads.append((qh * (ATT_SCALE * LOG2E)).astype(BF16))
    q = jnp.concatenate(heads, axis=0)
    chunks = [(kn_ref[c * GQA_CK:(c + 1) * GQA_CK, :], v_ref[c * GQA_CK:(c + 1) * GQA_CK, :])
              for c in range(S // GQA_CK)]
    chunks.append((kcn_ref[...], vc_ref[...]))
    m = l = acc = None
    for kk, vv in chunks:
        s = _dot_nt(q, kk)
        mc = jnp.max(s, axis=-1, keepdims=True)
        if m is None:
            m_new = mc
            p = jnp.exp2(s - m_new)
            l = jnp.sum(p, axis=-1, keepdims=True)
            acc = jnp.dot(p.astype(BF16), vv, preferred_element_type=F32)
        else:
            m_new = jnp.maximum(m, mc)
            alpha = jnp.exp2(m - m_new)
            p = jnp.exp2(s - m_new)
            l = alpha * l + jnp.sum(p, axis=-1, keepdims=True)
            acc = alpha * acc + jnp.dot(p.astype(BF16), vv, preferred_element_type=F32)
        m = m_new
    o = acc / l
    for g in range(GQA_GROUP):
        o_ref[:, g * HD:(g + 1) * HD] = o[g * GQA_TQ:(g + 1) * GQA_TQ].astype(BF16)


def gqa_attention(p, cosf, sinf, q_gain, k_gain):
    nq = S // GQA_TQ
    cb = S // L
    gw = GQA_GROUP * HD
    return pl.pallas_call(
        _gqa_kernel,
        grid=(B, GQA_KV_HEADS, nq),
        in_specs=[pl.BlockSpec((GQA_TQ, gw), lambda b, n, i: (b * nq + i, GQA_QCOL // GQA_GROUP + n)),
                  pl.BlockSpec((S, HD), lambda b, n, i: (b, GQA_KCOL + n)),
                  pl.BlockSpec((S, HD), lambda b, n, i: (b, GQA_VCOL + n)),
                  pl.BlockSpec((L, HD), lambda b, n, i: (B * cb + b, GQA_KCOL + n)),
                  pl.BlockSpec((L, HD), lambda b, n, i: (B * cb + b, GQA_VCOL + n)),
                  pl.BlockSpec((GQA_TQ, HD), lambda b, n, i: (i, 0)),
                  pl.BlockSpec((GQA_TQ, HD), lambda b, n, i: (i, 0)),
                  pl.BlockSpec((S, HD), lambda b, n, i: (0, 0)),
                  pl.BlockSpec((S, HD), lambda b, n, i: (0, 0)),
                  pl.BlockSpec((1, HD), lambda b, n, i: (0, 0)),
                  pl.BlockSpec((1, HD), lambda b, n, i: (0, 0))],
        out_specs=pl.BlockSpec((GQA_TQ, gw), lambda b, n, i: (b * nq + i, n)),
        out_shape=jax.ShapeDtypeStruct((T_LAT, GQA_Q_HEADS * HD), BF16),
        scratch_shapes=[pltpu.VMEM((S, HD), BF16), pltpu.VMEM((L, HD), BF16)],
        compiler_params=_cp(("arbitrary",) * 3),
        name="gqa_attention",
    )(p, p, p, p, p, cosf, sinf, cosf, sinf, q_gain.reshape(1, HD), k_gain.reshape(1, HD))


def _ctx_attn_kernel(p_ref, qg_ref, kg_ref, o_ref):
    def col(c):
        return p_ref[:, c * HD:(c + 1) * HD]

    for h in range(NA_HEADS):
        s = _dot_nt(col(h), col(NA_HEADS + h)) * ATT_SCALE
        o_ref[:, h * HD:(h + 1) * HD] = _softmax_av([(s, col(2 * NA_HEADS + h))]).astype(BF16)
    for n in range(GQA_KV_HEADS):
        kn = _rms_head(col(GQA_KCOL + n).astype(F32), kg_ref[...]).astype(BF16)
        v = col(GQA_VCOL + n)
        for g in range(GQA_GROUP):
            h = n * GQA_GROUP + g
            qn = _rms_head(col(GQA_QCOL + h).astype(F32), qg_ref[...]).astype(BF16)
            s = _dot_nt(qn, kn) * ATT_SCALE
            o_ref[:, (NA_HEADS + h) * HD:(NA_HEADS + h + 1) * HD] = _softmax_av([(s, v)]).astype(BF16)


def ctx_attention(p, q_gain, k_gain):
    cb = S // L
    return pl.pallas_call(
        _ctx_attn_kernel,
        grid=(B,),
        in_specs=[pl.BlockSpec((L, ATTN_IN), lambda b: (B * cb + b, 0)),
                  pl.BlockSpec((1, HD), lambda b: (0, 0)),
                  pl.BlockSpec((1, HD), lambda b: (0, 0))],
        out_specs=pl.BlockSpec((L, D), lambda b: (b, 0)),
        out_shape=jax.ShapeDtypeStruct((T_CTX, D), BF16),
        compiler_params=_cp(("arbitrary",)),
        name="ctx_attention",
    )(p, q_gain.reshape(1, HD), k_gain.reshape(1, HD))


def _log_sigmoid(x):
    return -(jnp.maximum(-x, 0.0) + jnp.log1p(jnp.exp(-jnp.abs(x))))


def _dot_hi(a, b):
    return jnp.dot(a, b, precision=lax.Precision.HIGHEST, preferred_element_type=F32)


def _gate_kernel(h_ref, wg_ref, wgt_ref, b_ref, bt_ref, lt_ref, ut_ref, col_ref, row_ref):
    nh = ML_HEADS
    hx = h_ref[...]
    g = jnp.dot(hx, wg_ref[...], preferred_element_type=F32) + b_ref[...]
    gt = _dot_nt(wgt_ref[...], hx) + bt_ref[...]
    li = g[:, 0:2 * nh]
    lf = _log_sigmoid(g[:, 2 * nh:4 * nh])
    lit = gt[0:2 * nh]
    lft = _log_sigmoid(gt[2 * nh:4 * nh])
    lt = lt_ref[...]
    ut = ut_ref[...]
    lane = lax.broadcasted_iota(I32, lf.shape, 1)
    bc = jnp.where(lane < nh, _dot_hi(lt, lf), _dot_hi(ut, lf))
    tot = jnp.sum(lf, axis=0, keepdims=True)
    aend = tot - bc + li
    col_ref[...] = jnp.concatenate([bc, aend, jnp.zeros((ML_CH, LANE - 4 * nh), F32)], axis=1)
    sub = lax.broadcasted_iota(I32, lft.shape, 0)
    bct = jnp.where(sub < nh, _dot_hi(lft, ut), _dot_hi(lft, lt))
    tott = jnp.sum(lft, axis=1, keepdims=True)
    gtr = lit - bct
    row_ref[0] = jnp.concatenate([bct, gtr, tott + gtr, jnp.broadcast_to(tott, bct.shape)], axis=0)


def mlstm_gates(hx, wg, gate_b):
    nh = ML_HEADS
    n_ch = T_ALL // ML_CH
    wg_pad = jnp.zeros((D, LANE), BF16).at[:, :4 * nh].set(wg.astype(BF16))
    b_pad = jnp.zeros((1, LANE), F32).at[0, :4 * nh].set(gate_b.reshape(-1))
    wgt = wg.astype(BF16).T
    bt = gate_b.reshape(4 * nh, 1).astype(F32)
    lt = jnp.tril(jnp.ones((ML_CH, ML_CH), F32))
    ut = jnp.triu(jnp.ones((ML_CH, ML_CH), F32))
    col, row = pl.pallas_call(
        _gate_kernel,
        grid=(n_ch,),
        in_specs=[pl.BlockSpec((ML_CH, D), lambda i: (i, 0)),
                  pl.BlockSpec((D, LANE), lambda i: (0, 0)),
                  pl.BlockSpec((4 * nh, D), lambda i: (0, 0)),
                  pl.BlockSpec((1, LANE), lambda i: (0, 0)),
                  pl.BlockSpec((4 * nh, 1), lambda i: (0, 0)),
                  pl.BlockSpec((ML_CH, ML_CH), lambda i: (0, 0)),
                  pl.BlockSpec((ML_CH, ML_CH), lambda i: (0, 0))],
        out_specs=[pl.BlockSpec((ML_CH, LANE), lambda i: (i, 0)),
                   pl.BlockSpec((1, 8 * nh, ML_CH), lambda i: (i, 0, 0))],
        out_shape=[jax.ShapeDtypeStruct((T_ALL, LANE), F32),
                   jax.ShapeDtypeStruct((n_ch, 8 * nh, ML_CH), F32)],
        compiler_params=_cp(("arbitrary",)),
        name="mlstm_gates",
    )(hx, wg_pad, wgt, b_pad, bt, lt, ut)
    col3 = col[:, :4 * nh].reshape(T_ALL, 2, 2 * nh).transpose(2, 0, 1)
    col3 = jnp.pad(col3, ((0, 0), (0, 0), (0, 6)))
    row4 = row.reshape(n_ch, 4, 2 * nh, ML_CH).transpose(2, 0, 1, 3)
    row4 = jnp.pad(row4, ((0, 0), (0, 0), (0, 4), (0, 0)))
    return col3, row4


def _mlstm_step(d, hh, q_ref, k_ref, v_ref, col_ref, row_ref, o_ref, c_ref, n_ref, m_ref):
    sl = d * ML_HPS + hh
    q = q_ref[:, hh * ML_QK:(hh + 1) * ML_QK]
    kf = k_ref[:, hh * ML_QK:(hh + 1) * ML_QK].astype(F32) * ML_KSCALE
    kb = kf.astype(BF16)
    v = v_ref[:, hh * ML_V:(hh + 1) * ML_V]
    col = col_ref[hh]
    row = row_ref[hh, 0]
    bc_col = col[:, 0:1]
    aend_col = col[:, 1:2]
    g_row = row[1:2]
    aend_row = row[2:3]
    btot = row[3:4, 0:1]
    m_st = m_ref[sl]
    c_st = c_ref[sl]
    n_st = n_ref[sl]
    m_new = jnp.maximum(btot + m_st, jnp.max(aend_row, axis=1, keepdims=True))

    r = lax.broadcasted_iota(I32, (ML_CH, ML_CH), 0)
    c = lax.broadcasted_iota(I32, (ML_CH, ML_CH), 1)
    causal = (r >= c) if d == 0 else (r <= c)
    d_mat = jnp.where(causal, bc_col + g_row, -jnp.inf)
    m_row = jnp.maximum(bc_col + m_st, jnp.max(d_mat, axis=1, keepdims=True))
    w_inter = jnp.exp(bc_col + m_st - m_row)
    s_mat = _dot_nt(q, kb) * jnp.exp(d_mat - m_row)
    num = (w_inter * jnp.dot(q, c_st.astype(BF16), preferred_element_type=F32)
           + jnp.dot(s_mat.astype(BF16), v, preferred_element_type=F32))
    den = (w_inter * jnp.sum(q.astype(F32) * n_st, axis=1, keepdims=True)
           + jnp.sum(s_mat, axis=1, keepdims=True))
    o_ref[:, hh * ML_V:(hh + 1) * ML_V] = num / jnp.maximum(jnp.abs(den), jnp.exp(-m_row))

    w_end_col = jnp.exp(aend_col - m_new)
    w_end_row = jnp.exp(aend_row - m_new)
    decay = jnp.exp(btot + m_st - m_new)
    kw = (kf * w_end_col).astype(BF16)
    c_ref[sl] = decay * c_st + lax.dot_general(kw, v, (((0,), (0,)), ((), ())), preferred_element_type=F32)
    w8 = jnp.broadcast_to(w_end_row, (8, ML_CH)).astype(BF16)
    n_ref[sl] = decay * n_st + jnp.dot(w8, kb, preferred_element_type=F32)[0:1]
    m_ref[sl] = m_new


def _mlstm_kernel(qf, kf, vf, colf, rowf, qb, kb, vb, colb, rowb, of, ob, c_ref, n_ref, m_ref):
    @pl.when(pl.program_id(2) == 0)
    def _():
        c_ref[...] = jnp.zeros_like(c_ref)
        n_ref[...] = jnp.zeros_like(n_ref)
        m_ref[...] = jnp.zeros_like(m_ref)

    for hh in range(ML_HPS):
        _mlstm_step(0, hh, qf, kf, vf, colf, rowf, of, c_ref, n_ref, m_ref)
        _mlstm_step(1, hh, qb, kb, vb, colb, rowb, ob, c_ref, n_ref, m_ref)


def mlstm_scan(p, col3, row4):
    n_lat = S // ML_CH
    steps = n_lat + 1
    lat_blocks = T_LAT // ML_CH
    hps = ML_HPS
    n_groups = ML_HEADS // hps
    kcol = (ML_HEADS * ML_QK) // (hps * ML_QK)
    vcol = (2 * ML_HEADS * ML_QK) // (hps * ML_V)

    def chunk(b, d, st):
        c = (st - 1) if d == 0 else (n_lat - st)
        return jnp.where(st == 0, lat_blocks + b, b * n_lat + c)

    def out_chunk(b, d, st):
        s1 = jnp.maximum(st, 1)
        return b * n_lat + ((s1 - 1) if d == 0 else (n_lat - s1))

    def dir_specs(d):
        return [pl.BlockSpec((ML_CH, hps * ML_QK), lambda b, h, s: (chunk(b, d, s), h)),
                pl.BlockSpec((ML_CH, hps * ML_QK), lambda b, h, s: (chunk(b, d, s), kcol + h)),
                pl.BlockSpec((ML_CH, hps * ML_V), lambda b, h, s: (chunk(b, d, s), vcol + h)),
                pl.BlockSpec((hps, ML_CH, 8), lambda b, h, s: (d * n_groups + h, chunk(b, d, s), 0)),
                pl.BlockSpec((hps, 1, 8, ML_CH), lambda b, h, s: (d * n_groups + h, chunk(b, d, s), 0, 0))]

    return pl.pallas_call(
        _mlstm_kernel,
        grid=(B, n_groups, steps),
        in_specs=dir_specs(0) + dir_specs(1),
        out_specs=[pl.BlockSpec((ML_CH, hps * ML_V), lambda b, h, s: (out_chunk(b, 0, s), h)),
                   pl.BlockSpec((ML_CH, hps * ML_V), lambda b, h, s: (out_chunk(b, 1, s), h))],
        out_shape=[jax.ShapeDtypeStruct((T_LAT, ML_HEADS * ML_V), F32),
                   jax.ShapeDtypeStruct((T_LAT, ML_HEADS * ML_V), F32)],
        scratch_shapes=[pltpu.VMEM((2 * hps, ML_QK, ML_V), F32), pltpu.VMEM((2 * hps, 1, ML_QK), F32),
                        pltpu.VMEM((2 * hps, 1, 1), F32)],
        compiler_params=_cp(("arbitrary",) * 3),
        name="mlstm_scan",
    )(p, p, p, col3, row4, p, p, p, col3, row4)


def _readout_kernel(hf_ref, hb_ref, o_ref, g_ref, a_ref):
    hs = hf_ref[...] + hb_ref[...]
    for h in range(ML_HEADS):
        sl = slice(h * ML_V, (h + 1) * ML_V)
        x = hs[:, sl]
        hn = x * lax.rsqrt(jnp.mean(x * x, axis=-1, keepdims=True) + EPS) * g_ref[:, sl]
        a_ref[:, sl] = (hn * jax.nn.sigmoid(o_ref[:, sl].astype(F32))).astype(BF16)


def mlstm_readout(hdir, p, head_gain):
    tm = 256
    ocol = (2 * ML_HEADS * ML_QK + ML_HEADS * ML_V) // D
    return pl.pallas_call(
        _readout_kernel,
        grid=(T_LAT // tm,),
        in_specs=[pl.BlockSpec((tm, D), lambda i: (i, 0)),
                  pl.BlockSpec((tm, D), lambda i: (i, 0)),
                  pl.BlockSpec((tm, D), lambda i: (i, ocol)),
                  pl.BlockSpec((1, D), lambda i: (0, 0))],
        out_specs=pl.BlockSpec((tm, D), lambda i: (i, 0)),
        out_shape=jax.ShapeDtypeStruct((T_LAT, D), BF16),
        compiler_params=_cp(("arbitrary",)),
        name="mlstm_readout",
    )(hdir[0], hdir[1], p, head_gain.reshape(1, D))


ROUTER_TM = 512


def _router_kernel(h_ref, w_ref, rb_ref, erow_ref, tri_ref, eidx_ref, wts_ref, pos_ref, cnt_ref, carry_ref):
    ng = N_GROUPS
    epg = N_EXPERTS // N_GROUPS
    tm = ROUTER_TM
    ninf = -jnp.inf

    @pl.when(pl.program_id(0) == 0)
    def _():
        carry_ref[...] = jnp.zeros_like(carry_ref)

    s = jax.nn.sigmoid(_dot_nt(w_ref[...], h_ref[...]))
    ssel = s + rb_ref[...]
    sraw = [s[ng * j:ng * (j + 1)] for j in range(epg)]
    slab = [ssel[ng * j:ng * (j + 1)] for j in range(epg)]
    m1 = functools.reduce(jnp.maximum, slab)
    jfirst = functools.reduce(jnp.minimum, [jnp.where(slab[j] == m1, j, epg) for j in range(epg)])
    m2 = functools.reduce(jnp.maximum, [jnp.where(jfirst == j, ninf, slab[j]) for j in range(epg)])
    gs = m1 + m2
    giota = lax.broadcasted_iota(I32, (ng, tm), 0)
    gsel = jnp.zeros((ng, tm), F32)
    for _ in range(TOPK_GROUPS):
        mx = jnp.max(gs, axis=0, keepdims=True)
        gi = jnp.min(jnp.where(gs == mx, giota, ng), axis=0, keepdims=True)
        hit = giota == gi
        gsel = jnp.where(hit, 1.0, gsel)
        gs = jnp.where(hit, ninf, gs)
    msl = [jnp.where(gsel > 0.0, slab[j], ninf) for j in range(epg)]
    eid = [giota * epg + j for j in range(epg)]
    selm = [jnp.zeros((ng, tm), F32) for _ in range(epg)]
    e_list, w_list = [], []
    for _ in range(TOP_K):
        mx = jnp.max(functools.reduce(jnp.maximum, msl), axis=0, keepdims=True)
        cand = functools.reduce(jnp.minimum, [jnp.where(msl[j] == mx, eid[j], N_EXPERTS) for j in range(epg)])
        esel = jnp.min(cand, axis=0, keepdims=True)
        hits = [eid[j] == esel for j in range(epg)]
        wk = functools.reduce(lambda a, b: a + b, [jnp.where(hits[j], sraw[j], 0.0) for j in range(epg)])
        w_list.append(jnp.sum(wk, axis=0, keepdims=True))
        e_list.append(esel)
        msl = [jnp.where(hits[j], ninf, msl[j]) for j in range(epg)]
        selm = [jnp.where(hits[j], 1.0, selm[j]) for j in range(epg)]
    wsum = functools.reduce(lambda a, b: a + b, w_list)
    wts_ref[...] = jnp.concatenate([w / wsum * ROUTED_SCALE for w in w_list], axis=0)
    eidx_ref[...] = jnp.concatenate(e_list, axis=0)
    sel = jnp.concatenate(selm, axis=0)
    carry = carry_ref[...]
    posfull = jnp.dot(sel.astype(BF16), tri_ref[...], preferred_element_type=F32) + carry
    erow = erow_ref[...]
    pos = [jnp.sum(jnp.where(erow == e, posfull, 0.0), axis=0, keepdims=True) for e in e_list]
    pos_ref[...] = jnp.concatenate(pos, axis=0).astype(I32)
    carry = carry + jnp.sum(sel, axis=1, keepdims=True)
    carry_ref[...] = carry
    cnt_ref[...] = carry


def moe_router(hx, router_w, router_b, n_tok):
    tm = ROUTER_TM
    epg = N_EXPERTS // N_GROUPS
    perm = (jnp.arange(N_EXPERTS) % N_GROUPS) * epg + jnp.arange(N_EXPERTS) // N_GROUPS
    w_t = router_w.astype(BF16).T[perm]
    rb = router_b.astype(F32)[perm].reshape(N_EXPERTS, 1)
    erow = perm.astype(I32).reshape(N_EXPERTS, 1)
    tri = jnp.triu(jnp.ones((tm, tm), BF16), 1)
    eidx, wts, pos, counts = pl.pallas_call(
        _router_kernel,
        grid=(n_tok // tm,),
        in_specs=[pl.BlockSpec((tm, D), lambda i: (i, 0)),
                  pl.BlockSpec((N_EXPERTS, D), lambda i: (0, 0)),
                  pl.BlockSpec((N_EXPERTS, 1), lambda i: (0, 0)),
                  pl.BlockSpec((N_EXPERTS, 1), lambda i: (0, 0)),
                  pl.BlockSpec((tm, tm), lambda i: (0, 0))],
        out_specs=[pl.BlockSpec((TOP_K, tm), lambda i: (0, i)),
                   pl.BlockSpec((TOP_K, tm), lambda i: (0, i)),
                   pl.BlockSpec((TOP_K, tm), lambda i: (0, i)),
                   pl.BlockSpec((N_EXPERTS, 1), lambda i: (0, 0))],
        out_shape=[jax.ShapeDtypeStruct((TOP_K, n_tok), I32),
                   jax.ShapeDtypeStruct((TOP_K, n_tok), F32),
                   jax.ShapeDtypeStruct((TOP_K, n_tok), I32),
                   jax.ShapeDtypeStruct((N_EXPERTS, 1), F32)],
        scratch_shapes=[pltpu.VMEM((N_EXPERTS, 1), F32)],
        compiler_params=_cp(("arbitrary",)),
        name="moe_router",
    )(hx, w_t, rb, erow, tri)
    return eidx, wts, pos, counts.reshape(N_EXPERTS)[perm]


DISPATCH_TM = 512


ROW_UNROLL = 8


def _dispatch_kernel(slot_ref, hx_ref, xs_hbm, sem):
    def issue(tt, carry):
        t8 = pl.multiple_of(tt * ROW_UNROLL, ROW_UNROLL)
        for j in range(ROW_UNROLL):
            src = hx_ref.at[pl.ds(t8 + j, 1), :]
            for k in range(TOP_K):
                slot = slot_ref[k * DISPATCH_TM + j + t8]
                pltpu.make_async_copy(src, xs_hbm.at[pl.ds(slot, 1), :], sem).start()
        return carry

    lax.fori_loop(0, DISPATCH_TM // ROW_UNROLL, issue, 0)
    for _ in range(TOP_K):
        pltpu.make_async_copy(hx_ref, xs_hbm.at[pl.ds(0, DISPATCH_TM), :], sem).wait()


def moe_dispatch(slots, hx_packed, n_tok, n_rows):
    tm = DISPATCH_TM
    return pl.pallas_call(
        _dispatch_kernel,
        grid=(n_tok // tm,),
        in_specs=[pl.BlockSpec((TOP_K * tm,), lambda i: (i,), memory_space=pltpu.SMEM),
                  pl.BlockSpec((tm, PK_W), lambda i: (i, 0))],
        out_specs=pl.BlockSpec(memory_space=pl.ANY),
        out_shape=jax.ShapeDtypeStruct((n_rows, PK_W), U32),
        scratch_shapes=[pltpu.SemaphoreType.DMA(())],
        compiler_params=_cp(("arbitrary",)),
        name="moe_dispatch",
    )(slots, hx_packed)


def _expert_kernel(be_ref, valid_ref, nused_ref, first_ref, next_ref, slot_ref,
                   xs_ref, w1_hbm, w3_hbm, w2_hbm, y_ref, w1s, w3s, w2s, w1b, w3b, w2b, xb, sems, *, layer):
    i = pl.program_id(0)
    bm = EXP_BM

    def weight_copies(e, s):
        return [pltpu.make_async_copy(w_hbm.at[layer, e], stage.at[s], sems.at[s, j])
                for j, (w_hbm, stage) in enumerate(((w1_hbm, w1s), (w3_hbm, w3s), (w2_hbm, w2s)))]

    @pl.when(i < nused_ref[0])
    def _():
        @pl.when(first_ref[i] == 1)
        def _():
            s = slot_ref[i]

            @pl.when(i == 0)
            def _():
                for cp in weight_copies(be_ref[0], 0):
                    cp.start()

            for cp in weight_copies(be_ref[i], s):
                cp.wait()

            @pl.when(next_ref[i] >= 0)
            def _():
                for cp in weight_copies(next_ref[i], 1 - s):
                    cp.start()

            w1b[...] = w1s[s].astype(BF16)
            w3b[...] = w3s[s].astype(BF16)
            w2b[...] = w2s[s].astype(BF16)

        rows = lax.broadcasted_iota(I32, (bm, PK_W), 0)
        lo, hi = _unpack_bf16_pairs(jnp.where(rows < valid_ref[i], xs_ref[...], jnp.uint32(0)))
        xb[:, :PK_W] = lo.astype(BF16)
        xb[:, PK_W:] = hi.astype(BF16)
        x = xb[...]
        h1 = jnp.dot(x, w1b[...], preferred_element_type=F32)
        h3 = jnp.dot(x, w3b[...], preferred_element_type=F32)
        a = (h1 * jax.nn.sigmoid(h1) * h3).astype(BF16)
        y = jnp.dot(a, w2b[...], preferred_element_type=F32)
        y_ref[...] = _pack_bf16_pairs(y.astype(BF16))


def moe_experts(block_e, valid, n_used, first, next_e, slot, xs, w1, w3, w2, layer, n_blocks):
    bm = EXP_BM

    def blk(i, be, va, nu, fi, ne, sl):
        return (jnp.minimum(i, nu[0] - 1), 0)

    grid_spec = pltpu.PrefetchScalarGridSpec(
        num_scalar_prefetch=6,
        grid=(n_blocks,),
        in_specs=[pl.BlockSpec((bm, PK_W), blk),
                  pl.BlockSpec(memory_space=pl.ANY),
                  pl.BlockSpec(memory_space=pl.ANY),
                  pl.BlockSpec(memory_space=pl.ANY)],
        out_specs=pl.BlockSpec((bm, PK_W), blk),
        scratch_shapes=[pltpu.VMEM((2, D, EXPERT_DIM), F32), pltpu.VMEM((2, D, EXPERT_DIM), F32),
                        pltpu.VMEM((2, EXPERT_DIM, D), F32),
                        pltpu.VMEM((D, EXPERT_DIM), BF16), pltpu.VMEM((D, EXPERT_DIM), BF16),
                        pltpu.VMEM((EXPERT_DIM, D), BF16), pltpu.VMEM((bm, D), BF16),
                        pltpu.SemaphoreType.DMA((2, 3))],
    )
    return pl.pallas_call(
        functools.partial(_expert_kernel, layer=layer),
        grid_spec=grid_spec,
        out_shape=jax.ShapeDtypeStruct((n_blocks * bm, PK_W), U32),
        compiler_params=_cp(("arbitrary",), vmem=56 * 1024 * 1024),
        name="moe_experts",
    )(block_e, valid, n_used, first, next_e, slot, xs, w1, w3, w2)


def _shared_kernel(x_ref, w1_ref, w3_ref, w2_ref, o_ref):
    x = x_ref[...]
    h1 = jnp.dot(x, w1_ref[...], preferred_element_type=F32)
    h3 = jnp.dot(x, w3_ref[...], preferred_element_type=F32)
    a = (h1 * jax.nn.sigmoid(h1) * h3).astype(BF16)
    o_ref[...] = jnp.dot(a, w2_ref[...], preferred_element_type=F32)


def shared_expert(hx, w1, w3, w2, n_tok):
    tm = 512
    return pl.pallas_call(
        _shared_kernel,
        grid=(n_tok // tm,),
        in_specs=[pl.BlockSpec((tm, D), lambda i: (i, 0)),
                  pl.BlockSpec((D, EXPERT_DIM), lambda i: (0, 0)),
                  pl.BlockSpec((D, EXPERT_DIM), lambda i: (0, 0)),
                  pl.BlockSpec((EXPERT_DIM, D), lambda i: (0, 0))],
        out_specs=pl.BlockSpec((tm, D), lambda i: (i, 0)),
        out_shape=jax.ShapeDtypeStruct((n_tok, D), F32),
        compiler_params=_cp(("arbitrary",)),
        name="shared_expert",
    )(hx, w1, w3, w2)


COMBINE_TM = 128


def _combine_kernel(slot_ref, w_ref, sh_ref, x_ref, mod_ref, y_hbm, *rest, final):
    fg_ref = rest[0] if final else None
    o_ref, buf, sem = rest[-3:]
    tm = COMBINE_TM

    def issue(tt, carry):
        t8 = pl.multiple_of(tt * ROW_UNROLL, ROW_UNROLL)
        for j in range(ROW_UNROLL):
            for k in range(TOP_K):
                src = y_hbm.at[pl.ds(slot_ref[k * tm + j + t8], 1), :]
                pltpu.make_async_copy(src, buf.at[pl.ds(k * tm + j + t8, 1), :], sem).start()
        return carry

    lax.fori_loop(0, tm // ROW_UNROLL, issue, 0)
    pltpu.make_async_copy(y_hbm.at[pl.ds(0, TOP_K * tm), :], buf, sem).wait()

    w = w_ref[...]
    acc_lo = acc_hi = None
    for k in range(TOP_K):
        lo, hi = _unpack_bf16_pairs(buf[k * tm:(k + 1) * tm, :])
        wk = w[:, k:k + 1]
        acc_lo = wk * lo if acc_lo is None else acc_lo + wk * lo
        acc_hi = wk * hi if acc_hi is None else acc_hi + wk * hi
    gate = mod_ref[0][5:6]
    o_lo = x_ref[:, :PK_W] + gate[:, :PK_W] * (sh_ref[:, :PK_W] + acc_lo)
    o_hi = x_ref[:, PK_W:] + gate[:, PK_W:] * (sh_ref[:, PK_W:] + acc_hi)
    if fg_ref is not None:
        ssq = jnp.sum(o_lo * o_lo, axis=-1, keepdims=True) + jnp.sum(o_hi * o_hi, axis=-1, keepdims=True)
        inv = lax.rsqrt(ssq / D + EPS)
        o_lo = o_lo * inv * fg_ref[:, :PK_W]
        o_hi = o_hi * inv * fg_ref[:, PK_W:]
    o_ref[:, :PK_W] = o_lo
    o_ref[:, PK_W:] = o_hi


def moe_combine(slots, wts_tok, shared, x, mod, y, n_tok, final_gain=None):
    tm = COMBINE_TM
    final = final_gain is not None
    in_specs = [pl.BlockSpec((TOP_K * tm,), lambda i: (i,), memory_space=pltpu.SMEM),
                pl.BlockSpec((tm, TOP_K), lambda i: (i, 0)),
                pl.BlockSpec((tm, D), lambda i: (i, 0)),
                pl.BlockSpec((tm, D), lambda i: (i, 0)),
                pl.BlockSpec((1, 6, D), lambda i: (_mod_row(i * tm), 0, 0)),
                pl.BlockSpec(memory_space=pl.ANY)]
    args = [slots, wts_tok, shared, x, mod, y]
    if final:
        in_specs.append(pl.BlockSpec((1, D), lambda i: (0, 0)))
        args.append(final_gain.reshape(1, D))
    return pl.pallas_call(
        functools.partial(_combine_kernel, final=final),
        grid=(n_tok // tm,),
        in_specs=in_specs,
        out_specs=pl.BlockSpec((tm, D), lambda i: (i, 0)),
        out_shape=jax.ShapeDtypeStruct((n_tok, D), F32),
        scratch_shapes=[pltpu.VMEM((TOP_K * tm, PK_W), U32), pltpu.SemaphoreType.DMA(())],
        compiler_params=_cp(("arbitrary",)),
        name="moe_combine",
    )(*args)


def _lookup(table, idx):
    e = jnp.arange(table.shape[0], dtype=I32).reshape((-1,) + (1,) * idx.ndim)
    return jnp.sum(jnp.where(idx[None] == e, table.reshape(e.shape), 0), axis=0)


def _tile_flat(slots, tm):
    k, t = slots.shape
    return slots.reshape(k, t // tm, tm).transpose(1, 0, 2).reshape(-1)


def moe_layer(x, mod, norm_gain, router_w, router_b, exp_w1, exp_w3, exp_w2, sw1, sw3, sw2, layer, n_tok,
              final_gain=None):
    bm = EXP_BM
    n_blocks = -(-n_tok * TOP_K // bm) + N_EXPERTS
    hx, hx_packed = norm_mod(x, norm_gain, mod, 3, n_tok, pack=True)
    eidx, wts, pos, counts = moe_router(hx, router_w, router_b, n_tok)
    shared = shared_expert(hx, sw1.astype(BF16), sw3.astype(BF16), sw2.astype(BF16), n_tok)
    cnt = counts.astype(I32)
    padded = (cnt + bm - 1) // bm * bm
    pad_end = jnp.cumsum(padded)
    pad_start = pad_end - padded
    slots = _lookup(pad_start, eidx) + pos
    blk_row = jnp.arange(n_blocks, dtype=I32) * bm
    block_e = jnp.minimum(jnp.sum((pad_end[:, None] <= blk_row[None, :]).astype(I32), axis=0), N_EXPERTS - 1)
    valid = jnp.clip(_lookup(cnt, block_e) - (blk_row - _lookup(pad_start, block_e)), 0, bm).astype(I32)
    n_used = (pad_end[-1:] // bm).astype(I32)
    prev_e = jnp.concatenate([jnp.full((1,), -1, I32), block_e[:-1]])
    first = ((blk_row < pad_end[-1]) & (block_e != prev_e)).astype(I32)
    stage_slot = ((jnp.cumsum(first) - 1) % 2).astype(I32)
    eids = jnp.arange(N_EXPERTS, dtype=I32)
    later = jnp.where((eids[None, :] > eids[:, None]) & (padded[None, :] > 0), eids[None, :], N_EXPERTS)
    next_nonempty = jnp.min(later, axis=1)
    next_nonempty = jnp.where(next_nonempty == N_EXPERTS, -1, next_nonempty)
    next_e = _lookup(next_nonempty, block_e).astype(I32)
    xs = moe_dispatch(_tile_flat(slots, DISPATCH_TM), hx_packed, n_tok, n_blocks * bm)
    y = moe_experts(block_e, valid, n_used, first, next_e, stage_slot, xs, exp_w1, exp_w3, exp_w2, layer, n_blocks)
    return moe_combine(_tile_flat(slots, COMBINE_TM), wts.T, shared, x, mod, y, n_tok, final_gain)


def _rope_tables():
    t = jnp.arange(S, dtype=I32)
    row = (t // GRID_W).astype(F32)
    col = (t % GRID_W).astype(F32)
    n_freq = HD // 4
    inv_freq = ROPE_THETA ** (-jnp.arange(n_freq, dtype=F32) / n_freq)
    ang = jnp.concatenate([row[:, None] * inv_freq, col[:, None] * inv_freq], axis=-1)
    cosf = jnp.repeat(jnp.cos(ang), 2, axis=-1)
    sinf = jnp.stack([-jnp.sin(ang), jnp.sin(ang)], axis=-1).reshape(S, HD)
    return cosf, sinf


def kernel(x, c, ctx, c_ctx, ada_w, ada_b, norm_mix, norm_ffn, attn_w_in, attn_w_out, attn_rpb, attn_q_gain,
           attn_k_gain, ml_w_in, ml_w_out, ml_gate_b, ml_head_gain, router_w, router_b, exp_w1, exp_w3, exp_w2,
           sh_w1, sh_w3, sh_w2, final_norm_gain):
    depth = ada_w.shape[0]
    xa = jnp.concatenate([x.reshape(T_LAT, D), ctx.reshape(T_CTX, D)], axis=0)
    cvec = jnp.concatenate([c, c_ctx[None], jnp.zeros((8 - B - 1, D), F32)], axis=0)
    mod_all = ada_ln(cvec, ada_w, ada_b).reshape(depth, 8, 6, D)
    cosf, sinf = _rope_tables()

    mod = mod_all[0]
    p = norm_matmul(xa, norm_mix[0], mod, attn_w_in[0].astype(BF16), emit_h=False)
    o_na = neighborhood_attention(p, na_bias_table(attn_rpb[0]))
    o_gqa = gqa_attention(p, cosf, sinf, attn_q_gain[0], attn_k_gain[0])
    o_ctx = ctx_attention(p, attn_q_gain[0], attn_k_gain[0])
    o_all = jnp.concatenate([jnp.concatenate([o_na, o_gqa], axis=1), o_ctx], axis=0)
    xa = matmul_gated_residual(o_all, attn_w_out[0].astype(BF16), xa, mod, 2)
    xa = moe_layer(xa, mod, norm_ffn[0], router_w[0], router_b[0], exp_w1, exp_w3, exp_w2,
                   sh_w1[0], sh_w3[0], sh_w2[0], 0, T_ALL)

    mod = mod_all[1]
    w_in = ml_w_in[0]
    p, hx = norm_matmul(xa, norm_mix[1], mod, w_in[:, :ML_MAIN].astype(BF16), emit_h=True)
    col3, row4 = mlstm_gates(hx, w_in[:, ML_MAIN:], ml_gate_b[0])
    hdir = mlstm_scan(p, col3, row4)
    a = mlstm_readout(hdir, p, ml_head_gain[0])
    xl = matmul_gated_residual(a, ml_w_out[0].astype(BF16), xa, mod, 2)
    xl = moe_layer(xl, mod, norm_ffn[1], router_w[1], router_b[1], exp_w1, exp_w3, exp_w2,
                   sh_w1[1], sh_w3[1], sh_w2[1], 1, T_LAT, final_gain=final_norm_gain)
    return xl.reshape(B, S, D)
```

```python
import functools

import jax
import jax.numpy as jnp
from jax import lax
from jax.experimental import pallas as pl
from jax.experimental.pallas import tpu as pltpu

F32 = jnp.float32
BF16 = jnp.bfloat16
I32 = jnp.int32
U32 = jnp.uint32

D = 2048
B = 4
S = 4096
L = 256
T_LAT = B * S
T_CTX = B * L
T_ALL = T_LAT + T_CTX
GRID_W = 64
ROWS = S // GRID_W
HD = 128
NA_HEADS = 8
NA_WIN_ROWS = 8
NA_WIN_COLS = 16
GQA_Q_HEADS = 8
GQA_KV_HEADS = 2
GQA_GROUP = 4
ROPE_THETA = 10000.0
ATTN_IN = 4608
ML_HEADS = 8
ML_V = 256
ML_QK = 128
ML_MAIN = 6144
N_EXPERTS = 64
TOP_K = 8
N_GROUPS = 8
TOPK_GROUPS = 4
EXPERT_DIM = 512
ROUTED_SCALE = 2.5
EPS = 1e-6
NEG_INF = -1e30
ATT_SCALE = HD ** -0.5
LOG2E = 1.4426950408889634
ML_KSCALE = ML_QK ** -0.5

LANE = 128
NA_QROWS = 4
NA_SLAB = NA_QROWS + NA_WIN_ROWS - 1
NA_QB = NA_QROWS * GRID_W
NA_KB = NA_SLAB * GRID_W
ML_CH = 256
ML_HPS = 2
EXP_BM = 512
PK_W = D // 2
VMEM_LIMIT = 48 * 1024 * 1024


def _cp(sem, vmem=VMEM_LIMIT):
    return pltpu.CompilerParams(dimension_semantics=sem, vmem_limit_bytes=vmem)


def _pack_bf16_pairs(xb):
    u = pltpu.bitcast(xb.astype(F32), U32)
    return (u[:, PK_W:] & jnp.uint32(0xFFFF0000)) | (u[:, :PK_W] >> 16)


def _unpack_bf16_pairs(u):
    return pltpu.bitcast(u << 16, F32), pltpu.bitcast(u & jnp.uint32(0xFFFF0000), F32)


def _mod_row(start_row):
    return jnp.where(start_row < T_LAT, start_row // S, B)


def _ada_kernel(c_ref, w_ref, b_ref, o_ref):
    c = c_ref[...]
    a = (c * jax.nn.sigmoid(c)).astype(BF16)
    w = w_ref[0].astype(BF16)
    o_ref[0] = jnp.dot(a, w, preferred_element_type=F32) + b_ref[0]


def ada_ln(cvec, ada_w, ada_b):
    depth = ada_w.shape[0]
    n = ada_w.shape[2]
    tn = 1024
    return pl.pallas_call(
        _ada_kernel,
        grid=(depth, n // tn),
        in_specs=[pl.BlockSpec((8, D), lambda l, j: (0, 0)),
                  pl.BlockSpec((1, D, tn), lambda l, j: (l, 0, j)),
                  pl.BlockSpec((1, 1, tn), lambda l, j: (l, 0, j))],
        out_specs=pl.BlockSpec((1, 8, tn), lambda l, j: (l, 0, j)),
        out_shape=jax.ShapeDtypeStruct((depth, 8, n), F32),
        compiler_params=_cp(("arbitrary", "arbitrary")),
        name="ada_ln",
    )(cvec, ada_w, ada_b.reshape(depth, 1, n))


def _norm_mod_kernel(x_ref, g_ref, mod_ref, *out_refs, base, pack):
    x = x_ref[...]
    y = x * lax.rsqrt(jnp.mean(x * x, axis=-1, keepdims=True) + EPS) * g_ref[...]
    m = mod_ref[0]
    h = y * (1.0 + m[base + 1:base + 2]) + m[base:base + 1]
    hb = h.astype(BF16)
    out_refs[0][...] = hb
    if pack:
        out_refs[1][...] = _pack_bf16_pairs(hb)


def norm_mod(x, gain, mod, base, n_rows, pack):
    tm = 256
    out_shape = [jax.ShapeDtypeStruct((n_rows, D), BF16)]
    out_specs = [pl.BlockSpec((tm, D), lambda i: (i, 0))]
    if pack:
        out_shape.append(jax.ShapeDtypeStruct((n_rows, PK_W), U32))
        out_specs.append(pl.BlockSpec((tm, PK_W), lambda i: (i, 0)))
    res = pl.pallas_call(
        functools.partial(_norm_mod_kernel, base=base, pack=pack),
        grid=(n_rows // tm,),
        in_specs=[pl.BlockSpec((tm, D), lambda i: (i, 0)),
                  pl.BlockSpec((1, D), lambda i: (0, 0)),
                  pl.BlockSpec((1, 6, D), lambda i: (_mod_row(i * tm), 0, 0))],
        out_specs=out_specs,
        out_shape=out_shape,
        compiler_params=_cp(("arbitrary",)),
        name="norm_mod",
    )(x, gain.reshape(1, D), mod)
    return res if pack else res[0]


def _norm_mm_kernel(x_ref, g_ref, mod_ref, w_ref, o_ref, *rest, emit_h):
    hb_ref = rest[-1]

    @pl.when(pl.program_id(1) == 0)
    def _():
        x = x_ref[...]
        y = x * lax.rsqrt(jnp.mean(x * x, axis=-1, keepdims=True) + EPS) * g_ref[...]
        m = mod_ref[0]
        hb_ref[...] = (y * (1.0 + m[1:2]) + m[0:1]).astype(BF16)
        if emit_h:
            rest[0][...] = hb_ref[...]

    o_ref[...] = jnp.dot(hb_ref[...], w_ref[...], preferred_element_type=F32).astype(o_ref.dtype)


def norm_matmul(x, gain, mod, w, emit_h, tm=1024, tn=512):
    m = x.shape[0]
    n = w.shape[1]
    out_shape = [jax.ShapeDtypeStruct((m, n), BF16)]
    out_specs = [pl.BlockSpec((tm, tn), lambda i, j: (i, j))]
    if emit_h:
        out_shape.append(jax.ShapeDtypeStruct((m, D), BF16))
        out_specs.append(pl.BlockSpec((tm, D), lambda i, j: (i, 0)))
    res = pl.pallas_call(
        functools.partial(_norm_mm_kernel, emit_h=emit_h),
        grid=(m // tm, n // tn),
        in_specs=[pl.BlockSpec((tm, D), lambda i, j: (i, 0)),
                  pl.BlockSpec((1, D), lambda i, j: (0, 0)),
                  pl.BlockSpec((1, 6, D), lambda i, j: (_mod_row(i * tm), 0, 0)),
                  pl.BlockSpec((D, tn), lambda i, j: (0, j))],
        out_specs=out_specs,
        out_shape=out_shape,
        scratch_shapes=[pltpu.VMEM((tm, D), BF16)],
        compiler_params=_cp(("arbitrary", "arbitrary")),
        name="norm_matmul",
    )(x, gain.reshape(1, D), mod, w)
    return res if emit_h else res[0]


def _mm_res_kernel(a_ref, w_ref, x_ref, mod_ref, o_ref, *, slot):
    acc = jnp.dot(a_ref[...], w_ref[...], preferred_element_type=F32)
    o_ref[...] = x_ref[...] + mod_ref[0][slot:slot + 1] * acc


def matmul_gated_residual(a, w, x, mod, slot, tm=1024, tn=512):
    m, k = a.shape
    n = w.shape[1]
    return pl.pallas_call(
        functools.partial(_mm_res_kernel, slot=slot),
        grid=(m // tm, n // tn),
        in_specs=[pl.BlockSpec((tm, k), lambda i, j: (i, 0)),
                  pl.BlockSpec((k, tn), lambda i, j: (0, j)),
                  pl.BlockSpec((tm, tn), lambda i, j: (i, j)),
                  pl.BlockSpec((1, 6, tn), lambda i, j: (_mod_row(i * tm), 0, j))],
        out_specs=pl.BlockSpec((tm, tn), lambda i, j: (i, j)),
        out_shape=jax.ShapeDtypeStruct((m, n), F32),
        compiler_params=_cp(("arbitrary", "arbitrary")),
        name="matmul_gated_residual",
    )(a, w, x, mod)


def _dot_nt(a, b):
    return lax.dot_general(a, b, (((1,), (1,)), ((), ())), preferred_element_type=F32)


def _rms_head(x, gain):
    return x * lax.rsqrt(jnp.mean(x * x, axis=-1, keepdims=True) + EPS) * gain


def _rope(x, cosf, sinf):
    lane = lax.broadcasted_iota(I32, x.shape, 1)
    nxt = pltpu.roll(x, LANE - 1, 1)
    prv = pltpu.roll(x, 1, 1)
    return x * cosf + jnp.where((lane & 1) == 0, nxt, prv) * sinf


def _softmax_av(parts):
    m = functools.reduce(jnp.maximum, [jnp.max(s, axis=-1, keepdims=True) for s, _ in parts])
    l = None
    o = None
    for s, v in parts:
        p = jnp.exp(s - m)
        li = jnp.sum(p, axis=-1, keepdims=True)
        oi = jnp.dot(p.astype(BF16), v, preferred_element_type=F32)
        l = li if l is None else l + li
        o = oi if o is None else o + oi
    return o / l


def _na_kernel(q_ref, k_ref, v_ref, kc_ref, vc_ref, tab_ref, o_ref):
    kc = kc_ref[...]
    vc = vc_ref[...]
    n_blocks = ROWS // NA_QROWS

    def body(j, carry):
        ks = jnp.clip(j * NA_QROWS - NA_WIN_ROWS // 2, 0, ROWS - NA_SLAB)
        typ = jnp.where(j == 0, 0, jnp.where(j == n_blocks - 1, 2, 1))
        qs = pl.multiple_of(j * NA_QB, NA_QB)
        kst = pl.multiple_of(ks * GRID_W, GRID_W)
        q = q_ref[pl.ds(qs, NA_QB), :]
        k = k_ref[pl.ds(kst, NA_KB), :]
        v = v_ref[pl.ds(kst, NA_KB), :]
        s_win = _dot_nt(q, k) * ATT_SCALE + tab_ref[typ, 0]
        s_ctx = _dot_nt(q, kc) * ATT_SCALE
        o_ref[pl.ds(qs, NA_QB), :] = _softmax_av([(s_win, v), (s_ctx, vc)]).astype(BF16)
        return carry

    lax.fori_loop(0, n_blocks, body, 0)


def na_bias_table(rpb):
    def one(r0, ks):
        r = r0 + jnp.arange(NA_QROWS)
        kr = ks + jnp.arange(NA_SLAB)
        start = jnp.clip(r - NA_WIN_ROWS // 2, 0, ROWS - NA_WIN_ROWS)
        row_ok = (kr[None, :] >= start[:, None]) & (kr[None, :] < start[:, None] + NA_WIN_ROWS)
        row_idx = jnp.clip(kr[None, :] - r[:, None] + NA_WIN_ROWS - 1, 0, 2 * NA_WIN_ROWS - 2)
        cq = jnp.arange(GRID_W)
        col_start = jnp.clip(cq - NA_WIN_COLS // 2, 0, GRID_W - NA_WIN_COLS)
        col_ok = (cq[None, :] >= col_start[:, None]) & (cq[None, :] < col_start[:, None] + NA_WIN_COLS)
        col_idx = jnp.clip(cq[None, :] - cq[:, None] + NA_WIN_COLS - 1, 0, 2 * NA_WIN_COLS - 2)
        r_hot = jax.nn.one_hot(row_idx, 2 * NA_WIN_ROWS - 1, dtype=F32)
        c_hot = jax.nn.one_hot(col_idx, 2 * NA_WIN_COLS - 1, dtype=F32)
        bias = jnp.einsum('qka,hab,xyb->hqxky', r_hot, rpb.astype(F32), c_hot, precision=lax.Precision.HIGHEST)
        ok = row_ok[:, None, :, None] & col_ok[None, :, None, :]
        return jnp.where(ok[None], bias, NEG_INF).reshape(NA_HEADS, NA_QB, NA_KB)

    mid = 2 * NA_QROWS
    last = ROWS - NA_QROWS
    return jnp.stack([one(0, 0), one(mid, mid - NA_WIN_ROWS // 2), one(last, ROWS - NA_SLAB)])


def neighborhood_attention(p, table):
    cb = S // L
    return pl.pallas_call(
        _na_kernel,
        grid=(NA_HEADS, B),
        in_specs=[pl.BlockSpec((S, HD), lambda h, b: (b, h)),
                  pl.BlockSpec((S, HD), lambda h, b: (b, NA_HEADS + h)),
                  pl.BlockSpec((S, HD), lambda h, b: (b, 2 * NA_HEADS + h)),
                  pl.BlockSpec((L, HD), lambda h, b: (B * cb + b, NA_HEADS + h)),
                  pl.BlockSpec((L, HD), lambda h, b: (B * cb + b, 2 * NA_HEADS + h)),
                  pl.BlockSpec((3, 1, NA_QB, NA_KB), lambda h, b: (0, h, 0, 0))],
        out_specs=pl.BlockSpec((S, HD), lambda h, b: (b, h)),
        out_shape=jax.ShapeDtypeStruct((T_LAT, NA_HEADS * HD), BF16),
        compiler_params=_cp(("arbitrary", "arbitrary")),
        name="neighborhood_attention",
    )(p, p, p, p, p, table)


GQA_TQ = 256
GQA_CK = 512
GQA_QCOL = 3 * NA_HEADS
GQA_KCOL = GQA_QCOL + GQA_Q_HEADS
GQA_VCOL = GQA_KCOL + GQA_KV_HEADS


def _gqa_kernel(q_ref, k_ref, v_ref, kc_ref, vc_ref, cq_ref, sq_ref, ck_ref, sk_ref, qg_ref, kg_ref,
                o_ref, kn_ref, kcn_ref):
    @pl.when(pl.program_id(2) == 0)
    def _():
        kn = _rope(_rms_head(k_ref[...].astype(F32), kg_ref[...]), ck_ref[...], sk_ref[...])
        kn_ref[...] = kn.astype(BF16)
        kcn_ref[...] = _rms_head(kc_ref[...].astype(F32), kg_ref[...]).astype(BF16)

    cos = cq_ref[...]
    sin = sq_ref[...]
    heads = []
    for g in range(GQA_GROUP):
        qh = _rope(_rms_head(q_ref[:, g * HD:(g + 1) * HD].astype(F32), qg_ref[...]), cos, sin)
        heads.append((qh * (ATT_SCALE * LOG2E)).astype(BF16))
    q = jnp.concatenate(heads, axis=0)
    chunks = [(kn_ref[c * GQA_CK:(c + 1) * GQA_CK, :], v_ref[c * GQA_CK:(c + 1) * GQA_CK, :])
              for c in range(S // GQA_CK)]
    chunks.append((kcn_ref[...], vc_ref[...]))
    m = l = acc = None
    for kk, vv in chunks:
        s = _dot_nt(q, kk)
        mc = jnp.max(s, axis=-1, keepdims=True)
        if m is None:
            m_new = mc
            p = jnp.exp2(s - m_new)
            l = jnp.sum(p, axis=-1, keepdims=True)
            acc = jnp.dot(p.astype(BF16), vv, preferred_element_type=F32)
        else:
            m_new = jnp.maximum(m, mc)
            alpha = jnp.exp2(m - m_new)
            p = jnp.exp2(s - m_new)
            l = alpha * l + jnp.sum(p, axis=-1, keepdims=True)
            acc = alpha * acc + jnp.dot(p.astype(BF16), vv, preferred_element_type=F32)
        m = m_new
    o = acc / l
    for g in range(GQA_GROUP):
        o_ref[:, g * HD:(g + 1) * HD] = o[g * GQA_TQ:(g + 1) * GQA_TQ].astype(BF16)


def gqa_attention(p, cosf, sinf, q_gain, k_gain):
    nq = S // GQA_TQ
    cb = S // L
    gw = GQA_GROUP * HD
    return pl.pallas_call(
        _gqa_kernel,
        grid=(B, GQA_KV_HEADS, nq),
        in_specs=[pl.BlockSpec((GQA_TQ, gw), lambda b, n, i: (b * nq + i, GQA_QCOL // GQA_GROUP + n)),
                  pl.BlockSpec((S, HD), lambda b, n, i: (b, GQA_KCOL + n)),
                  pl.BlockSpec((S, HD), lambda b, n, i: (b, GQA_VCOL + n)),
                  pl.BlockSpec((L, HD), lambda b, n, i: (B * cb + b, GQA_KCOL + n)),
                  pl.BlockSpec((L, HD), lambda b, n, i: (B * cb + b, GQA_VCOL + n)),
                  pl.BlockSpec((GQA_TQ, HD), lambda b, n, i: (i, 0)),
                  pl.BlockSpec((GQA_TQ, HD), lambda b, n, i: (i, 0)),
                  pl.BlockSpec((S, HD), lambda b, n, i: (0, 0)),
                  pl.BlockSpec((S, HD), lambda b, n, i: (0, 0)),
                  pl.BlockSpec((1, HD), lambda b, n, i: (0, 0)),
                  pl.BlockSpec((1, HD), lambda b, n, i: (0, 0))],
        out_specs=pl.BlockSpec((GQA_TQ, gw), lambda b, n, i: (b * nq + i, n)),
        out_shape=jax.ShapeDtypeStruct((T_LAT, GQA_Q_HEADS * HD), BF16),
        scratch_shapes=[pltpu.VMEM((S, HD), BF16), pltpu.VMEM((L, HD), BF16)],
        compiler_params=_cp(("arbitrary",) * 3),
        name="gqa_attention",
    )(p, p, p, p, p, cosf, sinf, cosf, sinf, q_gain.reshape(1, HD), k_gain.reshape(1, HD))


def _ctx_attn_kernel(p_ref, qg_ref, kg_ref, o_ref):
    def col(c):
        return p_ref[:, c * HD:(c + 1) * HD]

    for h in range(NA_HEADS):
        s = _dot_nt(col(h), col(NA_HEADS + h)) * ATT_SCALE
        o_ref[:, h * HD:(h + 1) * HD] = _softmax_av([(s, col(2 * NA_HEADS + h))]).astype(BF16)
    for n in range(GQA_KV_HEADS):
        kn = _rms_head(col(GQA_KCOL + n).astype(F32), kg_ref[...]).astype(BF16)
        v = col(GQA_VCOL + n)
        for g in range(GQA_GROUP):
            h = n * GQA_GROUP + g
            qn = _rms_head(col(GQA_QCOL + h).astype(F32), qg_ref[...]).astype(BF16)
            s = _dot_nt(qn, kn) * ATT_SCALE
            o_ref[:, (NA_HEADS + h) * HD:(NA_HEADS + h + 1) * HD] = _softmax_av([(s, v)]).astype(BF16)


def ctx_attention(p, q_gain, k_gain):
    cb = S // L
    return pl.pallas_call(
        _ctx_attn_kernel,
        grid=(B,),
        in_specs=[pl.BlockSpec((L, ATTN_IN), lambda b: (B * cb + b, 0)),
                  pl.BlockSpec((1, HD), lambda b: (0, 0)),
                  pl.BlockSpec((1, HD), lambda b: (0, 0))],
        out_specs=pl.BlockSpec((L, D), lambda b: (b, 0)),
        out_shape=jax.ShapeDtypeStruct((T_CTX, D), BF16),
        compiler_params=_cp(("arbitrary",)),
        name="ctx_attention",
    )(p, q_gain.reshape(1, HD), k_gain.reshape(1, HD))


def _log_sigmoid(x):
    return -(jnp.maximum(-x, 0.0) + jnp.log1p(jnp.exp(-jnp.abs(x))))


def _dot_hi(a, b):
    return jnp.dot(a, b, precision=lax.Precision.HIGHEST, preferred_element_type=F32)


def _gate_kernel(h_ref, wg_ref, wgt_ref, b_ref, bt_ref, lt_ref, ut_ref, col_ref, row_ref):
    nh = ML_HEADS
    hx = h_ref[...]
    g = jnp.dot(hx, wg_ref[...], preferred_element_type=F32) + b_ref[...]
    gt = _dot_nt(wgt_ref[...], hx) + bt_ref[...]
    li = g[:, 0:2 * nh]
    lf = _log_sigmoid(g[:, 2 * nh:4 * nh])
    lit = gt[0:2 * nh]
    lft = _log_sigmoid(gt[2 * nh:4 * nh])
    lt = lt_ref[...]
    ut = ut_ref[...]
    lane = lax.broadcasted_iota(I32, lf.shape, 1)
    bc = jnp.where(lane < nh, _dot_hi(lt, lf), _dot_hi(ut, lf))
    tot = jnp.sum(lf, axis=0, keepdims=True)
    aend = tot - bc + li
    col_ref[...] = jnp.concatenate([bc, aend, jnp.zeros((ML_CH, LANE - 4 * nh), F32)], axis=1)
    sub = lax.broadcasted_iota(I32, lft.shape, 0)
    bct = jnp.where(sub < nh, _dot_hi(lft, ut), _dot_hi(lft, lt))
    tott = jnp.sum(lft, axis=1, keepdims=True)
    gtr = lit - bct
    row_ref[0] = jnp.concatenate([bct, gtr, tott + gtr, jnp.broadcast_to(tott, bct.shape)], axis=0)


def mlstm_gates(hx, wg, gate_b):
    nh = ML_HEADS
    n_ch = T_ALL // ML_CH
    wg_pad = jnp.zeros((D, LANE), BF16).at[:, :4 * nh].set(wg.astype(BF16))
    b_pad = jnp.zeros((1, LANE), F32).at[0, :4 * nh].set(gate_b.reshape(-1))
    wgt = wg.astype(BF16).T
    bt = gate_b.reshape(4 * nh, 1).astype(F32)
    lt = jnp.tril(jnp.ones((ML_CH, ML_CH), F32))
    ut = jnp.triu(jnp.ones((ML_CH, ML_CH), F32))
    col, row = pl.pallas_call(
        _gate_kernel,
        grid=(n_ch,),
        in_specs=[pl.BlockSpec((ML_CH, D), lambda i: (i, 0)),
                  pl.BlockSpec((D, LANE), lambda i: (0, 0)),
                  pl.BlockSpec((4 * nh, D), lambda i: (0, 0)),
                  pl.BlockSpec((1, LANE), lambda i: (0, 0)),
                  pl.BlockSpec((4 * nh, 1), lambda i: (0, 0)),
                  pl.BlockSpec((ML_CH, ML_CH), lambda i: (0, 0)),
                  pl.BlockSpec((ML_CH, ML_CH), lambda i: (0, 0))],
        out_specs=[pl.BlockSpec((ML_CH, LANE), lambda i: (i, 0)),
                   pl.BlockSpec((1, 8 * nh, ML_CH), lambda i: (i, 0, 0))],
        out_shape=[jax.ShapeDtypeStruct((T_ALL, LANE), F32),
                   jax.ShapeDtypeStruct((n_ch, 8 * nh, ML_CH), F32)],
        compiler_params=_cp(("arbitrary",)),
        name="mlstm_gates",
    )(hx, wg_pad, wgt, b_pad, bt, lt, ut)
    col3 = col[:, :4 * nh].reshape(T_ALL, 2, 2 * nh).transpose(2, 0, 1)
    col3 = jnp.pad(col3, ((0, 0), (0, 0), (0, 6)))
    row4 = row.reshape(n_ch, 4, 2 * nh, ML_CH).transpose(2, 0, 1, 3)
    row4 = jnp.pad(row4, ((0, 0), (0, 0), (0, 4), (0, 0)))
    return col3, row4


def _mlstm_step(d, hh, q_ref, k_ref, v_ref, col_ref, row_ref, o_ref, c_ref, n_ref, m_ref):
    sl = d * ML_HPS + hh
    q = q_ref[:, hh * ML_QK:(hh + 1) * ML_QK]
    kf = k_ref[:, hh * ML_QK:(hh + 1) * ML_QK].astype(F32) * ML_KSCALE
    kb = kf.astype(BF16)
    v = v_ref[:, hh * ML_V:(hh + 1) * ML_V]
    col = col_ref[hh]
    row = row_ref[hh, 0]
    bc_col = col[:, 0:1]
    aend_col = col[:, 1:2]
    g_row = row[1:2]
    aend_row = row[2:3]
    btot = row[3:4, 0:1]
    m_st = m_ref[sl]
    c_st = c_ref[sl]
    n_st = n_ref[sl]
    m_new = jnp.maximum(btot + m_st, jnp.max(aend_row, axis=1, keepdims=True))

    r = lax.broadcasted_iota(I32, (ML_CH, ML_CH), 0)
    c = lax.broadcasted_iota(I32, (ML_CH, ML_CH), 1)
    causal = (r >= c) if d == 0 else (r <= c)
    d_mat = jnp.where(causal, bc_col + g_row, -jnp.inf)
    m_row = jnp.maximum(bc_col + m_st, jnp.max(d_mat, axis=1, keepdims=True))
    w_inter = jnp.exp(bc_col + m_st - m_row)
    s_mat = _dot_nt(q, kb) * jnp.exp(d_mat - m_row)
    num = (w_inter * jnp.dot(q, c_st.astype(BF16), preferred_element_type=F32)
           + jnp.dot(s_mat.astype(BF16), v, preferred_element_type=F32))
    den = (w_inter * jnp.sum(q.astype(F32) * n_st, axis=1, keepdims=True)
           + jnp.sum(s_mat, axis=1, keepdims=True))
    o_ref[:, hh * ML_V:(hh + 1) * ML_V] = num / jnp.maximum(jnp.abs(den), jnp.exp(-m_row))

    w_end_col = jnp.exp(aend_col - m_new)
    w_end_row = jnp.exp(aend_row - m_new)
    decay = jnp.exp(btot + m_st - m_new)
    kw = (kf * w_end_col).astype(BF16)
    c_ref[sl] = decay * c_st + lax.dot_general(kw, v, (((0,), (0,)), ((), ())), preferred_element_type=F32)
    w8 = jnp.broadcast_to(w_end_row, (8, ML_CH)).astype(BF16)
    n_ref[sl] = decay * n_st + jnp.dot(w8, kb, preferred_element_type=F32)[0:1]
    m_ref[sl] = m_new


def _mlstm_kernel(qf, kf, vf, colf, rowf, qb, kb, vb, colb, rowb, of, ob, c_ref, n_ref, m_ref):
    @pl.when(pl.program_id(2) == 0)
    def _():
        c_ref[...] = jnp.zeros_like(c_ref)
        n_ref[...] = jnp.zeros_like(n_ref)
        m_ref[...] = jnp.zeros_like(m_ref)

    for hh in range(ML_HPS):
        _mlstm_step(0, hh, qf, kf, vf, colf, rowf, of, c_ref, n_ref, m_ref)
        _mlstm_step(1, hh, qb, kb, vb, colb, rowb, ob, c_ref, n_ref, m_ref)


def mlstm_scan(p, col3, row4):
    n_lat = S // ML_CH
    steps = n_lat + 1
    lat_blocks = T_LAT // ML_CH
    hps = ML_HPS
    n_groups = ML_HEADS // hps
    kcol = (ML_HEADS * ML_QK) // (hps * ML_QK)
    vcol = (2 * ML_HEADS * ML_QK) // (hps * ML_V)

    def chunk(b, d, st):
        c = (st - 1) if d == 0 else (n_lat - st)
        return jnp.where(st == 0, lat_blocks + b, b * n_lat + c)

    def out_chunk(b, d, st):
        s1 = jnp.maximum(st, 1)
        return b * n_lat + ((s1 - 1) if d == 0 else (n_lat - s1))

    def dir_specs(d):
        return [pl.BlockSpec((ML_CH, hps * ML_QK), lambda b, h, s: (chunk(b, d, s), h)),
                pl.BlockSpec((ML_CH, hps * ML_QK), lambda b, h, s: (chunk(b, d, s), kcol + h)),
                pl.BlockSpec((ML_CH, hps * ML_V), lambda b, h, s: (chunk(b, d, s), vcol + h)),
                pl.BlockSpec((hps, ML_CH, 8), lambda b, h, s: (d * n_groups + h, chunk(b, d, s), 0)),
                pl.BlockSpec((hps, 1, 8, ML_CH), lambda b, h, s: (d * n_groups + h, chunk(b, d, s), 0, 0))]

    return pl.pallas_call(
        _mlstm_kernel,
        grid=(B, n_groups, steps),
        in_specs=dir_specs(0) + dir_specs(1),
        out_specs=[pl.BlockSpec((ML_CH, hps * ML_V), lambda b, h, s: (out_chunk(b, 0, s), h)),
                   pl.BlockSpec((ML_CH, hps * ML_V), lambda b, h, s: (out_chunk(b, 1, s), h))],
        out_shape=[jax.ShapeDtypeStruct((T_LAT, ML_HEADS * ML_V), F32),
                   jax.ShapeDtypeStruct((T_LAT, ML_HEADS * ML_V), F32)],
        scratch_shapes=[pltpu.VMEM((2 * hps, ML_QK, ML_V), F32), pltpu.VMEM((2 * hps, 1, ML_QK), F32),
                        pltpu.VMEM((2 * hps, 1, 1), F32)],
        compiler_params=_cp(("arbitrary",) * 3),
        name="mlstm_scan",
    )(p, p, p, col3, row4, p, p, p, col3, row4)


def _readout_kernel(hf_ref, hb_ref, o_ref, g_ref, a_ref):
    hs = hf_ref[...] + hb_ref[...]
    for h in range(ML_HEADS):
        sl = slice(h * ML_V, (h + 1) * ML_V)
        x = hs[:, sl]
        hn = x * lax.rsqrt(jnp.mean(x * x, axis=-1, keepdims=True) + EPS) * g_ref[:, sl]
        a_ref[:, sl] = (hn * jax.nn.sigmoid(o_ref[:, sl].astype(F32))).astype(BF16)


def mlstm_readout(hdir, p, head_gain):
    tm = 256
    ocol = (2 * ML_HEADS * ML_QK + ML_HEADS * ML_V) // D
    return pl.pallas_call(
        _readout_kernel,
        grid=(T_LAT // tm,),
        in_specs=[pl.BlockSpec((tm, D), lambda i: (i, 0)),
                  pl.BlockSpec((tm, D), lambda i: (i, 0)),
                  pl.BlockSpec((tm, D), lambda i: (i, ocol)),
                  pl.BlockSpec((1, D), lambda i: (0, 0))],
        out_specs=pl.BlockSpec((tm, D), lambda i: (i, 0)),
        out_shape=jax.ShapeDtypeStruct((T_LAT, D), BF16),
        compiler_params=_cp(("arbitrary",)),
        name="mlstm_readout",
    )(hdir[0], hdir[1], p, head_gain.reshape(1, D))


ROUTER_TM = 512


def _router_kernel(h_ref, w_ref, rb_ref, erow_ref, tri_ref, eidx_ref, wts_ref, pos_ref, cnt_ref, carry_ref):
    ng = N_GROUPS
    epg = N_EXPERTS // N_GROUPS
    tm = ROUTER_TM
    ninf = -jnp.inf

    @pl.when(pl.program_id(0) == 0)
    def _():
        carry_ref[...] = jnp.zeros_like(carry_ref)

    s = jax.nn.sigmoid(_dot_nt(w_ref[...], h_ref[...]))
    ssel = s + rb_ref[...]
    sraw = [s[ng * j:ng * (j + 1)] for j in range(epg)]
    slab = [ssel[ng * j:ng * (j + 1)] for j in range(epg)]
    m1 = functools.reduce(jnp.maximum, slab)
    jfirst = functools.reduce(jnp.minimum, [jnp.where(slab[j] == m1, j, epg) for j in range(epg)])
    m2 = functools.reduce(jnp.maximum, [jnp.where(jfirst == j, ninf, slab[j]) for j in range(epg)])
    gs = m1 + m2
    giota = lax.broadcasted_iota(I32, (ng, tm), 0)
    gsel = jnp.zeros((ng, tm), F32)
    for _ in range(TOPK_GROUPS):
        mx = jnp.max(gs, axis=0, keepdims=True)
        gi = jnp.min(jnp.where(gs == mx, giota, ng), axis=0, keepdims=True)
        hit = giota == gi
        gsel = jnp.where(hit, 1.0, gsel)
        gs = jnp.where(hit, ninf, gs)
    msl = [jnp.where(gsel > 0.0, slab[j], ninf) for j in range(epg)]
    eid = [giota * epg + j for j in range(epg)]
    selm = [jnp.zeros((ng, tm), F32) for _ in range(epg)]
    e_list, w_list = [], []
    for _ in range(TOP_K):
        mx = jnp.max(functools.reduce(jnp.maximum, msl), axis=0, keepdims=True)
        cand = functools.reduce(jnp.minimum, [jnp.where(msl[j] == mx, eid[j], N_EXPERTS) for j in range(epg)])
        esel = jnp.min(cand, axis=0, keepdims=True)
        hits = [eid[j] == esel for j in range(epg)]
        wk = functools.reduce(lambda a, b: a + b, [jnp.where(hits[j], sraw[j], 0.0) for j in range(epg)])
        w_list.append(jnp.sum(wk, axis=0, keepdims=True))
        e_list.append(esel)
        msl = [jnp.where(hits[j], ninf, msl[j]) for j in range(epg)]
        selm = [jnp.where(hits[j], 1.0, selm[j]) for j in range(epg)]
    wsum = functools.reduce(lambda a, b: a + b, w_list)
    wts_ref[...] = jnp.concatenate([w / wsum * ROUTED_SCALE for w in w_list], axis=0)
    eidx_ref[...] = jnp.concatenate(e_list, axis=0)
    sel = jnp.concatenate(selm, axis=0)
    carry = carry_ref[...]
    posfull = jnp.dot(sel.astype(BF16), tri_ref[...], preferred_element_type=F32) + carry
    erow = erow_ref[...]
    pos = [jnp.sum(jnp.where(erow == e, posfull, 0.0), axis=0, keepdims=True) for e in e_list]
    pos_ref[...] = jnp.concatenate(pos, axis=0).astype(I32)
    carry = carry + jnp.sum(sel, axis=1, keepdims=True)
    carry_ref[...] = carry
    cnt_ref[...] = carry


def moe_router(hx, router_w, router_b, n_tok):
    tm = ROUTER_TM
    epg = N_EXPERTS // N_GROUPS
    perm = (jnp.arange(N_EXPERTS) % N_GROUPS) * epg + jnp.arange(N_EXPERTS) // N_GROUPS
    w_t = router_w.astype(BF16).T[perm]
    rb = router_b.astype(F32)[perm].reshape(N_EXPERTS, 1)
    erow = perm.astype(I32).reshape(N_EXPERTS, 1)
    tri = jnp.triu(jnp.ones((tm, tm), BF16), 1)
    eidx, wts, pos, counts = pl.pallas_call(
        _router_kernel,
        grid=(n_tok // tm,),
        in_specs=[pl.BlockSpec((tm, D), lambda i: (i, 0)),
                  pl.BlockSpec((N_EXPERTS, D), lambda i: (0, 0)),
                  pl.BlockSpec((N_EXPERTS, 1), lambda i: (0, 0)),
                  pl.BlockSpec((N_EXPERTS, 1), lambda i: (0, 0)),
                  pl.BlockSpec((tm, tm), lambda i: (0, 0))],
        out_specs=[pl.BlockSpec((TOP_K, tm), lambda i: (0, i)),
                   pl.BlockSpec((TOP_K, tm), lambda i: (0, i)),
                   pl.BlockSpec((TOP_K, tm), lambda i: (0, i)),
                   pl.BlockSpec((N_EXPERTS, 1), lambda i: (0, 0))],
        out_shape=[jax.ShapeDtypeStruct((TOP_K, n_tok), I32),
                   jax.ShapeDtypeStruct((TOP_K, n_tok), F32),
                   jax.ShapeDtypeStruct((TOP_K, n_tok), I32),
                   jax.ShapeDtypeStruct((N_EXPERTS, 1), F32)],
        scratch_shapes=[pltpu.VMEM((N_EXPERTS, 1), F32)],
        compiler_params=_cp(("arbitrary",)),
        name="moe_router",
    )(hx, w_t, rb, erow, tri)
    return eidx, wts, pos, counts.reshape(N_EXPERTS)[perm]


DISPATCH_TM = 512


ROW_UNROLL = 8


def _dispatch_kernel(slot_ref, hx_ref, xs_hbm, sem):
    def issue(tt, carry):
        t8 = pl.multiple_of(tt * ROW_UNROLL, ROW_UNROLL)
        for j in range(ROW_UNROLL):
            src = hx_ref.at[pl.ds(t8 + j, 1), :]
            for k in range(TOP_K):
                slot = slot_ref[k * DISPATCH_TM + j + t8]
                pltpu.make_async_copy(src, xs_hbm.at[pl.ds(slot, 1), :], sem).start(priority=k % 2)
        return carry

    lax.fori_loop(0, DISPATCH_TM // ROW_UNROLL, issue, 0)
    for _ in range(TOP_K):
        pltpu.make_async_copy(hx_ref, xs_hbm.at[pl.ds(0, DISPATCH_TM), :], sem).wait()


def moe_dispatch(slots, hx_packed, n_tok, n_rows):
    tm = DISPATCH_TM
    return pl.pallas_call(
        _dispatch_kernel,
        grid=(n_tok // tm,),
        in_specs=[pl.BlockSpec((TOP_K * tm,), lambda i: (i,), memory_space=pltpu.SMEM),
                  pl.BlockSpec((tm, PK_W), lambda i: (i, 0))],
        out_specs=pl.BlockSpec(memory_space=pl.ANY),
        out_shape=jax.ShapeDtypeStruct((n_rows, PK_W), U32),
        scratch_shapes=[pltpu.SemaphoreType.DMA(())],
        compiler_params=_cp(("arbitrary",)),
        name="moe_dispatch",
    )(slots, hx_packed)


def _expert_kernel(be_ref, valid_ref, nused_ref, first_ref, next_ref, slot_ref,
                   xs_ref, w1_hbm, w3_hbm, w2_hbm, y_ref, w1s, w3s, w2s, w1b, w3b, w2b, xb, sems, *, layer):
    i = pl.program_id(0)
    bm = EXP_BM

    def weight_copies(e, s):
        return [pltpu.make_async_copy(w_hbm.at[layer, e], stage.at[s], sems.at[s, j])
                for j, (w_hbm, stage) in enumerate(((w1_hbm, w1s), (w3_hbm, w3s), (w2_hbm, w2s)))]

    @pl.when(i < nused_ref[0])
    def _():
        @pl.when(first_ref[i] == 1)
        def _():
            s = slot_ref[i]

            @pl.when(i == 0)
            def _():
                for cp in weight_copies(be_ref[0], 0):
                    cp.start()

            for cp in weight_copies(be_ref[i], s):
                cp.wait()

            @pl.when(next_ref[i] >= 0)
            def _():
                for cp in weight_copies(next_ref[i], 1 - s):
                    cp.start()

            w1b[...] = w1s[s].astype(BF16)
            w3b[...] = w3s[s].astype(BF16)
            w2b[...] = w2s[s].astype(BF16)

        rows = lax.broadcasted_iota(I32, (bm, PK_W), 0)
        lo, hi = _unpack_bf16_pairs(jnp.where(rows < valid_ref[i], xs_ref[...], jnp.uint32(0)))
        xb[:, :PK_W] = lo.astype(BF16)
        xb[:, PK_W:] = hi.astype(BF16)
        x = xb[...]
        h1 = jnp.dot(x, w1b[...], preferred_element_type=F32)
        h3 = jnp.dot(x, w3b[...], preferred_element_type=F32)
        a = (h1 * jax.nn.sigmoid(h1) * h3).astype(BF16)
        y = jnp.dot(a, w2b[...], preferred_element_type=F32)
        y_ref[...] = _pack_bf16_pairs(y.astype(BF16))


def moe_experts(block_e, valid, n_used, first, next_e, slot, xs, w1, w3, w2, layer, n_blocks):
    bm = EXP_BM

    def blk(i, be, va, nu, fi, ne, sl):
        return (jnp.minimum(i, nu[0] - 1), 0)

    grid_spec = pltpu.PrefetchScalarGridSpec(
        num_scalar_prefetch=6,
        grid=(n_blocks,),
        in_specs=[pl.BlockSpec((bm, PK_W), blk),
                  pl.BlockSpec(memory_space=pl.ANY),
                  pl.BlockSpec(memory_space=pl.ANY),
                  pl.BlockSpec(memory_space=pl.ANY)],
        out_specs=pl.BlockSpec((bm, PK_W), blk),
        scratch_shapes=[pltpu.VMEM((2, D, EXPERT_DIM), F32), pltpu.VMEM((2, D, EXPERT_DIM), F32),
                        pltpu.VMEM((2, EXPERT_DIM, D), F32),
                        pltpu.VMEM((D, EXPERT_DIM), BF16), pltpu.VMEM((D, EXPERT_DIM), BF16),
                        pltpu.VMEM((EXPERT_DIM, D), BF16), pltpu.VMEM((bm, D), BF16),
                        pltpu.SemaphoreType.DMA((2, 3))],
    )
    return pl.pallas_call(
        functools.partial(_expert_kernel, layer=layer),
        grid_spec=grid_spec,
        out_shape=jax.ShapeDtypeStruct((n_blocks * bm, PK_W), U32),
        compiler_params=_cp(("arbitrary",), vmem=56 * 1024 * 1024),
        name="moe_experts",
    )(block_e, valid, n_used, first, next_e, slot, xs, w1, w3, w2)


def _shared_kernel(x_ref, w1_ref, w3_ref, w2_ref, o_ref):
    x = x_ref[...]
    h1 = jnp.dot(x, w1_ref[...], preferred_element_type=F32)
    h3 = jnp.dot(x, w3_ref[...], preferred_element_type=F32)
    a = (h1 * jax.nn.sigmoid(h1) * h3).astype(BF16)
    o_ref[...] = jnp.dot(a, w2_ref[...], preferred_element_type=F32)


def shared_expert(hx, w1, w3, w2, n_tok):
    tm = 512
    return pl.pallas_call(
        _shared_kernel,
        grid=(n_tok // tm,),
        in_specs=[pl.BlockSpec((tm, D), lambda i: (i, 0)),
                  pl.BlockSpec((D, EXPERT_DIM), lambda i: (0, 0)),
                  pl.BlockSpec((D, EXPERT_DIM), lambda i: (0, 0)),
                  pl.BlockSpec((EXPERT_DIM, D), lambda i: (0, 0))],
        out_specs=pl.BlockSpec((tm, D), lambda i: (i, 0)),
        out_shape=jax.ShapeDtypeStruct((n_tok, D), F32),
        compiler_params=_cp(("arbitrary",)),
        name="shared_expert",
    )(hx, w1, w3, w2)


COMBINE_TM = 128


def _combine_kernel(slot_ref, w_ref, sh_ref, x_ref, mod_ref, y_hbm, *rest, final):
    fg_ref = rest[0] if final else None
    o_ref, buf, sem = rest[-3:]
    tm = COMBINE_TM

    def issue(tt, carry):
        t8 = pl.multiple_of(tt * ROW_UNROLL, ROW_UNROLL)
        for j in range(ROW_UNROLL):
            for k in range(TOP_K):
                src = y_hbm.at[pl.ds(slot_ref[k * tm + j + t8], 1), :]
                pltpu.make_async_copy(src, buf.at[pl.ds(k * tm + j + t8, 1), :], sem).start(priority=k % 2)
        return carry

    lax.fori_loop(0, tm // ROW_UNROLL, issue, 0)
    pltpu.make_async_copy(y_hbm.at[pl.ds(0, TOP_K * tm), :], buf, sem).wait()

    w = w_ref[...]
    acc_lo = acc_hi = None
    for k in range(TOP_K):
        lo, hi = _unpack_bf16_pairs(buf[k * tm:(k + 1) * tm, :])
        wk = w[:, k:k + 1]
        acc_lo = wk * lo if acc_lo is None else acc_lo + wk * lo
        acc_hi = wk * hi if acc_hi is None else acc_hi + wk * hi
    gate = mod_ref[0][5:6]
    o_lo = x_ref[:, :PK_W] + gate[:, :PK_W] * (sh_ref[:, :PK_W] + acc_lo)
    o_hi = x_ref[:, PK_W:] + gate[:, PK_W:] * (sh_ref[:, PK_W:] + acc_hi)
    if fg_ref is not None:
        ssq = jnp.sum(o_lo * o_lo, axis=-1, keepdims=True) + jnp.sum(o_hi * o_hi, axis=-1, keepdims=True)
        inv = lax.rsqrt(ssq / D + EPS)
        o_lo = o_lo * inv * fg_ref[:, :PK_W]
        o_hi = o_hi * inv * fg_ref[:, PK_W:]
    o_ref[:, :PK_W] = o_lo
    o_ref[:, PK_W:] = o_hi


def moe_combine(slots, wts_tok, shared, x, mod, y, n_tok, final_gain=None):
    tm = COMBINE_TM
    final = final_gain is not None
    in_specs = [pl.BlockSpec((TOP_K * tm,), lambda i: (i,), memory_space=pltpu.SMEM),
                pl.BlockSpec((tm, TOP_K), lambda i: (i, 0)),
                pl.BlockSpec((tm, D), lambda i: (i, 0)),
                pl.BlockSpec((tm, D), lambda i: (i, 0)),
                pl.BlockSpec((1, 6, D), lambda i: (_mod_row(i * tm), 0, 0)),
                pl.BlockSpec(memory_space=pl.ANY)]
    args = [slots, wts_tok, shared, x, mod, y]
    if final:
        in_specs.append(pl.BlockSpec((1, D), lambda i: (0, 0)))
        args.append(final_gain.reshape(1, D))
    return pl.pallas_call(
        functools.partial(_combine_kernel, final=final),
        grid=(n_tok // tm,),
        in_specs=in_specs,
        out_specs=pl.BlockSpec((tm, D), lambda i: (i, 0)),
        out_shape=jax.ShapeDtypeStruct((n_tok, D), F32),
        scratch_shapes=[pltpu.VMEM((TOP_K * tm, PK_W), U32), pltpu.SemaphoreType.DMA(())],
        compiler_params=_cp(("arbitrary",)),
        name="moe_combine",
    )(*args)


def _lookup(table, idx):
    e = jnp.arange(table.shape[0], dtype=I32).reshape((-1,) + (1,) * idx.ndim)
    return jnp.sum(jnp.where(idx[None] == e, table.reshape(e.shape), 0), axis=0)


def _tile_flat(slots, tm):
    k, t = slots.shape
    return slots.reshape(k, t // tm, tm).transpose(1, 0, 2).reshape(-1)


def moe_layer(x, mod, norm_gain, router_w, router_b, exp_w1, exp_w3, exp_w2, sw1, sw3, sw2, layer, n_tok,
              final_gain=None):
    bm = EXP_BM
    n_blocks = -(-n_tok * TOP_K // bm) + N_EXPERTS
    hx, hx_packed = norm_mod(x, norm_gain, mod, 3, n_tok, pack=True)
    eidx, wts, pos, counts = moe_router(hx, router_w, router_b, n_tok)
    shared = shared_expert(hx, sw1.astype(BF16), sw3.astype(BF16), sw2.astype(BF16), n_tok)
    cnt = counts.astype(I32)
    padded = (cnt + bm - 1) // bm * bm
    pad_end = jnp.cumsum(padded)
    pad_start = pad_end - padded
    slots = _lookup(pad_start, eidx) + pos
    blk_row = jnp.arange(n_blocks, dtype=I32) * bm
    block_e = jnp.minimum(jnp.sum((pad_end[:, None] <= blk_row[None, :]).astype(I32), axis=0), N_EXPERTS - 1)
    valid = jnp.clip(_lookup(cnt, block_e) - (blk_row - _lookup(pad_start, block_e)), 0, bm).astype(I32)
    n_used = (pad_end[-1:] // bm).astype(I32)
    prev_e = jnp.concatenate([jnp.full((1,), -1, I32), block_e[:-1]])
    first = ((blk_row < pad_end[-1]) & (block_e != prev_e)).astype(I32)
    stage_slot = ((jnp.cumsum(first) - 1) % 2).astype(I32)
    eids = jnp.arange(N_EXPERTS, dtype=I32)
    later = jnp.where((eids[None, :] > eids[:, None]) & (padded[None, :] > 0), eids[None, :], N_EXPERTS)
    next_nonempty = jnp.min(later, axis=1)
    next_nonempty = jnp.where(next_nonempty == N_EXPERTS, -1, next_nonempty)
    next_e = _lookup(next_nonempty, block_e).astype(I32)
    xs = moe_dispatch(_tile_flat(slots, DISPATCH_TM), hx_packed, n_tok, n_blocks * bm)
    y = moe_experts(block_e, valid, n_used, first, next_e, stage_slot, xs, exp_w1, exp_w3, exp_w2, layer, n_blocks)
    return moe_combine(_tile_flat(slots, COMBINE_TM), wts.T, shared, x, mod, y, n_tok, final_gain)


def _rope_tables():
    t = jnp.arange(S, dtype=I32)
    row = (t // GRID_W).astype(F32)
    col = (t % GRID_W).astype(F32)
    n_freq = HD // 4
    inv_freq = ROPE_THETA ** (-jnp.arange(n_freq, dtype=F32) / n_freq)
    ang = jnp.concatenate([row[:, None] * inv_freq, col[:, None] * inv_freq], axis=-1)
    cosf = jnp.repeat(jnp.cos(ang), 2, axis=-1)
    sinf = jnp.stack([-jnp.sin(ang), jnp.sin(ang)], axis=-1).reshape(S, HD)
    return cosf, sinf


def kernel(x, c, ctx, c_ctx, ada_w, ada_b, norm_mix, norm_ffn, attn_w_in, attn_w_out, attn_rpb, attn_q_gain,
           attn_k_gain, ml_w_in, ml_w_out, ml_gate_b, ml_head_gain, router_w, router_b, exp_w1, exp_w3, exp_w2,
           sh_w1, sh_w3, sh_w2, final_norm_gain):
    depth = ada_w.shape[0]
    xa = jnp.concatenate([x.reshape(T_LAT, D), ctx.reshape(T_CTX, D)], axis=0)
    cvec = jnp.concatenate([c, c_ctx[None], jnp.zeros((8 - B - 1, D), F32)], axis=0)
    mod_all = ada_ln(cvec, ada_w, ada_b).reshape(depth, 8, 6, D)
    cosf, sinf = _rope_tables()

    mod = mod_all[0]
    p = norm_matmul(xa, norm_mix[0], mod, attn_w_in[0].astype(BF16), emit_h=False)
    o_na = neighborhood_attention(p, na_bias_table(attn_rpb[0]))
    o_gqa = gqa_attention(p, cosf, sinf, attn_q_gain[0], attn_k_gain[0])
    o_ctx = ctx_attention(p, attn_q_gain[0], attn_k_gain[0])
    o_all = jnp.concatenate([jnp.concatenate([o_na, o_gqa], axis=1), o_ctx], axis=0)
    xa = matmul_gated_residual(o_all, attn_w_out[0].astype(BF16), xa, mod, 2)
    xa = moe_layer(xa, mod, norm_ffn[0], router_w[0], router_b[0], exp_w1, exp_w3, exp_w2,
                   sh_w1[0], sh_w3[0], sh_w2[0], 0, T_ALL)

    mod = mod_all[1]
    w_in = ml_w_in[0]
    p, hx = norm_matmul(xa, norm_mix[1], mod, w_in[:, :ML_MAIN].astype(BF16), emit_h=True)
    col3, row4 = mlstm_gates(hx, w_in[:, ML_MAIN:], ml_gate_b[0])
    hdir = mlstm_scan(p, col3, row4)
    a = mlstm_readout(hdir, p, ml_head_gain[0])
    xl = matmul_gated_residual(a, ml_w_out[0].astype(BF16), xa, mod, 2)
    xl = moe_layer(xl, mod, norm_ffn[1], router_w[1], router_b[1], exp_w1, exp_w3, exp_w2,
                   sh_w1[1], sh_w3[1], sh_w2[1], 1, T_LAT, final_gain=final_norm_gain)
    return xl.reshape(B, S, D)
```

```python
import functools

import jax
import jax.numpy as jnp
from jax import lax
from jax.experimental import pallas as pl
from jax.experimental.pallas import tpu as pltpu

F32 = jnp.float32
BF16 = jnp.bfloat16
I32 = jnp.int32
U32 = jnp.uint32

D = 2048
B = 4
S = 4096
L = 256
T_LAT = B * S
T_CTX = B * L
T_ALL = T_LAT + T_CTX
GRID_W = 64
ROWS = S // GRID_W
HD = 128
NA_HEADS = 8
NA_WIN_ROWS = 8
NA_WIN_COLS = 16
GQA_Q_HEADS = 8
GQA_KV_HEADS = 2
GQA_GROUP = 4
ROPE_THETA = 10000.0
ATTN_IN = 4608
ML_HEADS = 8
ML_V = 256
ML_QK = 128
ML_MAIN = 6144
N_EXPERTS = 64
TOP_K = 8
N_GROUPS = 8
TOPK_GROUPS = 4
EXPERT_DIM = 512
ROUTED_SCALE = 2.5
EPS = 1e-6
NEG_INF = -1e30
ATT_SCALE = HD ** -0.5
LOG2E = 1.4426950408889634
ML_KSCALE = ML_QK ** -0.5

LANE = 128
NA_QROWS = 4
NA_SLAB = NA_QROWS + NA_WIN_ROWS - 1
NA_QB = NA_QROWS * GRID_W
NA_KB = NA_SLAB * GRID_W
ML_CH = 256
ML_HPS = 2
EXP_BM = 512
PK_W = D // 2
PK_S = PK_W // LANE
VMEM_LIMIT = 48 * 1024 * 1024


def _cp(sem, vmem=VMEM_LIMIT):
    return pltpu.CompilerParams(dimension_semantics=sem, vmem_limit_bytes=vmem)


def _pack_bf16_pairs(xb):
    u = pltpu.bitcast(xb.astype(F32), U32)
    return (u[:, PK_W:] & jnp.uint32(0xFFFF0000)) | (u[:, :PK_W] >> 16)


def _unpack_bf16_pairs(u):
    return pltpu.bitcast(u << 16, F32), pltpu.bitcast(u & jnp.uint32(0xFFFF0000), F32)


def _store_row_tiles(ref, words):
    rows = words.shape[0]
    for s in range(PK_S):
        ref[pl.ds(s, rows, stride=PK_S), :] = words[:, s * LANE:(s + 1) * LANE]


def _load_row_tiles(ref, start, rows):
    return [ref[pl.ds(start + s, rows, stride=PK_S), :] for s in range(PK_S)]


def _mod_row(start_row):
    return jnp.where(start_row < T_LAT, start_row // S, B)


def _ada_kernel(c_ref, w_ref, b_ref, o_ref):
    c = c_ref[...]
    a = (c * jax.nn.sigmoid(c)).astype(BF16)
    w = w_ref[0].astype(BF16)
    o_ref[0] = jnp.dot(a, w, preferred_element_type=F32) + b_ref[0]


def ada_ln(cvec, ada_w, ada_b):
    depth = ada_w.shape[0]
    n = ada_w.shape[2]
    tn = 1024
    return pl.pallas_call(
        _ada_kernel,
        grid=(depth, n // tn),
        in_specs=[pl.BlockSpec((8, D), lambda l, j: (0, 0)),
                  pl.BlockSpec((1, D, tn), lambda l, j: (l, 0, j)),
                  pl.BlockSpec((1, 1, tn), lambda l, j: (l, 0, j))],
        out_specs=pl.BlockSpec((1, 8, tn), lambda l, j: (l, 0, j)),
        out_shape=jax.ShapeDtypeStruct((depth, 8, n), F32),
        compiler_params=_cp(("arbitrary", "arbitrary")),
        name="ada_ln",
    )(cvec, ada_w, ada_b.reshape(depth, 1, n))


def _norm_mod_kernel(x_ref, g_ref, mod_ref, *out_refs, base, pack):
    x = x_ref[...]
    y = x * lax.rsqrt(jnp.mean(x * x, axis=-1, keepdims=True) + EPS) * g_ref[...]
    m = mod_ref[0]
    h = y * (1.0 + m[base + 1:base + 2]) + m[base:base + 1]
    hb = h.astype(BF16)
    out_refs[0][...] = hb
    if pack:
        _store_row_tiles(out_refs[1], _pack_bf16_pairs(hb))


def norm_mod(x, gain, mod, base, n_rows, pack):
    tm = 256
    out_shape = [jax.ShapeDtypeStruct((n_rows, D), BF16)]
    out_specs = [pl.BlockSpec((tm, D), lambda i: (i, 0))]
    if pack:
        out_shape.append(jax.ShapeDtypeStruct((n_rows * PK_S, LANE), U32))
        out_specs.append(pl.BlockSpec((tm * PK_S, LANE), lambda i: (i, 0)))
    res = pl.pallas_call(
        functools.partial(_norm_mod_kernel, base=base, pack=pack),
        grid=(n_rows // tm,),
        in_specs=[pl.BlockSpec((tm, D), lambda i: (i, 0)),
                  pl.BlockSpec((1, D), lambda i: (0, 0)),
                  pl.BlockSpec((1, 6, D), lambda i: (_mod_row(i * tm), 0, 0))],
        out_specs=out_specs,
        out_shape=out_shape,
        compiler_params=_cp(("arbitrary",)),
        name="norm_mod",
    )(x, gain.reshape(1, D), mod)
    return res if pack else res[0]


def _norm_mm_kernel(x_ref, g_ref, mod_ref, w_ref, o_ref, *rest, emit_h):
    hb_ref = rest[-1]

    @pl.when(pl.program_id(1) == 0)
    def _():
        x = x_ref[...]
        y = x * lax.rsqrt(jnp.mean(x * x, axis=-1, keepdims=True) + EPS) * g_ref[...]
        m = mod_ref[0]
        hb_ref[...] = (y * (1.0 + m[1:2]) + m[0:1]).astype(BF16)
        if emit_h:
            rest[0][...] = hb_ref[...]

    o_ref[...] = jnp.dot(hb_ref[...], w_ref[...], preferred_element_type=F32).astype(o_ref.dtype)


def norm_matmul(x, gain, mod, w, emit_h, tm=1024, tn=512):
    m = x.shape[0]
    n = w.shape[1]
    out_shape = [jax.ShapeDtypeStruct((m, n), BF16)]
    out_specs = [pl.BlockSpec((tm, tn), lambda i, j: (i, j))]
    if emit_h:
        out_shape.append(jax.ShapeDtypeStruct((m, D), BF16))
        out_specs.append(pl.BlockSpec((tm, D), lambda i, j: (i, 0)))
    res = pl.pallas_call(
        functools.partial(_norm_mm_kernel, emit_h=emit_h),
        grid=(m // tm, n // tn),
        in_specs=[pl.BlockSpec((tm, D), lambda i, j: (i, 0)),
                  pl.BlockSpec((1, D), lambda i, j: (0, 0)),
                  pl.BlockSpec((1, 6, D), lambda i, j: (_mod_row(i * tm), 0, 0)),
                  pl.BlockSpec((D, tn), lambda i, j: (0, j))],
        out_specs=out_specs,
        out_shape=out_shape,
        scratch_shapes=[pltpu.VMEM((tm, D), BF16)],
        compiler_params=_cp(("arbitrary", "arbitrary")),
        name="norm_matmul",
    )(x, gain.reshape(1, D), mod, w)
    return res if emit_h else res[0]


def _mm_res_kernel(a_ref, w_ref, x_ref, mod_ref, o_ref, *, slot):
    acc = jnp.dot(a_ref[...], w_ref[...], preferred_element_type=F32)
    o_ref[...] = x_ref[...] + mod_ref[0][slot:slot + 1] * acc


def matmul_gated_residual(a, w, x, mod, slot, tm=1024, tn=512):
    m, k = a.shape
    n = w.shape[1]
    return pl.pallas_call(
        functools.partial(_mm_res_kernel, slot=slot),
        grid=(m // tm, n // tn),
        in_specs=[pl.BlockSpec((tm, k), lambda i, j: (i, 0)),
                  pl.BlockSpec((k, tn), lambda i, j: (0, j)),
                  pl.BlockSpec((tm, tn), lambda i, j: (i, j)),
                  pl.BlockSpec((1, 6, tn), lambda i, j: (_mod_row(i * tm), 0, j))],
        out_specs=pl.BlockSpec((tm, tn), lambda i, j: (i, j)),
        out_shape=jax.ShapeDtypeStruct((m, n), F32),
        compiler_params=_cp(("arbitrary", "arbitrary")),
        name="matmul_gated_residual",
    )(a, w, x, mod)


def _dot_nt(a, b):
    return lax.dot_general(a, b, (((1,), (1,)), ((), ())), preferred_element_type=F32)


def _rms_head(x, gain):
    return x * lax.rsqrt(jnp.mean(x * x, axis=-1, keepdims=True) + EPS) * gain


def _rope(x, cosf, sinf):
    lane = lax.broadcasted_iota(I32, x.shape, 1)
    nxt = pltpu.roll(x, LANE - 1, 1)
    prv = pltpu.roll(x, 1, 1)
    return x * cosf + jnp.where((lane & 1) == 0, nxt, prv) * sinf


def _softmax_av(parts):
    m = functools.reduce(jnp.maximum, [jnp.max(s, axis=-1, keepdims=True) for s, _ in parts])
    l = None
    o = None
    for s, v in parts:
        p = jnp.exp(s - m)
        li = jnp.sum(p, axis=-1, keepdims=True)
        oi = jnp.dot(p.astype(BF16), v, preferred_element_type=F32)
        l = li if l is None else l + li
        o = oi if o is None else o + oi
    return o / l


def _na_kernel(q_ref, k_ref, v_ref, kc_ref, vc_ref, tab_ref, o_ref):
    kc = kc_ref[...]
    vc = vc_ref[...]
    n_blocks = ROWS // NA_QROWS

    def body(j, carry):
        ks = jnp.clip(j * NA_QROWS - NA_WIN_ROWS // 2, 0, ROWS - NA_SLAB)
        typ = jnp.where(j == 0, 0, jnp.where(j == n_blocks - 1, 2, 1))
        qs = pl.multiple_of(j * NA_QB, NA_QB)
        kst = pl.multiple_of(ks * GRID_W, GRID_W)
        q = q_ref[pl.ds(qs, NA_QB), :]
        k = k_ref[pl.ds(kst, NA_KB), :]
        v = v_ref[pl.ds(kst, NA_KB), :]
        s_win = _dot_nt(q, k) * ATT_SCALE + tab_ref[typ, 0]
        s_ctx = _dot_nt(q, kc) * ATT_SCALE
        o_ref[pl.ds(qs, NA_QB), :] = _softmax_av([(s_win, v), (s_ctx, vc)]).astype(BF16)
        return carry

    lax.fori_loop(0, n_blocks, body, 0)


def na_bias_table(rpb):
    def one(r0, ks):
        r = r0 + jnp.arange(NA_QROWS)
        kr = ks + jnp.arange(NA_SLAB)
        start = jnp.clip(r - NA_WIN_ROWS // 2, 0, ROWS - NA_WIN_ROWS)
        row_ok = (kr[None, :] >= start[:, None]) & (kr[None, :] < start[:, None] + NA_WIN_ROWS)
        row_idx = jnp.clip(kr[None, :] - r[:, None] + NA_WIN_ROWS - 1, 0, 2 * NA_WIN_ROWS - 2)
        cq = jnp.arange(GRID_W)
        col_start = jnp.clip(cq - NA_WIN_COLS // 2, 0, GRID_W - NA_WIN_COLS)
        col_ok = (cq[None, :] >= col_start[:, None]) & (cq[None, :] < col_start[:, None] + NA_WIN_COLS)
        col_idx = jnp.clip(cq[None, :] - cq[:, None] + NA_WIN_COLS - 1, 0, 2 * NA_WIN_COLS - 2)
        r_hot = jax.nn.one_hot(row_idx, 2 * NA_WIN_ROWS - 1, dtype=F32)
        c_hot = jax.nn.one_hot(col_idx, 2 * NA_WIN_COLS - 1, dtype=F32)
        bias = jnp.einsum('qka,hab,xyb->hqxky', r_hot, rpb.astype(F32), c_hot, precision=lax.Precision.HIGHEST)
        ok = row_ok[:, None, :, None] & col_ok[None, :, None, :]
        return jnp.where(ok[None], bias, NEG_INF).reshape(NA_HEADS, NA_QB, NA_KB)

    mid = 2 * NA_QROWS
    last = ROWS - NA_QROWS
    return jnp.stack([one(0, 0), one(mid, mid - NA_WIN_ROWS // 2), one(last, ROWS - NA_SLAB)])


def neighborhood_attention(p, table):
    cb = S // L
    return pl.pallas_call(
        _na_kernel,
        grid=(NA_HEADS, B),
        in_specs=[pl.BlockSpec((S, HD), lambda h, b: (b, h)),
                  pl.BlockSpec((S, HD), lambda h, b: (b, NA_HEADS + h)),
                  pl.BlockSpec((S, HD), lambda h, b: (b, 2 * NA_HEADS + h)),
                  pl.BlockSpec((L, HD), lambda h, b: (B * cb + b, NA_HEADS + h)),
                  pl.BlockSpec((L, HD), lambda h, b: (B * cb + b, 2 * NA_HEADS + h)),
                  pl.BlockSpec((3, 1, NA_QB, NA_KB), lambda h, b: (0, h, 0, 0))],
        out_specs=pl.BlockSpec((S, HD), lambda h, b: (b, h)),
        out_shape=jax.ShapeDtypeStruct((T_LAT, NA_HEADS * HD), BF16),
        compiler_params=_cp(("arbitrary", "arbitrary")),
        name="neighborhood_attention",
    )(p, p, p, p, p, table)


GQA_TQ = 256
GQA_CK = 512
GQA_QCOL = 3 * NA_HEADS
GQA_KCOL = GQA_QCOL + GQA_Q_HEADS
GQA_VCOL = GQA_KCOL + GQA_KV_HEADS


def _gqa_kernel(q_ref, k_ref, v_ref, kc_ref, vc_ref, cq_ref, sq_ref, ck_ref, sk_ref, qg_ref, kg_ref,
                o_ref, kn_ref, kcn_ref):
    @pl.when(pl.program_id(2) == 0)
    def _():
        kn = _rope(_rms_head(k_ref[...].astype(F32), kg_ref[...]), ck_ref[...], sk_ref[...])
        kn_ref[...] = kn.astype(BF16)
        kcn_ref[...] = _rms_head(kc_ref[...].astype(F32), kg_ref[...]).astype(BF16)

    cos = cq_ref[...]
    sin = sq_ref[...]
    heads = []
    for g in range(GQA_GROUP):
        qh = _rope(_rms_head(q_ref[:, g * HD:(g + 1) * HD].astype(F32), qg_ref[...]), cos, sin)
        heads.append((qh * (ATT_SCALE * LOG2E)).astype(BF16))
    q = jnp.concatenate(heads, axis=0)
    chunks = [(kn_ref[c * GQA_CK:(c + 1) * GQA_CK, :], v_ref[c * GQA_CK:(c + 1) * GQA_CK, :])
              for c in range(S // GQA_CK)]
    chunks.append((kcn_ref[...], vc_ref[...]))
    m = l = acc = None
    for kk, vv in chunks:
        s = _dot_nt(q, kk)
        mc = jnp.max(s, axis=-1, keepdims=True)
        if m is None:
            m_new = mc
            p = jnp.exp2(s - m_new)
            l = jnp.sum(p, axis=-1, keepdims=True)
            acc = jnp.dot(p.astype(BF16), vv, preferred_element_type=F32)
        else:
            m_new = jnp.maximum(m, mc)
            alpha = jnp.exp2(m - m_new)
            p = jnp.exp2(s - m_new)
            l = alpha * l + jnp.sum(p, axis=-1, keepdims=True)
            acc = alpha * acc + jnp.dot(p.astype(BF16), vv, preferred_element_type=F32)
        m = m_new
    o = acc / l
    for g in range(GQA_GROUP):
        o_ref[:, g * HD:(g + 1) * HD] = o[g * GQA_TQ:(g + 1) * GQA_TQ].astype(BF16)


def gqa_attention(p, cosf, sinf, q_gain, k_gain):
    nq = S // GQA_TQ
    cb = S // L
    gw = GQA_GROUP * HD
    return pl.pallas_call(
        _gqa_kernel,
        grid=(B, GQA_KV_HEADS, nq),
        in_specs=[pl.BlockSpec((GQA_TQ, gw), lambda b, n, i: (b * nq + i, GQA_QCOL // GQA_GROUP + n)),
                  pl.BlockSpec((S, HD), lambda b, n, i: (b, GQA_KCOL + n)),
                  pl.BlockSpec((S, HD), lambda b, n, i: (b, GQA_VCOL + n)),
                  pl.BlockSpec((L, HD), lambda b, n, i: (B * cb + b, GQA_KCOL + n)),
                  pl.BlockSpec((L, HD), lambda b, n, i: (B * cb + b, GQA_VCOL + n)),
                  pl.BlockSpec((GQA_TQ, HD), lambda b, n, i: (i, 0)),
                  pl.BlockSpec((GQA_TQ, HD), lambda b, n, i: (i, 0)),
                  pl.BlockSpec((S, HD), lambda b, n, i: (0, 0)),
                  pl.BlockSpec((S, HD), lambda b, n, i: (0, 0)),
                  pl.BlockSpec((1, HD), lambda b, n, i: (0, 0)),
                  pl.BlockSpec((1, HD), lambda b, n, i: (0, 0))],
        out_specs=pl.BlockSpec((GQA_TQ, gw), lambda b, n, i: (b * nq + i, n)),
        out_shape=jax.ShapeDtypeStruct((T_LAT, GQA_Q_HEADS * HD), BF16),
        scratch_shapes=[pltpu.VMEM((S, HD), BF16), pltpu.VMEM((L, HD), BF16)],
        compiler_params=_cp(("arbitrary",) * 3),
        name="gqa_attention",
    )(p, p, p, p, p, cosf, sinf, cosf, sinf, q_gain.reshape(1, HD), k_gain.reshape(1, HD))


def _ctx_attn_kernel(p_ref, qg_ref, kg_ref, o_ref):
    def col(c):
        return p_ref[:, c * HD:(c + 1) * HD]

    for h in range(NA_HEADS):
        s = _dot_nt(col(h), col(NA_HEADS + h)) * ATT_SCALE
        o_ref[:, h * HD:(h + 1) * HD] = _softmax_av([(s, col(2 * NA_HEADS + h))]).astype(BF16)
    for n in range(GQA_KV_HEADS):
        kn = _rms_head(col(GQA_KCOL + n).astype(F32), kg_ref[...]).astype(BF16)
        v = col(GQA_VCOL + n)
        for g in range(GQA_GROUP):
            h = n * GQA_GROUP + g
            qn = _rms_head(col(GQA_QCOL + h).astype(F32), qg_ref[...]).astype(BF16)
            s = _dot_nt(qn, kn) * ATT_SCALE
            o_ref[:, (NA_HEADS + h) * HD:(NA_HEADS + h + 1) * HD] = _softmax_av([(s, v)]).astype(BF16)


def ctx_attention(p, q_gain, k_gain):
    cb = S // L
    return pl.pallas_call(
        _ctx_attn_kernel,
        grid=(B,),
        in_specs=[pl.BlockSpec((L, ATTN_IN), lambda b: (B * cb + b, 0)),
                  pl.BlockSpec((1, HD), lambda b: (0, 0)),
                  pl.BlockSpec((1, HD), lambda b: (0, 0))],
        out_specs=pl.BlockSpec((L, D), lambda b: (b, 0)),
        out_shape=jax.ShapeDtypeStruct((T_CTX, D), BF16),
        compiler_params=_cp(("arbitrary",)),
        name="ctx_attention",
    )(p, q_gain.reshape(1, HD), k_gain.reshape(1, HD))


def _log_sigmoid(x):
    return -(jnp.maximum(-x, 0.0) + jnp.log1p(jnp.exp(-jnp.abs(x))))


def _dot_hi(a, b):
    return jnp.dot(a, b, precision=lax.Precision.HIGHEST, preferred_element_type=F32)


def _gate_kernel(h_ref, wg_ref, wgt_ref, b_ref, bt_ref, lt_ref, ut_ref, col_ref, row_ref):
    nh = ML_HEADS
    hx = h_ref[...]
    g = jnp.dot(hx, wg_ref[...], preferred_element_type=F32) + b_ref[...]
    gt = _dot_nt(wgt_ref[...], hx) + bt_ref[...]
    li = g[:, 0:2 * nh]
    lf = _log_sigmoid(g[:, 2 * nh:4 * nh])
    lit = gt[0:2 * nh]
    lft = _log_sigmoid(gt[2 * nh:4 * nh])
    lt = lt_ref[...]
    ut = ut_ref[...]
    lane = lax.broadcasted_iota(I32, lf.shape, 1)
    bc = jnp.where(lane < nh, _dot_hi(lt, lf), _dot_hi(ut, lf))
    tot = jnp.sum(lf, axis=0, keepdims=True)
    aend = tot - bc + li
    col_ref[...] = jnp.concatenate([bc, aend, jnp.zeros((ML_CH, LANE - 4 * nh), F32)], axis=1)
    sub = lax.broadcasted_iota(I32, lft.shape, 0)
    bct = jnp.where(sub < nh, _dot_hi(lft, ut), _dot_hi(lft, lt))
    tott = jnp.sum(lft, axis=1, keepdims=True)
    gtr = lit - bct
    row_ref[0] = jnp.concatenate([bct, gtr, tott + gtr, jnp.broadcast_to(tott, bct.shape)], axis=0)


def mlstm_gates(hx, wg, gate_b):
    nh = ML_HEADS
    n_ch = T_ALL // ML_CH
    wg_pad = jnp.zeros((D, LANE), BF16).at[:, :4 * nh].set(wg.astype(BF16))
    b_pad = jnp.zeros((1, LANE), F32).at[0, :4 * nh].set(gate_b.reshape(-1))
    wgt = wg.astype(BF16).T
    bt = gate_b.reshape(4 * nh, 1).astype(F32)
    lt = jnp.tril(jnp.ones((ML_CH, ML_CH), F32))
    ut = jnp.triu(jnp.ones((ML_CH, ML_CH), F32))
    col, row = pl.pallas_call(
        _gate_kernel,
        grid=(n_ch,),
        in_specs=[pl.BlockSpec((ML_CH, D), lambda i: (i, 0)),
                  pl.BlockSpec((D, LANE), lambda i: (0, 0)),
                  pl.BlockSpec((4 * nh, D), lambda i: (0, 0)),
                  pl.BlockSpec((1, LANE), lambda i: (0, 0)),
                  pl.BlockSpec((4 * nh, 1), lambda i: (0, 0)),
                  pl.BlockSpec((ML_CH, ML_CH), lambda i: (0, 0)),
                  pl.BlockSpec((ML_CH, ML_CH), lambda i: (0, 0))],
        out_specs=[pl.BlockSpec((ML_CH, LANE), lambda i: (i, 0)),
                   pl.BlockSpec((1, 8 * nh, ML_CH), lambda i: (i, 0, 0))],
        out_shape=[jax.ShapeDtypeStruct((T_ALL, LANE), F32),
                   jax.ShapeDtypeStruct((n_ch, 8 * nh, ML_CH), F32)],
        compiler_params=_cp(("arbitrary",)),
        name="mlstm_gates",
    )(hx, wg_pad, wgt, b_pad, bt, lt, ut)
    col3 = col[:, :4 * nh].reshape(T_ALL, 2, 2 * nh).transpose(2, 0, 1)
    col3 = jnp.pad(col3, ((0, 0), (0, 0), (0, 6)))
    row4 = row.reshape(n_ch, 4, 2 * nh, ML_CH).transpose(2, 0, 1, 3)
    row4 = jnp.pad(row4, ((0, 0), (0, 0), (0, 4), (0, 0)))
    return col3, row4


def _mlstm_step(d, hh, q_ref, k_ref, v_ref, col_ref, row_ref, o_ref, c_ref, n_ref, m_ref):
    sl = d * ML_HPS + hh
    q = q_ref[:, hh * ML_QK:(hh + 1) * ML_QK]
    kf = k_ref[:, hh * ML_QK:(hh + 1) * ML_QK].astype(F32) * ML_KSCALE
    kb = kf.astype(BF16)
    v = v_ref[:, hh * ML_V:(hh + 1) * ML_V]
    col = col_ref[hh]
    row = row_ref[hh, 0]
    bc_col = col[:, 0:1]
    aend_col = col[:, 1:2]
    g_row = row[1:2]
    aend_row = row[2:3]
    btot = row[3:4, 0:1]
    m_st = m_ref[sl]
    c_st = c_ref[sl]
    n_st = n_ref[sl]
    m_new = jnp.maximum(btot + m_st, jnp.max(aend_row, axis=1, keepdims=True))

    r = lax.broadcasted_iota(I32, (ML_CH, ML_CH), 0)
    c = lax.broadcasted_iota(I32, (ML_CH, ML_CH), 1)
    causal = (r >= c) if d == 0 else (r <= c)
    d_mat = jnp.where(causal, bc_col + g_row, -jnp.inf)
    m_row = jnp.maximum(bc_col + m_st, jnp.max(d_mat, axis=1, keepdims=True))
    w_inter = jnp.exp(bc_col + m_st - m_row)
    s_mat = _dot_nt(q, kb) * jnp.exp(d_mat - m_row)
    num = (w_inter * jnp.dot(q, c_st.astype(BF16), preferred_element_type=F32)
           + jnp.dot(s_mat.astype(BF16), v, preferred_element_type=F32))
    den = (w_inter * jnp.sum(q.astype(F32) * n_st, axis=1, keepdims=True)
           + jnp.sum(s_mat, axis=1, keepdims=True))
    o_ref[:, hh * ML_V:(hh + 1) * ML_V] = num / jnp.maximum(jnp.abs(den), jnp.exp(-m_row))

    w_end_col = jnp.exp(aend_col - m_new)
    w_end_row = jnp.exp(aend_row - m_new)
    decay = jnp.exp(btot + m_st - m_new)
    kw = (kf * w_end_col).astype(BF16)
    c_ref[sl] = decay * c_st + lax.dot_general(kw, v, (((0,), (0,)), ((), ())), preferred_element_type=F32)
    w8 = jnp.broadcast_to(w_end_row, (8, ML_CH)).astype(BF16)
    n_ref[sl] = decay * n_st + jnp.dot(w8, kb, preferred_element_type=F32)[0:1]
    m_ref[sl] = m_new


def _mlstm_kernel(qf, kf, vf, colf, rowf, qb, kb, vb, colb, rowb, of, ob, c_ref, n_ref, m_ref):
    @pl.when(pl.program_id(2) == 0)
    def _():
        c_ref[...] = jnp.zeros_like(c_ref)
        n_ref[...] = jnp.zeros_like(n_ref)
        m_ref[...] = jnp.zeros_like(m_ref)

    for hh in range(ML_HPS):
        _mlstm_step(0, hh, qf, kf, vf, colf, rowf, of, c_ref, n_ref, m_ref)
        _mlstm_step(1, hh, qb, kb, vb, colb, rowb, ob, c_ref, n_ref, m_ref)


def mlstm_scan(p, col3, row4):
    n_lat = S // ML_CH
    steps = n_lat + 1
    lat_blocks = T_LAT // ML_CH
    hps = ML_HPS
    n_groups = ML_HEADS // hps
    kcol = (ML_HEADS * ML_QK) // (hps * ML_QK)
    vcol = (2 * ML_HEADS * ML_QK) // (hps * ML_V)

    def chunk(b, d, st):
        c = (st - 1) if d == 0 else (n_lat - st)
        return jnp.where(st == 0, lat_blocks + b, b * n_lat + c)

    def out_chunk(b, d, st):
        s1 = jnp.maximum(st, 1)
        return b * n_lat + ((s1 - 1) if d == 0 else (n_lat - s1))

    def dir_specs(d):
        return [pl.BlockSpec((ML_CH, hps * ML_QK), lambda b, h, s: (chunk(b, d, s), h)),
                pl.BlockSpec((ML_CH, hps * ML_QK), lambda b, h, s: (chunk(b, d, s), kcol + h)),
                pl.BlockSpec((ML_CH, hps * ML_V), lambda b, h, s: (chunk(b, d, s), vcol + h)),
                pl.BlockSpec((hps, ML_CH, 8), lambda b, h, s: (d * n_groups + h, chunk(b, d, s), 0)),
                pl.BlockSpec((hps, 1, 8, ML_CH), lambda b, h, s: (d * n_groups + h, chunk(b, d, s), 0, 0))]

    return pl.pallas_call(
        _mlstm_kernel,
        grid=(B, n_groups, steps),
        in_specs=dir_specs(0) + dir_specs(1),
        out_specs=[pl.BlockSpec((ML_CH, hps * ML_V), lambda b, h, s: (out_chunk(b, 0, s), h)),
                   pl.BlockSpec((ML_CH, hps * ML_V), lambda b, h, s: (out_chunk(b, 1, s), h))],
        out_shape=[jax.ShapeDtypeStruct((T_LAT, ML_HEADS * ML_V), F32),
                   jax.ShapeDtypeStruct((T_LAT, ML_HEADS * ML_V), F32)],
        scratch_shapes=[pltpu.VMEM((2 * hps, ML_QK, ML_V), F32), pltpu.VMEM((2 * hps, 1, ML_QK), F32),
                        pltpu.VMEM((2 * hps, 1, 1), F32)],
        compiler_params=_cp(("arbitrary",) * 3),
        name="mlstm_scan",
    )(p, p, p, col3, row4, p, p, p, col3, row4)


def _readout_kernel(hf_ref, hb_ref, o_ref, g_ref, a_ref):
    hs = hf_ref[...] + hb_ref[...]
    for h in range(ML_HEADS):
        sl = slice(h * ML_V, (h + 1) * ML_V)
        x = hs[:, sl]
        hn = x * lax.rsqrt(jnp.mean(x * x, axis=-1, keepdims=True) + EPS) * g_ref[:, sl]
        a_ref[:, sl] = (hn * jax.nn.sigmoid(o_ref[:, sl].astype(F32))).astype(BF16)


def mlstm_readout(hdir, p, head_gain):
    tm = 256
    ocol = (2 * ML_HEADS * ML_QK + ML_HEADS * ML_V) // D
    return pl.pallas_call(
        _readout_kernel,
        grid=(T_LAT // tm,),
        in_specs=[pl.BlockSpec((tm, D), lambda i: (i, 0)),
                  pl.BlockSpec((tm, D), lambda i: (i, 0)),
                  pl.BlockSpec((tm, D), lambda i: (i, ocol)),
                  pl.BlockSpec((1, D), lambda i: (0, 0))],
        out_specs=pl.BlockSpec((tm, D), lambda i: (i, 0)),
        out_shape=jax.ShapeDtypeStruct((T_LAT, D), BF16),
        compiler_params=_cp(("arbitrary",)),
        name="mlstm_readout",
    )(hdir[0], hdir[1], p, head_gain.reshape(1, D))


ROUTER_TM = 512


def _router_kernel(h_ref, w_ref, rb_ref, erow_ref, tri_ref, eidx_ref, wts_ref, pos_ref, cnt_ref, carry_ref):
    ng = N_GROUPS
    epg = N_EXPERTS // N_GROUPS
    tm = ROUTER_TM
    ninf = -jnp.inf

    @pl.when(pl.program_id(0) == 0)
    def _():
        carry_ref[...] = jnp.zeros_like(carry_ref)

    s = jax.nn.sigmoid(_dot_nt(w_ref[...], h_ref[...]))
    ssel = s + rb_ref[...]
    sraw = [s[ng * j:ng * (j + 1)] for j in range(epg)]
    slab = [ssel[ng * j:ng * (j + 1)] for j in range(epg)]
    m1 = functools.reduce(jnp.maximum, slab)
    jfirst = functools.reduce(jnp.minimum, [jnp.where(slab[j] == m1, j, epg) for j in range(epg)])
    m2 = functools.reduce(jnp.maximum, [jnp.where(jfirst == j, ninf, slab[j]) for j in range(epg)])
    gs = m1 + m2
    giota = lax.broadcasted_iota(I32, (ng, tm), 0)
    gsel = jnp.zeros((ng, tm), F32)
    for _ in range(TOPK_GROUPS):
        mx = jnp.max(gs, axis=0, keepdims=True)
        gi = jnp.min(jnp.where(gs == mx, giota, ng), axis=0, keepdims=True)
        hit = giota == gi
        gsel = jnp.where(hit, 1.0, gsel)
        gs = jnp.where(hit, ninf, gs)
    msl = [jnp.where(gsel > 0.0, slab[j], ninf) for j in range(epg)]
    eid = [giota * epg + j for j in range(epg)]
    selm = [jnp.zeros((ng, tm), F32) for _ in range(epg)]
    e_list, w_list = [], []
    for _ in range(TOP_K):
        mx = jnp.max(functools.reduce(jnp.maximum, msl), axis=0, keepdims=True)
        cand = functools.reduce(jnp.minimum, [jnp.where(msl[j] == mx, eid[j], N_EXPERTS) for j in range(epg)])
        esel = jnp.min(cand, axis=0, keepdims=True)
        hits = [eid[j] == esel for j in range(epg)]
        wk = functools.reduce(lambda a, b: a + b, [jnp.where(hits[j], sraw[j], 0.0) for j in range(epg)])
        w_list.append(jnp.sum(wk, axis=0, keepdims=True))
        e_list.append(esel)
        msl = [jnp.where(hits[j], ninf, msl[j]) for j in range(epg)]
        selm = [jnp.where(hits[j], 1.0, selm[j]) for j in range(epg)]
    wsum = functools.reduce(lambda a, b: a + b, w_list)
    wts_ref[...] = jnp.concatenate([w / wsum * ROUTED_SCALE for w in w_list], axis=0)
    eidx_ref[...] = jnp.concatenate(e_list, axis=0)
    sel = jnp.concatenate(selm, axis=0)
    carry = carry_ref[...]
    posfull = jnp.dot(sel.astype(BF16), tri_ref[...], preferred_element_type=F32) + carry
    erow = erow_ref[...]
    pos = [jnp.sum(jnp.where(erow == e, posfull, 0.0), axis=0, keepdims=True) for e in e_list]
    pos_ref[...] = jnp.concatenate(pos, axis=0).astype(I32)
    carry = carry + jnp.sum(sel, axis=1, keepdims=True)
    carry_ref[...] = carry
    cnt_ref[...] = carry


def moe_router(hx, router_w, router_b, n_tok):
    tm = ROUTER_TM
    epg = N_EXPERTS // N_GROUPS
    perm = (jnp.arange(N_EXPERTS) % N_GROUPS) * epg + jnp.arange(N_EXPERTS) // N_GROUPS
    w_t = router_w.astype(BF16).T[perm]
    rb = router_b.astype(F32)[perm].reshape(N_EXPERTS, 1)
    erow = perm.astype(I32).reshape(N_EXPERTS, 1)
    tri = jnp.triu(jnp.ones((tm, tm), BF16), 1)
    eidx, wts, pos, counts = pl.pallas_call(
        _router_kernel,
        grid=(n_tok // tm,),
        in_specs=[pl.BlockSpec((tm, D), lambda i: (i, 0)),
                  pl.BlockSpec((N_EXPERTS, D), lambda i: (0, 0)),
                  pl.BlockSpec((N_EXPERTS, 1), lambda i: (0, 0)),
                  pl.BlockSpec((N_EXPERTS, 1), lambda i: (0, 0)),
                  pl.BlockSpec((tm, tm), lambda i: (0, 0))],
        out_specs=[pl.BlockSpec((TOP_K, tm), lambda i: (0, i)),
                   pl.BlockSpec((TOP_K, tm), lambda i: (0, i)),
                   pl.BlockSpec((TOP_K, tm), lambda i: (0, i)),
                   pl.BlockSpec((N_EXPERTS, 1), lambda i: (0, 0))],
        out_shape=[jax.ShapeDtypeStruct((TOP_K, n_tok), I32),
                   jax.ShapeDtypeStruct((TOP_K, n_tok), F32),
                   jax.ShapeDtypeStruct((TOP_K, n_tok), I32),
                   jax.ShapeDtypeStruct((N_EXPERTS, 1), F32)],
        scratch_shapes=[pltpu.VMEM((N_EXPERTS, 1), F32)],
        compiler_params=_cp(("arbitrary",)),
        name="moe_router",
    )(hx, w_t, rb, erow, tri)
    return eidx, wts, pos, counts.reshape(N_EXPERTS)[perm]


DISPATCH_TM = 512


ROW_UNROLL = 8


def _dispatch_kernel(slot_ref, hx_ref, xs_hbm, sem):
    def issue(tt, carry):
        t8 = pl.multiple_of(tt * ROW_UNROLL, ROW_UNROLL)
        for j in range(ROW_UNROLL):
            src = hx_ref.at[pl.ds(pl.multiple_of((t8 + j) * PK_S, PK_S), PK_S), :]
            for k in range(TOP_K):
                row = pl.multiple_of(slot_ref[k * DISPATCH_TM + j + t8] * PK_S, PK_S)
                pltpu.make_async_copy(src, xs_hbm.at[pl.ds(row, PK_S), :], sem).start(priority=k % 2)
        return carry

    lax.fori_loop(0, DISPATCH_TM // ROW_UNROLL, issue, 0)
    for _ in range(TOP_K):
        pltpu.make_async_copy(hx_ref, xs_hbm.at[pl.ds(0, DISPATCH_TM * PK_S), :], sem).wait()


def moe_dispatch(slots, hx_packed, n_tok, n_rows):
    tm = DISPATCH_TM
    return pl.pallas_call(
        _dispatch_kernel,
        grid=(n_tok // tm,),
        in_specs=[pl.BlockSpec((TOP_K * tm,), lambda i: (i,), memory_space=pltpu.SMEM),
                  pl.BlockSpec((tm * PK_S, LANE), lambda i: (i, 0))],
        out_specs=pl.BlockSpec(memory_space=pl.ANY),
        out_shape=jax.ShapeDtypeStruct((n_rows * PK_S, LANE), U32),
        scratch_shapes=[pltpu.SemaphoreType.DMA(())],
        compiler_params=_cp(("arbitrary",)),
        name="moe_dispatch",
    )(slots, hx_packed)


def _expert_kernel(be_ref, valid_ref, nused_ref, first_ref, next_ref, slot_ref,
                   xs_ref, w1_hbm, w3_hbm, w2_hbm, y_ref, w1s, w3s, w2s, w1b, w3b, w2b, xb, sems, *, layer):
    i = pl.program_id(0)
    bm = EXP_BM

    def weight_copies(e, s):
        return [pltpu.make_async_copy(w_hbm.at[layer, e], stage.at[s], sems.at[s, j])
                for j, (w_hbm, stage) in enumerate(((w1_hbm, w1s), (w3_hbm, w3s), (w2_hbm, w2s)))]

    @pl.when(i < nused_ref[0])
    def _():
        @pl.when(first_ref[i] == 1)
        def _():
            s = slot_ref[i]

            @pl.when(i == 0)
            def _():
                for cp in weight_copies(be_ref[0], 0):
                    cp.start()

            for cp in weight_copies(be_ref[i], s):
                cp.wait()

            @pl.when(next_ref[i] >= 0)
            def _():
                for cp in weight_copies(next_ref[i], 1 - s):
                    cp.start()

            w1b[...] = w1s[s].astype(BF16)
            w3b[...] = w3s[s].astype(BF16)
            w2b[...] = w2s[s].astype(BF16)

        live = lax.broadcasted_iota(I32, (bm, LANE), 0) < valid_ref[i]
        for s, piece in enumerate(_load_row_tiles(xs_ref, 0, bm)):
            lo, hi = _unpack_bf16_pairs(jnp.where(live, piece, jnp.uint32(0)))
            xb[:, s * LANE:(s + 1) * LANE] = lo.astype(BF16)
            xb[:, PK_W + s * LANE:PK_W + (s + 1) * LANE] = hi.astype(BF16)
        x = xb[...]
        h1 = jnp.dot(x, w1b[...], preferred_element_type=F32)
        h3 = jnp.dot(x, w3b[...], preferred_element_type=F32)
        a = (h1 * jax.nn.sigmoid(h1) * h3).astype(BF16)
        y = jnp.dot(a, w2b[...], preferred_element_type=F32)
        _store_row_tiles(y_ref, _pack_bf16_pairs(y.astype(BF16)))


def moe_experts(block_e, valid, n_used, first, next_e, slot, xs, w1, w3, w2, layer, n_blocks):
    bm = EXP_BM

    def blk(i, be, va, nu, fi, ne, sl):
        return (jnp.minimum(i, nu[0] - 1), 0)

    grid_spec = pltpu.PrefetchScalarGridSpec(
        num_scalar_prefetch=6,
        grid=(n_blocks,),
        in_specs=[pl.BlockSpec((bm * PK_S, LANE), blk),
                  pl.BlockSpec(memory_space=pl.ANY),
                  pl.BlockSpec(memory_space=pl.ANY),
                  pl.BlockSpec(memory_space=pl.ANY)],
        out_specs=pl.BlockSpec((bm * PK_S, LANE), blk),
        scratch_shapes=[pltpu.VMEM((2, D, EXPERT_DIM), F32), pltpu.VMEM((2, D, EXPERT_DIM), F32),
                        pltpu.VMEM((2, EXPERT_DIM, D), F32),
                        pltpu.VMEM((D, EXPERT_DIM), BF16), pltpu.VMEM((D, EXPERT_DIM), BF16),
                        pltpu.VMEM((EXPERT_DIM, D), BF16), pltpu.VMEM((bm, D), BF16),
                        pltpu.SemaphoreType.DMA((2, 3))],
    )
    return pl.pallas_call(
        functools.partial(_expert_kernel, layer=layer),
        grid_spec=grid_spec,
        out_shape=jax.ShapeDtypeStruct((n_blocks * bm * PK_S, LANE), U32),
        compiler_params=_cp(("arbitrary",), vmem=56 * 1024 * 1024),
        name="moe_experts",
    )(block_e, valid, n_used, first, next_e, slot, xs, w1, w3, w2)


def _shared_kernel(x_ref, w1_ref, w3_ref, w2_ref, o_ref):
    x = x_ref[...]
    h1 = jnp.dot(x, w1_ref[...], preferred_element_type=F32)
    h3 = jnp.dot(x, w3_ref[...], preferred_element_type=F32)
    a = (h1 * jax.nn.sigmoid(h1) * h3).astype(BF16)
    o_ref[...] = jnp.dot(a, w2_ref[...], preferred_element_type=F32)


def shared_expert(hx, w1, w3, w2, n_tok):
    tm = 512
    return pl.pallas_call(
        _shared_kernel,
        grid=(n_tok // tm,),
        in_specs=[pl.BlockSpec((tm, D), lambda i: (i, 0)),
                  pl.BlockSpec((D, EXPERT_DIM), lambda i: (0, 0)),
                  pl.BlockSpec((D, EXPERT_DIM), lambda i: (0, 0)),
                  pl.BlockSpec((EXPERT_DIM, D), lambda i: (0, 0))],
        out_specs=pl.BlockSpec((tm, D), lambda i: (i, 0)),
        out_shape=jax.ShapeDtypeStruct((n_tok, D), F32),
        compiler_params=_cp(("arbitrary",)),
        name="shared_expert",
    )(hx, w1, w3, w2)


COMBINE_TM = 128


def _combine_kernel(slot_ref, w_ref, sh_ref, x_ref, mod_ref, y_hbm, *rest, final):
    fg_ref = rest[0] if final else None
    o_ref, buf, sem = rest[-3:]
    tm = COMBINE_TM

    def issue(tt, carry):
        t8 = pl.multiple_of(tt * ROW_UNROLL, ROW_UNROLL)
        for j in range(ROW_UNROLL):
            for k in range(TOP_K):
                row = pl.multiple_of(slot_ref[k * tm + j + t8] * PK_S, PK_S)
                dst = buf.at[pl.ds(pl.multiple_of((k * tm + j + t8) * PK_S, PK_S), PK_S), :]
                pltpu.make_async_copy(y_hbm.at[pl.ds(row, PK_S), :], dst, sem).start(priority=k % 2)
        return carry

    lax.fori_loop(0, tm // ROW_UNROLL, issue, 0)
    pltpu.make_async_copy(y_hbm.at[pl.ds(0, TOP_K * tm * PK_S), :], buf, sem).wait()

    w = w_ref[...]
    gate = mod_ref[0][5:6]
    acc = [None] * (2 * PK_S)
    for k in range(TOP_K):
        wk = w[:, k:k + 1]
        for s, piece in enumerate(_load_row_tiles(buf, k * tm * PK_S, tm)):
            for c, val in zip((s, PK_S + s), _unpack_bf16_pairs(piece)):
                acc[c] = wk * val if acc[c] is None else acc[c] + wk * val
    outs = []
    for c in range(2 * PK_S):
        sl = slice(c * LANE, (c + 1) * LANE)
        outs.append(x_ref[:, sl] + gate[:, sl] * (sh_ref[:, sl] + acc[c]))
    if fg_ref is not None:
        ssq = functools.reduce(lambda a, b: a + b, [jnp.sum(o * o, axis=-1, keepdims=True) for o in outs])
        inv = lax.rsqrt(ssq / D + EPS)
        outs = [o * inv * fg_ref[:, c * LANE:(c + 1) * LANE] for c, o in enumerate(outs)]
    for c, o in enumerate(outs):
        o_ref[:, c * LANE:(c + 1) * LANE] = o


def moe_combine(slots, wts_tok, shared, x, mod, y, n_tok, final_gain=None):
    tm = COMBINE_TM
    final = final_gain is not None
    in_specs = [pl.BlockSpec((TOP_K * tm,), lambda i: (i,), memory_space=pltpu.SMEM),
                pl.BlockSpec((tm, TOP_K), lambda i: (i, 0)),
                pl.BlockSpec((tm, D), lambda i: (i, 0)),
                pl.BlockSpec((tm, D), lambda i: (i, 0)),
                pl.BlockSpec((1, 6, D), lambda i: (_mod_row(i * tm), 0, 0)),
                pl.BlockSpec(memory_space=pl.ANY)]
    args = [slots, wts_tok, shared, x, mod, y]
    if final:
        in_specs.append(pl.BlockSpec((1, D), lambda i: (0, 0)))
        args.append(final_gain.reshape(1, D))
    return pl.pallas_call(
        functools.partial(_combine_kernel, final=final),
        grid=(n_tok // tm,),
        in_specs=in_specs,
        out_specs=pl.BlockSpec((tm, D), lambda i: (i, 0)),
        out_shape=jax.ShapeDtypeStruct((n_tok, D), F32),
        scratch_shapes=[pltpu.VMEM((TOP_K * tm * PK_S, LANE), U32), pltpu.SemaphoreType.DMA(())],
        compiler_params=_cp(("arbitrary",)),
        name="moe_combine",
    )(*args)


def _lookup(table, idx):
    e = jnp.arange(table.shape[0], dtype=I32).reshape((-1,) + (1,) * idx.ndim)
    return jnp.sum(jnp.where(idx[None] == e, table.reshape(e.shape), 0), axis=0)


def _tile_flat(slots, tm):
    k, t = slots.shape
    return slots.reshape(k, t // tm, tm).transpose(1, 0, 2).reshape(-1)


def moe_layer(x, mod, norm_gain, router_w, router_b, exp_w1, exp_w3, exp_w2, sw1, sw3, sw2, layer, n_tok,
              final_gain=None):
    bm = EXP_BM
    n_blocks = -(-n_tok * TOP_K // bm) + N_EXPERTS
    hx, hx_packed = norm_mod(x, norm_gain, mod, 3, n_tok, pack=True)
    eidx, wts, pos, counts = moe_router(hx, router_w, router_b, n_tok)
    shared = shared_expert(hx, sw1.astype(BF16), sw3.astype(BF16), sw2.astype(BF16), n_tok)
    cnt = counts.astype(I32)
    padded = (cnt + bm - 1) // bm * bm
    pad_end = jnp.cumsum(padded)
    pad_start = pad_end - padded
    slots = _lookup(pad_start, eidx) + pos
    blk_row = jnp.arange(n_blocks, dtype=I32) * bm
    block_e = jnp.minimum(jnp.sum((pad_end[:, None] <= blk_row[None, :]).astype(I32), axis=0), N_EXPERTS - 1)
    valid = jnp.clip(_lookup(cnt, block_e) - (blk_row - _lookup(pad_start, block_e)), 0, bm).astype(I32)
    n_used = (pad_end[-1:] // bm).astype(I32)
    prev_e = jnp.concatenate([jnp.full((1,), -1, I32), block_e[:-1]])
    first = ((blk_row < pad_end[-1]) & (block_e != prev_e)).astype(I32)
    stage_slot = ((jnp.cumsum(first) - 1) % 2).astype(I32)
    eids = jnp.arange(N_EXPERTS, dtype=I32)
    later = jnp.where((eids[None, :] > eids[:, None]) & (padded[None, :] > 0), eids[None, :], N_EXPERTS)
    next_nonempty = jnp.min(later, axis=1)
    next_nonempty = jnp.where(next_nonempty == N_EXPERTS, -1, next_nonempty)
    next_e = _lookup(next_nonempty, block_e).astype(I32)
    xs = moe_dispatch(_tile_flat(slots, DISPATCH_TM), hx_packed, n_tok, n_blocks * bm)
    y = moe_experts(block_e, valid, n_used, first, next_e, stage_slot, xs, exp_w1, exp_w3, exp_w2, layer, n_blocks)
    return moe_combine(_tile_flat(slots, COMBINE_TM), wts.T, shared, x, mod, y, n_tok, final_gain)


def _rope_tables():
    t = jnp.arange(S, dtype=I32)
    row = (t // GRID_W).astype(F32)
    col = (t % GRID_W).astype(F32)
    n_freq = HD // 4
    inv_freq = ROPE_THETA ** (-jnp.arange(n_freq, dtype=F32) / n_freq)
    ang = jnp.concatenate([row[:, None] * inv_freq, col[:, None] * inv_freq], axis=-1)
    cosf = jnp.repeat(jnp.cos(ang), 2, axis=-1)
    sinf = jnp.stack([-jnp.sin(ang), jnp.sin(ang)], axis=-1).reshape(S, HD)
    return cosf, sinf


def kernel(x, c, ctx, c_ctx, ada_w, ada_b, norm_mix, norm_ffn, attn_w_in, attn_w_out, attn_rpb, attn_q_gain,
           attn_k_gain, ml_w_in, ml_w_out, ml_gate_b, ml_head_gain, router_w, router_b, exp_w1, exp_w3, exp_w2,
           sh_w1, sh_w3, sh_w2, final_norm_gain):
    depth = ada_w.shape[0]
    xa = jnp.concatenate([x.reshape(T_LAT, D), ctx.reshape(T_CTX, D)], axis=0)
    cvec = jnp.concatenate([c, c_ctx[None], jnp.zeros((8 - B - 1, D), F32)], axis=0)
    mod_all = ada_ln(cvec, ada_w, ada_b).reshape(depth, 8, 6, D)
    cosf, sinf = _rope_tables()

    mod = mod_all[0]
    p = norm_matmul(xa, norm_mix[0], mod, attn_w_in[0].astype(BF16), emit_h=False)
    o_na = neighborhood_attention(p, na_bias_table(attn_rpb[0]))
    o_gqa = gqa_attention(p, cosf, sinf, attn_q_gain[0], attn_k_gain[0])
    o_ctx = ctx_attention(p, attn_q_gain[0], attn_k_gain[0])
    o_all = jnp.concatenate([jnp.concatenate([o_na, o_gqa], axis=1), o_ctx], axis=0)
    xa = matmul_gated_residual(o_all, attn_w_out[0].astype(BF16), xa, mod, 2)
    xa = moe_layer(xa, mod, norm_ffn[0], router_w[0], router_b[0], exp_w1, exp_w3, exp_w2,
                   sh_w1[0], sh_w3[0], sh_w2[0], 0, T_ALL)

    mod = mod_all[1]
    w_in = ml_w_in[0]
    p, hx = norm_matmul(xa, norm_mix[1], mod, w_in[:, :ML_MAIN].astype(BF16), emit_h=True)
    col3, row4 = mlstm_gates(hx, w_in[:, ML_MAIN:], ml_gate_b[0])
    hdir = mlstm_scan(p, col3, row4)
    a = mlstm_readout(hdir, p, ml_head_gain[0])
    xl = matmul_gated_residual(a, ml_w_out[0].astype(BF16), xa, mod, 2)
    xl = moe_layer(xl, mod, norm_ffn[1], router_w[1], router_b[1], exp_w1, exp_w3, exp_w2,
                   sh_w1[1], sh_w3[1], sh_w2[1], 1, T_LAT, final_gain=final_norm_gain)
    return xl.reshape(B, S, D)
```

```python
import functools

import jax
import jax.numpy as jnp
from jax import lax
from jax.experimental import pallas as pl
from jax.experimental.pallas import tpu as pltpu

F32 = jnp.float32
BF16 = jnp.bfloat16
I32 = jnp.int32
U32 = jnp.uint32

D = 2048
B = 4
S = 4096
L = 256
T_LAT = B * S
T_CTX = B * L
T_ALL = T_LAT + T_CTX
GRID_W = 64
ROWS = S // GRID_W
HD = 128
NA_HEADS = 8
NA_WIN_ROWS = 8
NA_WIN_COLS = 16
GQA_Q_HEADS = 8
GQA_KV_HEADS = 2
GQA_GROUP = 4
ROPE_THETA = 10000.0
ATTN_IN = 4608
ML_HEADS = 8
ML_V = 256
ML_QK = 128
ML_MAIN = 6144
N_EXPERTS = 64
TOP_K = 8
N_GROUPS = 8
TOPK_GROUPS = 4
EXPERT_DIM = 512
ROUTED_SCALE = 2.5
EPS = 1e-6
NEG_INF = -1e30
ATT_SCALE = HD ** -0.5
LOG2E = 1.4426950408889634
ML_KSCALE = ML_QK ** -0.5

LANE = 128
NA_QROWS = 4
NA_SLAB = NA_QROWS + NA_WIN_ROWS - 1
NA_QB = NA_QROWS * GRID_W
NA_KB = NA_SLAB * GRID_W
ML_CH = 256
ML_HPS = 4
EXP_BM = 512
PK_W = D // 2
PK_S = PK_W // LANE
VMEM_LIMIT = 48 * 1024 * 1024


def _cp(sem, vmem=VMEM_LIMIT):
    return pltpu.CompilerParams(dimension_semantics=sem, vmem_limit_bytes=vmem)


def _pack_bf16_pairs(xb):
    u = pltpu.bitcast(xb.astype(F32), U32)
    return (u[:, PK_W:] & jnp.uint32(0xFFFF0000)) | (u[:, :PK_W] >> 16)


def _unpack_bf16_pairs(u):
    return pltpu.bitcast(u << 16, F32), pltpu.bitcast(u & jnp.uint32(0xFFFF0000), F32)


def _store_row_tiles(ref, words):
    rows = words.shape[0]
    for s in range(PK_S):
        ref[pl.ds(s, rows, stride=PK_S), :] = words[:, s * LANE:(s + 1) * LANE]


def _load_row_tiles(ref, start, rows):
    return [ref[pl.ds(start + s, rows, stride=PK_S), :] for s in range(PK_S)]


def _mod_row(start_row):
    return jnp.where(start_row < T_LAT, start_row // S, B)


def _ada_kernel(c_ref, w_ref, b_ref, o_ref):
    c = c_ref[...]
    a = (c * jax.nn.sigmoid(c)).astype(BF16)
    w = w_ref[0].astype(BF16)
    o_ref[0] = jnp.dot(a, w, preferred_element_type=F32) + b_ref[0]


def ada_ln(cvec, ada_w, ada_b):
    depth = ada_w.shape[0]
    n = ada_w.shape[2]
    tn = 1024
    return pl.pallas_call(
        _ada_kernel,
        grid=(depth, n // tn),
        in_specs=[pl.BlockSpec((8, D), lambda l, j: (0, 0)),
                  pl.BlockSpec((1, D, tn), lambda l, j: (l, 0, j)),
                  pl.BlockSpec((1, 1, tn), lambda l, j: (l, 0, j))],
        out_specs=pl.BlockSpec((1, 8, tn), lambda l, j: (l, 0, j)),
        out_shape=jax.ShapeDtypeStruct((depth, 8, n), F32),
        compiler_params=_cp(("arbitrary", "arbitrary")),
        name="ada_ln",
    )(cvec, ada_w, ada_b.reshape(depth, 1, n))


def _norm_mod_kernel(x_ref, g_ref, mod_ref, *out_refs, base, pack):
    x = x_ref[...]
    y = x * lax.rsqrt(jnp.mean(x * x, axis=-1, keepdims=True) + EPS) * g_ref[...]
    m = mod_ref[0]
    h = y * (1.0 + m[base + 1:base + 2]) + m[base:base + 1]
    hb = h.astype(BF16)
    out_refs[0][...] = hb
    if pack:
        _store_row_tiles(out_refs[1], _pack_bf16_pairs(hb))


def norm_mod(x, gain, mod, base, n_rows, pack):
    tm = 256
    out_shape = [jax.ShapeDtypeStruct((n_rows, D), BF16)]
    out_specs = [pl.BlockSpec((tm, D), lambda i: (i, 0))]
    if pack:
        out_shape.append(jax.ShapeDtypeStruct((n_rows * PK_S, LANE), U32))
        out_specs.append(pl.BlockSpec((tm * PK_S, LANE), lambda i: (i, 0)))
    res = pl.pallas_call(
        functools.partial(_norm_mod_kernel, base=base, pack=pack),
        grid=(n_rows // tm,),
        in_specs=[pl.BlockSpec((tm, D), lambda i: (i, 0)),
                  pl.BlockSpec((1, D), lambda i: (0, 0)),
                  pl.BlockSpec((1, 6, D), lambda i: (_mod_row(i * tm), 0, 0))],
        out_specs=out_specs,
        out_shape=out_shape,
        compiler_params=_cp(("arbitrary",)),
        name="norm_mod",
    )(x, gain.reshape(1, D), mod)
    return res if pack else res[0]


def _norm_mm_kernel(x_ref, g_ref, mod_ref, w_ref, o_ref, *rest, emit_h):
    hb_ref = rest[-1]

    @pl.when(pl.program_id(1) == 0)
    def _():
        x = x_ref[...]
        y = x * lax.rsqrt(jnp.mean(x * x, axis=-1, keepdims=True) + EPS) * g_ref[...]
        m = mod_ref[0]
        hb_ref[...] = (y * (1.0 + m[1:2]) + m[0:1]).astype(BF16)
        if emit_h:
            rest[0][...] = hb_ref[...]

    o_ref[...] = jnp.dot(hb_ref[...], w_ref[...], preferred_element_type=F32).astype(o_ref.dtype)


def norm_matmul(x, gain, mod, w, emit_h, tm=1024, tn=512):
    m = x.shape[0]
    n = w.shape[1]
    out_shape = [jax.ShapeDtypeStruct((m, n), BF16)]
    out_specs = [pl.BlockSpec((tm, tn), lambda i, j: (i, j))]
    if emit_h:
        out_shape.append(jax.ShapeDtypeStruct((m, D), BF16))
        out_specs.append(pl.BlockSpec((tm, D), lambda i, j: (i, 0)))
    res = pl.pallas_call(
        functools.partial(_norm_mm_kernel, emit_h=emit_h),
        grid=(m // tm, n // tn),
        in_specs=[pl.BlockSpec((tm, D), lambda i, j: (i, 0)),
                  pl.BlockSpec((1, D), lambda i, j: (0, 0)),
                  pl.BlockSpec((1, 6, D), lambda i, j: (_mod_row(i * tm), 0, 0)),
                  pl.BlockSpec((D, tn), lambda i, j: (0, j))],
        out_specs=out_specs,
        out_shape=out_shape,
        scratch_shapes=[pltpu.VMEM((tm, D), BF16)],
        compiler_params=_cp(("arbitrary", "arbitrary")),
        name="norm_matmul",
    )(x, gain.reshape(1, D), mod, w)
    return res if emit_h else res[0]


def _mm_res_kernel(a_ref, w_ref, x_ref, mod_ref, o_ref, *, slot):
    acc = jnp.dot(a_ref[...], w_ref[...], preferred_element_type=F32)
    o_ref[...] = x_ref[...] + mod_ref[0][slot:slot + 1] * acc


def matmul_gated_residual(a, w, x, mod, slot, tm=1024, tn=512):
    m, k = a.shape
    n = w.shape[1]
    return pl.pallas_call(
        functools.partial(_mm_res_kernel, slot=slot),
        grid=(m // tm, n // tn),
        in_specs=[pl.BlockSpec((tm, k), lambda i, j: (i, 0)),
                  pl.BlockSpec((k, tn), lambda i, j: (0, j)),
                  pl.BlockSpec((tm, tn), lambda i, j: (i, j)),
                  pl.BlockSpec((1, 6, tn), lambda i, j: (_mod_row(i * tm), 0, j))],
        out_specs=pl.BlockSpec((tm, tn), lambda i, j: (i, j)),
        out_shape=jax.ShapeDtypeStruct((m, n), F32),
        compiler_params=_cp(("arbitrary", "arbitrary")),
        name="matmul_gated_residual",
    )(a, w, x, mod)


def _dot_nt(a, b):
    return lax.dot_general(a, b, (((1,), (1,)), ((), ())), preferred_element_type=F32)


def _rms_head(x, gain):
    return x * lax.rsqrt(jnp.mean(x * x, axis=-1, keepdims=True) + EPS) * gain


def _rope(x, cosf, sinf):
    lane = lax.broadcasted_iota(I32, x.shape, 1)
    nxt = pltpu.roll(x, LANE - 1, 1)
    prv = pltpu.roll(x, 1, 1)
    return x * cosf + jnp.where((lane & 1) == 0, nxt, prv) * sinf


def _softmax_av(parts):
    m = functools.reduce(jnp.maximum, [jnp.max(s, axis=-1, keepdims=True) for s, _ in parts])
    l = None
    o = None
    for s, v in parts:
        p = jnp.exp(s - m)
        li = jnp.sum(p, axis=-1, keepdims=True)
        oi = jnp.dot(p.astype(BF16), v, preferred_element_type=F32)
        l = li if l is None else l + li
        o = oi if o is None else o + oi
    return o / l


def _na_kernel(q_ref, k_ref, v_ref, kc_ref, vc_ref, tab_ref, o_ref):
    kc = kc_ref[...]
    vc = vc_ref[...]
    n_blocks = ROWS // NA_QROWS

    def body(j, carry):
        ks = jnp.clip(j * NA_QROWS - NA_WIN_ROWS // 2, 0, ROWS - NA_SLAB)
        typ = jnp.where(j == 0, 0, jnp.where(j == n_blocks - 1, 2, 1))
        qs = pl.multiple_of(j * NA_QB, NA_QB)
        kst = pl.multiple_of(ks * GRID_W, GRID_W)
        q = q_ref[pl.ds(qs, NA_QB), :]
        k = k_ref[pl.ds(kst, NA_KB), :]
        v = v_ref[pl.ds(kst, NA_KB), :]
        s_win = _dot_nt(q, k) * ATT_SCALE + tab_ref[typ, 0]
        s_ctx = _dot_nt(q, kc) * ATT_SCALE
        o_ref[pl.ds(qs, NA_QB), :] = _softmax_av([(s_win, v), (s_ctx, vc)]).astype(BF16)
        return carry

    lax.fori_loop(0, n_blocks, body, 0)


def na_bias_table(rpb):
    def one(r0, ks):
        r = r0 + jnp.arange(NA_QROWS)
        kr = ks + jnp.arange(NA_SLAB)
        start = jnp.clip(r - NA_WIN_ROWS // 2, 0, ROWS - NA_WIN_ROWS)
        row_ok = (kr[None, :] >= start[:, None]) & (kr[None, :] < start[:, None] + NA_WIN_ROWS)
        row_idx = jnp.clip(kr[None, :] - r[:, None] + NA_WIN_ROWS - 1, 0, 2 * NA_WIN_ROWS - 2)
        cq = jnp.arange(GRID_W)
        col_start = jnp.clip(cq - NA_WIN_COLS // 2, 0, GRID_W - NA_WIN_COLS)
        col_ok = (cq[None, :] >= col_start[:, None]) & (cq[None, :] < col_start[:, None] + NA_WIN_COLS)
        col_idx = jnp.clip(cq[None, :] - cq[:, None] + NA_WIN_COLS - 1, 0, 2 * NA_WIN_COLS - 2)
        r_hot = jax.nn.one_hot(row_idx, 2 * NA_WIN_ROWS - 1, dtype=F32)
        c_hot = jax.nn.one_hot(col_idx, 2 * NA_WIN_COLS - 1, dtype=F32)
        bias = jnp.einsum('qka,hab,xyb->hqxky', r_hot, rpb.astype(F32), c_hot, precision=lax.Precision.HIGHEST)
        ok = row_ok[:, None, :, None] & col_ok[None, :, None, :]
        return jnp.where(ok[None], bias, NEG_INF).reshape(NA_HEADS, NA_QB, NA_KB)

    mid = 2 * NA_QROWS
    last = ROWS - NA_QROWS
    return jnp.stack([one(0, 0), one(mid, mid - NA_WIN_ROWS // 2), one(last, ROWS - NA_SLAB)])


def neighborhood_attention(p, table):
    cb = S // L
    return pl.pallas_call(
        _na_kernel,
        grid=(NA_HEADS, B),
        in_specs=[pl.BlockSpec((S, HD), lambda h, b: (b, h)),
                  pl.BlockSpec((S, HD), lambda h, b: (b, NA_HEADS + h)),
                  pl.BlockSpec((S, HD), lambda h, b: (b, 2 * NA_HEADS + h)),
                  pl.BlockSpec((L, HD), lambda h, b: (B * cb + b, NA_HEADS + h)),
                  pl.BlockSpec((L, HD), lambda h, b: (B * cb + b, 2 * NA_HEADS + h)),
                  pl.BlockSpec((3, 1, NA_QB, NA_KB), lambda h, b: (0, h, 0, 0))],
        out_specs=pl.BlockSpec((S, HD), lambda h, b: (b, h)),
        out_shape=jax.ShapeDtypeStruct((T_ALL, D), BF16),
        compiler_params=_cp(("arbitrary", "arbitrary")),
        name="neighborhood_attention",
    )(p, p, p, p, p, table)


GQA_TQ = 256
GQA_CK = 512
GQA_QCOL = 3 * NA_HEADS
GQA_KCOL = GQA_QCOL + GQA_Q_HEADS
GQA_VCOL = GQA_KCOL + GQA_KV_HEADS


def _gqa_kernel(q_ref, k_ref, v_ref, kc_ref, vc_ref, cq_ref, sq_ref, ck_ref, sk_ref, qg_ref, kg_ref, o_prev,
                o_ref, kn_ref, kcn_ref):
    del o_prev
    @pl.when(pl.program_id(2) == 0)
    def _():
        kn = _rope(_rms_head(k_ref[...].astype(F32), kg_ref[...]), ck_ref[...], sk_ref[...])
        kn_ref[...] = kn.astype(BF16)
        kcn_ref[...] = _rms_head(kc_ref[...].astype(F32), kg_ref[...]).astype(BF16)

    cos = cq_ref[...]
    sin = sq_ref[...]
    heads = []
    for g in range(GQA_GROUP):
        qh = _rope(_rms_head(q_ref[:, g * HD:(g + 1) * HD].astype(F32), qg_ref[...]), cos, sin)
        heads.append((qh * (ATT_SCALE * LOG2E)).astype(BF16))
    q = jnp.concatenate(heads, axis=0)
    chunks = [(kn_ref[c * GQA_CK:(c + 1) * GQA_CK, :], v_ref[c * GQA_CK:(c + 1) * GQA_CK, :])
              for c in range(S // GQA_CK)]
    chunks.append((kcn_ref[...], vc_ref[...]))
    m = l = acc = None
    for kk, vv in chunks:
        s = _dot_nt(q, kk)
        mc = jnp.max(s, axis=-1, keepdims=True)
        if m is None:
            m_new = mc
            p = jnp.exp2(s - m_new)
            l = jnp.sum(p, axis=-1, keepdims=True)
            acc = jnp.dot(p.astype(BF16), vv, preferred_element_type=F32)
        else:
            m_new = jnp.maximum(m, mc)
            alpha = jnp.exp2(m - m_new)
            p = jnp.exp2(s - m_new)
            l = alpha * l + jnp.sum(p, axis=-1, keepdims=True)
            acc = alpha * acc + jnp.dot(p.astype(BF16), vv, preferred_element_type=F32)
        m = m_new
    o = acc / l
    for g in range(GQA_GROUP):
        o_ref[:, g * HD:(g + 1) * HD] = o[g * GQA_TQ:(g + 1) * GQA_TQ].astype(BF16)


def gqa_attention(p, cosf, sinf, q_gain, k_gain, o_buf):
    nq = S // GQA_TQ
    cb = S // L
    gw = GQA_GROUP * HD
    return pl.pallas_call(
        _gqa_kernel,
        grid=(B, GQA_KV_HEADS, nq),
        in_specs=[pl.BlockSpec((GQA_TQ, gw), lambda b, n, i: (b * nq + i, GQA_QCOL // GQA_GROUP + n)),
                  pl.BlockSpec((S, HD), lambda b, n, i: (b, GQA_KCOL + n)),
                  pl.BlockSpec((S, HD), lambda b, n, i: (b, GQA_VCOL + n)),
                  pl.BlockSpec((L, HD), lambda b, n, i: (B * cb + b, GQA_KCOL + n)),
                  pl.BlockSpec((L, HD), lambda b, n, i: (B * cb + b, GQA_VCOL + n)),
                  pl.BlockSpec((GQA_TQ, HD), lambda b, n, i: (i, 0)),
                  pl.BlockSpec((GQA_TQ, HD), lambda b, n, i: (i, 0)),
                  pl.BlockSpec((S, HD), lambda b, n, i: (0, 0)),
                  pl.BlockSpec((S, HD), lambda b, n, i: (0, 0)),
                  pl.BlockSpec((1, HD), lambda b, n, i: (0, 0)),
                  pl.BlockSpec((1, HD), lambda b, n, i: (0, 0)),
                  pl.BlockSpec(memory_space=pl.ANY)],
        out_specs=pl.BlockSpec((GQA_TQ, gw), lambda b, n, i: (b * nq + i, (NA_HEADS * HD) // gw + n)),
        out_shape=jax.ShapeDtypeStruct((T_ALL, D), BF16),
        input_output_aliases={11: 0},
        scratch_shapes=[pltpu.VMEM((S, HD), BF16), pltpu.VMEM((L, HD), BF16)],
        compiler_params=_cp(("arbitrary",) * 3),
        name="gqa_attention",
    )(p, p, p, p, p, cosf, sinf, cosf, sinf, q_gain.reshape(1, HD), k_gain.reshape(1, HD), o_buf)


def _ctx_attn_kernel(p_ref, qg_ref, kg_ref, o_prev, o_ref):
    del o_prev

    def col(c):
        return p_ref[:, c * HD:(c + 1) * HD]

    for h in range(NA_HEADS):
        s = _dot_nt(col(h), col(NA_HEADS + h)) * ATT_SCALE
        o_ref[:, h * HD:(h + 1) * HD] = _softmax_av([(s, col(2 * NA_HEADS + h))]).astype(BF16)
    for n in range(GQA_KV_HEADS):
        kn = _rms_head(col(GQA_KCOL + n).astype(F32), kg_ref[...]).astype(BF16)
        v = col(GQA_VCOL + n)
        for g in range(GQA_GROUP):
            h = n * GQA_GROUP + g
            qn = _rms_head(col(GQA_QCOL + h).astype(F32), qg_ref[...]).astype(BF16)
            s = _dot_nt(qn, kn) * ATT_SCALE
            o_ref[:, (NA_HEADS + h) * HD:(NA_HEADS + h + 1) * HD] = _softmax_av([(s, v)]).astype(BF16)


def ctx_attention(p, q_gain, k_gain, o_buf):
    cb = S // L
    return pl.pallas_call(
        _ctx_attn_kernel,
        grid=(B,),
        in_specs=[pl.BlockSpec((L, ATTN_IN), lambda b: (B * cb + b, 0)),
                  pl.BlockSpec((1, HD), lambda b: (0, 0)),
                  pl.BlockSpec((1, HD), lambda b: (0, 0)),
                  pl.BlockSpec(memory_space=pl.ANY)],
        out_specs=pl.BlockSpec((L, D), lambda b: (B * cb + b, 0)),
        out_shape=jax.ShapeDtypeStruct((T_ALL, D), BF16),
        input_output_aliases={3: 0},
        compiler_params=_cp(("arbitrary",)),
        name="ctx_attention",
    )(p, q_gain.reshape(1, HD), k_gain.reshape(1, HD), o_buf)


def _log_sigmoid(x):
    return -(jnp.maximum(-x, 0.0) + jnp.log1p(jnp.exp(-jnp.abs(x))))


def _dot_hi(a, b):
    return jnp.dot(a, b, precision=lax.Precision.HIGHEST, preferred_element_type=F32)


def _gate_kernel(h_ref, wg_ref, wgt_ref, b_ref, bt_ref, lt_ref, ut_ref, col_ref, row_ref):
    nh = ML_HEADS
    hx = h_ref[...]
    g = jnp.dot(hx, wg_ref[...], preferred_element_type=F32) + b_ref[...]
    gt = _dot_nt(wgt_ref[...], hx) + bt_ref[...]
    li = g[:, 0:2 * nh]
    lf = _log_sigmoid(g[:, 2 * nh:4 * nh])
    lit = gt[0:2 * nh]
    lft = _log_sigmoid(gt[2 * nh:4 * nh])
    lt = lt_ref[...]
    ut = ut_ref[...]
    lane = lax.broadcasted_iota(I32, lf.shape, 1)
    bc = jnp.where(lane < nh, _dot_hi(lt, lf), _dot_hi(ut, lf))
    tot = jnp.sum(lf, axis=0, keepdims=True)
    aend = tot - bc + li
    col_ref[...] = jnp.concatenate([bc, aend, jnp.zeros((ML_CH, LANE - 4 * nh), F32)], axis=1)
    sub = lax.broadcasted_iota(I32, lft.shape, 0)
    bct = jnp.where(sub < nh, _dot_hi(lft, ut), _dot_hi(lft, lt))
    tott = jnp.sum(lft, axis=1, keepdims=True)
    gtr = lit - bct
    row_ref[0] = jnp.concatenate([bct, gtr, tott + gtr, jnp.broadcast_to(tott, bct.shape)], axis=0)


def mlstm_gates(hx, wg, gate_b):
    nh = ML_HEADS
    n_ch = T_ALL // ML_CH
    wg_pad = jnp.zeros((D, LANE), BF16).at[:, :4 * nh].set(wg.astype(BF16))
    b_pad = jnp.zeros((1, LANE), F32).at[0, :4 * nh].set(gate_b.reshape(-1))
    wgt = wg.astype(BF16).T
    bt = gate_b.reshape(4 * nh, 1).astype(F32)
    lt = jnp.tril(jnp.ones((ML_CH, ML_CH), F32))
    ut = jnp.triu(jnp.ones((ML_CH, ML_CH), F32))
    col, row = pl.pallas_call(
        _gate_kernel,
        grid=(n_ch,),
        in_specs=[pl.BlockSpec((ML_CH, D), lambda i: (i, 0)),
                  pl.BlockSpec((D, LANE), lambda i: (0, 0)),
                  pl.BlockSpec((4 * nh, D), lambda i: (0, 0)),
                  pl.BlockSpec((1, LANE), lambda i: (0, 0)),
                  pl.BlockSpec((4 * nh, 1), lambda i: (0, 0)),
                  pl.BlockSpec((ML_CH, ML_CH), lambda i: (0, 0)),
                  pl.BlockSpec((ML_CH, ML_CH), lambda i: (0, 0))],
        out_specs=[pl.BlockSpec((ML_CH, LANE), lambda i: (i, 0)),
                   pl.BlockSpec((1, 8 * nh, ML_CH), lambda i: (i, 0, 0))],
        out_shape=[jax.ShapeDtypeStruct((T_ALL, LANE), F32),
                   jax.ShapeDtypeStruct((n_ch, 8 * nh, ML_CH), F32)],
        compiler_params=_cp(("arbitrary",)),
        name="mlstm_gates",
    )(hx, wg_pad, wgt, b_pad, bt, lt, ut)
    col3 = col[:, :4 * nh].reshape(T_ALL, 2, 2 * nh).transpose(2, 0, 1)
    col3 = jnp.pad(col3, ((0, 0), (0, 0), (0, 6)))
    row4 = row.reshape(n_ch, 4, 2 * nh, ML_CH).transpose(2, 0, 1, 3)
    row4 = jnp.pad(row4, ((0, 0), (0, 0), (0, 4), (0, 0)))
    return col3, row4


def _mlstm_step(d, hh, q_ref, k_ref, v_ref, col_ref, row_ref, o_ref, c_ref, n_ref, m_ref):
    sl = d * ML_HPS + hh
    q = q_ref[:, hh * ML_QK:(hh + 1) * ML_QK]
    kf = k_ref[:, hh * ML_QK:(hh + 1) * ML_QK].astype(F32) * ML_KSCALE
    kb = kf.astype(BF16)
    v = v_ref[:, hh * ML_V:(hh + 1) * ML_V]
    col = col_ref[hh]
    row = row_ref[hh, 0]
    bc_col = col[:, 0:1]
    aend_col = col[:, 1:2]
    g_row = row[1:2]
    aend_row = row[2:3]
    btot = row[3:4, 0:1]
    m_st = m_ref[sl]
    c_st = c_ref[sl]
    n_st = n_ref[sl]
    m_new = jnp.maximum(btot + m_st, jnp.max(aend_row, axis=1, keepdims=True))

    r = lax.broadcasted_iota(I32, (ML_CH, ML_CH), 0)
    c = lax.broadcasted_iota(I32, (ML_CH, ML_CH), 1)
    causal = (r >= c) if d == 0 else (r <= c)
    d_mat = jnp.where(causal, bc_col + g_row, -jnp.inf)
    m_row = jnp.maximum(bc_col + m_st, jnp.max(d_mat, axis=1, keepdims=True))
    w_inter = jnp.exp(bc_col + m_st - m_row)
    s_mat = _dot_nt(q, kb) * jnp.exp(d_mat - m_row)
    num = (w_inter * jnp.dot(q, c_st.astype(BF16), preferred_element_type=F32)
           + jnp.dot(s_mat.astype(BF16), v, preferred_element_type=F32))
    den = (w_inter * jnp.sum(q.astype(F32) * n_st, axis=1, keepdims=True)
           + jnp.sum(s_mat, axis=1, keepdims=True))
    o_ref[:, hh * ML_V:(hh + 1) * ML_V] = num / jnp.maximum(jnp.abs(den), jnp.exp(-m_row))

    w_end_col = jnp.exp(aend_col - m_new)
    w_end_row = jnp.exp(aend_row - m_new)
    decay = jnp.exp(btot + m_st - m_new)
    kw = (kf * w_end_col).astype(BF16)
    c_ref[sl] = decay * c_st + lax.dot_general(kw, v, (((0,), (0,)), ((), ())), preferred_element_type=F32)
    w8 = jnp.broadcast_to(w_end_row, (8, ML_CH)).astype(BF16)
    n_ref[sl] = decay * n_st + jnp.dot(w8, kb, preferred_element_type=F32)[0:1]
    m_ref[sl] = m_new


def _mlstm_kernel(qf, kf, vf, colf, rowf, qb, kb, vb, colb, rowb, of, ob, c_ref, n_ref, m_ref):
    @pl.when(pl.program_id(2) == 0)
    def _():
        c_ref[...] = jnp.zeros_like(c_ref)
        n_ref[...] = jnp.zeros_like(n_ref)
        m_ref[...] = jnp.zeros_like(m_ref)

    for hh in range(ML_HPS):
        _mlstm_step(0, hh, qf, kf, vf, colf, rowf, of, c_ref, n_ref, m_ref)
        _mlstm_step(1, hh, qb, kb, vb, colb, rowb, ob, c_ref, n_ref, m_ref)


def mlstm_scan(p, col3, row4):
    n_lat = S // ML_CH
    steps = n_lat + 1
    lat_blocks = T_LAT // ML_CH
    hps = ML_HPS
    n_groups = ML_HEADS // hps
    kcol = (ML_HEADS * ML_QK) // (hps * ML_QK)
    vcol = (2 * ML_HEADS * ML_QK) // (hps * ML_V)

    def chunk(b, d, st):
        c = (st - 1) if d == 0 else (n_lat - st)
        return jnp.where(st == 0, lat_blocks + b, b * n_lat + c)

    def out_chunk(b, d, st):
        s1 = jnp.maximum(st, 1)
        return b * n_lat + ((s1 - 1) if d == 0 else (n_lat - s1))

    def dir_specs(d):
        return [pl.BlockSpec((ML_CH, hps * ML_QK), lambda b, h, s: (chunk(b, d, s), h)),
                pl.BlockSpec((ML_CH, hps * ML_QK), lambda b, h, s: (chunk(b, d, s), kcol + h)),
                pl.BlockSpec((ML_CH, hps * ML_V), lambda b, h, s: (chunk(b, d, s), vcol + h)),
                pl.BlockSpec((hps, ML_CH, 8), lambda b, h, s: (d * n_groups + h, chunk(b, d, s), 0)),
                pl.BlockSpec((hps, 1, 8, ML_CH), lambda b, h, s: (d * n_groups + h, chunk(b, d, s), 0, 0))]

    return pl.pallas_call(
        _mlstm_kernel,
        grid=(B, n_groups, steps),
        in_specs=dir_specs(0) + dir_specs(1),
        out_specs=[pl.BlockSpec((ML_CH, hps * ML_V), lambda b, h, s: (out_chunk(b, 0, s), h)),
                   pl.BlockSpec((ML_CH, hps * ML_V), lambda b, h, s: (out_chunk(b, 1, s), h))],
        out_shape=[jax.ShapeDtypeStruct((T_LAT, ML_HEADS * ML_V), F32),
                   jax.ShapeDtypeStruct((T_LAT, ML_HEADS * ML_V), F32)],
        scratch_shapes=[pltpu.VMEM((2 * hps, ML_QK, ML_V), F32), pltpu.VMEM((2 * hps, 1, ML_QK), F32),
                        pltpu.VMEM((2 * hps, 1, 1), F32)],
        compiler_params=_cp(("arbitrary",) * 3),
        name="mlstm_scan",
    )(p, p, p, col3, row4, p, p, p, col3, row4)


def _readout_kernel(hf_ref, hb_ref, o_ref, g_ref, a_ref):
    hs = hf_ref[...] + hb_ref[...]
    for h in range(ML_HEADS):
        sl = slice(h * ML_V, (h + 1) * ML_V)
        x = hs[:, sl]
        hn = x * lax.rsqrt(jnp.mean(x * x, axis=-1, keepdims=True) + EPS) * g_ref[:, sl]
        a_ref[:, sl] = (hn * jax.nn.sigmoid(o_ref[:, sl].astype(F32))).astype(BF16)


def mlstm_readout(hdir, p, head_gain):
    tm = 256
    ocol = (2 * ML_HEADS * ML_QK + ML_HEADS * ML_V) // D
    return pl.pallas_call(
        _readout_kernel,
        grid=(T_LAT // tm,),
        in_specs=[pl.BlockSpec((tm, D), lambda i: (i, 0)),
                  pl.BlockSpec((tm, D), lambda i: (i, 0)),
                  pl.BlockSpec((tm, D), lambda i: (i, ocol)),
                  pl.BlockSpec((1, D), lambda i: (0, 0))],
        out_specs=pl.BlockSpec((tm, D), lambda i: (i, 0)),
        out_shape=jax.ShapeDtypeStruct((T_LAT, D), BF16),
        compiler_params=_cp(("arbitrary",)),
        name="mlstm_readout",
    )(hdir[0], hdir[1], p, head_gain.reshape(1, D))


ROUTER_TM = 512


def _router_kernel(h_ref, w_ref, rb_ref, erow_ref, tri_ref, eidx_ref, wts_ref, pos_ref, cnt_ref, carry_ref):
    ng = N_GROUPS
    epg = N_EXPERTS // N_GROUPS
    tm = ROUTER_TM
    ninf = -jnp.inf

    @pl.when(pl.program_id(0) == 0)
    def _():
        carry_ref[...] = jnp.zeros_like(carry_ref)

    s = jax.nn.sigmoid(_dot_nt(w_ref[...], h_ref[...]))
    ssel = s + rb_ref[...]
    sraw = [s[ng * j:ng * (j + 1)] for j in range(epg)]
    slab = [ssel[ng * j:ng * (j + 1)] for j in range(epg)]
    m1 = functools.reduce(jnp.maximum, slab)
    jfirst = functools.reduce(jnp.minimum, [jnp.where(slab[j] == m1, j, epg) for j in range(epg)])
    m2 = functools.reduce(jnp.maximum, [jnp.where(jfirst == j, ninf, slab[j]) for j in range(epg)])
    gs = m1 + m2
    giota = lax.broadcasted_iota(I32, (ng, tm), 0)
    gsel = jnp.zeros((ng, tm), F32)
    for _ in range(TOPK_GROUPS):
        mx = jnp.max(gs, axis=0, keepdims=True)
        gi = jnp.min(jnp.where(gs == mx, giota, ng), axis=0, keepdims=True)
        hit = giota == gi
        gsel = jnp.where(hit, 1.0, gsel)
        gs = jnp.where(hit, ninf, gs)
    msl = [jnp.where(gsel > 0.0, slab[j], ninf) for j in range(epg)]
    eid = [giota * epg + j for j in range(epg)]
    selm = [jnp.zeros((ng, tm), F32) for _ in range(epg)]
    e_list, w_list = [], []
    for _ in range(TOP_K):
        mx = jnp.max(functools.reduce(jnp.maximum, msl), axis=0, keepdims=True)
        cand = functools.reduce(jnp.minimum, [jnp.where(msl[j] == mx, eid[j], N_EXPERTS) for j in range(epg)])
        esel = jnp.min(cand, axis=0, keepdims=True)
        hits = [eid[j] == esel for j in range(epg)]
        wk = functools.reduce(lambda a, b: a + b, [jnp.where(hits[j], sraw[j], 0.0) for j in range(epg)])
        w_list.append(jnp.sum(wk, axis=0, keepdims=True))
        e_list.append(esel)
        msl = [jnp.where(hits[j], ninf, msl[j]) for j in range(epg)]
        selm = [jnp.where(hits[j], 1.0, selm[j]) for j in range(epg)]
    wsum = functools.reduce(lambda a, b: a + b, w_list)
    wts_ref[...] = jnp.concatenate([w / wsum * ROUTED_SCALE for w in w_list], axis=0)
    eidx_ref[...] = jnp.concatenate(e_list, axis=0)
    sel = jnp.concatenate(selm, axis=0)
    carry = carry_ref[...]
    posfull = jnp.dot(sel.astype(BF16), tri_ref[...], preferred_element_type=F32) + carry
    erow = erow_ref[...]
    pos = [jnp.sum(jnp.where(erow == e, posfull, 0.0), axis=0, keepdims=True) for e in e_list]
    pos_ref[...] = jnp.concatenate(pos, axis=0).astype(I32)
    carry = carry + jnp.sum(sel, axis=1, keepdims=True)
    carry_ref[...] = carry
    cnt_ref[...] = carry


def moe_router(hx, router_w, router_b, n_tok):
    tm = ROUTER_TM
    epg = N_EXPERTS // N_GROUPS
    perm = (jnp.arange(N_EXPERTS) % N_GROUPS) * epg + jnp.arange(N_EXPERTS) // N_GROUPS
    w_t = router_w.astype(BF16).T[perm]
    rb = router_b.astype(F32)[perm].reshape(N_EXPERTS, 1)
    erow = perm.astype(I32).reshape(N_EXPERTS, 1)
    tri = jnp.triu(jnp.ones((tm, tm), BF16), 1)
    eidx, wts, pos, counts = pl.pallas_call(
        _router_kernel,
        grid=(n_tok // tm,),
        in_specs=[pl.BlockSpec((tm, D), lambda i: (i, 0)),
                  pl.BlockSpec((N_EXPERTS, D), lambda i: (0, 0)),
                  pl.BlockSpec((N_EXPERTS, 1), lambda i: (0, 0)),
                  pl.BlockSpec((N_EXPERTS, 1), lambda i: (0, 0)),
                  pl.BlockSpec((tm, tm), lambda i: (0, 0))],
        out_specs=[pl.BlockSpec((TOP_K, tm), lambda i: (0, i)),
                   pl.BlockSpec((TOP_K, tm), lambda i: (0, i)),
                   pl.BlockSpec((TOP_K, tm), lambda i: (0, i)),
                   pl.BlockSpec((N_EXPERTS, 1), lambda i: (0, 0))],
        out_shape=[jax.ShapeDtypeStruct((TOP_K, n_tok), I32),
                   jax.ShapeDtypeStruct((TOP_K, n_tok), F32),
                   jax.ShapeDtypeStruct((TOP_K, n_tok), I32),
                   jax.ShapeDtypeStruct((N_EXPERTS, 1), F32)],
        scratch_shapes=[pltpu.VMEM((N_EXPERTS, 1), F32)],
        compiler_params=_cp(("arbitrary",)),
        name="moe_router",
    )(hx, w_t, rb, erow, tri)
    return eidx, wts, pos, counts.reshape(N_EXPERTS)[perm]


DISPATCH_TM = 512


ROW_UNROLL = 8


def _dispatch_kernel(slot_ref, hx_ref, xs_hbm, sem):
    def issue(tt, carry):
        t8 = pl.multiple_of(tt * ROW_UNROLL, ROW_UNROLL)
        for j in range(ROW_UNROLL):
            src = hx_ref.at[pl.ds(pl.multiple_of((t8 + j) * PK_S, PK_S), PK_S), :]
            for k in range(TOP_K):
                row = pl.multiple_of(slot_ref[k * DISPATCH_TM + j + t8] * PK_S, PK_S)
                pltpu.make_async_copy(src, xs_hbm.at[pl.ds(row, PK_S), :], sem).start(priority=k % 2)
        return carry

    lax.fori_loop(0, DISPATCH_TM // ROW_UNROLL, issue, 0)
    for _ in range(TOP_K):
        pltpu.make_async_copy(hx_ref, xs_hbm.at[pl.ds(0, DISPATCH_TM * PK_S), :], sem).wait()


def moe_dispatch(slots, hx_packed, n_tok, n_rows):
    tm = DISPATCH_TM
    return pl.pallas_call(
        _dispatch_kernel,
        grid=(n_tok // tm,),
        in_specs=[pl.BlockSpec((TOP_K * tm,), lambda i: (i,), memory_space=pltpu.SMEM),
                  pl.BlockSpec((tm * PK_S, LANE), lambda i: (i, 0))],
        out_specs=pl.BlockSpec(memory_space=pl.ANY),
        out_shape=jax.ShapeDtypeStruct((n_rows * PK_S, LANE), U32),
        scratch_shapes=[pltpu.SemaphoreType.DMA(())],
        compiler_params=_cp(("arbitrary",)),
        name="moe_dispatch",
    )(slots, hx_packed)


def _expert_kernel(be_ref, valid_ref, nused_ref, first_ref, next_ref, slot_ref,
                   xs_ref, w1_hbm, w3_hbm, w2_hbm, y_ref, w1s, w3s, w2s, w1b, w3b, w2b, xb, sems, *, layer):
    i = pl.program_id(0)
    bm = EXP_BM

    def weight_copies(e, s):
        return [pltpu.make_async_copy(w_hbm.at[layer, e], stage.at[s], sems.at[s, j])
                for j, (w_hbm, stage) in enumerate(((w1_hbm, w1s), (w3_hbm, w3s), (w2_hbm, w2s)))]

    @pl.when(i < nused_ref[0])
    def _():
        @pl.when(first_ref[i] == 1)
        def _():
            s = slot_ref[i]

            @pl.when(i == 0)
            def _():
                for cp in weight_copies(be_ref[0], 0):
                    cp.start()

            for cp in weight_copies(be_ref[i], s):
                cp.wait()

            @pl.when(next_ref[i] >= 0)
            def _():
                for cp in weight_copies(next_ref[i], 1 - s):
                    cp.start()

            w1b[...] = w1s[s].astype(BF16)
            w3b[...] = w3s[s].astype(BF16)
            w2b[...] = w2s[s].astype(BF16)

        live = lax.broadcasted_iota(I32, (bm, LANE), 0) < valid_ref[i]
        for s, piece in enumerate(_load_row_tiles(xs_ref, 0, bm)):
            lo, hi = _unpack_bf16_pairs(jnp.where(live, piece, jnp.uint32(0)))
            xb[:, s * LANE:(s + 1) * LANE] = lo.astype(BF16)
            xb[:, PK_W + s * LANE:PK_W + (s + 1) * LANE] = hi.astype(BF16)
        x = xb[...]
        h1 = jnp.dot(x, w1b[...], preferred_element_type=F32)
        h3 = jnp.dot(x, w3b[...], preferred_element_type=F32)
        a = (h1 * jax.nn.sigmoid(h1) * h3).astype(BF16)
        y = jnp.dot(a, w2b[...], preferred_element_type=F32)
        _store_row_tiles(y_ref, _pack_bf16_pairs(y.astype(BF16)))


def moe_experts(block_e, valid, n_used, first, next_e, slot, xs, w1, w3, w2, layer, n_blocks):
    bm = EXP_BM

    def blk(i, be, va, nu, fi, ne, sl):
        return (jnp.minimum(i, nu[0] - 1), 0)

    grid_spec = pltpu.PrefetchScalarGridSpec(
        num_scalar_prefetch=6,
        grid=(n_blocks,),
        in_specs=[pl.BlockSpec((bm * PK_S, LANE), blk),
                  pl.BlockSpec(memory_space=pl.ANY),
                  pl.BlockSpec(memory_space=pl.ANY),
                  pl.BlockSpec(memory_space=pl.ANY)],
        out_specs=pl.BlockSpec((bm * PK_S, LANE), blk),
        scratch_shapes=[pltpu.VMEM((2, D, EXPERT_DIM), F32), pltpu.VMEM((2, D, EXPERT_DIM), F32),
                        pltpu.VMEM((2, EXPERT_DIM, D), F32),
                        pltpu.VMEM((D, EXPERT_DIM), BF16), pltpu.VMEM((D, EXPERT_DIM), BF16),
                        pltpu.VMEM((EXPERT_DIM, D), BF16), pltpu.VMEM((bm, D), BF16),
                        pltpu.SemaphoreType.DMA((2, 3))],
    )
    return pl.pallas_call(
        functools.partial(_expert_kernel, layer=layer),
        grid_spec=grid_spec,
        out_shape=jax.ShapeDtypeStruct((n_blocks * bm * PK_S, LANE), U32),
        compiler_params=_cp(("arbitrary",), vmem=56 * 1024 * 1024),
        name="moe_experts",
    )(block_e, valid, n_used, first, next_e, slot, xs, w1, w3, w2)


def _shared_kernel(x_ref, w1_ref, w3_ref, w2_ref, o_ref):
    x = x_ref[...]
    h1 = jnp.dot(x, w1_ref[...], preferred_element_type=F32)
    h3 = jnp.dot(x, w3_ref[...], preferred_element_type=F32)
    a = (h1 * jax.nn.sigmoid(h1) * h3).astype(BF16)
    o_ref[...] = jnp.dot(a, w2_ref[...], preferred_element_type=F32)


def shared_expert(hx, w1, w3, w2, n_tok):
    tm = 512
    return pl.pallas_call(
        _shared_kernel,
        grid=(n_tok // tm,),
        in_specs=[pl.BlockSpec((tm, D), lambda i: (i, 0)),
                  pl.BlockSpec((D, EXPERT_DIM), lambda i: (0, 0)),
                  pl.BlockSpec((D, EXPERT_DIM), lambda i: (0, 0)),
                  pl.BlockSpec((EXPERT_DIM, D), lambda i: (0, 0))],
        out_specs=pl.BlockSpec((tm, D), lambda i: (i, 0)),
        out_shape=jax.ShapeDtypeStruct((n_tok, D), F32),
        compiler_params=_cp(("arbitrary",)),
        name="shared_expert",
    )(hx, w1, w3, w2)


COMBINE_TM = 128


def _combine_kernel(slot_ref, w_ref, sh_ref, x_ref, mod_ref, y_hbm, *rest, final):
    fg_ref = rest[0] if final else None
    o_ref, buf, sem = rest[-3:]
    tm = COMBINE_TM

    def issue(tt, carry):
        t8 = pl.multiple_of(tt * ROW_UNROLL, ROW_UNROLL)
        for j in range(ROW_UNROLL):
            for k in range(TOP_K):
                row = pl.multiple_of(slot_ref[k * tm + j + t8] * PK_S, PK_S)
                dst = buf.at[pl.ds(pl.multiple_of((k * tm + j + t8) * PK_S, PK_S), PK_S), :]
                pltpu.make_async_copy(y_hbm.at[pl.ds(row, PK_S), :], dst, sem).start(priority=k % 2)
        return carry

    lax.fori_loop(0, tm // ROW_UNROLL, issue, 0)
    pltpu.make_async_copy(y_hbm.at[pl.ds(0, TOP_K * tm * PK_S), :], buf, sem).wait()

    w = w_ref[...]
    gate = mod_ref[0][5:6]
    acc = [None] * (2 * PK_S)
    for k in range(TOP_K):
        wk = w[:, k:k + 1]
        for s, piece in enumerate(_load_row_tiles(buf, k * tm * PK_S, tm)):
            for c, val in zip((s, PK_S + s), _unpack_bf16_pairs(piece)):
                acc[c] = wk * val if acc[c] is None else acc[c] + wk * val
    outs = []
    for c in range(2 * PK_S):
        sl = slice(c * LANE, (c + 1) * LANE)
        outs.append(x_ref[:, sl] + gate[:, sl] * (sh_ref[:, sl] + acc[c]))
    if fg_ref is not None:
        ssq = functools.reduce(lambda a, b: a + b, [jnp.sum(o * o, axis=-1, keepdims=True) for o in outs])
        inv = lax.rsqrt(ssq / D + EPS)
        outs = [o * inv * fg_ref[:, c * LANE:(c + 1) * LANE] for c, o in enumerate(outs)]
    for c, o in enumerate(outs):
        o_ref[:, c * LANE:(c + 1) * LANE] = o


def moe_combine(slots, wts_tok, shared, x, mod, y, n_tok, final_gain=None):
    tm = COMBINE_TM
    final = final_gain is not None
    in_specs = [pl.BlockSpec((TOP_K * tm,), lambda i: (i,), memory_space=pltpu.SMEM),
                pl.BlockSpec((tm, TOP_K), lambda i: (i, 0)),
                pl.BlockSpec((tm, D), lambda i: (i, 0)),
                pl.BlockSpec((tm, D), lambda i: (i, 0)),
                pl.BlockSpec((1, 6, D), lambda i: (_mod_row(i * tm), 0, 0)),
                pl.BlockSpec(memory_space=pl.ANY)]
    args = [slots, wts_tok, shared, x, mod, y]
    if final:
        in_specs.append(pl.BlockSpec((1, D), lambda i: (0, 0)))
        args.append(final_gain.reshape(1, D))
    return pl.pallas_call(
        functools.partial(_combine_kernel, final=final),
        grid=(n_tok // tm,),
        in_specs=in_specs,
        out_specs=pl.BlockSpec((tm, D), lambda i: (i, 0)),
        out_shape=jax.ShapeDtypeStruct((n_tok, D), F32),
        scratch_shapes=[pltpu.VMEM((TOP_K * tm * PK_S, LANE), U32), pltpu.SemaphoreType.DMA(())],
        compiler_params=_cp(("arbitrary",)),
        name="moe_combine",
    )(*args)


def _lookup(table, idx):
    e = jnp.arange(table.shape[0], dtype=I32).reshape((-1,) + (1,) * idx.ndim)
    return jnp.sum(jnp.where(idx[None] == e, table.reshape(e.shape), 0), axis=0)


def _tile_flat(slots, tm):
    k, t = slots.shape
    return slots.reshape(k, t // tm, tm).transpose(1, 0, 2).reshape(-1)


def moe_layer(x, mod, norm_gain, router_w, router_b, exp_w1, exp_w3, exp_w2, sw1, sw3, sw2, layer, n_tok,
              final_gain=None):
    bm = EXP_BM
    n_blocks = -(-n_tok * TOP_K // bm) + N_EXPERTS
    hx, hx_packed = norm_mod(x, norm_gain, mod, 3, n_tok, pack=True)
    eidx, wts, pos, counts = moe_router(hx, router_w, router_b, n_tok)
    shared = shared_expert(hx, sw1.astype(BF16), sw3.astype(BF16), sw2.astype(BF16), n_tok)
    cnt = counts.astype(I32)
    padded = (cnt + bm - 1) // bm * bm
    pad_end = jnp.cumsum(padded)
    pad_start = pad_end - padded
    slots = _lookup(pad_start, eidx) + pos
    blk_row = jnp.arange(n_blocks, dtype=I32) * bm
    block_e = jnp.minimum(jnp.sum((pad_end[:, None] <= blk_row[None, :]).astype(I32), axis=0), N_EXPERTS - 1)
    valid = jnp.clip(_lookup(cnt, block_e) - (blk_row - _lookup(pad_start, block_e)), 0, bm).astype(I32)
    n_used = (pad_end[-1:] // bm).astype(I32)
    prev_e = jnp.concatenate([jnp.full((1,), -1, I32), block_e[:-1]])
    first = ((blk_row < pad_end[-1]) & (block_e != prev_e)).astype(I32)
    stage_slot = ((jnp.cumsum(first) - 1) % 2).astype(I32)
    eids = jnp.arange(N_EXPERTS, dtype=I32)
    later = jnp.where((eids[None, :] > eids[:, None]) & (padded[None, :] > 0), eids[None, :], N_EXPERTS)
    next_nonempty = jnp.min(later, axis=1)
    next_nonempty = jnp.where(next_nonempty == N_EXPERTS, -1, next_nonempty)
    next_e = _lookup(next_nonempty, block_e).astype(I32)
    xs = moe_dispatch(_tile_flat(slots, DISPATCH_TM), hx_packed, n_tok, n_blocks * bm)
    y = moe_experts(block_e, valid, n_used, first, next_e, stage_slot, xs, exp_w1, exp_w3, exp_w2, layer, n_blocks)
    return moe_combine(_tile_flat(slots, COMBINE_TM), wts.T, shared, x, mod, y, n_tok, final_gain)


def _rope_tables():
    t = jnp.arange(S, dtype=I32)
    row = (t // GRID_W).astype(F32)
    col = (t % GRID_W).astype(F32)
    n_freq = HD // 4
    inv_freq = ROPE_THETA ** (-jnp.arange(n_freq, dtype=F32) / n_freq)
    ang = jnp.concatenate([row[:, None] * inv_freq, col[:, None] * inv_freq], axis=-1)
    cosf = jnp.repeat(jnp.cos(ang), 2, axis=-1)
    sinf = jnp.stack([-jnp.sin(ang), jnp.sin(ang)], axis=-1).reshape(S, HD)
    return cosf, sinf


def kernel(x, c, ctx, c_ctx, ada_w, ada_b, norm_mix, norm_ffn, attn_w_in, attn_w_out, attn_rpb, attn_q_gain,
           attn_k_gain, ml_w_in, ml_w_out, ml_gate_b, ml_head_gain, router_w, router_b, exp_w1, exp_w3, exp_w2,
           sh_w1, sh_w3, sh_w2, final_norm_gain):
    depth = ada_w.shape[0]
    xa = jnp.concatenate([x.reshape(T_LAT, D), ctx.reshape(T_CTX, D)], axis=0)
    cvec = jnp.concatenate([c, c_ctx[None], jnp.zeros((8 - B - 1, D), F32)], axis=0)
    mod_all = ada_ln(cvec, ada_w, ada_b).reshape(depth, 8, 6, D)
    cosf, sinf = _rope_tables()

    mod = mod_all[0]
    p = norm_matmul(xa, norm_mix[0], mod, attn_w_in[0].astype(BF16), emit_h=False)
    o_all = neighborhood_attention(p, na_bias_table(attn_rpb[0]))
    o_all = gqa_attention(p, cosf, sinf, attn_q_gain[0], attn_k_gain[0], o_all)
    o_all = ctx_attention(p, attn_q_gain[0], attn_k_gain[0], o_all)
    xa = matmul_gated_residual(o_all, attn_w_out[0].astype(BF16), xa, mod, 2)
    xa = moe_layer(xa, mod, norm_ffn[0], router_w[0], router_b[0], exp_w1, exp_w3, exp_w2,
                   sh_w1[0], sh_w3[0], sh_w2[0], 0, T_ALL)

    mod = mod_all[1]
    w_in = ml_w_in[0]
    p, hx = norm_matmul(xa, norm_mix[1], mod, w_in[:, :ML_MAIN].astype(BF16), emit_h=True)
    col3, row4 = mlstm_gates(hx, w_in[:, ML_MAIN:], ml_gate_b[0])
    hdir = mlstm_scan(p, col3, row4)
    a = mlstm_readout(hdir, p, ml_head_gain[0])
    xl = matmul_gated_residual(a, ml_w_out[0].astype(BF16), xa, mod, 2)
    xl = moe_layer(xl, mod, norm_ffn[1], router_w[1], router_b[1], exp_w1, exp_w3, exp_w2,
                   sh_w1[1], sh_w3[1], sh_w2[1], 1, T_LAT, final_gain=final_norm_gain)
    return xl.reshape(B, S, D)
```

```python
import functools

import jax
import jax.numpy as jnp
from jax import lax
from jax.experimental import pallas as pl
from jax.experimental.pallas import tpu as pltpu

F32 = jnp.float32
BF16 = jnp.bfloat16
I32 = jnp.int32
U32 = jnp.uint32

D = 2048
B = 4
S = 4096
L = 256
T_LAT = B * S
T_CTX = B * L
T_ALL = T_LAT + T_CTX
GRID_W = 64
ROWS = S // GRID_W
HD = 128
NA_HEADS = 8
NA_WIN_ROWS = 8
NA_WIN_COLS = 16
GQA_Q_HEADS = 8
GQA_KV_HEADS = 2
GQA_GROUP = 4
ROPE_THETA = 10000.0
ATTN_IN = 4608
ML_HEADS = 8
ML_V = 256
ML_QK = 128
ML_MAIN = 6144
N_EXPERTS = 64
TOP_K = 8
N_GROUPS = 8
TOPK_GROUPS = 4
EXPERT_DIM = 512
ROUTED_SCALE = 2.5
EPS = 1e-6
NEG_INF = -1e30
ATT_SCALE = HD ** -0.5
LOG2E = 1.4426950408889634
ML_KSCALE = ML_QK ** -0.5

LANE = 128
NA_QROWS = 4
NA_SLAB = NA_QROWS + NA_WIN_ROWS - 1
NA_QB = NA_QROWS * GRID_W
NA_KB = NA_SLAB * GRID_W
ML_CH = 256
ML_HPS = 4
EXP_BM = 512
PK_W = D // 2
PK_S = PK_W // LANE
VMEM_LIMIT = 48 * 1024 * 1024


def _cp(sem, vmem=VMEM_LIMIT):
    return pltpu.CompilerParams(dimension_semantics=sem, vmem_limit_bytes=vmem)


def _pack_bf16_pairs(xb):
    u = pltpu.bitcast(xb.astype(F32), U32)
    return (u[:, PK_W:] & jnp.uint32(0xFFFF0000)) | (u[:, :PK_W] >> 16)


def _unpack_bf16_pairs(u):
    return pltpu.bitcast(u << 16, F32), pltpu.bitcast(u & jnp.uint32(0xFFFF0000), F32)


def _store_row_tiles(ref, words):
    rows = words.shape[0]
    for s in range(PK_S):
        ref[pl.ds(s, rows, stride=PK_S), :] = words[:, s * LANE:(s + 1) * LANE]


def _load_row_tiles(ref, start, rows):
    return [ref[pl.ds(start + s, rows, stride=PK_S), :] for s in range(PK_S)]


def _mod_row(start_row):
    return jnp.where(start_row < T_LAT, start_row // S, B)


def _ada_kernel(c_ref, w_ref, b_ref, o_ref):
    c = c_ref[...]
    a = (c * jax.nn.sigmoid(c)).astype(BF16)
    w = w_ref[0].astype(BF16)
    o_ref[0] = jnp.dot(a, w, preferred_element_type=F32) + b_ref[0]


def ada_ln(cvec, ada_w, ada_b):
    depth = ada_w.shape[0]
    n = ada_w.shape[2]
    tn = 1024
    return pl.pallas_call(
        _ada_kernel,
        grid=(depth, n // tn),
        in_specs=[pl.BlockSpec((8, D), lambda l, j: (0, 0)),
                  pl.BlockSpec((1, D, tn), lambda l, j: (l, 0, j)),
                  pl.BlockSpec((1, 1, tn), lambda l, j: (l, 0, j))],
        out_specs=pl.BlockSpec((1, 8, tn), lambda l, j: (l, 0, j)),
        out_shape=jax.ShapeDtypeStruct((depth, 8, n), F32),
        compiler_params=_cp(("arbitrary", "arbitrary")),
        name="ada_ln",
    )(cvec, ada_w, ada_b.reshape(depth, 1, n))


def _norm_mod_kernel(x_ref, g_ref, mod_ref, *out_refs, base, pack):
    x = x_ref[...]
    y = x * lax.rsqrt(jnp.mean(x * x, axis=-1, keepdims=True) + EPS) * g_ref[...]
    m = mod_ref[0]
    h = y * (1.0 + m[base + 1:base + 2]) + m[base:base + 1]
    hb = h.astype(BF16)
    out_refs[0][...] = hb
    if pack:
        _store_row_tiles(out_refs[1], _pack_bf16_pairs(hb))


def norm_mod(x, gain, mod, base, n_rows, pack):
    tm = 256
    out_shape = [jax.ShapeDtypeStruct((n_rows, D), BF16)]
    out_specs = [pl.BlockSpec((tm, D), lambda i: (i, 0))]
    if pack:
        out_shape.append(jax.ShapeDtypeStruct((n_rows * PK_S, LANE), U32))
        out_specs.append(pl.BlockSpec((tm * PK_S, LANE), lambda i: (i, 0)))
    res = pl.pallas_call(
        functools.partial(_norm_mod_kernel, base=base, pack=pack),
        grid=(n_rows // tm,),
        in_specs=[pl.BlockSpec((tm, D), lambda i: (i, 0)),
                  pl.BlockSpec((1, D), lambda i: (0, 0)),
                  pl.BlockSpec((1, 6, D), lambda i: (_mod_row(i * tm), 0, 0))],
        out_specs=out_specs,
        out_shape=out_shape,
        compiler_params=_cp(("arbitrary",)),
        name="norm_mod",
    )(x, gain.reshape(1, D), mod)
    return res if pack else res[0]


def _norm_mm_kernel(*refs, emit_h, n_lat):
    if n_lat is None:
        x_ref, g_ref, mod_ref, w_ref, o_ref = refs[:5]
        c_ref = None
    else:
        x_ref, c_ref, g_ref, mod_ref, w_ref, o_ref = refs[:6]
    hb_ref = refs[-1]

    def prologue(src_ref):
        x = src_ref[...]
        y = x * lax.rsqrt(jnp.mean(x * x, axis=-1, keepdims=True) + EPS) * g_ref[...]
        m = mod_ref[0]
        hb_ref[...] = (y * (1.0 + m[1:2]) + m[0:1]).astype(BF16)
        if emit_h:
            refs[-2][...] = hb_ref[...]

    first_col = pl.program_id(1) == 0
    if c_ref is None:
        pl.when(first_col)(lambda: prologue(x_ref))
    else:
        is_lat = pl.program_id(0) < n_lat
        pl.when(first_col & is_lat)(lambda: prologue(x_ref))
        pl.when(first_col & jnp.logical_not(is_lat))(lambda: prologue(c_ref))

    o_ref[...] = jnp.dot(hb_ref[...], w_ref[...], preferred_element_type=F32).astype(o_ref.dtype)


def norm_matmul(x, gain, mod, w, emit_h, x_ctx=None, tm=1024, tn=512):
    split = x_ctx is not None
    m = x.shape[0] + (x_ctx.shape[0] if split else 0)
    n = w.shape[1]
    n_lat = x.shape[0] // tm if split else None
    out_shape = [jax.ShapeDtypeStruct((m, n), BF16)]
    out_specs = [pl.BlockSpec((tm, tn), lambda i, j: (i, j))]
    if emit_h:
        out_shape.append(jax.ShapeDtypeStruct((m, D), BF16))
        out_specs.append(pl.BlockSpec((tm, D), lambda i, j: (i, 0)))
    if split:
        x_specs = [pl.BlockSpec((tm, D), lambda i, j: (jnp.minimum(i, n_lat - 1), 0)),
                   pl.BlockSpec((tm, D), lambda i, j: (0, 0))]
        x_args = [x, x_ctx]
    else:
        x_specs = [pl.BlockSpec((tm, D), lambda i, j: (i, 0))]
        x_args = [x]
    res = pl.pallas_call(
        functools.partial(_norm_mm_kernel, emit_h=emit_h, n_lat=n_lat),
        grid=(m // tm, n // tn),
        in_specs=x_specs + [pl.BlockSpec((1, D), lambda i, j: (0, 0)),
                            pl.BlockSpec((1, 6, D), lambda i, j: (_mod_row(i * tm), 0, 0)),
                            pl.BlockSpec((D, tn), lambda i, j: (0, j))],
        out_specs=out_specs,
        out_shape=out_shape,
        scratch_shapes=[pltpu.VMEM((tm, D), BF16)],
        compiler_params=_cp(("arbitrary", "arbitrary"), vmem=56 * 1024 * 1024),
        name="norm_matmul",
    )(*x_args, gain.reshape(1, D), mod, w)
    return res if emit_h else res[0]


def _mm_res_kernel(*refs, slot, n_lat):
    if n_lat is None:
        a_ref, w_ref, x_ref, mod_ref, o_ref = refs
        res = x_ref[...]
    else:
        a_ref, w_ref, x_ref, c_ref, mod_ref, o_ref = refs
        res = jnp.where(pl.program_id(0) < n_lat, x_ref[...], c_ref[...])
    acc = jnp.dot(a_ref[...], w_ref[...], preferred_element_type=F32)
    o_ref[...] = res + mod_ref[0][slot:slot + 1] * acc


def matmul_gated_residual(a, w, x, mod, slot, x_ctx=None, tm=1024, tn=512):
    m, k = a.shape
    n = w.shape[1]
    split = x_ctx is not None
    n_lat = x.shape[0] // tm if split else None
    if split:
        x_specs = [pl.BlockSpec((tm, tn), lambda i, j: (jnp.minimum(i, n_lat - 1), j)),
                   pl.BlockSpec((tm, tn), lambda i, j: (0, j))]
        x_args = [x, x_ctx]
    else:
        x_specs = [pl.BlockSpec((tm, tn), lambda i, j: (i, j))]
        x_args = [x]
    return pl.pallas_call(
        functools.partial(_mm_res_kernel, slot=slot, n_lat=n_lat),
        grid=(m // tm, n // tn),
        in_specs=[pl.BlockSpec((tm, k), lambda i, j: (i, 0)),
                  pl.BlockSpec((k, tn), lambda i, j: (0, j))] + x_specs
                 + [pl.BlockSpec((1, 6, tn), lambda i, j: (_mod_row(i * tm), 0, j))],
        out_specs=pl.BlockSpec((tm, tn), lambda i, j: (i, j)),
        out_shape=jax.ShapeDtypeStruct((m, n), F32),
        compiler_params=_cp(("arbitrary", "arbitrary")),
        name="matmul_gated_residual",
    )(a, w, *x_args, mod)


def _dot_nt(a, b):
    return lax.dot_general(a, b, (((1,), (1,)), ((), ())), preferred_element_type=F32)


def _rms_head(x, gain):
    return x * lax.rsqrt(jnp.mean(x * x, axis=-1, keepdims=True) + EPS) * gain


def _rope(x, cosf, sinf):
    lane = lax.broadcasted_iota(I32, x.shape, 1)
    nxt = pltpu.roll(x, LANE - 1, 1)
    prv = pltpu.roll(x, 1, 1)
    return x * cosf + jnp.where((lane & 1) == 0, nxt, prv) * sinf


def _softmax_av(parts):
    m = functools.reduce(jnp.maximum, [jnp.max(s, axis=-1, keepdims=True) for s, _ in parts])
    l = None
    o = None
    for s, v in parts:
        p = jnp.exp(s - m)
        li = jnp.sum(p, axis=-1, keepdims=True)
        oi = jnp.dot(p.astype(BF16), v, preferred_element_type=F32)
        l = li if l is None else l + li
        o = oi if o is None else o + oi
    return o / l


def _na_kernel(q_ref, k_ref, v_ref, kc_ref, vc_ref, tab_ref, o_ref):
    kc = kc_ref[...]
    vc = vc_ref[...]
    n_blocks = ROWS // NA_QROWS

    def body(j, carry):
        ks = jnp.clip(j * NA_QROWS - NA_WIN_ROWS // 2, 0, ROWS - NA_SLAB)
        typ = jnp.where(j == 0, 0, jnp.where(j == n_blocks - 1, 2, 1))
        qs = pl.multiple_of(j * NA_QB, NA_QB)
        kst = pl.multiple_of(ks * GRID_W, GRID_W)
        q = q_ref[pl.ds(qs, NA_QB), :]
        k = k_ref[pl.ds(kst, NA_KB), :]
        v = v_ref[pl.ds(kst, NA_KB), :]
        s_win = _dot_nt(q, k) * ATT_SCALE + tab_ref[typ, 0]
        s_ctx = _dot_nt(q, kc) * ATT_SCALE
        o_ref[pl.ds(qs, NA_QB), :] = _softmax_av([(s_win, v), (s_ctx, vc)]).astype(BF16)
        return carry

    lax.fori_loop(0, n_blocks, body, 0)


def na_bias_table(rpb):
    def one(r0, ks):
        r = r0 + jnp.arange(NA_QROWS)
        kr = ks + jnp.arange(NA_SLAB)
        start = jnp.clip(r - NA_WIN_ROWS // 2, 0, ROWS - NA_WIN_ROWS)
        row_ok = (kr[None, :] >= start[:, None]) & (kr[None, :] < start[:, None] + NA_WIN_ROWS)
        row_idx = jnp.clip(kr[None, :] - r[:, None] + NA_WIN_ROWS - 1, 0, 2 * NA_WIN_ROWS - 2)
        cq = jnp.arange(GRID_W)
        col_start = jnp.clip(cq - NA_WIN_COLS // 2, 0, GRID_W - NA_WIN_COLS)
        col_ok = (cq[None, :] >= col_start[:, None]) & (cq[None, :] < col_start[:, None] + NA_WIN_COLS)
        col_idx = jnp.clip(cq[None, :] - cq[:, None] + NA_WIN_COLS - 1, 0, 2 * NA_WIN_COLS - 2)
        r_hot = jax.nn.one_hot(row_idx, 2 * NA_WIN_ROWS - 1, dtype=F32)
        c_hot = jax.nn.one_hot(col_idx, 2 * NA_WIN_COLS - 1, dtype=F32)
        bias = jnp.einsum('qka,hab,xyb->hqxky', r_hot, rpb.astype(F32), c_hot, precision=lax.Precision.HIGHEST)
        ok = row_ok[:, None, :, None] & col_ok[None, :, None, :]
        return jnp.where(ok[None], bias, NEG_INF).reshape(NA_HEADS, NA_QB, NA_KB)

    mid = 2 * NA_QROWS
    last = ROWS - NA_QROWS
    return jnp.stack([one(0, 0), one(mid, mid - NA_WIN_ROWS // 2), one(last, ROWS - NA_SLAB)])


def neighborhood_attention(p, table):
    cb = S // L
    return pl.pallas_call(
        _na_kernel,
        grid=(NA_HEADS, B),
        in_specs=[pl.BlockSpec((S, HD), lambda h, b: (b, h)),
                  pl.BlockSpec((S, HD), lambda h, b: (b, NA_HEADS + h)),
                  pl.BlockSpec((S, HD), lambda h, b: (b, 2 * NA_HEADS + h)),
                  pl.BlockSpec((L, HD), lambda h, b: (B * cb + b, NA_HEADS + h)),
                  pl.BlockSpec((L, HD), lambda h, b: (B * cb + b, 2 * NA_HEADS + h)),
                  pl.BlockSpec((3, 1, NA_QB, NA_KB), lambda h, b: (0, h, 0, 0))],
        out_specs=pl.BlockSpec((S, HD), lambda h, b: (b, h)),
        out_shape=jax.ShapeDtypeStruct((T_ALL, D), BF16),
        compiler_params=_cp(("arbitrary", "arbitrary")),
        name="neighborhood_attention",
    )(p, p, p, p, p, table)


GQA_TQ = 256
GQA_CK = 512
GQA_QCOL = 3 * NA_HEADS
GQA_KCOL = GQA_QCOL + GQA_Q_HEADS
GQA_VCOL = GQA_KCOL + GQA_KV_HEADS


def _gqa_kernel(q_ref, k_ref, v_ref, kc_ref, vc_ref, cq_ref, sq_ref, ck_ref, sk_ref, qg_ref, kg_ref, o_prev,
                o_ref, kn_ref, kcn_ref):
    del o_prev
    @pl.when(pl.program_id(2) == 0)
    def _():
        kn = _rope(_rms_head(k_ref[...].astype(F32), kg_ref[...]), ck_ref[...], sk_ref[...])
        kn_ref[...] = kn.astype(BF16)
        kcn_ref[...] = _rms_head(kc_ref[...].astype(F32), kg_ref[...]).astype(BF16)

    cos = cq_ref[...]
    sin = sq_ref[...]
    heads = []
    for g in range(GQA_GROUP):
        qh = _rope(_rms_head(q_ref[:, g * HD:(g + 1) * HD].astype(F32), qg_ref[...]), cos, sin)
        heads.append((qh * (ATT_SCALE * LOG2E)).astype(BF16))
    q = jnp.concatenate(heads, axis=0)
    chunks = [(kn_ref[c * GQA_CK:(c + 1) * GQA_CK, :], v_ref[c * GQA_CK:(c + 1) * GQA_CK, :])
              for c in range(S // GQA_CK)]
    chunks.append((kcn_ref[...], vc_ref[...]))
    m = l = acc = None
    for kk, vv in chunks:
        s = _dot_nt(q, kk)
        mc = jnp.max(s, axis=-1, keepdims=True)
        if m is None:
            m_new = mc
            p = jnp.exp2(s - m_new)
            l = jnp.sum(p, axis=-1, keepdims=True)
            acc = jnp.dot(p.astype(BF16), vv, preferred_element_type=F32)
        else:
            m_new = jnp.maximum(m, mc)
            alpha = jnp.exp2(m - m_new)
            p = jnp.exp2(s - m_new)
            l = alpha * l + jnp.sum(p, axis=-1, keepdims=True)
            acc = alpha * acc + jnp.dot(p.astype(BF16), vv, preferred_element_type=F32)
        m = m_new
    o = acc / l
    for g in range(GQA_GROUP):
        o_ref[:, g * HD:(g + 1) * HD] = o[g * GQA_TQ:(g + 1) * GQA_TQ].astype(BF16)


def gqa_attention(p, cosf, sinf, q_gain, k_gain, o_buf):
    nq = S // GQA_TQ
    cb = S // L
    gw = GQA_GROUP * HD
    return pl.pallas_call(
        _gqa_kernel,
        grid=(B, GQA_KV_HEADS, nq),
        in_specs=[pl.BlockSpec((GQA_TQ, gw), lambda b, n, i: (b * nq + i, GQA_QCOL // GQA_GROUP + n)),
                  pl.BlockSpec((S, HD), lambda b, n, i: (b, GQA_KCOL + n)),
                  pl.BlockSpec((S, HD), lambda b, n, i: (b, GQA_VCOL + n)),
                  pl.BlockSpec((L, HD), lambda b, n, i: (B * cb + b, GQA_KCOL + n)),
                  pl.BlockSpec((L, HD), lambda b, n, i: (B * cb + b, GQA_VCOL + n)),
                  pl.BlockSpec((GQA_TQ, HD), lambda b, n, i: (i, 0)),
                  pl.BlockSpec((GQA_TQ, HD), lambda b, n, i: (i, 0)),
                  pl.BlockSpec((S, HD), lambda b, n, i: (0, 0)),
                  pl.BlockSpec((S, HD), lambda b, n, i: (0, 0)),
                  pl.BlockSpec((1, HD), lambda b, n, i: (0, 0)),
                  pl.BlockSpec((1, HD), lambda b, n, i: (0, 0)),
                  pl.BlockSpec(memory_space=pl.ANY)],
        out_specs=pl.BlockSpec((GQA_TQ, gw), lambda b, n, i: (b * nq + i, (NA_HEADS * HD) // gw + n)),
        out_shape=jax.ShapeDtypeStruct((T_ALL, D), BF16),
        input_output_aliases={11: 0},
        scratch_shapes=[pltpu.VMEM((S, HD), BF16), pltpu.VMEM((L, HD), BF16)],
        compiler_params=_cp(("arbitrary",) * 3),
        name="gqa_attention",
    )(p, p, p, p, p, cosf, sinf, cosf, sinf, q_gain.reshape(1, HD), k_gain.reshape(1, HD), o_buf)


def _ctx_attn_kernel(p_ref, qg_ref, kg_ref, o_prev, o_ref):
    del o_prev

    def col(c):
        return p_ref[:, c * HD:(c + 1) * HD]

    for h in range(NA_HEADS):
        s = _dot_nt(col(h), col(NA_HEADS + h)) * ATT_SCALE
        o_ref[:, h * HD:(h + 1) * HD] = _softmax_av([(s, col(2 * NA_HEADS + h))]).astype(BF16)
    for n in range(GQA_KV_HEADS):
        kn = _rms_head(col(GQA_KCOL + n).astype(F32), kg_ref[...]).astype(BF16)
        v = col(GQA_VCOL + n)
        for g in range(GQA_GROUP):
            h = n * GQA_GROUP + g
            qn = _rms_head(col(GQA_QCOL + h).astype(F32), qg_ref[...]).astype(BF16)
            s = _dot_nt(qn, kn) * ATT_SCALE
            o_ref[:, (NA_HEADS + h) * HD:(NA_HEADS + h + 1) * HD] = _softmax_av([(s, v)]).astype(BF16)


def ctx_attention(p, q_gain, k_gain, o_buf):
    cb = S // L
    return pl.pallas_call(
        _ctx_attn_kernel,
        grid=(B,),
        in_specs=[pl.BlockSpec((L, ATTN_IN), lambda b: (B * cb + b, 0)),
                  pl.BlockSpec((1, HD), lambda b: (0, 0)),
                  pl.BlockSpec((1, HD), lambda b: (0, 0)),
                  pl.BlockSpec(memory_space=pl.ANY)],
        out_specs=pl.BlockSpec((L, D), lambda b: (B * cb + b, 0)),
        out_shape=jax.ShapeDtypeStruct((T_ALL, D), BF16),
        input_output_aliases={3: 0},
        compiler_params=_cp(("arbitrary",)),
        name="ctx_attention",
    )(p, q_gain.reshape(1, HD), k_gain.reshape(1, HD), o_buf)


def _log_sigmoid(x):
    return -(jnp.maximum(-x, 0.0) + jnp.log1p(jnp.exp(-jnp.abs(x))))


def _dot_hi(a, b):
    return jnp.dot(a, b, precision=lax.Precision.HIGHEST, preferred_element_type=F32)


def _gate_kernel(h_ref, wg_ref, wgt_ref, b_ref, bt_ref, lt_ref, ut_ref, col_ref, row_ref):
    nh = ML_HEADS
    hx = h_ref[...]
    g = jnp.dot(hx, wg_ref[...], preferred_element_type=F32) + b_ref[...]
    gt = _dot_nt(wgt_ref[...], hx) + bt_ref[...]
    li = g[:, 0:2 * nh]
    lf = _log_sigmoid(g[:, 2 * nh:4 * nh])
    lit = gt[0:2 * nh]
    lft = _log_sigmoid(gt[2 * nh:4 * nh])
    lt = lt_ref[...]
    ut = ut_ref[...]
    lane = lax.broadcasted_iota(I32, lf.shape, 1)
    bc = jnp.where(lane < nh, _dot_hi(lt, lf), _dot_hi(ut, lf))
    tot = jnp.sum(lf, axis=0, keepdims=True)
    aend = tot - bc + li
    col_ref[...] = jnp.concatenate([bc, aend, jnp.zeros((ML_CH, LANE - 4 * nh), F32)], axis=1)
    sub = lax.broadcasted_iota(I32, lft.shape, 0)
    bct = jnp.where(sub < nh, _dot_hi(lft, ut), _dot_hi(lft, lt))
    tott = jnp.sum(lft, axis=1, keepdims=True)
    gtr = lit - bct
    row_ref[0] = jnp.concatenate([bct, gtr, tott + gtr, jnp.broadcast_to(tott, bct.shape)], axis=0)


def mlstm_gates(hx, wg, gate_b):
    nh = ML_HEADS
    n_ch = T_ALL // ML_CH
    wg_pad = jnp.zeros((D, LANE), BF16).at[:, :4 * nh].set(wg.astype(BF16))
    b_pad = jnp.zeros((1, LANE), F32).at[0, :4 * nh].set(gate_b.reshape(-1))
    wgt = wg.astype(BF16).T
    bt = gate_b.reshape(4 * nh, 1).astype(F32)
    lt = jnp.tril(jnp.ones((ML_CH, ML_CH), F32))
    ut = jnp.triu(jnp.ones((ML_CH, ML_CH), F32))
    col, row = pl.pallas_call(
        _gate_kernel,
        grid=(n_ch,),
        in_specs=[pl.BlockSpec((ML_CH, D), lambda i: (i, 0)),
                  pl.BlockSpec((D, LANE), lambda i: (0, 0)),
                  pl.BlockSpec((4 * nh, D), lambda i: (0, 0)),
                  pl.BlockSpec((1, LANE), lambda i: (0, 0)),
                  pl.BlockSpec((4 * nh, 1), lambda i: (0, 0)),
                  pl.BlockSpec((ML_CH, ML_CH), lambda i: (0, 0)),
                  pl.BlockSpec((ML_CH, ML_CH), lambda i: (0, 0))],
        out_specs=[pl.BlockSpec((ML_CH, LANE), lambda i: (i, 0)),
                   pl.BlockSpec((1, 8 * nh, ML_CH), lambda i: (i, 0, 0))],
        out_shape=[jax.ShapeDtypeStruct((T_ALL, LANE), F32),
                   jax.ShapeDtypeStruct((n_ch, 8 * nh, ML_CH), F32)],
        compiler_params=_cp(("arbitrary",)),
        name="mlstm_gates",
    )(hx, wg_pad, wgt, b_pad, bt, lt, ut)
    col3 = col[:, :4 * nh].reshape(T_ALL, 2, 2 * nh).transpose(2, 0, 1)
    col3 = jnp.pad(col3, ((0, 0), (0, 0), (0, 6)))
    row4 = row.reshape(n_ch, 4, 2 * nh, ML_CH).transpose(2, 0, 1, 3)
    row4 = jnp.pad(row4, ((0, 0), (0, 0), (0, 4), (0, 0)))
    return col3, row4


def _mlstm_step(d, hh, q_ref, k_ref, v_ref, col_ref, row_ref, o_ref, c_ref, n_ref, m_ref):
    sl = d * ML_HPS + hh
    q = q_ref[:, hh * ML_QK:(hh + 1) * ML_QK]
    kf = k_ref[:, hh * ML_QK:(hh + 1) * ML_QK].astype(F32) * ML_KSCALE
    kb = kf.astype(BF16)
    v = v_ref[:, hh * ML_V:(hh + 1) * ML_V]
    col = col_ref[hh]
    row = row_ref[hh, 0]
    bc_col = col[:, 0:1]
    aend_col = col[:, 1:2]
    g_row = row[1:2]
    aend_row = row[2:3]
    btot = row[3:4, 0:1]
    m_st = m_ref[sl]
    c_st = c_ref[sl]
    n_st = n_ref[sl]
    m_new = jnp.maximum(btot + m_st, jnp.max(aend_row, axis=1, keepdims=True))

    r = lax.broadcasted_iota(I32, (ML_CH, ML_CH), 0)
    c = lax.broadcasted_iota(I32, (ML_CH, ML_CH), 1)
    causal = (r >= c) if d == 0 else (r <= c)
    d_mat = jnp.where(causal, bc_col + g_row, -jnp.inf)
    m_row = jnp.maximum(bc_col + m_st, jnp.max(d_mat, axis=1, keepdims=True))
    w_inter = jnp.exp(bc_col + m_st - m_row)
    s_mat = _dot_nt(q, kb) * jnp.exp(d_mat - m_row)
    num = (w_inter * jnp.dot(q, c_st.astype(BF16), preferred_element_type=F32)
           + jnp.dot(s_mat.astype(BF16), v, preferred_element_type=F32))
    den = (w_inter * jnp.sum(q.astype(F32) * n_st, axis=1, keepdims=True)
           + jnp.sum(s_mat, axis=1, keepdims=True))
    o_ref[:, hh * ML_V:(hh + 1) * ML_V] = num / jnp.maximum(jnp.abs(den), jnp.exp(-m_row))

    w_end_col = jnp.exp(aend_col - m_new)
    w_end_row = jnp.exp(aend_row - m_new)
    decay = jnp.exp(btot + m_st - m_new)
    kw = (kf * w_end_col).astype(BF16)
    c_ref[sl] = decay * c_st + lax.dot_general(kw, v, (((0,), (0,)), ((), ())), preferred_element_type=F32)
    w8 = jnp.broadcast_to(w_end_row, (8, ML_CH)).astype(BF16)
    n_ref[sl] = decay * n_st + jnp.dot(w8, kb, preferred_element_type=F32)[0:1]
    m_ref[sl] = m_new


def _mlstm_kernel(qf, kf, vf, colf, rowf, qb, kb, vb, colb, rowb, of, ob, c_ref, n_ref, m_ref):
    @pl.when(pl.program_id(2) == 0)
    def _():
        c_ref[...] = jnp.zeros_like(c_ref)
        n_ref[...] = jnp.zeros_like(n_ref)
        m_ref[...] = jnp.zeros_like(m_ref)

    for hh in range(ML_HPS):
        _mlstm_step(0, hh, qf, kf, vf, colf, rowf, of, c_ref, n_ref, m_ref)
        _mlstm_step(1, hh, qb, kb, vb, colb, rowb, ob, c_ref, n_ref, m_ref)


def mlstm_scan(p, col3, row4):
    n_lat = S // ML_CH
    steps = n_lat + 1
    lat_blocks = T_LAT // ML_CH
    hps = ML_HPS
    n_groups = ML_HEADS // hps
    kcol = (ML_HEADS * ML_QK) // (hps * ML_QK)
    vcol = (2 * ML_HEADS * ML_QK) // (hps * ML_V)

    def chunk(b, d, st):
        c = (st - 1) if d == 0 else (n_lat - st)
        return jnp.where(st == 0, lat_blocks + b, b * n_lat + c)

    def out_chunk(b, d, st):
        s1 = jnp.maximum(st, 1)
        return b * n_lat + ((s1 - 1) if d == 0 else (n_lat - s1))

    def dir_specs(d):
        return [pl.BlockSpec((ML_CH, hps * ML_QK), lambda b, h, s: (chunk(b, d, s), h)),
                pl.BlockSpec((ML_CH, hps * ML_QK), lambda b, h, s: (chunk(b, d, s), kcol + h)),
                pl.BlockSpec((ML_CH, hps * ML_V), lambda b, h, s: (chunk(b, d, s), vcol + h)),
                pl.BlockSpec((hps, ML_CH, 8), lambda b, h, s: (d * n_groups + h, chunk(b, d, s), 0)),
                pl.BlockSpec((hps, 1, 8, ML_CH), lambda b, h, s: (d * n_groups + h, chunk(b, d, s), 0, 0))]

    return pl.pallas_call(
        _mlstm_kernel,
        grid=(B, n_groups, steps),
        in_specs=dir_specs(0) + dir_specs(1),
        out_specs=[pl.BlockSpec((ML_CH, hps * ML_V), lambda b, h, s: (out_chunk(b, 0, s), h)),
                   pl.BlockSpec((ML_CH, hps * ML_V), lambda b, h, s: (out_chunk(b, 1, s), h))],
        out_shape=[jax.ShapeDtypeStruct((T_LAT, ML_HEADS * ML_V), F32),
                   jax.ShapeDtypeStruct((T_LAT, ML_HEADS * ML_V), F32)],
        scratch_shapes=[pltpu.VMEM((2 * hps, ML_QK, ML_V), F32), pltpu.VMEM((2 * hps, 1, ML_QK), F32),
                        pltpu.VMEM((2 * hps, 1, 1), F32)],
        compiler_params=_cp(("arbitrary",) * 3),
        name="mlstm_scan",
    )(p, p, p, col3, row4, p, p, p, col3, row4)


def _readout_kernel(hf_ref, hb_ref, o_ref, g_ref, a_ref):
    hs = hf_ref[...] + hb_ref[...]
    for h in range(ML_HEADS):
        sl = slice(h * ML_V, (h + 1) * ML_V)
        x = hs[:, sl]
        hn = x * lax.rsqrt(jnp.mean(x * x, axis=-1, keepdims=True) + EPS) * g_ref[:, sl]
        a_ref[:, sl] = (hn * jax.nn.sigmoid(o_ref[:, sl].astype(F32))).astype(BF16)


def mlstm_readout(hdir, p, head_gain):
    tm = 256
    ocol = (2 * ML_HEADS * ML_QK + ML_HEADS * ML_V) // D
    return pl.pallas_call(
        _readout_kernel,
        grid=(T_LAT // tm,),
        in_specs=[pl.BlockSpec((tm, D), lambda i: (i, 0)),
                  pl.BlockSpec((tm, D), lambda i: (i, 0)),
                  pl.BlockSpec((tm, D), lambda i: (i, ocol)),
                  pl.BlockSpec((1, D), lambda i: (0, 0))],
        out_specs=pl.BlockSpec((tm, D), lambda i: (i, 0)),
        out_shape=jax.ShapeDtypeStruct((T_LAT, D), BF16),
        compiler_params=_cp(("arbitrary",)),
        name="mlstm_readout",
    )(hdir[0], hdir[1], p, head_gain.reshape(1, D))


ROUTER_TM = 512


def _router_kernel(h_ref, w_ref, rb_ref, erow_ref, tri_ref, eidx_ref, wts_ref, pos_ref, cnt_ref, carry_ref):
    ng = N_GROUPS
    epg = N_EXPERTS // N_GROUPS
    tm = ROUTER_TM
    ninf = -jnp.inf

    @pl.when(pl.program_id(0) == 0)
    def _():
        carry_ref[...] = jnp.zeros_like(carry_ref)

    s = jax.nn.sigmoid(_dot_nt(w_ref[...], h_ref[...]))
    ssel = s + rb_ref[...]
    sraw = [s[ng * j:ng * (j + 1)] for j in range(epg)]
    slab = [ssel[ng * j:ng * (j + 1)] for j in range(epg)]
    m1 = functools.reduce(jnp.maximum, slab)
    jfirst = functools.reduce(jnp.minimum, [jnp.where(slab[j] == m1, j, epg) for j in range(epg)])
    m2 = functools.reduce(jnp.maximum, [jnp.where(jfirst == j, ninf, slab[j]) for j in range(epg)])
    gs = m1 + m2
    giota = lax.broadcasted_iota(I32, (ng, tm), 0)
    gsel = jnp.zeros((ng, tm), F32)
    for _ in range(TOPK_GROUPS):
        mx = jnp.max(gs, axis=0, keepdims=True)
        gi = jnp.min(jnp.where(gs == mx, giota, ng), axis=0, keepdims=True)
        hit = giota == gi
        gsel = jnp.where(hit, 1.0, gsel)
        gs = jnp.where(hit, ninf, gs)
    msl = [jnp.where(gsel > 0.0, slab[j], ninf) for j in range(epg)]
    eid = [giota * epg + j for j in range(epg)]
    selm = [jnp.zeros((ng, tm), F32) for _ in range(epg)]
    e_list, w_list = [], []
    for _ in range(TOP_K):
        mx = jnp.max(functools.reduce(jnp.maximum, msl), axis=0, keepdims=True)
        cand = functools.reduce(jnp.minimum, [jnp.where(msl[j] == mx, eid[j], N_EXPERTS) for j in range(epg)])
        esel = jnp.min(cand, axis=0, keepdims=True)
        hits = [eid[j] == esel for j in range(epg)]
        wk = functools.reduce(lambda a, b: a + b, [jnp.where(hits[j], sraw[j], 0.0) for j in range(epg)])
        w_list.append(jnp.sum(wk, axis=0, keepdims=True))
        e_list.append(esel)
        msl = [jnp.where(hits[j], ninf, msl[j]) for j in range(epg)]
        selm = [jnp.where(hits[j], 1.0, selm[j]) for j in range(epg)]
    wsum = functools.reduce(lambda a, b: a + b, w_list)
    wts_ref[...] = jnp.concatenate([w / wsum * ROUTED_SCALE for w in w_list], axis=0)
    eidx_ref[...] = jnp.concatenate(e_list, axis=0)
    sel = jnp.concatenate(selm, axis=0)
    carry = carry_ref[...]
    posfull = jnp.dot(sel.astype(BF16), tri_ref[...], preferred_element_type=F32) + carry
    erow = erow_ref[...]
    pos = [jnp.sum(jnp.where(erow == e, posfull, 0.0), axis=0, keepdims=True) for e in e_list]
    pos_ref[...] = jnp.concatenate(pos, axis=0).astype(I32)
    carry = carry + jnp.sum(sel, axis=1, keepdims=True)
    carry_ref[...] = carry
    cnt_ref[...] = carry


def moe_router(hx, router_w, router_b, n_tok):
    tm = ROUTER_TM
    epg = N_EXPERTS // N_GROUPS
    perm = (jnp.arange(N_EXPERTS) % N_GROUPS) * epg + jnp.arange(N_EXPERTS) // N_GROUPS
    w_t = router_w.astype(BF16).T[perm]
    rb = router_b.astype(F32)[perm].reshape(N_EXPERTS, 1)
    erow = perm.astype(I32).reshape(N_EXPERTS, 1)
    tri = jnp.triu(jnp.ones((tm, tm), BF16), 1)
    eidx, wts, pos, counts = pl.pallas_call(
        _router_kernel,
        grid=(n_tok // tm,),
        in_specs=[pl.BlockSpec((tm, D), lambda i: (i, 0)),
                  pl.BlockSpec((N_EXPERTS, D), lambda i: (0, 0)),
                  pl.BlockSpec((N_EXPERTS, 1), lambda i: (0, 0)),
                  pl.BlockSpec((N_EXPERTS, 1), lambda i: (0, 0)),
                  pl.BlockSpec((tm, tm), lambda i: (0, 0))],
        out_specs=[pl.BlockSpec((TOP_K, tm), lambda i: (0, i)),
                   pl.BlockSpec((TOP_K, tm), lambda i: (0, i)),
                   pl.BlockSpec((TOP_K, tm), lambda i: (0, i)),
                   pl.BlockSpec((N_EXPERTS, 1), lambda i: (0, 0))],
        out_shape=[jax.ShapeDtypeStruct((TOP_K, n_tok), I32),
                   jax.ShapeDtypeStruct((TOP_K, n_tok), F32),
                   jax.ShapeDtypeStruct((TOP_K, n_tok), I32),
                   jax.ShapeDtypeStruct((N_EXPERTS, 1), F32)],
        scratch_shapes=[pltpu.VMEM((N_EXPERTS, 1), F32)],
        compiler_params=_cp(("arbitrary",)),
        name="moe_router",
    )(hx, w_t, rb, erow, tri)
    return eidx, wts, pos, counts.reshape(N_EXPERTS)[perm]


DISPATCH_TM = 512


ROW_UNROLL = 8


def _dispatch_kernel(slot_ref, hx_ref, xs_hbm, sem):
    def issue(tt, carry):
        t8 = pl.multiple_of(tt * ROW_UNROLL, ROW_UNROLL)
        for j in range(ROW_UNROLL):
            src = hx_ref.at[pl.ds(pl.multiple_of((t8 + j) * PK_S, PK_S), PK_S), :]
            for k in range(TOP_K):
                row = pl.multiple_of(slot_ref[k * DISPATCH_TM + j + t8] * PK_S, PK_S)
                pltpu.make_async_copy(src, xs_hbm.at[pl.ds(row, PK_S), :], sem).start(priority=k % 2)
        return carry

    lax.fori_loop(0, DISPATCH_TM // ROW_UNROLL, issue, 0)
    for _ in range(TOP_K):
        pltpu.make_async_copy(hx_ref, xs_hbm.at[pl.ds(0, DISPATCH_TM * PK_S), :], sem).wait()


def moe_dispatch(slots, hx_packed, n_tok, n_rows):
    tm = DISPATCH_TM
    return pl.pallas_call(
        _dispatch_kernel,
        grid=(n_tok // tm,),
        in_specs=[pl.BlockSpec((TOP_K * tm,), lambda i: (i,), memory_space=pltpu.SMEM),
                  pl.BlockSpec((tm * PK_S, LANE), lambda i: (i, 0))],
        out_specs=pl.BlockSpec(memory_space=pl.ANY),
        out_shape=jax.ShapeDtypeStruct((n_rows * PK_S, LANE), U32),
        scratch_shapes=[pltpu.SemaphoreType.DMA(())],
        compiler_params=_cp(("arbitrary",)),
        name="moe_dispatch",
    )(slots, hx_packed)


def _expert_kernel(be_ref, valid_ref, nused_ref, first_ref, next_ref, slot_ref,
                   xs_ref, w1_hbm, w3_hbm, w2_hbm, y_ref, w1s, w3s, w2s, w1b, w3b, w2b, xb, sems, *, layer):
    i = pl.program_id(0)
    bm = EXP_BM

    def weight_copies(e, s):
        return [pltpu.make_async_copy(w_hbm.at[layer, e], stage.at[s], sems.at[s, j])
                for j, (w_hbm, stage) in enumerate(((w1_hbm, w1s), (w3_hbm, w3s), (w2_hbm, w2s)))]

    @pl.when(i < nused_ref[0])
    def _():
        @pl.when(first_ref[i] == 1)
        def _():
            s = slot_ref[i]

            @pl.when(i == 0)
            def _():
                for cp in weight_copies(be_ref[0], 0):
                    cp.start()

            for cp in weight_copies(be_ref[i], s):
                cp.wait()

            @pl.when(next_ref[i] >= 0)
            def _():
                for cp in weight_copies(next_ref[i], 1 - s):
                    cp.start()

            w1b[...] = w1s[s].astype(BF16)
            w3b[...] = w3s[s].astype(BF16)
            w2b[...] = w2s[s].astype(BF16)

        live = lax.broadcasted_iota(I32, (bm, LANE), 0) < valid_ref[i]
        for s, piece in enumerate(_load_row_tiles(xs_ref, 0, bm)):
            lo, hi = _unpack_bf16_pairs(jnp.where(live, piece, jnp.uint32(0)))
            xb[:, s * LANE:(s + 1) * LANE] = lo.astype(BF16)
            xb[:, PK_W + s * LANE:PK_W + (s + 1) * LANE] = hi.astype(BF16)
        x = xb[...]
        h1 = jnp.dot(x, w1b[...], preferred_element_type=F32)
        h3 = jnp.dot(x, w3b[...], preferred_element_type=F32)
        a = (h1 * jax.nn.sigmoid(h1) * h3).astype(BF16)
        y = jnp.dot(a, w2b[...], preferred_element_type=F32)
        _store_row_tiles(y_ref, _pack_bf16_pairs(y.astype(BF16)))


def moe_experts(block_e, valid, n_used, first, next_e, slot, xs, w1, w3, w2, layer, n_blocks):
    bm = EXP_BM

    def blk(i, be, va, nu, fi, ne, sl):
        return (jnp.minimum(i, nu[0] - 1), 0)

    grid_spec = pltpu.PrefetchScalarGridSpec(
        num_scalar_prefetch=6,
        grid=(n_blocks,),
        in_specs=[pl.BlockSpec((bm * PK_S, LANE), blk),
                  pl.BlockSpec(memory_space=pl.ANY),
                  pl.BlockSpec(memory_space=pl.ANY),
                  pl.BlockSpec(memory_space=pl.ANY)],
        out_specs=pl.BlockSpec((bm * PK_S, LANE), blk),
        scratch_shapes=[pltpu.VMEM((2, D, EXPERT_DIM), F32), pltpu.VMEM((2, D, EXPERT_DIM), F32),
                        pltpu.VMEM((2, EXPERT_DIM, D), F32),
                        pltpu.VMEM((D, EXPERT_DIM), BF16), pltpu.VMEM((D, EXPERT_DIM), BF16),
                        pltpu.VMEM((EXPERT_DIM, D), BF16), pltpu.VMEM((bm, D), BF16),
                        pltpu.SemaphoreType.DMA((2, 3))],
    )
    return pl.pallas_call(
        functools.partial(_expert_kernel, layer=layer),
        grid_spec=grid_spec,
        out_shape=jax.ShapeDtypeStruct((n_blocks * bm * PK_S, LANE), U32),
        compiler_params=_cp(("arbitrary",), vmem=56 * 1024 * 1024),
        name="moe_experts",
    )(block_e, valid, n_used, first, next_e, slot, xs, w1, w3, w2)


def _shared_kernel(x_ref, w1_ref, w3_ref, w2_ref, o_ref):
    x = x_ref[...]
    h1 = jnp.dot(x, w1_ref[...], preferred_element_type=F32)
    h3 = jnp.dot(x, w3_ref[...], preferred_element_type=F32)
    a = (h1 * jax.nn.sigmoid(h1) * h3).astype(BF16)
    o_ref[...] = jnp.dot(a, w2_ref[...], preferred_element_type=F32)


def shared_expert(hx, w1, w3, w2, n_tok):
    tm = 512
    return pl.pallas_call(
        _shared_kernel,
        grid=(n_tok // tm,),
        in_specs=[pl.BlockSpec((tm, D), lambda i: (i, 0)),
                  pl.BlockSpec((D, EXPERT_DIM), lambda i: (0, 0)),
                  pl.BlockSpec((D, EXPERT_DIM), lambda i: (0, 0)),
                  pl.BlockSpec((EXPERT_DIM, D), lambda i: (0, 0))],
        out_specs=pl.BlockSpec((tm, D), lambda i: (i, 0)),
        out_shape=jax.ShapeDtypeStruct((n_tok, D), F32),
        compiler_params=_cp(("arbitrary",)),
        name="shared_expert",
    )(hx, w1, w3, w2)


COMBINE_TM = 128


def _combine_kernel(slot_ref, w_ref, sh_ref, x_ref, mod_ref, y_hbm, *rest, final):
    fg_ref = rest[0] if final else None
    o_ref, buf, sem = rest[-3:]
    tm = COMBINE_TM

    def issue(tt, carry):
        t8 = pl.multiple_of(tt * ROW_UNROLL, ROW_UNROLL)
        for j in range(ROW_UNROLL):
            for k in range(TOP_K):
                row = pl.multiple_of(slot_ref[k * tm + j + t8] * PK_S, PK_S)
                dst = buf.at[pl.ds(pl.multiple_of((k * tm + j + t8) * PK_S, PK_S), PK_S), :]
                pltpu.make_async_copy(y_hbm.at[pl.ds(row, PK_S), :], dst, sem).start(priority=k % 2)
        return carry

    lax.fori_loop(0, tm // ROW_UNROLL, issue, 0)
    pltpu.make_async_copy(y_hbm.at[pl.ds(0, TOP_K * tm * PK_S), :], buf, sem).wait()

    w = w_ref[...]
    gate = mod_ref[0][5:6]
    acc = [None] * (2 * PK_S)
    for k in range(TOP_K):
        wk = w[:, k:k + 1]
        for s, piece in enumerate(_load_row_tiles(buf, k * tm * PK_S, tm)):
            for c, val in zip((s, PK_S + s), _unpack_bf16_pairs(piece)):
                acc[c] = wk * val if acc[c] is None else acc[c] + wk * val
    outs = []
    for c in range(2 * PK_S):
        sl = slice(c * LANE, (c + 1) * LANE)
        outs.append(x_ref[:, sl] + gate[:, sl] * (sh_ref[:, sl] + acc[c]))
    if fg_ref is not None:
        ssq = functools.reduce(lambda a, b: a + b, [jnp.sum(o * o, axis=-1, keepdims=True) for o in outs])
        inv = lax.rsqrt(ssq / D + EPS)
        outs = [o * inv * fg_ref[:, c * LANE:(c + 1) * LANE] for c, o in enumerate(outs)]
    for c, o in enumerate(outs):
        o_ref[:, c * LANE:(c + 1) * LANE] = o


def moe_combine(slots, wts_tok, shared, x, mod, y, n_tok, final_gain=None):
    tm = COMBINE_TM
    final = final_gain is not None
    in_specs = [pl.BlockSpec((TOP_K * tm,), lambda i: (i,), memory_space=pltpu.SMEM),
                pl.BlockSpec((tm, TOP_K), lambda i: (i, 0)),
                pl.BlockSpec((tm, D), lambda i: (i, 0)),
                pl.BlockSpec((tm, D), lambda i: (i, 0)),
                pl.BlockSpec((1, 6, D), lambda i: (_mod_row(i * tm), 0, 0)),
                pl.BlockSpec(memory_space=pl.ANY)]
    args = [slots, wts_tok, shared, x, mod, y]
    if final:
        in_specs.append(pl.BlockSpec((1, D), lambda i: (0, 0)))
        args.append(final_gain.reshape(1, D))
    return pl.pallas_call(
        functools.partial(_combine_kernel, final=final),
        grid=(n_tok // tm,),
        in_specs=in_specs,
        out_specs=pl.BlockSpec((tm, D), lambda i: (i, 0)),
        out_shape=jax.ShapeDtypeStruct((n_tok, D), F32),
        scratch_shapes=[pltpu.VMEM((TOP_K * tm * PK_S, LANE), U32), pltpu.SemaphoreType.DMA(())],
        compiler_params=_cp(("arbitrary",)),
        name="moe_combine",
    )(*args)


def _lookup(table, idx):
    e = jnp.arange(table.shape[0], dtype=I32).reshape((-1,) + (1,) * idx.ndim)
    return jnp.sum(jnp.where(idx[None] == e, table.reshape(e.shape), 0), axis=0)


def _tile_flat(slots, tm):
    k, t = slots.shape
    return slots.reshape(k, t // tm, tm).transpose(1, 0, 2).reshape(-1)


def moe_layer(x, mod, norm_gain, router_w, router_b, exp_w1, exp_w3, exp_w2, sw1, sw3, sw2, layer, n_tok,
              final_gain=None):
    bm = EXP_BM
    n_blocks = -(-n_tok * TOP_K // bm) + N_EXPERTS
    hx, hx_packed = norm_mod(x, norm_gain, mod, 3, n_tok, pack=True)
    eidx, wts, pos, counts = moe_router(hx, router_w, router_b, n_tok)
    shared = shared_expert(hx, sw1.astype(BF16), sw3.astype(BF16), sw2.astype(BF16), n_tok)
    cnt = counts.astype(I32)
    padded = (cnt + bm - 1) // bm * bm
    pad_end = jnp.cumsum(padded)
    pad_start = pad_end - padded
    slots = _lookup(pad_start, eidx) + pos
    blk_row = jnp.arange(n_blocks, dtype=I32) * bm
    block_e = jnp.minimum(jnp.sum((pad_end[:, None] <= blk_row[None, :]).astype(I32), axis=0), N_EXPERTS - 1)
    valid = jnp.clip(_lookup(cnt, block_e) - (blk_row - _lookup(pad_start, block_e)), 0, bm).astype(I32)
    n_used = (pad_end[-1:] // bm).astype(I32)
    prev_e = jnp.concatenate([jnp.full((1,), -1, I32), block_e[:-1]])
    first = ((blk_row < pad_end[-1]) & (block_e != prev_e)).astype(I32)
    stage_slot = ((jnp.cumsum(first) - 1) % 2).astype(I32)
    eids = jnp.arange(N_EXPERTS, dtype=I32)
    later = jnp.where((eids[None, :] > eids[:, None]) & (padded[None, :] > 0), eids[None, :], N_EXPERTS)
    next_nonempty = jnp.min(later, axis=1)
    next_nonempty = jnp.where(next_nonempty == N_EXPERTS, -1, next_nonempty)
    next_e = _lookup(next_nonempty, block_e).astype(I32)
    xs = moe_dispatch(_tile_flat(slots, DISPATCH_TM), hx_packed, n_tok, n_blocks * bm)
    y = moe_experts(block_e, valid, n_used, first, next_e, stage_slot, xs, exp_w1, exp_w3, exp_w2, layer, n_blocks)
    return moe_combine(_tile_flat(slots, COMBINE_TM), wts.T, shared, x, mod, y, n_tok, final_gain)


def _rope_tables():
    t = jnp.arange(S, dtype=I32)
    row = (t // GRID_W).astype(F32)
    col = (t % GRID_W).astype(F32)
    n_freq = HD // 4
    inv_freq = ROPE_THETA ** (-jnp.arange(n_freq, dtype=F32) / n_freq)
    ang = jnp.concatenate([row[:, None] * inv_freq, col[:, None] * inv_freq], axis=-1)
    cosf = jnp.repeat(jnp.cos(ang), 2, axis=-1)
    sinf = jnp.stack([-jnp.sin(ang), jnp.sin(ang)], axis=-1).reshape(S, HD)
    return cosf, sinf


def kernel(x, c, ctx, c_ctx, ada_w, ada_b, norm_mix, norm_ffn, attn_w_in, attn_w_out, attn_rpb, attn_q_gain,
           attn_k_gain, ml_w_in, ml_w_out, ml_gate_b, ml_head_gain, router_w, router_b, exp_w1, exp_w3, exp_w2,
           sh_w1, sh_w3, sh_w2, final_norm_gain):
    depth = ada_w.shape[0]
    x_lat = x.reshape(T_LAT, D)
    x_ctx = ctx.reshape(T_CTX, D)
    cvec = jnp.concatenate([c, c_ctx[None], jnp.zeros((8 - B - 1, D), F32)], axis=0)
    mod_all = ada_ln(cvec, ada_w, ada_b).reshape(depth, 8, 6, D)
    cosf, sinf = _rope_tables()

    mod = mod_all[0]
    p = norm_matmul(x_lat, norm_mix[0], mod, attn_w_in[0].astype(BF16), emit_h=False, x_ctx=x_ctx)
    o_all = neighborhood_attention(p, na_bias_table(attn_rpb[0]))
    o_all = gqa_attention(p, cosf, sinf, attn_q_gain[0], attn_k_gain[0], o_all)
    o_all = ctx_attention(p, attn_q_gain[0], attn_k_gain[0], o_all)
    xa = matmul_gated_residual(o_all, attn_w_out[0].astype(BF16), x_lat, mod, 2, x_ctx=x_ctx)
    xa = moe_layer(xa, mod, norm_ffn[0], router_w[0], router_b[0], exp_w1, exp_w3, exp_w2,
                   sh_w1[0], sh_w3[0], sh_w2[0], 0, T_ALL)

    mod = mod_all[1]
    w_in = ml_w_in[0]
    p, hx = norm_matmul(xa, norm_mix[1], mod, w_in[:, :ML_MAIN].astype(BF16), emit_h=True)
    col3, row4 = mlstm_gates(hx, w_in[:, ML_MAIN:], ml_gate_b[0])
    hdir = mlstm_scan(p, col3, row4)
    a = mlstm_readout(hdir, p, ml_head_gain[0])
    xl = matmul_gated_residual(a, ml_w_out[0].astype(BF16), xa, mod, 2)
    xl = moe_layer(xl, mod, norm_ffn[1], router_w[1], router_b[1], exp_w1, exp_w3, exp_w2,
                   sh_w1[1], sh_w3[1], sh_w2[1], 1, T_LAT, final_gain=final_norm_gain)
    return xl.reshape(B, S, D)
```

```python
import functools

import jax
import jax.numpy as jnp
from jax import lax
from jax.experimental import pallas as pl
from jax.experimental.pallas import tpu as pltpu

F32 = jnp.float32
BF16 = jnp.bfloat16
I32 = jnp.int32
U32 = jnp.uint32

D = 2048
B = 4
S = 4096
L = 256
T_LAT = B * S
T_CTX = B * L
T_ALL = T_LAT + T_CTX
GRID_W = 64
ROWS = S // GRID_W
HD = 128
NA_HEADS = 8
NA_WIN_ROWS = 8
NA_WIN_COLS = 16
GQA_Q_HEADS = 8
GQA_KV_HEADS = 2
GQA_GROUP = 4
ROPE_THETA = 10000.0
ATTN_IN = 4608
ML_HEADS = 8
ML_V = 256
ML_QK = 128
ML_MAIN = 6144
N_EXPERTS = 64
TOP_K = 8
N_GROUPS = 8
TOPK_GROUPS = 4
EXPERT_DIM = 512
ROUTED_SCALE = 2.5
EPS = 1e-6
NEG_INF = -1e30
ATT_SCALE = HD ** -0.5
LOG2E = 1.4426950408889634
ML_KSCALE = ML_QK ** -0.5

LANE = 128
NA_QROWS = 4
NA_SLAB = NA_QROWS + NA_WIN_ROWS - 1
NA_QB = NA_QROWS * GRID_W
NA_KB = NA_SLAB * GRID_W
ML_CH = 256
EXP_BM = 512
PK_W = D // 2
PK_S = PK_W // LANE
VMEM_LIMIT = 48 * 1024 * 1024


def _cp(sem, vmem=VMEM_LIMIT):
    return pltpu.CompilerParams(dimension_semantics=sem, vmem_limit_bytes=vmem)


def _pack_bf16_pairs(xb):
    u = pltpu.bitcast(xb.astype(F32), U32)
    return (u[:, PK_W:] & jnp.uint32(0xFFFF0000)) | (u[:, :PK_W] >> 16)


def _unpack_bf16_pairs(u):
    return pltpu.bitcast(u << 16, F32), pltpu.bitcast(u & jnp.uint32(0xFFFF0000), F32)


def _store_row_tiles(ref, words):
    rows = words.shape[0]
    for s in range(PK_S):
        ref[pl.ds(s, rows, stride=PK_S), :] = words[:, s * LANE:(s + 1) * LANE]


def _load_row_tiles(ref, start, rows):
    return [ref[pl.ds(start + s, rows, stride=PK_S), :] for s in range(PK_S)]


def _mod_row(start_row):
    return jnp.where(start_row < T_LAT, start_row // S, B)


def _ada_kernel(c_ref, w_ref, b_ref, o_ref):
    c = c_ref[...]
    a = (c * jax.nn.sigmoid(c)).astype(BF16)
    w = w_ref[0].astype(BF16)
    o_ref[0] = jnp.dot(a, w, preferred_element_type=F32) + b_ref[0]


def ada_ln(cvec, ada_w, ada_b):
    depth = ada_w.shape[0]
    n = ada_w.shape[2]
    tn = 1024
    return pl.pallas_call(
        _ada_kernel,
        grid=(depth, n // tn),
        in_specs=[pl.BlockSpec((8, D), lambda l, j: (0, 0)),
                  pl.BlockSpec((1, D, tn), lambda l, j: (l, 0, j)),
                  pl.BlockSpec((1, 1, tn), lambda l, j: (l, 0, j))],
        out_specs=pl.BlockSpec((1, 8, tn), lambda l, j: (l, 0, j)),
        out_shape=jax.ShapeDtypeStruct((depth, 8, n), F32),
        compiler_params=_cp(("arbitrary", "arbitrary")),
        name="ada_ln",
    )(cvec, ada_w, ada_b.reshape(depth, 1, n))


def _norm_mod_kernel(x_ref, g_ref, mod_ref, *out_refs, base, pack):
    x = x_ref[...]
    y = x * lax.rsqrt(jnp.mean(x * x, axis=-1, keepdims=True) + EPS) * g_ref[...]
    m = mod_ref[0]
    h = y * (1.0 + m[base + 1:base + 2]) + m[base:base + 1]
    hb = h.astype(BF16)
    out_refs[0][...] = hb
    if pack:
        _store_row_tiles(out_refs[1], _pack_bf16_pairs(hb))


def norm_mod(x, gain, mod, base, n_rows, pack):
    tm = 256
    out_shape = [jax.ShapeDtypeStruct((n_rows, D), BF16)]
    out_specs = [pl.BlockSpec((tm, D), lambda i: (i, 0))]
    if pack:
        out_shape.append(jax.ShapeDtypeStruct((n_rows * PK_S, LANE), U32))
        out_specs.append(pl.BlockSpec((tm * PK_S, LANE), lambda i: (i, 0)))
    res = pl.pallas_call(
        functools.partial(_norm_mod_kernel, base=base, pack=pack),
        grid=(n_rows // tm,),
        in_specs=[pl.BlockSpec((tm, D), lambda i: (i, 0)),
                  pl.BlockSpec((1, D), lambda i: (0, 0)),
                  pl.BlockSpec((1, 6, D), lambda i: (_mod_row(i * tm), 0, 0))],
        out_specs=out_specs,
        out_shape=out_shape,
        compiler_params=_cp(("arbitrary",)),
        name="norm_mod",
    )(x, gain.reshape(1, D), mod)
    return res if pack else res[0]


def _norm_mm_kernel(*refs, emit_h, n_lat):
    if n_lat is None:
        x_ref, g_ref, mod_ref, w_ref, o_ref = refs[:5]
        c_ref = None
    else:
        x_ref, c_ref, g_ref, mod_ref, w_ref, o_ref = refs[:6]
    hb_ref = refs[-1]

    def prologue(src_ref):
        x = src_ref[...]
        y = x * lax.rsqrt(jnp.mean(x * x, axis=-1, keepdims=True) + EPS) * g_ref[...]
        m = mod_ref[0]
        hb_ref[...] = (y * (1.0 + m[1:2]) + m[0:1]).astype(BF16)
        if emit_h:
            refs[-2][...] = hb_ref[...]

    first_col = pl.program_id(1) == 0
    if c_ref is None:
        pl.when(first_col)(lambda: prologue(x_ref))
    else:
        is_lat = pl.program_id(0) < n_lat
        pl.when(first_col & is_lat)(lambda: prologue(x_ref))
        pl.when(first_col & jnp.logical_not(is_lat))(lambda: prologue(c_ref))

    o_ref[...] = jnp.dot(hb_ref[...], w_ref[...], preferred_element_type=F32).astype(o_ref.dtype)


def norm_matmul(x, gain, mod, w, emit_h, x_ctx=None, tm=1024, tn=512):
    split = x_ctx is not None
    m = x.shape[0] + (x_ctx.shape[0] if split else 0)
    n = w.shape[1]
    n_lat = x.shape[0] // tm if split else None
    out_shape = [jax.ShapeDtypeStruct((m, n), BF16)]
    out_specs = [pl.BlockSpec((tm, tn), lambda i, j: (i, j))]
    if emit_h:
        out_shape.append(jax.ShapeDtypeStruct((m, D), BF16))
        out_specs.append(pl.BlockSpec((tm, D), lambda i, j: (i, 0)))
    if split:
        x_specs = [pl.BlockSpec((tm, D), lambda i, j: (jnp.minimum(i, n_lat - 1), 0)),
                   pl.BlockSpec((tm, D), lambda i, j: (0, 0))]
        x_args = [x, x_ctx]
    else:
        x_specs = [pl.BlockSpec((tm, D), lambda i, j: (i, 0))]
        x_args = [x]
    res = pl.pallas_call(
        functools.partial(_norm_mm_kernel, emit_h=emit_h, n_lat=n_lat),
        grid=(m // tm, n // tn),
        in_specs=x_specs + [pl.BlockSpec((1, D), lambda i, j: (0, 0)),
                            pl.BlockSpec((1, 6, D), lambda i, j: (_mod_row(i * tm), 0, 0)),
                            pl.BlockSpec((D, tn), lambda i, j: (0, j))],
        out_specs=out_specs,
        out_shape=out_shape,
        scratch_shapes=[pltpu.VMEM((tm, D), BF16)],
        compiler_params=_cp(("arbitrary", "arbitrary"), vmem=56 * 1024 * 1024),
        name="norm_matmul",
    )(*x_args, gain.reshape(1, D), mod, w)
    return res if emit_h else res[0]


def _mm_res_kernel(*refs, slot, n_lat):
    if n_lat is None:
        a_ref, w_ref, x_ref, mod_ref, o_ref = refs
        res = x_ref[...]
    else:
        a_ref, w_ref, x_ref, c_ref, mod_ref, o_ref = refs
        res = jnp.where(pl.program_id(0) < n_lat, x_ref[...], c_ref[...])
    acc = jnp.dot(a_ref[...], w_ref[...], preferred_element_type=F32)
    o_ref[...] = res + mod_ref[0][slot:slot + 1] * acc


def matmul_gated_residual(a, w, x, mod, slot, x_ctx=None, tm=1024, tn=512):
    m, k = a.shape
    n = w.shape[1]
    split = x_ctx is not None
    n_lat = x.shape[0] // tm if split else None
    if split:
        x_specs = [pl.BlockSpec((tm, tn), lambda i, j: (jnp.minimum(i, n_lat - 1), j)),
                   pl.BlockSpec((tm, tn), lambda i, j: (0, j))]
        x_args = [x, x_ctx]
    else:
        x_specs = [pl.BlockSpec((tm, tn), lambda i, j: (i, j))]
        x_args = [x]
    return pl.pallas_call(
        functools.partial(_mm_res_kernel, slot=slot, n_lat=n_lat),
        grid=(m // tm, n // tn),
        in_specs=[pl.BlockSpec((tm, k), lambda i, j: (i, 0)),
                  pl.BlockSpec((k, tn), lambda i, j: (0, j))] + x_specs
                 + [pl.BlockSpec((1, 6, tn), lambda i, j: (_mod_row(i * tm), 0, j))],
        out_specs=pl.BlockSpec((tm, tn), lambda i, j: (i, j)),
        out_shape=jax.ShapeDtypeStruct((m, n), F32),
        compiler_params=_cp(("arbitrary", "arbitrary")),
        name="matmul_gated_residual",
    )(a, w, *x_args, mod)


def _dot_nt(a, b):
    return lax.dot_general(a, b, (((1,), (1,)), ((), ())), preferred_element_type=F32)


def _rms_head(x, gain):
    return x * lax.rsqrt(jnp.mean(x * x, axis=-1, keepdims=True) + EPS) * gain


def _rope(x, cosf, sinf):
    lane = lax.broadcasted_iota(I32, x.shape, 1)
    nxt = pltpu.roll(x, LANE - 1, 1)
    prv = pltpu.roll(x, 1, 1)
    return x * cosf + jnp.where((lane & 1) == 0, nxt, prv) * sinf


def _softmax_av(parts):
    m = functools.reduce(jnp.maximum, [jnp.max(s, axis=-1, keepdims=True) for s, _ in parts])
    l = None
    o = None
    for s, v in parts:
        p = jnp.exp(s - m)
        li = jnp.sum(p, axis=-1, keepdims=True)
        oi = jnp.dot(p.astype(BF16), v, preferred_element_type=F32)
        l = li if l is None else l + li
        o = oi if o is None else o + oi
    return o / l


def _na_kernel(q_ref, k_ref, v_ref, kc_ref, vc_ref, tab_ref, o_ref):
    kc = kc_ref[...]
    vc = vc_ref[...]
    n_blocks = ROWS // NA_QROWS

    def body(j, carry):
        ks = jnp.clip(j * NA_QROWS - NA_WIN_ROWS // 2, 0, ROWS - NA_SLAB)
        typ = jnp.where(j == 0, 0, jnp.where(j == n_blocks - 1, 2, 1))
        qs = pl.multiple_of(j * NA_QB, NA_QB)
        kst = pl.multiple_of(ks * GRID_W, GRID_W)
        q = q_ref[pl.ds(qs, NA_QB), :]
        k = k_ref[pl.ds(kst, NA_KB), :]
        v = v_ref[pl.ds(kst, NA_KB), :]
        s_win = _dot_nt(q, k) * ATT_SCALE + tab_ref[typ, 0]
        s_ctx = _dot_nt(q, kc) * ATT_SCALE
        o_ref[pl.ds(qs, NA_QB), :] = _softmax_av([(s_win, v), (s_ctx, vc)]).astype(BF16)
        return carry

    lax.fori_loop(0, n_blocks, body, 0)


def na_bias_table(rpb):
    def one(r0, ks):
        r = r0 + jnp.arange(NA_QROWS)
        kr = ks + jnp.arange(NA_SLAB)
        start = jnp.clip(r - NA_WIN_ROWS // 2, 0, ROWS - NA_WIN_ROWS)
        row_ok = (kr[None, :] >= start[:, None]) & (kr[None, :] < start[:, None] + NA_WIN_ROWS)
        row_idx = jnp.clip(kr[None, :] - r[:, None] + NA_WIN_ROWS - 1, 0, 2 * NA_WIN_ROWS - 2)
        cq = jnp.arange(GRID_W)
        col_start = jnp.clip(cq - NA_WIN_COLS // 2, 0, GRID_W - NA_WIN_COLS)
        col_ok = (cq[None, :] >= col_start[:, None]) & (cq[None, :] < col_start[:, None] + NA_WIN_COLS)
        col_idx = jnp.clip(cq[None, :] - cq[:, None] + NA_WIN_COLS - 1, 0, 2 * NA_WIN_COLS - 2)
        r_hot = jax.nn.one_hot(row_idx, 2 * NA_WIN_ROWS - 1, dtype=F32)
        c_hot = jax.nn.one_hot(col_idx, 2 * NA_WIN_COLS - 1, dtype=F32)
        bias = jnp.einsum('qka,hab,xyb->hqxky', r_hot, rpb.astype(F32), c_hot, precision=lax.Precision.HIGHEST)
        ok = row_ok[:, None, :, None] & col_ok[None, :, None, :]
        return jnp.where(ok[None], bias, NEG_INF).reshape(NA_HEADS, NA_QB, NA_KB)

    mid = 2 * NA_QROWS
    last = ROWS - NA_QROWS
    return jnp.stack([one(0, 0), one(mid, mid - NA_WIN_ROWS // 2), one(last, ROWS - NA_SLAB)])


def neighborhood_attention(p, table):
    cb = S // L
    return pl.pallas_call(
        _na_kernel,
        grid=(NA_HEADS, B),
        in_specs=[pl.BlockSpec((S, HD), lambda h, b: (b, h)),
                  pl.BlockSpec((S, HD), lambda h, b: (b, NA_HEADS + h)),
                  pl.BlockSpec((S, HD), lambda h, b: (b, 2 * NA_HEADS + h)),
                  pl.BlockSpec((L, HD), lambda h, b: (B * cb + b, NA_HEADS + h)),
                  pl.BlockSpec((L, HD), lambda h, b: (B * cb + b, 2 * NA_HEADS + h)),
                  pl.BlockSpec((3, 1, NA_QB, NA_KB), lambda h, b: (0, h, 0, 0))],
        out_specs=pl.BlockSpec((S, HD), lambda h, b: (b, h)),
        out_shape=jax.ShapeDtypeStruct((T_ALL, D), BF16),
        compiler_params=_cp(("arbitrary", "arbitrary")),
        name="neighborhood_attention",
    )(p, p, p, p, p, table)


GQA_TQ = 256
GQA_CK = 512
GQA_QCOL = 3 * NA_HEADS
GQA_KCOL = GQA_QCOL + GQA_Q_HEADS
GQA_VCOL = GQA_KCOL + GQA_KV_HEADS


def _gqa_kernel(q_ref, k_ref, v_ref, kc_ref, vc_ref, cq_ref, sq_ref, ck_ref, sk_ref, qg_ref, kg_ref, o_prev,
                o_ref, kn_ref, kcn_ref):
    del o_prev
    @pl.when(pl.program_id(2) == 0)
    def _():
        kn = _rope(_rms_head(k_ref[...].astype(F32), kg_ref[...]), ck_ref[...], sk_ref[...])
        kn_ref[...] = kn.astype(BF16)
        kcn_ref[...] = _rms_head(kc_ref[...].astype(F32), kg_ref[...]).astype(BF16)

    cos = cq_ref[...]
    sin = sq_ref[...]
    heads = []
    for g in range(GQA_GROUP):
        qh = _rope(_rms_head(q_ref[:, g * HD:(g + 1) * HD].astype(F32), qg_ref[...]), cos, sin)
        heads.append((qh * (ATT_SCALE * LOG2E)).astype(BF16))
    q = jnp.concatenate(heads, axis=0)
    chunks = [(kn_ref[c * GQA_CK:(c + 1) * GQA_CK, :], v_ref[c * GQA_CK:(c + 1) * GQA_CK, :])
              for c in range(S // GQA_CK)]
    chunks.append((kcn_ref[...], vc_ref[...]))
    m = l = acc = None
    for kk, vv in chunks:
        s = _dot_nt(q, kk)
        mc = jnp.max(s, axis=-1, keepdims=True)
        if m is None:
            m_new = mc
            p = jnp.exp2(s - m_new)
            l = jnp.sum(p, axis=-1, keepdims=True)
            acc = jnp.dot(p.astype(BF16), vv, preferred_element_type=F32)
        else:
            m_new = jnp.maximum(m, mc)
            alpha = jnp.exp2(m - m_new)
            p = jnp.exp2(s - m_new)
            l = alpha * l + jnp.sum(p, axis=-1, keepdims=True)
            acc = alpha * acc + jnp.dot(p.astype(BF16), vv, preferred_element_type=F32)
        m = m_new
    o = acc / l
    for g in range(GQA_GROUP):
        o_ref[:, g * HD:(g + 1) * HD] = o[g * GQA_TQ:(g + 1) * GQA_TQ].astype(BF16)


def gqa_attention(p, cosf, sinf, q_gain, k_gain, o_buf):
    nq = S // GQA_TQ
    cb = S // L
    gw = GQA_GROUP * HD
    return pl.pallas_call(
        _gqa_kernel,
        grid=(B, GQA_KV_HEADS, nq),
        in_specs=[pl.BlockSpec((GQA_TQ, gw), lambda b, n, i: (b * nq + i, GQA_QCOL // GQA_GROUP + n)),
                  pl.BlockSpec((S, HD), lambda b, n, i: (b, GQA_KCOL + n)),
                  pl.BlockSpec((S, HD), lambda b, n, i: (b, GQA_VCOL + n)),
                  pl.BlockSpec((L, HD), lambda b, n, i: (B * cb + b, GQA_KCOL + n)),
                  pl.BlockSpec((L, HD), lambda b, n, i: (B * cb + b, GQA_VCOL + n)),
                  pl.BlockSpec((GQA_TQ, HD), lambda b, n, i: (i, 0)),
                  pl.BlockSpec((GQA_TQ, HD), lambda b, n, i: (i, 0)),
                  pl.BlockSpec((S, HD), lambda b, n, i: (0, 0)),
                  pl.BlockSpec((S, HD), lambda b, n, i: (0, 0)),
                  pl.BlockSpec((1, HD), lambda b, n, i: (0, 0)),
                  pl.BlockSpec((1, HD), lambda b, n, i: (0, 0)),
                  pl.BlockSpec(memory_space=pl.ANY)],
        out_specs=pl.BlockSpec((GQA_TQ, gw), lambda b, n, i: (b * nq + i, (NA_HEADS * HD) // gw + n)),
        out_shape=jax.ShapeDtypeStruct((T_ALL, D), BF16),
        input_output_aliases={11: 0},
        scratch_shapes=[pltpu.VMEM((S, HD), BF16), pltpu.VMEM((L, HD), BF16)],
        compiler_params=_cp(("arbitrary",) * 3),
        name="gqa_attention",
    )(p, p, p, p, p, cosf, sinf, cosf, sinf, q_gain.reshape(1, HD), k_gain.reshape(1, HD), o_buf)


def _ctx_attn_kernel(p_ref, qg_ref, kg_ref, o_prev, o_ref):
    del o_prev

    def col(c):
        return p_ref[:, c * HD:(c + 1) * HD]

    for h in range(NA_HEADS):
        s = _dot_nt(col(h), col(NA_HEADS + h)) * ATT_SCALE
        o_ref[:, h * HD:(h + 1) * HD] = _softmax_av([(s, col(2 * NA_HEADS + h))]).astype(BF16)
    for n in range(GQA_KV_HEADS):
        kn = _rms_head(col(GQA_KCOL + n).astype(F32), kg_ref[...]).astype(BF16)
        v = col(GQA_VCOL + n)
        for g in range(GQA_GROUP):
            h = n * GQA_GROUP + g
            qn = _rms_head(col(GQA_QCOL + h).astype(F32), qg_ref[...]).astype(BF16)
            s = _dot_nt(qn, kn) * ATT_SCALE
            o_ref[:, (NA_HEADS + h) * HD:(NA_HEADS + h + 1) * HD] = _softmax_av([(s, v)]).astype(BF16)


def ctx_attention(p, q_gain, k_gain, o_buf):
    cb = S // L
    return pl.pallas_call(
        _ctx_attn_kernel,
        grid=(B,),
        in_specs=[pl.BlockSpec((L, ATTN_IN), lambda b: (B * cb + b, 0)),
                  pl.BlockSpec((1, HD), lambda b: (0, 0)),
                  pl.BlockSpec((1, HD), lambda b: (0, 0)),
                  pl.BlockSpec(memory_space=pl.ANY)],
        out_specs=pl.BlockSpec((L, D), lambda b: (B * cb + b, 0)),
        out_shape=jax.ShapeDtypeStruct((T_ALL, D), BF16),
        input_output_aliases={3: 0},
        compiler_params=_cp(("arbitrary",)),
        name="ctx_attention",
    )(p, q_gain.reshape(1, HD), k_gain.reshape(1, HD), o_buf)


def _log_sigmoid(x):
    return -(jnp.maximum(-x, 0.0) + jnp.log1p(jnp.exp(-jnp.abs(x))))


def _dot_hi(a, b):
    return jnp.dot(a, b, precision=lax.Precision.HIGHEST, preferred_element_type=F32)


def _gate_kernel(h_ref, wg_ref, wgt_ref, b_ref, bt_ref, lt_ref, ut_ref, col_ref, row_ref):
    nh = ML_HEADS
    hx = h_ref[...]
    g = jnp.dot(hx, wg_ref[...], preferred_element_type=F32) + b_ref[...]
    gt = _dot_nt(wgt_ref[...], hx) + bt_ref[...]
    li = g[:, 0:2 * nh]
    lf = _log_sigmoid(g[:, 2 * nh:4 * nh])
    lit = gt[0:2 * nh]
    lft = _log_sigmoid(gt[2 * nh:4 * nh])
    lt = lt_ref[...]
    ut = ut_ref[...]
    lane = lax.broadcasted_iota(I32, lf.shape, 1)
    bc = jnp.where(lane < nh, _dot_hi(lt, lf), _dot_hi(ut, lf))
    tot = jnp.sum(lf, axis=0, keepdims=True)
    aend = tot - bc + li
    col_ref[...] = jnp.concatenate([bc, aend, jnp.zeros((ML_CH, LANE - 4 * nh), F32)], axis=1)
    sub = lax.broadcasted_iota(I32, lft.shape, 0)
    bct = jnp.where(sub < nh, _dot_hi(lft, ut), _dot_hi(lft, lt))
    tott = jnp.sum(lft, axis=1, keepdims=True)
    gtr = lit - bct
    row_ref[0] = jnp.concatenate([bct, gtr, tott + gtr, jnp.broadcast_to(tott, bct.shape)], axis=0)


def mlstm_gates(hx, wg, gate_b):
    nh = ML_HEADS
    n_ch = T_ALL // ML_CH
    wg_pad = jnp.zeros((D, LANE), BF16).at[:, :4 * nh].set(wg.astype(BF16))
    b_pad = jnp.zeros((1, LANE), F32).at[0, :4 * nh].set(gate_b.reshape(-1))
    wgt = wg.astype(BF16).T
    bt = gate_b.reshape(4 * nh, 1).astype(F32)
    lt = jnp.tril(jnp.ones((ML_CH, ML_CH), F32))
    ut = jnp.triu(jnp.ones((ML_CH, ML_CH), F32))
    col, row = pl.pallas_call(
        _gate_kernel,
        grid=(n_ch,),
        in_specs=[pl.BlockSpec((ML_CH, D), lambda i: (i, 0)),
                  pl.BlockSpec((D, LANE), lambda i: (0, 0)),
                  pl.BlockSpec((4 * nh, D), lambda i: (0, 0)),
                  pl.BlockSpec((1, LANE), lambda i: (0, 0)),
                  pl.BlockSpec((4 * nh, 1), lambda i: (0, 0)),
                  pl.BlockSpec((ML_CH, ML_CH), lambda i: (0, 0)),
                  pl.BlockSpec((ML_CH, ML_CH), lambda i: (0, 0))],
        out_specs=[pl.BlockSpec((ML_CH, LANE), lambda i: (i, 0)),
                   pl.BlockSpec((1, 8 * nh, ML_CH), lambda i: (i, 0, 0))],
        out_shape=[jax.ShapeDtypeStruct((T_ALL, LANE), F32),
                   jax.ShapeDtypeStruct((n_ch, 8 * nh, ML_CH), F32)],
        compiler_params=_cp(("arbitrary",)),
        name="mlstm_gates",
    )(hx, wg_pad, wgt, b_pad, bt, lt, ut)
    return col, row


def _mlstm_step(d, hh, q_ref, k_ref, v_ref, col_ref, row_ref, o_ref, c_ref, n_ref, m_ref):
    nd = 2 * ML_HEADS
    sl = d * ML_HEADS + hh
    q = q_ref[:, hh * ML_QK:(hh + 1) * ML_QK]
    kf = k_ref[:, hh * ML_QK:(hh + 1) * ML_QK].astype(F32) * ML_KSCALE
    kb = kf.astype(BF16)
    v = v_ref[:, hh * ML_V:(hh + 1) * ML_V]
    bc_col = col_ref[:, sl:sl + 1]
    aend_col = col_ref[:, nd + sl:nd + sl + 1]
    g_row = row_ref[0, nd + sl:nd + sl + 1, :]
    aend_row = row_ref[0, 2 * nd + sl:2 * nd + sl + 1, :]
    btot = row_ref[0, 3 * nd + sl:3 * nd + sl + 1, 0:1]
    m_st = m_ref[sl]
    c_st = c_ref[sl]
    n_st = n_ref[sl]
    m_new = jnp.maximum(btot + m_st, jnp.max(aend_row, axis=1, keepdims=True))

    r = lax.broadcasted_iota(I32, (ML_CH, ML_CH), 0)
    c = lax.broadcasted_iota(I32, (ML_CH, ML_CH), 1)
    causal = (r >= c) if d == 0 else (r <= c)
    d_mat = jnp.where(causal, bc_col + g_row, -jnp.inf)
    m_row = jnp.maximum(bc_col + m_st, jnp.max(d_mat, axis=1, keepdims=True))
    w_inter = jnp.exp(bc_col + m_st - m_row)
    s_mat = _dot_nt(q, kb) * jnp.exp(d_mat - m_row)
    num = (w_inter * jnp.dot(q, c_st.astype(BF16), preferred_element_type=F32)
           + jnp.dot(s_mat.astype(BF16), v, preferred_element_type=F32))
    den = (w_inter * jnp.sum(q.astype(F32) * n_st, axis=1, keepdims=True)
           + jnp.sum(s_mat, axis=1, keepdims=True))
    o_ref[:, hh * ML_V:(hh + 1) * ML_V] = num / jnp.maximum(jnp.abs(den), jnp.exp(-m_row))

    w_end_col = jnp.exp(aend_col - m_new)
    w_end_row = jnp.exp(aend_row - m_new)
    decay = jnp.exp(btot + m_st - m_new)
    kw = (kf * w_end_col).astype(BF16)
    c_ref[sl] = decay * c_st + lax.dot_general(kw, v, (((0,), (0,)), ((), ())), preferred_element_type=F32)
    w8 = jnp.broadcast_to(w_end_row, (8, ML_CH)).astype(BF16)
    n_ref[sl] = decay * n_st + jnp.dot(w8, kb, preferred_element_type=F32)[0:1]
    m_ref[sl] = m_new


def _mlstm_kernel(qf, kf, vf, colf, rowf, qb, kb, vb, colb, rowb, of, ob, c_ref, n_ref, m_ref):
    @pl.when(pl.program_id(1) == 0)
    def _():
        c_ref[...] = jnp.zeros_like(c_ref)
        n_ref[...] = jnp.zeros_like(n_ref)
        m_ref[...] = jnp.zeros_like(m_ref)

    for hh in range(ML_HEADS):
        _mlstm_step(0, hh, qf, kf, vf, colf, rowf, of, c_ref, n_ref, m_ref)
        _mlstm_step(1, hh, qb, kb, vb, colb, rowb, ob, c_ref, n_ref, m_ref)


def mlstm_scan(p, col, row):
    n_lat = S // ML_CH
    steps = n_lat + 1
    lat_blocks = T_LAT // ML_CH
    qk_w = ML_HEADS * ML_QK
    v_w = ML_HEADS * ML_V
    n_chains = 2 * ML_HEADS

    def chunk(b, d, st):
        c = (st - 1) if d == 0 else (n_lat - st)
        return jnp.where(st == 0, lat_blocks + b, b * n_lat + c)

    def out_chunk(b, d, st):
        s1 = jnp.maximum(st, 1)
        return b * n_lat + ((s1 - 1) if d == 0 else (n_lat - s1))

    def dir_specs(d):
        return [pl.BlockSpec((ML_CH, qk_w), lambda b, s: (chunk(b, d, s), 0)),
                pl.BlockSpec((ML_CH, qk_w), lambda b, s: (chunk(b, d, s), 1)),
                pl.BlockSpec((ML_CH, v_w), lambda b, s: (chunk(b, d, s), (2 * qk_w) // v_w)),
                pl.BlockSpec((ML_CH, LANE), lambda b, s: (chunk(b, d, s), 0)),
                pl.BlockSpec((1, 4 * n_chains, ML_CH), lambda b, s: (chunk(b, d, s), 0, 0))]

    return pl.pallas_call(
        _mlstm_kernel,
        grid=(B, steps),
        in_specs=dir_specs(0) + dir_specs(1),
        out_specs=[pl.BlockSpec((ML_CH, v_w), lambda b, s: (out_chunk(b, 0, s), 0)),
                   pl.BlockSpec((ML_CH, v_w), lambda b, s: (out_chunk(b, 1, s), 0))],
        out_shape=[jax.ShapeDtypeStruct((T_LAT, v_w), F32), jax.ShapeDtypeStruct((T_LAT, v_w), F32)],
        scratch_shapes=[pltpu.VMEM((n_chains, ML_QK, ML_V), F32), pltpu.VMEM((n_chains, 1, ML_QK), F32),
                        pltpu.VMEM((n_chains, 1, 1), F32)],
        compiler_params=_cp(("arbitrary",) * 2),
        name="mlstm_scan",
    )(p, p, p, col, row, p, p, p, col, row)


def _readout_kernel(hf_ref, hb_ref, o_ref, g_ref, a_ref):
    hs = hf_ref[...] + hb_ref[...]
    for h in range(ML_HEADS):
        sl = slice(h * ML_V, (h + 1) * ML_V)
        x = hs[:, sl]
        hn = x * lax.rsqrt(jnp.mean(x * x, axis=-1, keepdims=True) + EPS) * g_ref[:, sl]
        a_ref[:, sl] = (hn * jax.nn.sigmoid(o_ref[:, sl].astype(F32))).astype(BF16)


def mlstm_readout(hdir, p, head_gain):
    tm = 256
    ocol = (2 * ML_HEADS * ML_QK + ML_HEADS * ML_V) // D
    return pl.pallas_call(
        _readout_kernel,
        grid=(T_LAT // tm,),
        in_specs=[pl.BlockSpec((tm, D), lambda i: (i, 0)),
                  pl.BlockSpec((tm, D), lambda i: (i, 0)),
                  pl.BlockSpec((tm, D), lambda i: (i, ocol)),
                  pl.BlockSpec((1, D), lambda i: (0, 0))],
        out_specs=pl.BlockSpec((tm, D), lambda i: (i, 0)),
        out_shape=jax.ShapeDtypeStruct((T_LAT, D), BF16),
        compiler_params=_cp(("arbitrary",)),
        name="mlstm_readout",
    )(hdir[0], hdir[1], p, head_gain.reshape(1, D))


ROUTER_TM = 512


def _router_kernel(h_ref, w_ref, rb_ref, erow_ref, tri_ref, eidx_ref, wts_ref, pos_ref, cnt_ref, carry_ref):
    ng = N_GROUPS
    epg = N_EXPERTS // N_GROUPS
    tm = ROUTER_TM
    ninf = -jnp.inf

    @pl.when(pl.program_id(0) == 0)
    def _():
        carry_ref[...] = jnp.zeros_like(carry_ref)

    s = jax.nn.sigmoid(_dot_nt(w_ref[...], h_ref[...]))
    ssel = s + rb_ref[...]
    sraw = [s[ng * j:ng * (j + 1)] for j in range(epg)]
    slab = [ssel[ng * j:ng * (j + 1)] for j in range(epg)]
    m1 = functools.reduce(jnp.maximum, slab)
    jfirst = functools.reduce(jnp.minimum, [jnp.where(slab[j] == m1, j, epg) for j in range(epg)])
    m2 = functools.reduce(jnp.maximum, [jnp.where(jfirst == j, ninf, slab[j]) for j in range(epg)])
    gs = m1 + m2
    giota = lax.broadcasted_iota(I32, (ng, tm), 0)
    gsel = jnp.zeros((ng, tm), F32)
    for _ in range(TOPK_GROUPS):
        mx = jnp.max(gs, axis=0, keepdims=True)
        gi = jnp.min(jnp.where(gs == mx, giota, ng), axis=0, keepdims=True)
        hit = giota == gi
        gsel = jnp.where(hit, 1.0, gsel)
        gs = jnp.where(hit, ninf, gs)
    msl = [jnp.where(gsel > 0.0, slab[j], ninf) for j in range(epg)]
    eid = [giota * epg + j for j in range(epg)]
    selm = [jnp.zeros((ng, tm), F32) for _ in range(epg)]
    e_list, w_list = [], []
    for _ in range(TOP_K):
        mx = jnp.max(functools.reduce(jnp.maximum, msl), axis=0, keepdims=True)
        cand = functools.reduce(jnp.minimum, [jnp.where(msl[j] == mx, eid[j], N_EXPERTS) for j in range(epg)])
        esel = jnp.min(cand, axis=0, keepdims=True)
        hits = [eid[j] == esel for j in range(epg)]
        wk = functools.reduce(lambda a, b: a + b, [jnp.where(hits[j], sraw[j], 0.0) for j in range(epg)])
        w_list.append(jnp.sum(wk, axis=0, keepdims=True))
        e_list.append(esel)
        msl = [jnp.where(hits[j], ninf, msl[j]) for j in range(epg)]
        selm = [jnp.where(hits[j], 1.0, selm[j]) for j in range(epg)]
    wsum = functools.reduce(lambda a, b: a + b, w_list)
    wts_ref[...] = jnp.concatenate([w / wsum * ROUTED_SCALE for w in w_list], axis=0)
    eidx_ref[...] = jnp.concatenate(e_list, axis=0)
    sel = jnp.concatenate(selm, axis=0)
    carry = carry_ref[...]
    posfull = jnp.dot(sel.astype(BF16), tri_ref[...], preferred_element_type=F32) + carry
    erow = erow_ref[...]
    pos = [jnp.sum(jnp.where(erow == e, posfull, 0.0), axis=0, keepdims=True) for e in e_list]
    pos_ref[...] = jnp.concatenate(pos, axis=0).astype(I32)
    carry = carry + jnp.sum(sel, axis=1, keepdims=True)
    carry_ref[...] = carry
    cnt_ref[...] = carry


def moe_router(hx, router_w, router_b, n_tok):
    tm = ROUTER_TM
    epg = N_EXPERTS // N_GROUPS
    perm = (jnp.arange(N_EXPERTS) % N_GROUPS) * epg + jnp.arange(N_EXPERTS) // N_GROUPS
    w_t = router_w.astype(BF16).T[perm]
    rb = router_b.astype(F32)[perm].reshape(N_EXPERTS, 1)
    erow = perm.astype(I32).reshape(N_EXPERTS, 1)
    tri = jnp.triu(jnp.ones((tm, tm), BF16), 1)
    eidx, wts, pos, counts = pl.pallas_call(
        _router_kernel,
        grid=(n_tok // tm,),
        in_specs=[pl.BlockSpec((tm, D), lambda i: (i, 0)),
                  pl.BlockSpec((N_EXPERTS, D), lambda i: (0, 0)),
                  pl.BlockSpec((N_EXPERTS, 1), lambda i: (0, 0)),
                  pl.BlockSpec((N_EXPERTS, 1), lambda i: (0, 0)),
                  pl.BlockSpec((tm, tm), lambda i: (0, 0))],
        out_specs=[pl.BlockSpec((TOP_K, tm), lambda i: (0, i)),
                   pl.BlockSpec((TOP_K, tm), lambda i: (0, i)),
                   pl.BlockSpec((TOP_K, tm), lambda i: (0, i)),
                   pl.BlockSpec((N_EXPERTS, 1), lambda i: (0, 0))],
        out_shape=[jax.ShapeDtypeStruct((TOP_K, n_tok), I32),
                   jax.ShapeDtypeStruct((TOP_K, n_tok), F32),
                   jax.ShapeDtypeStruct((TOP_K, n_tok), I32),
                   jax.ShapeDtypeStruct((N_EXPERTS, 1), F32)],
        scratch_shapes=[pltpu.VMEM((N_EXPERTS, 1), F32)],
        compiler_params=_cp(("arbitrary",)),
        name="moe_router",
    )(hx, w_t, rb, erow, tri)
    return eidx, wts, pos, counts.reshape(N_EXPERTS)[perm]


DISPATCH_TM = 512


ROW_UNROLL = 8


def _dispatch_kernel(slot_ref, hx_ref, xs_hbm, sem):
    def issue(tt, carry):
        t8 = pl.multiple_of(tt * ROW_UNROLL, ROW_UNROLL)
        for j in range(ROW_UNROLL):
            src = hx_ref.at[pl.ds(pl.multiple_of((t8 + j) * PK_S, PK_S), PK_S), :]
            for k in range(TOP_K):
                row = pl.multiple_of(slot_ref[k * DISPATCH_TM + j + t8] * PK_S, PK_S)
                pltpu.make_async_copy(src, xs_hbm.at[pl.ds(row, PK_S), :], sem).start(priority=k % 2)
        return carry

    lax.fori_loop(0, DISPATCH_TM // ROW_UNROLL, issue, 0)
    for _ in range(TOP_K):
        pltpu.make_async_copy(hx_ref, xs_hbm.at[pl.ds(0, DISPATCH_TM * PK_S), :], sem).wait()


def moe_dispatch(slots, hx_packed, n_tok, n_rows):
    tm = DISPATCH_TM
    return pl.pallas_call(
        _dispatch_kernel,
        grid=(n_tok // tm,),
        in_specs=[pl.BlockSpec((TOP_K * tm,), lambda i: (i,), memory_space=pltpu.SMEM),
                  pl.BlockSpec((tm * PK_S, LANE), lambda i: (i, 0))],
        out_specs=pl.BlockSpec(memory_space=pl.ANY),
        out_shape=jax.ShapeDtypeStruct((n_rows * PK_S, LANE), U32),
        scratch_shapes=[pltpu.SemaphoreType.DMA(())],
        compiler_params=_cp(("arbitrary",)),
        name="moe_dispatch",
    )(slots, hx_packed)


def _expert_kernel(be_ref, valid_ref, nused_ref, first_ref, next_ref, slot_ref,
                   xs_ref, w1_hbm, w3_hbm, w2_hbm, y_ref, w1s, w3s, w2s, w1b, w3b, w2b, xb, sems, *, layer):
    i = pl.program_id(0)
    bm = EXP_BM

    def weight_copies(e, s):
        return [pltpu.make_async_copy(w_hbm.at[layer, e], stage.at[s], sems.at[s, j])
                for j, (w_hbm, stage) in enumerate(((w1_hbm, w1s), (w3_hbm, w3s), (w2_hbm, w2s)))]

    @pl.when(i < nused_ref[0])
    def _():
        @pl.when(first_ref[i] == 1)
        def _():
            s = slot_ref[i]

            @pl.when(i == 0)
            def _():
                for cp in weight_copies(be_ref[0], 0):
                    cp.start()

            for cp in weight_copies(be_ref[i], s):
                cp.wait()

            @pl.when(next_ref[i] >= 0)
            def _():
                for cp in weight_copies(next_ref[i], 1 - s):
                    cp.start()

            w1b[...] = w1s[s].astype(BF16)
            w3b[...] = w3s[s].astype(BF16)
            w2b[...] = w2s[s].astype(BF16)

        live = lax.broadcasted_iota(I32, (bm, LANE), 0) < valid_ref[i]
        for s, piece in enumerate(_load_row_tiles(xs_ref, 0, bm)):
            lo, hi = _unpack_bf16_pairs(jnp.where(live, piece, jnp.uint32(0)))
            xb[:, s * LANE:(s + 1) * LANE] = lo.astype(BF16)
            xb[:, PK_W + s * LANE:PK_W + (s + 1) * LANE] = hi.astype(BF16)
        x = xb[...]
        h1 = jnp.dot(x, w1b[...], preferred_element_type=F32)
        h3 = jnp.dot(x, w3b[...], preferred_element_type=F32)
        a = (h1 * jax.nn.sigmoid(h1) * h3).astype(BF16)
        y = jnp.dot(a, w2b[...], preferred_element_type=F32)
        _store_row_tiles(y_ref, _pack_bf16_pairs(y.astype(BF16)))


def moe_experts(block_e, valid, n_used, first, next_e, slot, xs, w1, w3, w2, layer, n_blocks):
    bm = EXP_BM

    def blk(i, be, va, nu, fi, ne, sl):
        return (jnp.minimum(i, nu[0] - 1), 0)

    grid_spec = pltpu.PrefetchScalarGridSpec(
        num_scalar_prefetch=6,
        grid=(n_blocks,),
        in_specs=[pl.BlockSpec((bm * PK_S, LANE), blk),
                  pl.BlockSpec(memory_space=pl.ANY),
                  pl.BlockSpec(memory_space=pl.ANY),
                  pl.BlockSpec(memory_space=pl.ANY)],
        out_specs=pl.BlockSpec((bm * PK_S, LANE), blk),
        scratch_shapes=[pltpu.VMEM((2, D, EXPERT_DIM), F32), pltpu.VMEM((2, D, EXPERT_DIM), F32),
                        pltpu.VMEM((2, EXPERT_DIM, D), F32),
                        pltpu.VMEM((D, EXPERT_DIM), BF16), pltpu.VMEM((D, EXPERT_DIM), BF16),
                        pltpu.VMEM((EXPERT_DIM, D), BF16), pltpu.VMEM((bm, D), BF16),
                        pltpu.SemaphoreType.DMA((2, 3))],
    )
    return pl.pallas_call(
        functools.partial(_expert_kernel, layer=layer),
        grid_spec=grid_spec,
        out_shape=jax.ShapeDtypeStruct((n_blocks * bm * PK_S, LANE), U32),
        compiler_params=_cp(("arbitrary",), vmem=56 * 1024 * 1024),
        name="moe_experts",
    )(block_e, valid, n_used, first, next_e, slot, xs, w1, w3, w2)


def _shared_kernel(x_ref, w1_ref, w3_ref, w2_ref, o_ref):
    x = x_ref[...]
    h1 = jnp.dot(x, w1_ref[...], preferred_element_type=F32)
    h3 = jnp.dot(x, w3_ref[...], preferred_element_type=F32)
    a = (h1 * jax.nn.sigmoid(h1) * h3).astype(BF16)
    o_ref[...] = jnp.dot(a, w2_ref[...], preferred_element_type=F32)


def shared_expert(hx, w1, w3, w2, n_tok):
    tm = 512
    return pl.pallas_call(
        _shared_kernel,
        grid=(n_tok // tm,),
        in_specs=[pl.BlockSpec((tm, D), lambda i: (i, 0)),
                  pl.BlockSpec((D, EXPERT_DIM), lambda i: (0, 0)),
                  pl.BlockSpec((D, EXPERT_DIM), lambda i: (0, 0)),
                  pl.BlockSpec((EXPERT_DIM, D), lambda i: (0, 0))],
        out_specs=pl.BlockSpec((tm, D), lambda i: (i, 0)),
        out_shape=jax.ShapeDtypeStruct((n_tok, D), F32),
        compiler_params=_cp(("arbitrary",)),
        name="shared_expert",
    )(hx, w1, w3, w2)


COMBINE_TM = 128


def _combine_kernel(slot_ref, w_ref, sh_ref, x_ref, mod_ref, y_hbm, *rest, final):
    fg_ref = rest[0] if final else None
    o_ref, buf, sem = rest[-3:]
    tm = COMBINE_TM

    def issue(tt, carry):
        t8 = pl.multiple_of(tt * ROW_UNROLL, ROW_UNROLL)
        for j in range(ROW_UNROLL):
            for k in range(TOP_K):
                row = pl.multiple_of(slot_ref[k * tm + j + t8] * PK_S, PK_S)
                dst = buf.at[pl.ds(pl.multiple_of((k * tm + j + t8) * PK_S, PK_S), PK_S), :]
                pltpu.make_async_copy(y_hbm.at[pl.ds(row, PK_S), :], dst, sem).start(priority=k % 2)
        return carry

    lax.fori_loop(0, tm // ROW_UNROLL, issue, 0)
    pltpu.make_async_copy(y_hbm.at[pl.ds(0, TOP_K * tm * PK_S), :], buf, sem).wait()

    w = w_ref[...]
    gate = mod_ref[0][5:6]
    acc = [None] * (2 * PK_S)
    for k in range(TOP_K):
        wk = w[:, k:k + 1]
        for s, piece in enumerate(_load_row_tiles(buf, k * tm * PK_S, tm)):
            for c, val in zip((s, PK_S + s), _unpack_bf16_pairs(piece)):
                acc[c] = wk * val if acc[c] is None else acc[c] + wk * val
    outs = []
    for c in range(2 * PK_S):
        sl = slice(c * LANE, (c + 1) * LANE)
        outs.append(x_ref[:, sl] + gate[:, sl] * (sh_ref[:, sl] + acc[c]))
    if fg_ref is not None:
        ssq = functools.reduce(lambda a, b: a + b, [jnp.sum(o * o, axis=-1, keepdims=True) for o in outs])
        inv = lax.rsqrt(ssq / D + EPS)
        outs = [o * inv * fg_ref[:, c * LANE:(c + 1) * LANE] for c, o in enumerate(outs)]
    for c, o in enumerate(outs):
        o_ref[:, c * LANE:(c + 1) * LANE] = o


def moe_combine(slots, wts_tok, shared, x, mod, y, n_tok, final_gain=None):
    tm = COMBINE_TM
    final = final_gain is not None
    in_specs = [pl.BlockSpec((TOP_K * tm,), lambda i: (i,), memory_space=pltpu.SMEM),
                pl.BlockSpec((tm, TOP_K), lambda i: (i, 0)),
                pl.BlockSpec((tm, D), lambda i: (i, 0)),
                pl.BlockSpec((tm, D), lambda i: (i, 0)),
                pl.BlockSpec((1, 6, D), lambda i: (_mod_row(i * tm), 0, 0)),
                pl.BlockSpec(memory_space=pl.ANY)]
    args = [slots, wts_tok, shared, x, mod, y]
    if final:
        in_specs.append(pl.BlockSpec((1, D), lambda i: (0, 0)))
        args.append(final_gain.reshape(1, D))
    return pl.pallas_call(
        functools.partial(_combine_kernel, final=final),
        grid=(n_tok // tm,),
        in_specs=in_specs,
        out_specs=pl.BlockSpec((tm, D), lambda i: (i, 0)),
        out_shape=jax.ShapeDtypeStruct((n_tok, D), F32),
        scratch_shapes=[pltpu.VMEM((TOP_K * tm * PK_S, LANE), U32), pltpu.SemaphoreType.DMA(())],
        compiler_params=_cp(("arbitrary",)),
        name="moe_combine",
    )(*args)


def _lookup(table, idx):
    e = jnp.arange(table.shape[0], dtype=I32).reshape((-1,) + (1,) * idx.ndim)
    return jnp.sum(jnp.where(idx[None] == e, table.reshape(e.shape), 0), axis=0)


def _tile_flat(slots, tm):
    k, t = slots.shape
    return slots.reshape(k, t // tm, tm).transpose(1, 0, 2).reshape(-1)


def moe_layer(x, mod, norm_gain, router_w, router_b, exp_w1, exp_w3, exp_w2, sw1, sw3, sw2, layer, n_tok,
              final_gain=None):
    bm = EXP_BM
    n_blocks = -(-n_tok * TOP_K // bm) + N_EXPERTS
    hx, hx_packed = norm_mod(x, norm_gain, mod, 3, n_tok, pack=True)
    eidx, wts, pos, counts = moe_router(hx, router_w, router_b, n_tok)
    shared = shared_expert(hx, sw1.astype(BF16), sw3.astype(BF16), sw2.astype(BF16), n_tok)
    cnt = counts.astype(I32)
    padded = (cnt + bm - 1) // bm * bm
    pad_end = jnp.cumsum(padded)
    pad_start = pad_end - padded
    slots = _lookup(pad_start, eidx) + pos
    blk_row = jnp.arange(n_blocks, dtype=I32) * bm
    block_e = jnp.minimum(jnp.sum((pad_end[:, None] <= blk_row[None, :]).astype(I32), axis=0), N_EXPERTS - 1)
    valid = jnp.clip(_lookup(cnt, block_e) - (blk_row - _lookup(pad_start, block_e)), 0, bm).astype(I32)
    n_used = (pad_end[-1:] // bm).astype(I32)
    prev_e = jnp.concatenate([jnp.full((1,), -1, I32), block_e[:-1]])
    first = ((blk_row < pad_end[-1]) & (block_e != prev_e)).astype(I32)
    stage_slot = ((jnp.cumsum(first) - 1) % 2).astype(I32)
    eids = jnp.arange(N_EXPERTS, dtype=I32)
    later = jnp.where((eids[None, :] > eids[:, None]) & (padded[None, :] > 0), eids[None, :], N_EXPERTS)
    next_nonempty = jnp.min(later, axis=1)
    next_nonempty = jnp.where(next_nonempty == N_EXPERTS, -1, next_nonempty)
    next_e = _lookup(next_nonempty, block_e).astype(I32)
    xs = moe_dispatch(_tile_flat(slots, DISPATCH_TM), hx_packed, n_tok, n_blocks * bm)
    y = moe_experts(block_e, valid, n_used, first, next_e, stage_slot, xs, exp_w1, exp_w3, exp_w2, layer, n_blocks)
    return moe_combine(_tile_flat(slots, COMBINE_TM), wts.T, shared, x, mod, y, n_tok, final_gain)


def _rope_tables():
    t = jnp.arange(S, dtype=I32)
    row = (t // GRID_W).astype(F32)
    col = (t % GRID_W).astype(F32)
    n_freq = HD // 4
    inv_freq = ROPE_THETA ** (-jnp.arange(n_freq, dtype=F32) / n_freq)
    ang = jnp.concatenate([row[:, None] * inv_freq, col[:, None] * inv_freq], axis=-1)
    cosf = jnp.repeat(jnp.cos(ang), 2, axis=-1)
    sinf = jnp.stack([-jnp.sin(ang), jnp.sin(ang)], axis=-1).reshape(S, HD)
    return cosf, sinf


def kernel(x, c, ctx, c_ctx, ada_w, ada_b, norm_mix, norm_ffn, attn_w_in, attn_w_out, attn_rpb, attn_q_gain,
           attn_k_gain, ml_w_in, ml_w_out, ml_gate_b, ml_head_gain, router_w, router_b, exp_w1, exp_w3, exp_w2,
           sh_w1, sh_w3, sh_w2, final_norm_gain):
    depth = ada_w.shape[0]
    x_lat = x.reshape(T_LAT, D)
    x_ctx = ctx.reshape(T_CTX, D)
    cvec = jnp.concatenate([c, c_ctx[None], jnp.zeros((8 - B - 1, D), F32)], axis=0)
    mod_all = ada_ln(cvec, ada_w, ada_b).reshape(depth, 8, 6, D)
    cosf, sinf = _rope_tables()

    mod = mod_all[0]
    p = norm_matmul(x_lat, norm_mix[0], mod, attn_w_in[0].astype(BF16), emit_h=False, x_ctx=x_ctx)
    o_all = neighborhood_attention(p, na_bias_table(attn_rpb[0]))
    o_all = gqa_attention(p, cosf, sinf, attn_q_gain[0], attn_k_gain[0], o_all)
    o_all = ctx_attention(p, attn_q_gain[0], attn_k_gain[0], o_all)
    xa = matmul_gated_residual(o_all, attn_w_out[0].astype(BF16), x_lat, mod, 2, x_ctx=x_ctx)
    xa = moe_layer(xa, mod, norm_ffn[0], router_w[0], router_b[0], exp_w1, exp_w3, exp_w2,
                   sh_w1[0], sh_w3[0], sh_w2[0], 0, T_ALL)

    mod = mod_all[1]
    w_in = ml_w_in[0]
    p, hx = norm_matmul(xa, norm_mix[1], mod, w_in[:, :ML_MAIN].astype(BF16), emit_h=True)
    col, row = mlstm_gates(hx, w_in[:, ML_MAIN:], ml_gate_b[0])
    hdir = mlstm_scan(p, col, row)
    a = mlstm_readout(hdir, p, ml_head_gain[0])
    xl = matmul_gated_residual(a, ml_w_out[0].astype(BF16), xa, mod, 2)
    xl = moe_layer(xl, mod, norm_ffn[1], router_w[1], router_b[1], exp_w1, exp_w3, exp_w2,
                   sh_w1[1], sh_w3[1], sh_w2[1], 1, T_LAT, final_gain=final_norm_gain)
    return xl.reshape(B, S, D)
```

```python
import functools

import jax
import jax.numpy as jnp
from jax import lax
from jax.experimental import pallas as pl
from jax.experimental.pallas import tpu as pltpu

F32 = jnp.float32
BF16 = jnp.bfloat16
I32 = jnp.int32
U32 = jnp.uint32

D = 2048
B = 4
S = 4096
L = 256
T_LAT = B * S
T_CTX = B * L
T_ALL = T_LAT + T_CTX
GRID_W = 64
ROWS = S // GRID_W
HD = 128
NA_HEADS = 8
NA_WIN_ROWS = 8
NA_WIN_COLS = 16
GQA_Q_HEADS = 8
GQA_KV_HEADS = 2
GQA_GROUP = 4
ROPE_THETA = 10000.0
ATTN_IN = 4608
ML_HEADS = 8
ML_V = 256
ML_QK = 128
ML_MAIN = 6144
N_EXPERTS = 64
TOP_K = 8
N_GROUPS = 8
TOPK_GROUPS = 4
EXPERT_DIM = 512
ROUTED_SCALE = 2.5
EPS = 1e-6
NEG_INF = -1e30
ATT_SCALE = HD ** -0.5
LOG2E = 1.4426950408889634
ML_KSCALE = ML_QK ** -0.5

LANE = 128
NA_QROWS = 4
NA_SLAB = NA_QROWS + NA_WIN_ROWS - 1
NA_QB = NA_QROWS * GRID_W
NA_KB = NA_SLAB * GRID_W
ML_CH = 256
EXP_BM = 512
PK_W = D // 2
PK_S = PK_W // LANE
VMEM_LIMIT = 48 * 1024 * 1024


def _cp(sem, vmem=VMEM_LIMIT):
    return pltpu.CompilerParams(dimension_semantics=sem, vmem_limit_bytes=vmem)


def _pack_bf16_pairs(xb):
    u = pltpu.bitcast(xb.astype(F32), U32)
    return (u[:, PK_W:] & jnp.uint32(0xFFFF0000)) | (u[:, :PK_W] >> 16)


def _unpack_bf16_pairs(u):
    return pltpu.bitcast(u << 16, F32), pltpu.bitcast(u & jnp.uint32(0xFFFF0000), F32)


def _store_row_tiles(ref, words):
    rows = words.shape[0]
    for s in range(PK_S):
        ref[pl.ds(s, rows, stride=PK_S), :] = words[:, s * LANE:(s + 1) * LANE]


def _load_row_tiles(ref, start, rows):
    return [ref[pl.ds(start + s, rows, stride=PK_S), :] for s in range(PK_S)]


def _mod_row(start_row):
    return jnp.where(start_row < T_LAT, start_row // S, B)


def _ada_kernel(c_ref, w_ref, b_ref, o_ref):
    c = c_ref[...]
    a = (c * jax.nn.sigmoid(c)).astype(BF16)
    w = w_ref[0].astype(BF16)
    o_ref[0] = jnp.dot(a, w, preferred_element_type=F32) + b_ref[0]


def ada_ln(cvec, ada_w, ada_b):
    depth = ada_w.shape[0]
    n = ada_w.shape[2]
    tn = 1024
    return pl.pallas_call(
        _ada_kernel,
        grid=(depth, n // tn),
        in_specs=[pl.BlockSpec((8, D), lambda l, j: (0, 0)),
                  pl.BlockSpec((1, D, tn), lambda l, j: (l, 0, j)),
                  pl.BlockSpec((1, 1, tn), lambda l, j: (l, 0, j))],
        out_specs=pl.BlockSpec((1, 8, tn), lambda l, j: (l, 0, j)),
        out_shape=jax.ShapeDtypeStruct((depth, 8, n), F32),
        compiler_params=_cp(("arbitrary", "arbitrary")),
        name="ada_ln",
    )(cvec, ada_w, ada_b.reshape(depth, 1, n))


def _norm_mod_kernel(x_ref, g_ref, mod_ref, *out_refs, base, pack):
    x = x_ref[...]
    y = x * lax.rsqrt(jnp.mean(x * x, axis=-1, keepdims=True) + EPS) * g_ref[...]
    m = mod_ref[0]
    h = y * (1.0 + m[base + 1:base + 2]) + m[base:base + 1]
    hb = h.astype(BF16)
    out_refs[0][...] = hb
    if pack:
        _store_row_tiles(out_refs[1], _pack_bf16_pairs(hb))


def norm_mod(x, gain, mod, base, n_rows, pack):
    tm = 256
    out_shape = [jax.ShapeDtypeStruct((n_rows, D), BF16)]
    out_specs = [pl.BlockSpec((tm, D), lambda i: (i, 0))]
    if pack:
        out_shape.append(jax.ShapeDtypeStruct((n_rows * PK_S, LANE), U32))
        out_specs.append(pl.BlockSpec((tm * PK_S, LANE), lambda i: (i, 0)))
    res = pl.pallas_call(
        functools.partial(_norm_mod_kernel, base=base, pack=pack),
        grid=(n_rows // tm,),
        in_specs=[pl.BlockSpec((tm, D), lambda i: (i, 0)),
                  pl.BlockSpec((1, D), lambda i: (0, 0)),
                  pl.BlockSpec((1, 6, D), lambda i: (_mod_row(i * tm), 0, 0))],
        out_specs=out_specs,
        out_shape=out_shape,
        compiler_params=_cp(("arbitrary",)),
        name="norm_mod",
    )(x, gain.reshape(1, D), mod)
    return res if pack else res[0]


def _norm_mm_kernel(*refs, emit_h, n_lat):
    if n_lat is None:
        x_ref, g_ref, mod_ref, w_ref, o_ref = refs[:5]
        c_ref = None
    else:
        x_ref, c_ref, g_ref, mod_ref, w_ref, o_ref = refs[:6]
    hb_ref = refs[-1]

    def prologue(src_ref):
        x = src_ref[...]
        y = x * lax.rsqrt(jnp.mean(x * x, axis=-1, keepdims=True) + EPS) * g_ref[...]
        m = mod_ref[0]
        hb_ref[...] = (y * (1.0 + m[1:2]) + m[0:1]).astype(BF16)
        if emit_h:
            refs[-2][...] = hb_ref[...]

    first_col = pl.program_id(1) == 0
    if c_ref is None:
        pl.when(first_col)(lambda: prologue(x_ref))
    else:
        is_lat = pl.program_id(0) < n_lat
        pl.when(first_col & is_lat)(lambda: prologue(x_ref))
        pl.when(first_col & jnp.logical_not(is_lat))(lambda: prologue(c_ref))

    o_ref[...] = jnp.dot(hb_ref[...], w_ref[...], preferred_element_type=F32).astype(o_ref.dtype)


def norm_matmul(x, gain, mod, w, emit_h, x_ctx=None, tm=1024, tn=512):
    split = x_ctx is not None
    m = x.shape[0] + (x_ctx.shape[0] if split else 0)
    n = w.shape[1]
    n_lat = x.shape[0] // tm if split else None
    out_shape = [jax.ShapeDtypeStruct((m, n), BF16)]
    out_specs = [pl.BlockSpec((tm, tn), lambda i, j: (i, j))]
    if emit_h:
        out_shape.append(jax.ShapeDtypeStruct((m, D), BF16))
        out_specs.append(pl.BlockSpec((tm, D), lambda i, j: (i, 0)))
    if split:
        x_specs = [pl.BlockSpec((tm, D), lambda i, j: (jnp.minimum(i, n_lat - 1), 0)),
                   pl.BlockSpec((tm, D), lambda i, j: (0, 0))]
        x_args = [x, x_ctx]
    else:
        x_specs = [pl.BlockSpec((tm, D), lambda i, j: (i, 0))]
        x_args = [x]
    res = pl.pallas_call(
        functools.partial(_norm_mm_kernel, emit_h=emit_h, n_lat=n_lat),
        grid=(m // tm, n // tn),
        in_specs=x_specs + [pl.BlockSpec((1, D), lambda i, j: (0, 0)),
                            pl.BlockSpec((1, 6, D), lambda i, j: (_mod_row(i * tm), 0, 0)),
                            pl.BlockSpec((D, tn), lambda i, j: (0, j))],
        out_specs=out_specs,
        out_shape=out_shape,
        scratch_shapes=[pltpu.VMEM((tm, D), BF16)],
        compiler_params=_cp(("arbitrary", "arbitrary"), vmem=56 * 1024 * 1024),
        name="norm_matmul",
    )(*x_args, gain.reshape(1, D), mod, w)
    return res if emit_h else res[0]


def _mm_res_kernel(*refs, slot, n_lat):
    if n_lat is None:
        a_ref, w_ref, x_ref, mod_ref, o_ref = refs
        res = x_ref[...]
    else:
        a_ref, w_ref, x_ref, c_ref, mod_ref, o_ref = refs
        res = jnp.where(pl.program_id(0) < n_lat, x_ref[...], c_ref[...])
    acc = jnp.dot(a_ref[...], w_ref[...], preferred_element_type=F32)
    o_ref[...] = res + mod_ref[0][slot:slot + 1] * acc


def matmul_gated_residual(a, w, x, mod, slot, x_ctx=None, tm=1024, tn=512):
    m, k = a.shape
    n = w.shape[1]
    split = x_ctx is not None
    n_lat = x.shape[0] // tm if split else None
    if split:
        x_specs = [pl.BlockSpec((tm, tn), lambda i, j: (jnp.minimum(i, n_lat - 1), j)),
                   pl.BlockSpec((tm, tn), lambda i, j: (0, j))]
        x_args = [x, x_ctx]
    else:
        x_specs = [pl.BlockSpec((tm, tn), lambda i, j: (i, j))]
        x_args = [x]
    return pl.pallas_call(
        functools.partial(_mm_res_kernel, slot=slot, n_lat=n_lat),
        grid=(m // tm, n // tn),
        in_specs=[pl.BlockSpec((tm, k), lambda i, j: (i, 0)),
                  pl.BlockSpec((k, tn), lambda i, j: (0, j))] + x_specs
                 + [pl.BlockSpec((1, 6, tn), lambda i, j: (_mod_row(i * tm), 0, j))],
        out_specs=pl.BlockSpec((tm, tn), lambda i, j: (i, j)),
        out_shape=jax.ShapeDtypeStruct((m, n), F32),
        compiler_params=_cp(("arbitrary", "arbitrary")),
        name="matmul_gated_residual",
    )(a, w, *x_args, mod)


def _dot_nt(a, b):
    return lax.dot_general(a, b, (((1,), (1,)), ((), ())), preferred_element_type=F32)


def _rms_head(x, gain):
    return x * lax.rsqrt(jnp.mean(x * x, axis=-1, keepdims=True) + EPS) * gain


def _rope(x, cosf, sinf):
    lane = lax.broadcasted_iota(I32, x.shape, 1)
    nxt = pltpu.roll(x, LANE - 1, 1)
    prv = pltpu.roll(x, 1, 1)
    return x * cosf + jnp.where((lane & 1) == 0, nxt, prv) * sinf


def _softmax_av(parts):
    m = functools.reduce(jnp.maximum, [jnp.max(s, axis=-1, keepdims=True) for s, _ in parts])
    l = None
    o = None
    for s, v in parts:
        p = jnp.exp(s - m)
        li = jnp.sum(p, axis=-1, keepdims=True)
        oi = jnp.dot(p.astype(BF16), v, preferred_element_type=F32)
        l = li if l is None else l + li
        o = oi if o is None else o + oi
    return o / l


def _na_kernel(q_ref, k_ref, v_ref, kc_ref, vc_ref, tab_ref, o_ref):
    kc = kc_ref[...]
    vc = vc_ref[...]
    n_blocks = ROWS // NA_QROWS

    def body(j, carry):
        ks = jnp.clip(j * NA_QROWS - NA_WIN_ROWS // 2, 0, ROWS - NA_SLAB)
        typ = jnp.where(j == 0, 0, jnp.where(j == n_blocks - 1, 2, 1))
        qs = pl.multiple_of(j * NA_QB, NA_QB)
        kst = pl.multiple_of(ks * GRID_W, GRID_W)
        q = q_ref[pl.ds(qs, NA_QB), :]
        k = k_ref[pl.ds(kst, NA_KB), :]
        v = v_ref[pl.ds(kst, NA_KB), :]
        s_win = _dot_nt(q, k) * ATT_SCALE + tab_ref[typ, 0]
        s_ctx = _dot_nt(q, kc) * ATT_SCALE
        o_ref[pl.ds(qs, NA_QB), :] = _softmax_av([(s_win, v), (s_ctx, vc)]).astype(BF16)
        return carry

    lax.fori_loop(0, n_blocks, body, 0)


def na_bias_table(rpb):
    def one(r0, ks):
        r = r0 + jnp.arange(NA_QROWS)
        kr = ks + jnp.arange(NA_SLAB)
        start = jnp.clip(r - NA_WIN_ROWS // 2, 0, ROWS - NA_WIN_ROWS)
        row_ok = (kr[None, :] >= start[:, None]) & (kr[None, :] < start[:, None] + NA_WIN_ROWS)
        row_idx = jnp.clip(kr[None, :] - r[:, None] + NA_WIN_ROWS - 1, 0, 2 * NA_WIN_ROWS - 2)
        cq = jnp.arange(GRID_W)
        col_start = jnp.clip(cq - NA_WIN_COLS // 2, 0, GRID_W - NA_WIN_COLS)
        col_ok = (cq[None, :] >= col_start[:, None]) & (cq[None, :] < col_start[:, None] + NA_WIN_COLS)
        col_idx = jnp.clip(cq[None, :] - cq[:, None] + NA_WIN_COLS - 1, 0, 2 * NA_WIN_COLS - 2)
        r_hot = jax.nn.one_hot(row_idx, 2 * NA_WIN_ROWS - 1, dtype=F32)
        c_hot = jax.nn.one_hot(col_idx, 2 * NA_WIN_COLS - 1, dtype=F32)
        bias = jnp.einsum('qka,hab,xyb->hqxky', r_hot, rpb.astype(F32), c_hot, precision=lax.Precision.HIGHEST)
        ok = row_ok[:, None, :, None] & col_ok[None, :, None, :]
        return jnp.where(ok[None], bias, NEG_INF).reshape(NA_HEADS, NA_QB, NA_KB)

    mid = 2 * NA_QROWS
    last = ROWS - NA_QROWS
    return jnp.stack([one(0, 0), one(mid, mid - NA_WIN_ROWS // 2), one(last, ROWS - NA_SLAB)])


def neighborhood_attention(p, table):
    cb = S // L
    return pl.pallas_call(
        _na_kernel,
        grid=(NA_HEADS, B),
        in_specs=[pl.BlockSpec((S, HD), lambda h, b: (b, h)),
                  pl.BlockSpec((S, HD), lambda h, b: (b, NA_HEADS + h)),
                  pl.BlockSpec((S, HD), lambda h, b: (b, 2 * NA_HEADS + h)),
                  pl.BlockSpec((L, HD), lambda h, b: (B * cb + b, NA_HEADS + h)),
                  pl.BlockSpec((L, HD), lambda h, b: (B * cb + b, 2 * NA_HEADS + h)),
                  pl.BlockSpec((3, 1, NA_QB, NA_KB), lambda h, b: (0, h, 0, 0))],
        out_specs=pl.BlockSpec((S, HD), lambda h, b: (b, h)),
        out_shape=jax.ShapeDtypeStruct((T_ALL, D), BF16),
        compiler_params=_cp(("arbitrary", "arbitrary")),
        name="neighborhood_attention",
    )(p, p, p, p, p, table)


GQA_TQ = 256
GQA_CK = 512
GQA_QCOL = 3 * NA_HEADS
GQA_KCOL = GQA_QCOL + GQA_Q_HEADS
GQA_VCOL = GQA_KCOL + GQA_KV_HEADS


def _gqa_kernel(q_ref, k_ref, v_ref, kc_ref, vc_ref, cq_ref, sq_ref, ck_ref, sk_ref, qg_ref, kg_ref, o_prev,
                o_ref, kn_ref, kcn_ref):
    del o_prev
    @pl.when(pl.program_id(2) == 0)
    def _():
        kn = _rope(_rms_head(k_ref[...].astype(F32), kg_ref[...]), ck_ref[...], sk_ref[...])
        kn_ref[...] = kn.astype(BF16)
        kcn_ref[...] = _rms_head(kc_ref[...].astype(F32), kg_ref[...]).astype(BF16)

    cos = cq_ref[...]
    sin = sq_ref[...]
    heads = []
    for g in range(GQA_GROUP):
        qh = _rope(_rms_head(q_ref[:, g * HD:(g + 1) * HD].astype(F32), qg_ref[...]), cos, sin)
        heads.append((qh * (ATT_SCALE * LOG2E)).astype(BF16))
    q = jnp.concatenate(heads, axis=0)
    chunks = [(kn_ref[c * GQA_CK:(c + 1) * GQA_CK, :], v_ref[c * GQA_CK:(c + 1) * GQA_CK, :])
              for c in range(S // GQA_CK)]
    chunks.append((kcn_ref[...], vc_ref[...]))
    m = l = acc = None
    for kk, vv in chunks:
        s = _dot_nt(q, kk)
        mc = jnp.max(s, axis=-1, keepdims=True)
        if m is None:
            m_new = mc
            p = jnp.exp2(s - m_new)
            l = jnp.sum(p, axis=-1, keepdims=True)
            acc = jnp.dot(p.astype(BF16), vv, preferred_element_type=F32)
        else:
            m_new = jnp.maximum(m, mc)
            alpha = jnp.exp2(m - m_new)
            p = jnp.exp2(s - m_new)
            l = alpha * l + jnp.sum(p, axis=-1, keepdims=True)
            acc = alpha * acc + jnp.dot(p.astype(BF16), vv, preferred_element_type=F32)
        m = m_new
    o = acc / l
    for g in range(GQA_GROUP):
        o_ref[:, g * HD:(g + 1) * HD] = o[g * GQA_TQ:(g + 1) * GQA_TQ].astype(BF16)


def gqa_attention(p, cosf, sinf, q_gain, k_gain, o_buf):
    nq = S // GQA_TQ
    cb = S // L
    gw = GQA_GROUP * HD
    return pl.pallas_call(
        _gqa_kernel,
        grid=(B, GQA_KV_HEADS, nq),
        in_specs=[pl.BlockSpec((GQA_TQ, gw), lambda b, n, i: (b * nq + i, GQA_QCOL // GQA_GROUP + n)),
                  pl.BlockSpec((S, HD), lambda b, n, i: (b, GQA_KCOL + n)),
                  pl.BlockSpec((S, HD), lambda b, n, i: (b, GQA_VCOL + n)),
                  pl.BlockSpec((L, HD), lambda b, n, i: (B * cb + b, GQA_KCOL + n)),
                  pl.BlockSpec((L, HD), lambda b, n, i: (B * cb + b, GQA_VCOL + n)),
                  pl.BlockSpec((GQA_TQ, HD), lambda b, n, i: (i, 0)),
                  pl.BlockSpec((GQA_TQ, HD), lambda b, n, i: (i, 0)),
                  pl.BlockSpec((S, HD), lambda b, n, i: (0, 0)),
                  pl.BlockSpec((S, HD), lambda b, n, i: (0, 0)),
                  pl.BlockSpec((1, HD), lambda b, n, i: (0, 0)),
                  pl.BlockSpec((1, HD), lambda b, n, i: (0, 0)),
                  pl.BlockSpec(memory_space=pl.ANY)],
        out_specs=pl.BlockSpec((GQA_TQ, gw), lambda b, n, i: (b * nq + i, (NA_HEADS * HD) // gw + n)),
        out_shape=jax.ShapeDtypeStruct((T_ALL, D), BF16),
        input_output_aliases={11: 0},
        scratch_shapes=[pltpu.VMEM((S, HD), BF16), pltpu.VMEM((L, HD), BF16)],
        compiler_params=_cp(("arbitrary",) * 3),
        name="gqa_attention",
    )(p, p, p, p, p, cosf, sinf, cosf, sinf, q_gain.reshape(1, HD), k_gain.reshape(1, HD), o_buf)


def _ctx_attn_kernel(p_ref, qg_ref, kg_ref, o_prev, o_ref):
    del o_prev

    def col(c):
        return p_ref[:, c * HD:(c + 1) * HD]

    for h in range(NA_HEADS):
        s = _dot_nt(col(h), col(NA_HEADS + h)) * ATT_SCALE
        o_ref[:, h * HD:(h + 1) * HD] = _softmax_av([(s, col(2 * NA_HEADS + h))]).astype(BF16)
    for n in range(GQA_KV_HEADS):
        kn = _rms_head(col(GQA_KCOL + n).astype(F32), kg_ref[...]).astype(BF16)
        v = col(GQA_VCOL + n)
        for g in range(GQA_GROUP):
            h = n * GQA_GROUP + g
            qn = _rms_head(col(GQA_QCOL + h).astype(F32), qg_ref[...]).astype(BF16)
            s = _dot_nt(qn, kn) * ATT_SCALE
            o_ref[:, (NA_HEADS + h) * HD:(NA_HEADS + h + 1) * HD] = _softmax_av([(s, v)]).astype(BF16)


def ctx_attention(p, q_gain, k_gain, o_buf):
    cb = S // L
    return pl.pallas_call(
        _ctx_attn_kernel,
        grid=(B,),
        in_specs=[pl.BlockSpec((L, ATTN_IN), lambda b: (B * cb + b, 0)),
                  pl.BlockSpec((1, HD), lambda b: (0, 0)),
                  pl.BlockSpec((1, HD), lambda b: (0, 0)),
                  pl.BlockSpec(memory_space=pl.ANY)],
        out_specs=pl.BlockSpec((L, D), lambda b: (B * cb + b, 0)),
        out_shape=jax.ShapeDtypeStruct((T_ALL, D), BF16),
        input_output_aliases={3: 0},
        compiler_params=_cp(("arbitrary",)),
        name="ctx_attention",
    )(p, q_gain.reshape(1, HD), k_gain.reshape(1, HD), o_buf)


def _log_sigmoid(x):
    return -(jnp.maximum(-x, 0.0) + jnp.log1p(jnp.exp(-jnp.abs(x))))


def _dot_hi(a, b):
    return jnp.dot(a, b, precision=lax.Precision.HIGHEST, preferred_element_type=F32)


def _gate_kernel(h_ref, wg_ref, wgt_ref, b_ref, bt_ref, lt_ref, ut_ref, col_ref, row_ref):
    nh = ML_HEADS
    hx = h_ref[...]
    g = jnp.dot(hx, wg_ref[...], preferred_element_type=F32) + b_ref[...]
    gt = _dot_nt(wgt_ref[...], hx) + bt_ref[...]
    li = g[:, 0:2 * nh]
    lf = _log_sigmoid(g[:, 2 * nh:4 * nh])
    lit = gt[0:2 * nh]
    lft = _log_sigmoid(gt[2 * nh:4 * nh])
    lt = lt_ref[...]
    ut = ut_ref[...]
    lane = lax.broadcasted_iota(I32, lf.shape, 1)
    bc = jnp.where(lane < nh, _dot_hi(lt, lf), _dot_hi(ut, lf))
    tot = jnp.sum(lf, axis=0, keepdims=True)
    aend = tot - bc + li
    col_ref[...] = jnp.concatenate([bc, aend, jnp.zeros((ML_CH, LANE - 4 * nh), F32)], axis=1)
    sub = lax.broadcasted_iota(I32, lft.shape, 0)
    bct = jnp.where(sub < nh, _dot_hi(lft, ut), _dot_hi(lft, lt))
    tott = jnp.sum(lft, axis=1, keepdims=True)
    gtr = lit - bct
    row_ref[0] = jnp.concatenate([bct, gtr, tott + gtr, jnp.broadcast_to(tott, bct.shape)], axis=0)


def mlstm_gates(hx, wg, gate_b):
    nh = ML_HEADS
    n_ch = T_ALL // ML_CH
    wg_pad = jnp.zeros((D, LANE), BF16).at[:, :4 * nh].set(wg.astype(BF16))
    b_pad = jnp.zeros((1, LANE), F32).at[0, :4 * nh].set(gate_b.reshape(-1))
    wgt = wg.astype(BF16).T
    bt = gate_b.reshape(4 * nh, 1).astype(F32)
    lt = jnp.tril(jnp.ones((ML_CH, ML_CH), F32))
    ut = jnp.triu(jnp.ones((ML_CH, ML_CH), F32))
    col, row = pl.pallas_call(
        _gate_kernel,
        grid=(n_ch,),
        in_specs=[pl.BlockSpec((ML_CH, D), lambda i: (i, 0)),
                  pl.BlockSpec((D, LANE), lambda i: (0, 0)),
                  pl.BlockSpec((4 * nh, D), lambda i: (0, 0)),
                  pl.BlockSpec((1, LANE), lambda i: (0, 0)),
                  pl.BlockSpec((4 * nh, 1), lambda i: (0, 0)),
                  pl.BlockSpec((ML_CH, ML_CH), lambda i: (0, 0)),
                  pl.BlockSpec((ML_CH, ML_CH), lambda i: (0, 0))],
        out_specs=[pl.BlockSpec((ML_CH, LANE), lambda i: (i, 0)),
                   pl.BlockSpec((1, 8 * nh, ML_CH), lambda i: (i, 0, 0))],
        out_shape=[jax.ShapeDtypeStruct((T_ALL, LANE), F32),
                   jax.ShapeDtypeStruct((n_ch, 8 * nh, ML_CH), F32)],
        compiler_params=_cp(("arbitrary",)),
        name="mlstm_gates",
    )(hx, wg_pad, wgt, b_pad, bt, lt, ut)
    return col, row


def _mlstm_step(d, hh, q_ref, k_ref, v_ref, col_ref, row_ref, o_ref, c_ref, n_ref, m_ref):
    nd = 2 * ML_HEADS
    sl = d * ML_HEADS + hh
    q = q_ref[:, hh * ML_QK:(hh + 1) * ML_QK]
    kf = k_ref[:, hh * ML_QK:(hh + 1) * ML_QK].astype(F32) * ML_KSCALE
    kb = kf.astype(BF16)
    v = v_ref[:, hh * ML_V:(hh + 1) * ML_V]
    bc_col = col_ref[:, sl:sl + 1]
    aend_col = col_ref[:, nd + sl:nd + sl + 1]
    g_row = row_ref[0, nd + sl:nd + sl + 1, :]
    aend_row = row_ref[0, 2 * nd + sl:2 * nd + sl + 1, :]
    btot = row_ref[0, 3 * nd + sl:3 * nd + sl + 1, 0:1]
    m_st = m_ref[sl]
    c_st = c_ref[sl]
    n_st = n_ref[sl]
    m_new = jnp.maximum(btot + m_st, jnp.max(aend_row, axis=1, keepdims=True))

    r = lax.broadcasted_iota(I32, (ML_CH, ML_CH), 0)
    c = lax.broadcasted_iota(I32, (ML_CH, ML_CH), 1)
    causal = (r >= c) if d == 0 else (r <= c)
    d_mat = jnp.where(causal, bc_col + g_row, -jnp.inf)
    m_row = jnp.maximum(bc_col + m_st, jnp.max(d_mat, axis=1, keepdims=True))
    w_inter = jnp.exp(bc_col + m_st - m_row)
    s_mat = _dot_nt(q, kb) * jnp.exp(d_mat - m_row)
    num = (w_inter * jnp.dot(q, c_st.astype(BF16), preferred_element_type=F32)
           + jnp.dot(s_mat.astype(BF16), v, preferred_element_type=F32))
    den = (w_inter * jnp.sum(q.astype(F32) * n_st, axis=1, keepdims=True)
           + jnp.sum(s_mat, axis=1, keepdims=True))
    o_ref[:, hh * ML_V:(hh + 1) * ML_V] = num / jnp.maximum(jnp.abs(den), jnp.exp(-m_row))

    w_end_col = jnp.exp(aend_col - m_new)
    w_end_row = jnp.exp(aend_row - m_new)
    decay = jnp.exp(btot + m_st - m_new)
    kw = (kf * w_end_col).astype(BF16)
    c_ref[sl] = decay * c_st + lax.dot_general(kw, v, (((0,), (0,)), ((), ())), preferred_element_type=F32)
    w8 = jnp.broadcast_to(w_end_row, (8, ML_CH)).astype(BF16)
    n_ref[sl] = decay * n_st + jnp.dot(w8, kb, preferred_element_type=F32)[0:1]
    m_ref[sl] = m_new


def _mlstm_kernel(qf, kf, vf, colf, rowf, qb, kb, vb, colb, rowb, of, ob, c_ref, n_ref, m_ref):
    @pl.when(pl.program_id(1) == 0)
    def _():
        c_ref[...] = jnp.zeros_like(c_ref)
        n_ref[...] = jnp.zeros_like(n_ref)
        m_ref[...] = jnp.zeros_like(m_ref)

    for hh in range(ML_HEADS):
        _mlstm_step(0, hh, qf, kf, vf, colf, rowf, of, c_ref, n_ref, m_ref)
        _mlstm_step(1, hh, qb, kb, vb, colb, rowb, ob, c_ref, n_ref, m_ref)


def mlstm_scan(p, col, row):
    n_lat = S // ML_CH
    steps = n_lat + 1
    lat_blocks = T_LAT // ML_CH
    qk_w = ML_HEADS * ML_QK
    v_w = ML_HEADS * ML_V
    n_chains = 2 * ML_HEADS

    def chunk(b, d, st):
        c = (st - 1) if d == 0 else (n_lat - st)
        return jnp.where(st == 0, lat_blocks + b, b * n_lat + c)

    def out_chunk(b, d, st):
        s1 = jnp.maximum(st, 1)
        return b * n_lat + ((s1 - 1) if d == 0 else (n_lat - s1))

    def dir_specs(d):
        return [pl.BlockSpec((ML_CH, qk_w), lambda b, s: (chunk(b, d, s), 0)),
                pl.BlockSpec((ML_CH, qk_w), lambda b, s: (chunk(b, d, s), 1)),
                pl.BlockSpec((ML_CH, v_w), lambda b, s: (chunk(b, d, s), (2 * qk_w) // v_w)),
                pl.BlockSpec((ML_CH, LANE), lambda b, s: (chunk(b, d, s), 0)),
                pl.BlockSpec((1, 4 * n_chains, ML_CH), lambda b, s: (chunk(b, d, s), 0, 0))]

    return pl.pallas_call(
        _mlstm_kernel,
        grid=(B, steps),
        in_specs=dir_specs(0) + dir_specs(1),
        out_specs=[pl.BlockSpec((ML_CH, v_w), lambda b, s: (out_chunk(b, 0, s), 0)),
                   pl.BlockSpec((ML_CH, v_w), lambda b, s: (out_chunk(b, 1, s), 0))],
        out_shape=[jax.ShapeDtypeStruct((T_LAT, v_w), F32), jax.ShapeDtypeStruct((T_LAT, v_w), F32)],
        scratch_shapes=[pltpu.VMEM((n_chains, ML_QK, ML_V), F32), pltpu.VMEM((n_chains, 1, ML_QK), F32),
                        pltpu.VMEM((n_chains, 1, 1), F32)],
        compiler_params=_cp(("arbitrary",) * 2),
        name="mlstm_scan",
    )(p, p, p, col, row, p, p, p, col, row)


def _readout_kernel(hf_ref, hb_ref, o_ref, g_ref, a_ref):
    hs = hf_ref[...] + hb_ref[...]
    for h in range(ML_HEADS):
        sl = slice(h * ML_V, (h + 1) * ML_V)
        x = hs[:, sl]
        hn = x * lax.rsqrt(jnp.mean(x * x, axis=-1, keepdims=True) + EPS) * g_ref[:, sl]
        a_ref[:, sl] = (hn * jax.nn.sigmoid(o_ref[:, sl].astype(F32))).astype(BF16)


def mlstm_readout(hdir, p, head_gain):
    tm = 256
    ocol = (2 * ML_HEADS * ML_QK + ML_HEADS * ML_V) // D
    return pl.pallas_call(
        _readout_kernel,
        grid=(T_LAT // tm,),
        in_specs=[pl.BlockSpec((tm, D), lambda i: (i, 0)),
                  pl.BlockSpec((tm, D), lambda i: (i, 0)),
                  pl.BlockSpec((tm, D), lambda i: (i, ocol)),
                  pl.BlockSpec((1, D), lambda i: (0, 0))],
        out_specs=pl.BlockSpec((tm, D), lambda i: (i, 0)),
        out_shape=jax.ShapeDtypeStruct((T_LAT, D), BF16),
        compiler_params=_cp(("arbitrary",)),
        name="mlstm_readout",
    )(hdir[0], hdir[1], p, head_gain.reshape(1, D))


ROUTER_TM = 512


def _router_kernel(h_ref, w_ref, rb_ref, erow_ref, tri_ref, eidx_ref, wts_ref, pos_ref, cnt_ref, carry_ref):
    ng = N_GROUPS
    epg = N_EXPERTS // N_GROUPS
    tm = ROUTER_TM
    ninf = -jnp.inf

    @pl.when(pl.program_id(0) == 0)
    def _():
        carry_ref[...] = jnp.zeros_like(carry_ref)

    s = jax.nn.sigmoid(_dot_nt(w_ref[...], h_ref[...]))
    ssel = s + rb_ref[...]
    sraw = [s[ng * j:ng * (j + 1)] for j in range(epg)]
    slab = [ssel[ng * j:ng * (j + 1)] for j in range(epg)]
    m1 = functools.reduce(jnp.maximum, slab)
    jfirst = functools.reduce(jnp.minimum, [jnp.where(slab[j] == m1, j, epg) for j in range(epg)])
    m2 = functools.reduce(jnp.maximum, [jnp.where(jfirst == j, ninf, slab[j]) for j in range(epg)])
    gs = m1 + m2
    giota = lax.broadcasted_iota(I32, (ng, tm), 0)
    gsel = jnp.zeros((ng, tm), F32)
    for _ in range(TOPK_GROUPS):
        mx = jnp.max(gs, axis=0, keepdims=True)
        gi = jnp.min(jnp.where(gs == mx, giota, ng), axis=0, keepdims=True)
        hit = giota == gi
        gsel = jnp.where(hit, 1.0, gsel)
        gs = jnp.where(hit, ninf, gs)
    msl = [jnp.where(gsel > 0.0, slab[j], ninf) for j in range(epg)]
    eid = [giota * epg + j for j in range(epg)]
    selm = [jnp.zeros((ng, tm), F32) for _ in range(epg)]
    e_list, w_list = [], []
    for _ in range(TOP_K):
        mx = jnp.max(functools.reduce(jnp.maximum, msl), axis=0, keepdims=True)
        cand = functools.reduce(jnp.minimum, [jnp.where(msl[j] == mx, eid[j], N_EXPERTS) for j in range(epg)])
        esel = jnp.min(cand, axis=0, keepdims=True)
        hits = [eid[j] == esel for j in range(epg)]
        wk = functools.reduce(lambda a, b: a + b, [jnp.where(hits[j], sraw[j], 0.0) for j in range(epg)])
        w_list.append(jnp.sum(wk, axis=0, keepdims=True))
        e_list.append(esel)
        msl = [jnp.where(hits[j], ninf, msl[j]) for j in range(epg)]
        selm = [jnp.where(hits[j], 1.0, selm[j]) for j in range(epg)]
    wsum = functools.reduce(lambda a, b: a + b, w_list)
    wts_ref[...] = jnp.concatenate([w / wsum * ROUTED_SCALE for w in w_list], axis=0)
    eidx_ref[...] = jnp.concatenate(e_list, axis=0)
    sel = jnp.concatenate(selm, axis=0)
    carry = carry_ref[...]
    posfull = jnp.dot(sel.astype(BF16), tri_ref[...], preferred_element_type=F32) + carry
    erow = erow_ref[...]
    pos = [jnp.sum(jnp.where(erow == e, posfull, 0.0), axis=0, keepdims=True) for e in e_list]
    pos_ref[...] = jnp.concatenate(pos, axis=0).astype(I32)
    carry = carry + jnp.sum(sel, axis=1, keepdims=True)
    carry_ref[...] = carry
    cnt_ref[...] = carry


def moe_router(hx, router_w, router_b, n_tok):
    tm = ROUTER_TM
    epg = N_EXPERTS // N_GROUPS
    perm = (jnp.arange(N_EXPERTS) % N_GROUPS) * epg + jnp.arange(N_EXPERTS) // N_GROUPS
    w_t = router_w.astype(BF16).T[perm]
    rb = router_b.astype(F32)[perm].reshape(N_EXPERTS, 1)
    erow = perm.astype(I32).reshape(N_EXPERTS, 1)
    tri = jnp.triu(jnp.ones((tm, tm), BF16), 1)
    eidx, wts, pos, counts = pl.pallas_call(
        _router_kernel,
        grid=(n_tok // tm,),
        in_specs=[pl.BlockSpec((tm, D), lambda i: (i, 0)),
                  pl.BlockSpec((N_EXPERTS, D), lambda i: (0, 0)),
                  pl.BlockSpec((N_EXPERTS, 1), lambda i: (0, 0)),
                  pl.BlockSpec((N_EXPERTS, 1), lambda i: (0, 0)),
                  pl.BlockSpec((tm, tm), lambda i: (0, 0))],
        out_specs=[pl.BlockSpec((TOP_K, tm), lambda i: (0, i)),
                   pl.BlockSpec((TOP_K, tm), lambda i: (0, i)),
                   pl.BlockSpec((TOP_K, tm), lambda i: (0, i)),
                   pl.BlockSpec((N_EXPERTS, 1), lambda i: (0, 0))],
        out_shape=[jax.ShapeDtypeStruct((TOP_K, n_tok), I32),
                   jax.ShapeDtypeStruct((TOP_K, n_tok), F32),
                   jax.ShapeDtypeStruct((TOP_K, n_tok), I32),
                   jax.ShapeDtypeStruct((N_EXPERTS, 1), F32)],
        scratch_shapes=[pltpu.VMEM((N_EXPERTS, 1), F32)],
        compiler_params=_cp(("arbitrary",)),
        name="moe_router",
    )(hx, w_t, rb, erow, tri)
    return eidx, wts, pos, counts.reshape(N_EXPERTS)[perm]


DISPATCH_TM = 512


ROW_UNROLL = 8


def _dispatch_kernel(slot_ref, hx_ref, xs_hbm, sem):
    def issue(tt, carry):
        t8 = pl.multiple_of(tt * ROW_UNROLL, ROW_UNROLL)
        for j in range(ROW_UNROLL):
            src = hx_ref.at[pl.ds(pl.multiple_of((t8 + j) * PK_S, PK_S), PK_S), :]
            for k in range(TOP_K):
                row = pl.multiple_of(slot_ref[k * DISPATCH_TM + j + t8] * PK_S, PK_S)
                pltpu.make_async_copy(src, xs_hbm.at[pl.ds(row, PK_S), :], sem).start(priority=k % 2)
        return carry

    lax.fori_loop(0, DISPATCH_TM // ROW_UNROLL, issue, 0)
    for _ in range(TOP_K):
        pltpu.make_async_copy(hx_ref, xs_hbm.at[pl.ds(0, DISPATCH_TM * PK_S), :], sem).wait()


def moe_dispatch(slots, hx_packed, n_tok, n_rows):
    tm = DISPATCH_TM
    return pl.pallas_call(
        _dispatch_kernel,
        grid=(n_tok // tm,),
        in_specs=[pl.BlockSpec((TOP_K * tm,), lambda i: (i,), memory_space=pltpu.SMEM),
                  pl.BlockSpec((tm * PK_S, LANE), lambda i: (i, 0))],
        out_specs=pl.BlockSpec(memory_space=pl.ANY),
        out_shape=jax.ShapeDtypeStruct((n_rows * PK_S, LANE), U32),
        scratch_shapes=[pltpu.SemaphoreType.DMA(())],
        compiler_params=_cp(("arbitrary",)),
        name="moe_dispatch",
    )(slots, hx_packed)


def _expert_kernel(be_ref, valid_ref, nused_ref, first_ref, next_ref, slot_ref,
                   xs_ref, w1_hbm, w3_hbm, w2_hbm, y_ref, w1s, w3s, w2s, w1b, w3b, w2b, xb, sems, *, layer):
    i = pl.program_id(0)
    bm = EXP_BM

    def weight_copies(e, s):
        return [pltpu.make_async_copy(w_hbm.at[layer, e], stage.at[s], sems.at[s, j])
                for j, (w_hbm, stage) in enumerate(((w1_hbm, w1s), (w3_hbm, w3s), (w2_hbm, w2s)))]

    @pl.when(i < nused_ref[0])
    def _():
        @pl.when(first_ref[i] == 1)
        def _():
            s = slot_ref[i]

            @pl.when(i == 0)
            def _():
                for cp in weight_copies(be_ref[0], 0):
                    cp.start()

            for cp in weight_copies(be_ref[i], s):
                cp.wait()

            @pl.when(next_ref[i] >= 0)
            def _():
                for cp in weight_copies(next_ref[i], 1 - s):
                    cp.start()

            w1b[...] = w1s[s].astype(BF16)
            w3b[...] = w3s[s].astype(BF16)
            w2b[...] = w2s[s].astype(BF16)

        live = lax.broadcasted_iota(I32, (bm, LANE), 0) < valid_ref[i]
        for s, piece in enumerate(_load_row_tiles(xs_ref, 0, bm)):
            lo, hi = _unpack_bf16_pairs(jnp.where(live, piece, jnp.uint32(0)))
            xb[:, s * LANE:(s + 1) * LANE] = lo.astype(BF16)
            xb[:, PK_W + s * LANE:PK_W + (s + 1) * LANE] = hi.astype(BF16)
        x = xb[...]
        h1 = jnp.dot(x, w1b[...], preferred_element_type=F32)
        h3 = jnp.dot(x, w3b[...], preferred_element_type=F32)
        a = (h1 * jax.nn.sigmoid(h1) * h3).astype(BF16)
        y = jnp.dot(a, w2b[...], preferred_element_type=F32)
        _store_row_tiles(y_ref, _pack_bf16_pairs(y.astype(BF16)))


def moe_experts(block_e, valid, n_used, first, next_e, slot, xs, w1, w3, w2, layer, n_blocks):
    bm = EXP_BM

    def blk(i, be, va, nu, fi, ne, sl):
        return (jnp.minimum(i, nu[0] - 1), 0)

    grid_spec = pltpu.PrefetchScalarGridSpec(
        num_scalar_prefetch=6,
        grid=(n_blocks,),
        in_specs=[pl.BlockSpec((bm * PK_S, LANE), blk),
                  pl.BlockSpec(memory_space=pl.ANY),
                  pl.BlockSpec(memory_space=pl.ANY),
                  pl.BlockSpec(memory_space=pl.ANY)],
        out_specs=pl.BlockSpec((bm * PK_S, LANE), blk),
        scratch_shapes=[pltpu.VMEM((2, D, EXPERT_DIM), F32), pltpu.VMEM((2, D, EXPERT_DIM), F32),
                        pltpu.VMEM((2, EXPERT_DIM, D), F32),
                        pltpu.VMEM((D, EXPERT_DIM), BF16), pltpu.VMEM((D, EXPERT_DIM), BF16),
                        pltpu.VMEM((EXPERT_DIM, D), BF16), pltpu.VMEM((bm, D), BF16),
                        pltpu.SemaphoreType.DMA((2, 3))],
    )
    return pl.pallas_call(
        functools.partial(_expert_kernel, layer=layer),
        grid_spec=grid_spec,
        out_shape=jax.ShapeDtypeStruct((n_blocks * bm * PK_S, LANE), U32),
        compiler_params=_cp(("arbitrary",), vmem=56 * 1024 * 1024),
        name="moe_experts",
    )(block_e, valid, n_used, first, next_e, slot, xs, w1, w3, w2)


def _shared_kernel(x_ref, w1_ref, w3_ref, w2_ref, o_ref):
    x = x_ref[...]
    h1 = jnp.dot(x, w1_ref[...], preferred_element_type=F32)
    h3 = jnp.dot(x, w3_ref[...], preferred_element_type=F32)
    a = (h1 * jax.nn.sigmoid(h1) * h3).astype(BF16)
    o_ref[...] = jnp.dot(a, w2_ref[...], preferred_element_type=F32)


def shared_expert(hx, w1, w3, w2, n_tok):
    tm = 512
    return pl.pallas_call(
        _shared_kernel,
        grid=(n_tok // tm,),
        in_specs=[pl.BlockSpec((tm, D), lambda i: (i, 0)),
                  pl.BlockSpec((D, EXPERT_DIM), lambda i: (0, 0)),
                  pl.BlockSpec((D, EXPERT_DIM), lambda i: (0, 0)),
                  pl.BlockSpec((EXPERT_DIM, D), lambda i: (0, 0))],
        out_specs=pl.BlockSpec((tm, D), lambda i: (i, 0)),
        out_shape=jax.ShapeDtypeStruct((n_tok, D), F32),
        compiler_params=_cp(("arbitrary",)),
        name="shared_expert",
    )(hx, w1, w3, w2)


COMBINE_TM = 128


def _combine_kernel(slot0_ref, slotn_ref, w_ref, sh_ref, x_ref, mod_ref, y_hbm, *rest, final, n_tiles):
    fg_ref = rest[0] if final else None
    o_ref, buf, sems = rest[-3:]
    tm = COMBINE_TM
    i = pl.program_id(0)
    half_rows = TOP_K * tm * PK_S

    def start_gathers(slot_ref, par):
        half = buf.at[par]
        sem = sems.at[par]

        def issue(tt, carry):
            t8 = pl.multiple_of(tt * ROW_UNROLL, ROW_UNROLL)
            for j in range(ROW_UNROLL):
                for k in range(TOP_K):
                    row = pl.multiple_of(slot_ref[k * tm + j + t8] * PK_S, PK_S)
                    dst = half.at[pl.ds(pl.multiple_of((k * tm + j + t8) * PK_S, PK_S), PK_S), :]
                    pltpu.make_async_copy(y_hbm.at[pl.ds(row, PK_S), :], dst, sem).start(priority=k % 2)
            return carry

        lax.fori_loop(0, tm // ROW_UNROLL, issue, 0)

    @pl.when(i == 0)
    def _():
        start_gathers(slot0_ref, 0)

    @pl.when(i + 1 < n_tiles)
    def _():
        start_gathers(slotn_ref, (i + 1) % 2)

    cur = buf.at[i % 2]
    pltpu.make_async_copy(y_hbm.at[pl.ds(0, half_rows), :], cur, sems.at[i % 2]).wait()

    w = w_ref[...]
    gate = mod_ref[0][5:6]
    acc = [None] * (2 * PK_S)
    for k in range(TOP_K):
        wk = w[:, k:k + 1]
        for s, piece in enumerate(_load_row_tiles(cur, k * tm * PK_S, tm)):
            for c, val in zip((s, PK_S + s), _unpack_bf16_pairs(piece)):
                acc[c] = wk * val if acc[c] is None else acc[c] + wk * val
    outs = []
    for c in range(2 * PK_S):
        sl = slice(c * LANE, (c + 1) * LANE)
        outs.append(x_ref[:, sl] + gate[:, sl] * (sh_ref[:, sl] + acc[c]))
    if fg_ref is not None:
        ssq = functools.reduce(lambda a, b: a + b, [jnp.sum(o * o, axis=-1, keepdims=True) for o in outs])
        inv = lax.rsqrt(ssq / D + EPS)
        outs = [o * inv * fg_ref[:, c * LANE:(c + 1) * LANE] for c, o in enumerate(outs)]
    for c, o in enumerate(outs):
        o_ref[:, c * LANE:(c + 1) * LANE] = o


def moe_combine(slots, wts_tok, shared, x, mod, y, n_tok, final_gain=None):
    tm = COMBINE_TM
    n_tiles = n_tok // tm
    final = final_gain is not None
    in_specs = [pl.BlockSpec((TOP_K * tm,), lambda i: (0,), memory_space=pltpu.SMEM),
                pl.BlockSpec((TOP_K * tm,), lambda i: (jnp.minimum(i + 1, n_tiles - 1),), memory_space=pltpu.SMEM),
                pl.BlockSpec((tm, TOP_K), lambda i: (i, 0)),
                pl.BlockSpec((tm, D), lambda i: (i, 0)),
                pl.BlockSpec((tm, D), lambda i: (i, 0)),
                pl.BlockSpec((1, 6, D), lambda i: (_mod_row(i * tm), 0, 0)),
                pl.BlockSpec(memory_space=pl.ANY)]
    args = [slots, slots, wts_tok, shared, x, mod, y]
    if final:
        in_specs.append(pl.BlockSpec((1, D), lambda i: (0, 0)))
        args.append(final_gain.reshape(1, D))
    return pl.pallas_call(
        functools.partial(_combine_kernel, final=final, n_tiles=n_tiles),
        grid=(n_tiles,),
        in_specs=in_specs,
        out_specs=pl.BlockSpec((tm, D), lambda i: (i, 0)),
        out_shape=jax.ShapeDtypeStruct((n_tok, D), F32),
        scratch_shapes=[pltpu.VMEM((2, TOP_K * tm * PK_S, LANE), U32), pltpu.SemaphoreType.DMA((2,))],
        compiler_params=_cp(("arbitrary",)),
        name="moe_combine",
    )(*args)


def _lookup(table, idx):
    e = jnp.arange(table.shape[0], dtype=I32).reshape((-1,) + (1,) * idx.ndim)
    return jnp.sum(jnp.where(idx[None] == e, table.reshape(e.shape), 0), axis=0)


def _tile_flat(slots, tm):
    k, t = slots.shape
    return slots.reshape(k, t // tm, tm).transpose(1, 0, 2).reshape(-1)


def moe_layer(x, mod, norm_gain, router_w, router_b, exp_w1, exp_w3, exp_w2, sw1, sw3, sw2, layer, n_tok,
              final_gain=None):
    bm = EXP_BM
    n_blocks = -(-n_tok * TOP_K // bm) + N_EXPERTS
    hx, hx_packed = norm_mod(x, norm_gain, mod, 3, n_tok, pack=True)
    eidx, wts, pos, counts = moe_router(hx, router_w, router_b, n_tok)
    shared = shared_expert(hx, sw1.astype(BF16), sw3.astype(BF16), sw2.astype(BF16), n_tok)
    cnt = counts.astype(I32)
    padded = (cnt + bm - 1) // bm * bm
    pad_end = jnp.cumsum(padded)
    pad_start = pad_end - padded
    slots = _lookup(pad_start, eidx) + pos
    blk_row = jnp.arange(n_blocks, dtype=I32) * bm
    block_e = jnp.minimum(jnp.sum((pad_end[:, None] <= blk_row[None, :]).astype(I32), axis=0), N_EXPERTS - 1)
    valid = jnp.clip(_lookup(cnt, block_e) - (blk_row - _lookup(pad_start, block_e)), 0, bm).astype(I32)
    n_used = (pad_end[-1:] // bm).astype(I32)
    prev_e = jnp.concatenate([jnp.full((1,), -1, I32), block_e[:-1]])
    first = ((blk_row < pad_end[-1]) & (block_e != prev_e)).astype(I32)
    stage_slot = ((jnp.cumsum(first) - 1) % 2).astype(I32)
    eids = jnp.arange(N_EXPERTS, dtype=I32)
    later = jnp.where((eids[None, :] > eids[:, None]) & (padded[None, :] > 0), eids[None, :], N_EXPERTS)
    next_nonempty = jnp.min(later, axis=1)
    next_nonempty = jnp.where(next_nonempty == N_EXPERTS, -1, next_nonempty)
    next_e = _lookup(next_nonempty, block_e).astype(I32)
    xs = moe_dispatch(_tile_flat(slots, DISPATCH_TM), hx_packed, n_tok, n_blocks * bm)
    y = moe_experts(block_e, valid, n_used, first, next_e, stage_slot, xs, exp_w1, exp_w3, exp_w2, layer, n_blocks)
    return moe_combine(_tile_flat(slots, COMBINE_TM), wts.T, shared, x, mod, y, n_tok, final_gain)


def _rope_tables():
    t = jnp.arange(S, dtype=I32)
    row = (t // GRID_W).astype(F32)
    col = (t % GRID_W).astype(F32)
    n_freq = HD // 4
    inv_freq = ROPE_THETA ** (-jnp.arange(n_freq, dtype=F32) / n_freq)
    ang = jnp.concatenate([row[:, None] * inv_freq, col[:, None] * inv_freq], axis=-1)
    cosf = jnp.repeat(jnp.cos(ang), 2, axis=-1)
    sinf = jnp.stack([-jnp.sin(ang), jnp.sin(ang)], axis=-1).reshape(S, HD)
    return cosf, sinf


def kernel(x, c, ctx, c_ctx, ada_w, ada_b, norm_mix, norm_ffn, attn_w_in, attn_w_out, attn_rpb, attn_q_gain,
           attn_k_gain, ml_w_in, ml_w_out, ml_gate_b, ml_head_gain, router_w, router_b, exp_w1, exp_w3, exp_w2,
           sh_w1, sh_w3, sh_w2, final_norm_gain):
    depth = ada_w.shape[0]
    x_lat = x.reshape(T_LAT, D)
    x_ctx = ctx.reshape(T_CTX, D)
    cvec = jnp.concatenate([c, c_ctx[None], jnp.zeros((8 - B - 1, D), F32)], axis=0)
    mod_all = ada_ln(cvec, ada_w, ada_b).reshape(depth, 8, 6, D)
    cosf, sinf = _rope_tables()

    mod = mod_all[0]
    p = norm_matmul(x_lat, norm_mix[0], mod, attn_w_in[0].astype(BF16), emit_h=False, x_ctx=x_ctx)
    o_all = neighborhood_attention(p, na_bias_table(attn_rpb[0]))
    o_all = gqa_attention(p, cosf, sinf, attn_q_gain[0], attn_k_gain[0], o_all)
    o_all = ctx_attention(p, attn_q_gain[0], attn_k_gain[0], o_all)
    xa = matmul_gated_residual(o_all, attn_w_out[0].astype(BF16), x_lat, mod, 2, x_ctx=x_ctx)
    xa = moe_layer(xa, mod, norm_ffn[0], router_w[0], router_b[0], exp_w1, exp_w3, exp_w2,
                   sh_w1[0], sh_w3[0], sh_w2[0], 0, T_ALL)

    mod = mod_all[1]
    w_in = ml_w_in[0]
    p, hx = norm_matmul(xa, norm_mix[1], mod, w_in[:, :ML_MAIN].astype(BF16), emit_h=True)
    col, row = mlstm_gates(hx, w_in[:, ML_MAIN:], ml_gate_b[0])
    hdir = mlstm_scan(p, col, row)
    a = mlstm_readout(hdir, p, ml_head_gain[0])
    xl = matmul_gated_residual(a, ml_w_out[0].astype(BF16), xa, mod, 2)
    xl = moe_layer(xl, mod, norm_ffn[1], router_w[1], router_b[1], exp_w1, exp_w3, exp_w2,
                   sh_w1[1], sh_w3[1], sh_w2[1], 1, T_LAT, final_gain=final_norm_gain)
    return xl.reshape(B, S, D)
```

```python
import functools

import jax
import jax.numpy as jnp
from jax import lax
from jax.experimental import pallas as pl
from jax.experimental.pallas import tpu as pltpu

F32 = jnp.float32
BF16 = jnp.bfloat16
I32 = jnp.int32
U32 = jnp.uint32

D = 2048
B = 4
S = 4096
L = 256
T_LAT = B * S
T_CTX = B * L
T_ALL = T_LAT + T_CTX
GRID_W = 64
ROWS = S // GRID_W
HD = 128
NA_HEADS = 8
NA_WIN_ROWS = 8
NA_WIN_COLS = 16
GQA_Q_HEADS = 8
GQA_KV_HEADS = 2
GQA_GROUP = 4
ROPE_THETA = 10000.0
ATTN_IN = 4608
ML_HEADS = 8
ML_V = 256
ML_QK = 128
ML_MAIN = 6144
N_EXPERTS = 64
TOP_K = 8
N_GROUPS = 8
TOPK_GROUPS = 4
EXPERT_DIM = 512
ROUTED_SCALE = 2.5
EPS = 1e-6
NEG_INF = -1e30
ATT_SCALE = HD ** -0.5
LOG2E = 1.4426950408889634
ML_KSCALE = ML_QK ** -0.5

LANE = 128
NA_QROWS = 4
NA_SLAB = NA_QROWS + NA_WIN_ROWS - 1
NA_QB = NA_QROWS * GRID_W
NA_KB = NA_SLAB * GRID_W
ML_CH = 256
EXP_BM = 512
PK_W = D // 2
PK_S = PK_W // LANE
VMEM_LIMIT = 48 * 1024 * 1024


def _cp(sem, vmem=VMEM_LIMIT):
    return pltpu.CompilerParams(dimension_semantics=sem, vmem_limit_bytes=vmem)


def _pack_bf16_pairs(xb):
    u = pltpu.bitcast(xb.astype(F32), U32)
    return (u[:, PK_W:] & jnp.uint32(0xFFFF0000)) | (u[:, :PK_W] >> 16)


def _unpack_bf16_pairs(u):
    return pltpu.bitcast(u << 16, F32), pltpu.bitcast(u & jnp.uint32(0xFFFF0000), F32)


def _store_row_tiles(ref, words):
    rows = words.shape[0]
    for s in range(PK_S):
        ref[pl.ds(s, rows, stride=PK_S), :] = words[:, s * LANE:(s + 1) * LANE]


def _load_row_tiles(ref, start, rows):
    return [ref[pl.ds(start + s, rows, stride=PK_S), :] for s in range(PK_S)]


def _mod_row(start_row):
    return jnp.where(start_row < T_LAT, start_row // S, B)


def _ada_kernel(c_ref, w_ref, b_ref, o_ref):
    c = c_ref[...]
    a = (c * jax.nn.sigmoid(c)).astype(BF16)
    w = w_ref[0].astype(BF16)
    o_ref[0] = jnp.dot(a, w, preferred_element_type=F32) + b_ref[0]


def ada_ln(cvec, ada_w, ada_b):
    depth = ada_w.shape[0]
    n = ada_w.shape[2]
    tn = 1024
    return pl.pallas_call(
        _ada_kernel,
        grid=(depth, n // tn),
        in_specs=[pl.BlockSpec((8, D), lambda l, j: (0, 0)),
                  pl.BlockSpec((1, D, tn), lambda l, j: (l, 0, j)),
                  pl.BlockSpec((1, 1, tn), lambda l, j: (l, 0, j))],
        out_specs=pl.BlockSpec((1, 8, tn), lambda l, j: (l, 0, j)),
        out_shape=jax.ShapeDtypeStruct((depth, 8, n), F32),
        compiler_params=_cp(("arbitrary", "arbitrary")),
        name="ada_ln",
    )(cvec, ada_w, ada_b.reshape(depth, 1, n))


def _norm_mod_kernel(x_ref, g_ref, mod_ref, *out_refs, base, pack):
    x = x_ref[...]
    y = x * lax.rsqrt(jnp.mean(x * x, axis=-1, keepdims=True) + EPS) * g_ref[...]
    m = mod_ref[0]
    h = y * (1.0 + m[base + 1:base + 2]) + m[base:base + 1]
    hb = h.astype(BF16)
    out_refs[0][...] = hb
    if pack:
        _store_row_tiles(out_refs[1], _pack_bf16_pairs(hb))


def norm_mod(x, gain, mod, base, n_rows, pack):
    tm = 256
    out_shape = [jax.ShapeDtypeStruct((n_rows, D), BF16)]
    out_specs = [pl.BlockSpec((tm, D), lambda i: (i, 0))]
    if pack:
        out_shape.append(jax.ShapeDtypeStruct((n_rows * PK_S, LANE), U32))
        out_specs.append(pl.BlockSpec((tm * PK_S, LANE), lambda i: (i, 0)))
    res = pl.pallas_call(
        functools.partial(_norm_mod_kernel, base=base, pack=pack),
        grid=(n_rows // tm,),
        in_specs=[pl.BlockSpec((tm, D), lambda i: (i, 0)),
                  pl.BlockSpec((1, D), lambda i: (0, 0)),
                  pl.BlockSpec((1, 6, D), lambda i: (_mod_row(i * tm), 0, 0))],
        out_specs=out_specs,
        out_shape=out_shape,
        compiler_params=_cp(("arbitrary",)),
        name="norm_mod",
    )(x, gain.reshape(1, D), mod)
    return res if pack else res[0]


def _norm_mm_kernel(*refs, emit_h, n_lat):
    if n_lat is None:
        x_ref, g_ref, mod_ref, w_ref, o_ref = refs[:5]
        c_ref = None
    else:
        x_ref, c_ref, g_ref, mod_ref, w_ref, o_ref = refs[:6]
    hb_ref = refs[-1]

    def prologue(src_ref):
        x = src_ref[...]
        y = x * lax.rsqrt(jnp.mean(x * x, axis=-1, keepdims=True) + EPS) * g_ref[...]
        m = mod_ref[0]
        hb_ref[...] = (y * (1.0 + m[1:2]) + m[0:1]).astype(BF16)
        if emit_h:
            refs[-2][...] = hb_ref[...]

    first_col = pl.program_id(1) == 0
    if c_ref is None:
        pl.when(first_col)(lambda: prologue(x_ref))
    else:
        is_lat = pl.program_id(0) < n_lat
        pl.when(first_col & is_lat)(lambda: prologue(x_ref))
        pl.when(first_col & jnp.logical_not(is_lat))(lambda: prologue(c_ref))

    o_ref[...] = jnp.dot(hb_ref[...], w_ref[...], preferred_element_type=F32).astype(o_ref.dtype)


def norm_matmul(x, gain, mod, w, emit_h, x_ctx=None, tm=1024, tn=512):
    split = x_ctx is not None
    m = x.shape[0] + (x_ctx.shape[0] if split else 0)
    n = w.shape[1]
    n_lat = x.shape[0] // tm if split else None
    out_shape = [jax.ShapeDtypeStruct((m, n), BF16)]
    out_specs = [pl.BlockSpec((tm, tn), lambda i, j: (i, j))]
    if emit_h:
        out_shape.append(jax.ShapeDtypeStruct((m, D), BF16))
        out_specs.append(pl.BlockSpec((tm, D), lambda i, j: (i, 0)))
    if split:
        x_specs = [pl.BlockSpec((tm, D), lambda i, j: (jnp.minimum(i, n_lat - 1), 0)),
                   pl.BlockSpec((tm, D), lambda i, j: (0, 0))]
        x_args = [x, x_ctx]
    else:
        x_specs = [pl.BlockSpec((tm, D), lambda i, j: (i, 0))]
        x_args = [x]
    res = pl.pallas_call(
        functools.partial(_norm_mm_kernel, emit_h=emit_h, n_lat=n_lat),
        grid=(m // tm, n // tn),
        in_specs=x_specs + [pl.BlockSpec((1, D), lambda i, j: (0, 0)),
                            pl.BlockSpec((1, 6, D), lambda i, j: (_mod_row(i * tm), 0, 0)),
                            pl.BlockSpec((D, tn), lambda i, j: (0, j))],
        out_specs=out_specs,
        out_shape=out_shape,
        scratch_shapes=[pltpu.VMEM((tm, D), BF16)],
        compiler_params=_cp(("arbitrary", "arbitrary"), vmem=56 * 1024 * 1024),
        name="norm_matmul",
    )(*x_args, gain.reshape(1, D), mod, w)
    return res if emit_h else res[0]


def _mm_res_kernel(*refs, slot, n_lat):
    if n_lat is None:
        a_ref, w_ref, x_ref, mod_ref, o_ref = refs
        res = x_ref[...]
    else:
        a_ref, w_ref, x_ref, c_ref, mod_ref, o_ref = refs
        res = jnp.where(pl.program_id(0) < n_lat, x_ref[...], c_ref[...])
    acc = jnp.dot(a_ref[...], w_ref[...], preferred_element_type=F32)
    o_ref[...] = res + mod_ref[0][slot:slot + 1] * acc


def matmul_gated_residual(a, w, x, mod, slot, x_ctx=None, tm=1024, tn=512):
    m, k = a.shape
    n = w.shape[1]
    split = x_ctx is not None
    n_lat = x.shape[0] // tm if split else None
    if split:
        x_specs = [pl.BlockSpec((tm, tn), lambda i, j: (jnp.minimum(i, n_lat - 1), j)),
                   pl.BlockSpec((tm, tn), lambda i, j: (0, j))]
        x_args = [x, x_ctx]
    else:
        x_specs = [pl.BlockSpec((tm, tn), lambda i, j: (i, j))]
        x_args = [x]
    return pl.pallas_call(
        functools.partial(_mm_res_kernel, slot=slot, n_lat=n_lat),
        grid=(m // tm, n // tn),
        in_specs=[pl.BlockSpec((tm, k), lambda i, j: (i, 0)),
                  pl.BlockSpec((k, tn), lambda i, j: (0, j))] + x_specs
                 + [pl.BlockSpec((1, 6, tn), lambda i, j: (_mod_row(i * tm), 0, j))],
        out_specs=pl.BlockSpec((tm, tn), lambda i, j: (i, j)),
        out_shape=jax.ShapeDtypeStruct((m, n), F32),
        compiler_params=_cp(("arbitrary", "arbitrary")),
        name="matmul_gated_residual",
    )(a, w, *x_args, mod)


def _dot_nt(a, b):
    return lax.dot_general(a, b, (((1,), (1,)), ((), ())), preferred_element_type=F32)


def _rms_head(x, gain):
    return x * lax.rsqrt(jnp.mean(x * x, axis=-1, keepdims=True) + EPS) * gain


def _rope(x, cosf, sinf):
    lane = lax.broadcasted_iota(I32, x.shape, 1)
    nxt = pltpu.roll(x, LANE - 1, 1)
    prv = pltpu.roll(x, 1, 1)
    return x * cosf + jnp.where((lane & 1) == 0, nxt, prv) * sinf


def _softmax_av(parts):
    m = functools.reduce(jnp.maximum, [jnp.max(s, axis=-1, keepdims=True) for s, _ in parts])
    l = None
    o = None
    for s, v in parts:
        p = jnp.exp(s - m)
        li = jnp.sum(p, axis=-1, keepdims=True)
        oi = jnp.dot(p.astype(BF16), v, preferred_element_type=F32)
        l = li if l is None else l + li
        o = oi if o is None else o + oi
    return o / l


def _na_kernel(q_ref, k_ref, v_ref, kc_ref, vc_ref, tab_ref, o_ref):
    kc = kc_ref[...]
    vc = vc_ref[...]
    n_blocks = ROWS // NA_QROWS

    def body(j, carry):
        ks = jnp.clip(j * NA_QROWS - NA_WIN_ROWS // 2, 0, ROWS - NA_SLAB)
        typ = jnp.where(j == 0, 0, jnp.where(j == n_blocks - 1, 2, 1))
        qs = pl.multiple_of(j * NA_QB, NA_QB)
        kst = pl.multiple_of(ks * GRID_W, GRID_W)
        q = q_ref[pl.ds(qs, NA_QB), :]
        k = k_ref[pl.ds(kst, NA_KB), :]
        v = v_ref[pl.ds(kst, NA_KB), :]
        s_win = _dot_nt(q, k) * ATT_SCALE + tab_ref[typ, 0]
        s_ctx = _dot_nt(q, kc) * ATT_SCALE
        o_ref[pl.ds(qs, NA_QB), :] = _softmax_av([(s_win, v), (s_ctx, vc)]).astype(BF16)
        return carry

    lax.fori_loop(0, n_blocks, body, 0)


def na_bias_table(rpb):
    def one(r0, ks):
        r = r0 + jnp.arange(NA_QROWS)
        kr = ks + jnp.arange(NA_SLAB)
        start = jnp.clip(r - NA_WIN_ROWS // 2, 0, ROWS - NA_WIN_ROWS)
        row_ok = (kr[None, :] >= start[:, None]) & (kr[None, :] < start[:, None] + NA_WIN_ROWS)
        row_idx = jnp.clip(kr[None, :] - r[:, None] + NA_WIN_ROWS - 1, 0, 2 * NA_WIN_ROWS - 2)
        cq = jnp.arange(GRID_W)
        col_start = jnp.clip(cq - NA_WIN_COLS // 2, 0, GRID_W - NA_WIN_COLS)
        col_ok = (cq[None, :] >= col_start[:, None]) & (cq[None, :] < col_start[:, None] + NA_WIN_COLS)
        col_idx = jnp.clip(cq[None, :] - cq[:, None] + NA_WIN_COLS - 1, 0, 2 * NA_WIN_COLS - 2)
        r_hot = jax.nn.one_hot(row_idx, 2 * NA_WIN_ROWS - 1, dtype=F32)
        c_hot = jax.nn.one_hot(col_idx, 2 * NA_WIN_COLS - 1, dtype=F32)
        bias = jnp.einsum('qka,hab,xyb->hqxky', r_hot, rpb.astype(F32), c_hot, precision=lax.Precision.HIGHEST)
        ok = row_ok[:, None, :, None] & col_ok[None, :, None, :]
        return jnp.where(ok[None], bias, NEG_INF).reshape(NA_HEADS, NA_QB, NA_KB)

    mid = 2 * NA_QROWS
    last = ROWS - NA_QROWS
    return jnp.stack([one(0, 0), one(mid, mid - NA_WIN_ROWS // 2), one(last, ROWS - NA_SLAB)])


def neighborhood_attention(p, table):
    cb = S // L
    return pl.pallas_call(
        _na_kernel,
        grid=(NA_HEADS, B),
        in_specs=[pl.BlockSpec((S, HD), lambda h, b: (b, h)),
                  pl.BlockSpec((S, HD), lambda h, b: (b, NA_HEADS + h)),
                  pl.BlockSpec((S, HD), lambda h, b: (b, 2 * NA_HEADS + h)),
                  pl.BlockSpec((L, HD), lambda h, b: (B * cb + b, NA_HEADS + h)),
                  pl.BlockSpec((L, HD), lambda h, b: (B * cb + b, 2 * NA_HEADS + h)),
                  pl.BlockSpec((3, 1, NA_QB, NA_KB), lambda h, b: (0, h, 0, 0))],
        out_specs=pl.BlockSpec((S, HD), lambda h, b: (b, h)),
        out_shape=jax.ShapeDtypeStruct((T_ALL, D), BF16),
        compiler_params=_cp(("arbitrary", "arbitrary")),
        name="neighborhood_attention",
    )(p, p, p, p, p, table)


GQA_TQ = 256
GQA_CK = 512
GQA_QCOL = 3 * NA_HEADS
GQA_KCOL = GQA_QCOL + GQA_Q_HEADS
GQA_VCOL = GQA_KCOL + GQA_KV_HEADS


def _gqa_kernel(q_ref, k_ref, v_ref, kc_ref, vc_ref, cq_ref, sq_ref, ck_ref, sk_ref, qg_ref, kg_ref, o_prev,
                o_ref, kn_ref, kcn_ref):
    del o_prev
    @pl.when(pl.program_id(2) == 0)
    def _():
        kn = _rope(_rms_head(k_ref[...].astype(F32), kg_ref[...]), ck_ref[...], sk_ref[...])
        kn_ref[...] = kn.astype(BF16)
        kcn_ref[...] = _rms_head(kc_ref[...].astype(F32), kg_ref[...]).astype(BF16)

    cos = cq_ref[...]
    sin = sq_ref[...]
    heads = []
    for g in range(GQA_GROUP):
        qh = _rope(_rms_head(q_ref[:, g * HD:(g + 1) * HD].astype(F32), qg_ref[...]), cos, sin)
        heads.append((qh * (ATT_SCALE * LOG2E)).astype(BF16))
    q = jnp.concatenate(heads, axis=0)
    chunks = [(kn_ref[c * GQA_CK:(c + 1) * GQA_CK, :], v_ref[c * GQA_CK:(c + 1) * GQA_CK, :])
              for c in range(S // GQA_CK)]
    chunks.append((kcn_ref[...], vc_ref[...]))
    m = l = acc = None
    for kk, vv in chunks:
        s = _dot_nt(q, kk)
        mc = jnp.max(s, axis=-1, keepdims=True)
        if m is None:
            m_new = mc
            p = jnp.exp2(s - m_new)
            l = jnp.sum(p, axis=-1, keepdims=True)
            acc = jnp.dot(p.astype(BF16), vv, preferred_element_type=F32)
        else:
            m_new = jnp.maximum(m, mc)
            alpha = jnp.exp2(m - m_new)
            p = jnp.exp2(s - m_new)
            l = alpha * l + jnp.sum(p, axis=-1, keepdims=True)
            acc = alpha * acc + jnp.dot(p.astype(BF16), vv, preferred_element_type=F32)
        m = m_new
    o = acc / l
    for g in range(GQA_GROUP):
        o_ref[:, g * HD:(g + 1) * HD] = o[g * GQA_TQ:(g + 1) * GQA_TQ].astype(BF16)


def gqa_attention(p, cosf, sinf, q_gain, k_gain, o_buf):
    nq = S // GQA_TQ
    cb = S // L
    gw = GQA_GROUP * HD
    return pl.pallas_call(
        _gqa_kernel,
        grid=(B, GQA_KV_HEADS, nq),
        in_specs=[pl.BlockSpec((GQA_TQ, gw), lambda b, n, i: (b * nq + i, GQA_QCOL // GQA_GROUP + n)),
                  pl.BlockSpec((S, HD), lambda b, n, i: (b, GQA_KCOL + n)),
                  pl.BlockSpec((S, HD), lambda b, n, i: (b, GQA_VCOL + n)),
                  pl.BlockSpec((L, HD), lambda b, n, i: (B * cb + b, GQA_KCOL + n)),
                  pl.BlockSpec((L, HD), lambda b, n, i: (B * cb + b, GQA_VCOL + n)),
                  pl.BlockSpec((GQA_TQ, HD), lambda b, n, i: (i, 0)),
                  pl.BlockSpec((GQA_TQ, HD), lambda b, n, i: (i, 0)),
                  pl.BlockSpec((S, HD), lambda b, n, i: (0, 0)),
                  pl.BlockSpec((S, HD), lambda b, n, i: (0, 0)),
                  pl.BlockSpec((1, HD), lambda b, n, i: (0, 0)),
                  pl.BlockSpec((1, HD), lambda b, n, i: (0, 0)),
                  pl.BlockSpec(memory_space=pl.ANY)],
        out_specs=pl.BlockSpec((GQA_TQ, gw), lambda b, n, i: (b * nq + i, (NA_HEADS * HD) // gw + n)),
        out_shape=jax.ShapeDtypeStruct((T_ALL, D), BF16),
        input_output_aliases={11: 0},
        scratch_shapes=[pltpu.VMEM((S, HD), BF16), pltpu.VMEM((L, HD), BF16)],
        compiler_params=_cp(("arbitrary",) * 3),
        name="gqa_attention",
    )(p, p, p, p, p, cosf, sinf, cosf, sinf, q_gain.reshape(1, HD), k_gain.reshape(1, HD), o_buf)


def _ctx_attn_kernel(p_ref, qg_ref, kg_ref, o_prev, o_ref):
    del o_prev

    def col(c):
        return p_ref[:, c * HD:(c + 1) * HD]

    for h in range(NA_HEADS):
        s = _dot_nt(col(h), col(NA_HEADS + h)) * ATT_SCALE
        o_ref[:, h * HD:(h + 1) * HD] = _softmax_av([(s, col(2 * NA_HEADS + h))]).astype(BF16)
    for n in range(GQA_KV_HEADS):
        kn = _rms_head(col(GQA_KCOL + n).astype(F32), kg_ref[...]).astype(BF16)
        v = col(GQA_VCOL + n)
        for g in range(GQA_GROUP):
            h = n * GQA_GROUP + g
            qn = _rms_head(col(GQA_QCOL + h).astype(F32), qg_ref[...]).astype(BF16)
            s = _dot_nt(qn, kn) * ATT_SCALE
            o_ref[:, (NA_HEADS + h) * HD:(NA_HEADS + h + 1) * HD] = _softmax_av([(s, v)]).astype(BF16)


def ctx_attention(p, q_gain, k_gain, o_buf):
    cb = S // L
    return pl.pallas_call(
        _ctx_attn_kernel,
        grid=(B,),
        in_specs=[pl.BlockSpec((L, ATTN_IN), lambda b: (B * cb + b, 0)),
                  pl.BlockSpec((1, HD), lambda b: (0, 0)),
                  pl.BlockSpec((1, HD), lambda b: (0, 0)),
                  pl.BlockSpec(memory_space=pl.ANY)],
        out_specs=pl.BlockSpec((L, D), lambda b: (B * cb + b, 0)),
        out_shape=jax.ShapeDtypeStruct((T_ALL, D), BF16),
        input_output_aliases={3: 0},
        compiler_params=_cp(("arbitrary",)),
        name="ctx_attention",
    )(p, q_gain.reshape(1, HD), k_gain.reshape(1, HD), o_buf)


def _log_sigmoid(x):
    return -(jnp.maximum(-x, 0.0) + jnp.log1p(jnp.exp(-jnp.abs(x))))


def _dot_hi(a, b):
    return jnp.dot(a, b, precision=lax.Precision.HIGHEST, preferred_element_type=F32)


def _gate_kernel(h_ref, wg_ref, wgt_ref, b_ref, bt_ref, lt_ref, ut_ref, col_ref, row_ref):
    nh = ML_HEADS
    hx = h_ref[...]
    g = jnp.dot(hx, wg_ref[...], preferred_element_type=F32) + b_ref[...]
    gt = _dot_nt(wgt_ref[...], hx) + bt_ref[...]
    li = g[:, 0:2 * nh]
    lf = _log_sigmoid(g[:, 2 * nh:4 * nh])
    lit = gt[0:2 * nh]
    lft = _log_sigmoid(gt[2 * nh:4 * nh])
    lt = lt_ref[...]
    ut = ut_ref[...]
    lane = lax.broadcasted_iota(I32, lf.shape, 1)
    bc = jnp.where(lane < nh, _dot_hi(lt, lf), _dot_hi(ut, lf))
    tot = jnp.sum(lf, axis=0, keepdims=True)
    aend = tot - bc + li
    col_ref[...] = jnp.concatenate([bc, aend, jnp.zeros((ML_CH, LANE - 4 * nh), F32)], axis=1)
    sub = lax.broadcasted_iota(I32, lft.shape, 0)
    bct = jnp.where(sub < nh, _dot_hi(lft, ut), _dot_hi(lft, lt))
    tott = jnp.sum(lft, axis=1, keepdims=True)
    gtr = lit - bct
    row_ref[0] = jnp.concatenate([bct, gtr, tott + gtr, jnp.broadcast_to(tott, bct.shape)], axis=0)


def mlstm_gates(hx, wg, gate_b):
    nh = ML_HEADS
    n_ch = T_ALL // ML_CH
    wg_pad = jnp.zeros((D, LANE), BF16).at[:, :4 * nh].set(wg.astype(BF16))
    b_pad = jnp.zeros((1, LANE), F32).at[0, :4 * nh].set(gate_b.reshape(-1))
    wgt = wg.astype(BF16).T
    bt = gate_b.reshape(4 * nh, 1).astype(F32)
    lt = jnp.tril(jnp.ones((ML_CH, ML_CH), F32))
    ut = jnp.triu(jnp.ones((ML_CH, ML_CH), F32))
    col, row = pl.pallas_call(
        _gate_kernel,
        grid=(n_ch,),
        in_specs=[pl.BlockSpec((ML_CH, D), lambda i: (i, 0)),
                  pl.BlockSpec((D, LANE), lambda i: (0, 0)),
                  pl.BlockSpec((4 * nh, D), lambda i: (0, 0)),
                  pl.BlockSpec((1, LANE), lambda i: (0, 0)),
                  pl.BlockSpec((4 * nh, 1), lambda i: (0, 0)),
                  pl.BlockSpec((ML_CH, ML_CH), lambda i: (0, 0)),
                  pl.BlockSpec((ML_CH, ML_CH), lambda i: (0, 0))],
        out_specs=[pl.BlockSpec((ML_CH, LANE), lambda i: (i, 0)),
                   pl.BlockSpec((1, 8 * nh, ML_CH), lambda i: (i, 0, 0))],
        out_shape=[jax.ShapeDtypeStruct((T_ALL, LANE), F32),
                   jax.ShapeDtypeStruct((n_ch, 8 * nh, ML_CH), F32)],
        compiler_params=_cp(("arbitrary",)),
        name="mlstm_gates",
    )(hx, wg_pad, wgt, b_pad, bt, lt, ut)
    return col, row


def _mlstm_step(d, hh, q_ref, k_ref, v_ref, col_ref, row_ref, o_ref, c_ref, n_ref, m_ref):
    nd = 2 * ML_HEADS
    sl = d * ML_HEADS + hh
    q = q_ref[:, hh * ML_QK:(hh + 1) * ML_QK]
    kf = k_ref[:, hh * ML_QK:(hh + 1) * ML_QK].astype(F32) * ML_KSCALE
    kb = kf.astype(BF16)
    v = v_ref[:, hh * ML_V:(hh + 1) * ML_V]
    bc_col = col_ref[:, sl:sl + 1]
    aend_col = col_ref[:, nd + sl:nd + sl + 1]
    g_row = row_ref[0, nd + sl:nd + sl + 1, :]
    aend_row = row_ref[0, 2 * nd + sl:2 * nd + sl + 1, :]
    btot = row_ref[0, 3 * nd + sl:3 * nd + sl + 1, 0:1]
    m_st = m_ref[sl]
    c_st = c_ref[sl]
    n_st = n_ref[sl]
    m_new = jnp.maximum(btot + m_st, jnp.max(aend_row, axis=1, keepdims=True))

    r = lax.broadcasted_iota(I32, (ML_CH, ML_CH), 0)
    c = lax.broadcasted_iota(I32, (ML_CH, ML_CH), 1)
    causal = (r >= c) if d == 0 else (r <= c)
    d_mat = jnp.where(causal, bc_col + g_row, -jnp.inf)
    m_row = jnp.maximum(bc_col + m_st, jnp.max(d_mat, axis=1, keepdims=True))
    w_inter = jnp.exp(bc_col + m_st - m_row)
    s_mat = _dot_nt(q, kb) * jnp.exp(d_mat - m_row)
    num = (w_inter * jnp.dot(q, c_st.astype(BF16), preferred_element_type=F32)
           + jnp.dot(s_mat.astype(BF16), v, preferred_element_type=F32))
    den = (w_inter * jnp.sum(q.astype(F32) * n_st, axis=1, keepdims=True)
           + jnp.sum(s_mat, axis=1, keepdims=True))
    o_ref[:, hh * ML_V:(hh + 1) * ML_V] = num / jnp.maximum(jnp.abs(den), jnp.exp(-m_row))

    w_end_col = jnp.exp(aend_col - m_new)
    w_end_row = jnp.exp(aend_row - m_new)
    decay = jnp.exp(btot + m_st - m_new)
    kw = (kf * w_end_col).astype(BF16)
    c_ref[sl] = decay * c_st + lax.dot_general(kw, v, (((0,), (0,)), ((), ())), preferred_element_type=F32)
    w8 = jnp.broadcast_to(w_end_row, (8, ML_CH)).astype(BF16)
    n_ref[sl] = decay * n_st + jnp.dot(w8, kb, preferred_element_type=F32)[0:1]
    m_ref[sl] = m_new


def _mlstm_kernel(qf, kf, vf, colf, rowf, qb, kb, vb, colb, rowb, of, ob, c_ref, n_ref, m_ref):
    @pl.when(pl.program_id(1) == 0)
    def _():
        c_ref[...] = jnp.zeros_like(c_ref)
        n_ref[...] = jnp.zeros_like(n_ref)
        m_ref[...] = jnp.zeros_like(m_ref)

    for hh in range(ML_HEADS):
        _mlstm_step(0, hh, qf, kf, vf, colf, rowf, of, c_ref, n_ref, m_ref)
        _mlstm_step(1, hh, qb, kb, vb, colb, rowb, ob, c_ref, n_ref, m_ref)


def mlstm_scan(p, col, row):
    n_lat = S // ML_CH
    steps = n_lat + 1
    lat_blocks = T_LAT // ML_CH
    qk_w = ML_HEADS * ML_QK
    v_w = ML_HEADS * ML_V
    n_chains = 2 * ML_HEADS

    def chunk(b, d, st):
        c = (st - 1) if d == 0 else (n_lat - st)
        return jnp.where(st == 0, lat_blocks + b, b * n_lat + c)

    def out_chunk(b, d, st):
        s1 = jnp.maximum(st, 1)
        return b * n_lat + ((s1 - 1) if d == 0 else (n_lat - s1))

    def dir_specs(d):
        return [pl.BlockSpec((ML_CH, qk_w), lambda b, s: (chunk(b, d, s), 0)),
                pl.BlockSpec((ML_CH, qk_w), lambda b, s: (chunk(b, d, s), 1)),
                pl.BlockSpec((ML_CH, v_w), lambda b, s: (chunk(b, d, s), (2 * qk_w) // v_w)),
                pl.BlockSpec((ML_CH, LANE), lambda b, s: (chunk(b, d, s), 0)),
                pl.BlockSpec((1, 4 * n_chains, ML_CH), lambda b, s: (chunk(b, d, s), 0, 0))]

    return pl.pallas_call(
        _mlstm_kernel,
        grid=(B, steps),
        in_specs=dir_specs(0) + dir_specs(1),
        out_specs=[pl.BlockSpec((ML_CH, v_w), lambda b, s: (out_chunk(b, 0, s), 0)),
                   pl.BlockSpec((ML_CH, v_w), lambda b, s: (out_chunk(b, 1, s), 0))],
        out_shape=[jax.ShapeDtypeStruct((T_LAT, v_w), F32), jax.ShapeDtypeStruct((T_LAT, v_w), F32)],
        scratch_shapes=[pltpu.VMEM((n_chains, ML_QK, ML_V), F32), pltpu.VMEM((n_chains, 1, ML_QK), F32),
                        pltpu.VMEM((n_chains, 1, 1), F32)],
        compiler_params=_cp(("arbitrary",) * 2),
        name="mlstm_scan",
    )(p, p, p, col, row, p, p, p, col, row)


def _readout_kernel(hf_ref, hb_ref, o_ref, g_ref, a_ref):
    hs = hf_ref[...] + hb_ref[...]
    for h in range(ML_HEADS):
        sl = slice(h * ML_V, (h + 1) * ML_V)
        x = hs[:, sl]
        hn = x * lax.rsqrt(jnp.mean(x * x, axis=-1, keepdims=True) + EPS) * g_ref[:, sl]
        a_ref[:, sl] = (hn * jax.nn.sigmoid(o_ref[:, sl].astype(F32))).astype(BF16)


def mlstm_readout(hdir, p, head_gain):
    tm = 256
    ocol = (2 * ML_HEADS * ML_QK + ML_HEADS * ML_V) // D
    return pl.pallas_call(
        _readout_kernel,
        grid=(T_LAT // tm,),
        in_specs=[pl.BlockSpec((tm, D), lambda i: (i, 0)),
                  pl.BlockSpec((tm, D), lambda i: (i, 0)),
                  pl.BlockSpec((tm, D), lambda i: (i, ocol)),
                  pl.BlockSpec((1, D), lambda i: (0, 0))],
        out_specs=pl.BlockSpec((tm, D), lambda i: (i, 0)),
        out_shape=jax.ShapeDtypeStruct((T_LAT, D), BF16),
        compiler_params=_cp(("arbitrary",)),
        name="mlstm_readout",
    )(hdir[0], hdir[1], p, head_gain.reshape(1, D))


ROUTER_TM = 512


def _router_kernel(h_ref, w_ref, rb_ref, erow_ref, tri_ref, eidx_ref, wts_ref, pos_ref, cnt_ref, carry_ref):
    ng = N_GROUPS
    epg = N_EXPERTS // N_GROUPS
    tm = ROUTER_TM
    ninf = -jnp.inf

    @pl.when(pl.program_id(0) == 0)
    def _():
        carry_ref[...] = jnp.zeros_like(carry_ref)

    s = jax.nn.sigmoid(_dot_nt(w_ref[...], h_ref[...]))
    ssel = s + rb_ref[...]
    sraw = [s[ng * j:ng * (j + 1)] for j in range(epg)]
    slab = [ssel[ng * j:ng * (j + 1)] for j in range(epg)]
    m1 = functools.reduce(jnp.maximum, slab)
    jfirst = functools.reduce(jnp.minimum, [jnp.where(slab[j] == m1, j, epg) for j in range(epg)])
    m2 = functools.reduce(jnp.maximum, [jnp.where(jfirst == j, ninf, slab[j]) for j in range(epg)])
    gs = m1 + m2
    giota = lax.broadcasted_iota(I32, (ng, tm), 0)
    gsel = jnp.zeros((ng, tm), F32)
    for _ in range(TOPK_GROUPS):
        mx = jnp.max(gs, axis=0, keepdims=True)
        gi = jnp.min(jnp.where(gs == mx, giota, ng), axis=0, keepdims=True)
        hit = giota == gi
        gsel = jnp.where(hit, 1.0, gsel)
        gs = jnp.where(hit, ninf, gs)
    msl = [jnp.where(gsel > 0.0, slab[j], ninf) for j in range(epg)]
    eid = [giota * epg + j for j in range(epg)]
    selm = [jnp.zeros((ng, tm), F32) for _ in range(epg)]
    e_list, w_list = [], []
    for _ in range(TOP_K):
        mx = jnp.max(functools.reduce(jnp.maximum, msl), axis=0, keepdims=True)
        cand = functools.reduce(jnp.minimum, [jnp.where(msl[j] == mx, eid[j], N_EXPERTS) for j in range(epg)])
        esel = jnp.min(cand, axis=0, keepdims=True)
        hits = [eid[j] == esel for j in range(epg)]
        wk = functools.reduce(lambda a, b: a + b, [jnp.where(hits[j], sraw[j], 0.0) for j in range(epg)])
        w_list.append(jnp.sum(wk, axis=0, keepdims=True))
        e_list.append(esel)
        msl = [jnp.where(hits[j], ninf, msl[j]) for j in range(epg)]
        selm = [jnp.where(hits[j], 1.0, selm[j]) for j in range(epg)]
    wsum = functools.reduce(lambda a, b: a + b, w_list)
    wts_ref[...] = jnp.concatenate([w / wsum * ROUTED_SCALE for w in w_list], axis=0)
    eidx_ref[...] = jnp.concatenate(e_list, axis=0)
    sel = jnp.concatenate(selm, axis=0)
    carry = carry_ref[...]
    posfull = jnp.dot(sel.astype(BF16), tri_ref[...], preferred_element_type=F32) + carry
    erow = erow_ref[...]
    pos = [jnp.sum(jnp.where(erow == e, posfull, 0.0), axis=0, keepdims=True) for e in e_list]
    pos_ref[...] = jnp.concatenate(pos, axis=0).astype(I32)
    carry = carry + jnp.sum(sel, axis=1, keepdims=True)
    carry_ref[...] = carry
    cnt_ref[...] = carry


def moe_router(hx, router_w, router_b, n_tok):
    tm = ROUTER_TM
    epg = N_EXPERTS // N_GROUPS
    perm = (jnp.arange(N_EXPERTS) % N_GROUPS) * epg + jnp.arange(N_EXPERTS) // N_GROUPS
    w_t = router_w.astype(BF16).T[perm]
    rb = router_b.astype(F32)[perm].reshape(N_EXPERTS, 1)
    erow = perm.astype(I32).reshape(N_EXPERTS, 1)
    tri = jnp.triu(jnp.ones((tm, tm), BF16), 1)
    eidx, wts, pos, counts = pl.pallas_call(
        _router_kernel,
        grid=(n_tok // tm,),
        in_specs=[pl.BlockSpec((tm, D), lambda i: (i, 0)),
                  pl.BlockSpec((N_EXPERTS, D), lambda i: (0, 0)),
                  pl.BlockSpec((N_EXPERTS, 1), lambda i: (0, 0)),
                  pl.BlockSpec((N_EXPERTS, 1), lambda i: (0, 0)),
                  pl.BlockSpec((tm, tm), lambda i: (0, 0))],
        out_specs=[pl.BlockSpec((TOP_K, tm), lambda i: (0, i)),
                   pl.BlockSpec((TOP_K, tm), lambda i: (0, i)),
                   pl.BlockSpec((TOP_K, tm), lambda i: (0, i)),
                   pl.BlockSpec((N_EXPERTS, 1), lambda i: (0, 0))],
        out_shape=[jax.ShapeDtypeStruct((TOP_K, n_tok), I32),
                   jax.ShapeDtypeStruct((TOP_K, n_tok), F32),
                   jax.ShapeDtypeStruct((TOP_K, n_tok), I32),
                   jax.ShapeDtypeStruct((N_EXPERTS, 1), F32)],
        scratch_shapes=[pltpu.VMEM((N_EXPERTS, 1), F32)],
        compiler_params=_cp(("arbitrary",)),
        name="moe_router",
    )(hx, w_t, rb, erow, tri)
    return eidx, wts, pos, counts.reshape(N_EXPERTS)[perm]


DISPATCH_TM = 512


ROW_UNROLL = 8


def _dispatch_kernel(slot_ref, hx_ref, xs_hbm, sem):
    def issue(tt, carry):
        t8 = pl.multiple_of(tt * ROW_UNROLL, ROW_UNROLL)
        for j in range(ROW_UNROLL):
            src = hx_ref.at[pl.ds(pl.multiple_of((t8 + j) * PK_S, PK_S), PK_S), :]
            for k in range(TOP_K):
                row = pl.multiple_of(slot_ref[k * DISPATCH_TM + j + t8] * PK_S, PK_S)
                pltpu.make_async_copy(src, xs_hbm.at[pl.ds(row, PK_S), :], sem).start(priority=k % 2)
        return carry

    lax.fori_loop(0, DISPATCH_TM // ROW_UNROLL, issue, 0)
    for _ in range(TOP_K):
        pltpu.make_async_copy(hx_ref, xs_hbm.at[pl.ds(0, DISPATCH_TM * PK_S), :], sem).wait()


def moe_dispatch(slots, hx_packed, n_tok, n_rows):
    tm = DISPATCH_TM
    return pl.pallas_call(
        _dispatch_kernel,
        grid=(n_tok // tm,),
        in_specs=[pl.BlockSpec((TOP_K * tm,), lambda i: (i,), memory_space=pltpu.SMEM),
                  pl.BlockSpec((tm * PK_S, LANE), lambda i: (i, 0))],
        out_specs=pl.BlockSpec(memory_space=pl.ANY),
        out_shape=jax.ShapeDtypeStruct((n_rows * PK_S, LANE), U32),
        scratch_shapes=[pltpu.SemaphoreType.DMA(())],
        compiler_params=_cp(("arbitrary",)),
        name="moe_dispatch",
    )(slots, hx_packed)


def _expert_kernel(be_ref, valid_ref, nused_ref, first_ref, next_ref, slot_ref,
                   xs_ref, w1_hbm, w3_hbm, w2_hbm, y_ref, w1s, w3s, w2s, w1b, w3b, w2b, xb, sems, *, layer):
    i = pl.program_id(0)
    bm = EXP_BM

    def weight_copies(e, s):
        return [pltpu.make_async_copy(w_hbm.at[layer, e], stage.at[s], sems.at[s, j])
                for j, (w_hbm, stage) in enumerate(((w1_hbm, w1s), (w3_hbm, w3s), (w2_hbm, w2s)))]

    @pl.when(i < nused_ref[0])
    def _():
        @pl.when(first_ref[i] == 1)
        def _():
            s = slot_ref[i]

            @pl.when(i == 0)
            def _():
                for cp in weight_copies(be_ref[0], 0):
                    cp.start()

            for cp in weight_copies(be_ref[i], s):
                cp.wait()

            @pl.when(next_ref[i] >= 0)
            def _():
                for cp in weight_copies(next_ref[i], 1 - s):
                    cp.start(priority=1)

            w1b[...] = w1s[s].astype(BF16)
            w3b[...] = w3s[s].astype(BF16)
            w2b[...] = w2s[s].astype(BF16)

        live = lax.broadcasted_iota(I32, (bm, LANE), 0) < valid_ref[i]
        for s, piece in enumerate(_load_row_tiles(xs_ref, 0, bm)):
            lo, hi = _unpack_bf16_pairs(jnp.where(live, piece, jnp.uint32(0)))
            xb[:, s * LANE:(s + 1) * LANE] = lo.astype(BF16)
            xb[:, PK_W + s * LANE:PK_W + (s + 1) * LANE] = hi.astype(BF16)
        x = xb[...]
        h1 = jnp.dot(x, w1b[...], preferred_element_type=F32)
        h3 = jnp.dot(x, w3b[...], preferred_element_type=F32)
        a = (h1 * jax.nn.sigmoid(h1) * h3).astype(BF16)
        y = jnp.dot(a, w2b[...], preferred_element_type=F32)
        _store_row_tiles(y_ref, _pack_bf16_pairs(y.astype(BF16)))


def moe_experts(block_e, valid, n_used, first, next_e, slot, xs, w1, w3, w2, layer, n_blocks):
    bm = EXP_BM

    def blk(i, be, va, nu, fi, ne, sl):
        return (jnp.minimum(i, nu[0] - 1), 0)

    grid_spec = pltpu.PrefetchScalarGridSpec(
        num_scalar_prefetch=6,
        grid=(n_blocks,),
        in_specs=[pl.BlockSpec((bm * PK_S, LANE), blk),
                  pl.BlockSpec(memory_space=pl.ANY),
                  pl.BlockSpec(memory_space=pl.ANY),
                  pl.BlockSpec(memory_space=pl.ANY)],
        out_specs=pl.BlockSpec((bm * PK_S, LANE), blk),
        scratch_shapes=[pltpu.VMEM((2, D, EXPERT_DIM), F32), pltpu.VMEM((2, D, EXPERT_DIM), F32),
                        pltpu.VMEM((2, EXPERT_DIM, D), F32),
                        pltpu.VMEM((D, EXPERT_DIM), BF16), pltpu.VMEM((D, EXPERT_DIM), BF16),
                        pltpu.VMEM((EXPERT_DIM, D), BF16), pltpu.VMEM((bm, D), BF16),
                        pltpu.SemaphoreType.DMA((2, 3))],
    )
    return pl.pallas_call(
        functools.partial(_expert_kernel, layer=layer),
        grid_spec=grid_spec,
        out_shape=jax.ShapeDtypeStruct((n_blocks * bm * PK_S, LANE), U32),
        compiler_params=_cp(("arbitrary",), vmem=56 * 1024 * 1024),
        name="moe_experts",
    )(block_e, valid, n_used, first, next_e, slot, xs, w1, w3, w2)


def _shared_kernel(x_ref, w1_ref, w3_ref, w2_ref, o_ref):
    x = x_ref[...]
    h1 = jnp.dot(x, w1_ref[...], preferred_element_type=F32)
    h3 = jnp.dot(x, w3_ref[...], preferred_element_type=F32)
    a = (h1 * jax.nn.sigmoid(h1) * h3).astype(BF16)
    o_ref[...] = jnp.dot(a, w2_ref[...], preferred_element_type=F32)


def shared_expert(hx, w1, w3, w2, n_tok):
    tm = 512
    return pl.pallas_call(
        _shared_kernel,
        grid=(n_tok // tm,),
        in_specs=[pl.BlockSpec((tm, D), lambda i: (i, 0)),
                  pl.BlockSpec((D, EXPERT_DIM), lambda i: (0, 0)),
                  pl.BlockSpec((D, EXPERT_DIM), lambda i: (0, 0)),
                  pl.BlockSpec((EXPERT_DIM, D), lambda i: (0, 0))],
        out_specs=pl.BlockSpec((tm, D), lambda i: (i, 0)),
        out_shape=jax.ShapeDtypeStruct((n_tok, D), F32),
        compiler_params=_cp(("arbitrary",)),
        name="shared_expert",
    )(hx, w1, w3, w2)


COMBINE_TM = 128


def _combine_kernel(slot0_ref, slotn_ref, w_ref, sh_ref, x_ref, mod_ref, y_hbm, *rest, final, n_tiles):
    fg_ref = rest[0] if final else None
    o_ref, buf, sems = rest[-3:]
    tm = COMBINE_TM
    i = pl.program_id(0)
    half_rows = TOP_K * tm * PK_S

    def start_gathers(slot_ref, par):
        half = buf.at[par]
        sem = sems.at[par]

        def issue(tt, carry):
            t8 = pl.multiple_of(tt * ROW_UNROLL, ROW_UNROLL)
            for j in range(ROW_UNROLL):
                for k in range(TOP_K):
                    row = pl.multiple_of(slot_ref[k * tm + j + t8] * PK_S, PK_S)
                    dst = half.at[pl.ds(pl.multiple_of((k * tm + j + t8) * PK_S, PK_S), PK_S), :]
                    pltpu.make_async_copy(y_hbm.at[pl.ds(row, PK_S), :], dst, sem).start(priority=k % 2)
            return carry

        lax.fori_loop(0, tm // ROW_UNROLL, issue, 0)

    @pl.when(i == 0)
    def _():
        start_gathers(slot0_ref, 0)

    @pl.when(i + 1 < n_tiles)
    def _():
        start_gathers(slotn_ref, (i + 1) % 2)

    cur = buf.at[i % 2]
    pltpu.make_async_copy(y_hbm.at[pl.ds(0, half_rows), :], cur, sems.at[i % 2]).wait()

    w = w_ref[...]
    gate = mod_ref[0][5:6]
    acc = [None] * (2 * PK_S)
    for k in range(TOP_K):
        wk = w[:, k:k + 1]
        for s, piece in enumerate(_load_row_tiles(cur, k * tm * PK_S, tm)):
            for c, val in zip((s, PK_S + s), _unpack_bf16_pairs(piece)):
                acc[c] = wk * val if acc[c] is None else acc[c] + wk * val
    outs = []
    for c in range(2 * PK_S):
        sl = slice(c * LANE, (c + 1) * LANE)
        outs.append(x_ref[:, sl] + gate[:, sl] * (sh_ref[:, sl] + acc[c]))
    if fg_ref is not None:
        ssq = functools.reduce(lambda a, b: a + b, [jnp.sum(o * o, axis=-1, keepdims=True) for o in outs])
        inv = lax.rsqrt(ssq / D + EPS)
        outs = [o * inv * fg_ref[:, c * LANE:(c + 1) * LANE] for c, o in enumerate(outs)]
    for c, o in enumerate(outs):
        o_ref[:, c * LANE:(c + 1) * LANE] = o


def moe_combine(slots, wts_tok, shared, x, mod, y, n_tok, final_gain=None):
    tm = COMBINE_TM
    n_tiles = n_tok // tm
    final = final_gain is not None
    in_specs = [pl.BlockSpec((TOP_K * tm,), lambda i: (0,), memory_space=pltpu.SMEM),
                pl.BlockSpec((TOP_K * tm,), lambda i: (jnp.minimum(i + 1, n_tiles - 1),), memory_space=pltpu.SMEM),
                pl.BlockSpec((tm, TOP_K), lambda i: (i, 0)),
                pl.BlockSpec((tm, D), lambda i: (i, 0)),
                pl.BlockSpec((tm, D), lambda i: (i, 0)),
                pl.BlockSpec((1, 6, D), lambda i: (_mod_row(i * tm), 0, 0)),
                pl.BlockSpec(memory_space=pl.ANY)]
    args = [slots, slots, wts_tok, shared, x, mod, y]
    if final:
        in_specs.append(pl.BlockSpec((1, D), lambda i: (0, 0)))
        args.append(final_gain.reshape(1, D))
    return pl.pallas_call(
        functools.partial(_combine_kernel, final=final, n_tiles=n_tiles),
        grid=(n_tiles,),
        in_specs=in_specs,
        out_specs=pl.BlockSpec((tm, D), lambda i: (i, 0)),
        out_shape=jax.ShapeDtypeStruct((n_tok, D), F32),
        scratch_shapes=[pltpu.VMEM((2, TOP_K * tm * PK_S, LANE), U32), pltpu.SemaphoreType.DMA((2,))],
        compiler_params=_cp(("arbitrary",)),
        name="moe_combine",
    )(*args)


def _lookup(table, idx):
    e = jnp.arange(table.shape[0], dtype=I32).reshape((-1,) + (1,) * idx.ndim)
    return jnp.sum(jnp.where(idx[None] == e, table.reshape(e.shape), 0), axis=0)


def _tile_flat(slots, tm):
    k, t = slots.shape
    return slots.reshape(k, t // tm, tm).transpose(1, 0, 2).reshape(-1)


def moe_layer(x, mod, norm_gain, router_w, router_b, exp_w1, exp_w3, exp_w2, sw1, sw3, sw2, layer, n_tok,
              final_gain=None):
    bm = EXP_BM
    n_blocks = -(-n_tok * TOP_K // bm) + N_EXPERTS
    hx, hx_packed = norm_mod(x, norm_gain, mod, 3, n_tok, pack=True)
    eidx, wts, pos, counts = moe_router(hx, router_w, router_b, n_tok)
    shared = shared_expert(hx, sw1.astype(BF16), sw3.astype(BF16), sw2.astype(BF16), n_tok)
    cnt = counts.astype(I32)
    padded = (cnt + bm - 1) // bm * bm
    pad_end = jnp.cumsum(padded)
    pad_start = pad_end - padded
    slots = _lookup(pad_start, eidx) + pos
    blk_row = jnp.arange(n_blocks, dtype=I32) * bm
    block_e = jnp.minimum(jnp.sum((pad_end[:, None] <= blk_row[None, :]).astype(I32), axis=0), N_EXPERTS - 1)
    valid = jnp.clip(_lookup(cnt, block_e) - (blk_row - _lookup(pad_start, block_e)), 0, bm).astype(I32)
    n_used = (pad_end[-1:] // bm).astype(I32)
    prev_e = jnp.concatenate([jnp.full((1,), -1, I32), block_e[:-1]])
    first = ((blk_row < pad_end[-1]) & (block_e != prev_e)).astype(I32)
    stage_slot = ((jnp.cumsum(first) - 1) % 2).astype(I32)
    eids = jnp.arange(N_EXPERTS, dtype=I32)
    later = jnp.where((eids[None, :] > eids[:, None]) & (padded[None, :] > 0), eids[None, :], N_EXPERTS)
    next_nonempty = jnp.min(later, axis=1)
    next_nonempty = jnp.where(next_nonempty == N_EXPERTS, -1, next_nonempty)
    next_e = _lookup(next_nonempty, block_e).astype(I32)
    xs = moe_dispatch(_tile_flat(slots, DISPATCH_TM), hx_packed, n_tok, n_blocks * bm)
    y = moe_experts(block_e, valid, n_used, first, next_e, stage_slot, xs, exp_w1, exp_w3, exp_w2, layer, n_blocks)
    return moe_combine(_tile_flat(slots, COMBINE_TM), wts.T, shared, x, mod, y, n_tok, final_gain)


def _rope_tables():
    t = jnp.arange(S, dtype=I32)
    row = (t // GRID_W).astype(F32)
    col = (t % GRID_W).astype(F32)
    n_freq = HD // 4
    inv_freq = ROPE_THETA ** (-jnp.arange(n_freq, dtype=F32) / n_freq)
    ang = jnp.concatenate([row[:, None] * inv_freq, col[:, None] * inv_freq], axis=-1)
    cosf = jnp.repeat(jnp.cos(ang), 2, axis=-1)
    sinf = jnp.stack([-jnp.sin(ang), jnp.sin(ang)], axis=-1).reshape(S, HD)
    return cosf, sinf


def kernel(x, c, ctx, c_ctx, ada_w, ada_b, norm_mix, norm_ffn, attn_w_in, attn_w_out, attn_rpb, attn_q_gain,
           attn_k_gain, ml_w_in, ml_w_out, ml_gate_b, ml_head_gain, router_w, router_b, exp_w1, exp_w3, exp_w2,
           sh_w1, sh_w3, sh_w2, final_norm_gain):
    depth = ada_w.shape[0]
    x_lat = x.reshape(T_LAT, D)
    x_ctx = ctx.reshape(T_CTX, D)
    cvec = jnp.concatenate([c, c_ctx[None], jnp.zeros((8 - B - 1, D), F32)], axis=0)
    mod_all = ada_ln(cvec, ada_w, ada_b).reshape(depth, 8, 6, D)
    cosf, sinf = _rope_tables()

    mod = mod_all[0]
    p = norm_matmul(x_lat, norm_mix[0], mod, attn_w_in[0].astype(BF16), emit_h=False, x_ctx=x_ctx)
    o_all = neighborhood_attention(p, na_bias_table(attn_rpb[0]))
    o_all = gqa_attention(p, cosf, sinf, attn_q_gain[0], attn_k_gain[0], o_all)
    o_all = ctx_attention(p, attn_q_gain[0], attn_k_gain[0], o_all)
    xa = matmul_gated_residual(o_all, attn_w_out[0].astype(BF16), x_lat, mod, 2, x_ctx=x_ctx)
    xa = moe_layer(xa, mod, norm_ffn[0], router_w[0], router_b[0], exp_w1, exp_w3, exp_w2,
                   sh_w1[0], sh_w3[0], sh_w2[0], 0, T_ALL)

    mod = mod_all[1]
    w_in = ml_w_in[0]
    p, hx = norm_matmul(xa, norm_mix[1], mod, w_in[:, :ML_MAIN].astype(BF16), emit_h=True)
    col, row = mlstm_gates(hx, w_in[:, ML_MAIN:], ml_gate_b[0])
    hdir = mlstm_scan(p, col, row)
    a = mlstm_readout(hdir, p, ml_head_gain[0])
    xl = matmul_gated_residual(a, ml_w_out[0].astype(BF16), xa, mod, 2)
    xl = moe_layer(xl, mod, norm_ffn[1], router_w[1], router_b[1], exp_w1, exp_w3, exp_w2,
                   sh_w1[1], sh_w3[1], sh_w2[1], 1, T_LAT, final_gain=final_norm_gain)
    return xl.reshape(B, S, D)
```

```python
import functools

import jax
import jax.numpy as jnp
from jax import lax
from jax.experimental import pallas as pl
from jax.experimental.pallas import tpu as pltpu

F32 = jnp.float32
BF16 = jnp.bfloat16
I32 = jnp.int32
U32 = jnp.uint32

D = 2048
B = 4
S = 4096
L = 256
T_LAT = B * S
T_CTX = B * L
T_ALL = T_LAT + T_CTX
GRID_W = 64
ROWS = S // GRID_W
HD = 128
NA_HEADS = 8
NA_WIN_ROWS = 8
NA_WIN_COLS = 16
GQA_Q_HEADS = 8
GQA_KV_HEADS = 2
GQA_GROUP = 4
ROPE_THETA = 10000.0
ATTN_IN = 4608
ML_HEADS = 8
ML_V = 256
ML_QK = 128
ML_MAIN = 6144
N_EXPERTS = 64
TOP_K = 8
N_GROUPS = 8
TOPK_GROUPS = 4
EXPERT_DIM = 512
ROUTED_SCALE = 2.5
EPS = 1e-6
NEG_INF = -1e30
ATT_SCALE = HD ** -0.5
LOG2E = 1.4426950408889634
ML_KSCALE = ML_QK ** -0.5

LANE = 128
NA_QROWS = 4
NA_SLAB = NA_QROWS + NA_WIN_ROWS - 1
NA_QB = NA_QROWS * GRID_W
NA_KB = NA_SLAB * GRID_W
ML_CH = 256
EXP_BM = 512
PK_W = D // 2
PK_S = PK_W // LANE
VMEM_LIMIT = 48 * 1024 * 1024


def _cp(sem, vmem=VMEM_LIMIT):
    return pltpu.CompilerParams(dimension_semantics=sem, vmem_limit_bytes=vmem)


def _pack_bf16_pairs(xb):
    u = pltpu.bitcast(xb.astype(F32), U32)
    return (u[:, PK_W:] & jnp.uint32(0xFFFF0000)) | (u[:, :PK_W] >> 16)


def _unpack_bf16_pairs(u):
    return pltpu.bitcast(u << 16, F32), pltpu.bitcast(u & jnp.uint32(0xFFFF0000), F32)


def _store_row_tiles(ref, words):
    rows = words.shape[0]
    for s in range(PK_S):
        ref[pl.ds(s, rows, stride=PK_S), :] = words[:, s * LANE:(s + 1) * LANE]


def _load_row_tiles(ref, start, rows):
    return [ref[pl.ds(start + s, rows, stride=PK_S), :] for s in range(PK_S)]


def _mod_row(start_row):
    return jnp.where(start_row < T_LAT, start_row // S, B)


def _ada_kernel(c_ref, w_ref, b_ref, o_ref):
    c = c_ref[...]
    a = (c * jax.nn.sigmoid(c)).astype(BF16)
    w = w_ref[0].astype(BF16)
    o_ref[0] = jnp.dot(a, w, preferred_element_type=F32) + b_ref[0]


def ada_ln(cvec, ada_w, ada_b):
    depth = ada_w.shape[0]
    n = ada_w.shape[2]
    tn = 1024
    return pl.pallas_call(
        _ada_kernel,
        grid=(depth, n // tn),
        in_specs=[pl.BlockSpec((8, D), lambda l, j: (0, 0)),
                  pl.BlockSpec((1, D, tn), lambda l, j: (l, 0, j)),
                  pl.BlockSpec((1, 1, tn), lambda l, j: (l, 0, j))],
        out_specs=pl.BlockSpec((1, 8, tn), lambda l, j: (l, 0, j)),
        out_shape=jax.ShapeDtypeStruct((depth, 8, n), F32),
        compiler_params=_cp(("arbitrary", "arbitrary")),
        name="ada_ln",
    )(cvec, ada_w, ada_b.reshape(depth, 1, n))


def _norm_mod_kernel(x_ref, g_ref, mod_ref, *out_refs, base, pack):
    x = x_ref[...]
    y = x * lax.rsqrt(jnp.mean(x * x, axis=-1, keepdims=True) + EPS) * g_ref[...]
    m = mod_ref[0]
    h = y * (1.0 + m[base + 1:base + 2]) + m[base:base + 1]
    hb = h.astype(BF16)
    out_refs[0][...] = hb
    if pack:
        _store_row_tiles(out_refs[1], _pack_bf16_pairs(hb))


def norm_mod(x, gain, mod, base, n_rows, pack):
    tm = 256
    out_shape = [jax.ShapeDtypeStruct((n_rows, D), BF16)]
    out_specs = [pl.BlockSpec((tm, D), lambda i: (i, 0))]
    if pack:
        out_shape.append(jax.ShapeDtypeStruct((n_rows * PK_S, LANE), U32))
        out_specs.append(pl.BlockSpec((tm * PK_S, LANE), lambda i: (i, 0)))
    res = pl.pallas_call(
        functools.partial(_norm_mod_kernel, base=base, pack=pack),
        grid=(n_rows // tm,),
        in_specs=[pl.BlockSpec((tm, D), lambda i: (i, 0)),
                  pl.BlockSpec((1, D), lambda i: (0, 0)),
                  pl.BlockSpec((1, 6, D), lambda i: (_mod_row(i * tm), 0, 0))],
        out_specs=out_specs,
        out_shape=out_shape,
        compiler_params=_cp(("arbitrary",)),
        name="norm_mod",
    )(x, gain.reshape(1, D), mod)
    return res if pack else res[0]


def _norm_mm_kernel(*refs, emit_h, n_lat):
    if n_lat is None:
        x_ref, g_ref, mod_ref, w_ref, o_ref = refs[:5]
        c_ref = None
    else:
        x_ref, c_ref, g_ref, mod_ref, w_ref, o_ref = refs[:6]
    hb_ref = refs[-1]

    def prologue(src_ref):
        x = src_ref[...]
        y = x * lax.rsqrt(jnp.mean(x * x, axis=-1, keepdims=True) + EPS) * g_ref[...]
        m = mod_ref[0]
        hb_ref[...] = (y * (1.0 + m[1:2]) + m[0:1]).astype(BF16)
        if emit_h:
            refs[-2][...] = hb_ref[...]

    first_col = pl.program_id(1) == 0
    if c_ref is None:
        pl.when(first_col)(lambda: prologue(x_ref))
    else:
        is_lat = pl.program_id(0) < n_lat
        pl.when(first_col & is_lat)(lambda: prologue(x_ref))
        pl.when(first_col & jnp.logical_not(is_lat))(lambda: prologue(c_ref))

    o_ref[...] = jnp.dot(hb_ref[...], w_ref[...], preferred_element_type=F32).astype(o_ref.dtype)


def norm_matmul(x, gain, mod, w, emit_h, x_ctx=None, tm=1024, tn=512):
    split = x_ctx is not None
    m = x.shape[0] + (x_ctx.shape[0] if split else 0)
    n = w.shape[1]
    n_lat = x.shape[0] // tm if split else None
    out_shape = [jax.ShapeDtypeStruct((m, n), BF16)]
    out_specs = [pl.BlockSpec((tm, tn), lambda i, j: (i, j))]
    if emit_h:
        out_shape.append(jax.ShapeDtypeStruct((m, D), BF16))
        out_specs.append(pl.BlockSpec((tm, D), lambda i, j: (i, 0)))
    if split:
        x_specs = [pl.BlockSpec((tm, D), lambda i, j: (jnp.minimum(i, n_lat - 1), 0)),
                   pl.BlockSpec((tm, D), lambda i, j: (0, 0))]
        x_args = [x, x_ctx]
    else:
        x_specs = [pl.BlockSpec((tm, D), lambda i, j: (i, 0))]
        x_args = [x]
    res = pl.pallas_call(
        functools.partial(_norm_mm_kernel, emit_h=emit_h, n_lat=n_lat),
        grid=(m // tm, n // tn),
        in_specs=x_specs + [pl.BlockSpec((1, D), lambda i, j: (0, 0)),
                            pl.BlockSpec((1, 6, D), lambda i, j: (_mod_row(i * tm), 0, 0)),
                            pl.BlockSpec((D, tn), lambda i, j: (0, j))],
        out_specs=out_specs,
        out_shape=out_shape,
        scratch_shapes=[pltpu.VMEM((tm, D), BF16)],
        compiler_params=_cp(("arbitrary", "arbitrary"), vmem=56 * 1024 * 1024),
        name="norm_matmul",
    )(*x_args, gain.reshape(1, D), mod, w)
    return res if emit_h else res[0]


def _mm_res_kernel(*refs, slot, n_lat):
    if n_lat is None:
        a_ref, w_ref, x_ref, mod_ref, o_ref = refs
        res = x_ref[...]
    else:
        a_ref, w_ref, x_ref, c_ref, mod_ref, o_ref = refs
        res = jnp.where(pl.program_id(0) < n_lat, x_ref[...], c_ref[...])
    acc = jnp.dot(a_ref[...], w_ref[...], preferred_element_type=F32)
    o_ref[...] = res + mod_ref[0][slot:slot + 1] * acc


def matmul_gated_residual(a, w, x, mod, slot, x_ctx=None, tm=1024, tn=512):
    m, k = a.shape
    n = w.shape[1]
    split = x_ctx is not None
    n_lat = x.shape[0] // tm if split else None
    if split:
        x_specs = [pl.BlockSpec((tm, tn), lambda i, j: (jnp.minimum(i, n_lat - 1), j)),
                   pl.BlockSpec((tm, tn), lambda i, j: (0, j))]
        x_args = [x, x_ctx]
    else:
        x_specs = [pl.BlockSpec((tm, tn), lambda i, j: (i, j))]
        x_args = [x]
    return pl.pallas_call(
        functools.partial(_mm_res_kernel, slot=slot, n_lat=n_lat),
        grid=(m // tm, n // tn),
        in_specs=[pl.BlockSpec((tm, k), lambda i, j: (i, 0)),
                  pl.BlockSpec((k, tn), lambda i, j: (0, j))] + x_specs
                 + [pl.BlockSpec((1, 6, tn), lambda i, j: (_mod_row(i * tm), 0, j))],
        out_specs=pl.BlockSpec((tm, tn), lambda i, j: (i, j)),
        out_shape=jax.ShapeDtypeStruct((m, n), F32),
        compiler_params=_cp(("arbitrary", "arbitrary")),
        name="matmul_gated_residual",
    )(a, w, *x_args, mod)


def _dot_nt(a, b):
    return lax.dot_general(a, b, (((1,), (1,)), ((), ())), preferred_element_type=F32)


def _rms_head(x, gain):
    return x * lax.rsqrt(jnp.mean(x * x, axis=-1, keepdims=True) + EPS) * gain


def _rope(x, cosf, sinf):
    lane = lax.broadcasted_iota(I32, x.shape, 1)
    nxt = pltpu.roll(x, LANE - 1, 1)
    prv = pltpu.roll(x, 1, 1)
    return x * cosf + jnp.where((lane & 1) == 0, nxt, prv) * sinf


def _softmax_av(parts):
    m = functools.reduce(jnp.maximum, [jnp.max(s, axis=-1, keepdims=True) for s, _ in parts])
    l = None
    o = None
    for s, v in parts:
        p = jnp.exp(s - m)
        li = jnp.sum(p, axis=-1, keepdims=True)
        oi = jnp.dot(p.astype(BF16), v, preferred_element_type=F32)
        l = li if l is None else l + li
        o = oi if o is None else o + oi
    return o / l


def _na_kernel(q_ref, k_ref, v_ref, kc_ref, vc_ref, tab_ref, o_ref):
    kc = kc_ref[...]
    vc = vc_ref[...]
    n_blocks = ROWS // NA_QROWS

    def body(j, carry):
        ks = jnp.clip(j * NA_QROWS - NA_WIN_ROWS // 2, 0, ROWS - NA_SLAB)
        typ = jnp.where(j == 0, 0, jnp.where(j == n_blocks - 1, 2, 1))
        qs = pl.multiple_of(j * NA_QB, NA_QB)
        kst = pl.multiple_of(ks * GRID_W, GRID_W)
        q = q_ref[pl.ds(qs, NA_QB), :]
        k = k_ref[pl.ds(kst, NA_KB), :]
        v = v_ref[pl.ds(kst, NA_KB), :]
        s_win = _dot_nt(q, k) * ATT_SCALE + tab_ref[typ, 0]
        s_ctx = _dot_nt(q, kc) * ATT_SCALE
        o_ref[pl.ds(qs, NA_QB), :] = _softmax_av([(s_win, v), (s_ctx, vc)]).astype(BF16)
        return carry

    lax.fori_loop(0, n_blocks, body, 0)


def na_bias_table(rpb):
    def one(r0, ks):
        r = r0 + jnp.arange(NA_QROWS)
        kr = ks + jnp.arange(NA_SLAB)
        start = jnp.clip(r - NA_WIN_ROWS // 2, 0, ROWS - NA_WIN_ROWS)
        row_ok = (kr[None, :] >= start[:, None]) & (kr[None, :] < start[:, None] + NA_WIN_ROWS)
        row_idx = jnp.clip(kr[None, :] - r[:, None] + NA_WIN_ROWS - 1, 0, 2 * NA_WIN_ROWS - 2)
        cq = jnp.arange(GRID_W)
        col_start = jnp.clip(cq - NA_WIN_COLS // 2, 0, GRID_W - NA_WIN_COLS)
        col_ok = (cq[None, :] >= col_start[:, None]) & (cq[None, :] < col_start[:, None] + NA_WIN_COLS)
        col_idx = jnp.clip(cq[None, :] - cq[:, None] + NA_WIN_COLS - 1, 0, 2 * NA_WIN_COLS - 2)
        r_hot = jax.nn.one_hot(row_idx, 2 * NA_WIN_ROWS - 1, dtype=F32)
        c_hot = jax.nn.one_hot(col_idx, 2 * NA_WIN_COLS - 1, dtype=F32)
        bias = jnp.einsum('qka,hab,xyb->hqxky', r_hot, rpb.astype(F32), c_hot, precision=lax.Precision.HIGHEST)
        ok = row_ok[:, None, :, None] & col_ok[None, :, None, :]
        return jnp.where(ok[None], bias, NEG_INF).reshape(NA_HEADS, NA_QB, NA_KB)

    mid = 2 * NA_QROWS
    last = ROWS - NA_QROWS
    return jnp.stack([one(0, 0), one(mid, mid - NA_WIN_ROWS // 2), one(last, ROWS - NA_SLAB)])


def neighborhood_attention(p, table):
    cb = S // L
    return pl.pallas_call(
        _na_kernel,
        grid=(NA_HEADS, B),
        in_specs=[pl.BlockSpec((S, HD), lambda h, b: (b, h)),
                  pl.BlockSpec((S, HD), lambda h, b: (b, NA_HEADS + h)),
                  pl.BlockSpec((S, HD), lambda h, b: (b, 2 * NA_HEADS + h)),
                  pl.BlockSpec((L, HD), lambda h, b: (B * cb + b, NA_HEADS + h)),
                  pl.BlockSpec((L, HD), lambda h, b: (B * cb + b, 2 * NA_HEADS + h)),
                  pl.BlockSpec((3, 1, NA_QB, NA_KB), lambda h, b: (0, h, 0, 0))],
        out_specs=pl.BlockSpec((S, HD), lambda h, b: (b, h)),
        out_shape=jax.ShapeDtypeStruct((T_ALL, D), BF16),
        compiler_params=_cp(("arbitrary", "arbitrary")),
        name="neighborhood_attention",
    )(p, p, p, p, p, table)


GQA_TQ = 512
GQA_CK = 1024
GQA_QCOL = 3 * NA_HEADS
GQA_KCOL = GQA_QCOL + GQA_Q_HEADS
GQA_VCOL = GQA_KCOL + GQA_KV_HEADS


def _gqa_kernel(q_ref, k_ref, v_ref, kc_ref, vc_ref, cq_ref, sq_ref, ck_ref, sk_ref, qg_ref, kg_ref, o_prev,
                o_ref, kn_ref, kcn_ref):
    del o_prev
    @pl.when(pl.program_id(2) == 0)
    def _():
        kn = _rope(_rms_head(k_ref[...].astype(F32), kg_ref[...]), ck_ref[...], sk_ref[...])
        kn_ref[...] = kn.astype(BF16)
        kcn_ref[...] = _rms_head(kc_ref[...].astype(F32), kg_ref[...]).astype(BF16)

    cos = cq_ref[...]
    sin = sq_ref[...]
    heads = []
    for g in range(GQA_GROUP):
        qh = _rope(_rms_head(q_ref[:, g * HD:(g + 1) * HD].astype(F32), qg_ref[...]), cos, sin)
        heads.append((qh * (ATT_SCALE * LOG2E)).astype(BF16))
    q = jnp.concatenate(heads, axis=0)
    chunks = [(kn_ref[c * GQA_CK:(c + 1) * GQA_CK, :], v_ref[c * GQA_CK:(c + 1) * GQA_CK, :])
              for c in range(S // GQA_CK)]
    chunks.append((kcn_ref[...], vc_ref[...]))
    m = l = acc = None
    for kk, vv in chunks:
        s = _dot_nt(q, kk)
        mc = jnp.max(s, axis=-1, keepdims=True)
        if m is None:
            m_new = mc
            p = jnp.exp2(s - m_new)
            l = jnp.sum(p, axis=-1, keepdims=True)
            acc = jnp.dot(p.astype(BF16), vv, preferred_element_type=F32)
        else:
            m_new = jnp.maximum(m, mc)
            alpha = jnp.exp2(m - m_new)
            p = jnp.exp2(s - m_new)
            l = alpha * l + jnp.sum(p, axis=-1, keepdims=True)
            acc = alpha * acc + jnp.dot(p.astype(BF16), vv, preferred_element_type=F32)
        m = m_new
    o = acc / l
    for g in range(GQA_GROUP):
        o_ref[:, g * HD:(g + 1) * HD] = o[g * GQA_TQ:(g + 1) * GQA_TQ].astype(BF16)


def gqa_attention(p, cosf, sinf, q_gain, k_gain, o_buf):
    nq = S // GQA_TQ
    cb = S // L
    gw = GQA_GROUP * HD
    return pl.pallas_call(
        _gqa_kernel,
        grid=(B, GQA_KV_HEADS, nq),
        in_specs=[pl.BlockSpec((GQA_TQ, gw), lambda b, n, i: (b * nq + i, GQA_QCOL // GQA_GROUP + n)),
                  pl.BlockSpec((S, HD), lambda b, n, i: (b, GQA_KCOL + n)),
                  pl.BlockSpec((S, HD), lambda b, n, i: (b, GQA_VCOL + n)),
                  pl.BlockSpec((L, HD), lambda b, n, i: (B * cb + b, GQA_KCOL + n)),
                  pl.BlockSpec((L, HD), lambda b, n, i: (B * cb + b, GQA_VCOL + n)),
                  pl.BlockSpec((GQA_TQ, HD), lambda b, n, i: (i, 0)),
                  pl.BlockSpec((GQA_TQ, HD), lambda b, n, i: (i, 0)),
                  pl.BlockSpec((S, HD), lambda b, n, i: (0, 0)),
                  pl.BlockSpec((S, HD), lambda b, n, i: (0, 0)),
                  pl.BlockSpec((1, HD), lambda b, n, i: (0, 0)),
                  pl.BlockSpec((1, HD), lambda b, n, i: (0, 0)),
                  pl.BlockSpec(memory_space=pl.ANY)],
        out_specs=pl.BlockSpec((GQA_TQ, gw), lambda b, n, i: (b * nq + i, (NA_HEADS * HD) // gw + n)),
        out_shape=jax.ShapeDtypeStruct((T_ALL, D), BF16),
        input_output_aliases={11: 0},
        scratch_shapes=[pltpu.VMEM((S, HD), BF16), pltpu.VMEM((L, HD), BF16)],
        compiler_params=_cp(("arbitrary",) * 3),
        name="gqa_attention",
    )(p, p, p, p, p, cosf, sinf, cosf, sinf, q_gain.reshape(1, HD), k_gain.reshape(1, HD), o_buf)


def _ctx_attn_kernel(p_ref, qg_ref, kg_ref, o_prev, o_ref):
    del o_prev

    def col(c):
        return p_ref[:, c * HD:(c + 1) * HD]

    for h in range(NA_HEADS):
        s = _dot_nt(col(h), col(NA_HEADS + h)) * ATT_SCALE
        o_ref[:, h * HD:(h + 1) * HD] = _softmax_av([(s, col(2 * NA_HEADS + h))]).astype(BF16)
    for n in range(GQA_KV_HEADS):
        kn = _rms_head(col(GQA_KCOL + n).astype(F32), kg_ref[...]).astype(BF16)
        v = col(GQA_VCOL + n)
        for g in range(GQA_GROUP):
            h = n * GQA_GROUP + g
            qn = _rms_head(col(GQA_QCOL + h).astype(F32), qg_ref[...]).astype(BF16)
            s = _dot_nt(qn, kn) * ATT_SCALE
            o_ref[:, (NA_HEADS + h) * HD:(NA_HEADS + h + 1) * HD] = _softmax_av([(s, v)]).astype(BF16)


def ctx_attention(p, q_gain, k_gain, o_buf):
    cb = S // L
    return pl.pallas_call(
        _ctx_attn_kernel,
        grid=(B,),
        in_specs=[pl.BlockSpec((L, ATTN_IN), lambda b: (B * cb + b, 0)),
                  pl.BlockSpec((1, HD), lambda b: (0, 0)),
                  pl.BlockSpec((1, HD), lambda b: (0, 0)),
                  pl.BlockSpec(memory_space=pl.ANY)],
        out_specs=pl.BlockSpec((L, D), lambda b: (B * cb + b, 0)),
        out_shape=jax.ShapeDtypeStruct((T_ALL, D), BF16),
        input_output_aliases={3: 0},
        compiler_params=_cp(("arbitrary",)),
        name="ctx_attention",
    )(p, q_gain.reshape(1, HD), k_gain.reshape(1, HD), o_buf)


def _log_sigmoid(x):
    return -(jnp.maximum(-x, 0.0) + jnp.log1p(jnp.exp(-jnp.abs(x))))


def _dot_hi(a, b):
    return jnp.dot(a, b, precision=lax.Precision.HIGHEST, preferred_element_type=F32)


def _gate_kernel(h_ref, wg_ref, wgt_ref, b_ref, bt_ref, lt_ref, ut_ref, col_ref, row_ref):
    nh = ML_HEADS
    hx = h_ref[...]
    g = jnp.dot(hx, wg_ref[...], preferred_element_type=F32) + b_ref[...]
    gt = _dot_nt(wgt_ref[...], hx) + bt_ref[...]
    li = g[:, 0:2 * nh]
    lf = _log_sigmoid(g[:, 2 * nh:4 * nh])
    lit = gt[0:2 * nh]
    lft = _log_sigmoid(gt[2 * nh:4 * nh])
    lt = lt_ref[...]
    ut = ut_ref[...]
    lane = lax.broadcasted_iota(I32, lf.shape, 1)
    bc = jnp.where(lane < nh, _dot_hi(lt, lf), _dot_hi(ut, lf))
    tot = jnp.sum(lf, axis=0, keepdims=True)
    aend = tot - bc + li
    col_ref[...] = jnp.concatenate([bc, aend, jnp.zeros((ML_CH, LANE - 4 * nh), F32)], axis=1)
    sub = lax.broadcasted_iota(I32, lft.shape, 0)
    bct = jnp.where(sub < nh, _dot_hi(lft, ut), _dot_hi(lft, lt))
    tott = jnp.sum(lft, axis=1, keepdims=True)
    gtr = lit - bct
    row_ref[0] = jnp.concatenate([bct, gtr, tott + gtr, jnp.broadcast_to(tott, bct.shape)], axis=0)


def mlstm_gates(hx, wg, gate_b):
    nh = ML_HEADS
    n_ch = T_ALL // ML_CH
    wg_pad = jnp.zeros((D, LANE), BF16).at[:, :4 * nh].set(wg.astype(BF16))
    b_pad = jnp.zeros((1, LANE), F32).at[0, :4 * nh].set(gate_b.reshape(-1))
    wgt = wg.astype(BF16).T
    bt = gate_b.reshape(4 * nh, 1).astype(F32)
    lt = jnp.tril(jnp.ones((ML_CH, ML_CH), F32))
    ut = jnp.triu(jnp.ones((ML_CH, ML_CH), F32))
    col, row = pl.pallas_call(
        _gate_kernel,
        grid=(n_ch,),
        in_specs=[pl.BlockSpec((ML_CH, D), lambda i: (i, 0)),
                  pl.BlockSpec((D, LANE), lambda i: (0, 0)),
                  pl.BlockSpec((4 * nh, D), lambda i: (0, 0)),
                  pl.BlockSpec((1, LANE), lambda i: (0, 0)),
                  pl.BlockSpec((4 * nh, 1), lambda i: (0, 0)),
                  pl.BlockSpec((ML_CH, ML_CH), lambda i: (0, 0)),
                  pl.BlockSpec((ML_CH, ML_CH), lambda i: (0, 0))],
        out_specs=[pl.BlockSpec((ML_CH, LANE), lambda i: (i, 0)),
                   pl.BlockSpec((1, 8 * nh, ML_CH), lambda i: (i, 0, 0))],
        out_shape=[jax.ShapeDtypeStruct((T_ALL, LANE), F32),
                   jax.ShapeDtypeStruct((n_ch, 8 * nh, ML_CH), F32)],
        compiler_params=_cp(("arbitrary",)),
        name="mlstm_gates",
    )(hx, wg_pad, wgt, b_pad, bt, lt, ut)
    return col, row


def _mlstm_step(d, hh, q_ref, k_ref, v_ref, col_ref, row_ref, o_ref, c_ref, n_ref, m_ref):
    nd = 2 * ML_HEADS
    sl = d * ML_HEADS + hh
    q = q_ref[:, hh * ML_QK:(hh + 1) * ML_QK]
    kf = k_ref[:, hh * ML_QK:(hh + 1) * ML_QK].astype(F32) * ML_KSCALE
    kb = kf.astype(BF16)
    v = v_ref[:, hh * ML_V:(hh + 1) * ML_V]
    bc_col = col_ref[:, sl:sl + 1]
    aend_col = col_ref[:, nd + sl:nd + sl + 1]
    g_row = row_ref[0, nd + sl:nd + sl + 1, :]
    aend_row = row_ref[0, 2 * nd + sl:2 * nd + sl + 1, :]
    btot = row_ref[0, 3 * nd + sl:3 * nd + sl + 1, 0:1]
    m_st = m_ref[sl]
    c_st = c_ref[sl]
    n_st = n_ref[sl]
    m_new = jnp.maximum(btot + m_st, jnp.max(aend_row, axis=1, keepdims=True))

    r = lax.broadcasted_iota(I32, (ML_CH, ML_CH), 0)
    c = lax.broadcasted_iota(I32, (ML_CH, ML_CH), 1)
    causal = (r >= c) if d == 0 else (r <= c)
    d_mat = jnp.where(causal, bc_col + g_row, -jnp.inf)
    m_row = jnp.maximum(bc_col + m_st, jnp.max(d_mat, axis=1, keepdims=True))
    w_inter = jnp.exp(bc_col + m_st - m_row)
    s_mat = _dot_nt(q, kb) * jnp.exp(d_mat - m_row)
    num = (w_inter * jnp.dot(q, c_st.astype(BF16), preferred_element_type=F32)
           + jnp.dot(s_mat.astype(BF16), v, preferred_element_type=F32))
    den = (w_inter * jnp.sum(q.astype(F32) * n_st, axis=1, keepdims=True)
           + jnp.sum(s_mat, axis=1, keepdims=True))
    o_ref[:, hh * ML_V:(hh + 1) * ML_V] = num / jnp.maximum(jnp.abs(den), jnp.exp(-m_row))

    w_end_col = jnp.exp(aend_col - m_new)
    w_end_row = jnp.exp(aend_row - m_new)
    decay = jnp.exp(btot + m_st - m_new)
    kw = (kf * w_end_col).astype(BF16)
    c_ref[sl] = decay * c_st + lax.dot_general(kw, v, (((0,), (0,)), ((), ())), preferred_element_type=F32)
    w8 = jnp.broadcast_to(w_end_row, (8, ML_CH)).astype(BF16)
    n_ref[sl] = decay * n_st + jnp.dot(w8, kb, preferred_element_type=F32)[0:1]
    m_ref[sl] = m_new


def _mlstm_kernel(qf, kf, vf, colf, rowf, qb, kb, vb, colb, rowb, of, ob, c_ref, n_ref, m_ref):
    @pl.when(pl.program_id(1) == 0)
    def _():
        c_ref[...] = jnp.zeros_like(c_ref)
        n_ref[...] = jnp.zeros_like(n_ref)
        m_ref[...] = jnp.zeros_like(m_ref)

    for hh in range(ML_HEADS):
        _mlstm_step(0, hh, qf, kf, vf, colf, rowf, of, c_ref, n_ref, m_ref)
        _mlstm_step(1, hh, qb, kb, vb, colb, rowb, ob, c_ref, n_ref, m_ref)


def mlstm_scan(p, col, row):
    n_lat = S // ML_CH
    steps = n_lat + 1
    lat_blocks = T_LAT // ML_CH
    qk_w = ML_HEADS * ML_QK
    v_w = ML_HEADS * ML_V
    n_chains = 2 * ML_HEADS

    def chunk(b, d, st):
        c = (st - 1) if d == 0 else (n_lat - st)
        return jnp.where(st == 0, lat_blocks + b, b * n_lat + c)

    def out_chunk(b, d, st):
        s1 = jnp.maximum(st, 1)
        return b * n_lat + ((s1 - 1) if d == 0 else (n_lat - s1))

    def dir_specs(d):
        return [pl.BlockSpec((ML_CH, qk_w), lambda b, s: (chunk(b, d, s), 0)),
                pl.BlockSpec((ML_CH, qk_w), lambda b, s: (chunk(b, d, s), 1)),
                pl.BlockSpec((ML_CH, v_w), lambda b, s: (chunk(b, d, s), (2 * qk_w) // v_w)),
                pl.BlockSpec((ML_CH, LANE), lambda b, s: (chunk(b, d, s), 0)),
                pl.BlockSpec((1, 4 * n_chains, ML_CH), lambda b, s: (chunk(b, d, s), 0, 0))]

    return pl.pallas_call(
        _mlstm_kernel,
        grid=(B, steps),
        in_specs=dir_specs(0) + dir_specs(1),
        out_specs=[pl.BlockSpec((ML_CH, v_w), lambda b, s: (out_chunk(b, 0, s), 0)),
                   pl.BlockSpec((ML_CH, v_w), lambda b, s: (out_chunk(b, 1, s), 0))],
        out_shape=[jax.ShapeDtypeStruct((T_LAT, v_w), F32), jax.ShapeDtypeStruct((T_LAT, v_w), F32)],
        scratch_shapes=[pltpu.VMEM((n_chains, ML_QK, ML_V), F32), pltpu.VMEM((n_chains, 1, ML_QK), F32),
                        pltpu.VMEM((n_chains, 1, 1), F32)],
        compiler_params=_cp(("arbitrary",) * 2),
        name="mlstm_scan",
    )(p, p, p, col, row, p, p, p, col, row)


def _readout_kernel(hf_ref, hb_ref, o_ref, g_ref, a_ref):
    hs = hf_ref[...] + hb_ref[...]
    for h in range(ML_HEADS):
        sl = slice(h * ML_V, (h + 1) * ML_V)
        x = hs[:, sl]
        hn = x * lax.rsqrt(jnp.mean(x * x, axis=-1, keepdims=True) + EPS) * g_ref[:, sl]
        a_ref[:, sl] = (hn * jax.nn.sigmoid(o_ref[:, sl].astype(F32))).astype(BF16)


def mlstm_readout(hdir, p, head_gain):
    tm = 256
    ocol = (2 * ML_HEADS * ML_QK + ML_HEADS * ML_V) // D
    return pl.pallas_call(
        _readout_kernel,
        grid=(T_LAT // tm,),
        in_specs=[pl.BlockSpec((tm, D), lambda i: (i, 0)),
                  pl.BlockSpec((tm, D), lambda i: (i, 0)),
                  pl.BlockSpec((tm, D), lambda i: (i, ocol)),
                  pl.BlockSpec((1, D), lambda i: (0, 0))],
        out_specs=pl.BlockSpec((tm, D), lambda i: (i, 0)),
        out_shape=jax.ShapeDtypeStruct((T_LAT, D), BF16),
        compiler_params=_cp(("arbitrary",)),
        name="mlstm_readout",
    )(hdir[0], hdir[1], p, head_gain.reshape(1, D))


ROUTER_TM = 512


def _router_kernel(h_ref, w_ref, rb_ref, erow_ref, tri_ref, eidx_ref, wts_ref, pos_ref, cnt_ref, carry_ref):
    ng = N_GROUPS
    epg = N_EXPERTS // N_GROUPS
    tm = ROUTER_TM
    ninf = -jnp.inf

    @pl.when(pl.program_id(0) == 0)
    def _():
        carry_ref[...] = jnp.zeros_like(carry_ref)

    s = jax.nn.sigmoid(_dot_nt(w_ref[...], h_ref[...]))
    ssel = s + rb_ref[...]
    sraw = [s[ng * j:ng * (j + 1)] for j in range(epg)]
    slab = [ssel[ng * j:ng * (j + 1)] for j in range(epg)]
    m1 = functools.reduce(jnp.maximum, slab)
    jfirst = functools.reduce(jnp.minimum, [jnp.where(slab[j] == m1, j, epg) for j in range(epg)])
    m2 = functools.reduce(jnp.maximum, [jnp.where(jfirst == j, ninf, slab[j]) for j in range(epg)])
    gs = m1 + m2
    giota = lax.broadcasted_iota(I32, (ng, tm), 0)
    gsel = jnp.zeros((ng, tm), F32)
    for _ in range(TOPK_GROUPS):
        mx = jnp.max(gs, axis=0, keepdims=True)
        gi = jnp.min(jnp.where(gs == mx, giota, ng), axis=0, keepdims=True)
        hit = giota == gi
        gsel = jnp.where(hit, 1.0, gsel)
        gs = jnp.where(hit, ninf, gs)
    msl = [jnp.where(gsel > 0.0, slab[j], ninf) for j in range(epg)]
    eid = [giota * epg + j for j in range(epg)]
    selm = [jnp.zeros((ng, tm), F32) for _ in range(epg)]
    e_list, w_list = [], []
    for _ in range(TOP_K):
        mx = jnp.max(functools.reduce(jnp.maximum, msl), axis=0, keepdims=True)
        cand = functools.reduce(jnp.minimum, [jnp.where(msl[j] == mx, eid[j], N_EXPERTS) for j in range(epg)])
        esel = jnp.min(cand, axis=0, keepdims=True)
        hits = [eid[j] == esel for j in range(epg)]
        wk = functools.reduce(lambda a, b: a + b, [jnp.where(hits[j], sraw[j], 0.0) for j in range(epg)])
        w_list.append(jnp.sum(wk, axis=0, keepdims=True))
        e_list.append(esel)
        msl = [jnp.where(hits[j], ninf, msl[j]) for j in range(epg)]
        selm = [jnp.where(hits[j], 1.0, selm[j]) for j in range(epg)]
    wsum = functools.reduce(lambda a, b: a + b, w_list)
    wts_ref[...] = jnp.concatenate([w / wsum * ROUTED_SCALE for w in w_list], axis=0)
    eidx_ref[...] = jnp.concatenate(e_list, axis=0)
    sel = jnp.concatenate(selm, axis=0)
    carry = carry_ref[...]
    posfull = jnp.dot(sel.astype(BF16), tri_ref[...], preferred_element_type=F32) + carry
    erow = erow_ref[...]
    pos = [jnp.sum(jnp.where(erow == e, posfull, 0.0), axis=0, keepdims=True) for e in e_list]
    pos_ref[...] = jnp.concatenate(pos, axis=0).astype(I32)
    carry = carry + jnp.sum(sel, axis=1, keepdims=True)
    carry_ref[...] = carry
    cnt_ref[...] = carry


def moe_router(hx, router_w, router_b, n_tok):
    tm = ROUTER_TM
    epg = N_EXPERTS // N_GROUPS
    perm = (jnp.arange(N_EXPERTS) % N_GROUPS) * epg + jnp.arange(N_EXPERTS) // N_GROUPS
    w_t = router_w.astype(BF16).T[perm]
    rb = router_b.astype(F32)[perm].reshape(N_EXPERTS, 1)
    erow = perm.astype(I32).reshape(N_EXPERTS, 1)
    tri = jnp.triu(jnp.ones((tm, tm), BF16), 1)
    eidx, wts, pos, counts = pl.pallas_call(
        _router_kernel,
        grid=(n_tok // tm,),
        in_specs=[pl.BlockSpec((tm, D), lambda i: (i, 0)),
                  pl.BlockSpec((N_EXPERTS, D), lambda i: (0, 0)),
                  pl.BlockSpec((N_EXPERTS, 1), lambda i: (0, 0)),
                  pl.BlockSpec((N_EXPERTS, 1), lambda i: (0, 0)),
                  pl.BlockSpec((tm, tm), lambda i: (0, 0))],
        out_specs=[pl.BlockSpec((TOP_K, tm), lambda i: (0, i)),
                   pl.BlockSpec((TOP_K, tm), lambda i: (0, i)),
                   pl.BlockSpec((TOP_K, tm), lambda i: (0, i)),
                   pl.BlockSpec((N_EXPERTS, 1), lambda i: (0, 0))],
        out_shape=[jax.ShapeDtypeStruct((TOP_K, n_tok), I32),
                   jax.ShapeDtypeStruct((TOP_K, n_tok), F32),
                   jax.ShapeDtypeStruct((TOP_K, n_tok), I32),
                   jax.ShapeDtypeStruct((N_EXPERTS, 1), F32)],
        scratch_shapes=[pltpu.VMEM((N_EXPERTS, 1), F32)],
        compiler_params=_cp(("arbitrary",)),
        name="moe_router",
    )(hx, w_t, rb, erow, tri)
    return eidx, wts, pos, counts.reshape(N_EXPERTS)[perm]


DISPATCH_TM = 512


ROW_UNROLL = 8


def _dispatch_kernel(slot_ref, hx_ref, xs_hbm, sem):
    def issue(tt, carry):
        t8 = pl.multiple_of(tt * ROW_UNROLL, ROW_UNROLL)
        for j in range(ROW_UNROLL):
            src = hx_ref.at[pl.ds(pl.multiple_of((t8 + j) * PK_S, PK_S), PK_S), :]
            for k in range(TOP_K):
                row = pl.multiple_of(slot_ref[k * DISPATCH_TM + j + t8] * PK_S, PK_S)
                pltpu.make_async_copy(src, xs_hbm.at[pl.ds(row, PK_S), :], sem).start(priority=k % 2)
        return carry

    lax.fori_loop(0, DISPATCH_TM // ROW_UNROLL, issue, 0)
    for _ in range(TOP_K):
        pltpu.make_async_copy(hx_ref, xs_hbm.at[pl.ds(0, DISPATCH_TM * PK_S), :], sem).wait()


def moe_dispatch(slots, hx_packed, n_tok, n_rows):
    tm = DISPATCH_TM
    return pl.pallas_call(
        _dispatch_kernel,
        grid=(n_tok // tm,),
        in_specs=[pl.BlockSpec((TOP_K * tm,), lambda i: (i,), memory_space=pltpu.SMEM),
                  pl.BlockSpec((tm * PK_S, LANE), lambda i: (i, 0))],
        out_specs=pl.BlockSpec(memory_space=pl.ANY),
        out_shape=jax.ShapeDtypeStruct((n_rows * PK_S, LANE), U32),
        scratch_shapes=[pltpu.SemaphoreType.DMA(())],
        compiler_params=_cp(("arbitrary",)),
        name="moe_dispatch",
    )(slots, hx_packed)


def _expert_kernel(be_ref, valid_ref, nused_ref, first_ref, next_ref, slot_ref,
                   xs_ref, w1_hbm, w3_hbm, w2_hbm, y_ref, w1s, w3s, w2s, w1b, w3b, w2b, xb, sems, *, layer):
    i = pl.program_id(0)
    bm = EXP_BM

    def weight_copies(e, s):
        return [pltpu.make_async_copy(w_hbm.at[layer, e], stage.at[s], sems.at[s, j])
                for j, (w_hbm, stage) in enumerate(((w1_hbm, w1s), (w3_hbm, w3s), (w2_hbm, w2s)))]

    @pl.when(i < nused_ref[0])
    def _():
        @pl.when(first_ref[i] == 1)
        def _():
            s = slot_ref[i]

            @pl.when(i == 0)
            def _():
                for cp in weight_copies(be_ref[0], 0):
                    cp.start()

            for cp in weight_copies(be_ref[i], s):
                cp.wait()

            @pl.when(next_ref[i] >= 0)
            def _():
                for cp in weight_copies(next_ref[i], 1 - s):
                    cp.start(priority=1)

            w1b[...] = w1s[s].astype(BF16)
            w3b[...] = w3s[s].astype(BF16)
            w2b[...] = w2s[s].astype(BF16)

        live = lax.broadcasted_iota(I32, (bm, LANE), 0) < valid_ref[i]
        for s, piece in enumerate(_load_row_tiles(xs_ref, 0, bm)):
            lo, hi = _unpack_bf16_pairs(jnp.where(live, piece, jnp.uint32(0)))
            xb[:, s * LANE:(s + 1) * LANE] = lo.astype(BF16)
            xb[:, PK_W + s * LANE:PK_W + (s + 1) * LANE] = hi.astype(BF16)
        x = xb[...]
        h1 = jnp.dot(x, w1b[...], preferred_element_type=F32)
        h3 = jnp.dot(x, w3b[...], preferred_element_type=F32)
        a = (h1 * jax.nn.sigmoid(h1) * h3).astype(BF16)
        y = jnp.dot(a, w2b[...], preferred_element_type=F32)
        _store_row_tiles(y_ref, _pack_bf16_pairs(y.astype(BF16)))


def moe_experts(block_e, valid, n_used, first, next_e, slot, xs, w1, w3, w2, layer, n_blocks):
    bm = EXP_BM

    def blk(i, be, va, nu, fi, ne, sl):
        return (jnp.minimum(i, nu[0] - 1), 0)

    grid_spec = pltpu.PrefetchScalarGridSpec(
        num_scalar_prefetch=6,
        grid=(n_blocks,),
        in_specs=[pl.BlockSpec((bm * PK_S, LANE), blk),
                  pl.BlockSpec(memory_space=pl.ANY),
                  pl.BlockSpec(memory_space=pl.ANY),
                  pl.BlockSpec(memory_space=pl.ANY)],
        out_specs=pl.BlockSpec((bm * PK_S, LANE), blk),
        scratch_shapes=[pltpu.VMEM((2, D, EXPERT_DIM), F32), pltpu.VMEM((2, D, EXPERT_DIM), F32),
                        pltpu.VMEM((2, EXPERT_DIM, D), F32),
                        pltpu.VMEM((D, EXPERT_DIM), BF16), pltpu.VMEM((D, EXPERT_DIM), BF16),
                        pltpu.VMEM((EXPERT_DIM, D), BF16), pltpu.VMEM((bm, D), BF16),
                        pltpu.SemaphoreType.DMA((2, 3))],
    )
    return pl.pallas_call(
        functools.partial(_expert_kernel, layer=layer),
        grid_spec=grid_spec,
        out_shape=jax.ShapeDtypeStruct((n_blocks * bm * PK_S, LANE), U32),
        compiler_params=_cp(("arbitrary",), vmem=56 * 1024 * 1024),
        name="moe_experts",
    )(block_e, valid, n_used, first, next_e, slot, xs, w1, w3, w2)


def _shared_kernel(x_ref, w1_ref, w3_ref, w2_ref, o_ref):
    x = x_ref[...]
    h1 = jnp.dot(x, w1_ref[...], preferred_element_type=F32)
    h3 = jnp.dot(x, w3_ref[...], preferred_element_type=F32)
    a = (h1 * jax.nn.sigmoid(h1) * h3).astype(BF16)
    o_ref[...] = jnp.dot(a, w2_ref[...], preferred_element_type=F32)


def shared_expert(hx, w1, w3, w2, n_tok):
    tm = 512
    return pl.pallas_call(
        _shared_kernel,
        grid=(n_tok // tm,),
        in_specs=[pl.BlockSpec((tm, D), lambda i: (i, 0)),
                  pl.BlockSpec((D, EXPERT_DIM), lambda i: (0, 0)),
                  pl.BlockSpec((D, EXPERT_DIM), lambda i: (0, 0)),
                  pl.BlockSpec((EXPERT_DIM, D), lambda i: (0, 0))],
        out_specs=pl.BlockSpec((tm, D), lambda i: (i, 0)),
        out_shape=jax.ShapeDtypeStruct((n_tok, D), F32),
        compiler_params=_cp(("arbitrary",)),
        name="shared_expert",
    )(hx, w1, w3, w2)


COMBINE_TM = 128


def _combine_kernel(slot0_ref, slotn_ref, w_ref, sh_ref, x_ref, mod_ref, y_hbm, *rest, final, n_tiles):
    fg_ref = rest[0] if final else None
    o_ref, buf, sems = rest[-3:]
    tm = COMBINE_TM
    i = pl.program_id(0)
    half_rows = TOP_K * tm * PK_S

    def start_gathers(slot_ref, par):
        half = buf.at[par]
        sem = sems.at[par]

        def issue(tt, carry):
            t8 = pl.multiple_of(tt * ROW_UNROLL, ROW_UNROLL)
            for j in range(ROW_UNROLL):
                for k in range(TOP_K):
                    row = pl.multiple_of(slot_ref[k * tm + j + t8] * PK_S, PK_S)
                    dst = half.at[pl.ds(pl.multiple_of((k * tm + j + t8) * PK_S, PK_S), PK_S), :]
                    pltpu.make_async_copy(y_hbm.at[pl.ds(row, PK_S), :], dst, sem).start(priority=k % 2)
            return carry

        lax.fori_loop(0, tm // ROW_UNROLL, issue, 0)

    @pl.when(i == 0)
    def _():
        start_gathers(slot0_ref, 0)

    @pl.when(i + 1 < n_tiles)
    def _():
        start_gathers(slotn_ref, (i + 1) % 2)

    cur = buf.at[i % 2]
    pltpu.make_async_copy(y_hbm.at[pl.ds(0, half_rows), :], cur, sems.at[i % 2]).wait()

    w = w_ref[...]
    gate = mod_ref[0][5:6]
    acc = [None] * (2 * PK_S)
    for k in range(TOP_K):
        wk = w[:, k:k + 1]
        for s, piece in enumerate(_load_row_tiles(cur, k * tm * PK_S, tm)):
            for c, val in zip((s, PK_S + s), _unpack_bf16_pairs(piece)):
                acc[c] = wk * val if acc[c] is None else acc[c] + wk * val
    outs = []
    for c in range(2 * PK_S):
        sl = slice(c * LANE, (c + 1) * LANE)
        outs.append(x_ref[:, sl] + gate[:, sl] * (sh_ref[:, sl] + acc[c]))
    if fg_ref is not None:
        ssq = functools.reduce(lambda a, b: a + b, [jnp.sum(o * o, axis=-1, keepdims=True) for o in outs])
        inv = lax.rsqrt(ssq / D + EPS)
        outs = [o * inv * fg_ref[:, c * LANE:(c + 1) * LANE] for c, o in enumerate(outs)]
    for c, o in enumerate(outs):
        o_ref[:, c * LANE:(c + 1) * LANE] = o


def moe_combine(slots, wts_tok, shared, x, mod, y, n_tok, final_gain=None):
    tm = COMBINE_TM
    n_tiles = n_tok // tm
    final = final_gain is not None
    in_specs = [pl.BlockSpec((TOP_K * tm,), lambda i: (0,), memory_space=pltpu.SMEM),
                pl.BlockSpec((TOP_K * tm,), lambda i: (jnp.minimum(i + 1, n_tiles - 1),), memory_space=pltpu.SMEM),
                pl.BlockSpec((tm, TOP_K), lambda i: (i, 0)),
                pl.BlockSpec((tm, D), lambda i: (i, 0)),
                pl.BlockSpec((tm, D), lambda i: (i, 0)),
                pl.BlockSpec((1, 6, D), lambda i: (_mod_row(i * tm), 0, 0)),
                pl.BlockSpec(memory_space=pl.ANY)]
    args = [slots, slots, wts_tok, shared, x, mod, y]
    if final:
        in_specs.append(pl.BlockSpec((1, D), lambda i: (0, 0)))
        args.append(final_gain.reshape(1, D))
    return pl.pallas_call(
        functools.partial(_combine_kernel, final=final, n_tiles=n_tiles),
        grid=(n_tiles,),
        in_specs=in_specs,
        out_specs=pl.BlockSpec((tm, D), lambda i: (i, 0)),
        out_shape=jax.ShapeDtypeStruct((n_tok, D), F32),
        scratch_shapes=[pltpu.VMEM((2, TOP_K * tm * PK_S, LANE), U32), pltpu.SemaphoreType.DMA((2,))],
        compiler_params=_cp(("arbitrary",)),
        name="moe_combine",
    )(*args)


def _lookup(table, idx):
    e = jnp.arange(table.shape[0], dtype=I32).reshape((-1,) + (1,) * idx.ndim)
    return jnp.sum(jnp.where(idx[None] == e, table.reshape(e.shape), 0), axis=0)


def _tile_flat(slots, tm):
    k, t = slots.shape
    return slots.reshape(k, t // tm, tm).transpose(1, 0, 2).reshape(-1)


def moe_layer(x, mod, norm_gain, router_w, router_b, exp_w1, exp_w3, exp_w2, sw1, sw3, sw2, layer, n_tok,
              final_gain=None):
    bm = EXP_BM
    n_blocks = -(-n_tok * TOP_K // bm) + N_EXPERTS
    hx, hx_packed = norm_mod(x, norm_gain, mod, 3, n_tok, pack=True)
    eidx, wts, pos, counts = moe_router(hx, router_w, router_b, n_tok)
    shared = shared_expert(hx, sw1.astype(BF16), sw3.astype(BF16), sw2.astype(BF16), n_tok)
    cnt = counts.astype(I32)
    padded = (cnt + bm - 1) // bm * bm
    pad_end = jnp.cumsum(padded)
    pad_start = pad_end - padded
    slots = _lookup(pad_start, eidx) + pos
    blk_row = jnp.arange(n_blocks, dtype=I32) * bm
    block_e = jnp.minimum(jnp.sum((pad_end[:, None] <= blk_row[None, :]).astype(I32), axis=0), N_EXPERTS - 1)
    valid = jnp.clip(_lookup(cnt, block_e) - (blk_row - _lookup(pad_start, block_e)), 0, bm).astype(I32)
    n_used = (pad_end[-1:] // bm).astype(I32)
    prev_e = jnp.concatenate([jnp.full((1,), -1, I32), block_e[:-1]])
    first = ((blk_row < pad_end[-1]) & (block_e != prev_e)).astype(I32)
    stage_slot = ((jnp.cumsum(first) - 1) % 2).astype(I32)
    eids = jnp.arange(N_EXPERTS, dtype=I32)
    later = jnp.where((eids[None, :] > eids[:, None]) & (padded[None, :] > 0), eids[None, :], N_EXPERTS)
    next_nonempty = jnp.min(later, axis=1)
    next_nonempty = jnp.where(next_nonempty == N_EXPERTS, -1, next_nonempty)
    next_e = _lookup(next_nonempty, block_e).astype(I32)
    xs = moe_dispatch(_tile_flat(slots, DISPATCH_TM), hx_packed, n_tok, n_blocks * bm)
    y = moe_experts(block_e, valid, n_used, first, next_e, stage_slot, xs, exp_w1, exp_w3, exp_w2, layer, n_blocks)
    return moe_combine(_tile_flat(slots, COMBINE_TM), wts.T, shared, x, mod, y, n_tok, final_gain)


def _rope_tables():
    t = jnp.arange(S, dtype=I32)
    row = (t // GRID_W).astype(F32)
    col = (t % GRID_W).astype(F32)
    n_freq = HD // 4
    inv_freq = ROPE_THETA ** (-jnp.arange(n_freq, dtype=F32) / n_freq)
    ang = jnp.concatenate([row[:, None] * inv_freq, col[:, None] * inv_freq], axis=-1)
    cosf = jnp.repeat(jnp.cos(ang), 2, axis=-1)
    sinf = jnp.stack([-jnp.sin(ang), jnp.sin(ang)], axis=-1).reshape(S, HD)
    return cosf, sinf


def kernel(x, c, ctx, c_ctx, ada_w, ada_b, norm_mix, norm_ffn, attn_w_in, attn_w_out, attn_rpb, attn_q_gain,
           attn_k_gain, ml_w_in, ml_w_out, ml_gate_b, ml_head_gain, router_w, router_b, exp_w1, exp_w3, exp_w2,
           sh_w1, sh_w3, sh_w2, final_norm_gain):
    depth = ada_w.shape[0]
    x_lat = x.reshape(T_LAT, D)
    x_ctx = ctx.reshape(T_CTX, D)
    cvec = jnp.concatenate([c, c_ctx[None], jnp.zeros((8 - B - 1, D), F32)], axis=0)
    mod_all = ada_ln(cvec, ada_w, ada_b).reshape(depth, 8, 6, D)
    cosf, sinf = _rope_tables()

    mod = mod_all[0]
    p = norm_matmul(x_lat, norm_mix[0], mod, attn_w_in[0].astype(BF16), emit_h=False, x_ctx=x_ctx)
    o_all = neighborhood_attention(p, na_bias_table(attn_rpb[0]))
    o_all = gqa_attention(p, cosf, sinf, attn_q_gain[0], attn_k_gain[0], o_all)
    o_all = ctx_attention(p, attn_q_gain[0], attn_k_gain[0], o_all)
    xa = matmul_gated_residual(o_all, attn_w_out[0].astype(BF16), x_lat, mod, 2, x_ctx=x_ctx)
    xa = moe_layer(xa, mod, norm_ffn[0], router_w[0], router_b[0], exp_w1, exp_w3, exp_w2,
                   sh_w1[0], sh_w3[0], sh_w2[0], 0, T_ALL)

    mod = mod_all[1]
    w_in = ml_w_in[0]
    p, hx = norm_matmul(xa, norm_mix[1], mod, w_in[:, :ML_MAIN].astype(BF16), emit_h=True)
    col, row = mlstm_gates(hx, w_in[:, ML_MAIN:], ml_gate_b[0])
    hdir = mlstm_scan(p, col, row)
    a = mlstm_readout(hdir, p, ml_head_gain[0])
    xl = matmul_gated_residual(a, ml_w_out[0].astype(BF16), xa, mod, 2)
    xl = moe_layer(xl, mod, norm_ffn[1], router_w[1], router_b[1], exp_w1, exp_w3, exp_w2,
                   sh_w1[1], sh_w3[1], sh_w2[1], 1, T_LAT, final_gain=final_norm_gain)
    return xl.reshape(B, S, D)
```

```python
import functools

import jax
import jax.numpy as jnp
from jax import lax
from jax.experimental import pallas as pl
from jax.experimental.pallas import tpu as pltpu

F32 = jnp.float32
BF16 = jnp.bfloat16
I32 = jnp.int32
U32 = jnp.uint32

D = 2048
B = 4
S = 4096
L = 256
T_LAT = B * S
T_CTX = B * L
T_ALL = T_LAT + T_CTX
GRID_W = 64
ROWS = S // GRID_W
HD = 128
NA_HEADS = 8
NA_WIN_ROWS = 8
NA_WIN_COLS = 16
GQA_Q_HEADS = 8
GQA_KV_HEADS = 2
GQA_GROUP = 4
ROPE_THETA = 10000.0
ATTN_IN = 4608
ML_HEADS = 8
ML_V = 256
ML_QK = 128
ML_MAIN = 6144
N_EXPERTS = 64
TOP_K = 8
N_GROUPS = 8
TOPK_GROUPS = 4
EXPERT_DIM = 512
ROUTED_SCALE = 2.5
EPS = 1e-6
NEG_INF = -1e30
ATT_SCALE = HD ** -0.5
LOG2E = 1.4426950408889634
ML_KSCALE = ML_QK ** -0.5

LANE = 128
NA_QROWS = 4
NA_SLAB = NA_QROWS + NA_WIN_ROWS - 1
NA_QB = NA_QROWS * GRID_W
NA_KB = NA_SLAB * GRID_W
ML_CH = 256
EXP_BM = 512
PK_W = D // 2
PK_S = PK_W // LANE
VMEM_LIMIT = 48 * 1024 * 1024


def _cp(sem, vmem=VMEM_LIMIT):
    return pltpu.CompilerParams(dimension_semantics=sem, vmem_limit_bytes=vmem)


def _pack_bf16_pairs(xb):
    u = pltpu.bitcast(xb.astype(F32), U32)
    return (u[:, PK_W:] & jnp.uint32(0xFFFF0000)) | (u[:, :PK_W] >> 16)


def _unpack_bf16_pairs(u):
    return pltpu.bitcast(u << 16, F32), pltpu.bitcast(u & jnp.uint32(0xFFFF0000), F32)


def _store_row_tiles(ref, words):
    rows = words.shape[0]
    for s in range(PK_S):
        ref[pl.ds(s, rows, stride=PK_S), :] = words[:, s * LANE:(s + 1) * LANE]


def _load_row_tiles(ref, start, rows):
    return [ref[pl.ds(start + s, rows, stride=PK_S), :] for s in range(PK_S)]


def _mod_row(start_row):
    return jnp.where(start_row < T_LAT, start_row // S, B)


def _ada_kernel(c_ref, w_ref, b_ref, o_ref):
    c = c_ref[...]
    a = (c * jax.nn.sigmoid(c)).astype(BF16)
    w = w_ref[0].astype(BF16)
    o_ref[0] = jnp.dot(a, w, preferred_element_type=F32) + b_ref[0]


def ada_ln(cvec, ada_w, ada_b):
    depth = ada_w.shape[0]
    n = ada_w.shape[2]
    tn = 1024
    return pl.pallas_call(
        _ada_kernel,
        grid=(depth, n // tn),
        in_specs=[pl.BlockSpec((8, D), lambda l, j: (0, 0)),
                  pl.BlockSpec((1, D, tn), lambda l, j: (l, 0, j)),
                  pl.BlockSpec((1, 1, tn), lambda l, j: (l, 0, j))],
        out_specs=pl.BlockSpec((1, 8, tn), lambda l, j: (l, 0, j)),
        out_shape=jax.ShapeDtypeStruct((depth, 8, n), F32),
        compiler_params=_cp(("arbitrary", "arbitrary")),
        name="ada_ln",
    )(cvec, ada_w, ada_b.reshape(depth, 1, n))


def _norm_mod_kernel(x_ref, g_ref, mod_ref, *out_refs, base, pack):
    x = x_ref[...]
    y = x * lax.rsqrt(jnp.mean(x * x, axis=-1, keepdims=True) + EPS) * g_ref[...]
    m = mod_ref[0]
    h = y * (1.0 + m[base + 1:base + 2]) + m[base:base + 1]
    hb = h.astype(BF16)
    out_refs[0][...] = hb
    if pack:
        _store_row_tiles(out_refs[1], _pack_bf16_pairs(hb))


def norm_mod(x, gain, mod, base, n_rows, pack):
    tm = 256
    out_shape = [jax.ShapeDtypeStruct((n_rows, D), BF16)]
    out_specs = [pl.BlockSpec((tm, D), lambda i: (i, 0))]
    if pack:
        out_shape.append(jax.ShapeDtypeStruct((n_rows * PK_S, LANE), U32))
        out_specs.append(pl.BlockSpec((tm * PK_S, LANE), lambda i: (i, 0)))
    res = pl.pallas_call(
        functools.partial(_norm_mod_kernel, base=base, pack=pack),
        grid=(n_rows // tm,),
        in_specs=[pl.BlockSpec((tm, D), lambda i: (i, 0)),
                  pl.BlockSpec((1, D), lambda i: (0, 0)),
                  pl.BlockSpec((1, 6, D), lambda i: (_mod_row(i * tm), 0, 0))],
        out_specs=out_specs,
        out_shape=out_shape,
        compiler_params=_cp(("arbitrary",)),
        name="norm_mod",
    )(x, gain.reshape(1, D), mod)
    return res if pack else res[0]


def _norm_mm_kernel(*refs, emit_h, n_lat):
    if n_lat is None:
        x_ref, g_ref, mod_ref, w_ref, o_ref = refs[:5]
        c_ref = None
    else:
        x_ref, c_ref, g_ref, mod_ref, w_ref, o_ref = refs[:6]
    hb_ref = refs[-1]

    def prologue(src_ref):
        x = src_ref[...]
        y = x * lax.rsqrt(jnp.mean(x * x, axis=-1, keepdims=True) + EPS) * g_ref[...]
        m = mod_ref[0]
        hb_ref[...] = (y * (1.0 + m[1:2]) + m[0:1]).astype(BF16)
        if emit_h:
            refs[-2][...] = hb_ref[...]

    first_col = pl.program_id(1) == 0
    if c_ref is None:
        pl.when(first_col)(lambda: prologue(x_ref))
    else:
        is_lat = pl.program_id(0) < n_lat
        pl.when(first_col & is_lat)(lambda: prologue(x_ref))
        pl.when(first_col & jnp.logical_not(is_lat))(lambda: prologue(c_ref))

    o_ref[...] = jnp.dot(hb_ref[...], w_ref[...], preferred_element_type=F32).astype(o_ref.dtype)


def norm_matmul(x, gain, mod, w, emit_h, x_ctx=None, tm=1024, tn=512):
    split = x_ctx is not None
    m = x.shape[0] + (x_ctx.shape[0] if split else 0)
    n = w.shape[1]
    n_lat = x.shape[0] // tm if split else None
    out_shape = [jax.ShapeDtypeStruct((m, n), BF16)]
    out_specs = [pl.BlockSpec((tm, tn), lambda i, j: (i, j))]
    if emit_h:
        out_shape.append(jax.ShapeDtypeStruct((m, D), BF16))
        out_specs.append(pl.BlockSpec((tm, D), lambda i, j: (i, 0)))
    if split:
        x_specs = [pl.BlockSpec((tm, D), lambda i, j: (jnp.minimum(i, n_lat - 1), 0)),
                   pl.BlockSpec((tm, D), lambda i, j: (0, 0))]
        x_args = [x, x_ctx]
    else:
        x_specs = [pl.BlockSpec((tm, D), lambda i, j: (i, 0))]
        x_args = [x]
    res = pl.pallas_call(
        functools.partial(_norm_mm_kernel, emit_h=emit_h, n_lat=n_lat),
        grid=(m // tm, n // tn),
        in_specs=x_specs + [pl.BlockSpec((1, D), lambda i, j: (0, 0)),
                            pl.BlockSpec((1, 6, D), lambda i, j: (_mod_row(i * tm), 0, 0)),
                            pl.BlockSpec((D, tn), lambda i, j: (0, j))],
        out_specs=out_specs,
        out_shape=out_shape,
        scratch_shapes=[pltpu.VMEM((tm, D), BF16)],
        compiler_params=_cp(("arbitrary", "arbitrary"), vmem=56 * 1024 * 1024),
        name="norm_matmul",
    )(*x_args, gain.reshape(1, D), mod, w)
    return res if emit_h else res[0]


def _mm_res_kernel(*refs, slot, n_lat):
    if n_lat is None:
        a_ref, w_ref, x_ref, mod_ref, o_ref = refs
        res = x_ref[...]
    else:
        a_ref, w_ref, x_ref, c_ref, mod_ref, o_ref = refs
        res = jnp.where(pl.program_id(0) < n_lat, x_ref[...], c_ref[...])
    acc = jnp.dot(a_ref[...], w_ref[...], preferred_element_type=F32)
    o_ref[...] = res + mod_ref[0][slot:slot + 1] * acc


def matmul_gated_residual(a, w, x, mod, slot, x_ctx=None, tm=1024, tn=512):
    m, k = a.shape
    n = w.shape[1]
    split = x_ctx is not None
    n_lat = x.shape[0] // tm if split else None
    if split:
        x_specs = [pl.BlockSpec((tm, tn), lambda i, j: (jnp.minimum(i, n_lat - 1), j)),
                   pl.BlockSpec((tm, tn), lambda i, j: (0, j))]
        x_args = [x, x_ctx]
    else:
        x_specs = [pl.BlockSpec((tm, tn), lambda i, j: (i, j))]
        x_args = [x]
    return pl.pallas_call(
        functools.partial(_mm_res_kernel, slot=slot, n_lat=n_lat),
        grid=(m // tm, n // tn),
        in_specs=[pl.BlockSpec((tm, k), lambda i, j: (i, 0)),
                  pl.BlockSpec((k, tn), lambda i, j: (0, j))] + x_specs
                 + [pl.BlockSpec((1, 6, tn), lambda i, j: (_mod_row(i * tm), 0, j))],
        out_specs=pl.BlockSpec((tm, tn), lambda i, j: (i, j)),
        out_shape=jax.ShapeDtypeStruct((m, n), F32),
        compiler_params=_cp(("arbitrary", "arbitrary")),
        name="matmul_gated_residual",
    )(a, w, *x_args, mod)


def _dot_nt(a, b):
    return lax.dot_general(a, b, (((1,), (1,)), ((), ())), preferred_element_type=F32)


def _rms_head(x, gain):
    return x * lax.rsqrt(jnp.mean(x * x, axis=-1, keepdims=True) + EPS) * gain


def _rope(x, cosf, sinf):
    lane = lax.broadcasted_iota(I32, x.shape, 1)
    nxt = pltpu.roll(x, LANE - 1, 1)
    prv = pltpu.roll(x, 1, 1)
    return x * cosf + jnp.where((lane & 1) == 0, nxt, prv) * sinf


def _softmax_av(parts):
    m = functools.reduce(jnp.maximum, [jnp.max(s, axis=-1, keepdims=True) for s, _ in parts])
    l = None
    o = None
    for s, v in parts:
        p = jnp.exp(s - m)
        li = jnp.sum(p, axis=-1, keepdims=True)
        oi = jnp.dot(p.astype(BF16), v, preferred_element_type=F32)
        l = li if l is None else l + li
        o = oi if o is None else o + oi
    return o / l


def _na_kernel(q_ref, k_ref, v_ref, kc_ref, vc_ref, tab_ref, o_ref):
    kc = kc_ref[...]
    vc = vc_ref[...]
    n_blocks = ROWS // NA_QROWS

    def body(j, carry):
        ks = jnp.clip(j * NA_QROWS - NA_WIN_ROWS // 2, 0, ROWS - NA_SLAB)
        typ = jnp.where(j == 0, 0, jnp.where(j == n_blocks - 1, 2, 1))
        qs = pl.multiple_of(j * NA_QB, NA_QB)
        kst = pl.multiple_of(ks * GRID_W, GRID_W)
        q = q_ref[pl.ds(qs, NA_QB), :]
        k = k_ref[pl.ds(kst, NA_KB), :]
        v = v_ref[pl.ds(kst, NA_KB), :]
        s_win = _dot_nt(q, k) * ATT_SCALE + tab_ref[typ, 0]
        s_ctx = _dot_nt(q, kc) * ATT_SCALE
        o_ref[pl.ds(qs, NA_QB), :] = _softmax_av([(s_win, v), (s_ctx, vc)]).astype(BF16)
        return carry

    lax.fori_loop(0, n_blocks, body, 0)


def na_bias_table(rpb):
    def one(r0, ks):
        r = r0 + jnp.arange(NA_QROWS)
        kr = ks + jnp.arange(NA_SLAB)
        start = jnp.clip(r - NA_WIN_ROWS // 2, 0, ROWS - NA_WIN_ROWS)
        row_ok = (kr[None, :] >= start[:, None]) & (kr[None, :] < start[:, None] + NA_WIN_ROWS)
        row_idx = jnp.clip(kr[None, :] - r[:, None] + NA_WIN_ROWS - 1, 0, 2 * NA_WIN_ROWS - 2)
        cq = jnp.arange(GRID_W)
        col_start = jnp.clip(cq - NA_WIN_COLS // 2, 0, GRID_W - NA_WIN_COLS)
        col_ok = (cq[None, :] >= col_start[:, None]) & (cq[None, :] < col_start[:, None] + NA_WIN_COLS)
        col_idx = jnp.clip(cq[None, :] - cq[:, None] + NA_WIN_COLS - 1, 0, 2 * NA_WIN_COLS - 2)
        r_hot = jax.nn.one_hot(row_idx, 2 * NA_WIN_ROWS - 1, dtype=F32)
        c_hot = jax.nn.one_hot(col_idx, 2 * NA_WIN_COLS - 1, dtype=F32)
        bias = jnp.einsum('qka,hab,xyb->hqxky', r_hot, rpb.astype(F32), c_hot, precision=lax.Precision.HIGHEST)
        ok = row_ok[:, None, :, None] & col_ok[None, :, None, :]
        return jnp.where(ok[None], bias, NEG_INF).reshape(NA_HEADS, NA_QB, NA_KB)

    mid = 2 * NA_QROWS
    last = ROWS - NA_QROWS
    return jnp.stack([one(0, 0), one(mid, mid - NA_WIN_ROWS // 2), one(last, ROWS - NA_SLAB)])


def neighborhood_attention(p, table):
    cb = S // L
    return pl.pallas_call(
        _na_kernel,
        grid=(NA_HEADS, B),
        in_specs=[pl.BlockSpec((S, HD), lambda h, b: (b, h)),
                  pl.BlockSpec((S, HD), lambda h, b: (b, NA_HEADS + h)),
                  pl.BlockSpec((S, HD), lambda h, b: (b, 2 * NA_HEADS + h)),
                  pl.BlockSpec((L, HD), lambda h, b: (B * cb + b, NA_HEADS + h)),
                  pl.BlockSpec((L, HD), lambda h, b: (B * cb + b, 2 * NA_HEADS + h)),
                  pl.BlockSpec((3, 1, NA_QB, NA_KB), lambda h, b: (0, h, 0, 0))],
        out_specs=pl.BlockSpec((S, HD), lambda h, b: (b, h)),
        out_shape=jax.ShapeDtypeStruct((T_ALL, D), BF16),
        compiler_params=_cp(("arbitrary", "arbitrary")),
        name="neighborhood_attention",
    )(p, p, p, p, p, table)


GQA_TQ = 512
GQA_CK = 1024
GQA_QCOL = 3 * NA_HEADS
GQA_KCOL = GQA_QCOL + GQA_Q_HEADS
GQA_VCOL = GQA_KCOL + GQA_KV_HEADS


def _gqa_kernel(q_ref, k_ref, v_ref, kc_ref, vc_ref, cq_ref, sq_ref, ck_ref, sk_ref, qg_ref, kg_ref, o_prev,
                o_ref, kn_ref, kcn_ref):
    del o_prev
    @pl.when(pl.program_id(2) == 0)
    def _():
        kn = _rope(_rms_head(k_ref[...].astype(F32), kg_ref[...]), ck_ref[...], sk_ref[...])
        kn_ref[...] = kn.astype(BF16)
        kcn_ref[...] = _rms_head(kc_ref[...].astype(F32), kg_ref[...]).astype(BF16)

    cos = cq_ref[...]
    sin = sq_ref[...]
    heads = []
    for g in range(GQA_GROUP):
        qh = _rope(_rms_head(q_ref[:, g * HD:(g + 1) * HD].astype(F32), qg_ref[...]), cos, sin)
        heads.append((qh * (ATT_SCALE * LOG2E)).astype(BF16))
    q = jnp.concatenate(heads, axis=0)
    chunks = [(kn_ref[c * GQA_CK:(c + 1) * GQA_CK, :], v_ref[c * GQA_CK:(c + 1) * GQA_CK, :])
              for c in range(S // GQA_CK)]
    chunks.append((kcn_ref[...], vc_ref[...]))
    m = l = acc = None
    for kk, vv in chunks:
        s = _dot_nt(q, kk)
        mc = jnp.max(s, axis=-1, keepdims=True)
        if m is None:
            m_new = mc
            p = jnp.exp2(s - m_new)
            l = jnp.sum(p, axis=-1, keepdims=True)
            acc = jnp.dot(p.astype(BF16), vv, preferred_element_type=F32)
        else:
            m_new = jnp.maximum(m, mc)
            alpha = jnp.exp2(m - m_new)
            p = jnp.exp2(s - m_new)
            l = alpha * l + jnp.sum(p, axis=-1, keepdims=True)
            acc = alpha * acc + jnp.dot(p.astype(BF16), vv, preferred_element_type=F32)
        m = m_new
    o = acc / l
    for g in range(GQA_GROUP):
        o_ref[:, g * HD:(g + 1) * HD] = o[g * GQA_TQ:(g + 1) * GQA_TQ].astype(BF16)


def gqa_attention(p, cosf, sinf, q_gain, k_gain, o_buf):
    nq = S // GQA_TQ
    cb = S // L
    gw = GQA_GROUP * HD
    return pl.pallas_call(
        _gqa_kernel,
        grid=(B, GQA_KV_HEADS, nq),
        in_specs=[pl.BlockSpec((GQA_TQ, gw), lambda b, n, i: (b * nq + i, GQA_QCOL // GQA_GROUP + n)),
                  pl.BlockSpec((S, HD), lambda b, n, i: (b, GQA_KCOL + n)),
                  pl.BlockSpec((S, HD), lambda b, n, i: (b, GQA_VCOL + n)),
                  pl.BlockSpec((L, HD), lambda b, n, i: (B * cb + b, GQA_KCOL + n)),
                  pl.BlockSpec((L, HD), lambda b, n, i: (B * cb + b, GQA_VCOL + n)),
                  pl.BlockSpec((GQA_TQ, HD), lambda b, n, i: (i, 0)),
                  pl.BlockSpec((GQA_TQ, HD), lambda b, n, i: (i, 0)),
                  pl.BlockSpec((S, HD), lambda b, n, i: (0, 0)),
                  pl.BlockSpec((S, HD), lambda b, n, i: (0, 0)),
                  pl.BlockSpec((1, HD), lambda b, n, i: (0, 0)),
                  pl.BlockSpec((1, HD), lambda b, n, i: (0, 0)),
                  pl.BlockSpec(memory_space=pl.ANY)],
        out_specs=pl.BlockSpec((GQA_TQ, gw), lambda b, n, i: (b * nq + i, (NA_HEADS * HD) // gw + n)),
        out_shape=jax.ShapeDtypeStruct((T_ALL, D), BF16),
        input_output_aliases={11: 0},
        scratch_shapes=[pltpu.VMEM((S, HD), BF16), pltpu.VMEM((L, HD), BF16)],
        compiler_params=_cp(("arbitrary",) * 3),
        name="gqa_attention",
    )(p, p, p, p, p, cosf, sinf, cosf, sinf, q_gain.reshape(1, HD), k_gain.reshape(1, HD), o_buf)


def _ctx_attn_kernel(p_ref, qg_ref, kg_ref, o_prev, o_ref):
    del o_prev

    def col(c):
        return p_ref[:, c * HD:(c + 1) * HD]

    for h in range(NA_HEADS):
        s = _dot_nt(col(h), col(NA_HEADS + h)) * ATT_SCALE
        o_ref[:, h * HD:(h + 1) * HD] = _softmax_av([(s, col(2 * NA_HEADS + h))]).astype(BF16)
    for n in range(GQA_KV_HEADS):
        kn = _rms_head(col(GQA_KCOL + n).astype(F32), kg_ref[...]).astype(BF16)
        v = col(GQA_VCOL + n)
        for g in range(GQA_GROUP):
            h = n * GQA_GROUP + g
            qn = _rms_head(col(GQA_QCOL + h).astype(F32), qg_ref[...]).astype(BF16)
            s = _dot_nt(qn, kn) * ATT_SCALE
            o_ref[:, (NA_HEADS + h) * HD:(NA_HEADS + h + 1) * HD] = _softmax_av([(s, v)]).astype(BF16)


def ctx_attention(p, q_gain, k_gain, o_buf):
    cb = S // L
    return pl.pallas_call(
        _ctx_attn_kernel,
        grid=(B,),
        in_specs=[pl.BlockSpec((L, ATTN_IN), lambda b: (B * cb + b, 0)),
                  pl.BlockSpec((1, HD), lambda b: (0, 0)),
                  pl.BlockSpec((1, HD), lambda b: (0, 0)),
                  pl.BlockSpec(memory_space=pl.ANY)],
        out_specs=pl.BlockSpec((L, D), lambda b: (B * cb + b, 0)),
        out_shape=jax.ShapeDtypeStruct((T_ALL, D), BF16),
        input_output_aliases={3: 0},
        compiler_params=_cp(("arbitrary",)),
        name="ctx_attention",
    )(p, q_gain.reshape(1, HD), k_gain.reshape(1, HD), o_buf)


def _log_sigmoid(x):
    return -(jnp.maximum(-x, 0.0) + jnp.log1p(jnp.exp(-jnp.abs(x))))


def _dot_hi(a, b):
    return jnp.dot(a, b, precision=lax.Precision.HIGHEST, preferred_element_type=F32)


def _gate_kernel(h_ref, wg_ref, wgt_ref, b_ref, bt_ref, lt_ref, ut_ref, col_ref, row_ref):
    nh = ML_HEADS
    hx = h_ref[...]
    g = jnp.dot(hx, wg_ref[...], preferred_element_type=F32) + b_ref[...]
    gt = _dot_nt(wgt_ref[...], hx) + bt_ref[...]
    li = g[:, 0:2 * nh]
    lf = _log_sigmoid(g[:, 2 * nh:4 * nh])
    lit = gt[0:2 * nh]
    lft = _log_sigmoid(gt[2 * nh:4 * nh])
    lt = lt_ref[...]
    ut = ut_ref[...]
    lane = lax.broadcasted_iota(I32, lf.shape, 1)
    bc = jnp.where(lane < nh, _dot_hi(lt, lf), _dot_hi(ut, lf))
    tot = jnp.sum(lf, axis=0, keepdims=True)
    aend = tot - bc + li
    col_ref[...] = jnp.concatenate([bc, aend, jnp.zeros((ML_CH, LANE - 4 * nh), F32)], axis=1)
    sub = lax.broadcasted_iota(I32, lft.shape, 0)
    bct = jnp.where(sub < nh, _dot_hi(lft, ut), _dot_hi(lft, lt))
    tott = jnp.sum(lft, axis=1, keepdims=True)
    gtr = lit - bct
    row_ref[0] = jnp.concatenate([bct, gtr, tott + gtr, jnp.broadcast_to(tott, bct.shape)], axis=0)


def mlstm_gates(hx, wg, gate_b):
    nh = ML_HEADS
    n_ch = T_ALL // ML_CH
    wg_pad = jnp.zeros((D, LANE), BF16).at[:, :4 * nh].set(wg.astype(BF16))
    b_pad = jnp.zeros((1, LANE), F32).at[0, :4 * nh].set(gate_b.reshape(-1))
    wgt = wg.astype(BF16).T
    bt = gate_b.reshape(4 * nh, 1).astype(F32)
    lt = jnp.tril(jnp.ones((ML_CH, ML_CH), F32))
    ut = jnp.triu(jnp.ones((ML_CH, ML_CH), F32))
    col, row = pl.pallas_call(
        _gate_kernel,
        grid=(n_ch,),
        in_specs=[pl.BlockSpec((ML_CH, D), lambda i: (i, 0)),
                  pl.BlockSpec((D, LANE), lambda i: (0, 0)),
                  pl.BlockSpec((4 * nh, D), lambda i: (0, 0)),
                  pl.BlockSpec((1, LANE), lambda i: (0, 0)),
                  pl.BlockSpec((4 * nh, 1), lambda i: (0, 0)),
                  pl.BlockSpec((ML_CH, ML_CH), lambda i: (0, 0)),
                  pl.BlockSpec((ML_CH, ML_CH), lambda i: (0, 0))],
        out_specs=[pl.BlockSpec((ML_CH, LANE), lambda i: (i, 0)),
                   pl.BlockSpec((1, 8 * nh, ML_CH), lambda i: (i, 0, 0))],
        out_shape=[jax.ShapeDtypeStruct((T_ALL, LANE), F32),
                   jax.ShapeDtypeStruct((n_ch, 8 * nh, ML_CH), F32)],
        compiler_params=_cp(("arbitrary",)),
        name="mlstm_gates",
    )(hx, wg_pad, wgt, b_pad, bt, lt, ut)
    return col, row


def _mlstm_step(d, hh, q_ref, k_ref, v_ref, col_ref, row_ref, o_ref, c_ref, n_ref, m_ref):
    nd = 2 * ML_HEADS
    sl = d * ML_HEADS + hh
    q = q_ref[:, hh * ML_QK:(hh + 1) * ML_QK]
    kf = k_ref[:, hh * ML_QK:(hh + 1) * ML_QK].astype(F32) * ML_KSCALE
    kb = kf.astype(BF16)
    v = v_ref[:, hh * ML_V:(hh + 1) * ML_V]
    bc_col = col_ref[:, sl:sl + 1]
    aend_col = col_ref[:, nd + sl:nd + sl + 1]
    g_row = row_ref[0, nd + sl:nd + sl + 1, :]
    aend_row = row_ref[0, 2 * nd + sl:2 * nd + sl + 1, :]
    btot = row_ref[0, 3 * nd + sl:3 * nd + sl + 1, 0:1]
    m_st = m_ref[sl]
    c_st = c_ref[sl]
    n_st = n_ref[sl]
    m_new = jnp.maximum(btot + m_st, jnp.max(aend_row, axis=1, keepdims=True))

    r = lax.broadcasted_iota(I32, (ML_CH, ML_CH), 0)
    c = lax.broadcasted_iota(I32, (ML_CH, ML_CH), 1)
    causal = (r >= c) if d == 0 else (r <= c)
    d_mat = jnp.where(causal, bc_col + g_row, -jnp.inf)
    m_row = jnp.maximum(bc_col + m_st, jnp.max(d_mat, axis=1, keepdims=True))
    w_inter = jnp.exp(bc_col + m_st - m_row)
    s_mat = _dot_nt(q, kb) * jnp.exp(d_mat - m_row)
    num = (w_inter * jnp.dot(q, c_st.astype(BF16), preferred_element_type=F32)
           + jnp.dot(s_mat.astype(BF16), v, preferred_element_type=F32))
    den = (w_inter * jnp.sum(q.astype(F32) * n_st, axis=1, keepdims=True)
           + jnp.sum(s_mat, axis=1, keepdims=True))
    o_ref[:, hh * ML_V:(hh + 1) * ML_V] = num / jnp.maximum(jnp.abs(den), jnp.exp(-m_row))

    w_end_col = jnp.exp(aend_col - m_new)
    w_end_row = jnp.exp(aend_row - m_new)
    decay = jnp.exp(btot + m_st - m_new)
    kw = (kf * w_end_col).astype(BF16)
    c_ref[sl] = decay * c_st + lax.dot_general(kw, v, (((0,), (0,)), ((), ())), preferred_element_type=F32)
    w8 = jnp.broadcast_to(w_end_row, (8, ML_CH)).astype(BF16)
    n_ref[sl] = decay * n_st + jnp.dot(w8, kb, preferred_element_type=F32)[0:1]
    m_ref[sl] = m_new


def _mlstm_kernel(qf, kf, vf, colf, rowf, qb, kb, vb, colb, rowb, of, ob, c_ref, n_ref, m_ref):
    @pl.when(pl.program_id(1) == 0)
    def _():
        c_ref[...] = jnp.zeros_like(c_ref)
        n_ref[...] = jnp.zeros_like(n_ref)
        m_ref[...] = jnp.zeros_like(m_ref)

    for hh in range(ML_HEADS):
        _mlstm_step(0, hh, qf, kf, vf, colf, rowf, of, c_ref, n_ref, m_ref)
        _mlstm_step(1, hh, qb, kb, vb, colb, rowb, ob, c_ref, n_ref, m_ref)


def mlstm_scan(p, col, row):
    n_lat = S // ML_CH
    steps = n_lat + 1
    lat_blocks = T_LAT // ML_CH
    qk_w = ML_HEADS * ML_QK
    v_w = ML_HEADS * ML_V
    n_chains = 2 * ML_HEADS

    def chunk(b, d, st):
        c = (st - 1) if d == 0 else (n_lat - st)
        return jnp.where(st == 0, lat_blocks + b, b * n_lat + c)

    def out_chunk(b, d, st):
        s1 = jnp.maximum(st, 1)
        return b * n_lat + ((s1 - 1) if d == 0 else (n_lat - s1))

    def dir_specs(d):
        return [pl.BlockSpec((ML_CH, qk_w), lambda b, s: (chunk(b, d, s), 0)),
                pl.BlockSpec((ML_CH, qk_w), lambda b, s: (chunk(b, d, s), 1)),
                pl.BlockSpec((ML_CH, v_w), lambda b, s: (chunk(b, d, s), (2 * qk_w) // v_w)),
                pl.BlockSpec((ML_CH, LANE), lambda b, s: (chunk(b, d, s), 0)),
                pl.BlockSpec((1, 4 * n_chains, ML_CH), lambda b, s: (chunk(b, d, s), 0, 0))]

    return pl.pallas_call(
        _mlstm_kernel,
        grid=(B, steps),
        in_specs=dir_specs(0) + dir_specs(1),
        out_specs=[pl.BlockSpec((ML_CH, v_w), lambda b, s: (out_chunk(b, 0, s), 0)),
                   pl.BlockSpec((ML_CH, v_w), lambda b, s: (out_chunk(b, 1, s), 0))],
        out_shape=[jax.ShapeDtypeStruct((T_LAT, v_w), F32), jax.ShapeDtypeStruct((T_LAT, v_w), F32)],
        scratch_shapes=[pltpu.VMEM((n_chains, ML_QK, ML_V), F32), pltpu.VMEM((n_chains, 1, ML_QK), F32),
                        pltpu.VMEM((n_chains, 1, 1), F32)],
        compiler_params=_cp(("arbitrary",) * 2),
        name="mlstm_scan",
    )(p, p, p, col, row, p, p, p, col, row)


def _readout_kernel(hf_ref, hb_ref, o_ref, g_ref, a_ref):
    hs = hf_ref[...] + hb_ref[...]
    for h in range(ML_HEADS):
        sl = slice(h * ML_V, (h + 1) * ML_V)
        x = hs[:, sl]
        hn = x * lax.rsqrt(jnp.mean(x * x, axis=-1, keepdims=True) + EPS) * g_ref[:, sl]
        a_ref[:, sl] = (hn * jax.nn.sigmoid(o_ref[:, sl].astype(F32))).astype(BF16)


def mlstm_readout(hdir, p, head_gain):
    tm = 256
    ocol = (2 * ML_HEADS * ML_QK + ML_HEADS * ML_V) // D
    return pl.pallas_call(
        _readout_kernel,
        grid=(T_LAT // tm,),
        in_specs=[pl.BlockSpec((tm, D), lambda i: (i, 0)),
                  pl.BlockSpec((tm, D), lambda i: (i, 0)),
                  pl.BlockSpec((tm, D), lambda i: (i, ocol)),
                  pl.BlockSpec((1, D), lambda i: (0, 0))],
        out_specs=pl.BlockSpec((tm, D), lambda i: (i, 0)),
        out_shape=jax.ShapeDtypeStruct((T_LAT, D), BF16),
        compiler_params=_cp(("arbitrary",)),
        name="mlstm_readout",
    )(hdir[0], hdir[1], p, head_gain.reshape(1, D))


ROUTER_TM = 512


def _router_kernel(h_ref, w_ref, rb_ref, erow_ref, tri_ref, eidx_ref, wts_ref, pos_ref, cnt_ref, carry_ref):
    ng = N_GROUPS
    epg = N_EXPERTS // N_GROUPS
    tm = ROUTER_TM
    ninf = -jnp.inf

    @pl.when(pl.program_id(0) == 0)
    def _():
        carry_ref[...] = jnp.zeros_like(carry_ref)

    s = jax.nn.sigmoid(_dot_nt(w_ref[...], h_ref[...]))
    ssel = s + rb_ref[...]
    sraw = [s[ng * j:ng * (j + 1)] for j in range(epg)]
    slab = [ssel[ng * j:ng * (j + 1)] for j in range(epg)]
    m1 = functools.reduce(jnp.maximum, slab)
    jfirst = functools.reduce(jnp.minimum, [jnp.where(slab[j] == m1, j, epg) for j in range(epg)])
    m2 = functools.reduce(jnp.maximum, [jnp.where(jfirst == j, ninf, slab[j]) for j in range(epg)])
    gs = m1 + m2
    giota = lax.broadcasted_iota(I32, (ng, tm), 0)
    gsel = jnp.zeros((ng, tm), F32)
    for _ in range(TOPK_GROUPS):
        mx = jnp.max(gs, axis=0, keepdims=True)
        gi = jnp.min(jnp.where(gs == mx, giota, ng), axis=0, keepdims=True)
        hit = giota == gi
        gsel = jnp.where(hit, 1.0, gsel)
        gs = jnp.where(hit, ninf, gs)
    msl = [jnp.where(gsel > 0.0, slab[j], ninf) for j in range(epg)]
    eid = [giota * epg + j for j in range(epg)]
    selm = [jnp.zeros((ng, tm), F32) for _ in range(epg)]
    e_list, w_list = [], []
    for _ in range(TOP_K):
        mx = jnp.max(functools.reduce(jnp.maximum, msl), axis=0, keepdims=True)
        cand = functools.reduce(jnp.minimum, [jnp.where(msl[j] == mx, eid[j], N_EXPERTS) for j in range(epg)])
        esel = jnp.min(cand, axis=0, keepdims=True)
        hits = [eid[j] == esel for j in range(epg)]
        wk = functools.reduce(lambda a, b: a + b, [jnp.where(hits[j], sraw[j], 0.0) for j in range(epg)])
        w_list.append(jnp.sum(wk, axis=0, keepdims=True))
        e_list.append(esel)
        msl = [jnp.where(hits[j], ninf, msl[j]) for j in range(epg)]
        selm = [jnp.where(hits[j], 1.0, selm[j]) for j in range(epg)]
    wsum = functools.reduce(lambda a, b: a + b, w_list)
    wts_ref[...] = jnp.concatenate([w / wsum * ROUTED_SCALE for w in w_list], axis=0)
    eidx_ref[...] = jnp.concatenate(e_list, axis=0)
    sel = jnp.concatenate(selm, axis=0)
    carry = carry_ref[...]
    posfull = jnp.dot(sel.astype(BF16), tri_ref[...], preferred_element_type=F32) + carry
    erow = erow_ref[...]
    pos = [jnp.sum(jnp.where(erow == e, posfull, 0.0), axis=0, keepdims=True) for e in e_list]
    pos_ref[...] = jnp.concatenate(pos, axis=0).astype(I32)
    carry = carry + jnp.sum(sel, axis=1, keepdims=True)
    carry_ref[...] = carry
    cnt_ref[...] = carry


def moe_router(hx, router_w, router_b, n_tok):
    tm = ROUTER_TM
    epg = N_EXPERTS // N_GROUPS
    perm = (jnp.arange(N_EXPERTS) % N_GROUPS) * epg + jnp.arange(N_EXPERTS) // N_GROUPS
    w_t = router_w.astype(BF16).T[perm]
    rb = router_b.astype(F32)[perm].reshape(N_EXPERTS, 1)
    erow = perm.astype(I32).reshape(N_EXPERTS, 1)
    tri = jnp.triu(jnp.ones((tm, tm), BF16), 1)
    eidx, wts, pos, counts = pl.pallas_call(
        _router_kernel,
        grid=(n_tok // tm,),
        in_specs=[pl.BlockSpec((tm, D), lambda i: (i, 0)),
                  pl.BlockSpec((N_EXPERTS, D), lambda i: (0, 0)),
                  pl.BlockSpec((N_EXPERTS, 1), lambda i: (0, 0)),
                  pl.BlockSpec((N_EXPERTS, 1), lambda i: (0, 0)),
                  pl.BlockSpec((tm, tm), lambda i: (0, 0))],
        out_specs=[pl.BlockSpec((TOP_K, tm), lambda i: (0, i)),
                   pl.BlockSpec((TOP_K, tm), lambda i: (0, i)),
                   pl.BlockSpec((TOP_K, tm), lambda i: (0, i)),
                   pl.BlockSpec((N_EXPERTS, 1), lambda i: (0, 0))],
        out_shape=[jax.ShapeDtypeStruct((TOP_K, n_tok), I32),
                   jax.ShapeDtypeStruct((TOP_K, n_tok), F32),
                   jax.ShapeDtypeStruct((TOP_K, n_tok), I32),
                   jax.ShapeDtypeStruct((N_EXPERTS, 1), F32)],
        scratch_shapes=[pltpu.VMEM((N_EXPERTS, 1), F32)],
        compiler_params=_cp(("arbitrary",)),
        name="moe_router",
    )(hx, w_t, rb, erow, tri)
    return eidx, wts, pos, counts.reshape(N_EXPERTS)[perm]


DISPATCH_TM = 512


ROW_UNROLL = 8


def _dispatch_kernel(slot_ref, hx_ref, xs_hbm, sem):
    def issue(tt, carry):
        t8 = pl.multiple_of(tt * ROW_UNROLL, ROW_UNROLL)
        for j in range(ROW_UNROLL):
            src = hx_ref.at[pl.ds(pl.multiple_of((t8 + j) * PK_S, PK_S), PK_S), :]
            for k in range(TOP_K):
                row = pl.multiple_of(slot_ref[k * DISPATCH_TM + j + t8] * PK_S, PK_S)
                pltpu.make_async_copy(src, xs_hbm.at[pl.ds(row, PK_S), :], sem).start(priority=k % 2)
        return carry

    lax.fori_loop(0, DISPATCH_TM // ROW_UNROLL, issue, 0)
    for _ in range(TOP_K):
        pltpu.make_async_copy(hx_ref, xs_hbm.at[pl.ds(0, DISPATCH_TM * PK_S), :], sem).wait()


def moe_dispatch(slots, hx_packed, n_tok, n_rows):
    tm = DISPATCH_TM
    return pl.pallas_call(
        _dispatch_kernel,
        grid=(n_tok // tm,),
        in_specs=[pl.BlockSpec((TOP_K * tm,), lambda i: (i,), memory_space=pltpu.SMEM),
                  pl.BlockSpec((tm * PK_S, LANE), lambda i: (i, 0))],
        out_specs=pl.BlockSpec(memory_space=pl.ANY),
        out_shape=jax.ShapeDtypeStruct((n_rows * PK_S, LANE), U32),
        scratch_shapes=[pltpu.SemaphoreType.DMA(())],
        compiler_params=_cp(("arbitrary",)),
        name="moe_dispatch",
    )(slots, hx_packed)


def _expert_kernel(be_ref, valid_ref, nused_ref, first_ref, next_ref, slot_ref,
                   xs_ref, w1_hbm, w3_hbm, w2_hbm, y_ref, w1s, w3s, w2s, w1b, w3b, w2b, xb, sems, *, layer):
    i = pl.program_id(0)
    bm = EXP_BM

    def weight_copies(e, s):
        return [pltpu.make_async_copy(w_hbm.at[layer, e], stage.at[s], sems.at[s, j])
                for j, (w_hbm, stage) in enumerate(((w1_hbm, w1s), (w3_hbm, w3s), (w2_hbm, w2s)))]

    @pl.when(i < nused_ref[0])
    def _():
        @pl.when(first_ref[i] == 1)
        def _():
            s = slot_ref[i]

            @pl.when(i == 0)
            def _():
                for cp in weight_copies(be_ref[0], 0):
                    cp.start()

            for cp in weight_copies(be_ref[i], s):
                cp.wait()

            @pl.when(next_ref[i] >= 0)
            def _():
                for cp in weight_copies(next_ref[i], 1 - s):
                    cp.start(priority=1)

            w1b[...] = w1s[s].astype(BF16)
            w3b[...] = w3s[s].astype(BF16)
            w2b[...] = w2s[s].astype(BF16)

        live = lax.broadcasted_iota(I32, (bm, LANE), 0) < valid_ref[i]
        for s, piece in enumerate(_load_row_tiles(xs_ref, 0, bm)):
            lo, hi = _unpack_bf16_pairs(jnp.where(live, piece, jnp.uint32(0)))
            xb[:, s * LANE:(s + 1) * LANE] = lo.astype(BF16)
            xb[:, PK_W + s * LANE:PK_W + (s + 1) * LANE] = hi.astype(BF16)
        x = xb[...]
        h1 = jnp.dot(x, w1b[...], preferred_element_type=F32)
        h3 = jnp.dot(x, w3b[...], preferred_element_type=F32)
        a = (h1 * jax.nn.sigmoid(h1) * h3).astype(BF16)
        y = jnp.dot(a, w2b[...], preferred_element_type=F32)
        _store_row_tiles(y_ref, _pack_bf16_pairs(y.astype(BF16)))


def moe_experts(block_e, valid, n_used, first, next_e, slot, xs, w1, w3, w2, layer, n_blocks):
    bm = EXP_BM

    def blk(i, be, va, nu, fi, ne, sl):
        return (jnp.minimum(i, nu[0] - 1), 0)

    grid_spec = pltpu.PrefetchScalarGridSpec(
        num_scalar_prefetch=6,
        grid=(n_blocks,),
        in_specs=[pl.BlockSpec((bm * PK_S, LANE), blk),
                  pl.BlockSpec(memory_space=pl.ANY),
                  pl.BlockSpec(memory_space=pl.ANY),
                  pl.BlockSpec(memory_space=pl.ANY)],
        out_specs=pl.BlockSpec((bm * PK_S, LANE), blk),
        scratch_shapes=[pltpu.VMEM((2, D, EXPERT_DIM), F32), pltpu.VMEM((2, D, EXPERT_DIM), F32),
                        pltpu.VMEM((2, EXPERT_DIM, D), F32),
                        pltpu.VMEM((D, EXPERT_DIM), BF16), pltpu.VMEM((D, EXPERT_DIM), BF16),
                        pltpu.VMEM((EXPERT_DIM, D), BF16), pltpu.VMEM((bm, D), BF16),
                        pltpu.SemaphoreType.DMA((2, 3))],
    )
    return pl.pallas_call(
        functools.partial(_expert_kernel, layer=layer),
        grid_spec=grid_spec,
        out_shape=jax.ShapeDtypeStruct((n_blocks * bm * PK_S, LANE), U32),
        compiler_params=_cp(("arbitrary",), vmem=56 * 1024 * 1024),
        name="moe_experts",
    )(block_e, valid, n_used, first, next_e, slot, xs, w1, w3, w2)


def _shared_kernel(x_ref, w1_ref, w3_ref, w2_ref, o_ref):
    x = x_ref[...]
    h1 = jnp.dot(x, w1_ref[...], preferred_element_type=F32)
    h3 = jnp.dot(x, w3_ref[...], preferred_element_type=F32)
    a = (h1 * jax.nn.sigmoid(h1) * h3).astype(BF16)
    o_ref[...] = jnp.dot(a, w2_ref[...], preferred_element_type=F32)


def shared_expert(hx, w1, w3, w2, n_tok):
    tm = 512
    return pl.pallas_call(
        _shared_kernel,
        grid=(n_tok // tm,),
        in_specs=[pl.BlockSpec((tm, D), lambda i: (i, 0)),
                  pl.BlockSpec((D, EXPERT_DIM), lambda i: (0, 0)),
                  pl.BlockSpec((D, EXPERT_DIM), lambda i: (0, 0)),
                  pl.BlockSpec((EXPERT_DIM, D), lambda i: (0, 0))],
        out_specs=pl.BlockSpec((tm, D), lambda i: (i, 0)),
        out_shape=jax.ShapeDtypeStruct((n_tok, D), F32),
        compiler_params=_cp(("arbitrary",)),
        name="shared_expert",
    )(hx, w1, w3, w2)


COMBINE_TM = 256


def _combine_kernel(slot0_ref, slotn_ref, w_ref, sh_ref, x_ref, mod_ref, y_hbm, *rest, final, n_tiles):
    fg_ref = rest[0] if final else None
    o_ref, buf, sems = rest[-3:]
    tm = COMBINE_TM
    i = pl.program_id(0)
    half_rows = TOP_K * tm * PK_S

    def start_gathers(slot_ref, par):
        half = buf.at[par]
        sem = sems.at[par]

        def issue(tt, carry):
            t8 = pl.multiple_of(tt * ROW_UNROLL, ROW_UNROLL)
            for j in range(ROW_UNROLL):
                for k in range(TOP_K):
                    row = pl.multiple_of(slot_ref[k * tm + j + t8] * PK_S, PK_S)
                    dst = half.at[pl.ds(pl.multiple_of((k * tm + j + t8) * PK_S, PK_S), PK_S), :]
                    pltpu.make_async_copy(y_hbm.at[pl.ds(row, PK_S), :], dst, sem).start(priority=k % 2)
            return carry

        lax.fori_loop(0, tm // ROW_UNROLL, issue, 0)

    @pl.when(i == 0)
    def _():
        start_gathers(slot0_ref, 0)

    @pl.when(i + 1 < n_tiles)
    def _():
        start_gathers(slotn_ref, (i + 1) % 2)

    cur = buf.at[i % 2]
    pltpu.make_async_copy(y_hbm.at[pl.ds(0, half_rows), :], cur, sems.at[i % 2]).wait()

    w = w_ref[...]
    gate = mod_ref[0][5:6]
    acc = [None] * (2 * PK_S)
    for k in range(TOP_K):
        wk = w[:, k:k + 1]
        for s, piece in enumerate(_load_row_tiles(cur, k * tm * PK_S, tm)):
            for c, val in zip((s, PK_S + s), _unpack_bf16_pairs(piece)):
                acc[c] = wk * val if acc[c] is None else acc[c] + wk * val
    outs = []
    for c in range(2 * PK_S):
        sl = slice(c * LANE, (c + 1) * LANE)
        outs.append(x_ref[:, sl] + gate[:, sl] * (sh_ref[:, sl] + acc[c]))
    if fg_ref is not None:
        ssq = functools.reduce(lambda a, b: a + b, [jnp.sum(o * o, axis=-1, keepdims=True) for o in outs])
        inv = lax.rsqrt(ssq / D + EPS)
        outs = [o * inv * fg_ref[:, c * LANE:(c + 1) * LANE] for c, o in enumerate(outs)]
    for c, o in enumerate(outs):
        o_ref[:, c * LANE:(c + 1) * LANE] = o


def moe_combine(slots, wts_tok, shared, x, mod, y, n_tok, final_gain=None):
    tm = COMBINE_TM
    n_tiles = n_tok // tm
    final = final_gain is not None
    in_specs = [pl.BlockSpec((TOP_K * tm,), lambda i: (0,), memory_space=pltpu.SMEM),
                pl.BlockSpec((TOP_K * tm,), lambda i: (jnp.minimum(i + 1, n_tiles - 1),), memory_space=pltpu.SMEM),
                pl.BlockSpec((tm, TOP_K), lambda i: (i, 0)),
                pl.BlockSpec((tm, D), lambda i: (i, 0)),
                pl.BlockSpec((tm, D), lambda i: (i, 0)),
                pl.BlockSpec((1, 6, D), lambda i: (_mod_row(i * tm), 0, 0)),
                pl.BlockSpec(memory_space=pl.ANY)]
    args = [slots, slots, wts_tok, shared, x, mod, y]
    if final:
        in_specs.append(pl.BlockSpec((1, D), lambda i: (0, 0)))
        args.append(final_gain.reshape(1, D))
    return pl.pallas_call(
        functools.partial(_combine_kernel, final=final, n_tiles=n_tiles),
        grid=(n_tiles,),
        in_specs=in_specs,
        out_specs=pl.BlockSpec((tm, D), lambda i: (i, 0)),
        out_shape=jax.ShapeDtypeStruct((n_tok, D), F32),
        scratch_shapes=[pltpu.VMEM((2, TOP_K * tm * PK_S, LANE), U32), pltpu.SemaphoreType.DMA((2,))],
        compiler_params=_cp(("arbitrary",)),
        name="moe_combine",
    )(*args)


def _lookup(table, idx):
    e = jnp.arange(table.shape[0], dtype=I32).reshape((-1,) + (1,) * idx.ndim)
    return jnp.sum(jnp.where(idx[None] == e, table.reshape(e.shape), 0), axis=0)


def _tile_flat(slots, tm):
    k, t = slots.shape
    return slots.reshape(k, t // tm, tm).transpose(1, 0, 2).reshape(-1)


def moe_layer(x, mod, norm_gain, router_w, router_b, exp_w1, exp_w3, exp_w2, sw1, sw3, sw2, layer, n_tok,
              final_gain=None):
    bm = EXP_BM
    n_blocks = -(-n_tok * TOP_K // bm) + N_EXPERTS
    hx, hx_packed = norm_mod(x, norm_gain, mod, 3, n_tok, pack=True)
    eidx, wts, pos, counts = moe_router(hx, router_w, router_b, n_tok)
    shared = shared_expert(hx, sw1.astype(BF16), sw3.astype(BF16), sw2.astype(BF16), n_tok)
    cnt = counts.astype(I32)
    padded = (cnt + bm - 1) // bm * bm
    pad_end = jnp.cumsum(padded)
    pad_start = pad_end - padded
    slots = _lookup(pad_start, eidx) + pos
    blk_row = jnp.arange(n_blocks, dtype=I32) * bm
    block_e = jnp.minimum(jnp.sum((pad_end[:, None] <= blk_row[None, :]).astype(I32), axis=0), N_EXPERTS - 1)
    valid = jnp.clip(_lookup(cnt, block_e) - (blk_row - _lookup(pad_start, block_e)), 0, bm).astype(I32)
    n_used = (pad_end[-1:] // bm).astype(I32)
    prev_e = jnp.concatenate([jnp.full((1,), -1, I32), block_e[:-1]])
    first = ((blk_row < pad_end[-1]) & (block_e != prev_e)).astype(I32)
    stage_slot = ((jnp.cumsum(first) - 1) % 2).astype(I32)
    eids = jnp.arange(N_EXPERTS, dtype=I32)
    later = jnp.where((eids[None, :] > eids[:, None]) & (padded[None, :] > 0), eids[None, :], N_EXPERTS)
    next_nonempty = jnp.min(later, axis=1)
    next_nonempty = jnp.where(next_nonempty == N_EXPERTS, -1, next_nonempty)
    next_e = _lookup(next_nonempty, block_e).astype(I32)
    xs = moe_dispatch(_tile_flat(slots, DISPATCH_TM), hx_packed, n_tok, n_blocks * bm)
    y = moe_experts(block_e, valid, n_used, first, next_e, stage_slot, xs, exp_w1, exp_w3, exp_w2, layer, n_blocks)
    return moe_combine(_tile_flat(slots, COMBINE_TM), wts.T, shared, x, mod, y, n_tok, final_gain)


def _rope_tables():
    t = jnp.arange(S, dtype=I32)
    row = (t // GRID_W).astype(F32)
    col = (t % GRID_W).astype(F32)
    n_freq = HD // 4
    inv_freq = ROPE_THETA ** (-jnp.arange(n_freq, dtype=F32) / n_freq)
    ang = jnp.concatenate([row[:, None] * inv_freq, col[:, None] * inv_freq], axis=-1)
    cosf = jnp.repeat(jnp.cos(ang), 2, axis=-1)
    sinf = jnp.stack([-jnp.sin(ang), jnp.sin(ang)], axis=-1).reshape(S, HD)
    return cosf, sinf


def kernel(x, c, ctx, c_ctx, ada_w, ada_b, norm_mix, norm_ffn, attn_w_in, attn_w_out, attn_rpb, attn_q_gain,
           attn_k_gain, ml_w_in, ml_w_out, ml_gate_b, ml_head_gain, router_w, router_b, exp_w1, exp_w3, exp_w2,
           sh_w1, sh_w3, sh_w2, final_norm_gain):
    depth = ada_w.shape[0]
    x_lat = x.reshape(T_LAT, D)
    x_ctx = ctx.reshape(T_CTX, D)
    cvec = jnp.concatenate([c, c_ctx[None], jnp.zeros((8 - B - 1, D), F32)], axis=0)
    mod_all = ada_ln(cvec, ada_w, ada_b).reshape(depth, 8, 6, D)
    cosf, sinf = _rope_tables()

    mod = mod_all[0]
    p = norm_matmul(x_lat, norm_mix[0], mod, attn_w_in[0].astype(BF16), emit_h=False, x_ctx=x_ctx)
    o_all = neighborhood_attention(p, na_bias_table(attn_rpb[0]))
    o_all = gqa_attention(p, cosf, sinf, attn_q_gain[0], attn_k_gain[0], o_all)
    o_all = ctx_attention(p, attn_q_gain[0], attn_k_gain[0], o_all)
    xa = matmul_gated_residual(o_all, attn_w_out[0].astype(BF16), x_lat, mod, 2, x_ctx=x_ctx)
    xa = moe_layer(xa, mod, norm_ffn[0], router_w[0], router_b[0], exp_w1, exp_w3, exp_w2,
                   sh_w1[0], sh_w3[0], sh_w2[0], 0, T_ALL)

    mod = mod_all[1]
    w_in = ml_w_in[0]
    p, hx = norm_matmul(xa, norm_mix[1], mod, w_in[:, :ML_MAIN].astype(BF16), emit_h=True)
    col, row = mlstm_gates(hx, w_in[:, ML_MAIN:], ml_gate_b[0])
    hdir = mlstm_scan(p, col, row)
    a = mlstm_readout(hdir, p, ml_head_gain[0])
    xl = matmul_gated_residual(a, ml_w_out[0].astype(BF16), xa, mod, 2)
    xl = moe_layer(xl, mod, norm_ffn[1], router_w[1], router_b[1], exp_w1, exp_w3, exp_w2,
                   sh_w1[1], sh_w3[1], sh_w2[1], 1, T_LAT, final_gain=final_norm_gain)
    return xl.reshape(B, S, D)
```

```python
import functools

import jax
import jax.numpy as jnp
from jax import lax
from jax.experimental import pallas as pl
from jax.experimental.pallas import tpu as pltpu

F32 = jnp.float32
BF16 = jnp.bfloat16
I32 = jnp.int32
U32 = jnp.uint32

D = 2048
B = 4
S = 4096
L = 256
T_LAT = B * S
T_CTX = B * L
T_ALL = T_LAT + T_CTX
GRID_W = 64
ROWS = S // GRID_W
HD = 128
NA_HEADS = 8
NA_WIN_ROWS = 8
NA_WIN_COLS = 16
GQA_Q_HEADS = 8
GQA_KV_HEADS = 2
GQA_GROUP = 4
ROPE_THETA = 10000.0
ATTN_IN = 4608
ML_HEADS = 8
ML_V = 256
ML_QK = 128
ML_MAIN = 6144
N_EXPERTS = 64
TOP_K = 8
N_GROUPS = 8
TOPK_GROUPS = 4
EXPERT_DIM = 512
ROUTED_SCALE = 2.5
EPS = 1e-6
NEG_INF = -1e30
ATT_SCALE = HD ** -0.5
LOG2E = 1.4426950408889634
ML_KSCALE = ML_QK ** -0.5

LANE = 128
NA_QROWS = 4
NA_SLAB = NA_QROWS + NA_WIN_ROWS - 1
NA_QB = NA_QROWS * GRID_W
NA_KB = NA_SLAB * GRID_W
ML_CH = 256
EXP_BM = 512
PK_W = D // 2
PK_S = PK_W // LANE
VMEM_LIMIT = 48 * 1024 * 1024


def _cp(sem, vmem=VMEM_LIMIT):
    return pltpu.CompilerParams(dimension_semantics=sem, vmem_limit_bytes=vmem)


def _pack_bf16_pairs(xb):
    u = pltpu.bitcast(xb.astype(F32), U32)
    return (u[:, PK_W:] & jnp.uint32(0xFFFF0000)) | (u[:, :PK_W] >> 16)


def _unpack_bf16_pairs(u):
    return pltpu.bitcast(u << 16, F32), pltpu.bitcast(u & jnp.uint32(0xFFFF0000), F32)


def _store_row_tiles(ref, words):
    rows = words.shape[0]
    for s in range(PK_S):
        ref[pl.ds(s, rows, stride=PK_S), :] = words[:, s * LANE:(s + 1) * LANE]


def _load_row_tiles(ref, start, rows):
    return [ref[pl.ds(start + s, rows, stride=PK_S), :] for s in range(PK_S)]


def _mod_row(start_row):
    return jnp.where(start_row < T_LAT, start_row // S, B)


def _ada_kernel(c_ref, w_ref, b_ref, o_ref):
    c = c_ref[...]
    a = (c * jax.nn.sigmoid(c)).astype(BF16)
    w = w_ref[0].astype(BF16)
    o_ref[0] = jnp.dot(a, w, preferred_element_type=F32) + b_ref[0]


def ada_ln(cvec, ada_w, ada_b):
    depth = ada_w.shape[0]
    n = ada_w.shape[2]
    tn = 1024
    return pl.pallas_call(
        _ada_kernel,
        grid=(depth, n // tn),
        in_specs=[pl.BlockSpec((8, D), lambda l, j: (0, 0)),
                  pl.BlockSpec((1, D, tn), lambda l, j: (l, 0, j)),
                  pl.BlockSpec((1, 1, tn), lambda l, j: (l, 0, j))],
        out_specs=pl.BlockSpec((1, 8, tn), lambda l, j: (l, 0, j)),
        out_shape=jax.ShapeDtypeStruct((depth, 8, n), F32),
        compiler_params=_cp(("arbitrary", "arbitrary")),
        name="ada_ln",
    )(cvec, ada_w, ada_b.reshape(depth, 1, n))


def _norm_mod_kernel(x_ref, g_ref, mod_ref, *out_refs, base, pack):
    x = x_ref[...]
    y = x * lax.rsqrt(jnp.mean(x * x, axis=-1, keepdims=True) + EPS) * g_ref[...]
    m = mod_ref[0]
    h = y * (1.0 + m[base + 1:base + 2]) + m[base:base + 1]
    hb = h.astype(BF16)
    out_refs[0][...] = hb
    if pack:
        _store_row_tiles(out_refs[1], _pack_bf16_pairs(hb))


def norm_mod(x, gain, mod, base, n_rows, pack):
    tm = 256
    out_shape = [jax.ShapeDtypeStruct((n_rows, D), BF16)]
    out_specs = [pl.BlockSpec((tm, D), lambda i: (i, 0))]
    if pack:
        out_shape.append(jax.ShapeDtypeStruct((n_rows * PK_S, LANE), U32))
        out_specs.append(pl.BlockSpec((tm * PK_S, LANE), lambda i: (i, 0)))
    res = pl.pallas_call(
        functools.partial(_norm_mod_kernel, base=base, pack=pack),
        grid=(n_rows // tm,),
        in_specs=[pl.BlockSpec((tm, D), lambda i: (i, 0)),
                  pl.BlockSpec((1, D), lambda i: (0, 0)),
                  pl.BlockSpec((1, 6, D), lambda i: (_mod_row(i * tm), 0, 0))],
        out_specs=out_specs,
        out_shape=out_shape,
        compiler_params=_cp(("arbitrary",)),
        name="norm_mod",
    )(x, gain.reshape(1, D), mod)
    return res if pack else res[0]


def _norm_mm_kernel(*refs, emit_h, n_lat):
    if n_lat is None:
        x_ref, g_ref, mod_ref, w_ref, o_ref = refs[:5]
        c_ref = None
    else:
        x_ref, c_ref, g_ref, mod_ref, w_ref, o_ref = refs[:6]
    hb_ref = refs[-1]

    def prologue(src_ref):
        x = src_ref[...]
        y = x * lax.rsqrt(jnp.mean(x * x, axis=-1, keepdims=True) + EPS) * g_ref[...]
        m = mod_ref[0]
        hb_ref[...] = (y * (1.0 + m[1:2]) + m[0:1]).astype(BF16)
        if emit_h:
            refs[-2][...] = hb_ref[...]

    first_col = pl.program_id(1) == 0
    if c_ref is None:
        pl.when(first_col)(lambda: prologue(x_ref))
    else:
        is_lat = pl.program_id(0) < n_lat
        pl.when(first_col & is_lat)(lambda: prologue(x_ref))
        pl.when(first_col & jnp.logical_not(is_lat))(lambda: prologue(c_ref))

    o_ref[...] = jnp.dot(hb_ref[...], w_ref[...], preferred_element_type=F32).astype(o_ref.dtype)


def norm_matmul(x, gain, mod, w, emit_h, x_ctx=None, tm=1024, tn=512):
    split = x_ctx is not None
    m = x.shape[0] + (x_ctx.shape[0] if split else 0)
    n = w.shape[1]
    n_lat = x.shape[0] // tm if split else None
    out_shape = [jax.ShapeDtypeStruct((m, n), BF16)]
    out_specs = [pl.BlockSpec((tm, tn), lambda i, j: (i, j))]
    if emit_h:
        out_shape.append(jax.ShapeDtypeStruct((m, D), BF16))
        out_specs.append(pl.BlockSpec((tm, D), lambda i, j: (i, 0)))
    if split:
        x_specs = [pl.BlockSpec((tm, D), lambda i, j: (jnp.minimum(i, n_lat - 1), 0)),
                   pl.BlockSpec((tm, D), lambda i, j: (0, 0))]
        x_args = [x, x_ctx]
    else:
        x_specs = [pl.BlockSpec((tm, D), lambda i, j: (i, 0))]
        x_args = [x]
    res = pl.pallas_call(
        functools.partial(_norm_mm_kernel, emit_h=emit_h, n_lat=n_lat),
        grid=(m // tm, n // tn),
        in_specs=x_specs + [pl.BlockSpec((1, D), lambda i, j: (0, 0)),
                            pl.BlockSpec((1, 6, D), lambda i, j: (_mod_row(i * tm), 0, 0)),
                            pl.BlockSpec((D, tn), lambda i, j: (0, j))],
        out_specs=out_specs,
        out_shape=out_shape,
        scratch_shapes=[pltpu.VMEM((tm, D), BF16)],
        compiler_params=_cp(("arbitrary", "arbitrary"), vmem=56 * 1024 * 1024),
        name="norm_matmul",
    )(*x_args, gain.reshape(1, D), mod, w)
    return res if emit_h else res[0]


def _mm_res_kernel(*refs, slot, n_lat):
    if n_lat is None:
        a_ref, w_ref, x_ref, mod_ref, o_ref = refs
        res = x_ref[...]
    else:
        a_ref, w_ref, x_ref, c_ref, mod_ref, o_ref = refs
        res = jnp.where(pl.program_id(0) < n_lat, x_ref[...], c_ref[...])
    acc = jnp.dot(a_ref[...], w_ref[...], preferred_element_type=F32)
    o_ref[...] = res + mod_ref[0][slot:slot + 1] * acc


def matmul_gated_residual(a, w, x, mod, slot, x_ctx=None, tm=1024, tn=512):
    m, k = a.shape
    n = w.shape[1]
    split = x_ctx is not None
    n_lat = x.shape[0] // tm if split else None
    if split:
        x_specs = [pl.BlockSpec((tm, tn), lambda i, j: (jnp.minimum(i, n_lat - 1), j)),
                   pl.BlockSpec((tm, tn), lambda i, j: (0, j))]
        x_args = [x, x_ctx]
    else:
        x_specs = [pl.BlockSpec((tm, tn), lambda i, j: (i, j))]
        x_args = [x]
    return pl.pallas_call(
        functools.partial(_mm_res_kernel, slot=slot, n_lat=n_lat),
        grid=(m // tm, n // tn),
        in_specs=[pl.BlockSpec((tm, k), lambda i, j: (i, 0)),
                  pl.BlockSpec((k, tn), lambda i, j: (0, j))] + x_specs
                 + [pl.BlockSpec((1, 6, tn), lambda i, j: (_mod_row(i * tm), 0, j))],
        out_specs=pl.BlockSpec((tm, tn), lambda i, j: (i, j)),
        out_shape=jax.ShapeDtypeStruct((m, n), F32),
        compiler_params=_cp(("arbitrary", "arbitrary")),
        name="matmul_gated_residual",
    )(a, w, *x_args, mod)


def _dot_nt(a, b):
    return lax.dot_general(a, b, (((1,), (1,)), ((), ())), preferred_element_type=F32)


def _rms_head(x, gain):
    return x * lax.rsqrt(jnp.mean(x * x, axis=-1, keepdims=True) + EPS) * gain


def _rope(x, cosf, sinf):
    lane = lax.broadcasted_iota(I32, x.shape, 1)
    nxt = pltpu.roll(x, LANE - 1, 1)
    prv = pltpu.roll(x, 1, 1)
    return x * cosf + jnp.where((lane & 1) == 0, nxt, prv) * sinf


def _softmax_av(parts):
    m = functools.reduce(jnp.maximum, [jnp.max(s, axis=-1, keepdims=True) for s, _ in parts])
    l = None
    o = None
    for s, v in parts:
        p = jnp.exp(s - m)
        li = jnp.sum(p, axis=-1, keepdims=True)
        oi = jnp.dot(p.astype(BF16), v, preferred_element_type=F32)
        l = li if l is None else l + li
        o = oi if o is None else o + oi
    return o / l


def _na_kernel(q_ref, k_ref, v_ref, kc_ref, vc_ref, tab_ref, o_ref):
    kc = kc_ref[...]
    vc = vc_ref[...]
    n_blocks = ROWS // NA_QROWS

    def body(j, carry):
        ks = jnp.clip(j * NA_QROWS - NA_WIN_ROWS // 2, 0, ROWS - NA_SLAB)
        typ = jnp.where(j == 0, 0, jnp.where(j == n_blocks - 1, 2, 1))
        qs = pl.multiple_of(j * NA_QB, NA_QB)
        kst = pl.multiple_of(ks * GRID_W, GRID_W)
        q = q_ref[pl.ds(qs, NA_QB), :]
        k = k_ref[pl.ds(kst, NA_KB), :]
        v = v_ref[pl.ds(kst, NA_KB), :]
        s_win = _dot_nt(q, k) * ATT_SCALE + tab_ref[typ, 0]
        s_ctx = _dot_nt(q, kc) * ATT_SCALE
        o_ref[pl.ds(qs, NA_QB), :] = _softmax_av([(s_win, v), (s_ctx, vc)]).astype(BF16)
        return carry

    lax.fori_loop(0, n_blocks, body, 0)


def na_bias_table(rpb):
    def one(r0, ks):
        r = r0 + jnp.arange(NA_QROWS)
        kr = ks + jnp.arange(NA_SLAB)
        start = jnp.clip(r - NA_WIN_ROWS // 2, 0, ROWS - NA_WIN_ROWS)
        row_ok = (kr[None, :] >= start[:, None]) & (kr[None, :] < start[:, None] + NA_WIN_ROWS)
        row_idx = jnp.clip(kr[None, :] - r[:, None] + NA_WIN_ROWS - 1, 0, 2 * NA_WIN_ROWS - 2)
        cq = jnp.arange(GRID_W)
        col_start = jnp.clip(cq - NA_WIN_COLS // 2, 0, GRID_W - NA_WIN_COLS)
        col_ok = (cq[None, :] >= col_start[:, None]) & (cq[None, :] < col_start[:, None] + NA_WIN_COLS)
        col_idx = jnp.clip(cq[None, :] - cq[:, None] + NA_WIN_COLS - 1, 0, 2 * NA_WIN_COLS - 2)
        r_hot = jax.nn.one_hot(row_idx, 2 * NA_WIN_ROWS - 1, dtype=F32)
        c_hot = jax.nn.one_hot(col_idx, 2 * NA_WIN_COLS - 1, dtype=F32)
        bias = jnp.einsum('qka,hab,xyb->hqxky', r_hot, rpb.astype(F32), c_hot, precision=lax.Precision.HIGHEST)
        ok = row_ok[:, None, :, None] & col_ok[None, :, None, :]
        return jnp.where(ok[None], bias, NEG_INF).reshape(NA_HEADS, NA_QB, NA_KB)

    mid = 2 * NA_QROWS
    last = ROWS - NA_QROWS
    return jnp.stack([one(0, 0), one(mid, mid - NA_WIN_ROWS // 2), one(last, ROWS - NA_SLAB)])


def neighborhood_attention(p, table):
    cb = S // L
    return pl.pallas_call(
        _na_kernel,
        grid=(NA_HEADS, B),
        in_specs=[pl.BlockSpec((S, HD), lambda h, b: (b, h)),
                  pl.BlockSpec((S, HD), lambda h, b: (b, NA_HEADS + h)),
                  pl.BlockSpec((S, HD), lambda h, b: (b, 2 * NA_HEADS + h)),
                  pl.BlockSpec((L, HD), lambda h, b: (B * cb + b, NA_HEADS + h)),
                  pl.BlockSpec((L, HD), lambda h, b: (B * cb + b, 2 * NA_HEADS + h)),
                  pl.BlockSpec((3, 1, NA_QB, NA_KB), lambda h, b: (0, h, 0, 0))],
        out_specs=pl.BlockSpec((S, HD), lambda h, b: (b, h)),
        out_shape=jax.ShapeDtypeStruct((T_ALL, D), BF16),
        compiler_params=_cp(("arbitrary", "arbitrary")),
        name="neighborhood_attention",
    )(p, p, p, p, p, table)


GQA_TQ = 512
GQA_CK = 1024
GQA_QCOL = 3 * NA_HEADS
GQA_KCOL = GQA_QCOL + GQA_Q_HEADS
GQA_VCOL = GQA_KCOL + GQA_KV_HEADS


def _gqa_kernel(q_ref, k_ref, v_ref, kc_ref, vc_ref, cq_ref, sq_ref, ck_ref, sk_ref, qg_ref, kg_ref, o_prev,
                o_ref, kn_ref, kcn_ref):
    del o_prev
    @pl.when(pl.program_id(2) == 0)
    def _():
        kn = _rope(_rms_head(k_ref[...].astype(F32), kg_ref[...]), ck_ref[...], sk_ref[...])
        kn_ref[...] = kn.astype(BF16)
        kcn_ref[...] = _rms_head(kc_ref[...].astype(F32), kg_ref[...]).astype(BF16)

    cos = cq_ref[...]
    sin = sq_ref[...]
    heads = []
    for g in range(GQA_GROUP):
        qh = _rope(_rms_head(q_ref[:, g * HD:(g + 1) * HD].astype(F32), qg_ref[...]), cos, sin)
        heads.append((qh * (ATT_SCALE * LOG2E)).astype(BF16))
    q = jnp.concatenate(heads, axis=0)
    chunks = [(kn_ref[c * GQA_CK:(c + 1) * GQA_CK, :], v_ref[c * GQA_CK:(c + 1) * GQA_CK, :])
              for c in range(S // GQA_CK)]
    chunks.append((kcn_ref[...], vc_ref[...]))
    m = l = acc = None
    for kk, vv in chunks:
        s = _dot_nt(q, kk)
        mc = jnp.max(s, axis=-1, keepdims=True)
        if m is None:
            m_new = mc
            p = jnp.exp2(s - m_new)
            l = jnp.sum(p, axis=-1, keepdims=True)
            acc = jnp.dot(p.astype(BF16), vv, preferred_element_type=F32)
        else:
            m_new = jnp.maximum(m, mc)
            alpha = jnp.exp2(m - m_new)
            p = jnp.exp2(s - m_new)
            l = alpha * l + jnp.sum(p, axis=-1, keepdims=True)
            acc = alpha * acc + jnp.dot(p.astype(BF16), vv, preferred_element_type=F32)
        m = m_new
    o = acc / l
    for g in range(GQA_GROUP):
        o_ref[:, g * HD:(g + 1) * HD] = o[g * GQA_TQ:(g + 1) * GQA_TQ].astype(BF16)


def gqa_attention(p, cosf, sinf, q_gain, k_gain, o_buf):
    nq = S // GQA_TQ
    cb = S // L
    gw = GQA_GROUP * HD
    return pl.pallas_call(
        _gqa_kernel,
        grid=(B, GQA_KV_HEADS, nq),
        in_specs=[pl.BlockSpec((GQA_TQ, gw), lambda b, n, i: (b * nq + i, GQA_QCOL // GQA_GROUP + n)),
                  pl.BlockSpec((S, HD), lambda b, n, i: (b, GQA_KCOL + n)),
                  pl.BlockSpec((S, HD), lambda b, n, i: (b, GQA_VCOL + n)),
                  pl.BlockSpec((L, HD), lambda b, n, i: (B * cb + b, GQA_KCOL + n)),
                  pl.BlockSpec((L, HD), lambda b, n, i: (B * cb + b, GQA_VCOL + n)),
                  pl.BlockSpec((GQA_TQ, HD), lambda b, n, i: (i, 0)),
                  pl.BlockSpec((GQA_TQ, HD), lambda b, n, i: (i, 0)),
                  pl.BlockSpec((S, HD), lambda b, n, i: (0, 0)),
                  pl.BlockSpec((S, HD), lambda b, n, i: (0, 0)),
                  pl.BlockSpec((1, HD), lambda b, n, i: (0, 0)),
                  pl.BlockSpec((1, HD), lambda b, n, i: (0, 0)),
                  pl.BlockSpec(memory_space=pl.ANY)],
        out_specs=pl.BlockSpec((GQA_TQ, gw), lambda b, n, i: (b * nq + i, (NA_HEADS * HD) // gw + n)),
        out_shape=jax.ShapeDtypeStruct((T_ALL, D), BF16),
        input_output_aliases={11: 0},
        scratch_shapes=[pltpu.VMEM((S, HD), BF16), pltpu.VMEM((L, HD), BF16)],
        compiler_params=_cp(("arbitrary",) * 3),
        name="gqa_attention",
    )(p, p, p, p, p, cosf, sinf, cosf, sinf, q_gain.reshape(1, HD), k_gain.reshape(1, HD), o_buf)


def _ctx_attn_kernel(p_ref, qg_ref, kg_ref, o_prev, o_ref):
    del o_prev

    def col(c):
        return p_ref[:, c * HD:(c + 1) * HD]

    for h in range(NA_HEADS):
        s = _dot_nt(col(h), col(NA_HEADS + h)) * ATT_SCALE
        o_ref[:, h * HD:(h + 1) * HD] = _softmax_av([(s, col(2 * NA_HEADS + h))]).astype(BF16)
    for n in range(GQA_KV_HEADS):
        kn = _rms_head(col(GQA_KCOL + n).astype(F32), kg_ref[...]).astype(BF16)
        v = col(GQA_VCOL + n)
        for g in range(GQA_GROUP):
            h = n * GQA_GROUP + g
            qn = _rms_head(col(GQA_QCOL + h).astype(F32), qg_ref[...]).astype(BF16)
            s = _dot_nt(qn, kn) * ATT_SCALE
            o_ref[:, (NA_HEADS + h) * HD:(NA_HEADS + h + 1) * HD] = _softmax_av([(s, v)]).astype(BF16)


def ctx_attention(p, q_gain, k_gain, o_buf):
    cb = S // L
    return pl.pallas_call(
        _ctx_attn_kernel,
        grid=(B,),
        in_specs=[pl.BlockSpec((L, ATTN_IN), lambda b: (B * cb + b, 0)),
                  pl.BlockSpec((1, HD), lambda b: (0, 0)),
                  pl.BlockSpec((1, HD), lambda b: (0, 0)),
                  pl.BlockSpec(memory_space=pl.ANY)],
        out_specs=pl.BlockSpec((L, D), lambda b: (B * cb + b, 0)),
        out_shape=jax.ShapeDtypeStruct((T_ALL, D), BF16),
        input_output_aliases={3: 0},
        compiler_params=_cp(("arbitrary",)),
        name="ctx_attention",
    )(p, q_gain.reshape(1, HD), k_gain.reshape(1, HD), o_buf)


def _log_sigmoid(x):
    return -(jnp.maximum(-x, 0.0) + jnp.log1p(jnp.exp(-jnp.abs(x))))


def _dot_hi(a, b):
    return jnp.dot(a, b, precision=lax.Precision.HIGHEST, preferred_element_type=F32)


def _gate_kernel(h_ref, wg_ref, wgt_ref, b_ref, bt_ref, lt_ref, ut_ref, col_ref, row_ref):
    nh = ML_HEADS
    hx = h_ref[...]
    g = jnp.dot(hx, wg_ref[...], preferred_element_type=F32) + b_ref[...]
    gt = _dot_nt(wgt_ref[...], hx) + bt_ref[...]
    li = g[:, 0:2 * nh]
    lf = _log_sigmoid(g[:, 2 * nh:4 * nh])
    lit = gt[0:2 * nh]
    lft = _log_sigmoid(gt[2 * nh:4 * nh])
    lt = lt_ref[...]
    ut = ut_ref[...]
    lane = lax.broadcasted_iota(I32, lf.shape, 1)
    bc = jnp.where(lane < nh, _dot_hi(lt, lf), _dot_hi(ut, lf))
    tot = jnp.sum(lf, axis=0, keepdims=True)
    aend = tot - bc + li
    col_ref[...] = jnp.concatenate([bc, aend, jnp.zeros((ML_CH, LANE - 4 * nh), F32)], axis=1)
    sub = lax.broadcasted_iota(I32, lft.shape, 0)
    bct = jnp.where(sub < nh, _dot_hi(lft, ut), _dot_hi(lft, lt))
    tott = jnp.sum(lft, axis=1, keepdims=True)
    gtr = lit - bct
    row_ref[0] = jnp.concatenate([bct, gtr, tott + gtr, jnp.broadcast_to(tott, bct.shape)], axis=0)


def mlstm_gates(hx, wg, gate_b):
    nh = ML_HEADS
    n_ch = T_ALL // ML_CH
    wg_pad = jnp.zeros((D, LANE), BF16).at[:, :4 * nh].set(wg.astype(BF16))
    b_pad = jnp.zeros((1, LANE), F32).at[0, :4 * nh].set(gate_b.reshape(-1))
    wgt = wg.astype(BF16).T
    bt = gate_b.reshape(4 * nh, 1).astype(F32)
    lt = jnp.tril(jnp.ones((ML_CH, ML_CH), F32))
    ut = jnp.triu(jnp.ones((ML_CH, ML_CH), F32))
    col, row = pl.pallas_call(
        _gate_kernel,
        grid=(n_ch,),
        in_specs=[pl.BlockSpec((ML_CH, D), lambda i: (i, 0)),
                  pl.BlockSpec((D, LANE), lambda i: (0, 0)),
                  pl.BlockSpec((4 * nh, D), lambda i: (0, 0)),
                  pl.BlockSpec((1, LANE), lambda i: (0, 0)),
                  pl.BlockSpec((4 * nh, 1), lambda i: (0, 0)),
                  pl.BlockSpec((ML_CH, ML_CH), lambda i: (0, 0)),
                  pl.BlockSpec((ML_CH, ML_CH), lambda i: (0, 0))],
        out_specs=[pl.BlockSpec((ML_CH, LANE), lambda i: (i, 0)),
                   pl.BlockSpec((1, 8 * nh, ML_CH), lambda i: (i, 0, 0))],
        out_shape=[jax.ShapeDtypeStruct((T_ALL, LANE), F32),
                   jax.ShapeDtypeStruct((n_ch, 8 * nh, ML_CH), F32)],
        compiler_params=_cp(("arbitrary",)),
        name="mlstm_gates",
    )(hx, wg_pad, wgt, b_pad, bt, lt, ut)
    return col, row


def _mlstm_step(d, hh, q_ref, k_ref, v_ref, col_ref, row_ref, o_ref, c_ref, n_ref, m_ref):
    nd = 2 * ML_HEADS
    sl = d * ML_HEADS + hh
    q = q_ref[:, hh * ML_QK:(hh + 1) * ML_QK]
    kf = k_ref[:, hh * ML_QK:(hh + 1) * ML_QK].astype(F32) * ML_KSCALE
    kb = kf.astype(BF16)
    v = v_ref[:, hh * ML_V:(hh + 1) * ML_V]
    bc_col = col_ref[:, sl:sl + 1]
    aend_col = col_ref[:, nd + sl:nd + sl + 1]
    g_row = row_ref[0, nd + sl:nd + sl + 1, :]
    aend_row = row_ref[0, 2 * nd + sl:2 * nd + sl + 1, :]
    btot = row_ref[0, 3 * nd + sl:3 * nd + sl + 1, 0:1]
    m_st = m_ref[sl]
    c_st = c_ref[sl]
    n_st = n_ref[sl]
    m_new = jnp.maximum(btot + m_st, jnp.max(aend_row, axis=1, keepdims=True))

    r = lax.broadcasted_iota(I32, (ML_CH, ML_CH), 0)
    c = lax.broadcasted_iota(I32, (ML_CH, ML_CH), 1)
    causal = (r >= c) if d == 0 else (r <= c)
    d_mat = jnp.where(causal, bc_col + g_row, -jnp.inf)
    m_row = jnp.maximum(bc_col + m_st, jnp.max(d_mat, axis=1, keepdims=True))
    w_inter = jnp.exp(bc_col + m_st - m_row)
    s_mat = _dot_nt(q, kb) * jnp.exp(d_mat - m_row)
    num = (w_inter * jnp.dot(q, c_st.astype(BF16), preferred_element_type=F32)
           + jnp.dot(s_mat.astype(BF16), v, preferred_element_type=F32))
    den = (w_inter * jnp.sum(q.astype(F32) * n_st, axis=1, keepdims=True)
           + jnp.sum(s_mat, axis=1, keepdims=True))
    h_out = num / jnp.maximum(jnp.abs(den), jnp.exp(-m_row))
    o_ref[:, hh * ML_V:(hh + 1) * ML_V] = h_out.astype(o_ref.dtype)

    w_end_col = jnp.exp(aend_col - m_new)
    w_end_row = jnp.exp(aend_row - m_new)
    decay = jnp.exp(btot + m_st - m_new)
    kw = (kf * w_end_col).astype(BF16)
    c_ref[sl] = decay * c_st + lax.dot_general(kw, v, (((0,), (0,)), ((), ())), preferred_element_type=F32)
    w8 = jnp.broadcast_to(w_end_row, (8, ML_CH)).astype(BF16)
    n_ref[sl] = decay * n_st + jnp.dot(w8, kb, preferred_element_type=F32)[0:1]
    m_ref[sl] = m_new


def _mlstm_kernel(qf, kf, vf, colf, rowf, qb, kb, vb, colb, rowb, of, ob, c_ref, n_ref, m_ref):
    @pl.when(pl.program_id(1) == 0)
    def _():
        c_ref[...] = jnp.zeros_like(c_ref)
        n_ref[...] = jnp.zeros_like(n_ref)
        m_ref[...] = jnp.zeros_like(m_ref)

    for hh in range(ML_HEADS):
        _mlstm_step(0, hh, qf, kf, vf, colf, rowf, of, c_ref, n_ref, m_ref)
        _mlstm_step(1, hh, qb, kb, vb, colb, rowb, ob, c_ref, n_ref, m_ref)


def mlstm_scan(p, col, row):
    n_lat = S // ML_CH
    steps = n_lat + 1
    lat_blocks = T_LAT // ML_CH
    qk_w = ML_HEADS * ML_QK
    v_w = ML_HEADS * ML_V
    n_chains = 2 * ML_HEADS

    def chunk(b, d, st):
        c = (st - 1) if d == 0 else (n_lat - st)
        return jnp.where(st == 0, lat_blocks + b, b * n_lat + c)

    def out_chunk(b, d, st):
        s1 = jnp.maximum(st, 1)
        return b * n_lat + ((s1 - 1) if d == 0 else (n_lat - s1))

    def dir_specs(d):
        return [pl.BlockSpec((ML_CH, qk_w), lambda b, s: (chunk(b, d, s), 0)),
                pl.BlockSpec((ML_CH, qk_w), lambda b, s: (chunk(b, d, s), 1)),
                pl.BlockSpec((ML_CH, v_w), lambda b, s: (chunk(b, d, s), (2 * qk_w) // v_w)),
                pl.BlockSpec((ML_CH, LANE), lambda b, s: (chunk(b, d, s), 0)),
                pl.BlockSpec((1, 4 * n_chains, ML_CH), lambda b, s: (chunk(b, d, s), 0, 0))]

    return pl.pallas_call(
        _mlstm_kernel,
        grid=(B, steps),
        in_specs=dir_specs(0) + dir_specs(1),
        out_specs=[pl.BlockSpec((ML_CH, v_w), lambda b, s: (out_chunk(b, 0, s), 0)),
                   pl.BlockSpec((ML_CH, v_w), lambda b, s: (out_chunk(b, 1, s), 0))],
        out_shape=[jax.ShapeDtypeStruct((T_LAT, v_w), BF16), jax.ShapeDtypeStruct((T_LAT, v_w), BF16)],
        scratch_shapes=[pltpu.VMEM((n_chains, ML_QK, ML_V), F32), pltpu.VMEM((n_chains, 1, ML_QK), F32),
                        pltpu.VMEM((n_chains, 1, 1), F32)],
        compiler_params=_cp(("arbitrary",) * 2),
        name="mlstm_scan",
    )(p, p, p, col, row, p, p, p, col, row)


def _readout_kernel(hf_ref, hb_ref, o_ref, g_ref, a_ref):
    hs = hf_ref[...].astype(F32) + hb_ref[...].astype(F32)
    for h in range(ML_HEADS):
        sl = slice(h * ML_V, (h + 1) * ML_V)
        x = hs[:, sl]
        hn = x * lax.rsqrt(jnp.mean(x * x, axis=-1, keepdims=True) + EPS) * g_ref[:, sl]
        a_ref[:, sl] = (hn * jax.nn.sigmoid(o_ref[:, sl].astype(F32))).astype(BF16)


def mlstm_readout(hdir, p, head_gain):
    tm = 256
    ocol = (2 * ML_HEADS * ML_QK + ML_HEADS * ML_V) // D
    return pl.pallas_call(
        _readout_kernel,
        grid=(T_LAT // tm,),
        in_specs=[pl.BlockSpec((tm, D), lambda i: (i, 0)),
                  pl.BlockSpec((tm, D), lambda i: (i, 0)),
                  pl.BlockSpec((tm, D), lambda i: (i, ocol)),
                  pl.BlockSpec((1, D), lambda i: (0, 0))],
        out_specs=pl.BlockSpec((tm, D), lambda i: (i, 0)),
        out_shape=jax.ShapeDtypeStruct((T_LAT, D), BF16),
        compiler_params=_cp(("arbitrary",)),
        name="mlstm_readout",
    )(hdir[0], hdir[1], p, head_gain.reshape(1, D))


ROUTER_TM = 512


def _router_kernel(h_ref, w_ref, rb_ref, erow_ref, tri_ref, eidx_ref, wts_ref, pos_ref, cnt_ref, carry_ref):
    ng = N_GROUPS
    epg = N_EXPERTS // N_GROUPS
    tm = ROUTER_TM
    ninf = -jnp.inf

    @pl.when(pl.program_id(0) == 0)
    def _():
        carry_ref[...] = jnp.zeros_like(carry_ref)

    s = jax.nn.sigmoid(_dot_nt(w_ref[...], h_ref[...]))
    ssel = s + rb_ref[...]
    sraw = [s[ng * j:ng * (j + 1)] for j in range(epg)]
    slab = [ssel[ng * j:ng * (j + 1)] for j in range(epg)]
    m1 = functools.reduce(jnp.maximum, slab)
    jfirst = functools.reduce(jnp.minimum, [jnp.where(slab[j] == m1, j, epg) for j in range(epg)])
    m2 = functools.reduce(jnp.maximum, [jnp.where(jfirst == j, ninf, slab[j]) for j in range(epg)])
    gs = m1 + m2
    giota = lax.broadcasted_iota(I32, (ng, tm), 0)
    gsel = jnp.zeros((ng, tm), F32)
    for _ in range(TOPK_GROUPS):
        mx = jnp.max(gs, axis=0, keepdims=True)
        gi = jnp.min(jnp.where(gs == mx, giota, ng), axis=0, keepdims=True)
        hit = giota == gi
        gsel = jnp.where(hit, 1.0, gsel)
        gs = jnp.where(hit, ninf, gs)
    msl = [jnp.where(gsel > 0.0, slab[j], ninf) for j in range(epg)]
    eid = [giota * epg + j for j in range(epg)]
    selm = [jnp.zeros((ng, tm), F32) for _ in range(epg)]
    e_list, w_list = [], []
    for _ in range(TOP_K):
        mx = jnp.max(functools.reduce(jnp.maximum, msl), axis=0, keepdims=True)
        cand = functools.reduce(jnp.minimum, [jnp.where(msl[j] == mx, eid[j], N_EXPERTS) for j in range(epg)])
        esel = jnp.min(cand, axis=0, keepdims=True)
        hits = [eid[j] == esel for j in range(epg)]
        wk = functools.reduce(lambda a, b: a + b, [jnp.where(hits[j], sraw[j], 0.0) for j in range(epg)])
        w_list.append(jnp.sum(wk, axis=0, keepdims=True))
        e_list.append(esel)
        msl = [jnp.where(hits[j], ninf, msl[j]) for j in range(epg)]
        selm = [jnp.where(hits[j], 1.0, selm[j]) for j in range(epg)]
    wsum = functools.reduce(lambda a, b: a + b, w_list)
    wts_ref[...] = jnp.concatenate([w / wsum * ROUTED_SCALE for w in w_list], axis=0)
    eidx_ref[...] = jnp.concatenate(e_list, axis=0)
    sel = jnp.concatenate(selm, axis=0)
    carry = carry_ref[...]
    posfull = jnp.dot(sel.astype(BF16), tri_ref[...], preferred_element_type=F32) + carry
    erow = erow_ref[...]
    pos = [jnp.sum(jnp.where(erow == e, posfull, 0.0), axis=0, keepdims=True) for e in e_list]
    pos_ref[...] = jnp.concatenate(pos, axis=0).astype(I32)
    carry = carry + jnp.sum(sel, axis=1, keepdims=True)
    carry_ref[...] = carry
    cnt_ref[...] = carry


def moe_router(hx, router_w, router_b, n_tok):
    tm = ROUTER_TM
    epg = N_EXPERTS // N_GROUPS
    perm = (jnp.arange(N_EXPERTS) % N_GROUPS) * epg + jnp.arange(N_EXPERTS) // N_GROUPS
    w_t = router_w.astype(BF16).T[perm]
    rb = router_b.astype(F32)[perm].reshape(N_EXPERTS, 1)
    erow = perm.astype(I32).reshape(N_EXPERTS, 1)
    tri = jnp.triu(jnp.ones((tm, tm), BF16), 1)
    eidx, wts, pos, counts = pl.pallas_call(
        _router_kernel,
        grid=(n_tok // tm,),
        in_specs=[pl.BlockSpec((tm, D), lambda i: (i, 0)),
                  pl.BlockSpec((N_EXPERTS, D), lambda i: (0, 0)),
                  pl.BlockSpec((N_EXPERTS, 1), lambda i: (0, 0)),
                  pl.BlockSpec((N_EXPERTS, 1), lambda i: (0, 0)),
                  pl.BlockSpec((tm, tm), lambda i: (0, 0))],
        out_specs=[pl.BlockSpec((TOP_K, tm), lambda i: (0, i)),
                   pl.BlockSpec((TOP_K, tm), lambda i: (0, i)),
                   pl.BlockSpec((TOP_K, tm), lambda i: (0, i)),
                   pl.BlockSpec((N_EXPERTS, 1), lambda i: (0, 0))],
        out_shape=[jax.ShapeDtypeStruct((TOP_K, n_tok), I32),
                   jax.ShapeDtypeStruct((TOP_K, n_tok), F32),
                   jax.ShapeDtypeStruct((TOP_K, n_tok), I32),
                   jax.ShapeDtypeStruct((N_EXPERTS, 1), F32)],
        scratch_shapes=[pltpu.VMEM((N_EXPERTS, 1), F32)],
        compiler_params=_cp(("arbitrary",)),
        name="moe_router",
    )(hx, w_t, rb, erow, tri)
    return eidx, wts, pos, counts.reshape(N_EXPERTS)[perm]


DISPATCH_TM = 512


ROW_UNROLL = 8


def _dispatch_kernel(slot_ref, hx_ref, xs_hbm, sem):
    def issue(tt, carry):
        t8 = pl.multiple_of(tt * ROW_UNROLL, ROW_UNROLL)
        for j in range(ROW_UNROLL):
            src = hx_ref.at[pl.ds(pl.multiple_of((t8 + j) * PK_S, PK_S), PK_S), :]
            for k in range(TOP_K):
                row = pl.multiple_of(slot_ref[k * DISPATCH_TM + j + t8] * PK_S, PK_S)
                pltpu.make_async_copy(src, xs_hbm.at[pl.ds(row, PK_S), :], sem).start(priority=k % 2)
        return carry

    lax.fori_loop(0, DISPATCH_TM // ROW_UNROLL, issue, 0)
    for _ in range(TOP_K):
        pltpu.make_async_copy(hx_ref, xs_hbm.at[pl.ds(0, DISPATCH_TM * PK_S), :], sem).wait()


def moe_dispatch(slots, hx_packed, n_tok, n_rows):
    tm = DISPATCH_TM
    return pl.pallas_call(
        _dispatch_kernel,
        grid=(n_tok // tm,),
        in_specs=[pl.BlockSpec((TOP_K * tm,), lambda i: (i,), memory_space=pltpu.SMEM),
                  pl.BlockSpec((tm * PK_S, LANE), lambda i: (i, 0))],
        out_specs=pl.BlockSpec(memory_space=pl.ANY),
        out_shape=jax.ShapeDtypeStruct((n_rows * PK_S, LANE), U32),
        scratch_shapes=[pltpu.SemaphoreType.DMA(())],
        compiler_params=_cp(("arbitrary",)),
        name="moe_dispatch",
    )(slots, hx_packed)


def _expert_kernel(be_ref, valid_ref, nused_ref, first_ref, next_ref, slot_ref,
                   xs_ref, w1_hbm, w3_hbm, w2_hbm, y_ref, w1s, w3s, w2s, w1b, w3b, w2b, xb, sems, *, layer):
    i = pl.program_id(0)
    bm = EXP_BM

    def weight_copies(e, s):
        return [pltpu.make_async_copy(w_hbm.at[layer, e], stage.at[s], sems.at[s, j])
                for j, (w_hbm, stage) in enumerate(((w1_hbm, w1s), (w3_hbm, w3s), (w2_hbm, w2s)))]

    @pl.when(i < nused_ref[0])
    def _():
        @pl.when(first_ref[i] == 1)
        def _():
            s = slot_ref[i]

            @pl.when(i == 0)
            def _():
                for cp in weight_copies(be_ref[0], 0):
                    cp.start()

            for cp in weight_copies(be_ref[i], s):
                cp.wait()

            @pl.when(next_ref[i] >= 0)
            def _():
                for cp in weight_copies(next_ref[i], 1 - s):
                    cp.start(priority=1)

            w1b[...] = w1s[s].astype(BF16)
            w3b[...] = w3s[s].astype(BF16)
            w2b[...] = w2s[s].astype(BF16)

        live = lax.broadcasted_iota(I32, (bm, LANE), 0) < valid_ref[i]
        for s, piece in enumerate(_load_row_tiles(xs_ref, 0, bm)):
            lo, hi = _unpack_bf16_pairs(jnp.where(live, piece, jnp.uint32(0)))
            xb[:, s * LANE:(s + 1) * LANE] = lo.astype(BF16)
            xb[:, PK_W + s * LANE:PK_W + (s + 1) * LANE] = hi.astype(BF16)
        x = xb[...]
        h1 = jnp.dot(x, w1b[...], preferred_element_type=F32)
        h3 = jnp.dot(x, w3b[...], preferred_element_type=F32)
        a = (h1 * jax.nn.sigmoid(h1) * h3).astype(BF16)
        y = jnp.dot(a, w2b[...], preferred_element_type=F32)
        _store_row_tiles(y_ref, _pack_bf16_pairs(y.astype(BF16)))


def moe_experts(block_e, valid, n_used, first, next_e, slot, xs, w1, w3, w2, layer, n_blocks):
    bm = EXP_BM

    def blk(i, be, va, nu, fi, ne, sl):
        return (jnp.minimum(i, nu[0] - 1), 0)

    grid_spec = pltpu.PrefetchScalarGridSpec(
        num_scalar_prefetch=6,
        grid=(n_blocks,),
        in_specs=[pl.BlockSpec((bm * PK_S, LANE), blk),
                  pl.BlockSpec(memory_space=pl.ANY),
                  pl.BlockSpec(memory_space=pl.ANY),
                  pl.BlockSpec(memory_space=pl.ANY)],
        out_specs=pl.BlockSpec((bm * PK_S, LANE), blk),
        scratch_shapes=[pltpu.VMEM((2, D, EXPERT_DIM), F32), pltpu.VMEM((2, D, EXPERT_DIM), F32),
                        pltpu.VMEM((2, EXPERT_DIM, D), F32),
                        pltpu.VMEM((D, EXPERT_DIM), BF16), pltpu.VMEM((D, EXPERT_DIM), BF16),
                        pltpu.VMEM((EXPERT_DIM, D), BF16), pltpu.VMEM((bm, D), BF16),
                        pltpu.SemaphoreType.DMA((2, 3))],
    )
    return pl.pallas_call(
        functools.partial(_expert_kernel, layer=layer),
        grid_spec=grid_spec,
        out_shape=jax.ShapeDtypeStruct((n_blocks * bm * PK_S, LANE), U32),
        compiler_params=_cp(("arbitrary",), vmem=56 * 1024 * 1024),
        name="moe_experts",
    )(block_e, valid, n_used, first, next_e, slot, xs, w1, w3, w2)


def _shared_kernel(x_ref, w1_ref, w3_ref, w2_ref, o_ref):
    x = x_ref[...]
    h1 = jnp.dot(x, w1_ref[...], preferred_element_type=F32)
    h3 = jnp.dot(x, w3_ref[...], preferred_element_type=F32)
    a = (h1 * jax.nn.sigmoid(h1) * h3).astype(BF16)
    o_ref[...] = jnp.dot(a, w2_ref[...], preferred_element_type=F32)


def shared_expert(hx, w1, w3, w2, n_tok):
    tm = 512
    return pl.pallas_call(
        _shared_kernel,
        grid=(n_tok // tm,),
        in_specs=[pl.BlockSpec((tm, D), lambda i: (i, 0)),
                  pl.BlockSpec((D, EXPERT_DIM), lambda i: (0, 0)),
                  pl.BlockSpec((D, EXPERT_DIM), lambda i: (0, 0)),
                  pl.BlockSpec((EXPERT_DIM, D), lambda i: (0, 0))],
        out_specs=pl.BlockSpec((tm, D), lambda i: (i, 0)),
        out_shape=jax.ShapeDtypeStruct((n_tok, D), F32),
        compiler_params=_cp(("arbitrary",)),
        name="shared_expert",
    )(hx, w1, w3, w2)


COMBINE_TM = 128


def _combine_kernel(slot0_ref, slotn_ref, w_ref, sh_ref, x_ref, mod_ref, y_hbm, *rest, final, n_tiles):
    fg_ref = rest[0] if final else None
    o_ref, buf, sems = rest[-3:]
    tm = COMBINE_TM
    i = pl.program_id(0)
    half_rows = TOP_K * tm * PK_S

    def start_gathers(slot_ref, par):
        half = buf.at[par]
        sem = sems.at[par]

        def issue(tt, carry):
            t8 = pl.multiple_of(tt * ROW_UNROLL, ROW_UNROLL)
            for j in range(ROW_UNROLL):
                for k in range(TOP_K):
                    row = pl.multiple_of(slot_ref[k * tm + j + t8] * PK_S, PK_S)
                    dst = half.at[pl.ds(pl.multiple_of((k * tm + j + t8) * PK_S, PK_S), PK_S), :]
                    pltpu.make_async_copy(y_hbm.at[pl.ds(row, PK_S), :], dst, sem).start(priority=k % 2)
            return carry

        lax.fori_loop(0, tm // ROW_UNROLL, issue, 0)

    @pl.when(i == 0)
    def _():
        start_gathers(slot0_ref, 0)

    @pl.when(i + 1 < n_tiles)
    def _():
        start_gathers(slotn_ref, (i + 1) % 2)

    cur = buf.at[i % 2]
    pltpu.make_async_copy(y_hbm.at[pl.ds(0, half_rows), :], cur, sems.at[i % 2]).wait()

    w = w_ref[...]
    gate = mod_ref[0][5:6]
    acc = [None] * (2 * PK_S)
    for k in range(TOP_K):
        wk = w[:, k:k + 1]
        for s, piece in enumerate(_load_row_tiles(cur, k * tm * PK_S, tm)):
            for c, val in zip((s, PK_S + s), _unpack_bf16_pairs(piece)):
                acc[c] = wk * val if acc[c] is None else acc[c] + wk * val
    outs = []
    for c in range(2 * PK_S):
        sl = slice(c * LANE, (c + 1) * LANE)
        outs.append(x_ref[:, sl] + gate[:, sl] * (sh_ref[:, sl] + acc[c]))
    if fg_ref is not None:
        ssq = functools.reduce(lambda a, b: a + b, [jnp.sum(o * o, axis=-1, keepdims=True) for o in outs])
        inv = lax.rsqrt(ssq / D + EPS)
        outs = [o * inv * fg_ref[:, c * LANE:(c + 1) * LANE] for c, o in enumerate(outs)]
    for c, o in enumerate(outs):
        o_ref[:, c * LANE:(c + 1) * LANE] = o


def moe_combine(slots, wts_tok, shared, x, mod, y, n_tok, final_gain=None):
    tm = COMBINE_TM
    n_tiles = n_tok // tm
    final = final_gain is not None
    in_specs = [pl.BlockSpec((TOP_K * tm,), lambda i: (0,), memory_space=pltpu.SMEM),
                pl.BlockSpec((TOP_K * tm,), lambda i: (jnp.minimum(i + 1, n_tiles - 1),), memory_space=pltpu.SMEM),
                pl.BlockSpec((tm, TOP_K), lambda i: (i, 0)),
                pl.BlockSpec((tm, D), lambda i: (i, 0)),
                pl.BlockSpec((tm, D), lambda i: (i, 0)),
                pl.BlockSpec((1, 6, D), lambda i: (_mod_row(i * tm), 0, 0)),
                pl.BlockSpec(memory_space=pl.ANY)]
    args = [slots, slots, wts_tok, shared, x, mod, y]
    if final:
        in_specs.append(pl.BlockSpec((1, D), lambda i: (0, 0)))
        args.append(final_gain.reshape(1, D))
    return pl.pallas_call(
        functools.partial(_combine_kernel, final=final, n_tiles=n_tiles),
        grid=(n_tiles,),
        in_specs=in_specs,
        out_specs=pl.BlockSpec((tm, D), lambda i: (i, 0)),
        out_shape=jax.ShapeDtypeStruct((n_tok, D), F32),
        scratch_shapes=[pltpu.VMEM((2, TOP_K * tm * PK_S, LANE), U32), pltpu.SemaphoreType.DMA((2,))],
        compiler_params=_cp(("arbitrary",)),
        name="moe_combine",
    )(*args)


def _lookup(table, idx):
    e = jnp.arange(table.shape[0], dtype=I32).reshape((-1,) + (1,) * idx.ndim)
    return jnp.sum(jnp.where(idx[None] == e, table.reshape(e.shape), 0), axis=0)


def _tile_flat(slots, tm):
    k, t = slots.shape
    return slots.reshape(k, t // tm, tm).transpose(1, 0, 2).reshape(-1)


def moe_layer(x, mod, norm_gain, router_w, router_b, exp_w1, exp_w3, exp_w2, sw1, sw3, sw2, layer, n_tok,
              final_gain=None):
    bm = EXP_BM
    n_blocks = -(-n_tok * TOP_K // bm) + N_EXPERTS
    hx, hx_packed = norm_mod(x, norm_gain, mod, 3, n_tok, pack=True)
    eidx, wts, pos, counts = moe_router(hx, router_w, router_b, n_tok)
    shared = shared_expert(hx, sw1.astype(BF16), sw3.astype(BF16), sw2.astype(BF16), n_tok)
    cnt = counts.astype(I32)
    padded = (cnt + bm - 1) // bm * bm
    pad_end = jnp.cumsum(padded)
    pad_start = pad_end - padded
    slots = _lookup(pad_start, eidx) + pos
    blk_row = jnp.arange(n_blocks, dtype=I32) * bm
    block_e = jnp.minimum(jnp.sum((pad_end[:, None] <= blk_row[None, :]).astype(I32), axis=0), N_EXPERTS - 1)
    valid = jnp.clip(_lookup(cnt, block_e) - (blk_row - _lookup(pad_start, block_e)), 0, bm).astype(I32)
    n_used = (pad_end[-1:] // bm).astype(I32)
    prev_e = jnp.concatenate([jnp.full((1,), -1, I32), block_e[:-1]])
    first = ((blk_row < pad_end[-1]) & (block_e != prev_e)).astype(I32)
    stage_slot = ((jnp.cumsum(first) - 1) % 2).astype(I32)
    eids = jnp.arange(N_EXPERTS, dtype=I32)
    later = jnp.where((eids[None, :] > eids[:, None]) & (padded[None, :] > 0), eids[None, :], N_EXPERTS)
    next_nonempty = jnp.min(later, axis=1)
    next_nonempty = jnp.where(next_nonempty == N_EXPERTS, -1, next_nonempty)
    next_e = _lookup(next_nonempty, block_e).astype(I32)
    xs = moe_dispatch(_tile_flat(slots, DISPATCH_TM), hx_packed, n_tok, n_blocks * bm)
    y = moe_experts(block_e, valid, n_used, first, next_e, stage_slot, xs, exp_w1, exp_w3, exp_w2, layer, n_blocks)
    return moe_combine(_tile_flat(slots, COMBINE_TM), wts.T, shared, x, mod, y, n_tok, final_gain)


def _rope_tables():
    t = jnp.arange(S, dtype=I32)
    row = (t // GRID_W).astype(F32)
    col = (t % GRID_W).astype(F32)
    n_freq = HD // 4
    inv_freq = ROPE_THETA ** (-jnp.arange(n_freq, dtype=F32) / n_freq)
    ang = jnp.concatenate([row[:, None] * inv_freq, col[:, None] * inv_freq], axis=-1)
    cosf = jnp.repeat(jnp.cos(ang), 2, axis=-1)
    sinf = jnp.stack([-jnp.sin(ang), jnp.sin(ang)], axis=-1).reshape(S, HD)
    return cosf, sinf


def kernel(x, c, ctx, c_ctx, ada_w, ada_b, norm_mix, norm_ffn, attn_w_in, attn_w_out, attn_rpb, attn_q_gain,
           attn_k_gain, ml_w_in, ml_w_out, ml_gate_b, ml_head_gain, router_w, router_b, exp_w1, exp_w3, exp_w2,
           sh_w1, sh_w3, sh_w2, final_norm_gain):
    depth = ada_w.shape[0]
    x_lat = x.reshape(T_LAT, D)
    x_ctx = ctx.reshape(T_CTX, D)
    cvec = jnp.concatenate([c, c_ctx[None], jnp.zeros((8 - B - 1, D), F32)], axis=0)
    mod_all = ada_ln(cvec, ada_w, ada_b).reshape(depth, 8, 6, D)
    cosf, sinf = _rope_tables()

    mod = mod_all[0]
    p = norm_matmul(x_lat, norm_mix[0], mod, attn_w_in[0].astype(BF16), emit_h=False, x_ctx=x_ctx)
    o_all = neighborhood_attention(p, na_bias_table(attn_rpb[0]))
    o_all = gqa_attention(p, cosf, sinf, attn_q_gain[0], attn_k_gain[0], o_all)
    o_all = ctx_attention(p, attn_q_gain[0], attn_k_gain[0], o_all)
    xa = matmul_gated_residual(o_all, attn_w_out[0].astype(BF16), x_lat, mod, 2, x_ctx=x_ctx)
    xa = moe_layer(xa, mod, norm_ffn[0], router_w[0], router_b[0], exp_w1, exp_w3, exp_w2,
                   sh_w1[0], sh_w3[0], sh_w2[0], 0, T_ALL)

    mod = mod_all[1]
    w_in = ml_w_in[0]
    p, hx = norm_matmul(xa, norm_mix[1], mod, w_in[:, :ML_MAIN].astype(BF16), emit_h=True)
    col, row = mlstm_gates(hx, w_in[:, ML_MAIN:], ml_gate_b[0])
    hdir = mlstm_scan(p, col, row)
    a = mlstm_readout(hdir, p, ml_head_gain[0])
    xl = matmul_gated_residual(a, ml_w_out[0].astype(BF16), xa, mod, 2)
    xl = moe_layer(xl, mod, norm_ffn[1], router_w[1], router_b[1], exp_w1, exp_w3, exp_w2,
                   sh_w1[1], sh_w3[1], sh_w2[1], 1, T_LAT, final_gain=final_norm_gain)
    return xl.reshape(B, S, D)
```

```python
import functools

import jax
import jax.numpy as jnp
from jax import lax
from jax.experimental import pallas as pl
from jax.experimental.pallas import tpu as pltpu

F32 = jnp.float32
BF16 = jnp.bfloat16
I32 = jnp.int32
U32 = jnp.uint32

D = 2048
B = 4
S = 4096
L = 256
T_LAT = B * S
T_CTX = B * L
T_ALL = T_LAT + T_CTX
GRID_W = 64
ROWS = S // GRID_W
HD = 128
NA_HEADS = 8
NA_WIN_ROWS = 8
NA_WIN_COLS = 16
GQA_Q_HEADS = 8
GQA_KV_HEADS = 2
GQA_GROUP = 4
ROPE_THETA = 10000.0
ATTN_IN = 4608
ML_HEADS = 8
ML_V = 256
ML_QK = 128
ML_MAIN = 6144
N_EXPERTS = 64
TOP_K = 8
N_GROUPS = 8
TOPK_GROUPS = 4
EXPERT_DIM = 512
ROUTED_SCALE = 2.5
EPS = 1e-6
NEG_INF = -1e30
ATT_SCALE = HD ** -0.5
LOG2E = 1.4426950408889634
ML_KSCALE = ML_QK ** -0.5

LANE = 128
NA_QROWS = 4
NA_SLAB = NA_QROWS + NA_WIN_ROWS - 1
NA_QB = NA_QROWS * GRID_W
NA_KB = NA_SLAB * GRID_W
ML_CH = 256
EXP_BM = 512
PK_W = D // 2
PK_S = PK_W // LANE
VMEM_LIMIT = 48 * 1024 * 1024


def _cp(sem, vmem=VMEM_LIMIT):
    return pltpu.CompilerParams(dimension_semantics=sem, vmem_limit_bytes=vmem)


def _pack_bf16_pairs(xb):
    u = pltpu.bitcast(xb.astype(F32), U32)
    return (u[:, PK_W:] & jnp.uint32(0xFFFF0000)) | (u[:, :PK_W] >> 16)


def _unpack_bf16_pairs(u):
    return pltpu.bitcast(u << 16, F32), pltpu.bitcast(u & jnp.uint32(0xFFFF0000), F32)


def _store_row_tiles(ref, words):
    rows = words.shape[0]
    for s in range(PK_S):
        ref[pl.ds(s, rows, stride=PK_S), :] = words[:, s * LANE:(s + 1) * LANE]


def _load_row_tiles(ref, start, rows):
    return [ref[pl.ds(start + s, rows, stride=PK_S), :] for s in range(PK_S)]


def _mod_row(start_row):
    return jnp.where(start_row < T_LAT, start_row // S, B)


def _ada_kernel(c_ref, w_ref, b_ref, o_ref):
    c = c_ref[...]
    a = (c * jax.nn.sigmoid(c)).astype(BF16)
    w = w_ref[0].astype(BF16)
    o_ref[0] = jnp.dot(a, w, preferred_element_type=F32) + b_ref[0]


def ada_ln(cvec, ada_w, ada_b):
    depth = ada_w.shape[0]
    n = ada_w.shape[2]
    tn = 1024
    return pl.pallas_call(
        _ada_kernel,
        grid=(depth, n // tn),
        in_specs=[pl.BlockSpec((8, D), lambda l, j: (0, 0)),
                  pl.BlockSpec((1, D, tn), lambda l, j: (l, 0, j)),
                  pl.BlockSpec((1, 1, tn), lambda l, j: (l, 0, j))],
        out_specs=pl.BlockSpec((1, 8, tn), lambda l, j: (l, 0, j)),
        out_shape=jax.ShapeDtypeStruct((depth, 8, n), F32),
        compiler_params=_cp(("arbitrary", "arbitrary")),
        name="ada_ln",
    )(cvec, ada_w, ada_b.reshape(depth, 1, n))


def _norm_mod_kernel(x_ref, g_ref, mod_ref, *out_refs, base, pack):
    x = x_ref[...]
    y = x * lax.rsqrt(jnp.mean(x * x, axis=-1, keepdims=True) + EPS) * g_ref[...]
    m = mod_ref[0]
    h = y * (1.0 + m[base + 1:base + 2]) + m[base:base + 1]
    hb = h.astype(BF16)
    out_refs[0][...] = hb
    if pack:
        _store_row_tiles(out_refs[1], _pack_bf16_pairs(hb))


def norm_mod(x, gain, mod, base, n_rows, pack):
    tm = 256
    out_shape = [jax.ShapeDtypeStruct((n_rows, D), BF16)]
    out_specs = [pl.BlockSpec((tm, D), lambda i: (i, 0))]
    if pack:
        out_shape.append(jax.ShapeDtypeStruct((n_rows * PK_S, LANE), U32))
        out_specs.append(pl.BlockSpec((tm * PK_S, LANE), lambda i: (i, 0)))
    res = pl.pallas_call(
        functools.partial(_norm_mod_kernel, base=base, pack=pack),
        grid=(n_rows // tm,),
        in_specs=[pl.BlockSpec((tm, D), lambda i: (i, 0)),
                  pl.BlockSpec((1, D), lambda i: (0, 0)),
                  pl.BlockSpec((1, 6, D), lambda i: (_mod_row(i * tm), 0, 0))],
        out_specs=out_specs,
        out_shape=out_shape,
        compiler_params=_cp(("arbitrary",)),
        name="norm_mod",
    )(x, gain.reshape(1, D), mod)
    return res if pack else res[0]


def _norm_mm_kernel(*refs, emit_h, n_lat):
    if n_lat is None:
        x_ref, g_ref, mod_ref, w_ref, o_ref = refs[:5]
        c_ref = None
    else:
        x_ref, c_ref, g_ref, mod_ref, w_ref, o_ref = refs[:6]
    hb_ref = refs[-1]

    def prologue(src_ref):
        x = src_ref[...]
        y = x * lax.rsqrt(jnp.mean(x * x, axis=-1, keepdims=True) + EPS) * g_ref[...]
        m = mod_ref[0]
        hb_ref[...] = (y * (1.0 + m[1:2]) + m[0:1]).astype(BF16)
        if emit_h:
            refs[-2][...] = hb_ref[...]

    first_col = pl.program_id(1) == 0
    if c_ref is None:
        pl.when(first_col)(lambda: prologue(x_ref))
    else:
        is_lat = pl.program_id(0) < n_lat
        pl.when(first_col & is_lat)(lambda: prologue(x_ref))
        pl.when(first_col & jnp.logical_not(is_lat))(lambda: prologue(c_ref))

    o_ref[...] = jnp.dot(hb_ref[...], w_ref[...], preferred_element_type=F32).astype(o_ref.dtype)


def norm_matmul(x, gain, mod, w, emit_h, x_ctx=None, tm=1024, tn=512):
    split = x_ctx is not None
    m = x.shape[0] + (x_ctx.shape[0] if split else 0)
    n = w.shape[1]
    n_lat = x.shape[0] // tm if split else None
    out_shape = [jax.ShapeDtypeStruct((m, n), BF16)]
    out_specs = [pl.BlockSpec((tm, tn), lambda i, j: (i, j))]
    if emit_h:
        out_shape.append(jax.ShapeDtypeStruct((m, D), BF16))
        out_specs.append(pl.BlockSpec((tm, D), lambda i, j: (i, 0)))
    if split:
        x_specs = [pl.BlockSpec((tm, D), lambda i, j: (jnp.minimum(i, n_lat - 1), 0)),
                   pl.BlockSpec((tm, D), lambda i, j: (0, 0))]
        x_args = [x, x_ctx]
    else:
        x_specs = [pl.BlockSpec((tm, D), lambda i, j: (i, 0))]
        x_args = [x]
    res = pl.pallas_call(
        functools.partial(_norm_mm_kernel, emit_h=emit_h, n_lat=n_lat),
        grid=(m // tm, n // tn),
        in_specs=x_specs + [pl.BlockSpec((1, D), lambda i, j: (0, 0)),
                            pl.BlockSpec((1, 6, D), lambda i, j: (_mod_row(i * tm), 0, 0)),
                            pl.BlockSpec((D, tn), lambda i, j: (0, j))],
        out_specs=out_specs,
        out_shape=out_shape,
        scratch_shapes=[pltpu.VMEM((tm, D), BF16)],
        compiler_params=_cp(("arbitrary", "arbitrary"), vmem=56 * 1024 * 1024),
        name="norm_matmul",
    )(*x_args, gain.reshape(1, D), mod, w)
    return res if emit_h else res[0]


def _mm_res_kernel(*refs, slot, n_lat):
    if n_lat is None:
        a_ref, w_ref, x_ref, mod_ref, o_ref = refs
        res = x_ref[...]
    else:
        a_ref, w_ref, x_ref, c_ref, mod_ref, o_ref = refs
        res = jnp.where(pl.program_id(0) < n_lat, x_ref[...], c_ref[...])
    acc = jnp.dot(a_ref[...], w_ref[...], preferred_element_type=F32)
    o_ref[...] = res + mod_ref[0][slot:slot + 1] * acc


def matmul_gated_residual(a, w, x, mod, slot, x_ctx=None, tm=1024, tn=512):
    m, k = a.shape
    n = w.shape[1]
    split = x_ctx is not None
    n_lat = x.shape[0] // tm if split else None
    if split:
        x_specs = [pl.BlockSpec((tm, tn), lambda i, j: (jnp.minimum(i, n_lat - 1), j)),
                   pl.BlockSpec((tm, tn), lambda i, j: (0, j))]
        x_args = [x, x_ctx]
    else:
        x_specs = [pl.BlockSpec((tm, tn), lambda i, j: (i, j))]
        x_args = [x]
    return pl.pallas_call(
        functools.partial(_mm_res_kernel, slot=slot, n_lat=n_lat),
        grid=(m // tm, n // tn),
        in_specs=[pl.BlockSpec((tm, k), lambda i, j: (i, 0)),
                  pl.BlockSpec((k, tn), lambda i, j: (0, j))] + x_specs
                 + [pl.BlockSpec((1, 6, tn), lambda i, j: (_mod_row(i * tm), 0, j))],
        out_specs=pl.BlockSpec((tm, tn), lambda i, j: (i, j)),
        out_shape=jax.ShapeDtypeStruct((m, n), F32),
        compiler_params=_cp(("arbitrary", "arbitrary")),
        name="matmul_gated_residual",
    )(a, w, *x_args, mod)


def _dot_nt(a, b):
    return lax.dot_general(a, b, (((1,), (1,)), ((), ())), preferred_element_type=F32)


def _rms_head(x, gain):
    return x * lax.rsqrt(jnp.mean(x * x, axis=-1, keepdims=True) + EPS) * gain


def _rope(x, cosf, sinf):
    lane = lax.broadcasted_iota(I32, x.shape, 1)
    nxt = pltpu.roll(x, LANE - 1, 1)
    prv = pltpu.roll(x, 1, 1)
    return x * cosf + jnp.where((lane & 1) == 0, nxt, prv) * sinf


def _softmax_av(parts):
    m = functools.reduce(jnp.maximum, [jnp.max(s, axis=-1, keepdims=True) for s, _ in parts])
    l = None
    o = None
    for s, v in parts:
        p = jnp.exp(s - m)
        li = jnp.sum(p, axis=-1, keepdims=True)
        oi = jnp.dot(p.astype(BF16), v, preferred_element_type=F32)
        l = li if l is None else l + li
        o = oi if o is None else o + oi
    return o / l


def _na_kernel(q_ref, k_ref, v_ref, kc_ref, vc_ref, tab_ref, o_ref):
    kc = kc_ref[...]
    vc = vc_ref[...]
    n_blocks = ROWS // NA_QROWS

    def body(j, carry):
        ks = jnp.clip(j * NA_QROWS - NA_WIN_ROWS // 2, 0, ROWS - NA_SLAB)
        typ = jnp.where(j == 0, 0, jnp.where(j == n_blocks - 1, 2, 1))
        qs = pl.multiple_of(j * NA_QB, NA_QB)
        kst = pl.multiple_of(ks * GRID_W, GRID_W)
        q = q_ref[pl.ds(qs, NA_QB), :]
        k = k_ref[pl.ds(kst, NA_KB), :]
        v = v_ref[pl.ds(kst, NA_KB), :]
        s_win = _dot_nt(q, k) * ATT_SCALE + tab_ref[typ, 0]
        s_ctx = _dot_nt(q, kc) * ATT_SCALE
        o_ref[pl.ds(qs, NA_QB), :] = _softmax_av([(s_win, v), (s_ctx, vc)]).astype(BF16)
        return carry

    lax.fori_loop(0, n_blocks, body, 0)


def na_bias_table(rpb):
    def one(r0, ks):
        r = r0 + jnp.arange(NA_QROWS)
        kr = ks + jnp.arange(NA_SLAB)
        start = jnp.clip(r - NA_WIN_ROWS // 2, 0, ROWS - NA_WIN_ROWS)
        row_ok = (kr[None, :] >= start[:, None]) & (kr[None, :] < start[:, None] + NA_WIN_ROWS)
        row_idx = jnp.clip(kr[None, :] - r[:, None] + NA_WIN_ROWS - 1, 0, 2 * NA_WIN_ROWS - 2)
        cq = jnp.arange(GRID_W)
        col_start = jnp.clip(cq - NA_WIN_COLS // 2, 0, GRID_W - NA_WIN_COLS)
        col_ok = (cq[None, :] >= col_start[:, None]) & (cq[None, :] < col_start[:, None] + NA_WIN_COLS)
        col_idx = jnp.clip(cq[None, :] - cq[:, None] + NA_WIN_COLS - 1, 0, 2 * NA_WIN_COLS - 2)
        r_hot = jax.nn.one_hot(row_idx, 2 * NA_WIN_ROWS - 1, dtype=F32)
        c_hot = jax.nn.one_hot(col_idx, 2 * NA_WIN_COLS - 1, dtype=F32)
        bias = jnp.einsum('qka,hab,xyb->hqxky', r_hot, rpb.astype(F32), c_hot, precision=lax.Precision.HIGHEST)
        ok = row_ok[:, None, :, None] & col_ok[None, :, None, :]
        return jnp.where(ok[None], bias, NEG_INF).reshape(NA_HEADS, NA_QB, NA_KB)

    mid = 2 * NA_QROWS
    last = ROWS - NA_QROWS
    return jnp.stack([one(0, 0), one(mid, mid - NA_WIN_ROWS // 2), one(last, ROWS - NA_SLAB)])


def neighborhood_attention(p, table):
    cb = S // L
    return pl.pallas_call(
        _na_kernel,
        grid=(NA_HEADS, B),
        in_specs=[pl.BlockSpec((S, HD), lambda h, b: (b, h)),
                  pl.BlockSpec((S, HD), lambda h, b: (b, NA_HEADS + h)),
                  pl.BlockSpec((S, HD), lambda h, b: (b, 2 * NA_HEADS + h)),
                  pl.BlockSpec((L, HD), lambda h, b: (B * cb + b, NA_HEADS + h)),
                  pl.BlockSpec((L, HD), lambda h, b: (B * cb + b, 2 * NA_HEADS + h)),
                  pl.BlockSpec((3, 1, NA_QB, NA_KB), lambda h, b: (0, h, 0, 0))],
        out_specs=pl.BlockSpec((S, HD), lambda h, b: (b, h)),
        out_shape=jax.ShapeDtypeStruct((T_ALL, D), BF16),
        compiler_params=_cp(("arbitrary", "arbitrary")),
        name="neighborhood_attention",
    )(p, p, p, p, p, table)


GQA_TQ = 512
GQA_CK = 1024
GQA_QCOL = 3 * NA_HEADS
GQA_KCOL = GQA_QCOL + GQA_Q_HEADS
GQA_VCOL = GQA_KCOL + GQA_KV_HEADS


def _gqa_kernel(q_ref, k_ref, v_ref, kc_ref, vc_ref, cq_ref, sq_ref, ck_ref, sk_ref, qg_ref, kg_ref, o_prev,
                o_ref, kn_ref, kcn_ref):
    del o_prev
    @pl.when(pl.program_id(2) == 0)
    def _():
        kn = _rope(_rms_head(k_ref[...].astype(F32), kg_ref[...]), ck_ref[...], sk_ref[...])
        kn_ref[...] = kn.astype(BF16)
        kcn_ref[...] = _rms_head(kc_ref[...].astype(F32), kg_ref[...]).astype(BF16)

    cos = cq_ref[...]
    sin = sq_ref[...]
    heads = []
    for g in range(GQA_GROUP):
        qh = _rope(_rms_head(q_ref[:, g * HD:(g + 1) * HD].astype(F32), qg_ref[...]), cos, sin)
        heads.append((qh * (ATT_SCALE * LOG2E)).astype(BF16))
    q = jnp.concatenate(heads, axis=0)
    chunks = [(kn_ref[c * GQA_CK:(c + 1) * GQA_CK, :], v_ref[c * GQA_CK:(c + 1) * GQA_CK, :])
              for c in range(S // GQA_CK)]
    chunks.append((kcn_ref[...], vc_ref[...]))
    m = l = acc = None
    for kk, vv in chunks:
        s = _dot_nt(q, kk)
        mc = jnp.max(s, axis=-1, keepdims=True)
        if m is None:
            m_new = mc
            p = jnp.exp2(s - m_new)
            l = jnp.sum(p, axis=-1, keepdims=True)
            acc = jnp.dot(p.astype(BF16), vv, preferred_element_type=F32)
        else:
            m_new = jnp.maximum(m, mc)
            alpha = jnp.exp2(m - m_new)
            p = jnp.exp2(s - m_new)
            l = alpha * l + jnp.sum(p, axis=-1, keepdims=True)
            acc = alpha * acc + jnp.dot(p.astype(BF16), vv, preferred_element_type=F32)
        m = m_new
    o = acc / l
    for g in range(GQA_GROUP):
        o_ref[:, g * HD:(g + 1) * HD] = o[g * GQA_TQ:(g + 1) * GQA_TQ].astype(BF16)


def gqa_attention(p, cosf, sinf, q_gain, k_gain, o_buf):
    nq = S // GQA_TQ
    cb = S // L
    gw = GQA_GROUP * HD
    return pl.pallas_call(
        _gqa_kernel,
        grid=(B, GQA_KV_HEADS, nq),
        in_specs=[pl.BlockSpec((GQA_TQ, gw), lambda b, n, i: (b * nq + i, GQA_QCOL // GQA_GROUP + n)),
                  pl.BlockSpec((S, HD), lambda b, n, i: (b, GQA_KCOL + n)),
                  pl.BlockSpec((S, HD), lambda b, n, i: (b, GQA_VCOL + n)),
                  pl.BlockSpec((L, HD), lambda b, n, i: (B * cb + b, GQA_KCOL + n)),
                  pl.BlockSpec((L, HD), lambda b, n, i: (B * cb + b, GQA_VCOL + n)),
                  pl.BlockSpec((GQA_TQ, HD), lambda b, n, i: (i, 0)),
                  pl.BlockSpec((GQA_TQ, HD), lambda b, n, i: (i, 0)),
                  pl.BlockSpec((S, HD), lambda b, n, i: (0, 0)),
                  pl.BlockSpec((S, HD), lambda b, n, i: (0, 0)),
                  pl.BlockSpec((1, HD), lambda b, n, i: (0, 0)),
                  pl.BlockSpec((1, HD), lambda b, n, i: (0, 0)),
                  pl.BlockSpec(memory_space=pl.ANY)],
        out_specs=pl.BlockSpec((GQA_TQ, gw), lambda b, n, i: (b * nq + i, (NA_HEADS * HD) // gw + n)),
        out_shape=jax.ShapeDtypeStruct((T_ALL, D), BF16),
        input_output_aliases={11: 0},
        scratch_shapes=[pltpu.VMEM((S, HD), BF16), pltpu.VMEM((L, HD), BF16)],
        compiler_params=_cp(("arbitrary",) * 3),
        name="gqa_attention",
    )(p, p, p, p, p, cosf, sinf, cosf, sinf, q_gain.reshape(1, HD), k_gain.reshape(1, HD), o_buf)


def _ctx_attn_kernel(p_ref, qg_ref, kg_ref, o_prev, o_ref):
    del o_prev

    def col(c):
        return p_ref[:, c * HD:(c + 1) * HD]

    for h in range(NA_HEADS):
        s = _dot_nt(col(h), col(NA_HEADS + h)) * ATT_SCALE
        o_ref[:, h * HD:(h + 1) * HD] = _softmax_av([(s, col(2 * NA_HEADS + h))]).astype(BF16)
    for n in range(GQA_KV_HEADS):
        kn = _rms_head(col(GQA_KCOL + n).astype(F32), kg_ref[...]).astype(BF16)
        v = col(GQA_VCOL + n)
        for g in range(GQA_GROUP):
            h = n * GQA_GROUP + g
            qn = _rms_head(col(GQA_QCOL + h).astype(F32), qg_ref[...]).astype(BF16)
            s = _dot_nt(qn, kn) * ATT_SCALE
            o_ref[:, (NA_HEADS + h) * HD:(NA_HEADS + h + 1) * HD] = _softmax_av([(s, v)]).astype(BF16)


def ctx_attention(p, q_gain, k_gain, o_buf):
    cb = S // L
    return pl.pallas_call(
        _ctx_attn_kernel,
        grid=(B,),
        in_specs=[pl.BlockSpec((L, ATTN_IN), lambda b: (B * cb + b, 0)),
                  pl.BlockSpec((1, HD), lambda b: (0, 0)),
                  pl.BlockSpec((1, HD), lambda b: (0, 0)),
                  pl.BlockSpec(memory_space=pl.ANY)],
        out_specs=pl.BlockSpec((L, D), lambda b: (B * cb + b, 0)),
        out_shape=jax.ShapeDtypeStruct((T_ALL, D), BF16),
        input_output_aliases={3: 0},
        compiler_params=_cp(("arbitrary",)),
        name="ctx_attention",
    )(p, q_gain.reshape(1, HD), k_gain.reshape(1, HD), o_buf)


def _log_sigmoid(x):
    return -(jnp.maximum(-x, 0.0) + jnp.log1p(jnp.exp(-jnp.abs(x))))


def _dot_hi(a, b):
    return jnp.dot(a, b, precision=lax.Precision.HIGHEST, preferred_element_type=F32)


def _gate_kernel(h_ref, wg_ref, wgt_ref, b_ref, bt_ref, lt_ref, ut_ref, col_ref, row_ref):
    nh = ML_HEADS
    hx = h_ref[...]
    g = jnp.dot(hx, wg_ref[...], preferred_element_type=F32) + b_ref[...]
    gt = _dot_nt(wgt_ref[...], hx) + bt_ref[...]
    li = g[:, 0:2 * nh]
    lf = _log_sigmoid(g[:, 2 * nh:4 * nh])
    lit = gt[0:2 * nh]
    lft = _log_sigmoid(gt[2 * nh:4 * nh])
    lt = lt_ref[...]
    ut = ut_ref[...]
    lane = lax.broadcasted_iota(I32, lf.shape, 1)
    bc = jnp.where(lane < nh, _dot_hi(lt, lf), _dot_hi(ut, lf))
    tot = jnp.sum(lf, axis=0, keepdims=True)
    aend = tot - bc + li
    col_ref[...] = jnp.concatenate([bc, aend, jnp.zeros((ML_CH, LANE - 4 * nh), F32)], axis=1)
    sub = lax.broadcasted_iota(I32, lft.shape, 0)
    bct = jnp.where(sub < nh, _dot_hi(lft, ut), _dot_hi(lft, lt))
    tott = jnp.sum(lft, axis=1, keepdims=True)
    gtr = lit - bct
    row_ref[0] = jnp.concatenate([bct, gtr, tott + gtr, jnp.broadcast_to(tott, bct.shape)], axis=0)


def mlstm_gates(hx, wg, gate_b):
    nh = ML_HEADS
    n_ch = T_ALL // ML_CH
    wg_pad = jnp.zeros((D, LANE), BF16).at[:, :4 * nh].set(wg.astype(BF16))
    b_pad = jnp.zeros((1, LANE), F32).at[0, :4 * nh].set(gate_b.reshape(-1))
    wgt = wg.astype(BF16).T
    bt = gate_b.reshape(4 * nh, 1).astype(F32)
    lt = jnp.tril(jnp.ones((ML_CH, ML_CH), F32))
    ut = jnp.triu(jnp.ones((ML_CH, ML_CH), F32))
    col, row = pl.pallas_call(
        _gate_kernel,
        grid=(n_ch,),
        in_specs=[pl.BlockSpec((ML_CH, D), lambda i: (i, 0)),
                  pl.BlockSpec((D, LANE), lambda i: (0, 0)),
                  pl.BlockSpec((4 * nh, D), lambda i: (0, 0)),
                  pl.BlockSpec((1, LANE), lambda i: (0, 0)),
                  pl.BlockSpec((4 * nh, 1), lambda i: (0, 0)),
                  pl.BlockSpec((ML_CH, ML_CH), lambda i: (0, 0)),
                  pl.BlockSpec((ML_CH, ML_CH), lambda i: (0, 0))],
        out_specs=[pl.BlockSpec((ML_CH, LANE), lambda i: (i, 0)),
                   pl.BlockSpec((1, 8 * nh, ML_CH), lambda i: (i, 0, 0))],
        out_shape=[jax.ShapeDtypeStruct((T_ALL, LANE), F32),
                   jax.ShapeDtypeStruct((n_ch, 8 * nh, ML_CH), F32)],
        compiler_params=_cp(("arbitrary",)),
        name="mlstm_gates",
    )(hx, wg_pad, wgt, b_pad, bt, lt, ut)
    return col, row


def _mlstm_step(d, hh, q_ref, k_ref, v_ref, col_ref, row_ref, o_ref, c_ref, n_ref, m_ref):
    nd = 2 * ML_HEADS
    sl = d * ML_HEADS + hh
    q = q_ref[:, hh * ML_QK:(hh + 1) * ML_QK]
    kf = k_ref[:, hh * ML_QK:(hh + 1) * ML_QK].astype(F32) * ML_KSCALE
    kb = kf.astype(BF16)
    v = v_ref[:, hh * ML_V:(hh + 1) * ML_V]
    bc_col = col_ref[:, sl:sl + 1]
    aend_col = col_ref[:, nd + sl:nd + sl + 1]
    g_row = row_ref[0, nd + sl:nd + sl + 1, :]
    aend_row = row_ref[0, 2 * nd + sl:2 * nd + sl + 1, :]
    btot = row_ref[0, 3 * nd + sl:3 * nd + sl + 1, 0:1]
    m_st = m_ref[sl]
    c_st = c_ref[sl]
    n_st = n_ref[sl]
    m_new = jnp.maximum(btot + m_st, jnp.max(aend_row, axis=1, keepdims=True))

    r = lax.broadcasted_iota(I32, (ML_CH, ML_CH), 0)
    c = lax.broadcasted_iota(I32, (ML_CH, ML_CH), 1)
    causal = (r >= c) if d == 0 else (r <= c)
    d_mat = jnp.where(causal, bc_col + g_row, -jnp.inf)
    m_row = jnp.maximum(bc_col + m_st, jnp.max(d_mat, axis=1, keepdims=True))
    w_inter = jnp.exp(bc_col + m_st - m_row)
    s_mat = _dot_nt(q, kb) * jnp.exp(d_mat - m_row)
    num = (w_inter * jnp.dot(q, c_st.astype(BF16), preferred_element_type=F32)
           + jnp.dot(s_mat.astype(BF16), v, preferred_element_type=F32))
    den = (w_inter * jnp.sum(q.astype(F32) * n_st, axis=1, keepdims=True)
           + jnp.sum(s_mat, axis=1, keepdims=True))
    h_out = num / jnp.maximum(jnp.abs(den), jnp.exp(-m_row))
    o_ref[:, hh * ML_V:(hh + 1) * ML_V] = h_out.astype(o_ref.dtype)

    w_end_col = jnp.exp(aend_col - m_new)
    w_end_row = jnp.exp(aend_row - m_new)
    decay = jnp.exp(btot + m_st - m_new)
    kw = (kf * w_end_col).astype(BF16)
    c_ref[sl] = decay * c_st + lax.dot_general(kw, v, (((0,), (0,)), ((), ())), preferred_element_type=F32)
    w8 = jnp.broadcast_to(w_end_row, (8, ML_CH)).astype(BF16)
    n_ref[sl] = decay * n_st + jnp.dot(w8, kb, preferred_element_type=F32)[0:1]
    m_ref[sl] = m_new


def _mlstm_kernel(qf, kf, vf, colf, rowf, qb, kb, vb, colb, rowb, of, ob, c_ref, n_ref, m_ref):
    @pl.when(pl.program_id(1) == 0)
    def _():
        c_ref[...] = jnp.zeros_like(c_ref)
        n_ref[...] = jnp.zeros_like(n_ref)
        m_ref[...] = jnp.zeros_like(m_ref)

    for hh in range(ML_HEADS):
        _mlstm_step(0, hh, qf, kf, vf, colf, rowf, of, c_ref, n_ref, m_ref)
        _mlstm_step(1, hh, qb, kb, vb, colb, rowb, ob, c_ref, n_ref, m_ref)


def mlstm_scan(p, col, row):
    n_lat = S // ML_CH
    steps = n_lat + 1
    lat_blocks = T_LAT // ML_CH
    qk_w = ML_HEADS * ML_QK
    v_w = ML_HEADS * ML_V
    n_chains = 2 * ML_HEADS

    def chunk(b, d, st):
        c = (st - 1) if d == 0 else (n_lat - st)
        return jnp.where(st == 0, lat_blocks + b, b * n_lat + c)

    def out_chunk(b, d, st):
        s1 = jnp.maximum(st, 1)
        return b * n_lat + ((s1 - 1) if d == 0 else (n_lat - s1))

    def dir_specs(d):
        return [pl.BlockSpec((ML_CH, qk_w), lambda b, s: (chunk(b, d, s), 0)),
                pl.BlockSpec((ML_CH, qk_w), lambda b, s: (chunk(b, d, s), 1)),
                pl.BlockSpec((ML_CH, v_w), lambda b, s: (chunk(b, d, s), (2 * qk_w) // v_w)),
                pl.BlockSpec((ML_CH, LANE), lambda b, s: (chunk(b, d, s), 0)),
                pl.BlockSpec((1, 4 * n_chains, ML_CH), lambda b, s: (chunk(b, d, s), 0, 0))]

    return pl.pallas_call(
        _mlstm_kernel,
        grid=(B, steps),
        in_specs=dir_specs(0) + dir_specs(1),
        out_specs=[pl.BlockSpec((ML_CH, v_w), lambda b, s: (out_chunk(b, 0, s), 0)),
                   pl.BlockSpec((ML_CH, v_w), lambda b, s: (out_chunk(b, 1, s), 0))],
        out_shape=[jax.ShapeDtypeStruct((T_LAT, v_w), BF16), jax.ShapeDtypeStruct((T_LAT, v_w), BF16)],
        scratch_shapes=[pltpu.VMEM((n_chains, ML_QK, ML_V), F32), pltpu.VMEM((n_chains, 1, ML_QK), F32),
                        pltpu.VMEM((n_chains, 1, 1), F32)],
        compiler_params=_cp(("arbitrary",) * 2),
        name="mlstm_scan",
    )(p, p, p, col, row, p, p, p, col, row)


def _readout_kernel(hf_ref, hb_ref, o_ref, g_ref, a_ref):
    hs = hf_ref[...].astype(F32) + hb_ref[...].astype(F32)
    for h in range(ML_HEADS):
        sl = slice(h * ML_V, (h + 1) * ML_V)
        x = hs[:, sl]
        hn = x * lax.rsqrt(jnp.mean(x * x, axis=-1, keepdims=True) + EPS) * g_ref[:, sl]
        a_ref[:, sl] = (hn * jax.nn.sigmoid(o_ref[:, sl].astype(F32))).astype(BF16)


def mlstm_readout(hdir, p, head_gain):
    tm = 256
    ocol = (2 * ML_HEADS * ML_QK + ML_HEADS * ML_V) // D
    return pl.pallas_call(
        _readout_kernel,
        grid=(T_LAT // tm,),
        in_specs=[pl.BlockSpec((tm, D), lambda i: (i, 0)),
                  pl.BlockSpec((tm, D), lambda i: (i, 0)),
                  pl.BlockSpec((tm, D), lambda i: (i, ocol)),
                  pl.BlockSpec((1, D), lambda i: (0, 0))],
        out_specs=pl.BlockSpec((tm, D), lambda i: (i, 0)),
        out_shape=jax.ShapeDtypeStruct((T_LAT, D), BF16),
        compiler_params=_cp(("arbitrary",)),
        name="mlstm_readout",
    )(hdir[0], hdir[1], p, head_gain.reshape(1, D))


ROUTER_TM = 512


def _router_kernel(h_ref, w_ref, rb_ref, erow_ref, tri_ref, eidx_ref, wts_ref, pos_ref, cnt_ref, carry_ref):
    ng = N_GROUPS
    epg = N_EXPERTS // N_GROUPS
    tm = ROUTER_TM
    ninf = -jnp.inf

    @pl.when(pl.program_id(0) == 0)
    def _():
        carry_ref[...] = jnp.zeros_like(carry_ref)

    s = jax.nn.sigmoid(_dot_nt(w_ref[...], h_ref[...]))
    ssel = s + rb_ref[...]
    sraw = [s[ng * j:ng * (j + 1)] for j in range(epg)]
    slab = [ssel[ng * j:ng * (j + 1)] for j in range(epg)]
    m1 = functools.reduce(jnp.maximum, slab)
    jfirst = functools.reduce(jnp.minimum, [jnp.where(slab[j] == m1, j, epg) for j in range(epg)])
    m2 = functools.reduce(jnp.maximum, [jnp.where(jfirst == j, ninf, slab[j]) for j in range(epg)])
    gs = m1 + m2
    giota = lax.broadcasted_iota(I32, (ng, tm), 0)
    gsel = jnp.zeros((ng, tm), F32)
    for _ in range(TOPK_GROUPS):
        mx = jnp.max(gs, axis=0, keepdims=True)
        gi = jnp.min(jnp.where(gs == mx, giota, ng), axis=0, keepdims=True)
        hit = giota == gi
        gsel = jnp.where(hit, 1.0, gsel)
        gs = jnp.where(hit, ninf, gs)
    msl = [jnp.where(gsel > 0.0, slab[j], ninf) for j in range(epg)]
    eid = [giota * epg + j for j in range(epg)]
    selm = [jnp.zeros((ng, tm), F32) for _ in range(epg)]
    e_list, w_list = [], []
    for _ in range(TOP_K):
        mx = jnp.max(functools.reduce(jnp.maximum, msl), axis=0, keepdims=True)
        cand = functools.reduce(jnp.minimum, [jnp.where(msl[j] == mx, eid[j], N_EXPERTS) for j in range(epg)])
        esel = jnp.min(cand, axis=0, keepdims=True)
        hits = [eid[j] == esel for j in range(epg)]
        wk = functools.reduce(lambda a, b: a + b, [jnp.where(hits[j], sraw[j], 0.0) for j in range(epg)])
        w_list.append(jnp.sum(wk, axis=0, keepdims=True))
        e_list.append(esel)
        msl = [jnp.where(hits[j], ninf, msl[j]) for j in range(epg)]
        selm = [jnp.where(hits[j], 1.0, selm[j]) for j in range(epg)]
    wsum = functools.reduce(lambda a, b: a + b, w_list)
    wts_ref[...] = jnp.concatenate([w / wsum * ROUTED_SCALE for w in w_list], axis=0)
    eidx_ref[...] = jnp.concatenate(e_list, axis=0)
    sel = jnp.concatenate(selm, axis=0)
    carry = carry_ref[...]
    posfull = jnp.dot(sel.astype(BF16), tri_ref[...], preferred_element_type=F32) + carry
    erow = erow_ref[...]
    pos = [jnp.sum(jnp.where(erow == e, posfull, 0.0), axis=0, keepdims=True) for e in e_list]
    pos_ref[...] = jnp.concatenate(pos, axis=0).astype(I32)
    carry = carry + jnp.sum(sel, axis=1, keepdims=True)
    carry_ref[...] = carry
    cnt_ref[...] = carry


def moe_router(hx, router_w, router_b, n_tok):
    tm = ROUTER_TM
    epg = N_EXPERTS // N_GROUPS
    perm = (jnp.arange(N_EXPERTS) % N_GROUPS) * epg + jnp.arange(N_EXPERTS) // N_GROUPS
    w_t = router_w.astype(BF16).T[perm]
    rb = router_b.astype(F32)[perm].reshape(N_EXPERTS, 1)
    erow = perm.astype(I32).reshape(N_EXPERTS, 1)
    tri = jnp.triu(jnp.ones((tm, tm), BF16), 1)
    eidx, wts, pos, counts = pl.pallas_call(
        _router_kernel,
        grid=(n_tok // tm,),
        in_specs=[pl.BlockSpec((tm, D), lambda i: (i, 0)),
                  pl.BlockSpec((N_EXPERTS, D), lambda i: (0, 0)),
                  pl.BlockSpec((N_EXPERTS, 1), lambda i: (0, 0)),
                  pl.BlockSpec((N_EXPERTS, 1), lambda i: (0, 0)),
                  pl.BlockSpec((tm, tm), lambda i: (0, 0))],
        out_specs=[pl.BlockSpec((TOP_K, tm), lambda i: (0, i)),
                   pl.BlockSpec((TOP_K, tm), lambda i: (0, i)),
                   pl.BlockSpec((TOP_K, tm), lambda i: (0, i)),
                   pl.BlockSpec((N_EXPERTS, 1), lambda i: (0, 0))],
        out_shape=[jax.ShapeDtypeStruct((TOP_K, n_tok), I32),
                   jax.ShapeDtypeStruct((TOP_K, n_tok), F32),
                   jax.ShapeDtypeStruct((TOP_K, n_tok), I32),
                   jax.ShapeDtypeStruct((N_EXPERTS, 1), F32)],
        scratch_shapes=[pltpu.VMEM((N_EXPERTS, 1), F32)],
        compiler_params=_cp(("arbitrary",)),
        name="moe_router",
    )(hx, w_t, rb, erow, tri)
    return eidx, wts, pos, counts.reshape(N_EXPERTS)[perm]


DISPATCH_TM = 512


ROW_UNROLL = 8


def _dispatch_kernel(slot_ref, hx_ref, xs_hbm, sem):
    def issue(tt, carry):
        t8 = pl.multiple_of(tt * ROW_UNROLL, ROW_UNROLL)
        for j in range(ROW_UNROLL):
            src = hx_ref.at[pl.ds(pl.multiple_of((t8 + j) * PK_S, PK_S), PK_S), :]
            for k in range(TOP_K):
                row = pl.multiple_of(slot_ref[k * DISPATCH_TM + j + t8] * PK_S, PK_S)
                pltpu.make_async_copy(src, xs_hbm.at[pl.ds(row, PK_S), :], sem).start(priority=k % 2)
        return carry

    lax.fori_loop(0, DISPATCH_TM // ROW_UNROLL, issue, 0)
    for _ in range(TOP_K):
        pltpu.make_async_copy(hx_ref, xs_hbm.at[pl.ds(0, DISPATCH_TM * PK_S), :], sem).wait()


def moe_dispatch(slots, hx_packed, n_tok, n_rows):
    tm = DISPATCH_TM
    return pl.pallas_call(
        _dispatch_kernel,
        grid=(n_tok // tm,),
        in_specs=[pl.BlockSpec((TOP_K * tm,), lambda i: (i,), memory_space=pltpu.SMEM),
                  pl.BlockSpec((tm * PK_S, LANE), lambda i: (i, 0))],
        out_specs=pl.BlockSpec(memory_space=pl.ANY),
        out_shape=jax.ShapeDtypeStruct((n_rows * PK_S, LANE), U32),
        scratch_shapes=[pltpu.SemaphoreType.DMA(())],
        compiler_params=_cp(("arbitrary",)),
        name="moe_dispatch",
    )(slots, hx_packed)


def _expert_kernel(be_ref, valid_ref, nused_ref, first_ref, next_ref, slot_ref,
                   xs_ref, w1_hbm, w3_hbm, w2_hbm, y_ref, w1s, w3s, w2s, w1b, w3b, w2b, xb, sems, *, layer):
    i = pl.program_id(0)
    bm = EXP_BM

    def weight_copies(e, s):
        return [pltpu.make_async_copy(w_hbm.at[layer, e], stage.at[s], sems.at[s, j])
                for j, (w_hbm, stage) in enumerate(((w1_hbm, w1s), (w3_hbm, w3s), (w2_hbm, w2s)))]

    @pl.when(i < nused_ref[0])
    def _():
        @pl.when(first_ref[i] == 1)
        def _():
            s = slot_ref[i]

            @pl.when(i == 0)
            def _():
                for cp in weight_copies(be_ref[0], 0):
                    cp.start()

            for cp in weight_copies(be_ref[i], s):
                cp.wait()

            @pl.when(next_ref[i] >= 0)
            def _():
                for cp in weight_copies(next_ref[i], 1 - s):
                    cp.start(priority=1)

            w1b[...] = w1s[s].astype(BF16)
            w3b[...] = w3s[s].astype(BF16)
            w2b[...] = w2s[s].astype(BF16)

        live = lax.broadcasted_iota(I32, (bm, LANE), 0) < valid_ref[i]
        for s, piece in enumerate(_load_row_tiles(xs_ref, 0, bm)):
            lo, hi = _unpack_bf16_pairs(jnp.where(live, piece, jnp.uint32(0)))
            xb[:, s * LANE:(s + 1) * LANE] = lo.astype(BF16)
            xb[:, PK_W + s * LANE:PK_W + (s + 1) * LANE] = hi.astype(BF16)
        x = xb[...]
        h1 = jnp.dot(x, w1b[...], preferred_element_type=F32)
        h3 = jnp.dot(x, w3b[...], preferred_element_type=F32)
        a = (h1 * jax.nn.sigmoid(h1) * h3).astype(BF16)
        y = jnp.dot(a, w2b[...], preferred_element_type=F32)
        _store_row_tiles(y_ref, _pack_bf16_pairs(y.astype(BF16)))


def moe_experts(block_e, valid, n_used, first, next_e, slot, xs, w1, w3, w2, layer, n_blocks):
    bm = EXP_BM

    def blk(i, be, va, nu, fi, ne, sl):
        return (jnp.minimum(i, nu[0] - 1), 0)

    grid_spec = pltpu.PrefetchScalarGridSpec(
        num_scalar_prefetch=6,
        grid=(n_blocks,),
        in_specs=[pl.BlockSpec((bm * PK_S, LANE), blk),
                  pl.BlockSpec(memory_space=pl.ANY),
                  pl.BlockSpec(memory_space=pl.ANY),
                  pl.BlockSpec(memory_space=pl.ANY)],
        out_specs=pl.BlockSpec((bm * PK_S, LANE), blk),
        scratch_shapes=[pltpu.VMEM((2, D, EXPERT_DIM), F32), pltpu.VMEM((2, D, EXPERT_DIM), F32),
                        pltpu.VMEM((2, EXPERT_DIM, D), F32),
                        pltpu.VMEM((D, EXPERT_DIM), BF16), pltpu.VMEM((D, EXPERT_DIM), BF16),
                        pltpu.VMEM((EXPERT_DIM, D), BF16), pltpu.VMEM((bm, D), BF16),
                        pltpu.SemaphoreType.DMA((2, 3))],
    )
    return pl.pallas_call(
        functools.partial(_expert_kernel, layer=layer),
        grid_spec=grid_spec,
        out_shape=jax.ShapeDtypeStruct((n_blocks * bm * PK_S, LANE), U32),
        compiler_params=_cp(("arbitrary",), vmem=56 * 1024 * 1024),
        name="moe_experts",
    )(block_e, valid, n_used, first, next_e, slot, xs, w1, w3, w2)


def _shared_kernel(x_ref, w1_ref, w3_ref, w2_ref, o_ref):
    x = x_ref[...]
    h1 = jnp.dot(x, w1_ref[...], preferred_element_type=F32)
    h3 = jnp.dot(x, w3_ref[...], preferred_element_type=F32)
    a = (h1 * jax.nn.sigmoid(h1) * h3).astype(BF16)
    o_ref[...] = jnp.dot(a, w2_ref[...], preferred_element_type=F32).astype(o_ref.dtype)


def shared_expert(hx, w1, w3, w2, n_tok):
    tm = 512
    return pl.pallas_call(
        _shared_kernel,
        grid=(n_tok // tm,),
        in_specs=[pl.BlockSpec((tm, D), lambda i: (i, 0)),
                  pl.BlockSpec((D, EXPERT_DIM), lambda i: (0, 0)),
                  pl.BlockSpec((D, EXPERT_DIM), lambda i: (0, 0)),
                  pl.BlockSpec((EXPERT_DIM, D), lambda i: (0, 0))],
        out_specs=pl.BlockSpec((tm, D), lambda i: (i, 0)),
        out_shape=jax.ShapeDtypeStruct((n_tok, D), BF16),
        compiler_params=_cp(("arbitrary",)),
        name="shared_expert",
    )(hx, w1, w3, w2)


COMBINE_TM = 128


def _combine_kernel(slot0_ref, slotn_ref, w_ref, sh_ref, x_ref, mod_ref, y_hbm, *rest, final, n_tiles):
    fg_ref = rest[0] if final else None
    o_ref, buf, sems = rest[-3:]
    tm = COMBINE_TM
    i = pl.program_id(0)
    half_rows = TOP_K * tm * PK_S

    def start_gathers(slot_ref, par):
        half = buf.at[par]
        sem = sems.at[par]

        def issue(tt, carry):
            t8 = pl.multiple_of(tt * ROW_UNROLL, ROW_UNROLL)
            for j in range(ROW_UNROLL):
                for k in range(TOP_K):
                    row = pl.multiple_of(slot_ref[k * tm + j + t8] * PK_S, PK_S)
                    dst = half.at[pl.ds(pl.multiple_of((k * tm + j + t8) * PK_S, PK_S), PK_S), :]
                    pltpu.make_async_copy(y_hbm.at[pl.ds(row, PK_S), :], dst, sem).start(priority=k % 2)
            return carry

        lax.fori_loop(0, tm // ROW_UNROLL, issue, 0)

    @pl.when(i == 0)
    def _():
        start_gathers(slot0_ref, 0)

    @pl.when(i + 1 < n_tiles)
    def _():
        start_gathers(slotn_ref, (i + 1) % 2)

    cur = buf.at[i % 2]
    pltpu.make_async_copy(y_hbm.at[pl.ds(0, half_rows), :], cur, sems.at[i % 2]).wait()

    w = w_ref[...]
    gate = mod_ref[0][5:6]
    acc = [None] * (2 * PK_S)
    for k in range(TOP_K):
        wk = w[:, k:k + 1]
        for s, piece in enumerate(_load_row_tiles(cur, k * tm * PK_S, tm)):
            for c, val in zip((s, PK_S + s), _unpack_bf16_pairs(piece)):
                acc[c] = wk * val if acc[c] is None else acc[c] + wk * val
    outs = []
    for c in range(2 * PK_S):
        sl = slice(c * LANE, (c + 1) * LANE)
        outs.append(x_ref[:, sl] + gate[:, sl] * (sh_ref[:, sl].astype(F32) + acc[c]))
    if fg_ref is not None:
        ssq = functools.reduce(lambda a, b: a + b, [jnp.sum(o * o, axis=-1, keepdims=True) for o in outs])
        inv = lax.rsqrt(ssq / D + EPS)
        outs = [o * inv * fg_ref[:, c * LANE:(c + 1) * LANE] for c, o in enumerate(outs)]
    for c, o in enumerate(outs):
        o_ref[:, c * LANE:(c + 1) * LANE] = o


def moe_combine(slots, wts_tok, shared, x, mod, y, n_tok, final_gain=None):
    tm = COMBINE_TM
    n_tiles = n_tok // tm
    final = final_gain is not None
    in_specs = [pl.BlockSpec((TOP_K * tm,), lambda i: (0,), memory_space=pltpu.SMEM),
                pl.BlockSpec((TOP_K * tm,), lambda i: (jnp.minimum(i + 1, n_tiles - 1),), memory_space=pltpu.SMEM),
                pl.BlockSpec((tm, TOP_K), lambda i: (i, 0)),
                pl.BlockSpec((tm, D), lambda i: (i, 0)),
                pl.BlockSpec((tm, D), lambda i: (i, 0)),
                pl.BlockSpec((1, 6, D), lambda i: (_mod_row(i * tm), 0, 0)),
                pl.BlockSpec(memory_space=pl.ANY)]
    args = [slots, slots, wts_tok, shared, x, mod, y]
    if final:
        in_specs.append(pl.BlockSpec((1, D), lambda i: (0, 0)))
        args.append(final_gain.reshape(1, D))
    return pl.pallas_call(
        functools.partial(_combine_kernel, final=final, n_tiles=n_tiles),
        grid=(n_tiles,),
        in_specs=in_specs,
        out_specs=pl.BlockSpec((tm, D), lambda i: (i, 0)),
        out_shape=jax.ShapeDtypeStruct((n_tok, D), F32),
        scratch_shapes=[pltpu.VMEM((2, TOP_K * tm * PK_S, LANE), U32), pltpu.SemaphoreType.DMA((2,))],
        compiler_params=_cp(("arbitrary",)),
        name="moe_combine",
    )(*args)


def _lookup(table, idx):
    e = jnp.arange(table.shape[0], dtype=I32).reshape((-1,) + (1,) * idx.ndim)
    return jnp.sum(jnp.where(idx[None] == e, table.reshape(e.shape), 0), axis=0)


def _tile_flat(slots, tm):
    k, t = slots.shape
    return slots.reshape(k, t // tm, tm).transpose(1, 0, 2).reshape(-1)


def moe_layer(x, mod, norm_gain, router_w, router_b, exp_w1, exp_w3, exp_w2, sw1, sw3, sw2, layer, n_tok,
              final_gain=None):
    bm = EXP_BM
    n_blocks = -(-n_tok * TOP_K // bm) + N_EXPERTS
    hx, hx_packed = norm_mod(x, norm_gain, mod, 3, n_tok, pack=True)
    eidx, wts, pos, counts = moe_router(hx, router_w, router_b, n_tok)
    shared = shared_expert(hx, sw1.astype(BF16), sw3.astype(BF16), sw2.astype(BF16), n_tok)
    cnt = counts.astype(I32)
    padded = (cnt + bm - 1) // bm * bm
    pad_end = jnp.cumsum(padded)
    pad_start = pad_end - padded
    slots = _lookup(pad_start, eidx) + pos
    blk_row = jnp.arange(n_blocks, dtype=I32) * bm
    block_e = jnp.minimum(jnp.sum((pad_end[:, None] <= blk_row[None, :]).astype(I32), axis=0), N_EXPERTS - 1)
    valid = jnp.clip(_lookup(cnt, block_e) - (blk_row - _lookup(pad_start, block_e)), 0, bm).astype(I32)
    n_used = (pad_end[-1:] // bm).astype(I32)
    prev_e = jnp.concatenate([jnp.full((1,), -1, I32), block_e[:-1]])
    first = ((blk_row < pad_end[-1]) & (block_e != prev_e)).astype(I32)
    stage_slot = ((jnp.cumsum(first) - 1) % 2).astype(I32)
    eids = jnp.arange(N_EXPERTS, dtype=I32)
    later = jnp.where((eids[None, :] > eids[:, None]) & (padded[None, :] > 0), eids[None, :], N_EXPERTS)
    next_nonempty = jnp.min(later, axis=1)
    next_nonempty = jnp.where(next_nonempty == N_EXPERTS, -1, next_nonempty)
    next_e = _lookup(next_nonempty, block_e).astype(I32)
    xs = moe_dispatch(_tile_flat(slots, DISPATCH_TM), hx_packed, n_tok, n_blocks * bm)
    y = moe_experts(block_e, valid, n_used, first, next_e, stage_slot, xs, exp_w1, exp_w3, exp_w2, layer, n_blocks)
    return moe_combine(_tile_flat(slots, COMBINE_TM), wts.T, shared, x, mod, y, n_tok, final_gain)


def _rope_tables():
    t = jnp.arange(S, dtype=I32)
    row = (t // GRID_W).astype(F32)
    col = (t % GRID_W).astype(F32)
    n_freq = HD // 4
    inv_freq = ROPE_THETA ** (-jnp.arange(n_freq, dtype=F32) / n_freq)
    ang = jnp.concatenate([row[:, None] * inv_freq, col[:, None] * inv_freq], axis=-1)
    cosf = jnp.repeat(jnp.cos(ang), 2, axis=-1)
    sinf = jnp.stack([-jnp.sin(ang), jnp.sin(ang)], axis=-1).reshape(S, HD)
    return cosf, sinf


def kernel(x, c, ctx, c_ctx, ada_w, ada_b, norm_mix, norm_ffn, attn_w_in, attn_w_out, attn_rpb, attn_q_gain,
           attn_k_gain, ml_w_in, ml_w_out, ml_gate_b, ml_head_gain, router_w, router_b, exp_w1, exp_w3, exp_w2,
           sh_w1, sh_w3, sh_w2, final_norm_gain):
    depth = ada_w.shape[0]
    x_lat = x.reshape(T_LAT, D)
    x_ctx = ctx.reshape(T_CTX, D)
    cvec = jnp.concatenate([c, c_ctx[None], jnp.zeros((8 - B - 1, D), F32)], axis=0)
    mod_all = ada_ln(cvec, ada_w, ada_b).reshape(depth, 8, 6, D)
    cosf, sinf = _rope_tables()

    mod = mod_all[0]
    p = norm_matmul(x_lat, norm_mix[0], mod, attn_w_in[0].astype(BF16), emit_h=False, x_ctx=x_ctx)
    o_all = neighborhood_attention(p, na_bias_table(attn_rpb[0]))
    o_all = gqa_attention(p, cosf, sinf, attn_q_gain[0], attn_k_gain[0], o_all)
    o_all = ctx_attention(p, attn_q_gain[0], attn_k_gain[0], o_all)
    xa = matmul_gated_residual(o_all, attn_w_out[0].astype(BF16), x_lat, mod, 2, x_ctx=x_ctx)
    xa = moe_layer(xa, mod, norm_ffn[0], router_w[0], router_b[0], exp_w1, exp_w3, exp_w2,
                   sh_w1[0], sh_w3[0], sh_w2[0], 0, T_ALL)

    mod = mod_all[1]
    w_in = ml_w_in[0]
    p, hx = norm_matmul(xa, norm_mix[1], mod, w_in[:, :ML_MAIN].astype(BF16), emit_h=True)
    col, row = mlstm_gates(hx, w_in[:, ML_MAIN:], ml_gate_b[0])
    hdir = mlstm_scan(p, col, row)
    a = mlstm_readout(hdir, p, ml_head_gain[0])
    xl = matmul_gated_residual(a, ml_w_out[0].astype(BF16), xa, mod, 2)
    xl = moe_layer(xl, mod, norm_ffn[1], router_w[1], router_b[1], exp_w1, exp_w3, exp_w2,
                   sh_w1[1], sh_w3[1], sh_w2[1], 1, T_LAT, final_gain=final_norm_gain)
    return xl.reshape(B, S, D)
```

```python
import functools

import jax
import jax.numpy as jnp
from jax import lax
from jax.experimental import pallas as pl
from jax.experimental.pallas import tpu as pltpu

F32 = jnp.float32
BF16 = jnp.bfloat16
I32 = jnp.int32
U32 = jnp.uint32

D = 2048
B = 4
S = 4096
L = 256
T_LAT = B * S
T_CTX = B * L
T_ALL = T_LAT + T_CTX
GRID_W = 64
ROWS = S // GRID_W
HD = 128
NA_HEADS = 8
NA_WIN_ROWS = 8
NA_WIN_COLS = 16
GQA_Q_HEADS = 8
GQA_KV_HEADS = 2
GQA_GROUP = 4
ROPE_THETA = 10000.0
ATTN_IN = 4608
ML_HEADS = 8
ML_V = 256
ML_QK = 128
ML_MAIN = 6144
N_EXPERTS = 64
TOP_K = 8
N_GROUPS = 8
TOPK_GROUPS = 4
EXPERT_DIM = 512
ROUTED_SCALE = 2.5
EPS = 1e-6
NEG_INF = -1e30
ATT_SCALE = HD ** -0.5
LOG2E = 1.4426950408889634
ML_KSCALE = ML_QK ** -0.5

LANE = 128
NA_QROWS = 4
NA_SLAB = NA_QROWS + NA_WIN_ROWS - 1
NA_QB = NA_QROWS * GRID_W
NA_KB = NA_SLAB * GRID_W
ML_CH = 256
EXP_BM = 512
PK_W = D // 2
PK_S = PK_W // LANE
VMEM_LIMIT = 48 * 1024 * 1024


def _cp(sem, vmem=VMEM_LIMIT):
    return pltpu.CompilerParams(dimension_semantics=sem, vmem_limit_bytes=vmem)


def _pack_bf16_pairs(xb):
    u = pltpu.bitcast(xb.astype(F32), U32)
    return (u[:, PK_W:] & jnp.uint32(0xFFFF0000)) | (u[:, :PK_W] >> 16)


def _unpack_bf16_pairs(u):
    return pltpu.bitcast(u << 16, F32), pltpu.bitcast(u & jnp.uint32(0xFFFF0000), F32)


def _store_row_tiles(ref, words):
    rows = words.shape[0]
    for s in range(PK_S):
        ref[pl.ds(s, rows, stride=PK_S), :] = words[:, s * LANE:(s + 1) * LANE]


def _load_row_tiles(ref, start, rows):
    return [ref[pl.ds(start + s, rows, stride=PK_S), :] for s in range(PK_S)]


def _mod_row(start_row):
    return jnp.where(start_row < T_LAT, start_row // S, B)


def _ada_kernel(c_ref, w_ref, b_ref, o_ref):
    c = c_ref[...]
    a = (c * jax.nn.sigmoid(c)).astype(BF16)
    w = w_ref[0].astype(BF16)
    o_ref[0] = jnp.dot(a, w, preferred_element_type=F32) + b_ref[0]


def ada_ln(cvec, ada_w, ada_b):
    depth = ada_w.shape[0]
    n = ada_w.shape[2]
    tn = 1024
    return pl.pallas_call(
        _ada_kernel,
        grid=(depth, n // tn),
        in_specs=[pl.BlockSpec((8, D), lambda l, j: (0, 0)),
                  pl.BlockSpec((1, D, tn), lambda l, j: (l, 0, j)),
                  pl.BlockSpec((1, 1, tn), lambda l, j: (l, 0, j))],
        out_specs=pl.BlockSpec((1, 8, tn), lambda l, j: (l, 0, j)),
        out_shape=jax.ShapeDtypeStruct((depth, 8, n), F32),
        compiler_params=_cp(("arbitrary", "arbitrary")),
        name="ada_ln",
    )(cvec, ada_w, ada_b.reshape(depth, 1, n))


def _norm_mod_kernel(x_ref, g_ref, mod_ref, *out_refs, base, pack):
    x = x_ref[...]
    y = x * lax.rsqrt(jnp.mean(x * x, axis=-1, keepdims=True) + EPS) * g_ref[...]
    m = mod_ref[0]
    h = y * (1.0 + m[base + 1:base + 2]) + m[base:base + 1]
    hb = h.astype(BF16)
    out_refs[0][...] = hb
    if pack:
        _store_row_tiles(out_refs[1], _pack_bf16_pairs(hb))


def norm_mod(x, gain, mod, base, n_rows, pack):
    tm = 256
    out_shape = [jax.ShapeDtypeStruct((n_rows, D), BF16)]
    out_specs = [pl.BlockSpec((tm, D), lambda i: (i, 0))]
    if pack:
        out_shape.append(jax.ShapeDtypeStruct((n_rows * PK_S, LANE), U32))
        out_specs.append(pl.BlockSpec((tm * PK_S, LANE), lambda i: (i, 0)))
    res = pl.pallas_call(
        functools.partial(_norm_mod_kernel, base=base, pack=pack),
        grid=(n_rows // tm,),
        in_specs=[pl.BlockSpec((tm, D), lambda i: (i, 0)),
                  pl.BlockSpec((1, D), lambda i: (0, 0)),
                  pl.BlockSpec((1, 6, D), lambda i: (_mod_row(i * tm), 0, 0))],
        out_specs=out_specs,
        out_shape=out_shape,
        compiler_params=_cp(("arbitrary",)),
        name="norm_mod",
    )(x, gain.reshape(1, D), mod)
    return res if pack else res[0]


def _norm_mm_kernel(*refs, emit_h, n_lat):
    if n_lat is None:
        x_ref, g_ref, mod_ref, w_ref, o_ref = refs[:5]
        c_ref = None
    else:
        x_ref, c_ref, g_ref, mod_ref, w_ref, o_ref = refs[:6]
    hb_ref = refs[-1]

    def prologue(src_ref):
        x = src_ref[...]
        y = x * lax.rsqrt(jnp.mean(x * x, axis=-1, keepdims=True) + EPS) * g_ref[...]
        m = mod_ref[0]
        hb_ref[...] = (y * (1.0 + m[1:2]) + m[0:1]).astype(BF16)
        if emit_h:
            refs[-2][...] = hb_ref[...]

    first_col = pl.program_id(1) == 0
    if c_ref is None:
        pl.when(first_col)(lambda: prologue(x_ref))
    else:
        is_lat = pl.program_id(0) < n_lat
        pl.when(first_col & is_lat)(lambda: prologue(x_ref))
        pl.when(first_col & jnp.logical_not(is_lat))(lambda: prologue(c_ref))

    o_ref[...] = jnp.dot(hb_ref[...], w_ref[...], preferred_element_type=F32).astype(o_ref.dtype)


def norm_matmul(x, gain, mod, w, emit_h, x_ctx=None, tm=1024, tn=512):
    split = x_ctx is not None
    m = x.shape[0] + (x_ctx.shape[0] if split else 0)
    n = w.shape[1]
    n_lat = x.shape[0] // tm if split else None
    out_shape = [jax.ShapeDtypeStruct((m, n), BF16)]
    out_specs = [pl.BlockSpec((tm, tn), lambda i, j: (i, j))]
    if emit_h:
        out_shape.append(jax.ShapeDtypeStruct((m, D), BF16))
        out_specs.append(pl.BlockSpec((tm, D), lambda i, j: (i, 0)))
    if split:
        x_specs = [pl.BlockSpec((tm, D), lambda i, j: (jnp.minimum(i, n_lat - 1), 0)),
                   pl.BlockSpec((tm, D), lambda i, j: (0, 0))]
        x_args = [x, x_ctx]
    else:
        x_specs = [pl.BlockSpec((tm, D), lambda i, j: (i, 0))]
        x_args = [x]
    res = pl.pallas_call(
        functools.partial(_norm_mm_kernel, emit_h=emit_h, n_lat=n_lat),
        grid=(m // tm, n // tn),
        in_specs=x_specs + [pl.BlockSpec((1, D), lambda i, j: (0, 0)),
                            pl.BlockSpec((1, 6, D), lambda i, j: (_mod_row(i * tm), 0, 0)),
                            pl.BlockSpec((D, tn), lambda i, j: (0, j))],
        out_specs=out_specs,
        out_shape=out_shape,
        scratch_shapes=[pltpu.VMEM((tm, D), BF16)],
        compiler_params=_cp(("arbitrary", "arbitrary"), vmem=56 * 1024 * 1024),
        name="norm_matmul",
    )(*x_args, gain.reshape(1, D), mod, w)
    return res if emit_h else res[0]


def _mm_res_kernel(*refs, slot, n_lat):
    if n_lat is None:
        a_ref, w_ref, x_ref, mod_ref, o_ref = refs
        res = x_ref[...]
    else:
        a_ref, w_ref, x_ref, c_ref, mod_ref, o_ref = refs
        res = jnp.where(pl.program_id(0) < n_lat, x_ref[...], c_ref[...])
    acc = jnp.dot(a_ref[...], w_ref[...], preferred_element_type=F32)
    o_ref[...] = res + mod_ref[0][slot:slot + 1] * acc


def matmul_gated_residual(a, w, x, mod, slot, x_ctx=None, tm=1024, tn=512):
    m, k = a.shape
    n = w.shape[1]
    split = x_ctx is not None
    n_lat = x.shape[0] // tm if split else None
    if split:
        x_specs = [pl.BlockSpec((tm, tn), lambda i, j: (jnp.minimum(i, n_lat - 1), j)),
                   pl.BlockSpec((tm, tn), lambda i, j: (0, j))]
        x_args = [x, x_ctx]
    else:
        x_specs = [pl.BlockSpec((tm, tn), lambda i, j: (i, j))]
        x_args = [x]
    return pl.pallas_call(
        functools.partial(_mm_res_kernel, slot=slot, n_lat=n_lat),
        grid=(m // tm, n // tn),
        in_specs=[pl.BlockSpec((tm, k), lambda i, j: (i, 0)),
                  pl.BlockSpec((k, tn), lambda i, j: (0, j))] + x_specs
                 + [pl.BlockSpec((1, 6, tn), lambda i, j: (_mod_row(i * tm), 0, j))],
        out_specs=pl.BlockSpec((tm, tn), lambda i, j: (i, j)),
        out_shape=jax.ShapeDtypeStruct((m, n), F32),
        compiler_params=_cp(("arbitrary", "arbitrary")),
        name="matmul_gated_residual",
    )(a, w, *x_args, mod)


def _dot_nt(a, b):
    return lax.dot_general(a, b, (((1,), (1,)), ((), ())), preferred_element_type=F32)


def _rms_head(x, gain):
    return x * lax.rsqrt(jnp.mean(x * x, axis=-1, keepdims=True) + EPS) * gain


def _rope(x, cosf, sinf):
    lane = lax.broadcasted_iota(I32, x.shape, 1)
    nxt = pltpu.roll(x, LANE - 1, 1)
    prv = pltpu.roll(x, 1, 1)
    return x * cosf + jnp.where((lane & 1) == 0, nxt, prv) * sinf


def _softmax_av(parts):
    m = functools.reduce(jnp.maximum, [jnp.max(s, axis=-1, keepdims=True) for s, _ in parts])
    l = None
    o = None
    for s, v in parts:
        p = jnp.exp(s - m)
        li = jnp.sum(p, axis=-1, keepdims=True)
        oi = jnp.dot(p.astype(BF16), v, preferred_element_type=F32)
        l = li if l is None else l + li
        o = oi if o is None else o + oi
    return o / l


def _na_kernel(q_ref, k_ref, v_ref, kc_ref, vc_ref, tab_ref, o_ref):
    kc = kc_ref[...]
    vc = vc_ref[...]
    n_blocks = ROWS // NA_QROWS

    def body(j, carry):
        ks = jnp.clip(j * NA_QROWS - NA_WIN_ROWS // 2, 0, ROWS - NA_SLAB)
        typ = jnp.where(j == 0, 0, jnp.where(j == n_blocks - 1, 2, 1))
        qs = pl.multiple_of(j * NA_QB, NA_QB)
        kst = pl.multiple_of(ks * GRID_W, GRID_W)
        q = q_ref[pl.ds(qs, NA_QB), :]
        k = k_ref[pl.ds(kst, NA_KB), :]
        v = v_ref[pl.ds(kst, NA_KB), :]
        s_win = _dot_nt(q, k) * ATT_SCALE + tab_ref[typ, 0]
        s_ctx = _dot_nt(q, kc) * ATT_SCALE
        o_ref[pl.ds(qs, NA_QB), :] = _softmax_av([(s_win, v), (s_ctx, vc)]).astype(BF16)
        return carry

    lax.fori_loop(0, n_blocks, body, 0)


def na_bias_table(rpb):
    def one(r0, ks):
        r = r0 + jnp.arange(NA_QROWS)
        kr = ks + jnp.arange(NA_SLAB)
        start = jnp.clip(r - NA_WIN_ROWS // 2, 0, ROWS - NA_WIN_ROWS)
        row_ok = (kr[None, :] >= start[:, None]) & (kr[None, :] < start[:, None] + NA_WIN_ROWS)
        row_idx = jnp.clip(kr[None, :] - r[:, None] + NA_WIN_ROWS - 1, 0, 2 * NA_WIN_ROWS - 2)
        cq = jnp.arange(GRID_W)
        col_start = jnp.clip(cq - NA_WIN_COLS // 2, 0, GRID_W - NA_WIN_COLS)
        col_ok = (cq[None, :] >= col_start[:, None]) & (cq[None, :] < col_start[:, None] + NA_WIN_COLS)
        col_idx = jnp.clip(cq[None, :] - cq[:, None] + NA_WIN_COLS - 1, 0, 2 * NA_WIN_COLS - 2)
        r_hot = jax.nn.one_hot(row_idx, 2 * NA_WIN_ROWS - 1, dtype=F32)
        c_hot = jax.nn.one_hot(col_idx, 2 * NA_WIN_COLS - 1, dtype=F32)
        bias = jnp.einsum('qka,hab,xyb->hqxky', r_hot, rpb.astype(F32), c_hot, precision=lax.Precision.HIGHEST)
        ok = row_ok[:, None, :, None] & col_ok[None, :, None, :]
        return jnp.where(ok[None], bias, NEG_INF).reshape(NA_HEADS, NA_QB, NA_KB)

    mid = 2 * NA_QROWS
    last = ROWS - NA_QROWS
    return jnp.stack([one(0, 0), one(mid, mid - NA_WIN_ROWS // 2), one(last, ROWS - NA_SLAB)])


def neighborhood_attention(p, table):
    cb = S // L
    return pl.pallas_call(
        _na_kernel,
        grid=(NA_HEADS, B),
        in_specs=[pl.BlockSpec((S, HD), lambda h, b: (b, h)),
                  pl.BlockSpec((S, HD), lambda h, b: (b, NA_HEADS + h)),
                  pl.BlockSpec((S, HD), lambda h, b: (b, 2 * NA_HEADS + h)),
                  pl.BlockSpec((L, HD), lambda h, b: (B * cb + b, NA_HEADS + h)),
                  pl.BlockSpec((L, HD), lambda h, b: (B * cb + b, 2 * NA_HEADS + h)),
                  pl.BlockSpec((3, 1, NA_QB, NA_KB), lambda h, b: (0, h, 0, 0))],
        out_specs=pl.BlockSpec((S, HD), lambda h, b: (b, h)),
        out_shape=jax.ShapeDtypeStruct((T_ALL, D), BF16),
        compiler_params=_cp(("arbitrary", "arbitrary")),
        name="neighborhood_attention",
    )(p, p, p, p, p, table)


GQA_TQ = 512
GQA_CK = 1024
GQA_QCOL = 3 * NA_HEADS
GQA_KCOL = GQA_QCOL + GQA_Q_HEADS
GQA_VCOL = GQA_KCOL + GQA_KV_HEADS


def _gqa_kernel(q_ref, k_ref, v_ref, kc_ref, vc_ref, cq_ref, sq_ref, ck_ref, sk_ref, qg_ref, kg_ref, o_prev,
                o_ref, kn_ref, kcn_ref):
    del o_prev
    @pl.when(pl.program_id(2) == 0)
    def _():
        kn = _rope(_rms_head(k_ref[...].astype(F32), kg_ref[...]), ck_ref[...], sk_ref[...])
        kn_ref[...] = kn.astype(BF16)
        kcn_ref[...] = _rms_head(kc_ref[...].astype(F32), kg_ref[...]).astype(BF16)

    cos = cq_ref[...]
    sin = sq_ref[...]
    heads = []
    for g in range(GQA_GROUP):
        qh = _rope(_rms_head(q_ref[:, g * HD:(g + 1) * HD].astype(F32), qg_ref[...]), cos, sin)
        heads.append((qh * (ATT_SCALE * LOG2E)).astype(BF16))
    q = jnp.concatenate(heads, axis=0)
    chunks = [(kn_ref[c * GQA_CK:(c + 1) * GQA_CK, :], v_ref[c * GQA_CK:(c + 1) * GQA_CK, :])
              for c in range(S // GQA_CK)]
    chunks.append((kcn_ref[...], vc_ref[...]))
    m = l = acc = None
    for kk, vv in chunks:
        s = _dot_nt(q, kk)
        mc = jnp.max(s, axis=-1, keepdims=True)
        if m is None:
            m_new = mc
            p = jnp.exp2(s - m_new)
            l = jnp.sum(p, axis=-1, keepdims=True)
            acc = jnp.dot(p.astype(BF16), vv, preferred_element_type=F32)
        else:
            m_new = jnp.maximum(m, mc)
            alpha = jnp.exp2(m - m_new)
            p = jnp.exp2(s - m_new)
            l = alpha * l + jnp.sum(p, axis=-1, keepdims=True)
            acc = alpha * acc + jnp.dot(p.astype(BF16), vv, preferred_element_type=F32)
        m = m_new
    o = acc / l
    for g in range(GQA_GROUP):
        o_ref[:, g * HD:(g + 1) * HD] = o[g * GQA_TQ:(g + 1) * GQA_TQ].astype(BF16)


def gqa_attention(p, cosf, sinf, q_gain, k_gain, o_buf):
    nq = S // GQA_TQ
    cb = S // L
    gw = GQA_GROUP * HD
    return pl.pallas_call(
        _gqa_kernel,
        grid=(B, GQA_KV_HEADS, nq),
        in_specs=[pl.BlockSpec((GQA_TQ, gw), lambda b, n, i: (b * nq + i, GQA_QCOL // GQA_GROUP + n)),
                  pl.BlockSpec((S, HD), lambda b, n, i: (b, GQA_KCOL + n)),
                  pl.BlockSpec((S, HD), lambda b, n, i: (b, GQA_VCOL + n)),
                  pl.BlockSpec((L, HD), lambda b, n, i: (B * cb + b, GQA_KCOL + n)),
                  pl.BlockSpec((L, HD), lambda b, n, i: (B * cb + b, GQA_VCOL + n)),
                  pl.BlockSpec((GQA_TQ, HD), lambda b, n, i: (i, 0)),
                  pl.BlockSpec((GQA_TQ, HD), lambda b, n, i: (i, 0)),
                  pl.BlockSpec((S, HD), lambda b, n, i: (0, 0)),
                  pl.BlockSpec((S, HD), lambda b, n, i: (0, 0)),
                  pl.BlockSpec((1, HD), lambda b, n, i: (0, 0)),
                  pl.BlockSpec((1, HD), lambda b, n, i: (0, 0)),
                  pl.BlockSpec(memory_space=pl.ANY)],
        out_specs=pl.BlockSpec((GQA_TQ, gw), lambda b, n, i: (b * nq + i, (NA_HEADS * HD) // gw + n)),
        out_shape=jax.ShapeDtypeStruct((T_ALL, D), BF16),
        input_output_aliases={11: 0},
        scratch_shapes=[pltpu.VMEM((S, HD), BF16), pltpu.VMEM((L, HD), BF16)],
        compiler_params=_cp(("arbitrary",) * 3),
        name="gqa_attention",
    )(p, p, p, p, p, cosf, sinf, cosf, sinf, q_gain.reshape(1, HD), k_gain.reshape(1, HD), o_buf)


def _ctx_attn_kernel(p_ref, qg_ref, kg_ref, o_prev, o_ref):
    del o_prev

    def col(c):
        return p_ref[:, c * HD:(c + 1) * HD]

    for h in range(NA_HEADS):
        s = _dot_nt(col(h), col(NA_HEADS + h)) * ATT_SCALE
        o_ref[:, h * HD:(h + 1) * HD] = _softmax_av([(s, col(2 * NA_HEADS + h))]).astype(BF16)
    for n in range(GQA_KV_HEADS):
        kn = _rms_head(col(GQA_KCOL + n).astype(F32), kg_ref[...]).astype(BF16)
        v = col(GQA_VCOL + n)
        for g in range(GQA_GROUP):
            h = n * GQA_GROUP + g
            qn = _rms_head(col(GQA_QCOL + h).astype(F32), qg_ref[...]).astype(BF16)
            s = _dot_nt(qn, kn) * ATT_SCALE
            o_ref[:, (NA_HEADS + h) * HD:(NA_HEADS + h + 1) * HD] = _softmax_av([(s, v)]).astype(BF16)


def ctx_attention(p, q_gain, k_gain, o_buf):
    cb = S // L
    return pl.pallas_call(
        _ctx_attn_kernel,
        grid=(B,),
        in_specs=[pl.BlockSpec((L, ATTN_IN), lambda b: (B * cb + b, 0)),
                  pl.BlockSpec((1, HD), lambda b: (0, 0)),
                  pl.BlockSpec((1, HD), lambda b: (0, 0)),
                  pl.BlockSpec(memory_space=pl.ANY)],
        out_specs=pl.BlockSpec((L, D), lambda b: (B * cb + b, 0)),
        out_shape=jax.ShapeDtypeStruct((T_ALL, D), BF16),
        input_output_aliases={3: 0},
        compiler_params=_cp(("arbitrary",)),
        name="ctx_attention",
    )(p, q_gain.reshape(1, HD), k_gain.reshape(1, HD), o_buf)


def _log_sigmoid(x):
    return -(jnp.maximum(-x, 0.0) + jnp.log1p(jnp.exp(-jnp.abs(x))))


def _dot_hi(a, b):
    return jnp.dot(a, b, precision=lax.Precision.HIGHEST, preferred_element_type=F32)


def _gate_kernel(h_ref, wg_ref, wgt_ref, b_ref, bt_ref, lt_ref, ut_ref, col_ref, row_ref):
    nh = ML_HEADS
    hx = h_ref[...]
    g = jnp.dot(hx, wg_ref[...], preferred_element_type=F32) + b_ref[...]
    gt = _dot_nt(wgt_ref[...], hx) + bt_ref[...]
    li = g[:, 0:2 * nh]
    lf = _log_sigmoid(g[:, 2 * nh:4 * nh])
    lit = gt[0:2 * nh]
    lft = _log_sigmoid(gt[2 * nh:4 * nh])
    lt = lt_ref[...]
    ut = ut_ref[...]
    lane = lax.broadcasted_iota(I32, lf.shape, 1)
    bc = jnp.where(lane < nh, _dot_hi(lt, lf), _dot_hi(ut, lf))
    tot = jnp.sum(lf, axis=0, keepdims=True)
    aend = tot - bc + li
    col_ref[...] = jnp.concatenate([bc, aend, jnp.zeros((ML_CH, LANE - 4 * nh), F32)], axis=1)
    sub = lax.broadcasted_iota(I32, lft.shape, 0)
    bct = jnp.where(sub < nh, _dot_hi(lft, ut), _dot_hi(lft, lt))
    tott = jnp.sum(lft, axis=1, keepdims=True)
    gtr = lit - bct
    row_ref[0] = jnp.concatenate([bct, gtr, tott + gtr, jnp.broadcast_to(tott, bct.shape)], axis=0)


def mlstm_gates(hx, wg, gate_b):
    nh = ML_HEADS
    n_ch = T_ALL // ML_CH
    wg_pad = jnp.zeros((D, LANE), BF16).at[:, :4 * nh].set(wg.astype(BF16))
    b_pad = jnp.zeros((1, LANE), F32).at[0, :4 * nh].set(gate_b.reshape(-1))
    wgt = wg.astype(BF16).T
    bt = gate_b.reshape(4 * nh, 1).astype(F32)
    lt = jnp.tril(jnp.ones((ML_CH, ML_CH), F32))
    ut = jnp.triu(jnp.ones((ML_CH, ML_CH), F32))
    col, row = pl.pallas_call(
        _gate_kernel,
        grid=(n_ch,),
        in_specs=[pl.BlockSpec((ML_CH, D), lambda i: (i, 0)),
                  pl.BlockSpec((D, LANE), lambda i: (0, 0)),
                  pl.BlockSpec((4 * nh, D), lambda i: (0, 0)),
                  pl.BlockSpec((1, LANE), lambda i: (0, 0)),
                  pl.BlockSpec((4 * nh, 1), lambda i: (0, 0)),
                  pl.BlockSpec((ML_CH, ML_CH), lambda i: (0, 0)),
                  pl.BlockSpec((ML_CH, ML_CH), lambda i: (0, 0))],
        out_specs=[pl.BlockSpec((ML_CH, LANE), lambda i: (i, 0)),
                   pl.BlockSpec((1, 8 * nh, ML_CH), lambda i: (i, 0, 0))],
        out_shape=[jax.ShapeDtypeStruct((T_ALL, LANE), F32),
                   jax.ShapeDtypeStruct((n_ch, 8 * nh, ML_CH), F32)],
        compiler_params=_cp(("arbitrary",)),
        name="mlstm_gates",
    )(hx, wg_pad, wgt, b_pad, bt, lt, ut)
    return col, row


def _mlstm_step(d, hh, q_ref, k_ref, v_ref, col_ref, row_ref, o_ref, c_ref, n_ref, m_ref):
    nd = 2 * ML_HEADS
    sl = d * ML_HEADS + hh
    q = q_ref[:, hh * ML_QK:(hh + 1) * ML_QK]
    kf = k_ref[:, hh * ML_QK:(hh + 1) * ML_QK].astype(F32) * ML_KSCALE
    kb = kf.astype(BF16)
    v = v_ref[:, hh * ML_V:(hh + 1) * ML_V]
    bc_col = col_ref[:, sl:sl + 1]
    aend_col = col_ref[:, nd + sl:nd + sl + 1]
    g_row = row_ref[0, nd + sl:nd + sl + 1, :]
    aend_row = row_ref[0, 2 * nd + sl:2 * nd + sl + 1, :]
    btot = row_ref[0, 3 * nd + sl:3 * nd + sl + 1, 0:1]
    m_st = m_ref[sl]
    c_st = c_ref[sl]
    n_st = n_ref[sl]
    m_new = jnp.maximum(btot + m_st, jnp.max(aend_row, axis=1, keepdims=True))

    r = lax.broadcasted_iota(I32, (ML_CH, ML_CH), 0)
    c = lax.broadcasted_iota(I32, (ML_CH, ML_CH), 1)
    causal = (r >= c) if d == 0 else (r <= c)
    d_mat = jnp.where(causal, bc_col + g_row, -jnp.inf)
    m_row = jnp.maximum(bc_col + m_st, jnp.max(d_mat, axis=1, keepdims=True))
    w_inter = jnp.exp(bc_col + m_st - m_row)
    s_mat = _dot_nt(q, kb) * jnp.exp(d_mat - m_row)
    num = (w_inter * jnp.dot(q, c_st.astype(BF16), preferred_element_type=F32)
           + jnp.dot(s_mat.astype(BF16), v, preferred_element_type=F32))
    qn = _dot_nt(q, jnp.broadcast_to(n_st, (8, ML_QK)).astype(BF16))[:, 0:1]
    den = w_inter * qn + jnp.sum(s_mat, axis=1, keepdims=True)
    h_out = num / jnp.maximum(jnp.abs(den), jnp.exp(-m_row))
    o_ref[:, hh * ML_V:(hh + 1) * ML_V] = h_out.astype(o_ref.dtype)

    w_end_col = jnp.exp(aend_col - m_new)
    w_end_row = jnp.exp(aend_row - m_new)
    decay = jnp.exp(btot + m_st - m_new)
    kw = (kf * w_end_col).astype(BF16)
    c_ref[sl] = decay * c_st + lax.dot_general(kw, v, (((0,), (0,)), ((), ())), preferred_element_type=F32)
    w8 = jnp.broadcast_to(w_end_row, (8, ML_CH)).astype(BF16)
    n_ref[sl] = decay * n_st + jnp.dot(w8, kb, preferred_element_type=F32)[0:1]
    m_ref[sl] = m_new


def _mlstm_kernel(qf, kf, vf, colf, rowf, qb, kb, vb, colb, rowb, of, ob, c_ref, n_ref, m_ref):
    @pl.when(pl.program_id(1) == 0)
    def _():
        c_ref[...] = jnp.zeros_like(c_ref)
        n_ref[...] = jnp.zeros_like(n_ref)
        m_ref[...] = jnp.zeros_like(m_ref)

    for hh in range(ML_HEADS):
        _mlstm_step(0, hh, qf, kf, vf, colf, rowf, of, c_ref, n_ref, m_ref)
        _mlstm_step(1, hh, qb, kb, vb, colb, rowb, ob, c_ref, n_ref, m_ref)


def mlstm_scan(p, col, row):
    n_lat = S // ML_CH
    steps = n_lat + 1
    lat_blocks = T_LAT // ML_CH
    qk_w = ML_HEADS * ML_QK
    v_w = ML_HEADS * ML_V
    n_chains = 2 * ML_HEADS

    def chunk(b, d, st):
        c = (st - 1) if d == 0 else (n_lat - st)
        return jnp.where(st == 0, lat_blocks + b, b * n_lat + c)

    def out_chunk(b, d, st):
        s1 = jnp.maximum(st, 1)
        return b * n_lat + ((s1 - 1) if d == 0 else (n_lat - s1))

    def dir_specs(d):
        return [pl.BlockSpec((ML_CH, qk_w), lambda b, s: (chunk(b, d, s), 0)),
                pl.BlockSpec((ML_CH, qk_w), lambda b, s: (chunk(b, d, s), 1)),
                pl.BlockSpec((ML_CH, v_w), lambda b, s: (chunk(b, d, s), (2 * qk_w) // v_w)),
                pl.BlockSpec((ML_CH, LANE), lambda b, s: (chunk(b, d, s), 0)),
                pl.BlockSpec((1, 4 * n_chains, ML_CH), lambda b, s: (chunk(b, d, s), 0, 0))]

    return pl.pallas_call(
        _mlstm_kernel,
        grid=(B, steps),
        in_specs=dir_specs(0) + dir_specs(1),
        out_specs=[pl.BlockSpec((ML_CH, v_w), lambda b, s: (out_chunk(b, 0, s), 0)),
                   pl.BlockSpec((ML_CH, v_w), lambda b, s: (out_chunk(b, 1, s), 0))],
        out_shape=[jax.ShapeDtypeStruct((T_LAT, v_w), BF16), jax.ShapeDtypeStruct((T_LAT, v_w), BF16)],
        scratch_shapes=[pltpu.VMEM((n_chains, ML_QK, ML_V), F32), pltpu.VMEM((n_chains, 1, ML_QK), F32),
                        pltpu.VMEM((n_chains, 1, 1), F32)],
        compiler_params=_cp(("arbitrary",) * 2),
        name="mlstm_scan",
    )(p, p, p, col, row, p, p, p, col, row)


def _readout_kernel(hf_ref, hb_ref, o_ref, g_ref, a_ref):
    hs = hf_ref[...].astype(F32) + hb_ref[...].astype(F32)
    for h in range(ML_HEADS):
        sl = slice(h * ML_V, (h + 1) * ML_V)
        x = hs[:, sl]
        hn = x * lax.rsqrt(jnp.mean(x * x, axis=-1, keepdims=True) + EPS) * g_ref[:, sl]
        a_ref[:, sl] = (hn * jax.nn.sigmoid(o_ref[:, sl].astype(F32))).astype(BF16)


def mlstm_readout(hdir, p, head_gain):
    tm = 256
    ocol = (2 * ML_HEADS * ML_QK + ML_HEADS * ML_V) // D
    return pl.pallas_call(
        _readout_kernel,
        grid=(T_LAT // tm,),
        in_specs=[pl.BlockSpec((tm, D), lambda i: (i, 0)),
                  pl.BlockSpec((tm, D), lambda i: (i, 0)),
                  pl.BlockSpec((tm, D), lambda i: (i, ocol)),
                  pl.BlockSpec((1, D), lambda i: (0, 0))],
        out_specs=pl.BlockSpec((tm, D), lambda i: (i, 0)),
        out_shape=jax.ShapeDtypeStruct((T_LAT, D), BF16),
        compiler_params=_cp(("arbitrary",)),
        name="mlstm_readout",
    )(hdir[0], hdir[1], p, head_gain.reshape(1, D))


ROUTER_TM = 512


def _router_kernel(h_ref, w_ref, rb_ref, erow_ref, tri_ref, eidx_ref, wts_ref, pos_ref, cnt_ref, carry_ref):
    ng = N_GROUPS
    epg = N_EXPERTS // N_GROUPS
    tm = ROUTER_TM
    ninf = -jnp.inf

    @pl.when(pl.program_id(0) == 0)
    def _():
        carry_ref[...] = jnp.zeros_like(carry_ref)

    s = jax.nn.sigmoid(_dot_nt(w_ref[...], h_ref[...]))
    ssel = s + rb_ref[...]
    sraw = [s[ng * j:ng * (j + 1)] for j in range(epg)]
    slab = [ssel[ng * j:ng * (j + 1)] for j in range(epg)]
    m1 = functools.reduce(jnp.maximum, slab)
    jfirst = functools.reduce(jnp.minimum, [jnp.where(slab[j] == m1, j, epg) for j in range(epg)])
    m2 = functools.reduce(jnp.maximum, [jnp.where(jfirst == j, ninf, slab[j]) for j in range(epg)])
    gs = m1 + m2
    giota = lax.broadcasted_iota(I32, (ng, tm), 0)
    gsel = jnp.zeros((ng, tm), F32)
    for _ in range(TOPK_GROUPS):
        mx = jnp.max(gs, axis=0, keepdims=True)
        gi = jnp.min(jnp.where(gs == mx, giota, ng), axis=0, keepdims=True)
        hit = giota == gi
        gsel = jnp.where(hit, 1.0, gsel)
        gs = jnp.where(hit, ninf, gs)
    msl = [jnp.where(gsel > 0.0, slab[j], ninf) for j in range(epg)]
    eid = [giota * epg + j for j in range(epg)]
    selm = [jnp.zeros((ng, tm), F32) for _ in range(epg)]
    e_list, w_list = [], []
    for _ in range(TOP_K):
        mx = jnp.max(functools.reduce(jnp.maximum, msl), axis=0, keepdims=True)
        cand = functools.reduce(jnp.minimum, [jnp.where(msl[j] == mx, eid[j], N_EXPERTS) for j in range(epg)])
        esel = jnp.min(cand, axis=0, keepdims=True)
        hits = [eid[j] == esel for j in range(epg)]
        wk = functools.reduce(lambda a, b: a + b, [jnp.where(hits[j], sraw[j], 0.0) for j in range(epg)])
        w_list.append(jnp.sum(wk, axis=0, keepdims=True))
        e_list.append(esel)
        msl = [jnp.where(hits[j], ninf, msl[j]) for j in range(epg)]
        selm = [jnp.where(hits[j], 1.0, selm[j]) for j in range(epg)]
    wsum = functools.reduce(lambda a, b: a + b, w_list)
    wts_ref[...] = jnp.concatenate([w / wsum * ROUTED_SCALE for w in w_list], axis=0)
    eidx_ref[...] = jnp.concatenate(e_list, axis=0)
    sel = jnp.concatenate(selm, axis=0)
    carry = carry_ref[...]
    posfull = jnp.dot(sel.astype(BF16), tri_ref[...], preferred_element_type=F32) + carry
    erow = erow_ref[...]
    pos = [jnp.sum(jnp.where(erow == e, posfull, 0.0), axis=0, keepdims=True) for e in e_list]
    pos_ref[...] = jnp.concatenate(pos, axis=0).astype(I32)
    carry = carry + jnp.sum(sel, axis=1, keepdims=True)
    carry_ref[...] = carry
    cnt_ref[...] = carry


def moe_router(hx, router_w, router_b, n_tok):
    tm = ROUTER_TM
    epg = N_EXPERTS // N_GROUPS
    perm = (jnp.arange(N_EXPERTS) % N_GROUPS) * epg + jnp.arange(N_EXPERTS) // N_GROUPS
    w_t = router_w.astype(BF16).T[perm]
    rb = router_b.astype(F32)[perm].reshape(N_EXPERTS, 1)
    erow = perm.astype(I32).reshape(N_EXPERTS, 1)
    tri = jnp.triu(jnp.ones((tm, tm), BF16), 1)
    eidx, wts, pos, counts = pl.pallas_call(
        _router_kernel,
        grid=(n_tok // tm,),
        in_specs=[pl.BlockSpec((tm, D), lambda i: (i, 0)),
                  pl.BlockSpec((N_EXPERTS, D), lambda i: (0, 0)),
                  pl.BlockSpec((N_EXPERTS, 1), lambda i: (0, 0)),
                  pl.BlockSpec((N_EXPERTS, 1), lambda i: (0, 0)),
                  pl.BlockSpec((tm, tm), lambda i: (0, 0))],
        out_specs=[pl.BlockSpec((TOP_K, tm), lambda i: (0, i)),
                   pl.BlockSpec((TOP_K, tm), lambda i: (0, i)),
                   pl.BlockSpec((TOP_K, tm), lambda i: (0, i)),
                   pl.BlockSpec((N_EXPERTS, 1), lambda i: (0, 0))],
        out_shape=[jax.ShapeDtypeStruct((TOP_K, n_tok), I32),
                   jax.ShapeDtypeStruct((TOP_K, n_tok), F32),
                   jax.ShapeDtypeStruct((TOP_K, n_tok), I32),
                   jax.ShapeDtypeStruct((N_EXPERTS, 1), F32)],
        scratch_shapes=[pltpu.VMEM((N_EXPERTS, 1), F32)],
        compiler_params=_cp(("arbitrary",)),
        name="moe_router",
    )(hx, w_t, rb, erow, tri)
    return eidx, wts, pos, counts.reshape(N_EXPERTS)[perm]


DISPATCH_TM = 512


ROW_UNROLL = 8


def _dispatch_kernel(slot_ref, hx_ref, xs_hbm, sem):
    def issue(tt, carry):
        t8 = pl.multiple_of(tt * ROW_UNROLL, ROW_UNROLL)
        for j in range(ROW_UNROLL):
            src = hx_ref.at[pl.ds(pl.multiple_of((t8 + j) * PK_S, PK_S), PK_S), :]
            for k in range(TOP_K):
                row = pl.multiple_of(slot_ref[k * DISPATCH_TM + j + t8] * PK_S, PK_S)
                pltpu.make_async_copy(src, xs_hbm.at[pl.ds(row, PK_S), :], sem).start(priority=k % 2)
        return carry

    lax.fori_loop(0, DISPATCH_TM // ROW_UNROLL, issue, 0)
    for _ in range(TOP_K):
        pltpu.make_async_copy(hx_ref, xs_hbm.at[pl.ds(0, DISPATCH_TM * PK_S), :], sem).wait()


def moe_dispatch(slots, hx_packed, n_tok, n_rows):
    tm = DISPATCH_TM
    return pl.pallas_call(
        _dispatch_kernel,
        grid=(n_tok // tm,),
        in_specs=[pl.BlockSpec((TOP_K * tm,), lambda i: (i,), memory_space=pltpu.SMEM),
                  pl.BlockSpec((tm * PK_S, LANE), lambda i: (i, 0))],
        out_specs=pl.BlockSpec(memory_space=pl.ANY),
        out_shape=jax.ShapeDtypeStruct((n_rows * PK_S, LANE), U32),
        scratch_shapes=[pltpu.SemaphoreType.DMA(())],
        compiler_params=_cp(("arbitrary",)),
        name="moe_dispatch",
    )(slots, hx_packed)


def _expert_kernel(be_ref, valid_ref, nused_ref, first_ref, next_ref, slot_ref,
                   xs_ref, w1_hbm, w3_hbm, w2_hbm, y_ref, w1s, w3s, w2s, w1b, w3b, w2b, xb, sems, *, layer):
    i = pl.program_id(0)
    bm = EXP_BM

    def weight_copies(e, s):
        return [pltpu.make_async_copy(w_hbm.at[layer, e], stage.at[s], sems.at[s, j])
                for j, (w_hbm, stage) in enumerate(((w1_hbm, w1s), (w3_hbm, w3s), (w2_hbm, w2s)))]

    @pl.when(i < nused_ref[0])
    def _():
        @pl.when(first_ref[i] == 1)
        def _():
            s = slot_ref[i]

            @pl.when(i == 0)
            def _():
                for cp in weight_copies(be_ref[0], 0):
                    cp.start()

            for cp in weight_copies(be_ref[i], s):
                cp.wait()

            @pl.when(next_ref[i] >= 0)
            def _():
                for cp in weight_copies(next_ref[i], 1 - s):
                    cp.start(priority=1)

            w1b[...] = w1s[s].astype(BF16)
            w3b[...] = w3s[s].astype(BF16)
            w2b[...] = w2s[s].astype(BF16)

        live = lax.broadcasted_iota(I32, (bm, LANE), 0) < valid_ref[i]
        for s, piece in enumerate(_load_row_tiles(xs_ref, 0, bm)):
            lo, hi = _unpack_bf16_pairs(jnp.where(live, piece, jnp.uint32(0)))
            xb[:, s * LANE:(s + 1) * LANE] = lo.astype(BF16)
            xb[:, PK_W + s * LANE:PK_W + (s + 1) * LANE] = hi.astype(BF16)
        x = xb[...]
        h1 = jnp.dot(x, w1b[...], preferred_element_type=F32)
        h3 = jnp.dot(x, w3b[...], preferred_element_type=F32)
        a = (h1 * jax.nn.sigmoid(h1) * h3).astype(BF16)
        y = jnp.dot(a, w2b[...], preferred_element_type=F32)
        _store_row_tiles(y_ref, _pack_bf16_pairs(y.astype(BF16)))


def moe_experts(block_e, valid, n_used, first, next_e, slot, xs, w1, w3, w2, layer, n_blocks):
    bm = EXP_BM

    def blk(i, be, va, nu, fi, ne, sl):
        return (jnp.minimum(i, nu[0] - 1), 0)

    grid_spec = pltpu.PrefetchScalarGridSpec(
        num_scalar_prefetch=6,
        grid=(n_blocks,),
        in_specs=[pl.BlockSpec((bm * PK_S, LANE), blk),
                  pl.BlockSpec(memory_space=pl.ANY),
                  pl.BlockSpec(memory_space=pl.ANY),
                  pl.BlockSpec(memory_space=pl.ANY)],
        out_specs=pl.BlockSpec((bm * PK_S, LANE), blk),
        scratch_shapes=[pltpu.VMEM((2, D, EXPERT_DIM), F32), pltpu.VMEM((2, D, EXPERT_DIM), F32),
                        pltpu.VMEM((2, EXPERT_DIM, D), F32),
                        pltpu.VMEM((D, EXPERT_DIM), BF16), pltpu.VMEM((D, EXPERT_DIM), BF16),
                        pltpu.VMEM((EXPERT_DIM, D), BF16), pltpu.VMEM((bm, D), BF16),
                        pltpu.SemaphoreType.DMA((2, 3))],
    )
    return pl.pallas_call(
        functools.partial(_expert_kernel, layer=layer),
        grid_spec=grid_spec,
        out_shape=jax.ShapeDtypeStruct((n_blocks * bm * PK_S, LANE), U32),
        compiler_params=_cp(("arbitrary",), vmem=56 * 1024 * 1024),
        name="moe_experts",
    )(block_e, valid, n_used, first, next_e, slot, xs, w1, w3, w2)


def _shared_kernel(x_ref, w1_ref, w3_ref, w2_ref, o_ref):
    x = x_ref[...]
    h1 = jnp.dot(x, w1_ref[...], preferred_element_type=F32)
    h3 = jnp.dot(x, w3_ref[...], preferred_element_type=F32)
    a = (h1 * jax.nn.sigmoid(h1) * h3).astype(BF16)
    o_ref[...] = jnp.dot(a, w2_ref[...], preferred_element_type=F32)


def shared_expert(hx, w1, w3, w2, n_tok):
    tm = 512
    return pl.pallas_call(
        _shared_kernel,
        grid=(n_tok // tm,),
        in_specs=[pl.BlockSpec((tm, D), lambda i: (i, 0)),
                  pl.BlockSpec((D, EXPERT_DIM), lambda i: (0, 0)),
                  pl.BlockSpec((D, EXPERT_DIM), lambda i: (0, 0)),
                  pl.BlockSpec((EXPERT_DIM, D), lambda i: (0, 0))],
        out_specs=pl.BlockSpec((tm, D), lambda i: (i, 0)),
        out_shape=jax.ShapeDtypeStruct((n_tok, D), F32),
        compiler_params=_cp(("arbitrary",)),
        name="shared_expert",
    )(hx, w1, w3, w2)


COMBINE_TM = 128


def _combine_kernel(slot0_ref, slotn_ref, w_ref, sh_ref, x_ref, mod_ref, y_hbm, *rest, final, n_tiles):
    fg_ref = rest[0] if final else None
    o_ref, buf, sems = rest[-3:]
    tm = COMBINE_TM
    i = pl.program_id(0)
    half_rows = TOP_K * tm * PK_S

    def start_gathers(slot_ref, par):
        half = buf.at[par]
        sem = sems.at[par]

        def issue(tt, carry):
            t8 = pl.multiple_of(tt * ROW_UNROLL, ROW_UNROLL)
            for j in range(ROW_UNROLL):
                for k in range(TOP_K):
                    row = pl.multiple_of(slot_ref[k * tm + j + t8] * PK_S, PK_S)
                    dst = half.at[pl.ds(pl.multiple_of((k * tm + j + t8) * PK_S, PK_S), PK_S), :]
                    pltpu.make_async_copy(y_hbm.at[pl.ds(row, PK_S), :], dst, sem).start(priority=k % 2)
            return carry

        lax.fori_loop(0, tm // ROW_UNROLL, issue, 0)

    @pl.when(i == 0)
    def _():
        start_gathers(slot0_ref, 0)

    @pl.when(i + 1 < n_tiles)
    def _():
        start_gathers(slotn_ref, (i + 1) % 2)

    cur = buf.at[i % 2]
    pltpu.make_async_copy(y_hbm.at[pl.ds(0, half_rows), :], cur, sems.at[i % 2]).wait()

    w = w_ref[...]
    gate = mod_ref[0][5:6]
    acc = [None] * (2 * PK_S)
    for k in range(TOP_K):
        wk = w[:, k:k + 1]
        for s, piece in enumerate(_load_row_tiles(cur, k * tm * PK_S, tm)):
            for c, val in zip((s, PK_S + s), _unpack_bf16_pairs(piece)):
                acc[c] = wk * val if acc[c] is None else acc[c] + wk * val
    outs = []
    for c in range(2 * PK_S):
        sl = slice(c * LANE, (c + 1) * LANE)
        outs.append(x_ref[:, sl] + gate[:, sl] * (sh_ref[:, sl] + acc[c]))
    if fg_ref is not None:
        ssq = functools.reduce(lambda a, b: a + b, [jnp.sum(o * o, axis=-1, keepdims=True) for o in outs])
        inv = lax.rsqrt(ssq / D + EPS)
        outs = [o * inv * fg_ref[:, c * LANE:(c + 1) * LANE] for c, o in enumerate(outs)]
    for c, o in enumerate(outs):
        o_ref[:, c * LANE:(c + 1) * LANE] = o


def moe_combine(slots, wts_tok, shared, x, mod, y, n_tok, final_gain=None):
    tm = COMBINE_TM
    n_tiles = n_tok // tm
    final = final_gain is not None
    in_specs = [pl.BlockSpec((TOP_K * tm,), lambda i: (0,), memory_space=pltpu.SMEM),
                pl.BlockSpec((TOP_K * tm,), lambda i: (jnp.minimum(i + 1, n_tiles - 1),), memory_space=pltpu.SMEM),
                pl.BlockSpec((tm, TOP_K), lambda i: (i, 0)),
                pl.BlockSpec((tm, D), lambda i: (i, 0)),
                pl.BlockSpec((tm, D), lambda i: (i, 0)),
                pl.BlockSpec((1, 6, D), lambda i: (_mod_row(i * tm), 0, 0)),
                pl.BlockSpec(memory_space=pl.ANY)]
    args = [slots, slots, wts_tok, shared, x, mod, y]
    if final:
        in_specs.append(pl.BlockSpec((1, D), lambda i: (0, 0)))
        args.append(final_gain.reshape(1, D))
    return pl.pallas_call(
        functools.partial(_combine_kernel, final=final, n_tiles=n_tiles),
        grid=(n_tiles,),
        in_specs=in_specs,
        out_specs=pl.BlockSpec((tm, D), lambda i: (i, 0)),
        out_shape=jax.ShapeDtypeStruct((n_tok, D), F32),
        scratch_shapes=[pltpu.VMEM((2, TOP_K * tm * PK_S, LANE), U32), pltpu.SemaphoreType.DMA((2,))],
        compiler_params=_cp(("arbitrary",)),
        name="moe_combine",
    )(*args)


def _lookup(table, idx):
    e = jnp.arange(table.shape[0], dtype=I32).reshape((-1,) + (1,) * idx.ndim)
    return jnp.sum(jnp.where(idx[None] == e, table.reshape(e.shape), 0), axis=0)


def _tile_flat(slots, tm):
    k, t = slots.shape
    return slots.reshape(k, t // tm, tm).transpose(1, 0, 2).reshape(-1)


def moe_layer(x, mod, norm_gain, router_w, router_b, exp_w1, exp_w3, exp_w2, sw1, sw3, sw2, layer, n_tok,
              final_gain=None):
    bm = EXP_BM
    n_blocks = -(-n_tok * TOP_K // bm) + N_EXPERTS
    hx, hx_packed = norm_mod(x, norm_gain, mod, 3, n_tok, pack=True)
    eidx, wts, pos, counts = moe_router(hx, router_w, router_b, n_tok)
    shared = shared_expert(hx, sw1.astype(BF16), sw3.astype(BF16), sw2.astype(BF16), n_tok)
    cnt = counts.astype(I32)
    padded = (cnt + bm - 1) // bm * bm
    pad_end = jnp.cumsum(padded)
    pad_start = pad_end - padded
    slots = _lookup(pad_start, eidx) + pos
    blk_row = jnp.arange(n_blocks, dtype=I32) * bm
    block_e = jnp.minimum(jnp.sum((pad_end[:, None] <= blk_row[None, :]).astype(I32), axis=0), N_EXPERTS - 1)
    valid = jnp.clip(_lookup(cnt, block_e) - (blk_row - _lookup(pad_start, block_e)), 0, bm).astype(I32)
    n_used = (pad_end[-1:] // bm).astype(I32)
    prev_e = jnp.concatenate([jnp.full((1,), -1, I32), block_e[:-1]])
    first = ((blk_row < pad_end[-1]) & (block_e != prev_e)).astype(I32)
    stage_slot = ((jnp.cumsum(first) - 1) % 2).astype(I32)
    eids = jnp.arange(N_EXPERTS, dtype=I32)
    later = jnp.where((eids[None, :] > eids[:, None]) & (padded[None, :] > 0), eids[None, :], N_EXPERTS)
    next_nonempty = jnp.min(later, axis=1)
    next_nonempty = jnp.where(next_nonempty == N_EXPERTS, -1, next_nonempty)
    next_e = _lookup(next_nonempty, block_e).astype(I32)
    xs = moe_dispatch(_tile_flat(slots, DISPATCH_TM), hx_packed, n_tok, n_blocks * bm)
    y = moe_experts(block_e, valid, n_used, first, next_e, stage_slot, xs, exp_w1, exp_w3, exp_w2, layer, n_blocks)
    return moe_combine(_tile_flat(slots, COMBINE_TM), wts.T, shared, x, mod, y, n_tok, final_gain)


def _rope_tables():
    t = jnp.arange(S, dtype=I32)
    row = (t // GRID_W).astype(F32)
    col = (t % GRID_W).astype(F32)
    n_freq = HD // 4
    inv_freq = ROPE_THETA ** (-jnp.arange(n_freq, dtype=F32) / n_freq)
    ang = jnp.concatenate([row[:, None] * inv_freq, col[:, None] * inv_freq], axis=-1)
    cosf = jnp.repeat(jnp.cos(ang), 2, axis=-1)
    sinf = jnp.stack([-jnp.sin(ang), jnp.sin(ang)], axis=-1).reshape(S, HD)
    return cosf, sinf


def kernel(x, c, ctx, c_ctx, ada_w, ada_b, norm_mix, norm_ffn, attn_w_in, attn_w_out, attn_rpb, attn_q_gain,
           attn_k_gain, ml_w_in, ml_w_out, ml_gate_b, ml_head_gain, router_w, router_b, exp_w1, exp_w3, exp_w2,
           sh_w1, sh_w3, sh_w2, final_norm_gain):
    depth = ada_w.shape[0]
    x_lat = x.reshape(T_LAT, D)
    x_ctx = ctx.reshape(T_CTX, D)
    cvec = jnp.concatenate([c, c_ctx[None], jnp.zeros((8 - B - 1, D), F32)], axis=0)
    mod_all = ada_ln(cvec, ada_w, ada_b).reshape(depth, 8, 6, D)
    cosf, sinf = _rope_tables()

    mod = mod_all[0]
    p = norm_matmul(x_lat, norm_mix[0], mod, attn_w_in[0].astype(BF16), emit_h=False, x_ctx=x_ctx)
    o_all = neighborhood_attention(p, na_bias_table(attn_rpb[0]))
    o_all = gqa_attention(p, cosf, sinf, attn_q_gain[0], attn_k_gain[0], o_all)
    o_all = ctx_attention(p, attn_q_gain[0], attn_k_gain[0], o_all)
    xa = matmul_gated_residual(o_all, attn_w_out[0].astype(BF16), x_lat, mod, 2, x_ctx=x_ctx)
    xa = moe_layer(xa, mod, norm_ffn[0], router_w[0], router_b[0], exp_w1, exp_w3, exp_w2,
                   sh_w1[0], sh_w3[0], sh_w2[0], 0, T_ALL)

    mod = mod_all[1]
    w_in = ml_w_in[0]
    p, hx = norm_matmul(xa, norm_mix[1], mod, w_in[:, :ML_MAIN].astype(BF16), emit_h=True)
    col, row = mlstm_gates(hx, w_in[:, ML_MAIN:], ml_gate_b[0])
    hdir = mlstm_scan(p, col, row)
    a = mlstm_readout(hdir, p, ml_head_gain[0])
    xl = matmul_gated_residual(a, ml_w_out[0].astype(BF16), xa, mod, 2)
    xl = moe_layer(xl, mod, norm_ffn[1], router_w[1], router_b[1], exp_w1, exp_w3, exp_w2,
                   sh_w1[1], sh_w3[1], sh_w2[1], 1, T_LAT, final_gain=final_norm_gain)
    return xl.reshape(B, S, D)
```
